```python
import jax, jax.numpy as jnp
from jax import lax
import numpy as np

D_MODEL = 1024
BATCH = 16
SEQ = 2048
DEPTH = 2

GRID_W = 64
CTX_LEN = 256
Q_BLOCK = 128
ROPE_THETA = 10000.0
EPS = 1e-6
N_EVEN = (DEPTH + 1) // 2
N_ODD = DEPTH // 2
MIX_HALF = D_MODEL // 2

A_HEAD_DIM = 64
A_Q_HEADS = MIX_HALF // A_HEAD_DIM
A_KV_HEADS = 2
A_GROUP = A_Q_HEADS // A_KV_HEADS
B_GROUPS = 8
B_WIDTH = MIX_HALF
B_GROUP_DIM = B_WIDTH // B_GROUPS
B_CHUNK = 128
C_HEADS = 8
C_NOPE = 64
C_ROPE = 32
C_V = MIX_HALF // C_HEADS
C_Q_RANK = D_MODEL // 4
C_KV_RANK = D_MODEL // 8
D_WIDTH = MIX_HALF
D_CONV = 31
FF_DIM = 4 * D_MODEL
N_MOD = 6

EV_Q = A_Q_HEADS * A_HEAD_DIM
EV_KV = A_KV_HEADS * A_HEAD_DIM
EV_IN = EV_Q + 2 * EV_KV + 2 * B_WIDTH
OD_IN = C_Q_RANK + C_KV_RANK + C_ROPE + 2 * D_WIDTH

kernel_name = "hybrid_gqa_gmlp_mla_conformer_prefix_dit"


def rms_norm(x, g):
    xf = x.astype(jnp.float32)
    y = xf * lax.rsqrt(jnp.mean(xf * xf, axis=-1, keepdims=True) + EPS)
    return (y * g.astype(jnp.float32)).astype(x.dtype)


def layer_norm(x, g, b):
    xf = x.astype(jnp.float32)
    mu = jnp.mean(xf, axis=-1, keepdims=True)
    var = jnp.mean(jnp.square(xf - mu), axis=-1, keepdims=True)
    y = (xf - mu) * lax.rsqrt(var + EPS)
    return (y * g.astype(jnp.float32) + b.astype(jnp.float32)).astype(x.dtype)


def modulate(x, g, shift, scale):
    return rms_norm(x, g) * (1 + scale) + shift


def axial_angles(length, d_rot):
    rows = length // GRID_W
    row = jnp.broadcast_to(jnp.arange(rows)[:, None], (rows, GRID_W)).reshape(-1).astype(jnp.float32)
    col = jnp.broadcast_to(jnp.arange(GRID_W)[None, :], (rows, GRID_W)).reshape(-1).astype(jnp.float32)
    d_axis = d_rot // 2
    inv = ROPE_THETA ** (-jnp.arange(0, d_axis, 2, dtype=jnp.float32) / d_axis)
    return jnp.concatenate([row[:, None] * inv, col[:, None] * inv], axis=-1)


def apply_rope(x, ang):
    d = x.shape[-1]
    xf = x.astype(jnp.float32).reshape(x.shape[:-1] + (d // 2, 2))
    cos, sin = jnp.cos(ang), jnp.sin(ang)
    x0, x1 = xf[..., 0], xf[..., 1]
    out = jnp.stack([x0 * cos - x1 * sin, x0 * sin + x1 * cos], axis=-1)
    return out.reshape(x.shape).astype(x.dtype)


def to_heads(t, n_heads, head_dim):
    b, l, _ = t.shape
    return t.reshape(b, l, n_heads, head_dim).transpose(0, 2, 1, 3)


def from_heads(o):
    b, n, l, hd = o.shape
    return o.transpose(0, 2, 1, 3).reshape(b, l, n * hd)


def block_attention(q, k, v):
    b, hk, g, lq, dk = q.shape
    scale = dk ** -0.5
    qb = jnp.moveaxis(q.reshape(b, hk, g, lq // Q_BLOCK, Q_BLOCK, dk), 3, 0)

    def one_block(qi):
        s = jnp.einsum("bhgqd,bhkd->bhgqk", qi, k, preferred_element_type=jnp.float32) * scale
        p = jax.nn.softmax(s, axis=-1)
        return jnp.einsum("bhgqk,bhkd->bhgqd", p.astype(v.dtype), v)

    o = lax.map(one_block, qb)
    return jnp.moveaxis(o, 0, 3).reshape(b, hk, g, lq, v.shape[-1])


def spatial_gating(z, norm_g, w_s, b_s):
    b, l, _ = z.shape
    u, v = jnp.split(jax.nn.gelu(z), 2, axis=-1)
    v = rms_norm(v.reshape(b, l, B_GROUPS, B_GROUP_DIM), norm_g)
    v = v.reshape(b, l // B_CHUNK, B_CHUNK, B_GROUPS, B_GROUP_DIM)
    sv = jnp.einsum("gpq,bnqgc->bnpgc", w_s, v) + b_s.T[None, None, :, :, None]
    return u * sv.reshape(b, l, B_WIDTH)


def even_mixer(h_lat, h_ctx, need_ctx, w_in, q_norm_g, k_norm_g, sgu_norm_g, sgu_w, sgu_b):
    cuts = [EV_Q, EV_Q + EV_KV, EV_Q + 2 * EV_KV]
    w_q, w_k, w_v, w_z = jnp.split(w_in, cuts, axis=1)

    def gqa_q(qp, ang):
        b, l, _ = qp.shape
        q = rms_norm(to_heads(qp, A_Q_HEADS, A_HEAD_DIM), q_norm_g)
        if ang is not None:
            q = apply_rope(q, ang)
        return q.reshape(b, A_KV_HEADS, A_GROUP, l, A_HEAD_DIM)

    def gqa_kv(kp, vp, ang):
        k = rms_norm(to_heads(kp, A_KV_HEADS, A_HEAD_DIM), k_norm_g)
        if ang is not None:
            k = apply_rope(k, ang)
        return k, to_heads(vp, A_KV_HEADS, A_HEAD_DIM)

    def merge(o):
        b, hk, g, l, d = o.shape
        return from_heads(o.reshape(b, hk * g, l, d))

    b, l, _ = h_lat.shape
    ang = axial_angles(l, A_HEAD_DIM)
    qp, kp, vp, zp = jnp.split(h_lat @ w_in, cuts, axis=-1)
    kc, vc = gqa_kv(h_ctx @ w_k, h_ctx @ w_v, None)
    kl, vl = gqa_kv(kp, vp, ang)
    o_att = block_attention(gqa_q(qp, ang), jnp.concatenate([kc, kl], axis=2),
                            jnp.concatenate([vc, vl], axis=2))
    out_lat = jnp.concatenate([merge(o_att), spatial_gating(zp, sgu_norm_g, sgu_w, sgu_b)], axis=-1)
    out_ctx = None
    if need_ctx:
        oc = block_attention(gqa_q(h_ctx @ w_q, None), kc, vc)
        out_ctx = jnp.concatenate([merge(oc), spatial_gating(h_ctx @ w_z, sgu_norm_g, sgu_w, sgu_b)],
                                  axis=-1)
    return out_lat, out_ctx


def odd_mixer(h_lat, h_ctx, need_ctx, w_in, q_norm_g, kv_norm_g, w_uq, w_ukv, conv_w, conv_b,
              ln_g, ln_b):
    cuts = [C_Q_RANK, C_Q_RANK + C_KV_RANK, C_Q_RANK + C_KV_RANK + C_ROPE]
    w_cq, w_ckv, w_kr, w_cv = jnp.split(w_in, cuts, axis=1)

    def mla_q(cq, ang):
        b, l, _ = cq.shape
        q = to_heads(rms_norm(cq, q_norm_g) @ w_uq, C_HEADS, C_NOPE + C_ROPE)
        qn, qr = jnp.split(q, [C_NOPE], axis=-1)
        if ang is not None:
            qr = apply_rope(qr, ang)
        return jnp.concatenate([qn, qr], axis=-1)[:, :, None]

    def mla_kv(ckv, kr, ang):
        b, l, _ = ckv.shape
        kv = to_heads(rms_norm(ckv, kv_norm_g) @ w_ukv, C_HEADS, C_NOPE + C_V)
        kn, v = jnp.split(kv, [C_NOPE], axis=-1)
        kr = kr[:, None]
        if ang is not None:
            kr = apply_rope(kr, ang)
        k = jnp.concatenate([kn, jnp.broadcast_to(kr, (b, C_HEADS, l, C_ROPE))], axis=-1)
        return k, v

    def conformer(z):
        a, gt = jnp.split(z, 2, axis=-1)
        y = a * jax.nn.sigmoid(gt)
        y = lax.conv_general_dilated(y, conv_w[:, None, :], window_strides=(1,),
                                     padding=[(D_CONV // 2, D_CONV // 2)],
                                     dimension_numbers=("NWC", "WIO", "NWC"),
                                     feature_group_count=D_WIDTH) + conv_b
        return jax.nn.silu(layer_norm(y, ln_g, ln_b))

    b, l, _ = h_lat.shape
    ang = axial_angles(l, C_ROPE)
    cq, ckv, kr, zc = jnp.split(h_lat @ w_in, cuts, axis=-1)
    kc, vc = mla_kv(h_ctx @ w_ckv, h_ctx @ w_kr, None)
    kl, vl = mla_kv(ckv, kr, ang)
    o_att = block_attention(mla_q(cq, ang), jnp.concatenate([kc, kl], axis=2),
                            jnp.concatenate([vc, vl], axis=2))[:, :, 0]
    out_lat = jnp.concatenate([from_heads(o_att), conformer(zc)], axis=-1)
    out_ctx = None
    if need_ctx:
        oc = block_attention(mla_q(h_ctx @ w_cq, None), kc, vc)[:, :, 0]
        out_ctx = jnp.concatenate([from_heads(oc), conformer(h_ctx @ w_cv)], axis=-1)
    return out_lat, out_ctx


def sq_relu_mlp(h, w1, w2):
    return jnp.square(jax.nn.relu(h @ w1)) @ w2


def _fwd_setup_inputs(seed: int = 0) -> dict:
    key = jax.random.key(seed)
    ks = jax.random.split(key, 32)

    def nrm(k, shape, scale=1.0):
        return jax.random.normal(k, shape, jnp.float32) * scale

    def gain(k, shape):
        return 1.0 + 0.05 * jax.random.normal(k, shape, jnp.float32)

    D = D_MODEL
    return {
        "x": nrm(ks[0], (BATCH, SEQ, D)),
        "c": nrm(ks[1], (BATCH, D)),
        "ctx": nrm(ks[2], (BATCH, CTX_LEN, D)),
        "c_ctx": nrm(ks[3], (D,)),
        "ada_w": nrm(ks[4], (DEPTH, D, N_MOD * D), 0.5 * D ** -0.5),
        "ada_b": nrm(ks[5], (DEPTH, N_MOD * D), 0.02),
        "norm1_g": gain(ks[6], (DEPTH, D)),
        "norm2_g": gain(ks[7], (DEPTH, D)),
        "w_out": nrm(ks[8], (DEPTH, D, D), D ** -0.5),
        "mlp_w1": nrm(ks[9], (DEPTH, D, FF_DIM), D ** -0.5),
        "mlp_w2": nrm(ks[10], (DEPTH, FF_DIM, D), FF_DIM ** -0.5),
        "ev_w_in": nrm(ks[11], (N_EVEN, D, EV_IN), D ** -0.5),
        "ev_q_norm_g": gain(ks[12], (N_EVEN, A_HEAD_DIM)),
        "ev_k_norm_g": gain(ks[13], (N_EVEN, A_HEAD_DIM)),
        "ev_sgu_norm_g": gain(ks[14], (N_EVEN, B_GROUPS, B_GROUP_DIM)),
        "ev_sgu_w": nrm(ks[15], (N_EVEN, B_GROUPS, B_CHUNK, B_CHUNK), B_CHUNK ** -0.5),
        "ev_sgu_b": gain(ks[16], (N_EVEN, B_GROUPS, B_CHUNK)),
        "od_w_in": nrm(ks[17], (N_ODD, D, OD_IN), D ** -0.5),
        "od_q_norm_g": gain(ks[18], (N_ODD, C_Q_RANK)),
        "od_kv_norm_g": gain(ks[19], (N_ODD, C_KV_RANK)),
        "od_w_uq": nrm(ks[20], (N_ODD, C_Q_RANK, C_HEADS * (C_NOPE + C_ROPE)), C_Q_RANK ** -0.5),
        "od_w_ukv": nrm(ks[21], (N_ODD, C_KV_RANK, C_HEADS * (C_NOPE + C_V)), C_KV_RANK ** -0.5),
        "od_conv_w": nrm(ks[22], (N_ODD, D_CONV, D_WIDTH), D_CONV ** -0.5),
        "od_conv_b": nrm(ks[23], (N_ODD, D_WIDTH), 0.02),
        "od_ln_g": gain(ks[24], (N_ODD, D_WIDTH)),
        "od_ln_b": nrm(ks[25], (N_ODD, D_WIDTH), 0.02),
        "final_g": gain(ks[26], (D,)),
    }


def _fwd_reference(x, c, ctx, c_ctx, ada_w, ada_b, norm1_g, norm2_g, w_out, mlp_w1, mlp_w2,
              ev_w_in, ev_q_norm_g, ev_k_norm_g, ev_sgu_norm_g, ev_sgu_w, ev_sgu_b,
              od_w_in, od_q_norm_g, od_kv_norm_g, od_w_uq, od_w_ukv, od_conv_w, od_conv_b,
              od_ln_g, od_ln_b, final_g):
    x_lat, x_ctx = x, ctx
    silu_c = jax.nn.silu(c)
    silu_cc = jax.nn.silu(c_ctx)
    for i in range(DEPTH):
        last = i == DEPTH - 1
        j = i // 2
        m = jnp.split(silu_c @ ada_w[i] + ada_b[i], N_MOD, axis=-1)
        sh1, sc1, g1, sh2, sc2, g2 = [t[:, None, :] for t in m]
        sh1c, sc1c, g1c, sh2c, sc2c, g2c = jnp.split(silu_cc @ ada_w[i] + ada_b[i], N_MOD, axis=-1)

        h_lat = modulate(x_lat, norm1_g[i], sh1, sc1)
        h_ctx = modulate(x_ctx, norm1_g[i], sh1c, sc1c)
        if i % 2 == 0:
            o_lat, o_ctx = even_mixer(h_lat, h_ctx, not last, ev_w_in[j], ev_q_norm_g[j],
                                      ev_k_norm_g[j], ev_sgu_norm_g[j], ev_sgu_w[j], ev_sgu_b[j])
        else:
            o_lat, o_ctx = odd_mixer(h_lat, h_ctx, not last, od_w_in[j], od_q_norm_g[j],
                                     od_kv_norm_g[j], od_w_uq[j], od_w_ukv[j], od_conv_w[j],
                                     od_conv_b[j], od_ln_g[j], od_ln_b[j])

        x_lat = x_lat + g1 * (o_lat @ w_out[i])
        x_lat = x_lat + g2 * sq_relu_mlp(modulate(x_lat, norm2_g[i], sh2, sc2), mlp_w1[i], mlp_w2[i])
        if not last:
            x_ctx = x_ctx + g1c * (o_ctx @ w_out[i])
            x_ctx = x_ctx + g2c * sq_relu_mlp(modulate(x_ctx, norm2_g[i], sh2c, sc2c),
                                              mlp_w1[i], mlp_w2[i])
    return rms_norm(x_lat, final_g)


import jax as _jax
import jax.numpy as _jnp

TWIN_FORMAT = 'train_step'
FWD_PARAMS = ['x', 'c', 'ctx', 'c_ctx', 'ada_w', 'ada_b', 'norm1_g', 'norm2_g', 'w_out', 'mlp_w1', 'mlp_w2', 'ev_w_in', 'ev_q_norm_g', 'ev_k_norm_g', 'ev_sgu_norm_g', 'ev_sgu_w', 'ev_sgu_b', 'od_w_in', 'od_q_norm_g', 'od_kv_norm_g', 'od_w_uq', 'od_w_ukv', 'od_conv_w', 'od_conv_b', 'od_ln_g', 'od_ln_b', 'final_g']
TWIN_WEIGHTS = ['c_ctx', 'ada_w', 'ada_b', 'norm1_g', 'norm2_g', 'w_out', 'mlp_w1', 'mlp_w2', 'ev_w_in', 'ev_q_norm_g', 'ev_k_norm_g', 'ev_sgu_norm_g', 'ev_sgu_w', 'ev_sgu_b', 'od_w_in', 'od_q_norm_g', 'od_kv_norm_g', 'od_w_uq', 'od_w_ukv', 'od_conv_w', 'od_conv_b', 'od_ln_g', 'od_ln_b', 'final_g']
TWIN_DIFF_INPUT = 'x'
TWIN_INPUTS = ['x', 'c', 'ctx', 'c_ctx', 'ada_w', 'ada_b', 'norm1_g', 'norm2_g', 'w_out', 'mlp_w1', 'mlp_w2', 'ev_w_in', 'ev_q_norm_g', 'ev_k_norm_g', 'ev_sgu_norm_g', 'ev_sgu_w', 'ev_sgu_b', 'od_w_in', 'od_q_norm_g', 'od_kv_norm_g', 'od_w_uq', 'od_w_ukv', 'od_conv_w', 'od_conv_b', 'od_ln_g', 'od_ln_b', 'final_g', 'loss_target', 'm_c_ctx', 'm_ada_w', 'm_ada_b', 'm_norm1_g', 'm_norm2_g', 'm_w_out', 'm_mlp_w1', 'm_mlp_w2', 'm_ev_w_in', 'm_ev_q_norm_g', 'm_ev_k_norm_g', 'm_ev_sgu_norm_g', 'm_ev_sgu_w', 'm_ev_sgu_b', 'm_od_w_in', 'm_od_q_norm_g', 'm_od_kv_norm_g', 'm_od_w_uq', 'm_od_w_ukv', 'm_od_conv_w', 'm_od_conv_b', 'm_od_ln_g', 'm_od_ln_b', 'm_final_g', 'v_c_ctx', 'v_ada_w', 'v_ada_b', 'v_norm1_g', 'v_norm2_g', 'v_w_out', 'v_mlp_w1', 'v_mlp_w2', 'v_ev_w_in', 'v_ev_q_norm_g', 'v_ev_k_norm_g', 'v_ev_sgu_norm_g', 'v_ev_sgu_w', 'v_ev_sgu_b', 'v_od_w_in', 'v_od_q_norm_g', 'v_od_kv_norm_g', 'v_od_w_uq', 'v_od_w_ukv', 'v_od_conv_w', 'v_od_conv_b', 'v_od_ln_g', 'v_od_ln_b', 'v_final_g']
TWIN_OUTPUTS = ['loss', 'grad_x', 'grad_c_ctx', 'grad_ada_w', 'grad_ada_b', 'grad_norm1_g', 'grad_norm2_g', 'grad_w_out', 'grad_mlp_w1', 'grad_mlp_w2', 'grad_ev_w_in', 'grad_ev_q_norm_g', 'grad_ev_k_norm_g', 'grad_ev_sgu_norm_g', 'grad_ev_sgu_w', 'grad_ev_sgu_b', 'grad_od_w_in', 'grad_od_q_norm_g', 'grad_od_kv_norm_g', 'grad_od_w_uq', 'grad_od_w_ukv', 'grad_od_conv_w', 'grad_od_conv_b', 'grad_od_ln_g', 'grad_od_ln_b', 'grad_final_g', 'delta_c_ctx', 'delta_ada_w', 'delta_ada_b', 'delta_norm1_g', 'delta_norm2_g', 'delta_w_out', 'delta_mlp_w1', 'delta_mlp_w2', 'delta_ev_w_in', 'delta_ev_q_norm_g', 'delta_ev_k_norm_g', 'delta_ev_sgu_norm_g', 'delta_ev_sgu_w', 'delta_ev_sgu_b', 'delta_od_w_in', 'delta_od_q_norm_g', 'delta_od_kv_norm_g', 'delta_od_w_uq', 'delta_od_w_ukv', 'delta_od_conv_w', 'delta_od_conv_b', 'delta_od_ln_g', 'delta_od_ln_b', 'delta_final_g', 'new_m_c_ctx', 'new_m_ada_w', 'new_m_ada_b', 'new_m_norm1_g', 'new_m_norm2_g', 'new_m_w_out', 'new_m_mlp_w1', 'new_m_mlp_w2', 'new_m_ev_w_in', 'new_m_ev_q_norm_g', 'new_m_ev_k_norm_g', 'new_m_ev_sgu_norm_g', 'new_m_ev_sgu_w', 'new_m_ev_sgu_b', 'new_m_od_w_in', 'new_m_od_q_norm_g', 'new_m_od_kv_norm_g', 'new_m_od_w_uq', 'new_m_od_w_ukv', 'new_m_od_conv_w', 'new_m_od_conv_b', 'new_m_od_ln_g', 'new_m_od_ln_b', 'new_m_final_g', 'new_v_c_ctx', 'new_v_ada_w', 'new_v_ada_b', 'new_v_norm1_g', 'new_v_norm2_g', 'new_v_w_out', 'new_v_mlp_w1', 'new_v_mlp_w2', 'new_v_ev_w_in', 'new_v_ev_q_norm_g', 'new_v_ev_k_norm_g', 'new_v_ev_sgu_norm_g', 'new_v_ev_sgu_w', 'new_v_ev_sgu_b', 'new_v_od_w_in', 'new_v_od_q_norm_g', 'new_v_od_kv_norm_g', 'new_v_od_w_uq', 'new_v_od_w_ukv', 'new_v_od_conv_w', 'new_v_od_conv_b', 'new_v_od_ln_g', 'new_v_od_ln_b', 'new_v_final_g']
TWIN_LEAF_KINDS = {'loss': 'loss', 'grad_x': 'grad_x', 'grad_c_ctx': 'grad_w', 'grad_ada_w': 'grad_w', 'grad_ada_b': 'grad_w', 'grad_norm1_g': 'grad_w', 'grad_norm2_g': 'grad_w', 'grad_w_out': 'grad_w', 'grad_mlp_w1': 'grad_w', 'grad_mlp_w2': 'grad_w', 'grad_ev_w_in': 'grad_w', 'grad_ev_q_norm_g': 'grad_w', 'grad_ev_k_norm_g': 'grad_w', 'grad_ev_sgu_norm_g': 'grad_w', 'grad_ev_sgu_w': 'grad_w', 'grad_ev_sgu_b': 'grad_w', 'grad_od_w_in': 'grad_w', 'grad_od_q_norm_g': 'grad_w', 'grad_od_kv_norm_g': 'grad_w', 'grad_od_w_uq': 'grad_w', 'grad_od_w_ukv': 'grad_w', 'grad_od_conv_w': 'grad_w', 'grad_od_conv_b': 'grad_w', 'grad_od_ln_g': 'grad_w', 'grad_od_ln_b': 'grad_w', 'grad_final_g': 'grad_w', 'delta_c_ctx': 'delta_w', 'delta_ada_w': 'delta_w', 'delta_ada_b': 'delta_w', 'delta_norm1_g': 'delta_w', 'delta_norm2_g': 'delta_w', 'delta_w_out': 'delta_w', 'delta_mlp_w1': 'delta_w', 'delta_mlp_w2': 'delta_w', 'delta_ev_w_in': 'delta_w', 'delta_ev_q_norm_g': 'delta_w', 'delta_ev_k_norm_g': 'delta_w', 'delta_ev_sgu_norm_g': 'delta_w', 'delta_ev_sgu_w': 'delta_w', 'delta_ev_sgu_b': 'delta_w', 'delta_od_w_in': 'delta_w', 'delta_od_q_norm_g': 'delta_w', 'delta_od_kv_norm_g': 'delta_w', 'delta_od_w_uq': 'delta_w', 'delta_od_w_ukv': 'delta_w', 'delta_od_conv_w': 'delta_w', 'delta_od_conv_b': 'delta_w', 'delta_od_ln_g': 'delta_w', 'delta_od_ln_b': 'delta_w', 'delta_final_g': 'delta_w', 'new_m_c_ctx': 'new_m', 'new_m_ada_w': 'new_m', 'new_m_ada_b': 'new_m', 'new_m_norm1_g': 'new_m', 'new_m_norm2_g': 'new_m', 'new_m_w_out': 'new_m', 'new_m_mlp_w1': 'new_m', 'new_m_mlp_w2': 'new_m', 'new_m_ev_w_in': 'new_m', 'new_m_ev_q_norm_g': 'new_m', 'new_m_ev_k_norm_g': 'new_m', 'new_m_ev_sgu_norm_g': 'new_m', 'new_m_ev_sgu_w': 'new_m', 'new_m_ev_sgu_b': 'new_m', 'new_m_od_w_in': 'new_m', 'new_m_od_q_norm_g': 'new_m', 'new_m_od_kv_norm_g': 'new_m', 'new_m_od_w_uq': 'new_m', 'new_m_od_w_ukv': 'new_m', 'new_m_od_conv_w': 'new_m', 'new_m_od_conv_b': 'new_m', 'new_m_od_ln_g': 'new_m', 'new_m_od_ln_b': 'new_m', 'new_m_final_g': 'new_m', 'new_v_c_ctx': 'new_v', 'new_v_ada_w': 'new_v', 'new_v_ada_b': 'new_v', 'new_v_norm1_g': 'new_v', 'new_v_norm2_g': 'new_v', 'new_v_w_out': 'new_v', 'new_v_mlp_w1': 'new_v', 'new_v_mlp_w2': 'new_v', 'new_v_ev_w_in': 'new_v', 'new_v_ev_q_norm_g': 'new_v', 'new_v_ev_k_norm_g': 'new_v', 'new_v_ev_sgu_norm_g': 'new_v', 'new_v_ev_sgu_w': 'new_v', 'new_v_ev_sgu_b': 'new_v', 'new_v_od_w_in': 'new_v', 'new_v_od_q_norm_g': 'new_v', 'new_v_od_kv_norm_g': 'new_v', 'new_v_od_w_uq': 'new_v', 'new_v_od_w_ukv': 'new_v', 'new_v_od_conv_w': 'new_v', 'new_v_od_conv_b': 'new_v', 'new_v_od_ln_g': 'new_v', 'new_v_od_ln_b': 'new_v', 'new_v_final_g': 'new_v'}


def _forward(args):
    return _fwd_reference(*[args[k] for k in FWD_PARAMS])


def _output_shape():
    out = _jax.eval_shape(lambda: _forward(_fwd_setup_inputs(0)))
    return out.shape, out.dtype

N_MICROBATCH = 1
ADAM_LR = 0.001
ADAM_B1 = 0.9
ADAM_B2 = 0.999
ADAM_EPS = 1e-08
ADAM_WD = 0.01
ADAM_STEP = 10
PER_EXAMPLE_BATCH_AXIS = {'x': 0, 'c': 0, 'ctx': 0, 'loss_target': 0}
SHARED_INPUTS = []
_WEIGHT_DTYPES = {'c_ctx': _jnp.float32, 'ada_w': _jnp.float32, 'ada_b': _jnp.float32, 'norm1_g': _jnp.float32, 'norm2_g': _jnp.float32, 'w_out': _jnp.float32, 'mlp_w1': _jnp.float32, 'mlp_w2': _jnp.float32, 'ev_w_in': _jnp.float32, 'ev_q_norm_g': _jnp.float32, 'ev_k_norm_g': _jnp.float32, 'ev_sgu_norm_g': _jnp.float32, 'ev_sgu_w': _jnp.float32, 'ev_sgu_b': _jnp.float32, 'od_w_in': _jnp.float32, 'od_q_norm_g': _jnp.float32, 'od_kv_norm_g': _jnp.float32, 'od_w_uq': _jnp.float32, 'od_w_ukv': _jnp.float32, 'od_conv_w': _jnp.float32, 'od_conv_b': _jnp.float32, 'od_ln_g': _jnp.float32, 'od_ln_b': _jnp.float32, 'final_g': _jnp.float32}
MOMENT_SCALE = {'c_ctx': 9.476285e-03, 'ada_w': 1.551927e-01, 'ada_b': 2.706643e-01, 'norm1_g': 4.133888e-02, 'norm2_g': 7.866725e-02, 'w_out': 4.107039e-02, 'mlp_w1': 4.362341e-02, 'mlp_w2': 1.004638e-01, 'ev_w_in': 4.252833e-02, 'ev_q_norm_g': 1.407441e-02, 'ev_k_norm_g': 1.340479e-02, 'ev_sgu_norm_g': 3.922925e-02, 'ev_sgu_w': 2.818594e-02, 'ev_sgu_b': 2.760278e-02, 'od_w_in': 2.666306e-02, 'od_q_norm_g': 8.132647e-03, 'od_kv_norm_g': 8.663055e-02, 'od_w_uq': 4.509493e-03, 'od_w_ukv': 2.205977e-02, 'od_conv_w': 3.336374e-02, 'od_conv_b': 9.300647e-02, 'od_ln_g': 5.409900e-02, 'od_ln_b': 5.527040e-02, 'final_g': 3.251604e+01}


def _to_microbatches(a, axis):
    t = _jnp.moveaxis(a, axis, 0)
    t = t.reshape((N_MICROBATCH, t.shape[0] // N_MICROBATCH) + t.shape[1:])
    return _jnp.moveaxis(t, 1, axis + 1)


def setup_inputs(seed: int = 0) -> dict:
    inp = _fwd_setup_inputs(seed)
    key = _jax.random.fold_in(_jax.random.key(seed), 7919)
    shape, _ = _output_shape()
    out = dict(inp)
    out["loss_target"] = _jax.random.normal(_jax.random.fold_in(key, 0), shape, _jnp.float32)
    for i, name in enumerate(TWIN_WEIGHTS):
        w = inp[name].astype(_jnp.float32)
        if MOMENT_SCALE is None:
            s = _jnp.sqrt(_jnp.mean(_jnp.square(w)) + 1e-30)
        else:
            s = MOMENT_SCALE[name]
        km, kv = _jax.random.split(_jax.random.fold_in(key, i + 1))
        out[name] = w
        out["m_" + name] = s * _jax.random.normal(km, w.shape, _jnp.float32)
        out["v_" + name] = (s * s) * _jax.random.uniform(kv, w.shape, _jnp.float32, 0.5, 1.5)
    if N_MICROBATCH > 1:
        for name, axis in PER_EXAMPLE_BATCH_AXIS.items():
            out[name] = _to_microbatches(out[name], axis)
    return {'x': out['x'], 'c': out['c'], 'ctx': out['ctx'], 'c_ctx': out['c_ctx'], 'ada_w': out['ada_w'], 'ada_b': out['ada_b'], 'norm1_g': out['norm1_g'], 'norm2_g': out['norm2_g'], 'w_out': out['w_out'], 'mlp_w1': out['mlp_w1'], 'mlp_w2': out['mlp_w2'], 'ev_w_in': out['ev_w_in'], 'ev_q_norm_g': out['ev_q_norm_g'], 'ev_k_norm_g': out['ev_k_norm_g'], 'ev_sgu_norm_g': out['ev_sgu_norm_g'], 'ev_sgu_w': out['ev_sgu_w'], 'ev_sgu_b': out['ev_sgu_b'], 'od_w_in': out['od_w_in'], 'od_q_norm_g': out['od_q_norm_g'], 'od_kv_norm_g': out['od_kv_norm_g'], 'od_w_uq': out['od_w_uq'], 'od_w_ukv': out['od_w_ukv'], 'od_conv_w': out['od_conv_w'], 'od_conv_b': out['od_conv_b'], 'od_ln_g': out['od_ln_g'], 'od_ln_b': out['od_ln_b'], 'final_g': out['final_g'], 'loss_target': out['loss_target'], 'm_c_ctx': out['m_c_ctx'], 'm_ada_w': out['m_ada_w'], 'm_ada_b': out['m_ada_b'], 'm_norm1_g': out['m_norm1_g'], 'm_norm2_g': out['m_norm2_g'], 'm_w_out': out['m_w_out'], 'm_mlp_w1': out['m_mlp_w1'], 'm_mlp_w2': out['m_mlp_w2'], 'm_ev_w_in': out['m_ev_w_in'], 'm_ev_q_norm_g': out['m_ev_q_norm_g'], 'm_ev_k_norm_g': out['m_ev_k_norm_g'], 'm_ev_sgu_norm_g': out['m_ev_sgu_norm_g'], 'm_ev_sgu_w': out['m_ev_sgu_w'], 'm_ev_sgu_b': out['m_ev_sgu_b'], 'm_od_w_in': out['m_od_w_in'], 'm_od_q_norm_g': out['m_od_q_norm_g'], 'm_od_kv_norm_g': out['m_od_kv_norm_g'], 'm_od_w_uq': out['m_od_w_uq'], 'm_od_w_ukv': out['m_od_w_ukv'], 'm_od_conv_w': out['m_od_conv_w'], 'm_od_conv_b': out['m_od_conv_b'], 'm_od_ln_g': out['m_od_ln_g'], 'm_od_ln_b': out['m_od_ln_b'], 'm_final_g': out['m_final_g'], 'v_c_ctx': out['v_c_ctx'], 'v_ada_w': out['v_ada_w'], 'v_ada_b': out['v_ada_b'], 'v_norm1_g': out['v_norm1_g'], 'v_norm2_g': out['v_norm2_g'], 'v_w_out': out['v_w_out'], 'v_mlp_w1': out['v_mlp_w1'], 'v_mlp_w2': out['v_mlp_w2'], 'v_ev_w_in': out['v_ev_w_in'], 'v_ev_q_norm_g': out['v_ev_q_norm_g'], 'v_ev_k_norm_g': out['v_ev_k_norm_g'], 'v_ev_sgu_norm_g': out['v_ev_sgu_norm_g'], 'v_ev_sgu_w': out['v_ev_sgu_w'], 'v_ev_sgu_b': out['v_ev_sgu_b'], 'v_od_w_in': out['v_od_w_in'], 'v_od_q_norm_g': out['v_od_q_norm_g'], 'v_od_kv_norm_g': out['v_od_kv_norm_g'], 'v_od_w_uq': out['v_od_w_uq'], 'v_od_w_ukv': out['v_od_w_ukv'], 'v_od_conv_w': out['v_od_conv_w'], 'v_od_conv_b': out['v_od_conv_b'], 'v_od_ln_g': out['v_od_ln_g'], 'v_od_ln_b': out['v_od_ln_b'], 'v_final_g': out['v_final_g']}


def _loss(weights, diff, rest, loss_target):
    with _jax.named_scope("forward"):
        args = {**rest, TWIN_DIFF_INPUT: diff, **{k: w.astype(_WEIGHT_DTYPES[k]) for k, w in weights.items()}}
        y = _forward(args)
    with _jax.named_scope("loss_head"):
        err = _jnp.square(y.astype(_jnp.float32) - loss_target)
        return 0.5 * _jnp.sum(_jnp.mean(err, axis=-1)) if err.ndim else 0.5 * err


def _adamw(w, g, m, v):
    m = ADAM_B1 * m + (1.0 - ADAM_B1) * g
    v = ADAM_B2 * v + (1.0 - ADAM_B2) * _jnp.square(g)
    m_hat = m / (1.0 - ADAM_B1 ** ADAM_STEP)
    v_hat = v / (1.0 - ADAM_B2 ** ADAM_STEP)
    delta = -ADAM_LR * (m_hat / (_jnp.sqrt(v_hat) + ADAM_EPS) + ADAM_WD * w)
    return delta, m, v


def reference(x, c, ctx, c_ctx, ada_w, ada_b, norm1_g, norm2_g, w_out, mlp_w1, mlp_w2, ev_w_in, ev_q_norm_g, ev_k_norm_g, ev_sgu_norm_g, ev_sgu_w, ev_sgu_b, od_w_in, od_q_norm_g, od_kv_norm_g, od_w_uq, od_w_ukv, od_conv_w, od_conv_b, od_ln_g, od_ln_b, final_g, loss_target, m_c_ctx, m_ada_w, m_ada_b, m_norm1_g, m_norm2_g, m_w_out, m_mlp_w1, m_mlp_w2, m_ev_w_in, m_ev_q_norm_g, m_ev_k_norm_g, m_ev_sgu_norm_g, m_ev_sgu_w, m_ev_sgu_b, m_od_w_in, m_od_q_norm_g, m_od_kv_norm_g, m_od_w_uq, m_od_w_ukv, m_od_conv_w, m_od_conv_b, m_od_ln_g, m_od_ln_b, m_final_g, v_c_ctx, v_ada_w, v_ada_b, v_norm1_g, v_norm2_g, v_w_out, v_mlp_w1, v_mlp_w2, v_ev_w_in, v_ev_q_norm_g, v_ev_k_norm_g, v_ev_sgu_norm_g, v_ev_sgu_w, v_ev_sgu_b, v_od_w_in, v_od_q_norm_g, v_od_kv_norm_g, v_od_w_uq, v_od_w_ukv, v_od_conv_w, v_od_conv_b, v_od_ln_g, v_od_ln_b, v_final_g):
    given = dict(x=x, c=c, ctx=ctx, c_ctx=c_ctx, ada_w=ada_w, ada_b=ada_b, norm1_g=norm1_g, norm2_g=norm2_g, w_out=w_out, mlp_w1=mlp_w1, mlp_w2=mlp_w2, ev_w_in=ev_w_in, ev_q_norm_g=ev_q_norm_g, ev_k_norm_g=ev_k_norm_g, ev_sgu_norm_g=ev_sgu_norm_g, ev_sgu_w=ev_sgu_w, ev_sgu_b=ev_sgu_b, od_w_in=od_w_in, od_q_norm_g=od_q_norm_g, od_kv_norm_g=od_kv_norm_g, od_w_uq=od_w_uq, od_w_ukv=od_w_ukv, od_conv_w=od_conv_w, od_conv_b=od_conv_b, od_ln_g=od_ln_g, od_ln_b=od_ln_b, final_g=final_g, loss_target=loss_target, m_c_ctx=m_c_ctx, m_ada_w=m_ada_w, m_ada_b=m_ada_b, m_norm1_g=m_norm1_g, m_norm2_g=m_norm2_g, m_w_out=m_w_out, m_mlp_w1=m_mlp_w1, m_mlp_w2=m_mlp_w2, m_ev_w_in=m_ev_w_in, m_ev_q_norm_g=m_ev_q_norm_g, m_ev_k_norm_g=m_ev_k_norm_g, m_ev_sgu_norm_g=m_ev_sgu_norm_g, m_ev_sgu_w=m_ev_sgu_w, m_ev_sgu_b=m_ev_sgu_b, m_od_w_in=m_od_w_in, m_od_q_norm_g=m_od_q_norm_g, m_od_kv_norm_g=m_od_kv_norm_g, m_od_w_uq=m_od_w_uq, m_od_w_ukv=m_od_w_ukv, m_od_conv_w=m_od_conv_w, m_od_conv_b=m_od_conv_b, m_od_ln_g=m_od_ln_g, m_od_ln_b=m_od_ln_b, m_final_g=m_final_g, v_c_ctx=v_c_ctx, v_ada_w=v_ada_w, v_ada_b=v_ada_b, v_norm1_g=v_norm1_g, v_norm2_g=v_norm2_g, v_w_out=v_w_out, v_mlp_w1=v_mlp_w1, v_mlp_w2=v_mlp_w2, v_ev_w_in=v_ev_w_in, v_ev_q_norm_g=v_ev_q_norm_g, v_ev_k_norm_g=v_ev_k_norm_g, v_ev_sgu_norm_g=v_ev_sgu_norm_g, v_ev_sgu_w=v_ev_sgu_w, v_ev_sgu_b=v_ev_sgu_b, v_od_w_in=v_od_w_in, v_od_q_norm_g=v_od_q_norm_g, v_od_kv_norm_g=v_od_kv_norm_g, v_od_w_uq=v_od_w_uq, v_od_w_ukv=v_od_w_ukv, v_od_conv_w=v_od_conv_w, v_od_conv_b=v_od_conv_b, v_od_ln_g=v_od_ln_g, v_od_ln_b=v_od_ln_b, v_final_g=v_final_g)
    weights = {n: given[n] for n in TWIN_WEIGHTS}
    shared = {n: given[n] for n in SHARED_INPUTS}
    per_example = {n: given[n] for n in ['x', 'c', 'ctx']}
    grad_fn = _jax.value_and_grad(_loss, argnums=(0, 1))

    def one_microbatch(ex, loss_target):
        ex = dict(ex)
        diff = ex.pop(TWIN_DIFF_INPUT)
        return grad_fn(weights, diff, {**shared, **ex}, loss_target)

    if N_MICROBATCH == 1:
        loss, (grad_w, grad_x) = one_microbatch(per_example, given["loss_target"])
    else:
        def body(carry, xs):
            loss_sum, grad_sum = carry
            l_k, (gw_k, gx_k) = one_microbatch(xs[0], xs[1])
            with _jax.named_scope("update"):
                return (loss_sum + l_k, _jax.tree.map(_jnp.add, grad_sum, gw_k)), gx_k

        init = (_jnp.zeros((), _jnp.float32), _jax.tree.map(_jnp.zeros_like, weights))
        (loss, grad_w), grad_x = _jax.lax.scan(body, init, (per_example, given["loss_target"]))
    with _jax.named_scope("update"):
        delta_w, new_m, new_v = {}, {}, {}
        for n in TWIN_WEIGHTS:
            delta_w[n], new_m[n], new_v[n] = _adamw(weights[n], grad_w[n], given["m_" + n], given["v_" + n])
    return (loss, grad_x, *[grad_w[n] for n in TWIN_WEIGHTS], *[delta_w[n] for n in TWIN_WEIGHTS],
            *[new_m[n] for n in TWIN_WEIGHTS], *[new_v[n] for n in TWIN_WEIGHTS])
```

```python
import functools
import math

import numpy as np
import jax
import jax.numpy as jnp
from jax import lax
from jax.experimental import pallas as pl
from jax.experimental.pallas import tpu as pltpu

F32 = jnp.float32
BF = jnp.bfloat16
HI = lax.Precision.HIGHEST
MESH = pl.DeviceIdType.MESH

D = 1024
L = 2048
LC = 256
SEQ = L + LC
NEX = 2
R = NEX * SEQ
TB = 256
BPE = SEQ // TB
NBLK = R // TB
GRID_W = 64
FF = 4 * D
EPS = 1e-6
ROPE_THETA = 10000.0
N_MOD = 6
EV_IN = 1792
OD_IN = 1440
OD_PAD = 1536
VMEM_LIMIT = 60 * 1024 * 1024

ADAM_LR = 0.001
ADAM_B1 = 0.9
ADAM_B2 = 0.999
ADAM_EPS = 1e-08
ADAM_WD = 0.01
ADAM_STEP = 10

NT = (((1,), (1,)), ((), ()))
TN = (((0,), (0,)), ((), ()))


def _cparams(sem=None):
    return pltpu.CompilerParams(dimension_semantics=sem, vmem_limit_bytes=VMEM_LIMIT)


@jax.custom_vjp
def _mm(a, b):
    return jnp.dot(a.astype(BF), b.astype(BF), preferred_element_type=F32)


def _mm_fwd(a, b):
    return _mm(a, b), (a, b)


def _mm_bwd(res, g):
    a, b = res
    gb = g.astype(BF)
    da = lax.dot_general(gb, b.astype(BF), NT, preferred_element_type=F32)
    db = lax.dot_general(a.astype(BF), gb, TN, preferred_element_type=F32)
    return da, db


_mm.defvjp(_mm_fwd, _mm_bwd)


@jax.custom_vjp
def _swap(x):
    n = x.shape[-1]
    ax = x.ndim - 1
    lane = lax.broadcasted_iota(jnp.int32, x.shape, ax)
    return jnp.where(lane % 2 == 0, pltpu.roll(x, n - 1, ax), pltpu.roll(x, 1, ax))


_swap.defvjp(lambda x: (_swap(x), None), lambda _, g: (_swap(g),))


def _rope(x, cos, sin):
    return x * cos + _swap(x) * sin


def _rmsn(x, g):
    return x * lax.rsqrt(jnp.mean(x * x, axis=-1, keepdims=True) + EPS) * g


def _grmsn(x, g, avg):
    ms = jnp.dot(x * x, avg, precision=HI, preferred_element_type=F32)
    return x * lax.rsqrt(ms + EPS) * g


def _modnorm(x, g, sh, sc):
    return _rmsn(x, g) * (1.0 + sc) + sh


def _gelu(x):
    return 0.5 * x * (1.0 + jnp.tanh(0.7978845608028654 * (x + 0.044715 * (x * x * x))))


def _silu(x):
    return x * jax.nn.sigmoid(x)


def _acc(ref, val, first):
    @pl.when(first)
    def _():
        ref[...] = val

    @pl.when(jnp.logical_not(first))
    def _():
        ref[...] += val


def _seg(i):
    return 2 * (i // BPE) + jnp.minimum(i % BPE, 1)


def _seg_first(i):
    return (i % BPE) <= 1


def _rb_call(name, body, row_in=(), mod_in=(), pos_in=(), full_in=(), shift_in=(),
             row_out=(), seg_out=(), acc_out=(), scratch=()):
    in_specs, args = [], []
    for a in row_in:
        in_specs.append(pl.BlockSpec((TB, a.shape[1]), lambda i: (i, 0)))
        args.append(a)
    for tab, m in mod_in:
        in_specs.append(pl.BlockSpec((1, 1, D), lambda i, m=m: (_seg(i) * N_MOD + m, 0, 0)))
        args.append(tab)
    for a in pos_in:
        in_specs.append(pl.BlockSpec((TB, a.shape[1]), lambda i: (i % BPE, 0)))
        args.append(a)
    for a in full_in:
        in_specs.append(pl.BlockSpec(a.shape, lambda i, n=a.ndim: (0,) * n))
        args.append(a)
    for a, d in shift_in:
        in_specs.append(pl.BlockSpec((TB, a.shape[1]), lambda i, d=d: (jnp.clip(i + d, 0, NBLK - 1), 0)))
        args.append(a)
    out_specs, out_shape = [], []
    for w, dt in row_out:
        out_specs.append(pl.BlockSpec((TB, w), lambda i: (i, 0)))
        out_shape.append(jax.ShapeDtypeStruct((R, w), dt))
    for w in seg_out:
        out_specs.append(pl.BlockSpec((1, 1, w), lambda i: (_seg(i), 0, 0)))
        out_shape.append(jax.ShapeDtypeStruct((4, 1, w), F32))
    for shp in acc_out:
        out_specs.append(pl.BlockSpec(shp, lambda i, n=len(shp): (0,) * n))
        out_shape.append(jax.ShapeDtypeStruct(shp, F32))

    def kern(*refs):
        body(pl.program_id(0), *refs)

    sem = ("arbitrary",) if (seg_out or acc_out) else ("parallel",)
    return pl.pallas_call(kern, grid=(NBLK,), in_specs=in_specs, out_specs=out_specs, out_shape=out_shape,
                          scratch_shapes=list(scratch), compiler_params=_cparams(sem), name=name)(*args)


def modnorm_fwd(name, x, mods, g, m_sh, m_sc):
    def body(i, x_ref, sh_ref, sc_ref, g_ref, h_ref):
        h_ref[...] = _modnorm(x_ref[...], g_ref[...], sh_ref[0], sc_ref[0]).astype(BF)

    return _rb_call(name, body, row_in=(x,), mod_in=((mods, m_sh), (mods, m_sc)), full_in=(g,),
                    row_out=((D, BF),))[0]


def modnorm_bwd(name, x, dh, dx_in, mods, g, m_sh, m_sc):
    def body(i, x_ref, dh_ref, dxin_ref, sh_ref, sc_ref, g_ref, dx_ref, dsh_ref, dsc_ref, dg_ref):
        _, vjp = jax.vjp(_modnorm, x_ref[...], g_ref[...], sh_ref[0], sc_ref[0])
        dx, dg, dsh, dsc = vjp(dh_ref[...].astype(F32))
        dx_ref[...] = dxin_ref[...] + dx
        _acc(dsh_ref, dsh[None], _seg_first(i))
        _acc(dsc_ref, dsc[None], _seg_first(i))
        _acc(dg_ref, dg, i == 0)

    return _rb_call(name, body, row_in=(x, dh, dx_in), mod_in=((mods, m_sh), (mods, m_sc)), full_in=(g,),
                    row_out=((D, F32),), seg_out=(D, D), acc_out=((1, D),))


def gate_bwd(name, dx, y, mods, m_gate):
    def body(i, dx_ref, y_ref, gt_ref, dy_ref, dgt_ref):
        dxv = dx_ref[...]
        dy_ref[...] = (dxv * gt_ref[0]).astype(BF)
        _acc(dgt_ref, jnp.sum(dxv * y_ref[...].astype(F32), axis=0, keepdims=True)[None], _seg_first(i))

    return _rb_call(name, body, row_in=(dx, y), mod_in=((mods, m_gate),), row_out=((D, BF),), seg_out=(D,))


def proj_in(name, h, w):
    n = w.shape[1]

    def body(i, h_ref, w_ref, o_ref):
        o_ref[...] = jnp.dot(h_ref[...], w_ref[...], preferred_element_type=F32).astype(BF)

    return _rb_call(name, body, row_in=(h,), full_in=(w,), row_out=((n, BF),))[0]


def proj_out(name, a1, a2, w, x, mods, m_gate):
    k1 = a1.shape[1]

    def body(i, a1_ref, a2_ref, x_ref, gt_ref, w_ref, xo_ref, y_ref):
        y = jnp.dot(a1_ref[...], w_ref[:k1, :], preferred_element_type=F32)
        y = y + jnp.dot(a2_ref[...], w_ref[k1:, :], preferred_element_type=F32)
        y_ref[...] = y.astype(BF)
        xo_ref[...] = x_ref[...] + gt_ref[0] * y

    return _rb_call(name, body, row_in=(a1, a2, x), mod_in=((mods, m_gate),), full_in=(w,),
                    row_out=((D, F32), (D, BF)))


def mlp_up(name, h, w1):
    def body(i, h_ref, w_ref, a_ref, f_ref):
        hv = h_ref[...]
        for n in range(4):
            a = jnp.dot(hv, w_ref[n], preferred_element_type=F32)
            a_ref[:, n * D:(n + 1) * D] = a.astype(BF)
            r = jnp.maximum(a, 0.0)
            f_ref[:, n * D:(n + 1) * D] = (r * r).astype(BF)

    return _rb_call(name, body, row_in=(h,), full_in=(w1,), row_out=((FF, BF), (FF, BF)))


def mlp_down(name, f, w2, x, mods, m_gate):
    def body(i, f_ref, x_ref, gt_ref, w_ref, xo_ref, y_ref):
        y = jnp.dot(f_ref[:, 0:D], w_ref[0], preferred_element_type=F32)
        for n in range(1, 4):
            y = y + jnp.dot(f_ref[:, n * D:(n + 1) * D], w_ref[n], preferred_element_type=F32)
        y_ref[...] = y.astype(BF)
        xo_ref[...] = x_ref[...] + gt_ref[0] * y

    return _rb_call(name, body, row_in=(f, x), mod_in=((mods, m_gate),), full_in=(w2,),
                    row_out=((D, F32), (D, BF)))


def mm_nt(name, g, w):
    k = w.shape[0]

    def body(i, g_ref, w_ref, o_ref):
        o_ref[...] = lax.dot_general(g_ref[...], w_ref[...], NT, preferred_element_type=F32).astype(BF)

    return _rb_call(name, body, row_in=(g,), full_in=(w,), row_out=((k, BF),))[0]


def mlp_bwd_da(name, dy, w2, a):
    def body(i, dy_ref, a_ref, w_ref, da_ref):
        dyv = dy_ref[...]
        for n in range(4):
            df = lax.dot_general(dyv, w_ref[n], NT, preferred_element_type=F32)
            av = a_ref[:, n * D:(n + 1) * D].astype(F32)
            da_ref[:, n * D:(n + 1) * D] = (df * (2.0 * jnp.maximum(av, 0.0))).astype(BF)

    return _rb_call(name, body, row_in=(dy, a), full_in=(w2,), row_out=((FF, BF),))[0]


def mlp_bwd_dh(name, da, w1):
    def body(i, da_ref, w_ref, dh_ref):
        acc = lax.dot_general(da_ref[:, 0:D], w_ref[0], NT, preferred_element_type=F32)
        for n in range(1, 4):
            acc = acc + lax.dot_general(da_ref[:, n * D:(n + 1) * D], w_ref[n], NT, preferred_element_type=F32)
        dh_ref[...] = acc.astype(BF)

    return _rb_call(name, body, row_in=(da,), full_in=(w1,), row_out=((D, BF),))[0]


TN_ROWS = 512


def mm_tn(name, a, g, tiles, th, tw):
    nt = len(tiles)
    acs = jnp.asarray([t[0] for t in tiles], jnp.int32)
    gcs = jnp.asarray([t[1] for t in tiles], jnp.int32)
    nr = R // TN_ROWS

    def kern(ac_ref, gc_ref, a_ref, g_ref, o_ref, acc_ref):
        r = pl.program_id(1)

        @pl.when(r == 0)
        def _():
            acc_ref[...] = jnp.zeros_like(acc_ref)

        acc_ref[...] += lax.dot_general(a_ref[...], g_ref[...], TN, preferred_element_type=F32)

        @pl.when(r == nr - 1)
        def _():
            o_ref[...] = acc_ref[...].astype(BF)

    grid_spec = pltpu.PrefetchScalarGridSpec(
        num_scalar_prefetch=2, grid=(nt, nr),
        in_specs=[pl.BlockSpec((TN_ROWS, th), lambda t, r, ac, gc: (r, ac[t])),
                  pl.BlockSpec((TN_ROWS, tw), lambda t, r, ac, gc: (r, gc[t]))],
        out_specs=pl.BlockSpec((None, th, tw), lambda t, r, ac, gc: (t, 0, 0)),
        scratch_shapes=[pltpu.VMEM((th, tw), F32)])
    return pl.pallas_call(kern, grid_spec=grid_spec, out_shape=jax.ShapeDtypeStruct((nt, th, tw), BF),
                          compiler_params=_cparams(("parallel", "arbitrary")), name=name)(acs, gcs, a, g)


def _even_tok(q, k, zu, zv, gq, gk, gs, ws, bs, cq, sq, ck, sk, avg, masks):
    qr = _rope(_grmsn(q, gq, avg), cq, sq)
    kr = _rope(_grmsn(k, gk, avg[:128, :128]), ck, sk)
    u = _gelu(zu)
    v = _grmsn(_gelu(zv), gs, avg)
    sv = None
    for g in range(8):
        t = masks[g] * (_mm(ws[g], v) + bs[g])
        sv = t if sv is None else sv + t
    return qr, kr, u * sv


def even_tok_fwd(p, cos, sin, gq, gk, gs, sgu_w, sgu_b, avg, masks):
    def body(i, p_ref, cos_ref, sin_ref, gq_ref, gk_ref, gs_ref, w_ref, b_ref, avg_ref, mk_ref, q_ref, kv_ref, m_ref):
        avgv = avg_ref[...]
        ws = [w_ref[g] for g in range(8)]
        bs = [b_ref[g] for g in range(8)]
        mks = [mk_ref[g] for g in range(8)]
        for c in range(2):
            rs = pl.ds(c * 128, 128)
            qr, kr, m = _even_tok(
                p_ref[rs, 0:512].astype(F32), p_ref[rs, 512:640].astype(F32),
                p_ref[rs, 768:1280].astype(F32), p_ref[rs, 1280:1792].astype(F32),
                gq_ref[...], gk_ref[...], gs_ref[...], ws, bs,
                cos_ref[rs, :], sin_ref[rs, :], cos_ref[rs, 0:128], sin_ref[rs, 0:128], avgv, mks)
            q_ref[rs, :] = qr.astype(BF)
            kv_ref[rs, 0:128] = kr.astype(BF)
            kv_ref[rs, 128:256] = p_ref[rs, 640:768]
            m_ref[rs, :] = m.astype(BF)

    return _rb_call("even_tok_fwd", body, row_in=(p,), pos_in=(cos, sin),
                    full_in=(gq, gk, gs, sgu_w, sgu_b, avg, masks), row_out=((512, BF), (256, BF), (512, BF)))


def even_tok_bwd(p, dq, dkv, dcat, cos, sin, gq, gk, gs, sgu_w, sgu_b, avg, masks):
    def body(i, p_ref, dq_ref, dkv_ref, dcat_ref, cos_ref, sin_ref, gq_ref, gk_ref, gs_ref, w_ref, b_ref,
             avg_ref, mk_ref, dp_ref, dgq_ref, dgk_ref, dgs_ref, dw_ref, db_ref):
        avgv = avg_ref[...]
        ws = [w_ref[g] for g in range(8)]
        bs = [b_ref[g] for g in range(8)]
        mks = [mk_ref[g] for g in range(8)]
        tot = None
        for c in range(2):
            rs = pl.ds(c * 128, 128)
            cq, sq, ck, sk = cos_ref[rs, :], sin_ref[rs, :], cos_ref[rs, 0:128], sin_ref[rs, 0:128]

            def f(q, k, zu, zv, gq, gk, gs, ws, bs):
                return _even_tok(q, k, zu, zv, gq, gk, gs, ws, bs, cq, sq, ck, sk, avgv, mks)

            _, vjp = jax.vjp(f, p_ref[rs, 0:512].astype(F32), p_ref[rs, 512:640].astype(F32),
                             p_ref[rs, 768:1280].astype(F32), p_ref[rs, 1280:1792].astype(F32),
                             gq_ref[...], gk_ref[...], gs_ref[...], ws, bs)
            d = vjp((dq_ref[rs, :].astype(F32), dkv_ref[rs, 0:128], dcat_ref[rs, 512:1024].astype(F32)))
            dp_ref[rs, 0:512] = d[0].astype(BF)
            dp_ref[rs, 512:640] = d[1].astype(BF)
            dp_ref[rs, 640:768] = dkv_ref[rs, 128:256].astype(BF)
            dp_ref[rs, 768:1280] = d[2].astype(BF)
            dp_ref[rs, 1280:1792] = d[3].astype(BF)
            part = [d[4], d[5], d[6]] + list(d[7]) + list(d[8])
            tot = part if tot is None else [a + b for a, b in zip(tot, part)]
        refs = [dgq_ref, dgk_ref, dgs_ref] + [dw_ref.at[g] for g in range(8)] + [db_ref.at[g] for g in range(8)]
        for ref, val in zip(refs, tot):
            _acc(ref, val, i == 0)

    return _rb_call("even_tok_bwd", body, row_in=(p, dq, dkv, dcat), pos_in=(cos, sin),
                    full_in=(gq, gk, gs, sgu_w, sgu_b, avg, masks), row_out=((EV_IN, BF),),
                    acc_out=((1, 512), (1, 128), (1, 512), (8, 128, 128), (8, 128, 1)))


def _odd_tok(cq, ckv, kr, za, zg, gq, gkv, wqn, wqr, wkk, wkv, cr, sr, ck, sk):
    cqn = _rmsn(cq, gq)
    qn = _mm(cqn, wqn)
    qr = _rope(_mm(cqn, wqr), cr, sr)
    ckn = _rmsn(ckv, gkv)
    kn = _mm(ckn, wkk)
    v = _mm(ckn, wkv)
    krr = _rope(kr, ck, sk)
    y = za * jax.nn.sigmoid(zg)
    return qn, qr, kn, v, krr, y


def odd_tok_fwd(p, cos, sin, gq, gkv, wqn, wqr, wkk, wkv):
    def body(i, p_ref, cos_ref, sin_ref, gq_ref, gkv_ref, wqn_ref, wqr_ref, wkk_ref, wkv_ref, q_ref, kv_ref, y_ref):
        qn, qr, kn, v, krr, y = _odd_tok(
            p_ref[:, 0:256].astype(F32), p_ref[:, 256:384].astype(F32), p_ref[:, 384:512].astype(F32),
            p_ref[:, 512:1024].astype(F32), p_ref[:, 1024:1536].astype(F32),
            gq_ref[...], gkv_ref[...], wqn_ref[...], wqr_ref[...], wkk_ref[...], wkv_ref[...],
            cos_ref[:, 0:256], sin_ref[:, 0:256], cos_ref[:, 256:384], sin_ref[:, 256:384])
        q_ref[:, 0:512] = qn.astype(BF)
        q_ref[:, 512:768] = qr.astype(BF)
        kv_ref[:, 0:512] = kn.astype(BF)
        kv_ref[:, 512:1024] = v.astype(BF)
        kv_ref[:, 1024:1152] = krr.astype(BF)
        y_ref[...] = y.astype(BF)

    return _rb_call("odd_tok_fwd", body, row_in=(p,), pos_in=(cos, sin), full_in=(gq, gkv, wqn, wqr, wkk, wkv),
                    row_out=((768, BF), (1152, BF), (512, BF)))


def odd_tok_bwd(p, dq, dkv, dy, cos, sin, gq, gkv, wqn, wqr, wkk, wkv):
    def body(i, p_ref, dq_ref, dkv_ref, dy_ref, cos_ref, sin_ref, gq_ref, gkv_ref, wqn_ref, wqr_ref, wkk_ref,
             wkv_ref, dp_ref, dgq_ref, dgkv_ref, dwqn_ref, dwqr_ref, dwkk_ref, dwkv_ref):
        cr, sr, ck, sk = cos_ref[:, 0:256], sin_ref[:, 0:256], cos_ref[:, 256:384], sin_ref[:, 256:384]

        def f(cq, ckv, kr, za, zg, gq, gkv, wqn, wqr, wkk, wkv):
            return _odd_tok(cq, ckv, kr, za, zg, gq, gkv, wqn, wqr, wkk, wkv, cr, sr, ck, sk)

        _, vjp = jax.vjp(f, p_ref[:, 0:256].astype(F32), p_ref[:, 256:384].astype(F32),
                         p_ref[:, 384:512].astype(F32), p_ref[:, 512:1024].astype(F32),
                         p_ref[:, 1024:1536].astype(F32), gq_ref[...], gkv_ref[...], wqn_ref[...],
                         wqr_ref[...], wkk_ref[...], wkv_ref[...])
        d = vjp((dq_ref[:, 0:512].astype(F32), dq_ref[:, 512:768].astype(F32), dkv_ref[:, 0:512],
                 dkv_ref[:, 512:1024], dkv_ref[:, 1024:1152], dy_ref[...].astype(F32)))
        dp_ref[:, 0:256] = d[0].astype(BF)
        dp_ref[:, 256:384] = d[1].astype(BF)
        dp_ref[:, 384:512] = d[2].astype(BF)
        dp_ref[:, 512:1024] = d[3].astype(BF)
        dp_ref[:, 1024:1536] = d[4].astype(BF)
        for ref, val in zip((dgq_ref, dgkv_ref, dwqn_ref, dwqr_ref, dwkk_ref, dwkv_ref), d[5:]):
            _acc(ref, val, i == 0)

    return _rb_call("odd_tok_bwd", body, row_in=(p, dq, dkv, dy), pos_in=(cos, sin),
                    full_in=(gq, gkv, wqn, wqr, wkk, wkv), row_out=((OD_PAD, BF),),
                    acc_out=((1, 256), (1, 128), (256, 512), (256, 256), (128, 512), (128, 512)))


GQA_HEADS = [([(64 * h, 64 * (h // 4), 64)], 128 + 64 * (h // 4)) for h in range(8)]
MLA_HEADS = [([(64 * h, 64 * h, 64), (512 + 32 * h, 1024, 32)], 512 + 64 * h) for h in range(8)]


def _scores(q_ref, kv_ref, parts, scale, valid):
    s = None
    for qo, ko, w in parts:
        t = lax.dot_general(q_ref[:, qo:qo + w], kv_ref[:, ko:ko + w], NT, preferred_element_type=F32)
        s = t if s is None else s + t
    return jnp.where(valid, s * scale, -1e30)


def _key_mask(j):
    kpos = lax.broadcasted_iota(jnp.int32, (TB, SEQ), 1)
    return kpos < jnp.where(j == 0, LC, SEQ)


def attn_fwd(name, q, kv, heads, dk):
    scale = dk ** -0.5
    qw, kvw = q.shape[1], kv.shape[1]

    def kern(q_ref, kv_ref, o_ref, lse_ref):
        valid = _key_mask(pl.program_id(1))
        for h, (parts, vo) in enumerate(heads):
            s = _scores(q_ref, kv_ref, parts, scale, valid)
            m = jnp.max(s, axis=-1, keepdims=True)
            p = jnp.exp(s - m)
            l = jnp.sum(p, axis=-1, keepdims=True)
            o = jnp.dot(p.astype(BF), kv_ref[:, vo:vo + 64], preferred_element_type=F32) / l
            o_ref[:, 64 * h:64 * h + 64] = o.astype(BF)
            lse_ref[:, h:h + 1] = m + jnp.log(l)

    return pl.pallas_call(
        kern, grid=(NEX, BPE),
        in_specs=[pl.BlockSpec((TB, qw), lambda e, j: (e * BPE + j, 0)),
                  pl.BlockSpec((SEQ, kvw), lambda e, j: (e, 0))],
        out_specs=[pl.BlockSpec((TB, 512), lambda e, j: (e * BPE + j, 0)),
                   pl.BlockSpec((TB, 8), lambda e, j: (e * BPE + j, 0))],
        out_shape=[jax.ShapeDtypeStruct((R, 512), BF), jax.ShapeDtypeStruct((R, 8), F32)],
        compiler_params=_cparams(("parallel", "arbitrary")), name=name)(q, kv)


def attn_bwd(name, q, kv, o, dcat, lse, heads, dk):
    scale = dk ** -0.5
    qw, kvw = q.shape[1], kv.shape[1]

    def kern(q_ref, kv_ref, o_ref, do_ref, lse_ref, dq_ref, dkv_ref):
        j = pl.program_id(1)
        valid = _key_mask(j)

        @pl.when(j == 0)
        def _():
            dkv_ref[...] = jnp.zeros_like(dkv_ref)

        for h, (parts, vo) in enumerate(heads):
            s = _scores(q_ref, kv_ref, parts, scale, valid)
            p = jnp.exp(s - lse_ref[:, h:h + 1])
            do = do_ref[:, 64 * h:64 * h + 64]
            dsum = jnp.sum(do.astype(F32) * o_ref[:, 64 * h:64 * h + 64].astype(F32), axis=-1, keepdims=True)
            dp = lax.dot_general(do, kv_ref[:, vo:vo + 64], NT, preferred_element_type=F32)
            ds = (p * (dp - dsum) * scale).astype(BF)
            dkv_ref[:, vo:vo + 64] += lax.dot_general(p.astype(BF), do, TN, preferred_element_type=F32)
            for qo, ko, w in parts:
                dq_ref[:, qo:qo + w] = jnp.dot(ds, kv_ref[:, ko:ko + w], preferred_element_type=F32).astype(BF)
                dkv_ref[:, ko:ko + w] += lax.dot_general(ds, q_ref[:, qo:qo + w], TN, preferred_element_type=F32)

    return pl.pallas_call(
        kern, grid=(NEX, BPE),
        in_specs=[pl.BlockSpec((TB, qw), lambda e, j: (e * BPE + j, 0)),
                  pl.BlockSpec((SEQ, kvw), lambda e, j: (e, 0)),
                  pl.BlockSpec((TB, 512), lambda e, j: (e * BPE + j, 0)),
                  pl.BlockSpec((TB, 512), lambda e, j: (e * BPE + j, 0)),
                  pl.BlockSpec((TB, 8), lambda e, j: (e * BPE + j, 0))],
        out_specs=[pl.BlockSpec((TB, qw), lambda e, j: (e * BPE + j, 0)),
                   pl.BlockSpec((SEQ, kvw), lambda e, j: (e, 0))],
        out_shape=[jax.ShapeDtypeStruct((R, qw), BF), jax.ShapeDtypeStruct((R, kvw), F32)],
        compiler_params=_cparams(("parallel", "arbitrary")), name=name)(q, kv, o, dcat, lse)


HALO = 16
CONV_K = 31


def _fill_ext(ext_ref, prev_ref, cur_ref, next_ref, i):
    j = i % BPE
    has_prev = (j >= 2).astype(F32)
    has_next = jnp.logical_and(j >= 1, j <= BPE - 2).astype(F32)
    ext_ref[0:HALO, :] = prev_ref[TB - HALO:TB, :].astype(F32) * has_prev
    ext_ref[HALO:HALO + TB, :] = cur_ref[...].astype(F32)
    ext_ref[HALO + TB:2 * HALO + TB, :] = next_ref[0:HALO, :].astype(F32) * has_next


def _ln_silu(z, g, b):
    mu = jnp.mean(z, axis=-1, keepdims=True)
    zc = z - mu
    var = jnp.mean(zc * zc, axis=-1, keepdims=True)
    return _silu(zc * lax.rsqrt(var + EPS) * g + b)


def conf_fwd(y, cw, cb, lg, lb):
    def body(i, cur_ref, cw_ref, cb_ref, lg_ref, lb_ref, prev_ref, next_ref, z_ref, c_ref, ext_ref):
        _fill_ext(ext_ref, prev_ref, cur_ref, next_ref, i)
        acc = ext_ref[1:1 + TB, :] * cw_ref[0:1, :]
        for k in range(1, CONV_K):
            acc = acc + ext_ref[k + 1:k + 1 + TB, :] * cw_ref[k:k + 1, :]
        z = acc + cb_ref[...]
        z_ref[...] = z.astype(BF)
        c_ref[...] = _ln_silu(z, lg_ref[...], lb_ref[...]).astype(BF)

    return _rb_call("conf_fwd", body, row_in=(y,), full_in=(cw, cb, lg, lb), shift_in=((y, -1), (y, 1)),
                    row_out=((512, BF), (512, BF)), scratch=(pltpu.VMEM((TB + 2 * HALO, 512), F32),))


def conf_bwd_ln(z, dcat, lg, lb):
    def body(i, z_ref, dcat_ref, lg_ref, lb_ref, dz_ref, dlg_ref, dlb_ref, dcb_ref):
        _, vjp = jax.vjp(_ln_silu, z_ref[...].astype(F32), lg_ref[...], lb_ref[...])
        dz, dlg, dlb = vjp(dcat_ref[:, 512:1024].astype(F32))
        dz_ref[...] = dz.astype(BF)
        _acc(dlg_ref, dlg, i == 0)
        _acc(dlb_ref, dlb, i == 0)
        _acc(dcb_ref, jnp.sum(dz, axis=0, keepdims=True), i == 0)

    return _rb_call("conf_bwd_ln", body, row_in=(z, dcat), full_in=(lg, lb), row_out=((512, BF),),
                    acc_out=((1, 512), (1, 512), (1, 512)))


def conf_bwd_conv(y, dz, cw):
    def body(i, y_ref, dz_ref, cw_ref, yp_ref, yn_ref, dzp_ref, dzn_ref, dy_ref, dcw_ref, exty_ref, extd_ref):
        _fill_ext(exty_ref, yp_ref, y_ref, yn_ref, i)
        _fill_ext(extd_ref, dzp_ref, dz_ref, dzn_ref, i)
        dzv = dz_ref[...].astype(F32)
        @pl.when(i == 0)
        def _():
            dcw_ref[...] = jnp.zeros_like(dcw_ref)

        acc = None
        for k in range(CONV_K):
            t = extd_ref[CONV_K - k:CONV_K - k + TB, :] * cw_ref[k:k + 1, :]
            acc = t if acc is None else acc + t
            dcw_ref[k:k + 1, :] += jnp.sum(dzv * exty_ref[k + 1:k + 1 + TB, :], axis=0, keepdims=True)
        dy_ref[...] = acc.astype(BF)

    return _rb_call("conf_bwd_conv", body, row_in=(y, dz), full_in=(cw,),
                    shift_in=((y, -1), (y, 1), (dz, -1), (dz, 1)), row_out=((512, BF),), acc_out=((32, 512),),
                    scratch=(pltpu.VMEM((TB + 2 * HALO, 512), F32), pltpu.VMEM((TB + 2 * HALO, 512), F32)))


def final_loss(x, target, fg):
    lpb = L // TB

    def kern(x_ref, t_ref, g_ref, dx_ref, loss_ref, dg_ref):
        i = pl.program_id(0)
        lat = (i % BPE) >= 1
        xv, tv = x_ref[...], t_ref[...]

        def f(x, g):
            err = _rmsn(x, g) - tv
            rowsum = jnp.sum(err * err, axis=-1, keepdims=True)
            return jnp.sum(rowsum, axis=0, keepdims=True) * (0.5 / D)

        lv, vjp = jax.vjp(f, xv, g_ref[...])
        dx, dg = vjp(jnp.ones((1, 1), F32))
        m = lat.astype(F32)
        dx_ref[...] = dx * m
        _acc(loss_ref, jnp.zeros((8, 128), F32) + lv * m, i == 0)
        _acc(dg_ref, dg * m, i == 0)

    return pl.pallas_call(
        kern, grid=(NBLK,),
        in_specs=[pl.BlockSpec((TB, D), lambda i: (i, 0)),
                  pl.BlockSpec((TB, D), lambda i: ((i // BPE) * lpb + jnp.maximum(i % BPE - 1, 0), 0)),
                  pl.BlockSpec((1, D), lambda i: (0, 0))],
        out_specs=[pl.BlockSpec((TB, D), lambda i: (i, 0)), pl.BlockSpec((8, 128), lambda i: (0, 0)),
                   pl.BlockSpec((1, D), lambda i: (0, 0))],
        out_shape=[jax.ShapeDtypeStruct((R, D), F32), jax.ShapeDtypeStruct((8, 128), F32),
                   jax.ShapeDtypeStruct((1, D), F32)],
        compiler_params=_cparams(("arbitrary",)), name="final_loss")(x, target, fg)


NC = 24


def mods_fwd(call, ada_w, ada_b):
    cols = ada_w.shape[2]

    def kern(c_ref, w_ref, b_ref, o_ref):
        o_ref[...] = jnp.dot(_silu(c_ref[...]), w_ref[...], precision=HI, preferred_element_type=F32) + b_ref[...]

    return pl.pallas_call(
        kern, grid=(2,),
        in_specs=[pl.BlockSpec((NC, D), lambda l: (0, 0)), pl.BlockSpec((None, D, cols), lambda l: (l, 0, 0)),
                  pl.BlockSpec((None, 1, cols), lambda l: (l, 0, 0))],
        out_specs=pl.BlockSpec((None, NC, cols), lambda l: (l, 0, 0)),
        out_shape=jax.ShapeDtypeStruct((2, NC, cols), F32),
        compiler_params=_cparams(("parallel",)), name="mods_fwd")(call, ada_w, ada_b)


def ada_bwd(call, ada_w, dm):
    cols = ada_w.shape[2]

    def kern(c_ref, w_ref, dm_ref, gw_ref, dc_ref):
        l = pl.program_id(0)
        gw_ref[...] = lax.dot_general(_silu(c_ref[...]), dm_ref[...], TN, precision=HI, preferred_element_type=F32)
        part = lax.dot_general(dm_ref[16:24, :], w_ref[...], NT, precision=HI, preferred_element_type=F32)
        cc = c_ref[16:17, :]
        sg = jax.nn.sigmoid(cc)
        _acc(dc_ref, part * (sg * (1.0 + cc * (1.0 - sg))), l == 0)

    return pl.pallas_call(
        kern, grid=(2,),
        in_specs=[pl.BlockSpec((NC, D), lambda l: (0, 0)), pl.BlockSpec((None, D, cols), lambda l: (l, 0, 0)),
                  pl.BlockSpec((None, NC, cols), lambda l: (l, 0, 0))],
        out_specs=[pl.BlockSpec((None, D, cols), lambda l: (l, 0, 0)), pl.BlockSpec((8, D), lambda l: (0, 0))],
        out_shape=[jax.ShapeDtypeStruct((2, D, cols), F32), jax.ShapeDtypeStruct((8, D), F32)],
        compiler_params=_cparams(("arbitrary",)), name="ada_bwd")(call, ada_w, dm)


def sum_lead(name, a, scale_last=None):
    n, r, c = a.shape
    tr = r
    for cand in (512, 256, 128, 64, 32, 16, 8):
        if r % cand == 0 and cand * c * 4 * n <= 8 * 1024 * 1024:
            tr = cand
            break

    def kern(a_ref, o_ref):
        acc = a_ref[0].astype(F32)
        for k in range(1, n):
            acc = acc + a_ref[k].astype(F32)
        o_ref[...] = acc

    return pl.pallas_call(kern, grid=(r // tr,), in_specs=[pl.BlockSpec((n, tr, c), lambda i: (0, i, 0))],
                          out_specs=pl.BlockSpec((tr, c), lambda i: (i, 0)),
                          out_shape=jax.ShapeDtypeStruct((r, c), F32),
                          compiler_params=_cparams(("parallel",)), name=name)(a)


def add_pairs(name, a, b):
    _, r, c = a.shape

    def kern(a_ref, b_ref, o_ref):
        o_ref[...] = (a_ref[...].astype(F32) + b_ref[...].astype(F32)).astype(BF)

    spec = pl.BlockSpec((None, r, c), lambda i: (i, 0, 0))
    return pl.pallas_call(kern, grid=(4,), in_specs=[spec, spec], out_specs=spec,
                          out_shape=jax.ShapeDtypeStruct(a.shape, BF),
                          compiler_params=_cparams(("parallel",)), name=name)(a, b)


def adamw(name, w, g, m, v):
    r, c = w.shape
    tr = r
    for cand in (512, 256, 128, 64, 32, 16, 8):
        if r % cand == 0 and cand * c * 4 <= 2 * 1024 * 1024:
            tr = cand
            break
    c1 = 1.0 / (1.0 - ADAM_B1 ** ADAM_STEP)
    c2 = 1.0 / (1.0 - ADAM_B2 ** ADAM_STEP)

    def kern(w_ref, g_ref, m_ref, v_ref, d_ref, mo_ref, vo_ref):
        gv = g_ref[...]
        mn = ADAM_B1 * m_ref[...] + (1.0 - ADAM_B1) * gv
        vn = ADAM_B2 * v_ref[...] + (1.0 - ADAM_B2) * (gv * gv)
        d_ref[...] = -ADAM_LR * ((mn * c1) / (jnp.sqrt(vn * c2) + ADAM_EPS) + ADAM_WD * w_ref[...])
        mo_ref[...] = mn
        vo_ref[...] = vn

    spec = pl.BlockSpec((tr, c), lambda i: (i, 0))
    shp = jax.ShapeDtypeStruct((r, c), F32)
    return pl.pallas_call(kern, grid=(r // tr,), in_specs=[spec] * 4, out_specs=[spec] * 3, out_shape=[shp] * 3,
                          compiler_params=_cparams(("parallel",)), name=name)(w, g, m, v)


def all_gather8(name, xs):
    m_per, n = xs.shape

    def body(x_ref, out_ref, send_sems, recv_sems, local_sem):
        x, y, c = lax.axis_index("x"), lax.axis_index("y"), lax.axis_index("c")
        me, sibling = (x, y, c), (x, y, 1 - c)
        chips = [(1 - x, y), (x, 1 - y), (1 - x, 1 - y)]

        def rows(px, py, pc):
            return out_ref.at[pl.ds((4 * px + 2 * py + pc) * m_per, m_per), :]

        def copy(k, block, to, src=None):
            return pltpu.make_async_remote_copy(
                src_ref=rows(*block) if src is None else src, dst_ref=rows(*block),
                send_sem=send_sems.at[k], recv_sem=recv_sems.at[k], device_id=to, device_id_type=MESH)

        mine = pltpu.make_async_copy(x_ref, rows(*me), local_sem)
        mine.start()
        first = [copy(0, me, sibling, src=x_ref)]
        first += [copy(1 + j, me, (*chip, c), src=x_ref) for j, chip in enumerate(chips)]
        for cp in first:
            cp.start()
        passed = [copy(4 + j, (*chip, c), sibling) for j, chip in enumerate(chips)]
        for j, chip in enumerate(chips):
            copy(1 + j, (*chip, c), me).wait_recv()
            passed[j].start()
        copy(0, sibling, me).wait_recv()
        for j, chip in enumerate(chips):
            copy(4 + j, (*chip, 1 - c), me).wait_recv()
        for cp in first + passed:
            cp.wait_send()
        mine.wait()

    return pl.pallas_call(
        body, out_shape=jax.ShapeDtypeStruct((8 * m_per, n), xs.dtype),
        in_specs=[pl.BlockSpec(memory_space=pltpu.VMEM)], out_specs=pl.BlockSpec(memory_space=pltpu.VMEM),
        scratch_shapes=[pltpu.SemaphoreType.DMA((7,)), pltpu.SemaphoreType.DMA((7,)), pltpu.SemaphoreType.DMA],
        compiler_params=pltpu.CompilerParams(vmem_limit_bytes=VMEM_LIMIT), name=name)(xs)


def sibling_exchange(name, srcs):
    n = len(srcs)

    def body(*refs):
        src, dst = refs[:n], refs[n:2 * n]
        send_sems, recv_sems = refs[2 * n], refs[2 * n + 1]
        sibling = (lax.axis_index("x"), lax.axis_index("y"), 1 - lax.axis_index("c"))
        cps = [pltpu.make_async_remote_copy(src_ref=src[k], dst_ref=dst[k], send_sem=send_sems.at[k],
                                            recv_sem=recv_sems.at[k], device_id=sibling, device_id_type=MESH)
               for k in range(n)]
        for cp in cps:
            cp.start()
        for cp in cps:
            cp.wait_recv()
        for cp in cps:
            cp.wait_send()

    anyspec = pl.BlockSpec(memory_space=pl.ANY)
    return pl.pallas_call(
        body, out_shape=[jax.ShapeDtypeStruct(s.shape, s.dtype) for s in srcs],
        in_specs=[anyspec] * n, out_specs=[anyspec] * n,
        scratch_shapes=[pltpu.SemaphoreType.DMA((n,)), pltpu.SemaphoreType.DMA((n,))], name=name)(*srcs)


def chip_exchange(name, srcs, scatter):
    n = len(srcs)
    shapes = [s.shape[1:] if scatter else s.shape for s in srcs]

    def body(*refs):
        src, dst = refs[:n], refs[n:2 * n]
        send_sems, recv_sems, local_sems = refs[2 * n], refs[2 * n + 1], refs[2 * n + 2]
        x, y, c = lax.axis_index("x"), lax.axis_index("y"), lax.axis_index("c")
        me = 2 * x + y
        chips = [(1 - x, y), (x, 1 - y), (1 - x, 1 - y)]
        local, sends, recvs = [], [], []
        for a in range(n):
            own = src[a].at[me] if scatter else src[a]
            local.append(pltpu.make_async_copy(own, dst[a].at[me], local_sems.at[a]))
            for j, (px, py) in enumerate(chips):
                to = 2 * px + py
                sends.append(pltpu.make_async_remote_copy(
                    src_ref=src[a].at[to] if scatter else src[a], dst_ref=dst[a].at[me],
                    send_sem=send_sems.at[3 * a + j], recv_sem=recv_sems.at[3 * a + j],
                    device_id=(px, py, c), device_id_type=MESH))
                recvs.append(pltpu.make_async_remote_copy(
                    src_ref=own, dst_ref=dst[a].at[to],
                    send_sem=send_sems.at[3 * a + j], recv_sem=recv_sems.at[3 * a + j],
                    device_id=(px, py, c), device_id_type=MESH))
        for cp in local + sends:
            cp.start()
        for cp in recvs:
            cp.wait_recv()
        for cp in sends:
            cp.wait_send()
        for cp in local:
            cp.wait()

    anyspec = pl.BlockSpec(memory_space=pl.ANY)
    return pl.pallas_call(
        body, out_shape=[jax.ShapeDtypeStruct((4,) + tuple(shp), s.dtype) for shp, s in zip(shapes, srcs)],
        in_specs=[anyspec] * n, out_specs=[anyspec] * n,
        scratch_shapes=[pltpu.SemaphoreType.DMA((3 * n,)), pltpu.SemaphoreType.DMA((3 * n,)),
                        pltpu.SemaphoreType.DMA((n,))], name=name)(*srcs)


def _rope_tables(d_rot, reps):
    rows = L // GRID_W
    row = np.repeat(np.arange(rows), GRID_W).astype(np.float32)
    col = np.tile(np.arange(GRID_W), rows).astype(np.float32)
    d_axis = d_rot // 2
    inv = (ROPE_THETA ** (-np.arange(0, d_axis, 2, dtype=np.float32) / d_axis)).astype(np.float32)
    ang = jnp.asarray(np.concatenate([row[:, None] * inv, col[:, None] * inv], axis=-1))
    cos, sin = jnp.cos(ang), jnp.sin(ang)
    c = jnp.repeat(cos, 2, axis=-1)
    s = jnp.stack([-sin, sin], axis=-1).reshape(L, d_rot)
    c = jnp.concatenate([jnp.ones((LC, d_rot), F32), c], axis=0)
    s = jnp.concatenate([jnp.zeros((LC, d_rot), F32), s], axis=0)
    return jnp.tile(c, (1, reps)), jnp.tile(s, (1, reps))


def _group_consts():
    g = np.arange(512) // 64
    avg = (g[:, None] == g[None, :]).astype(np.float32) / 64.0
    masks = (np.arange(8)[:, None] == g[None, :]).astype(np.float32).reshape(8, 1, 512)
    return jnp.asarray(avg), jnp.asarray(masks)


def _pack(items):
    flat = jnp.concatenate([a.reshape(-1).astype(F32) for a in items])
    n = flat.shape[0]
    rows = -(-n // D)
    rows = -(-rows // 8) * 8
    return jnp.pad(flat, (0, rows * D - n)).reshape(rows, D)


def _unpack(buf, shapes):
    lead = buf.shape[:-2]
    flat = buf.reshape(lead + (-1,))
    out, off = [], 0
    for shp in shapes:
        n = int(np.prod(shp))
        out.append(flat[..., off:off + n].reshape(lead + tuple(shp)))
        off += n
    return out


def _to_hs(g, rows_sharded):
    k, n = g.shape
    if rows_sharded:
        return g.reshape(4, 2, k // 8, n).transpose(1, 0, 2, 3)
    return g.reshape(2, k // 2, 4, n // 4).transpose(0, 2, 1, 3)


def _layer_fwd(i, x, mods, prm, consts):
    sv = {}
    sv["x0"] = x
    h = modnorm_fwd(f"norm1_fwd_{i}", x, mods, prm["norm1_g"], 0, 1)
    sv["h"] = h
    p = proj_in(f"proj_in_{i}", h, prm["w_in"])
    sv["p"] = p
    if i == 0:
        q, kv, m2 = even_tok_fwd(p, consts["cos_e"], consts["sin_e"], prm["gq"], prm["gk"], prm["gs"],
                                 prm["sgu_w"], prm["sgu_b"], consts["avg"], consts["masks"])
        o, lse = attn_fwd("attn_fwd_0", q, kv, GQA_HEADS, 64)
        sv.update(q=q, kv=kv)
    else:
        q, kv, y = odd_tok_fwd(p, consts["cos_o"], consts["sin_o"], prm["gq"], prm["gkv"], prm["wqn"], prm["wqr"],
                               prm["wkk"], prm["wkv"])
        o, lse = attn_fwd("attn_fwd_1", q, kv, MLA_HEADS, 96)
        z, m2 = conf_fwd(y, prm["conv_w"], prm["conv_b"], prm["ln_g"], prm["ln_b"])
        sv.update(q=q, kv=kv, y=y, z=z)
    sv.update(o=o, lse=lse, m2=m2)
    x1, y1 = proj_out(f"proj_out_{i}", o, m2, prm["w_out"], x, mods, 2)
    sv.update(x1=x1, y1=y1)
    h2 = modnorm_fwd(f"norm2_fwd_{i}", x1, mods, prm["norm2_g"], 3, 4)
    a, f = mlp_up(f"mlp_up_{i}", h2, prm["w1"])
    x2, y2 = mlp_down(f"mlp_down_{i}", f, prm["w2"], x1, mods, 5)
    sv.update(h2=h2, a=a, f=f, y2=y2)
    return x2, sv


def _layer_bwd(i, dx, sv, mods, prm, consts):
    gr = {}
    dy2, dg2 = gate_bwd(f"gate2_bwd_{i}", dx, sv["y2"], mods, 5)
    da = mlp_bwd_da(f"mlp_bwd_da_{i}", dy2, prm["w2"], sv["a"])
    tiles8 = [(h, j) for h in range(2) for j in range(4)]
    gr["w1"] = mm_tn(f"grad_w1_{i}", sv["h2"], da, tiles8, 512, D).reshape(2, 4, 512, D)
    gr["w2"] = mm_tn(f"grad_w2_{i}", sv["f"], dy2, [(2 * j + h, 0) for h in range(2) for j in range(4)],
                     512, D).reshape(2, 4, 512, D)
    dh2 = mlp_bwd_dh(f"mlp_bwd_dh_{i}", da, prm["w1"])
    dx1, dsh2, dsc2, gr["norm2_g"] = modnorm_bwd(f"norm2_bwd_{i}", sv["x1"], dh2, dx, mods, prm["norm2_g"], 3, 4)
    dy1, dg1 = gate_bwd(f"gate1_bwd_{i}", dx1, sv["y1"], mods, 2)
    dcat = mm_nt(f"proj_out_bwd_{i}", dy1, prm["w_out"])
    t4 = [(2 * j + h, 0) for h in range(2) for j in range(2)]
    go = mm_tn(f"grad_wout_a_{i}", sv["o"], dy1, t4, 128, D).reshape(2, 2, 128, D)
    gm = mm_tn(f"grad_wout_b_{i}", sv["m2"], dy1, t4, 128, D).reshape(2, 2, 128, D)
    gr["w_out"] = jnp.concatenate([go, gm], axis=1)
    if i == 0:
        dq, dkv = attn_bwd("attn_bwd_0", sv["q"], sv["kv"], sv["o"], dcat, sv["lse"], GQA_HEADS, 64)
        dp, gr["gq"], gr["gk"], gr["gs"], gr["sgu_w"], gr["sgu_b"] = even_tok_bwd(
            sv["p"], dq, dkv, dcat, consts["cos_e"], consts["sin_e"], prm["gq"], prm["gk"],
            prm["gs"], prm["sgu_w"], prm["sgu_b"], consts["avg"], consts["masks"])
    else:
        dq, dkv = attn_bwd("attn_bwd_1", sv["q"], sv["kv"], sv["o"], dcat, sv["lse"], MLA_HEADS, 96)
        dz, gr["ln_g"], gr["ln_b"], gr["conv_b"] = conf_bwd_ln(sv["z"], dcat, prm["ln_g"], prm["ln_b"])
        dyc, gr["conv_w"] = conf_bwd_conv(sv["y"], dz, prm["conv_w"])
        dp, gr["gq"], gr["gkv"], gr["wqn"], gr["wqr"], gr["wkk"], gr["wkv"] = odd_tok_bwd(
            sv["p"], dq, dkv, dyc, consts["cos_o"], consts["sin_o"], prm["gq"], prm["gkv"], prm["wqn"], prm["wqr"],
            prm["wkk"], prm["wkv"])
    n_in = prm["w_in"].shape[1]
    gr["w_in"] = mm_tn(f"grad_win_{i}", sv["h"], dp, [(0, 0), (1, 0)], 512, n_in)
    dh = mm_nt(f"proj_in_bwd_{i}", dp, prm["w_in"])
    dx0, dsh1, dsc1, gr["norm1_g"] = modnorm_bwd(f"norm1_bwd_{i}", sv["x0"], dh, dx1, mods, prm["norm1_g"], 0, 1)
    dmods = jnp.concatenate([dsh1, dsc1, dg1, dsh2, dsc2, dg2], axis=1)
    return dx0, dmods, gr


def local_step(xcat, target, mods, prms, final_g):
    avg, masks = _group_consts()
    cos_e, sin_e = _rope_tables(64, 8)
    cq, sq = _rope_tables(32, 8)
    ck, sk = _rope_tables(32, 1)
    padc = jnp.ones((SEQ, 96), F32)
    pads = jnp.zeros((SEQ, 96), F32)
    consts = dict(avg=avg, masks=masks, cos_e=cos_e, sin_e=sin_e,
                  cos_o=jnp.concatenate([cq, ck, padc], axis=1), sin_o=jnp.concatenate([sq, sk, pads], axis=1))
    x = xcat
    saved = []
    for i in range(2):
        x, sv = _layer_fwd(i, x, mods[i], prms[i], consts)
        saved.append(sv)
    dx, loss, dfg = final_loss(x, target, final_g)
    dmods, grads = [None, None], [None, None]
    for i in (1, 0):
        dx, dmods[i], grads[i] = _layer_bwd(i, dx, saved[i], mods[i], prms[i], consts)
    return loss, dx, dmods, grads, dfg


def _row(v):
    return v.reshape(1, -1).astype(F32)


def prep_params(ev_w_in, od_w_in, w_out, w1, w2, w_uq, w_ukv, small):
    od = jnp.concatenate([od_w_in[:, 0:416], jnp.zeros((D, 96), od_w_in.dtype), od_w_in[:, 416:OD_IN]], axis=1)
    uq = w_uq.reshape(256, 8, 96)
    ukv = w_ukv.reshape(128, 8, 128)
    p0 = dict(w_in=ev_w_in, w_out=w_out[0], w1=w1[0], w2=w2[0],
              norm1_g=_row(small["norm1_g"][0]), norm2_g=_row(small["norm2_g"][0]),
              gq=jnp.tile(_row(small["ev_q_norm_g"]), (1, 8)), gk=jnp.tile(_row(small["ev_k_norm_g"]), (1, 2)),
              gs=_row(small["ev_sgu_norm_g"]), sgu_w=small["ev_sgu_w"].reshape(8, 128, 128).astype(F32),
              sgu_b=small["ev_sgu_b"].reshape(8, 128, 1).astype(F32))
    p1 = dict(w_in=od, w_out=w_out[1], w1=w1[1], w2=w2[1],
              norm1_g=_row(small["norm1_g"][1]), norm2_g=_row(small["norm2_g"][1]),
              gq=_row(small["od_q_norm_g"]), gkv=_row(small["od_kv_norm_g"]),
              wqn=uq[:, :, :64].reshape(256, 512), wqr=uq[:, :, 64:].reshape(256, 256),
              wkk=ukv[:, :, :64].reshape(128, 512), wkv=ukv[:, :, 64:].reshape(128, 512),
              conv_w=jnp.pad(small["od_conv_w"].reshape(CONV_K, 512).astype(F32), ((0, 1), (0, 0))),
              conv_b=_row(small["od_conv_b"]), ln_g=_row(small["od_ln_g"]), ln_b=_row(small["od_ln_b"]))
    return [p0, p1]


def small_grads_natural(grads, dfg):
    g0, g1 = grads
    return dict(
        norm1_g=jnp.concatenate([g0["norm1_g"], g1["norm1_g"]], axis=0),
        norm2_g=jnp.concatenate([g0["norm2_g"], g1["norm2_g"]], axis=0),
        ev_q_norm_g=g0["gq"].reshape(8, 64).sum(0).reshape(1, 64),
        ev_k_norm_g=g0["gk"].reshape(2, 64).sum(0).reshape(1, 64),
        ev_sgu_norm_g=g0["gs"].reshape(1, 8, 64),
        ev_sgu_w=g0["sgu_w"].reshape(1, 8, 128, 128),
        ev_sgu_b=g0["sgu_b"].reshape(1, 8, 128),
        od_q_norm_g=g1["gq"].reshape(1, 256),
        od_kv_norm_g=g1["gkv"].reshape(1, 128),
        od_conv_w=g1["conv_w"][0:CONV_K].reshape(1, CONV_K, 512),
        od_conv_b=g1["conv_b"].reshape(1, 512),
        od_ln_g=g1["ln_g"].reshape(1, 512),
        od_ln_b=g1["ln_b"].reshape(1, 512),
        final_g=dfg.reshape(D))


def big_grads_hs(grads):
    g0, g1 = grads
    od = g1["w_in"].reshape(D, OD_PAD)
    od = jnp.concatenate([od[:, 0:416], od[:, 512:OD_PAD]], axis=1)
    uq = jnp.concatenate([g1["wqn"].reshape(256, 8, 64), g1["wqr"].reshape(256, 8, 32)], axis=2).reshape(256, 768)
    ukv = jnp.concatenate([g1["wkk"].reshape(128, 8, 64), g1["wkv"].reshape(128, 8, 64)], axis=2).reshape(128, 1024)

    def cols(g):
        k, n = g.shape
        return g.reshape(2, k // 2, 4, n // 4).transpose(0, 2, 1, 3).astype(BF)

    return dict(ev_w_in=cols(g0["w_in"].reshape(D, EV_IN)), od_w_in=cols(od),
                w_out=[g0["w_out"], g1["w_out"]], mlp_w1=[g0["w1"], g1["w1"]], mlp_w2=[g0["w2"], g1["w2"]],
                od_w_uq=cols(uq), od_w_ukv=cols(ukv))


def grads_to_natural(grads, dfg):
    out = small_grads_natural(grads, dfg)
    hs = big_grads_hs(grads)

    def from_cols(a):
        return a.transpose(0, 2, 1, 3).reshape(2 * a.shape[2], 4 * a.shape[3])

    def from_rows(a):
        return a.transpose(1, 0, 2, 3).reshape(8 * a.shape[2], a.shape[3])

    out["ev_w_in"] = from_cols(hs["ev_w_in"])[None]
    out["od_w_in"] = from_cols(hs["od_w_in"])[None]
    out["od_w_uq"] = from_cols(hs["od_w_uq"])[None]
    out["od_w_ukv"] = from_cols(hs["od_w_ukv"])[None]
    out["w_out"] = jnp.stack([from_rows(a) for a in hs["w_out"]])
    out["mlp_w1"] = jnp.stack([from_cols(a) for a in hs["mlp_w1"]])
    out["mlp_w2"] = jnp.stack([from_rows(a) for a in hs["mlp_w2"]])
    return out


WEIGHT_NAMES = ['c_ctx', 'ada_w', 'ada_b', 'norm1_g', 'norm2_g', 'w_out', 'mlp_w1', 'mlp_w2', 'ev_w_in',
                'ev_q_norm_g', 'ev_k_norm_g', 'ev_sgu_norm_g', 'ev_sgu_w', 'ev_sgu_b', 'od_w_in', 'od_q_norm_g',
                'od_kv_norm_g', 'od_w_uq', 'od_w_ukv', 'od_conv_w', 'od_conv_b', 'od_ln_g', 'od_ln_b', 'final_g']
REPL_SMALL = ['norm1_g', 'norm2_g', 'ev_q_norm_g', 'ev_k_norm_g', 'ev_sgu_norm_g', 'ev_sgu_w', 'ev_sgu_b',
              'od_kv_norm_g', 'final_g']
SHARD_SMALL = ['od_q_norm_g', 'od_conv_w', 'od_conv_b', 'od_ln_g', 'od_ln_b']
BIG = ['w_out', 'mlp_w1', 'mlp_w2', 'ev_w_in', 'od_w_in', 'od_w_uq', 'od_w_ukv']


def _gather_last(parts):
    return jnp.concatenate([parts[k] for k in range(4)], axis=-1)


def _reduce_big(hs_list, ci):
    mine = [lax.dynamic_index_in_dim(a, ci, 0, keepdims=False) for a in hs_list]
    other = [lax.dynamic_index_in_dim(a, 1 - ci, 0, keepdims=False) for a in hs_list]
    got = sibling_exchange("rs_sibling_in", other)
    pair = [add_pairs(f"rs_add_{k}", a, b) for k, (a, b) in enumerate(zip(mine, got))]
    land = chip_exchange("rs_chips", pair, scatter=True)
    half = [sum_lead(f"rs_sum_{k}", a) for k, a in enumerate(land)]
    sib = sibling_exchange("rs_sibling_out", half)
    return [jnp.where(ci == 0, jnp.concatenate([h, s], axis=0), jnp.concatenate([s, h], axis=0))
            for h, s in zip(half, sib)]


def kernel(x, c, ctx, c_ctx, ada_w, ada_b, norm1_g, norm2_g, w_out, mlp_w1, mlp_w2, ev_w_in, ev_q_norm_g, ev_k_norm_g, ev_sgu_norm_g, ev_sgu_w, ev_sgu_b, od_w_in, od_q_norm_g, od_kv_norm_g, od_w_uq, od_w_ukv, od_conv_w, od_conv_b, od_ln_g, od_ln_b, final_g, loss_target, m_c_ctx, m_ada_w, m_ada_b, m_norm1_g, m_norm2_g, m_w_out, m_mlp_w1, m_mlp_w2, m_ev_w_in, m_ev_q_norm_g, m_ev_k_norm_g, m_ev_sgu_norm_g, m_ev_sgu_w, m_ev_sgu_b, m_od_w_in, m_od_q_norm_g, m_od_kv_norm_g, m_od_w_uq, m_od_w_ukv, m_od_conv_w, m_od_conv_b, m_od_ln_g, m_od_ln_b, m_final_g, v_c_ctx, v_ada_w, v_ada_b, v_norm1_g, v_norm2_g, v_w_out, v_mlp_w1, v_mlp_w2, v_ev_w_in, v_ev_q_norm_g, v_ev_k_norm_g, v_ev_sgu_norm_g, v_ev_sgu_w, v_ev_sgu_b, v_od_w_in, v_od_q_norm_g, v_od_kv_norm_g, v_od_w_uq, v_od_w_ukv, v_od_conv_w, v_od_conv_b, v_od_ln_g, v_od_ln_b, v_final_g):
    w = dict(c_ctx=c_ctx, ada_w=ada_w, ada_b=ada_b, norm1_g=norm1_g, norm2_g=norm2_g, w_out=w_out, mlp_w1=mlp_w1,
             mlp_w2=mlp_w2, ev_w_in=ev_w_in, ev_q_norm_g=ev_q_norm_g, ev_k_norm_g=ev_k_norm_g,
             ev_sgu_norm_g=ev_sgu_norm_g, ev_sgu_w=ev_sgu_w, ev_sgu_b=ev_sgu_b, od_w_in=od_w_in,
             od_q_norm_g=od_q_norm_g, od_kv_norm_g=od_kv_norm_g, od_w_uq=od_w_uq, od_w_ukv=od_w_ukv,
             od_conv_w=od_conv_w, od_conv_b=od_conv_b, od_ln_g=od_ln_g, od_ln_b=od_ln_b, final_g=final_g)
    mom = dict(c_ctx=m_c_ctx, ada_w=m_ada_w, ada_b=m_ada_b, norm1_g=m_norm1_g, norm2_g=m_norm2_g, w_out=m_w_out,
               mlp_w1=m_mlp_w1, mlp_w2=m_mlp_w2, ev_w_in=m_ev_w_in, ev_q_norm_g=m_ev_q_norm_g,
               ev_k_norm_g=m_ev_k_norm_g, ev_sgu_norm_g=m_ev_sgu_norm_g, ev_sgu_w=m_ev_sgu_w, ev_sgu_b=m_ev_sgu_b,
               od_w_in=m_od_w_in, od_q_norm_g=m_od_q_norm_g, od_kv_norm_g=m_od_kv_norm_g, od_w_uq=m_od_w_uq,
               od_w_ukv=m_od_w_ukv, od_conv_w=m_od_conv_w, od_conv_b=m_od_conv_b, od_ln_g=m_od_ln_g,
               od_ln_b=m_od_ln_b, final_g=m_final_g)
    var = dict(c_ctx=v_c_ctx, ada_w=v_ada_w, ada_b=v_ada_b, norm1_g=v_norm1_g, norm2_g=v_norm2_g, w_out=v_w_out,
               mlp_w1=v_mlp_w1, mlp_w2=v_mlp_w2, ev_w_in=v_ev_w_in, ev_q_norm_g=v_ev_q_norm_g,
               ev_k_norm_g=v_ev_k_norm_g, ev_sgu_norm_g=v_ev_sgu_norm_g, ev_sgu_w=v_ev_sgu_w, ev_sgu_b=v_ev_sgu_b,
               od_w_in=v_od_w_in, od_q_norm_g=v_od_q_norm_g, od_kv_norm_g=v_od_kv_norm_g, od_w_uq=v_od_w_uq,
               od_w_ukv=v_od_w_ukv, od_conv_w=v_od_conv_w, od_conv_b=v_od_conv_b, od_ln_g=v_od_ln_g,
               od_ln_b=v_od_ln_b, final_g=v_final_g)
    xi, yi, ci = lax.axis_index("x"), lax.axis_index("y"), lax.axis_index("c")
    chip = 2 * xi + yi
    dev = 2 * chip + ci

    shard_shapes = [w[n].shape for n in SHARD_SMALL]
    g0 = all_gather8("ag_small", _pack([c] + [w[n] for n in SHARD_SMALL]))
    g0 = g0.reshape(8, -1, D)
    parts = _unpack(g0, [c.shape] + shard_shapes)
    c_all = parts[0].reshape(16, D)
    small_full = {n: _gather_last(p[0::2]) for n, p in zip(SHARD_SMALL, parts[1:])}
    call = jnp.concatenate([c_all, c_ctx.reshape(1, D), jnp.zeros((NC - 17, D), F32)], axis=0)

    cols = ada_w.shape[2]
    ada_b_sh = lax.dynamic_slice(ada_b, (0, chip * cols), (2, cols)).reshape(2, 1, cols)
    mt = mods_fwd(call, ada_w, ada_b_sh)
    mt = all_gather8("ag_mods", mt.reshape(2 * NC, cols)).reshape(8, 2, NC, cols)
    table = mt[0::2].transpose(1, 2, 0, 3).reshape(2, NC, 4 * cols)
    mods = []
    for i in range(2):
        lat = lax.dynamic_slice(table[i], (2 * dev, 0), (2, 4 * cols))
        mc = table[i, 16]
        mods.append(jnp.stack([mc, lat[0], mc, lat[1]]).reshape(4 * N_MOD, 1, D))

    names, srcs = [], []
    for n in BIG:
        for i in range(w[n].shape[0]):
            names.append((n, i))
            srcs.append(w[n][i].astype(BF))
    full = dict(zip(names, chip_exchange("gather_weights", srcs, scatter=False)))
    small = {n: w[n] for n in REPL_SMALL}
    small.update(small_full)
    prms = prep_params(
        ev_w_in=_gather_last(full[("ev_w_in", 0)]), od_w_in=_gather_last(full[("od_w_in", 0)]),
        w_out=[full[("w_out", i)].reshape(D, D) for i in range(2)],
        w1=[full[("mlp_w1", i)] for i in range(2)], w2=[full[("mlp_w2", i)] for i in range(2)],
        w_uq=_gather_last(full[("od_w_uq", 0)]), w_ukv=_gather_last(full[("od_w_ukv", 0)]), small=small)

    xcat = jnp.concatenate([ctx, x], axis=1).reshape(R, D)
    loss_p, dx, dmods, grads, dfg = local_step(xcat, loss_target.reshape(NEX * L, D), mods, prms,
                                               final_g.reshape(1, D))
    grad_x = dx.reshape(NEX, SEQ, D)[:, LC:]

    sg = small_grads_natural(grads, dfg)
    dm = jnp.stack([d.reshape(4, N_MOD * D) for d in dmods])
    small_names = REPL_SMALL + SHARD_SMALL
    items = [dm[:, 1::2], dm[:, 0] + dm[:, 2]] + [sg[n] for n in small_names] + [loss_p[0:1, 0:1]]
    shapes = [a.shape for a in items]
    g1 = all_gather8("ag_grads", _pack(items))
    rows1 = g1.shape[0] // 8
    g1 = g1.reshape(8, rows1, D)
    tot = _unpack(sum_lead("sum_small", g1), shapes)
    dm_lat = _unpack(g1, shapes[:1])[0]
    dm_lat = dm_lat.transpose(1, 0, 2, 3).reshape(2, 16, N_MOD * D)
    dm_all = jnp.concatenate([dm_lat, tot[1][:, None], jnp.zeros((2, NC - 17, N_MOD * D), F32)], axis=1)
    gsum = dict(zip(small_names, tot[2:2 + len(small_names)]))
    loss = tot[-1].reshape(())
    grad = {n: gsum[n].reshape(w[n].shape) for n in REPL_SMALL}
    for n in SHARD_SMALL:
        k = w[n].shape[-1]
        grad[n] = lax.dynamic_slice_in_dim(gsum[n], chip * k, k, axis=gsum[n].ndim - 1)
    grad["ada_b"] = sum_lead("sum_ada_b", dm_all.transpose(1, 0, 2).reshape(NC, 2 * N_MOD, D)).reshape(2, N_MOD * D)

    dm_sh = lax.dynamic_slice(dm_all, (0, 0, chip * cols), (2, NC, cols))
    grad["ada_w"], dcc = ada_bwd(call, ada_w, dm_sh)
    dcc = all_gather8("ag_cctx", dcc).reshape(8, 8, D)
    grad["c_ctx"] = sum_lead("sum_cctx", dcc[0::2])[0]

    hs = big_grads_hs(grads)
    hs_names, hs_list = [], []
    for n in BIG:
        for i, a in enumerate(hs[n] if isinstance(hs[n], list) else [hs[n]]):
            hs_names.append((n, i))
            hs_list.append(a)
    red = dict(zip(hs_names, _reduce_big(hs_list, ci)))
    for n in BIG:
        grad[n] = jnp.stack([red[(n, i)] for i in range(w[n].shape[0])]).reshape(w[n].shape)

    delta, new_m, new_v = {}, {}, {}
    for n in ['ada_w'] + BIG:
        shp = w[n].shape
        two_d = (shp[0] * shp[1], shp[2])
        d_, m_, v_ = adamw(f"adamw_{n}", w[n].reshape(two_d), grad[n].reshape(two_d), mom[n].reshape(two_d),
                           var[n].reshape(two_d))
        delta[n], new_m[n], new_v[n] = d_.reshape(shp), m_.reshape(shp), v_.reshape(shp)
    rest = [n for n in WEIGHT_NAMES if n not in ['ada_w'] + BIG]
    rshapes = [w[n].shape for n in rest]
    d_, m_, v_ = adamw("adamw_small", _pack([w[n] for n in rest]), _pack([grad[n] for n in rest]),
                       _pack([mom[n] for n in rest]), _pack([var[n] for n in rest]))
    for dst, buf in ((delta, d_), (new_m, m_), (new_v, v_)):
        dst.update(zip(rest, _unpack(buf, rshapes)))

    return (loss, grad_x, *[grad[n] for n in WEIGHT_NAMES], *[delta[n] for n in WEIGHT_NAMES],
            *[new_m[n] for n in WEIGHT_NAMES], *[new_v[n] for n in WEIGHT_NAMES])
```

```python
import functools
import math

import numpy as np
import jax
import jax.numpy as jnp
from jax import lax
from jax.experimental import pallas as pl
from jax.experimental.pallas import tpu as pltpu

F32 = jnp.float32
BF = jnp.bfloat16
HI = lax.Precision.HIGHEST
MESH = pl.DeviceIdType.MESH

D = 1024
L = 2048
LC = 256
SEQ = L + LC
NEX = 2
R = NEX * SEQ
TB = 256
BPE = SEQ // TB
NBLK = R // TB
GRID_W = 64
FF = 4 * D
EPS = 1e-6
ROPE_THETA = 10000.0
N_MOD = 6
EV_IN = 1792
OD_IN = 1440
OD_PAD = 1536
VMEM_LIMIT = 60 * 1024 * 1024

ADAM_LR = 0.001
ADAM_B1 = 0.9
ADAM_B2 = 0.999
ADAM_EPS = 1e-08
ADAM_WD = 0.01
ADAM_STEP = 10

NT = (((1,), (1,)), ((), ()))
TN = (((0,), (0,)), ((), ()))


def _cparams(sem=None):
    return pltpu.CompilerParams(dimension_semantics=sem, vmem_limit_bytes=VMEM_LIMIT)


@jax.custom_vjp
def _mm(a, b):
    return jnp.dot(a.astype(BF), b.astype(BF), preferred_element_type=F32)


def _mm_fwd(a, b):
    return _mm(a, b), (a, b)


def _mm_bwd(res, g):
    a, b = res
    gb = g.astype(BF)
    da = lax.dot_general(gb, b.astype(BF), NT, preferred_element_type=F32)
    db = lax.dot_general(a.astype(BF), gb, TN, preferred_element_type=F32)
    return da, db


_mm.defvjp(_mm_fwd, _mm_bwd)


@jax.custom_vjp
def _swap(x):
    n = x.shape[-1]
    ax = x.ndim - 1
    lane = lax.broadcasted_iota(jnp.int32, x.shape, ax)
    return jnp.where(lane % 2 == 0, pltpu.roll(x, n - 1, ax), pltpu.roll(x, 1, ax))


_swap.defvjp(lambda x: (_swap(x), None), lambda _, g: (_swap(g),))


def _rope(x, cos, sin):
    return x * cos + _swap(x) * sin


def _rmsn(x, g):
    return x * lax.rsqrt(jnp.mean(x * x, axis=-1, keepdims=True) + EPS) * g


def _grmsn(x, g, avg):
    ms = jnp.dot(x * x, avg, precision=HI, preferred_element_type=F32)
    return x * lax.rsqrt(ms + EPS) * g


def _modnorm(x, g, sh, sc):
    return _rmsn(x, g) * (1.0 + sc) + sh


def _gelu(x):
    return 0.5 * x * (1.0 + jnp.tanh(0.7978845608028654 * (x + 0.044715 * (x * x * x))))


def _silu(x):
    return x * jax.nn.sigmoid(x)


def _acc(ref, val, first):
    @pl.when(first)
    def _():
        ref[...] = val

    @pl.when(jnp.logical_not(first))
    def _():
        ref[...] += val


def _seg(i):
    return 2 * (i // BPE) + jnp.minimum(i % BPE, 1)


def _seg_first(i):
    return (i % BPE) <= 1


def _rb_call(name, body, row_in=(), mod_in=(), pos_in=(), full_in=(), shift_in=(),
             row_out=(), seg_out=(), acc_out=(), scratch=()):
    in_specs, args = [], []
    for a in row_in:
        in_specs.append(pl.BlockSpec((TB, a.shape[1]), lambda i: (i, 0)))
        args.append(a)
    for tab, m in mod_in:
        in_specs.append(pl.BlockSpec((1, 1, D), lambda i, m=m: (_seg(i) * N_MOD + m, 0, 0)))
        args.append(tab)
    for a in pos_in:
        in_specs.append(pl.BlockSpec((TB, a.shape[1]), lambda i: (i % BPE, 0)))
        args.append(a)
    for a in full_in:
        in_specs.append(pl.BlockSpec(a.shape, lambda i, n=a.ndim: (0,) * n))
        args.append(a)
    for a, d in shift_in:
        in_specs.append(pl.BlockSpec((TB, a.shape[1]), lambda i, d=d: (jnp.clip(i + d, 0, NBLK - 1), 0)))
        args.append(a)
    out_specs, out_shape = [], []
    for w, dt in row_out:
        out_specs.append(pl.BlockSpec((TB, w), lambda i: (i, 0)))
        out_shape.append(jax.ShapeDtypeStruct((R, w), dt))
    for w in seg_out:
        out_specs.append(pl.BlockSpec((1, 1, w), lambda i: (_seg(i), 0, 0)))
        out_shape.append(jax.ShapeDtypeStruct((4, 1, w), F32))
    for shp in acc_out:
        out_specs.append(pl.BlockSpec(shp, lambda i, n=len(shp): (0,) * n))
        out_shape.append(jax.ShapeDtypeStruct(shp, F32))

    def kern(*refs):
        body(pl.program_id(0), *refs)

    sem = ("arbitrary",) if (seg_out or acc_out) else ("parallel",)
    return pl.pallas_call(kern, grid=(NBLK,), in_specs=in_specs, out_specs=out_specs, out_shape=out_shape,
                          scratch_shapes=list(scratch), compiler_params=_cparams(sem), name=name)(*args)


def modnorm_fwd(name, x, mods, g, m_sh, m_sc):
    def body(i, x_ref, sh_ref, sc_ref, g_ref, h_ref):
        h_ref[...] = _modnorm(x_ref[...], g_ref[...], sh_ref[0], sc_ref[0]).astype(BF)

    return _rb_call(name, body, row_in=(x,), mod_in=((mods, m_sh), (mods, m_sc)), full_in=(g,),
                    row_out=((D, BF),))[0]


def modnorm_bwd(name, x, dh, dx_in, mods, g, m_sh, m_sc):
    def body(i, x_ref, dh_ref, dxin_ref, sh_ref, sc_ref, g_ref, dx_ref, dsh_ref, dsc_ref, dg_ref):
        _, vjp = jax.vjp(_modnorm, x_ref[...], g_ref[...], sh_ref[0], sc_ref[0])
        dx, dg, dsh, dsc = vjp(dh_ref[...].astype(F32))
        dx_ref[...] = dxin_ref[...] + dx
        _acc(dsh_ref, dsh[None], _seg_first(i))
        _acc(dsc_ref, dsc[None], _seg_first(i))
        _acc(dg_ref, dg, i == 0)

    return _rb_call(name, body, row_in=(x, dh, dx_in), mod_in=((mods, m_sh), (mods, m_sc)), full_in=(g,),
                    row_out=((D, F32),), seg_out=(D, D), acc_out=((1, D),))


def gate_bwd(name, dx, y, mods, m_gate):
    def body(i, dx_ref, y_ref, gt_ref, dy_ref, dgt_ref):
        dxv = dx_ref[...]
        dy_ref[...] = (dxv * gt_ref[0]).astype(BF)
        _acc(dgt_ref, jnp.sum(dxv * y_ref[...].astype(F32), axis=0, keepdims=True)[None], _seg_first(i))

    return _rb_call(name, body, row_in=(dx, y), mod_in=((mods, m_gate),), row_out=((D, BF),), seg_out=(D,))


def proj_in(name, h, w):
    n = w.shape[1]

    def body(i, h_ref, w_ref, o_ref):
        o_ref[...] = jnp.dot(h_ref[...], w_ref[...], preferred_element_type=F32).astype(BF)

    return _rb_call(name, body, row_in=(h,), full_in=(w,), row_out=((n, BF),))[0]


def proj_out(name, a1, a2, w, x, mods, m_gate):
    k1 = a1.shape[1]

    def body(i, a1_ref, a2_ref, x_ref, gt_ref, w_ref, xo_ref, y_ref):
        y = jnp.dot(a1_ref[...], w_ref[:k1, :], preferred_element_type=F32)
        y = y + jnp.dot(a2_ref[...], w_ref[k1:, :], preferred_element_type=F32)
        y_ref[...] = y.astype(BF)
        xo_ref[...] = x_ref[...] + gt_ref[0] * y

    return _rb_call(name, body, row_in=(a1, a2, x), mod_in=((mods, m_gate),), full_in=(w,),
                    row_out=((D, F32), (D, BF)))


def mlp_up(name, h, w1):
    def body(i, h_ref, w_ref, a_ref, f_ref):
        hv = h_ref[...]
        for n in range(4):
            a = jnp.dot(hv, w_ref[n], preferred_element_type=F32)
            a_ref[:, n * D:(n + 1) * D] = a.astype(BF)
            r = jnp.maximum(a, 0.0)
            f_ref[:, n * D:(n + 1) * D] = (r * r).astype(BF)

    return _rb_call(name, body, row_in=(h,), full_in=(w1,), row_out=((FF, BF), (FF, BF)))


def mlp_down(name, f, w2, x, mods, m_gate):
    def body(i, f_ref, x_ref, gt_ref, w_ref, xo_ref, y_ref):
        y = jnp.dot(f_ref[:, 0:D], w_ref[0], preferred_element_type=F32)
        for n in range(1, 4):
            y = y + jnp.dot(f_ref[:, n * D:(n + 1) * D], w_ref[n], preferred_element_type=F32)
        y_ref[...] = y.astype(BF)
        xo_ref[...] = x_ref[...] + gt_ref[0] * y

    return _rb_call(name, body, row_in=(f, x), mod_in=((mods, m_gate),), full_in=(w2,),
                    row_out=((D, F32), (D, BF)))


def mm_nt(name, g, w):
    k = w.shape[0]

    def body(i, g_ref, w_ref, o_ref):
        o_ref[...] = lax.dot_general(g_ref[...], w_ref[...], NT, preferred_element_type=F32).astype(BF)

    return _rb_call(name, body, row_in=(g,), full_in=(w,), row_out=((k, BF),))[0]


def mlp_bwd_da(name, dy, w2, a):
    def body(i, dy_ref, a_ref, w_ref, da_ref):
        dyv = dy_ref[...]
        for n in range(4):
            df = lax.dot_general(dyv, w_ref[n], NT, preferred_element_type=F32)
            av = a_ref[:, n * D:(n + 1) * D].astype(F32)
            da_ref[:, n * D:(n + 1) * D] = (df * (2.0 * jnp.maximum(av, 0.0))).astype(BF)

    return _rb_call(name, body, row_in=(dy, a), full_in=(w2,), row_out=((FF, BF),))[0]


def mlp_bwd_dh(name, da, w1):
    def body(i, da_ref, w_ref, dh_ref):
        acc = lax.dot_general(da_ref[:, 0:D], w_ref[0], NT, preferred_element_type=F32)
        for n in range(1, 4):
            acc = acc + lax.dot_general(da_ref[:, n * D:(n + 1) * D], w_ref[n], NT, preferred_element_type=F32)
        dh_ref[...] = acc.astype(BF)

    return _rb_call(name, body, row_in=(da,), full_in=(w1,), row_out=((D, BF),))[0]


TN_ROWS = 512


def mm_tn(name, a, g, tiles, th, tw):
    nt = len(tiles)
    acs = jnp.asarray([t[0] for t in tiles], jnp.int32)
    gcs = jnp.asarray([t[1] for t in tiles], jnp.int32)
    nr = R // TN_ROWS

    def kern(ac_ref, gc_ref, a_ref, g_ref, o_ref, acc_ref):
        r = pl.program_id(1)

        @pl.when(r == 0)
        def _():
            acc_ref[...] = jnp.zeros_like(acc_ref)

        acc_ref[...] += lax.dot_general(a_ref[...], g_ref[...], TN, preferred_element_type=F32)

        @pl.when(r == nr - 1)
        def _():
            o_ref[...] = acc_ref[...].astype(BF)

    grid_spec = pltpu.PrefetchScalarGridSpec(
        num_scalar_prefetch=2, grid=(nt, nr),
        in_specs=[pl.BlockSpec((TN_ROWS, th), lambda t, r, ac, gc: (r, ac[t])),
                  pl.BlockSpec((TN_ROWS, tw), lambda t, r, ac, gc: (r, gc[t]))],
        out_specs=pl.BlockSpec((None, th, tw), lambda t, r, ac, gc: (t, 0, 0)),
        scratch_shapes=[pltpu.VMEM((th, tw), F32)])
    return pl.pallas_call(kern, grid_spec=grid_spec, out_shape=jax.ShapeDtypeStruct((nt, th, tw), BF),
                          compiler_params=_cparams(("parallel", "arbitrary")), name=name)(acs, gcs, a, g)


def _even_tok(q, k, zu, zv, gq, gk, gs, ws, bs, cq, sq, ck, sk, avg, masks):
    qr = _rope(_grmsn(q, gq, avg), cq, sq)
    kr = _rope(_grmsn(k, gk, avg[:128, :128]), ck, sk)
    u = _gelu(zu)
    v = _grmsn(_gelu(zv), gs, avg)
    sv = None
    for g in range(8):
        t = masks[g] * (_mm(ws[g], v) + bs[g])
        sv = t if sv is None else sv + t
    return qr, kr, u * sv


def even_tok_fwd(p, cos, sin, gq, gk, gs, sgu_w, sgu_b, avg, masks):
    def body(i, p_ref, cos_ref, sin_ref, gq_ref, gk_ref, gs_ref, w_ref, b_ref, avg_ref, mk_ref, q_ref, kv_ref, m_ref):
        avgv = avg_ref[...]
        ws = [w_ref[g] for g in range(8)]
        bs = [b_ref[g] for g in range(8)]
        mks = [mk_ref[g] for g in range(8)]
        for c in range(2):
            rs = pl.ds(c * 128, 128)
            qr, kr, m = _even_tok(
                p_ref[rs, 0:512].astype(F32), p_ref[rs, 512:640].astype(F32),
                p_ref[rs, 768:1280].astype(F32), p_ref[rs, 1280:1792].astype(F32),
                gq_ref[...], gk_ref[...], gs_ref[...], ws, bs,
                cos_ref[rs, :], sin_ref[rs, :], cos_ref[rs, 0:128], sin_ref[rs, 0:128], avgv, mks)
            q_ref[rs, :] = qr.astype(BF)
            kv_ref[rs, 0:128] = kr.astype(BF)
            kv_ref[rs, 128:256] = p_ref[rs, 640:768]
            m_ref[rs, :] = m.astype(BF)

    return _rb_call("even_tok_fwd", body, row_in=(p,), pos_in=(cos, sin),
                    full_in=(gq, gk, gs, sgu_w, sgu_b, avg, masks), row_out=((512, BF), (256, BF), (512, BF)))


def even_tok_bwd(p, dq, dkv, dcat, cos, sin, gq, gk, gs, sgu_w, sgu_b, avg, masks):
    def body(i, p_ref, dq_ref, dkv_ref, dcat_ref, cos_ref, sin_ref, gq_ref, gk_ref, gs_ref, w_ref, b_ref,
             avg_ref, mk_ref, dp_ref, dgq_ref, dgk_ref, dgs_ref, dw_ref, db_ref):
        avgv = avg_ref[...]
        ws = [w_ref[g] for g in range(8)]
        bs = [b_ref[g] for g in range(8)]
        mks = [mk_ref[g] for g in range(8)]
        tot = None
        for c in range(2):
            rs = pl.ds(c * 128, 128)
            cq, sq, ck, sk = cos_ref[rs, :], sin_ref[rs, :], cos_ref[rs, 0:128], sin_ref[rs, 0:128]

            def f(q, k, zu, zv, gq, gk, gs, ws, bs):
                return _even_tok(q, k, zu, zv, gq, gk, gs, ws, bs, cq, sq, ck, sk, avgv, mks)

            _, vjp = jax.vjp(f, p_ref[rs, 0:512].astype(F32), p_ref[rs, 512:640].astype(F32),
                             p_ref[rs, 768:1280].astype(F32), p_ref[rs, 1280:1792].astype(F32),
                             gq_ref[...], gk_ref[...], gs_ref[...], ws, bs)
            d = vjp((dq_ref[rs, :].astype(F32), dkv_ref[rs, 0:128], dcat_ref[rs, 512:1024].astype(F32)))
            dp_ref[rs, 0:512] = d[0].astype(BF)
            dp_ref[rs, 512:640] = d[1].astype(BF)
            dp_ref[rs, 640:768] = dkv_ref[rs, 128:256].astype(BF)
            dp_ref[rs, 768:1280] = d[2].astype(BF)
            dp_ref[rs, 1280:1792] = d[3].astype(BF)
            part = [d[4], d[5], d[6]] + list(d[7]) + list(d[8])
            tot = part if tot is None else [a + b for a, b in zip(tot, part)]
        refs = [dgq_ref, dgk_ref, dgs_ref] + [dw_ref.at[g] for g in range(8)] + [db_ref.at[g] for g in range(8)]
        for ref, val in zip(refs, tot):
            _acc(ref, val, i == 0)

    return _rb_call("even_tok_bwd", body, row_in=(p, dq, dkv, dcat), pos_in=(cos, sin),
                    full_in=(gq, gk, gs, sgu_w, sgu_b, avg, masks), row_out=((EV_IN, BF),),
                    acc_out=((1, 512), (1, 128), (1, 512), (8, 128, 128), (8, 128, 1)))


def _odd_tok(cq, ckv, kr, za, zg, gq, gkv, wqn, wqr, wkk, wkv, cr, sr, ck, sk):
    cqn = _rmsn(cq, gq)
    qn = _mm(cqn, wqn)
    qr = _rope(_mm(cqn, wqr), cr, sr)
    ckn = _rmsn(ckv, gkv)
    kn = _mm(ckn, wkk)
    v = _mm(ckn, wkv)
    krr = _rope(kr, ck, sk)
    y = za * jax.nn.sigmoid(zg)
    return qn, qr, kn, v, krr, y


def odd_tok_fwd(p, cos, sin, gq, gkv, wqn, wqr, wkk, wkv):
    def body(i, p_ref, cos_ref, sin_ref, gq_ref, gkv_ref, wqn_ref, wqr_ref, wkk_ref, wkv_ref, q_ref, kv_ref, y_ref):
        qn, qr, kn, v, krr, y = _odd_tok(
            p_ref[:, 0:256].astype(F32), p_ref[:, 256:384].astype(F32), p_ref[:, 384:512].astype(F32),
            p_ref[:, 512:1024].astype(F32), p_ref[:, 1024:1536].astype(F32),
            gq_ref[...], gkv_ref[...], wqn_ref[...], wqr_ref[...], wkk_ref[...], wkv_ref[...],
            cos_ref[:, 0:256], sin_ref[:, 0:256], cos_ref[:, 256:384], sin_ref[:, 256:384])
        q_ref[:, 0:512] = qn.astype(BF)
        q_ref[:, 512:768] = qr.astype(BF)
        kv_ref[:, 0:512] = kn.astype(BF)
        kv_ref[:, 512:1024] = v.astype(BF)
        kv_ref[:, 1024:1152] = krr.astype(BF)
        y_ref[...] = y.astype(BF)

    return _rb_call("odd_tok_fwd", body, row_in=(p,), pos_in=(cos, sin), full_in=(gq, gkv, wqn, wqr, wkk, wkv),
                    row_out=((768, BF), (1152, BF), (512, BF)))


def odd_tok_bwd(p, dq, dkv, dy, cos, sin, gq, gkv, wqn, wqr, wkk, wkv):
    def body(i, p_ref, dq_ref, dkv_ref, dy_ref, cos_ref, sin_ref, gq_ref, gkv_ref, wqn_ref, wqr_ref, wkk_ref,
             wkv_ref, dp_ref, dgq_ref, dgkv_ref, dwqn_ref, dwqr_ref, dwkk_ref, dwkv_ref):
        cr, sr, ck, sk = cos_ref[:, 0:256], sin_ref[:, 0:256], cos_ref[:, 256:384], sin_ref[:, 256:384]

        def f(cq, ckv, kr, za, zg, gq, gkv, wqn, wqr, wkk, wkv):
            return _odd_tok(cq, ckv, kr, za, zg, gq, gkv, wqn, wqr, wkk, wkv, cr, sr, ck, sk)

        _, vjp = jax.vjp(f, p_ref[:, 0:256].astype(F32), p_ref[:, 256:384].astype(F32),
                         p_ref[:, 384:512].astype(F32), p_ref[:, 512:1024].astype(F32),
                         p_ref[:, 1024:1536].astype(F32), gq_ref[...], gkv_ref[...], wqn_ref[...],
                         wqr_ref[...], wkk_ref[...], wkv_ref[...])
        d = vjp((dq_ref[:, 0:512].astype(F32), dq_ref[:, 512:768].astype(F32), dkv_ref[:, 0:512],
                 dkv_ref[:, 512:1024], dkv_ref[:, 1024:1152], dy_ref[...].astype(F32)))
        dp_ref[:, 0:256] = d[0].astype(BF)
        dp_ref[:, 256:384] = d[1].astype(BF)
        dp_ref[:, 384:512] = d[2].astype(BF)
        dp_ref[:, 512:1024] = d[3].astype(BF)
        dp_ref[:, 1024:1536] = d[4].astype(BF)
        for ref, val in zip((dgq_ref, dgkv_ref, dwqn_ref, dwqr_ref, dwkk_ref, dwkv_ref), d[5:]):
            _acc(ref, val, i == 0)

    return _rb_call("odd_tok_bwd", body, row_in=(p, dq, dkv, dy), pos_in=(cos, sin),
                    full_in=(gq, gkv, wqn, wqr, wkk, wkv), row_out=((OD_PAD, BF),),
                    acc_out=((1, 256), (1, 128), (256, 512), (256, 256), (128, 512), (128, 512)))


GQA_HEADS = [([(64 * h, 64 * (h // 4), 64)], 128 + 64 * (h // 4)) for h in range(8)]
MLA_HEADS = [([(64 * h, 64 * h, 64), (512 + 32 * h, 1024, 32)], 512 + 64 * h) for h in range(8)]


def _scores(q_ref, kv_ref, parts, scale, valid):
    s = None
    for qo, ko, w in parts:
        t = lax.dot_general(q_ref[:, qo:qo + w], kv_ref[:, ko:ko + w], NT, preferred_element_type=F32)
        s = t if s is None else s + t
    return jnp.where(valid, s * scale, -1e30)


def _key_mask(j):
    kpos = lax.broadcasted_iota(jnp.int32, (TB, SEQ), 1)
    return kpos < jnp.where(j == 0, LC, SEQ)


def attn_fwd(name, q, kv, heads, dk):
    scale = dk ** -0.5
    qw, kvw = q.shape[1], kv.shape[1]

    def kern(q_ref, kv_ref, o_ref, lse_ref):
        valid = _key_mask(pl.program_id(1))
        for h, (parts, vo) in enumerate(heads):
            s = _scores(q_ref, kv_ref, parts, scale, valid)
            m = jnp.max(s, axis=-1, keepdims=True)
            p = jnp.exp(s - m)
            l = jnp.sum(p, axis=-1, keepdims=True)
            o = jnp.dot(p.astype(BF), kv_ref[:, vo:vo + 64], preferred_element_type=F32) / l
            o_ref[:, 64 * h:64 * h + 64] = o.astype(BF)
            lse_ref[:, h:h + 1] = m + jnp.log(l)

    return pl.pallas_call(
        kern, grid=(NEX, BPE),
        in_specs=[pl.BlockSpec((TB, qw), lambda e, j: (e * BPE + j, 0)),
                  pl.BlockSpec((SEQ, kvw), lambda e, j: (e, 0))],
        out_specs=[pl.BlockSpec((TB, 512), lambda e, j: (e * BPE + j, 0)),
                   pl.BlockSpec((TB, 8), lambda e, j: (e * BPE + j, 0))],
        out_shape=[jax.ShapeDtypeStruct((R, 512), BF), jax.ShapeDtypeStruct((R, 8), F32)],
        compiler_params=_cparams(("parallel", "arbitrary")), name=name)(q, kv)


def attn_bwd(name, q, kv, o, dcat, lse, heads, dk):
    scale = dk ** -0.5
    qw, kvw = q.shape[1], kv.shape[1]

    def kern(q_ref, kv_ref, o_ref, do_ref, lse_ref, dq_ref, dkv_ref):
        j = pl.program_id(1)
        valid = _key_mask(j)

        @pl.when(j == 0)
        def _():
            dkv_ref[...] = jnp.zeros_like(dkv_ref)

        for h, (parts, vo) in enumerate(heads):
            s = _scores(q_ref, kv_ref, parts, scale, valid)
            p = jnp.exp(s - lse_ref[:, h:h + 1])
            do = do_ref[:, 64 * h:64 * h + 64]
            dsum = jnp.sum(do.astype(F32) * o_ref[:, 64 * h:64 * h + 64].astype(F32), axis=-1, keepdims=True)
            dp = lax.dot_general(do, kv_ref[:, vo:vo + 64], NT, preferred_element_type=F32)
            ds = (p * (dp - dsum) * scale).astype(BF)
            dkv_ref[:, vo:vo + 64] += lax.dot_general(p.astype(BF), do, TN, preferred_element_type=F32)
            for qo, ko, w in parts:
                dq_ref[:, qo:qo + w] = jnp.dot(ds, kv_ref[:, ko:ko + w], preferred_element_type=F32).astype(BF)
                dkv_ref[:, ko:ko + w] += lax.dot_general(ds, q_ref[:, qo:qo + w], TN, preferred_element_type=F32)

    return pl.pallas_call(
        kern, grid=(NEX, BPE),
        in_specs=[pl.BlockSpec((TB, qw), lambda e, j: (e * BPE + j, 0)),
                  pl.BlockSpec((SEQ, kvw), lambda e, j: (e, 0)),
                  pl.BlockSpec((TB, 512), lambda e, j: (e * BPE + j, 0)),
                  pl.BlockSpec((TB, 512), lambda e, j: (e * BPE + j, 0)),
                  pl.BlockSpec((TB, 8), lambda e, j: (e * BPE + j, 0))],
        out_specs=[pl.BlockSpec((TB, qw), lambda e, j: (e * BPE + j, 0)),
                   pl.BlockSpec((SEQ, kvw), lambda e, j: (e, 0))],
        out_shape=[jax.ShapeDtypeStruct((R, qw), BF), jax.ShapeDtypeStruct((R, kvw), F32)],
        compiler_params=_cparams(("parallel", "arbitrary")), name=name)(q, kv, o, dcat, lse)


HALO = 16
CONV_K = 31


def _fill_ext(ext_ref, prev_ref, cur_ref, next_ref, i):
    j = i % BPE
    has_prev = (j >= 2).astype(F32)
    has_next = jnp.logical_and(j >= 1, j <= BPE - 2).astype(F32)
    ext_ref[0:HALO, :] = prev_ref[TB - HALO:TB, :].astype(F32) * has_prev
    ext_ref[HALO:HALO + TB, :] = cur_ref[...].astype(F32)
    ext_ref[HALO + TB:2 * HALO + TB, :] = next_ref[0:HALO, :].astype(F32) * has_next


def _ln_silu(z, g, b):
    mu = jnp.mean(z, axis=-1, keepdims=True)
    zc = z - mu
    var = jnp.mean(zc * zc, axis=-1, keepdims=True)
    return _silu(zc * lax.rsqrt(var + EPS) * g + b)


def conf_fwd(y, cw, cb, lg, lb):
    def body(i, cur_ref, cw_ref, cb_ref, lg_ref, lb_ref, prev_ref, next_ref, z_ref, c_ref, ext_ref):
        _fill_ext(ext_ref, prev_ref, cur_ref, next_ref, i)
        acc = ext_ref[1:1 + TB, :] * cw_ref[0:1, :]
        for k in range(1, CONV_K):
            acc = acc + ext_ref[k + 1:k + 1 + TB, :] * cw_ref[k:k + 1, :]
        z = acc + cb_ref[...]
        z_ref[...] = z.astype(BF)
        c_ref[...] = _ln_silu(z, lg_ref[...], lb_ref[...]).astype(BF)

    return _rb_call("conf_fwd", body, row_in=(y,), full_in=(cw, cb, lg, lb), shift_in=((y, -1), (y, 1)),
                    row_out=((512, BF), (512, BF)), scratch=(pltpu.VMEM((TB + 2 * HALO, 512), F32),))


def conf_bwd_ln(z, dcat, lg, lb):
    def body(i, z_ref, dcat_ref, lg_ref, lb_ref, dz_ref, dlg_ref, dlb_ref, dcb_ref):
        _, vjp = jax.vjp(_ln_silu, z_ref[...].astype(F32), lg_ref[...], lb_ref[...])
        dz, dlg, dlb = vjp(dcat_ref[:, 512:1024].astype(F32))
        dz_ref[...] = dz.astype(BF)
        _acc(dlg_ref, dlg, i == 0)
        _acc(dlb_ref, dlb, i == 0)
        _acc(dcb_ref, jnp.sum(dz, axis=0, keepdims=True), i == 0)

    return _rb_call("conf_bwd_ln", body, row_in=(z, dcat), full_in=(lg, lb), row_out=((512, BF),),
                    acc_out=((1, 512), (1, 512), (1, 512)))


def conf_bwd_conv(y, dz, cw):
    def body(i, y_ref, dz_ref, cw_ref, yp_ref, yn_ref, dzp_ref, dzn_ref, dy_ref, dcw_ref, exty_ref, extd_ref):
        _fill_ext(exty_ref, yp_ref, y_ref, yn_ref, i)
        _fill_ext(extd_ref, dzp_ref, dz_ref, dzn_ref, i)
        dzv = dz_ref[...].astype(F32)
        @pl.when(i == 0)
        def _():
            dcw_ref[...] = jnp.zeros_like(dcw_ref)

        acc = None
        for k in range(CONV_K):
            t = extd_ref[CONV_K - k:CONV_K - k + TB, :] * cw_ref[k:k + 1, :]
            acc = t if acc is None else acc + t
            dcw_ref[k:k + 1, :] += jnp.sum(dzv * exty_ref[k + 1:k + 1 + TB, :], axis=0, keepdims=True)
        dy_ref[...] = acc.astype(BF)

    return _rb_call("conf_bwd_conv", body, row_in=(y, dz), full_in=(cw,),
                    shift_in=((y, -1), (y, 1), (dz, -1), (dz, 1)), row_out=((512, BF),), acc_out=((32, 512),),
                    scratch=(pltpu.VMEM((TB + 2 * HALO, 512), F32), pltpu.VMEM((TB + 2 * HALO, 512), F32)))


def final_loss(x, target, fg):
    lpb = L // TB

    def kern(x_ref, t_ref, g_ref, dx_ref, loss_ref, dg_ref):
        i = pl.program_id(0)
        lat = (i % BPE) >= 1
        xv, tv = x_ref[...], t_ref[...]

        def f(x, g):
            err = _rmsn(x, g) - tv
            rowsum = jnp.sum(err * err, axis=-1, keepdims=True)
            return jnp.sum(rowsum, axis=0, keepdims=True) * (0.5 / D)

        lv, vjp = jax.vjp(f, xv, g_ref[...])
        dx, dg = vjp(jnp.ones((1, 1), F32))
        m = lat.astype(F32)
        dx_ref[...] = dx * m
        _acc(loss_ref, jnp.zeros((8, 128), F32) + lv * m, i == 0)
        _acc(dg_ref, dg * m, i == 0)

    return pl.pallas_call(
        kern, grid=(NBLK,),
        in_specs=[pl.BlockSpec((TB, D), lambda i: (i, 0)),
                  pl.BlockSpec((TB, D), lambda i: ((i // BPE) * lpb + jnp.maximum(i % BPE - 1, 0), 0)),
                  pl.BlockSpec((1, D), lambda i: (0, 0))],
        out_specs=[pl.BlockSpec((TB, D), lambda i: (i, 0)), pl.BlockSpec((8, 128), lambda i: (0, 0)),
                   pl.BlockSpec((1, D), lambda i: (0, 0))],
        out_shape=[jax.ShapeDtypeStruct((R, D), F32), jax.ShapeDtypeStruct((8, 128), F32),
                   jax.ShapeDtypeStruct((1, D), F32)],
        compiler_params=_cparams(("arbitrary",)), name="final_loss")(x, target, fg)


NC = 24


def mods_fwd(call, ada_w, ada_b):
    cols = ada_w.shape[2]

    def kern(c_ref, w_ref, b_ref, o_ref):
        o_ref[...] = jnp.dot(_silu(c_ref[...]), w_ref[...], precision=HI, preferred_element_type=F32) + b_ref[...]

    return pl.pallas_call(
        kern, grid=(2,),
        in_specs=[pl.BlockSpec((NC, D), lambda l: (0, 0)), pl.BlockSpec((None, D, cols), lambda l: (l, 0, 0)),
                  pl.BlockSpec((None, 1, cols), lambda l: (l, 0, 0))],
        out_specs=pl.BlockSpec((None, NC, cols), lambda l: (l, 0, 0)),
        out_shape=jax.ShapeDtypeStruct((2, NC, cols), F32),
        compiler_params=_cparams(("parallel",)), name="mods_fwd")(call, ada_w, ada_b)


def ada_bwd(call, ada_w, dm):
    cols = ada_w.shape[2]

    def kern(c_ref, w_ref, dm_ref, gw_ref, dc_ref):
        l = pl.program_id(0)
        gw_ref[...] = lax.dot_general(_silu(c_ref[...]), dm_ref[...], TN, precision=HI, preferred_element_type=F32)
        part = lax.dot_general(dm_ref[16:24, :], w_ref[...], NT, precision=HI, preferred_element_type=F32)
        cc = c_ref[16:17, :]
        sg = jax.nn.sigmoid(cc)
        _acc(dc_ref, part * (sg * (1.0 + cc * (1.0 - sg))), l == 0)

    return pl.pallas_call(
        kern, grid=(2,),
        in_specs=[pl.BlockSpec((NC, D), lambda l: (0, 0)), pl.BlockSpec((None, D, cols), lambda l: (l, 0, 0)),
                  pl.BlockSpec((None, NC, cols), lambda l: (l, 0, 0))],
        out_specs=[pl.BlockSpec((None, D, cols), lambda l: (l, 0, 0)), pl.BlockSpec((8, D), lambda l: (0, 0))],
        out_shape=[jax.ShapeDtypeStruct((2, D, cols), F32), jax.ShapeDtypeStruct((8, D), F32)],
        compiler_params=_cparams(("arbitrary",)), name="ada_bwd")(call, ada_w, dm)


def sum_lead(name, a, scale_last=None):
    n, r, c = a.shape
    tr = r
    for cand in (512, 256, 128, 64, 32, 16, 8):
        if r % cand == 0 and cand * c * 4 * n <= 8 * 1024 * 1024:
            tr = cand
            break

    def kern(a_ref, o_ref):
        acc = a_ref[0].astype(F32)
        for k in range(1, n):
            acc = acc + a_ref[k].astype(F32)
        o_ref[...] = acc

    return pl.pallas_call(kern, grid=(r // tr,), in_specs=[pl.BlockSpec((n, tr, c), lambda i: (0, i, 0))],
                          out_specs=pl.BlockSpec((tr, c), lambda i: (i, 0)),
                          out_shape=jax.ShapeDtypeStruct((r, c), F32),
                          compiler_params=_cparams(("parallel",)), name=name)(a)


def add_pairs(name, a, b):
    _, r, c = a.shape

    def kern(a_ref, b_ref, o_ref):
        o_ref[...] = (a_ref[...].astype(F32) + b_ref[...].astype(F32)).astype(BF)

    spec = pl.BlockSpec((None, r, c), lambda i: (i, 0, 0))
    return pl.pallas_call(kern, grid=(4,), in_specs=[spec, spec], out_specs=spec,
                          out_shape=jax.ShapeDtypeStruct(a.shape, BF),
                          compiler_params=_cparams(("parallel",)), name=name)(a, b)


def adamw(name, w, g, m, v):
    r, c = w.shape
    tr = r
    for cand in (512, 256, 128, 64, 32, 16, 8):
        if r % cand == 0 and cand * c * 4 <= 2 * 1024 * 1024:
            tr = cand
            break
    c1 = 1.0 / (1.0 - ADAM_B1 ** ADAM_STEP)
    c2 = 1.0 / (1.0 - ADAM_B2 ** ADAM_STEP)

    def kern(w_ref, g_ref, m_ref, v_ref, d_ref, mo_ref, vo_ref):
        gv = g_ref[...]
        mn = ADAM_B1 * m_ref[...] + (1.0 - ADAM_B1) * gv
        vn = ADAM_B2 * v_ref[...] + (1.0 - ADAM_B2) * (gv * gv)
        d_ref[...] = -ADAM_LR * ((mn * c1) / (jnp.sqrt(vn * c2) + ADAM_EPS) + ADAM_WD * w_ref[...])
        mo_ref[...] = mn
        vo_ref[...] = vn

    spec = pl.BlockSpec((tr, c), lambda i: (i, 0))
    shp = jax.ShapeDtypeStruct((r, c), F32)
    return pl.pallas_call(kern, grid=(r // tr,), in_specs=[spec] * 4, out_specs=[spec] * 3, out_shape=[shp] * 3,
                          compiler_params=_cparams(("parallel",)), name=name)(w, g, m, v)


def all_gather8(name, xs):
    m_per, n = xs.shape

    def body(x_ref, out_ref, send_sems, recv_sems, local_sem):
        x, y, c = lax.axis_index("x"), lax.axis_index("y"), lax.axis_index("c")
        me, sibling = (x, y, c), (x, y, 1 - c)
        chips = [(1 - x, y), (x, 1 - y), (1 - x, 1 - y)]

        def rows(px, py, pc):
            return out_ref.at[pl.ds((4 * px + 2 * py + pc) * m_per, m_per), :]

        def copy(k, block, to, src=None):
            return pltpu.make_async_remote_copy(
                src_ref=rows(*block) if src is None else src, dst_ref=rows(*block),
                send_sem=send_sems.at[k], recv_sem=recv_sems.at[k], device_id=to, device_id_type=MESH)

        mine = pltpu.make_async_copy(x_ref, rows(*me), local_sem)
        mine.start()
        first = [copy(0, me, sibling, src=x_ref)]
        first += [copy(1 + j, me, (*chip, c), src=x_ref) for j, chip in enumerate(chips)]
        for cp in first:
            cp.start()
        passed = [copy(4 + j, (*chip, c), sibling) for j, chip in enumerate(chips)]
        for j, chip in enumerate(chips):
            copy(1 + j, (*chip, c), me).wait_recv()
            passed[j].start()
        copy(0, sibling, me).wait_recv()
        for j, chip in enumerate(chips):
            copy(4 + j, (*chip, 1 - c), me).wait_recv()
        for cp in first + passed:
            cp.wait_send()
        mine.wait()

    return pl.pallas_call(
        body, out_shape=jax.ShapeDtypeStruct((8 * m_per, n), xs.dtype),
        in_specs=[pl.BlockSpec(memory_space=pltpu.VMEM)], out_specs=pl.BlockSpec(memory_space=pltpu.VMEM),
        scratch_shapes=[pltpu.SemaphoreType.DMA((7,)), pltpu.SemaphoreType.DMA((7,)), pltpu.SemaphoreType.DMA],
        compiler_params=pltpu.CompilerParams(vmem_limit_bytes=VMEM_LIMIT), name=name)(xs)


def sibling_exchange(name, srcs):
    n = len(srcs)

    def body(*refs):
        src, dst = refs[:n], refs[n:2 * n]
        send_sems, recv_sems = refs[2 * n], refs[2 * n + 1]
        sibling = (lax.axis_index("x"), lax.axis_index("y"), 1 - lax.axis_index("c"))
        cps = [pltpu.make_async_remote_copy(src_ref=src[k], dst_ref=dst[k], send_sem=send_sems.at[k],
                                            recv_sem=recv_sems.at[k], device_id=sibling, device_id_type=MESH)
               for k in range(n)]
        for cp in cps:
            cp.start()
        for cp in cps:
            cp.wait_recv()
        for cp in cps:
            cp.wait_send()

    anyspec = pl.BlockSpec(memory_space=pl.ANY)
    return pl.pallas_call(
        body, out_shape=[jax.ShapeDtypeStruct(s.shape, s.dtype) for s in srcs],
        in_specs=[anyspec] * n, out_specs=[anyspec] * n,
        scratch_shapes=[pltpu.SemaphoreType.DMA((n,)), pltpu.SemaphoreType.DMA((n,))], name=name)(*srcs)


def chip_exchange(name, srcs, scatter):
    n = len(srcs)
    shapes = [s.shape[1:] if scatter else s.shape for s in srcs]

    def body(*refs):
        src, dst = refs[:n], refs[n:2 * n]
        send_sems, recv_sems, local_sems = refs[2 * n], refs[2 * n + 1], refs[2 * n + 2]
        x, y, c = lax.axis_index("x"), lax.axis_index("y"), lax.axis_index("c")
        me = 2 * x + y
        chips = [(1 - x, y), (x, 1 - y), (1 - x, 1 - y)]
        local, sends, recvs = [], [], []
        for a in range(n):
            own = src[a].at[me] if scatter else src[a]
            local.append(pltpu.make_async_copy(own, dst[a].at[me], local_sems.at[a]))
            for j, (px, py) in enumerate(chips):
                to = 2 * px + py
                sends.append(pltpu.make_async_remote_copy(
                    src_ref=src[a].at[to] if scatter else src[a], dst_ref=dst[a].at[me],
                    send_sem=send_sems.at[3 * a + j], recv_sem=recv_sems.at[3 * a + j],
                    device_id=(px, py, c), device_id_type=MESH))
                recvs.append(pltpu.make_async_remote_copy(
                    src_ref=own, dst_ref=dst[a].at[to],
                    send_sem=send_sems.at[3 * a + j], recv_sem=recv_sems.at[3 * a + j],
                    device_id=(px, py, c), device_id_type=MESH))
        for cp in local + sends:
            cp.start()
        for cp in recvs:
            cp.wait_recv()
        for cp in sends:
            cp.wait_send()
        for cp in local:
            cp.wait()

    anyspec = pl.BlockSpec(memory_space=pl.ANY)
    return pl.pallas_call(
        body, out_shape=[jax.ShapeDtypeStruct((4,) + tuple(shp), s.dtype) for shp, s in zip(shapes, srcs)],
        in_specs=[anyspec] * n, out_specs=[anyspec] * n,
        scratch_shapes=[pltpu.SemaphoreType.DMA((3 * n,)), pltpu.SemaphoreType.DMA((3 * n,)),
                        pltpu.SemaphoreType.DMA((n,))], name=name)(*srcs)


def _chip_copies(src, land, send_sems, recv_sems, scatter):
    x, y, c = lax.axis_index("x"), lax.axis_index("y"), lax.axis_index("c")
    me = 2 * x + y
    pairs = []
    for a in range(len(src)):
        for j, (px, py) in enumerate([(1 - x, y), (x, 1 - y), (1 - x, 1 - y)]):
            to = 2 * px + py
            out = src[a].at[to] if scatter else src[a]
            kw = dict(send_sem=send_sems.at[3 * a + j], recv_sem=recv_sems.at[3 * a + j], device_id=(px, py, c),
                      device_id_type=MESH)
            pairs.append((pltpu.make_async_remote_copy(src_ref=out, dst_ref=land[a].at[me], **kw),
                          pltpu.make_async_remote_copy(src_ref=out, dst_ref=land[a].at[to], **kw)))
    return pairs


_HBM = pl.BlockSpec(memory_space=pltpu.HBM)
_SEM = pl.BlockSpec(memory_space=pltpu.SEMAPHORE)


def chip_exchange_start(name, groups, scatter):
    sizes = [len(s) for s, _ in groups]
    flat = [a for s, l in groups for a in list(s) + list(l)]
    ng = len(groups)

    def body(*refs):
        ins, outs = refs[:len(flat)], refs[len(flat):]
        off = 0
        for g, n in enumerate(sizes):
            src, land = ins[off:off + n], ins[off + n:off + 2 * n]
            off += 2 * n
            for send, _ in _chip_copies(src, land, outs[2 * g], outs[2 * g + 1], scatter):
                send.start()
        outs[-1][...] = jnp.zeros_like(outs[-1])

    out_shape = []
    for n in sizes:
        out_shape += [pltpu.SemaphoreType.DMA((3 * n,)), pltpu.SemaphoreType.DMA((3 * n,))]
    out_shape += [pltpu.HBM(a.shape, a.dtype) for a in flat] + [jax.ShapeDtypeStruct((8, 128), F32)]
    res = pl.pallas_call(
        body, out_shape=tuple(out_shape), in_specs=[_HBM] * len(flat),
        out_specs=tuple([_SEM] * (2 * ng) + [_HBM] * len(flat) + [pl.BlockSpec(memory_space=pltpu.VMEM)]),
        input_output_aliases={k: 2 * ng + k for k in range(len(flat))},
        compiler_params=pltpu.CompilerParams(has_side_effects=pltpu.SideEffectType.DATAFLOW_SIDE_EFFECTING),
        name=name)(*[pltpu.with_memory_space_constraint(a, pltpu.HBM) for a in flat])
    handles, off = [], 2 * ng
    for g, n in enumerate(sizes):
        handles.append((res[2 * g], res[2 * g + 1], list(res[off:off + n]), list(res[off + n:off + 2 * n])))
        off += 2 * n
    return handles, res[-1]


def chip_exchange_wait(name, handle, after, scatter):
    send_sems, recv_sems, srcs, lands = handle
    n = len(srcs)

    def body(*refs):
        src, land = refs[:n], refs[n:2 * n]
        for send, recv in _chip_copies(src, land, refs[2 * n], refs[2 * n + 1], scatter):
            send.wait_send()
            recv.wait_recv()

    res = pl.pallas_call(
        body, out_shape=tuple(pltpu.HBM(a.shape, a.dtype) for a in srcs + lands),
        in_specs=[_HBM] * (2 * n) + [_SEM, _SEM, pl.BlockSpec(memory_space=pl.ANY)],
        out_specs=tuple([_HBM] * (2 * n)), input_output_aliases={k: k for k in range(2 * n)},
        compiler_params=pltpu.CompilerParams(has_side_effects=pltpu.SideEffectType.DATAFLOW_SIDE_EFFECTING),
        name=name)(*srcs, *lands, send_sems, recv_sems, after)
    return list(res[n:])


def _own_slab(slab, chip):
    return lax.dynamic_update_index_in_dim(jnp.zeros((4,) + slab.shape, slab.dtype), slab, chip, 0)


def _rope_tables(d_rot, reps):
    rows = L // GRID_W
    row = np.repeat(np.arange(rows), GRID_W).astype(np.float32)
    col = np.tile(np.arange(GRID_W), rows).astype(np.float32)
    d_axis = d_rot // 2
    inv = (ROPE_THETA ** (-np.arange(0, d_axis, 2, dtype=np.float32) / d_axis)).astype(np.float32)
    ang = jnp.asarray(np.concatenate([row[:, None] * inv, col[:, None] * inv], axis=-1))
    cos, sin = jnp.cos(ang), jnp.sin(ang)
    c = jnp.repeat(cos, 2, axis=-1)
    s = jnp.stack([-sin, sin], axis=-1).reshape(L, d_rot)
    c = jnp.concatenate([jnp.ones((LC, d_rot), F32), c], axis=0)
    s = jnp.concatenate([jnp.zeros((LC, d_rot), F32), s], axis=0)
    return jnp.tile(c, (1, reps)), jnp.tile(s, (1, reps))


def _group_consts():
    g = np.arange(512) // 64
    avg = (g[:, None] == g[None, :]).astype(np.float32) / 64.0
    masks = (np.arange(8)[:, None] == g[None, :]).astype(np.float32).reshape(8, 1, 512)
    return jnp.asarray(avg), jnp.asarray(masks)


def _pack(items):
    flat = jnp.concatenate([a.reshape(-1).astype(F32) for a in items])
    n = flat.shape[0]
    rows = -(-n // D)
    rows = -(-rows // 8) * 8
    return jnp.pad(flat, (0, rows * D - n)).reshape(rows, D)


def _unpack(buf, shapes):
    lead = buf.shape[:-2]
    flat = buf.reshape(lead + (-1,))
    out, off = [], 0
    for shp in shapes:
        n = int(np.prod(shp))
        out.append(flat[..., off:off + n].reshape(lead + tuple(shp)))
        off += n
    return out


def _to_hs(g, rows_sharded):
    k, n = g.shape
    if rows_sharded:
        return g.reshape(4, 2, k // 8, n).transpose(1, 0, 2, 3)
    return g.reshape(2, k // 2, 4, n // 4).transpose(0, 2, 1, 3)


def _arrive(prm, key, after):
    if callable(prm[key]):
        prm[key](after)
    return prm[key]


def _layer_fwd(i, x, mods, prm, consts):
    sv = {}
    sv["x0"] = x
    h = modnorm_fwd(f"norm1_fwd_{i}", x, mods, prm["norm1_g"], 0, 1)
    sv["h"] = h
    p = proj_in(f"proj_in_{i}", h, _arrive(prm, "w_in", h))
    sv["p"] = p
    if i == 0:
        q, kv, m2 = even_tok_fwd(p, consts["cos_e"], consts["sin_e"], prm["gq"], prm["gk"], prm["gs"],
                                 prm["sgu_w"], prm["sgu_b"], consts["avg"], consts["masks"])
        o, lse = attn_fwd("attn_fwd_0", q, kv, GQA_HEADS, 64)
        sv.update(q=q, kv=kv)
    else:
        q, kv, y = odd_tok_fwd(p, consts["cos_o"], consts["sin_o"], prm["gq"], prm["gkv"], prm["wqn"], prm["wqr"],
                               prm["wkk"], prm["wkv"])
        o, lse = attn_fwd("attn_fwd_1", q, kv, MLA_HEADS, 96)
        z, m2 = conf_fwd(y, prm["conv_w"], prm["conv_b"], prm["ln_g"], prm["ln_b"])
        sv.update(q=q, kv=kv, y=y, z=z)
    sv.update(o=o, lse=lse, m2=m2)
    x1, y1 = proj_out(f"proj_out_{i}", o, m2, _arrive(prm, "w_out", o), x, mods, 2)
    sv.update(x1=x1, y1=y1)
    h2 = modnorm_fwd(f"norm2_fwd_{i}", x1, mods, prm["norm2_g"], 3, 4)
    a, f = mlp_up(f"mlp_up_{i}", h2, prm["w1"])
    x2, y2 = mlp_down(f"mlp_down_{i}", f, prm["w2"], x1, mods, 5)
    sv.update(h2=h2, a=a, f=f, y2=y2)
    return x2, sv


def _layer_bwd(i, dx, sv, mods, prm, consts):
    gr = {}
    dy2, dg2 = gate_bwd(f"gate2_bwd_{i}", dx, sv["y2"], mods, 5)
    da = mlp_bwd_da(f"mlp_bwd_da_{i}", dy2, prm["w2"], sv["a"])
    tiles8 = [(h, j) for h in range(2) for j in range(4)]
    gr["w1"] = mm_tn(f"grad_w1_{i}", sv["h2"], da, tiles8, 512, D).reshape(2, 4, 512, D)
    gr["w2"] = mm_tn(f"grad_w2_{i}", sv["f"], dy2, [(2 * j + h, 0) for h in range(2) for j in range(4)],
                     512, D).reshape(2, 4, 512, D)
    dh2 = mlp_bwd_dh(f"mlp_bwd_dh_{i}", da, prm["w1"])
    dx1, dsh2, dsc2, gr["norm2_g"] = modnorm_bwd(f"norm2_bwd_{i}", sv["x1"], dh2, dx, mods, prm["norm2_g"], 3, 4)
    dy1, dg1 = gate_bwd(f"gate1_bwd_{i}", dx1, sv["y1"], mods, 2)
    dcat = mm_nt(f"proj_out_bwd_{i}", dy1, prm["w_out"])
    t4 = [(2 * j + h, 0) for h in range(2) for j in range(2)]
    go = mm_tn(f"grad_wout_a_{i}", sv["o"], dy1, t4, 128, D).reshape(2, 2, 128, D)
    gm = mm_tn(f"grad_wout_b_{i}", sv["m2"], dy1, t4, 128, D).reshape(2, 2, 128, D)
    gr["w_out"] = jnp.concatenate([go, gm], axis=1)
    if i == 0:
        dq, dkv = attn_bwd("attn_bwd_0", sv["q"], sv["kv"], sv["o"], dcat, sv["lse"], GQA_HEADS, 64)
        dp, gr["gq"], gr["gk"], gr["gs"], gr["sgu_w"], gr["sgu_b"] = even_tok_bwd(
            sv["p"], dq, dkv, dcat, consts["cos_e"], consts["sin_e"], prm["gq"], prm["gk"],
            prm["gs"], prm["sgu_w"], prm["sgu_b"], consts["avg"], consts["masks"])
    else:
        dq, dkv = attn_bwd("attn_bwd_1", sv["q"], sv["kv"], sv["o"], dcat, sv["lse"], MLA_HEADS, 96)
        dz, gr["ln_g"], gr["ln_b"], gr["conv_b"] = conf_bwd_ln(sv["z"], dcat, prm["ln_g"], prm["ln_b"])
        dyc, gr["conv_w"] = conf_bwd_conv(sv["y"], dz, prm["conv_w"])
        dp, gr["gq"], gr["gkv"], gr["wqn"], gr["wqr"], gr["wkk"], gr["wkv"] = odd_tok_bwd(
            sv["p"], dq, dkv, dyc, consts["cos_o"], consts["sin_o"], prm["gq"], prm["gkv"], prm["wqn"], prm["wqr"],
            prm["wkk"], prm["wkv"])
    n_in = prm["w_in"].shape[1]
    gr["w_in"] = mm_tn(f"grad_win_{i}", sv["h"], dp, [(0, 0), (1, 0)], 512, n_in)
    dh = mm_nt(f"proj_in_bwd_{i}", dp, prm["w_in"])
    dx0, dsh1, dsc1, gr["norm1_g"] = modnorm_bwd(f"norm1_bwd_{i}", sv["x0"], dh, dx1, mods, prm["norm1_g"], 0, 1)
    dmods = jnp.concatenate([dsh1, dsc1, dg1, dsh2, dsc2, dg2], axis=1)
    return dx0, dmods, gr


def local_step(xcat, target, mods, prms, final_g, on_grads=None):
    avg, masks = _group_consts()
    cos_e, sin_e = _rope_tables(64, 8)
    cq, sq = _rope_tables(32, 8)
    ck, sk = _rope_tables(32, 1)
    padc = jnp.ones((SEQ, 96), F32)
    pads = jnp.zeros((SEQ, 96), F32)
    consts = dict(avg=avg, masks=masks, cos_e=cos_e, sin_e=sin_e,
                  cos_o=jnp.concatenate([cq, ck, padc], axis=1), sin_o=jnp.concatenate([sq, sk, pads], axis=1))
    x = xcat
    saved = []
    for i in range(2):
        x, sv = _layer_fwd(i, x, mods[i], prms[i], consts)
        saved.append(sv)
    dx, loss, dfg = final_loss(x, target, final_g)
    dmods, grads = [None, None], [None, None]
    for i in (1, 0):
        dx, dmods[i], grads[i] = _layer_bwd(i, dx, saved[i], mods[i], prms[i], consts)
        if on_grads is not None:
            on_grads(i, grads[i], dx)
    return loss, dx, dmods, grads, dfg


def _row(v):
    return v.reshape(1, -1).astype(F32)


def odd_in_params(od_w_in, w_uq, w_ukv):
    od = jnp.concatenate([od_w_in[:, 0:416], jnp.zeros((D, 96), od_w_in.dtype), od_w_in[:, 416:OD_IN]], axis=1)
    uq = w_uq.reshape(256, 8, 96)
    ukv = w_ukv.reshape(128, 8, 128)
    return dict(w_in=od, wqn=uq[:, :, :64].reshape(256, 512), wqr=uq[:, :, 64:].reshape(256, 256),
                wkk=ukv[:, :, :64].reshape(128, 512), wkv=ukv[:, :, 64:].reshape(128, 512))


def small_params(small):
    p0 = dict(norm1_g=_row(small["norm1_g"][0]), norm2_g=_row(small["norm2_g"][0]),
              gq=jnp.tile(_row(small["ev_q_norm_g"]), (1, 8)), gk=jnp.tile(_row(small["ev_k_norm_g"]), (1, 2)),
              gs=_row(small["ev_sgu_norm_g"]), sgu_w=small["ev_sgu_w"].reshape(8, 128, 128).astype(F32),
              sgu_b=small["ev_sgu_b"].reshape(8, 128, 1).astype(F32))
    p1 = dict(norm1_g=_row(small["norm1_g"][1]), norm2_g=_row(small["norm2_g"][1]),
              gq=_row(small["od_q_norm_g"]), gkv=_row(small["od_kv_norm_g"]),
              conv_w=jnp.pad(small["od_conv_w"].reshape(CONV_K, 512).astype(F32), ((0, 1), (0, 0))),
              conv_b=_row(small["od_conv_b"]), ln_g=_row(small["od_ln_g"]), ln_b=_row(small["od_ln_b"]))
    return [p0, p1]


def prep_params(ev_w_in, od_w_in, w_out, w1, w2, w_uq, w_ukv, small):
    p0, p1 = small_params(small)
    p0.update(w_in=ev_w_in, w_out=w_out[0], w1=w1[0], w2=w2[0])
    p1.update(odd_in_params(od_w_in, w_uq, w_ukv), w_out=w_out[1], w1=w1[1], w2=w2[1])
    return [p0, p1]


def small_grads_natural(grads, dfg):
    g0, g1 = grads
    return dict(
        norm1_g=jnp.concatenate([g0["norm1_g"], g1["norm1_g"]], axis=0),
        norm2_g=jnp.concatenate([g0["norm2_g"], g1["norm2_g"]], axis=0),
        ev_q_norm_g=g0["gq"].reshape(8, 64).sum(0).reshape(1, 64),
        ev_k_norm_g=g0["gk"].reshape(2, 64).sum(0).reshape(1, 64),
        ev_sgu_norm_g=g0["gs"].reshape(1, 8, 64),
        ev_sgu_w=g0["sgu_w"].reshape(1, 8, 128, 128),
        ev_sgu_b=g0["sgu_b"].reshape(1, 8, 128),
        od_q_norm_g=g1["gq"].reshape(1, 256),
        od_kv_norm_g=g1["gkv"].reshape(1, 128),
        od_conv_w=g1["conv_w"][0:CONV_K].reshape(1, CONV_K, 512),
        od_conv_b=g1["conv_b"].reshape(1, 512),
        od_ln_g=g1["ln_g"].reshape(1, 512),
        od_ln_b=g1["ln_b"].reshape(1, 512),
        final_g=dfg.reshape(D))


def layer_grads_hs(i, g):
    def cols(a):
        k, n = a.shape
        return a.reshape(2, k // 2, 4, n // 4).transpose(0, 2, 1, 3).astype(BF)

    tail = [(("w_out", i), g["w_out"]), (("mlp_w1", i), g["w1"]), (("mlp_w2", i), g["w2"])]
    if i == 0:
        return [(("ev_w_in", 0), cols(g["w_in"].reshape(D, EV_IN)))] + tail
    od = g["w_in"].reshape(D, OD_PAD)
    od = jnp.concatenate([od[:, 0:416], od[:, 512:OD_PAD]], axis=1)
    uq = jnp.concatenate([g["wqn"].reshape(256, 8, 64), g["wqr"].reshape(256, 8, 32)], axis=2).reshape(256, 768)
    ukv = jnp.concatenate([g["wkk"].reshape(128, 8, 64), g["wkv"].reshape(128, 8, 64)], axis=2).reshape(128, 1024)
    return [(("od_w_in", 0), cols(od)), (("od_w_uq", 0), cols(uq)), (("od_w_ukv", 0), cols(ukv))] + tail


def big_grads_hs(grads):
    d = dict(layer_grads_hs(0, grads[0]) + layer_grads_hs(1, grads[1]))
    return dict(ev_w_in=d[("ev_w_in", 0)], od_w_in=d[("od_w_in", 0)], od_w_uq=d[("od_w_uq", 0)],
                od_w_ukv=d[("od_w_ukv", 0)], w_out=[d[("w_out", 0)], d[("w_out", 1)]],
                mlp_w1=[d[("mlp_w1", 0)], d[("mlp_w1", 1)]], mlp_w2=[d[("mlp_w2", 0)], d[("mlp_w2", 1)]])


def grads_to_natural(grads, dfg):
    out = small_grads_natural(grads, dfg)
    hs = big_grads_hs(grads)

    def from_cols(a):
        return a.transpose(0, 2, 1, 3).reshape(2 * a.shape[2], 4 * a.shape[3])

    def from_rows(a):
        return a.transpose(1, 0, 2, 3).reshape(8 * a.shape[2], a.shape[3])

    out["ev_w_in"] = from_cols(hs["ev_w_in"])[None]
    out["od_w_in"] = from_cols(hs["od_w_in"])[None]
    out["od_w_uq"] = from_cols(hs["od_w_uq"])[None]
    out["od_w_ukv"] = from_cols(hs["od_w_ukv"])[None]
    out["w_out"] = jnp.stack([from_rows(a) for a in hs["w_out"]])
    out["mlp_w1"] = jnp.stack([from_cols(a) for a in hs["mlp_w1"]])
    out["mlp_w2"] = jnp.stack([from_rows(a) for a in hs["mlp_w2"]])
    return out


WEIGHT_NAMES = ['c_ctx', 'ada_w', 'ada_b', 'norm1_g', 'norm2_g', 'w_out', 'mlp_w1', 'mlp_w2', 'ev_w_in',
                'ev_q_norm_g', 'ev_k_norm_g', 'ev_sgu_norm_g', 'ev_sgu_w', 'ev_sgu_b', 'od_w_in', 'od_q_norm_g',
                'od_kv_norm_g', 'od_w_uq', 'od_w_ukv', 'od_conv_w', 'od_conv_b', 'od_ln_g', 'od_ln_b', 'final_g']
REPL_SMALL = ['norm1_g', 'norm2_g', 'ev_q_norm_g', 'ev_k_norm_g', 'ev_sgu_norm_g', 'ev_sgu_w', 'ev_sgu_b',
              'od_kv_norm_g', 'final_g']
SHARD_SMALL = ['od_q_norm_g', 'od_conv_w', 'od_conv_b', 'od_ln_g', 'od_ln_b']
BIG = ['w_out', 'mlp_w1', 'mlp_w2', 'ev_w_in', 'od_w_in', 'od_w_uq', 'od_w_ukv']


def _gather_last(parts):
    return jnp.concatenate([parts[k] for k in range(4)], axis=-1)


def _reduce_begin(tag, hs_list, ci, chip):
    mine = [lax.dynamic_index_in_dim(a, ci, 0, keepdims=False) for a in hs_list]
    other = [lax.dynamic_index_in_dim(a, 1 - ci, 0, keepdims=False) for a in hs_list]
    got = sibling_exchange(f"rs_sibling_in_{tag}", other)
    pair = [add_pairs(f"rs_add_{tag}_{k}", a, b) for k, (a, b) in enumerate(zip(mine, got))]
    lands = [_own_slab(lax.dynamic_index_in_dim(p, chip, 0, keepdims=False), chip) for p in pair]
    (handle,), token = chip_exchange_start(f"rs_chips_start_{tag}", [(pair, lands)], scatter=True)
    return handle, token


def _reduce_end(tag, handle, after):
    land = chip_exchange_wait(f"rs_chips_wait_{tag}", handle, after, scatter=True)
    return [sum_lead(f"rs_sum_{tag}_{k}", a) for k, a in enumerate(land)]


def _reduce_finish(halves, ci):
    sib = sibling_exchange("rs_sibling_out", halves)
    return [jnp.where(ci == 0, jnp.concatenate([h, s], axis=0), jnp.concatenate([s, h], axis=0))
            for h, s in zip(halves, sib)]


def kernel(x, c, ctx, c_ctx, ada_w, ada_b, norm1_g, norm2_g, w_out, mlp_w1, mlp_w2, ev_w_in, ev_q_norm_g, ev_k_norm_g, ev_sgu_norm_g, ev_sgu_w, ev_sgu_b, od_w_in, od_q_norm_g, od_kv_norm_g, od_w_uq, od_w_ukv, od_conv_w, od_conv_b, od_ln_g, od_ln_b, final_g, loss_target, m_c_ctx, m_ada_w, m_ada_b, m_norm1_g, m_norm2_g, m_w_out, m_mlp_w1, m_mlp_w2, m_ev_w_in, m_ev_q_norm_g, m_ev_k_norm_g, m_ev_sgu_norm_g, m_ev_sgu_w, m_ev_sgu_b, m_od_w_in, m_od_q_norm_g, m_od_kv_norm_g, m_od_w_uq, m_od_w_ukv, m_od_conv_w, m_od_conv_b, m_od_ln_g, m_od_ln_b, m_final_g, v_c_ctx, v_ada_w, v_ada_b, v_norm1_g, v_norm2_g, v_w_out, v_mlp_w1, v_mlp_w2, v_ev_w_in, v_ev_q_norm_g, v_ev_k_norm_g, v_ev_sgu_norm_g, v_ev_sgu_w, v_ev_sgu_b, v_od_w_in, v_od_q_norm_g, v_od_kv_norm_g, v_od_w_uq, v_od_w_ukv, v_od_conv_w, v_od_conv_b, v_od_ln_g, v_od_ln_b, v_final_g):
    w = dict(c_ctx=c_ctx, ada_w=ada_w, ada_b=ada_b, norm1_g=norm1_g, norm2_g=norm2_g, w_out=w_out, mlp_w1=mlp_w1,
             mlp_w2=mlp_w2, ev_w_in=ev_w_in, ev_q_norm_g=ev_q_norm_g, ev_k_norm_g=ev_k_norm_g,
             ev_sgu_norm_g=ev_sgu_norm_g, ev_sgu_w=ev_sgu_w, ev_sgu_b=ev_sgu_b, od_w_in=od_w_in,
             od_q_norm_g=od_q_norm_g, od_kv_norm_g=od_kv_norm_g, od_w_uq=od_w_uq, od_w_ukv=od_w_ukv,
             od_conv_w=od_conv_w, od_conv_b=od_conv_b, od_ln_g=od_ln_g, od_ln_b=od_ln_b, final_g=final_g)
    mom = dict(c_ctx=m_c_ctx, ada_w=m_ada_w, ada_b=m_ada_b, norm1_g=m_norm1_g, norm2_g=m_norm2_g, w_out=m_w_out,
               mlp_w1=m_mlp_w1, mlp_w2=m_mlp_w2, ev_w_in=m_ev_w_in, ev_q_norm_g=m_ev_q_norm_g,
               ev_k_norm_g=m_ev_k_norm_g, ev_sgu_norm_g=m_ev_sgu_norm_g, ev_sgu_w=m_ev_sgu_w, ev_sgu_b=m_ev_sgu_b,
               od_w_in=m_od_w_in, od_q_norm_g=m_od_q_norm_g, od_kv_norm_g=m_od_kv_norm_g, od_w_uq=m_od_w_uq,
               od_w_ukv=m_od_w_ukv, od_conv_w=m_od_conv_w, od_conv_b=m_od_conv_b, od_ln_g=m_od_ln_g,
               od_ln_b=m_od_ln_b, final_g=m_final_g)
    var = dict(c_ctx=v_c_ctx, ada_w=v_ada_w, ada_b=v_ada_b, norm1_g=v_norm1_g, norm2_g=v_norm2_g, w_out=v_w_out,
               mlp_w1=v_mlp_w1, mlp_w2=v_mlp_w2, ev_w_in=v_ev_w_in, ev_q_norm_g=v_ev_q_norm_g,
               ev_k_norm_g=v_ev_k_norm_g, ev_sgu_norm_g=v_ev_sgu_norm_g, ev_sgu_w=v_ev_sgu_w, ev_sgu_b=v_ev_sgu_b,
               od_w_in=v_od_w_in, od_q_norm_g=v_od_q_norm_g, od_kv_norm_g=v_od_kv_norm_g, od_w_uq=v_od_w_uq,
               od_w_ukv=v_od_w_ukv, od_conv_w=v_od_conv_w, od_conv_b=v_od_conv_b, od_ln_g=v_od_ln_g,
               od_ln_b=v_od_ln_b, final_g=v_final_g)
    xi, yi, ci = lax.axis_index("x"), lax.axis_index("y"), lax.axis_index("c")
    chip = 2 * xi + yi
    dev = 2 * chip + ci

    shard_shapes = [w[n].shape for n in SHARD_SMALL]
    g0 = all_gather8("ag_small", _pack([c] + [w[n] for n in SHARD_SMALL]))
    g0 = g0.reshape(8, -1, D)
    parts = _unpack(g0, [c.shape] + shard_shapes)
    c_all = parts[0].reshape(16, D)
    small_full = {n: _gather_last(p[0::2]) for n, p in zip(SHARD_SMALL, parts[1:])}
    call = jnp.concatenate([c_all, c_ctx.reshape(1, D), jnp.zeros((NC - 17, D), F32)], axis=0)

    cols = ada_w.shape[2]
    ada_b_sh = lax.dynamic_slice(ada_b, (0, chip * cols), (2, cols)).reshape(2, 1, cols)
    mt = mods_fwd(call, ada_w, ada_b_sh)
    mt = all_gather8("ag_mods", mt.reshape(2 * NC, cols)).reshape(8, 2, NC, cols)
    table = mt[0::2].transpose(1, 2, 0, 3).reshape(2, NC, 4 * cols)
    mods = []
    for i in range(2):
        lat = lax.dynamic_slice(table[i], (2 * dev, 0), (2, 4 * cols))
        mc = table[i, 16]
        mods.append(jnp.stack([mc, lat[0], mc, lat[1]]).reshape(4 * N_MOD, 1, D))

    order = [[("ev_w_in", 0)], [("w_out", 0), ("mlp_w1", 0), ("mlp_w2", 0)],
             [("od_w_in", 0), ("od_w_uq", 0), ("od_w_ukv", 0), ("w_out", 1), ("mlp_w1", 1), ("mlp_w2", 1)]]
    groups = []
    for names in order:
        srcs = [w[n][i].astype(BF) for n, i in names]
        groups.append((srcs, [_own_slab(s, chip) for s in srcs]))
    handles, token = chip_exchange_start("gather_start", groups, scatter=False)
    mods[0] = mods[0] + token[0, 0]
    small = {n: w[n] for n in REPL_SMALL}
    small.update(small_full)
    prms = small_params(small)

    def arrive_ev_in(after):
        (ev,) = chip_exchange_wait("gather_wait_0", handles[0], after, scatter=False)
        prms[0]["w_in"] = _gather_last(ev)

    def arrive_ev_rest(after):
        wo, w1, w2 = chip_exchange_wait("gather_wait_1", handles[1], after, scatter=False)
        prms[0].update(w_out=wo.reshape(D, D), w1=w1, w2=w2)

    def arrive_od(after):
        od, uq, ukv, wo, w1, w2 = chip_exchange_wait("gather_wait_2", handles[2], after, scatter=False)
        prms[1].update(odd_in_params(_gather_last(od), _gather_last(uq), _gather_last(ukv)),
                       w_out=wo.reshape(D, D), w1=w1, w2=w2)

    prms[0]["w_in"] = arrive_ev_in
    prms[0]["w_out"] = arrive_ev_rest
    prms[1]["w_in"] = arrive_od

    in_flight = {}

    def on_grads(i, g, after):
        if i == 1:
            names, arrs = zip(*layer_grads_hs(1, g))
            in_flight[1] = (names, _reduce_begin("l1", list(arrs), ci, chip)[0])

    xcat = jnp.concatenate([ctx, x], axis=1).reshape(R, D)
    loss_p, dx, dmods, grads, dfg = local_step(xcat, loss_target.reshape(NEX * L, D), mods, prms,
                                               final_g.reshape(1, D), on_grads)
    grad_x = dx.reshape(NEX, SEQ, D)[:, LC:]
    names1, handle1 = in_flight[1]
    halves1 = _reduce_end("l1", handle1, dx)
    names0, arrs0 = zip(*layer_grads_hs(0, grads[0]))
    handle0, token0 = _reduce_begin("l0", list(arrs0), ci, chip)
    halves0 = _reduce_end("l0", handle0, token0)
    red = dict(zip(names0 + names1, _reduce_finish(halves0 + halves1, ci)))

    sg = small_grads_natural(grads, dfg)
    dm = jnp.stack([d.reshape(4, N_MOD * D) for d in dmods])
    small_names = REPL_SMALL + SHARD_SMALL
    items = [dm[:, 1::2], dm[:, 0] + dm[:, 2]] + [sg[n] for n in small_names] + [loss_p[0:1, 0:1]]
    shapes = [a.shape for a in items]
    g1 = all_gather8("ag_grads", _pack(items))
    rows1 = g1.shape[0] // 8
    g1 = g1.reshape(8, rows1, D)
    tot = _unpack(sum_lead("sum_small", g1), shapes)
    dm_lat = _unpack(g1, shapes[:1])[0]
    dm_lat = dm_lat.transpose(1, 0, 2, 3).reshape(2, 16, N_MOD * D)
    dm_all = jnp.concatenate([dm_lat, tot[1][:, None], jnp.zeros((2, NC - 17, N_MOD * D), F32)], axis=1)
    gsum = dict(zip(small_names, tot[2:2 + len(small_names)]))
    loss = tot[-1].reshape(())
    grad = {n: gsum[n].reshape(w[n].shape) for n in REPL_SMALL}
    for n in SHARD_SMALL:
        k = w[n].shape[-1]
        grad[n] = lax.dynamic_slice_in_dim(gsum[n], chip * k, k, axis=gsum[n].ndim - 1)
    grad["ada_b"] = sum_lead("sum_ada_b", dm_all.transpose(1, 0, 2).reshape(NC, 2 * N_MOD, D)).reshape(2, N_MOD * D)

    dm_sh = lax.dynamic_slice(dm_all, (0, 0, chip * cols), (2, NC, cols))
    grad["ada_w"], dcc = ada_bwd(call, ada_w, dm_sh)
    dcc = all_gather8("ag_cctx", dcc).reshape(8, 8, D)
    grad["c_ctx"] = sum_lead("sum_cctx", dcc[0::2])[0]

    for n in BIG:
        grad[n] = jnp.stack([red[(n, i)] for i in range(w[n].shape[0])]).reshape(w[n].shape)

    delta, new_m, new_v = {}, {}, {}
    for n in ['ada_w'] + BIG:
        shp = w[n].shape
        two_d = (shp[0] * shp[1], shp[2])
        d_, m_, v_ = adamw(f"adamw_{n}", w[n].reshape(two_d), grad[n].reshape(two_d), mom[n].reshape(two_d),
                           var[n].reshape(two_d))
        delta[n], new_m[n], new_v[n] = d_.reshape(shp), m_.reshape(shp), v_.reshape(shp)
    rest = [n for n in WEIGHT_NAMES if n not in ['ada_w'] + BIG]
    rshapes = [w[n].shape for n in rest]
    d_, m_, v_ = adamw("adamw_small", _pack([w[n] for n in rest]), _pack([grad[n] for n in rest]),
                       _pack([mom[n] for n in rest]), _pack([var[n] for n in rest]))
    for dst, buf in ((delta, d_), (new_m, m_), (new_v, v_)):
        dst.update(zip(rest, _unpack(buf, rshapes)))

    return (loss, grad_x, *[grad[n] for n in WEIGHT_NAMES], *[delta[n] for n in WEIGHT_NAMES],
            *[new_m[n] for n in WEIGHT_NAMES], *[new_v[n] for n in WEIGHT_NAMES])
```

```python
import functools
import math

import numpy as np
import jax
import jax.numpy as jnp
from jax import lax
from jax.experimental import pallas as pl
from jax.experimental.pallas import tpu as pltpu

F32 = jnp.float32
BF = jnp.bfloat16
HI = lax.Precision.HIGHEST
MESH = pl.DeviceIdType.MESH

D = 1024
L = 2048
LC = 256
SEQ = L + LC
NEX = 2
R = NEX * SEQ
TB = 256
BPE = SEQ // TB
NBLK = R // TB
GRID_W = 64
FF = 4 * D
EPS = 1e-6
ROPE_THETA = 10000.0
N_MOD = 6
EV_IN = 1792
OD_IN = 1440
OD_PAD = 1536
VMEM_LIMIT = 60 * 1024 * 1024

ADAM_LR = 0.001
ADAM_B1 = 0.9
ADAM_B2 = 0.999
ADAM_EPS = 1e-08
ADAM_WD = 0.01
ADAM_STEP = 10

NT = (((1,), (1,)), ((), ()))
TN = (((0,), (0,)), ((), ()))


def _cparams(sem=None):
    return pltpu.CompilerParams(dimension_semantics=sem, vmem_limit_bytes=VMEM_LIMIT)


@jax.custom_vjp
def _mm(a, b):
    return jnp.dot(a.astype(BF), b.astype(BF), preferred_element_type=F32)


def _mm_fwd(a, b):
    return _mm(a, b), (a, b)


def _mm_bwd(res, g):
    a, b = res
    gb = g.astype(BF)
    da = lax.dot_general(gb, b.astype(BF), NT, preferred_element_type=F32)
    db = lax.dot_general(a.astype(BF), gb, TN, preferred_element_type=F32)
    return da, db


_mm.defvjp(_mm_fwd, _mm_bwd)


@jax.custom_vjp
def _swap(x):
    n = x.shape[-1]
    ax = x.ndim - 1
    lane = lax.broadcasted_iota(jnp.int32, x.shape, ax)
    return jnp.where(lane % 2 == 0, pltpu.roll(x, n - 1, ax), pltpu.roll(x, 1, ax))


_swap.defvjp(lambda x: (_swap(x), None), lambda _, g: (_swap(g),))


def _rope(x, cos, sin):
    return x * cos + _swap(x) * sin


def _rmsn(x, g):
    return x * lax.rsqrt(jnp.mean(x * x, axis=-1, keepdims=True) + EPS) * g


def _grmsn(x, g, avg):
    ms = jnp.dot(x * x, avg, precision=HI, preferred_element_type=F32)
    return x * lax.rsqrt(ms + EPS) * g


def _modnorm(x, g, sh, sc):
    return _rmsn(x, g) * (1.0 + sc) + sh


def _gelu(x):
    return 0.5 * x * (1.0 + jnp.tanh(0.7978845608028654 * (x + 0.044715 * (x * x * x))))


def _silu(x):
    return x * jax.nn.sigmoid(x)


def _acc(ref, val, first):
    @pl.when(first)
    def _():
        ref[...] = val

    @pl.when(jnp.logical_not(first))
    def _():
        ref[...] += val


def _seg(i):
    return 2 * (i // BPE) + jnp.minimum(i % BPE, 1)


def _seg_first(i):
    return (i % BPE) <= 1


def _rb_call(name, body, row_in=(), mod_in=(), pos_in=(), full_in=(), shift_in=(),
             row_out=(), seg_out=(), acc_out=(), scratch=()):
    in_specs, args = [], []
    for a in row_in:
        in_specs.append(pl.BlockSpec((TB, a.shape[1]), lambda i: (i, 0)))
        args.append(a)
    for tab, m in mod_in:
        in_specs.append(pl.BlockSpec((1, 1, D), lambda i, m=m: (_seg(i) * N_MOD + m, 0, 0)))
        args.append(tab)
    for a in pos_in:
        in_specs.append(pl.BlockSpec((TB, a.shape[1]), lambda i: (i % BPE, 0)))
        args.append(a)
    for a in full_in:
        in_specs.append(pl.BlockSpec(a.shape, lambda i, n=a.ndim: (0,) * n))
        args.append(a)
    for a, d in shift_in:
        in_specs.append(pl.BlockSpec((TB, a.shape[1]), lambda i, d=d: (jnp.clip(i + d, 0, NBLK - 1), 0)))
        args.append(a)
    out_specs, out_shape = [], []
    for w, dt in row_out:
        out_specs.append(pl.BlockSpec((TB, w), lambda i: (i, 0)))
        out_shape.append(jax.ShapeDtypeStruct((R, w), dt))
    for w in seg_out:
        out_specs.append(pl.BlockSpec((1, 1, w), lambda i: (_seg(i), 0, 0)))
        out_shape.append(jax.ShapeDtypeStruct((4, 1, w), F32))
    for shp in acc_out:
        out_specs.append(pl.BlockSpec(shp, lambda i, n=len(shp): (0,) * n))
        out_shape.append(jax.ShapeDtypeStruct(shp, F32))

    def kern(*refs):
        body(pl.program_id(0), *refs)

    sem = ("arbitrary",) if (seg_out or acc_out) else ("parallel",)
    return pl.pallas_call(kern, grid=(NBLK,), in_specs=in_specs, out_specs=out_specs, out_shape=out_shape,
                          scratch_shapes=list(scratch), compiler_params=_cparams(sem), name=name)(*args)


def modnorm_fwd(name, x, mods, g, m_sh, m_sc):
    def body(i, x_ref, sh_ref, sc_ref, g_ref, h_ref):
        h_ref[...] = _modnorm(x_ref[...], g_ref[...], sh_ref[0], sc_ref[0]).astype(BF)

    return _rb_call(name, body, row_in=(x,), mod_in=((mods, m_sh), (mods, m_sc)), full_in=(g,),
                    row_out=((D, BF),))[0]


def modnorm_bwd(name, x, dh, dx_in, mods, g, m_sh, m_sc):
    def body(i, x_ref, dh_ref, dxin_ref, sh_ref, sc_ref, g_ref, dx_ref, dsh_ref, dsc_ref, dg_ref):
        _, vjp = jax.vjp(_modnorm, x_ref[...], g_ref[...], sh_ref[0], sc_ref[0])
        dx, dg, dsh, dsc = vjp(dh_ref[...].astype(F32))
        dx_ref[...] = dxin_ref[...] + dx
        _acc(dsh_ref, dsh[None], _seg_first(i))
        _acc(dsc_ref, dsc[None], _seg_first(i))
        _acc(dg_ref, dg, i == 0)

    return _rb_call(name, body, row_in=(x, dh, dx_in), mod_in=((mods, m_sh), (mods, m_sc)), full_in=(g,),
                    row_out=((D, F32),), seg_out=(D, D), acc_out=((1, D),))


def gate_bwd(name, dx, y, mods, m_gate):
    def body(i, dx_ref, y_ref, gt_ref, dy_ref, dgt_ref):
        dxv = dx_ref[...]
        dy_ref[...] = (dxv * gt_ref[0]).astype(BF)
        _acc(dgt_ref, jnp.sum(dxv * y_ref[...].astype(F32), axis=0, keepdims=True)[None], _seg_first(i))

    return _rb_call(name, body, row_in=(dx, y), mod_in=((mods, m_gate),), row_out=((D, BF),), seg_out=(D,))


def proj_in(name, h, w):
    n = w.shape[1]

    def body(i, h_ref, w_ref, o_ref):
        o_ref[...] = jnp.dot(h_ref[...], w_ref[...], preferred_element_type=F32).astype(BF)

    return _rb_call(name, body, row_in=(h,), full_in=(w,), row_out=((n, BF),))[0]


def proj_out(name, a1, a2, w, x, mods, m_gate):
    k1 = a1.shape[1]

    def body(i, a1_ref, a2_ref, x_ref, gt_ref, w_ref, xo_ref, y_ref):
        y = jnp.dot(a1_ref[...], w_ref[:k1, :], preferred_element_type=F32)
        y = y + jnp.dot(a2_ref[...], w_ref[k1:, :], preferred_element_type=F32)
        y_ref[...] = y.astype(BF)
        xo_ref[...] = x_ref[...] + gt_ref[0] * y

    return _rb_call(name, body, row_in=(a1, a2, x), mod_in=((mods, m_gate),), full_in=(w,),
                    row_out=((D, F32), (D, BF)))


def mlp_up(name, h, w1):
    def body(i, h_ref, w_ref, a_ref, f_ref):
        hv = h_ref[...]
        for n in range(4):
            a = jnp.dot(hv, w_ref[n], preferred_element_type=F32)
            a_ref[:, n * D:(n + 1) * D] = a.astype(BF)
            r = jnp.maximum(a, 0.0)
            f_ref[:, n * D:(n + 1) * D] = (r * r).astype(BF)

    return _rb_call(name, body, row_in=(h,), full_in=(w1,), row_out=((FF, BF), (FF, BF)))


def mlp_down(name, f, w2, x, mods, m_gate):
    def body(i, f_ref, x_ref, gt_ref, w_ref, xo_ref, y_ref):
        y = jnp.dot(f_ref[:, 0:D], w_ref[0], preferred_element_type=F32)
        for n in range(1, 4):
            y = y + jnp.dot(f_ref[:, n * D:(n + 1) * D], w_ref[n], preferred_element_type=F32)
        y_ref[...] = y.astype(BF)
        xo_ref[...] = x_ref[...] + gt_ref[0] * y

    return _rb_call(name, body, row_in=(f, x), mod_in=((mods, m_gate),), full_in=(w2,),
                    row_out=((D, F32), (D, BF)))


def mm_nt(name, g, w):
    k = w.shape[0]

    def body(i, g_ref, w_ref, o_ref):
        o_ref[...] = lax.dot_general(g_ref[...], w_ref[...], NT, preferred_element_type=F32).astype(BF)

    return _rb_call(name, body, row_in=(g,), full_in=(w,), row_out=((k, BF),))[0]


def mlp_bwd_da(name, dy, w2, a):
    def body(i, dy_ref, a_ref, w_ref, da_ref):
        dyv = dy_ref[...]
        for n in range(4):
            df = lax.dot_general(dyv, w_ref[n], NT, preferred_element_type=F32)
            av = a_ref[:, n * D:(n + 1) * D].astype(F32)
            da_ref[:, n * D:(n + 1) * D] = (df * (2.0 * jnp.maximum(av, 0.0))).astype(BF)

    return _rb_call(name, body, row_in=(dy, a), full_in=(w2,), row_out=((FF, BF),))[0]


def mlp_bwd_dh(name, da, w1):
    def body(i, da_ref, w_ref, dh_ref):
        acc = lax.dot_general(da_ref[:, 0:D], w_ref[0], NT, preferred_element_type=F32)
        for n in range(1, 4):
            acc = acc + lax.dot_general(da_ref[:, n * D:(n + 1) * D], w_ref[n], NT, preferred_element_type=F32)
        dh_ref[...] = acc.astype(BF)

    return _rb_call(name, body, row_in=(da,), full_in=(w1,), row_out=((D, BF),))[0]


TN_ROWS = 512


def mm_tn(name, a, g, tiles, th, tw):
    nt = len(tiles)
    acs = jnp.asarray([t[0] for t in tiles], jnp.int32)
    gcs = jnp.asarray([t[1] for t in tiles], jnp.int32)
    nr = R // TN_ROWS

    def kern(ac_ref, gc_ref, a_ref, g_ref, o_ref, acc_ref):
        r = pl.program_id(1)

        @pl.when(r == 0)
        def _():
            acc_ref[...] = jnp.zeros_like(acc_ref)

        acc_ref[...] += lax.dot_general(a_ref[...], g_ref[...], TN, preferred_element_type=F32)

        @pl.when(r == nr - 1)
        def _():
            o_ref[...] = acc_ref[...].astype(BF)

    grid_spec = pltpu.PrefetchScalarGridSpec(
        num_scalar_prefetch=2, grid=(nt, nr),
        in_specs=[pl.BlockSpec((TN_ROWS, th), lambda t, r, ac, gc: (r, ac[t])),
                  pl.BlockSpec((TN_ROWS, tw), lambda t, r, ac, gc: (r, gc[t]))],
        out_specs=pl.BlockSpec((None, th, tw), lambda t, r, ac, gc: (t, 0, 0)),
        scratch_shapes=[pltpu.VMEM((th, tw), F32)])
    return pl.pallas_call(kern, grid_spec=grid_spec, out_shape=jax.ShapeDtypeStruct((nt, th, tw), BF),
                          compiler_params=_cparams(("parallel", "arbitrary")), name=name)(acs, gcs, a, g)


def _even_tok(q, k, zu, zv, gq, gk, gs, ws, bs, cq, sq, ck, sk, avg, masks):
    qr = _rope(_grmsn(q, gq, avg), cq, sq) * GQA_SCALE
    kr = _rope(_grmsn(k, gk, avg[:128, :128]), ck, sk)
    u = _gelu(zu)
    v = _grmsn(_gelu(zv), gs, avg)
    sv = None
    for g in range(8):
        t = masks[g] * (_mm(ws[g], v) + bs[g])
        sv = t if sv is None else sv + t
    return qr, kr, u * sv


def even_tok_fwd(p, cos, sin, gq, gk, gs, sgu_w, sgu_b, avg, masks):
    def body(i, p_ref, cos_ref, sin_ref, gq_ref, gk_ref, gs_ref, w_ref, b_ref, avg_ref, mk_ref, q_ref, kv_ref, m_ref):
        avgv = avg_ref[...]
        ws = [w_ref[g] for g in range(8)]
        bs = [b_ref[g] for g in range(8)]
        mks = [mk_ref[g] for g in range(8)]
        for c in range(2):
            rs = pl.ds(c * 128, 128)
            qr, kr, m = _even_tok(
                p_ref[rs, 0:512].astype(F32), p_ref[rs, 512:640].astype(F32),
                p_ref[rs, 768:1280].astype(F32), p_ref[rs, 1280:1792].astype(F32),
                gq_ref[...], gk_ref[...], gs_ref[...], ws, bs,
                cos_ref[rs, :], sin_ref[rs, :], cos_ref[rs, 0:128], sin_ref[rs, 0:128], avgv, mks)
            q_ref[rs, :] = qr.astype(BF)
            kv_ref[rs, 0:128] = kr.astype(BF)
            kv_ref[rs, 128:256] = p_ref[rs, 640:768]
            m_ref[rs, :] = m.astype(BF)

    return _rb_call("even_tok_fwd", body, row_in=(p,), pos_in=(cos, sin),
                    full_in=(gq, gk, gs, sgu_w, sgu_b, avg, masks), row_out=((512, BF), (256, BF), (512, BF)))


def even_tok_bwd(p, dq, dkv, dcat, cos, sin, gq, gk, gs, sgu_w, sgu_b, avg, masks):
    def body(i, p_ref, dq_ref, dkv_ref, dcat_ref, cos_ref, sin_ref, gq_ref, gk_ref, gs_ref, w_ref, b_ref,
             avg_ref, mk_ref, dp_ref, dgq_ref, dgk_ref, dgs_ref, dw_ref, db_ref):
        avgv = avg_ref[...]
        ws = [w_ref[g] for g in range(8)]
        bs = [b_ref[g] for g in range(8)]
        mks = [mk_ref[g] for g in range(8)]
        tot = None
        for c in range(2):
            rs = pl.ds(c * 128, 128)
            cq, sq, ck, sk = cos_ref[rs, :], sin_ref[rs, :], cos_ref[rs, 0:128], sin_ref[rs, 0:128]

            def f(q, k, zu, zv, gq, gk, gs, ws, bs):
                return _even_tok(q, k, zu, zv, gq, gk, gs, ws, bs, cq, sq, ck, sk, avgv, mks)

            _, vjp = jax.vjp(f, p_ref[rs, 0:512].astype(F32), p_ref[rs, 512:640].astype(F32),
                             p_ref[rs, 768:1280].astype(F32), p_ref[rs, 1280:1792].astype(F32),
                             gq_ref[...], gk_ref[...], gs_ref[...], ws, bs)
            d = vjp((dq_ref[rs, :].astype(F32), dkv_ref[rs, 0:128], dcat_ref[rs, 512:1024].astype(F32)))
            dp_ref[rs, 0:512] = d[0].astype(BF)
            dp_ref[rs, 512:640] = d[1].astype(BF)
            dp_ref[rs, 640:768] = dkv_ref[rs, 128:256].astype(BF)
            dp_ref[rs, 768:1280] = d[2].astype(BF)
            dp_ref[rs, 1280:1792] = d[3].astype(BF)
            part = [d[4], d[5], d[6]] + list(d[7]) + list(d[8])
            tot = part if tot is None else [a + b for a, b in zip(tot, part)]
        refs = [dgq_ref, dgk_ref, dgs_ref] + [dw_ref.at[g] for g in range(8)] + [db_ref.at[g] for g in range(8)]
        for ref, val in zip(refs, tot):
            _acc(ref, val, i == 0)

    return _rb_call("even_tok_bwd", body, row_in=(p, dq, dkv, dcat), pos_in=(cos, sin),
                    full_in=(gq, gk, gs, sgu_w, sgu_b, avg, masks), row_out=((EV_IN, BF),),
                    acc_out=((1, 512), (1, 128), (1, 512), (8, 128, 128), (8, 128, 1)))


MLA_SCALE = 96 ** -0.5
GQA_SCALE = 64 ** -0.5


def _odd_tok(cq, ckv, kr, za, zg, gq, gkv, wq, wkk, wkv, spread, cr, sr, ck, sk):
    cqn = _rmsn(cq, gq)
    q = _rope(_mm(cqn, wq), cr, sr) * MLA_SCALE
    ckn = _rmsn(ckv, gkv)
    k = _mm(ckn, wkk) + _mm(_rope(kr, ck, sk), spread)
    v = _mm(ckn, wkv)
    y = za * jax.nn.sigmoid(zg)
    return q, k, v, y


def odd_tok_fwd(p, cos, sin, gq, gkv, wq, wkk, wkv, spread):
    def body(i, p_ref, cos_ref, sin_ref, gq_ref, gkv_ref, wq_ref, wkk_ref, wkv_ref, sp_ref, q_ref, kv_ref, y_ref):
        q, k, v, y = _odd_tok(
            p_ref[:, 0:256].astype(F32), p_ref[:, 256:384].astype(F32), p_ref[:, 384:512].astype(F32),
            p_ref[:, 512:1024].astype(F32), p_ref[:, 1024:1536].astype(F32),
            gq_ref[...], gkv_ref[...], wq_ref[...], wkk_ref[...], wkv_ref[...], sp_ref[...],
            cos_ref[:, 0:768], sin_ref[:, 0:768], cos_ref[:, 768:896], sin_ref[:, 768:896])
        q_ref[...] = q.astype(BF)
        kv_ref[:, 0:768] = k.astype(BF)
        kv_ref[:, 768:1280] = v.astype(BF)
        y_ref[...] = y.astype(BF)

    return _rb_call("odd_tok_fwd", body, row_in=(p,), pos_in=(cos, sin), full_in=(gq, gkv, wq, wkk, wkv, spread),
                    row_out=((768, BF), (1280, BF), (512, BF)))


def odd_tok_bwd(p, dq, dkv, dy, cos, sin, gq, gkv, wq, wkk, wkv, spread):
    def body(i, p_ref, dq_ref, dkv_ref, dy_ref, cos_ref, sin_ref, gq_ref, gkv_ref, wq_ref, wkk_ref, wkv_ref, sp_ref,
             dp_ref, dgq_ref, dgkv_ref, dwq_ref, dwkk_ref, dwkv_ref):
        cr, sr, ck, sk = cos_ref[:, 0:768], sin_ref[:, 0:768], cos_ref[:, 768:896], sin_ref[:, 768:896]
        spread_v = sp_ref[...]

        def f(cq, ckv, kr, za, zg, gq, gkv, wq, wkk, wkv):
            return _odd_tok(cq, ckv, kr, za, zg, gq, gkv, wq, wkk, wkv, spread_v, cr, sr, ck, sk)

        _, vjp = jax.vjp(f, p_ref[:, 0:256].astype(F32), p_ref[:, 256:384].astype(F32),
                         p_ref[:, 384:512].astype(F32), p_ref[:, 512:1024].astype(F32),
                         p_ref[:, 1024:1536].astype(F32), gq_ref[...], gkv_ref[...], wq_ref[...],
                         wkk_ref[...], wkv_ref[...])
        d = vjp((dq_ref[...].astype(F32), dkv_ref[:, 0:768], dkv_ref[:, 768:1280], dy_ref[...].astype(F32)))
        dp_ref[:, 0:256] = d[0].astype(BF)
        dp_ref[:, 256:384] = d[1].astype(BF)
        dp_ref[:, 384:512] = d[2].astype(BF)
        dp_ref[:, 512:1024] = d[3].astype(BF)
        dp_ref[:, 1024:1536] = d[4].astype(BF)
        for ref, val in zip((dgq_ref, dgkv_ref, dwq_ref, dwkk_ref, dwkv_ref), d[5:]):
            _acc(ref, val, i == 0)

    return _rb_call("odd_tok_bwd", body, row_in=(p, dq, dkv, dy), pos_in=(cos, sin),
                    full_in=(gq, gkv, wq, wkk, wkv, spread), row_out=((OD_PAD, BF),),
                    acc_out=((1, 256), (1, 128), (256, 768), (128, 768), (128, 512)))


GQA_HEADS = [(64 * h, 64 * (h // 4), 64, 128 + 64 * (h // 4)) for h in range(8)]
MLA_HEADS = [(96 * h, 96 * h, 96, 768 + 64 * h) for h in range(8)]


def _by_block(j, run):
    @pl.when(j == 0)
    def _():
        run(LC)

    @pl.when(j > 0)
    def _():
        run(SEQ)


def attn_fwd(name, q, kv, heads):
    qw, kvw = q.shape[1], kv.shape[1]

    def kern(q_ref, kv_ref, o_ref, lse_ref):
        def run(nk):
            for h, (qo, ko, w, vo) in enumerate(heads):
                s = lax.dot_general(q_ref[:, qo:qo + w], kv_ref[0:nk, ko:ko + w], NT, preferred_element_type=F32)
                m = jnp.max(s, axis=-1, keepdims=True)
                p = jnp.exp(s - m)
                l = jnp.sum(p, axis=-1, keepdims=True)
                o = jnp.dot(p.astype(BF), kv_ref[0:nk, vo:vo + 64], preferred_element_type=F32) / l
                o_ref[:, 64 * h:64 * h + 64] = o.astype(BF)
                lse_ref[:, h:h + 1] = m + jnp.log(l)

        _by_block(pl.program_id(1), run)

    return pl.pallas_call(
        kern, grid=(NEX, BPE),
        in_specs=[pl.BlockSpec((TB, qw), lambda e, j: (e * BPE + j, 0)),
                  pl.BlockSpec((SEQ, kvw), lambda e, j: (e, 0))],
        out_specs=[pl.BlockSpec((TB, 512), lambda e, j: (e * BPE + j, 0)),
                   pl.BlockSpec((TB, 8), lambda e, j: (e * BPE + j, 0))],
        out_shape=[jax.ShapeDtypeStruct((R, 512), BF), jax.ShapeDtypeStruct((R, 8), F32)],
        compiler_params=_cparams(("parallel", "arbitrary")), name=name)(q, kv)


def attn_bwd(name, q, kv, o, dcat, lse, heads):
    qw, kvw = q.shape[1], kv.shape[1]

    def kern(q_ref, kv_ref, o_ref, do_ref, lse_ref, dq_ref, dkv_ref):
        j = pl.program_id(1)

        @pl.when(j == 0)
        def _():
            dkv_ref[...] = jnp.zeros_like(dkv_ref)

        def run(nk):
            for h, (qo, ko, w, vo) in enumerate(heads):
                qh = q_ref[:, qo:qo + w]
                kh = kv_ref[0:nk, ko:ko + w]
                s = lax.dot_general(qh, kh, NT, preferred_element_type=F32)
                p = jnp.exp(s - lse_ref[:, h:h + 1])
                do = do_ref[:, 64 * h:64 * h + 64]
                dsum = jnp.sum(do.astype(F32) * o_ref[:, 64 * h:64 * h + 64].astype(F32), axis=-1, keepdims=True)
                dp = lax.dot_general(do, kv_ref[0:nk, vo:vo + 64], NT, preferred_element_type=F32)
                ds = (p * (dp - dsum)).astype(BF)
                dkv_ref[0:nk, vo:vo + 64] += lax.dot_general(p.astype(BF), do, TN, preferred_element_type=F32)
                dq_ref[:, qo:qo + w] = jnp.dot(ds, kh, preferred_element_type=F32).astype(BF)
                dkv_ref[0:nk, ko:ko + w] += lax.dot_general(ds, qh, TN, preferred_element_type=F32)

        _by_block(j, run)

    return pl.pallas_call(
        kern, grid=(NEX, BPE),
        in_specs=[pl.BlockSpec((TB, qw), lambda e, j: (e * BPE + j, 0)),
                  pl.BlockSpec((SEQ, kvw), lambda e, j: (e, 0)),
                  pl.BlockSpec((TB, 512), lambda e, j: (e * BPE + j, 0)),
                  pl.BlockSpec((TB, 512), lambda e, j: (e * BPE + j, 0)),
                  pl.BlockSpec((TB, 8), lambda e, j: (e * BPE + j, 0))],
        out_specs=[pl.BlockSpec((TB, qw), lambda e, j: (e * BPE + j, 0)),
                   pl.BlockSpec((SEQ, kvw), lambda e, j: (e, 0))],
        out_shape=[jax.ShapeDtypeStruct((R, qw), BF), jax.ShapeDtypeStruct((R, kvw), F32)],
        compiler_params=_cparams(("parallel", "arbitrary")), name=name)(q, kv, o, dcat, lse)


HALO = 16
CONV_K = 31


def _fill_ext(ext_ref, prev_ref, cur_ref, next_ref, i):
    j = i % BPE
    has_prev = (j >= 2).astype(F32)
    has_next = jnp.logical_and(j >= 1, j <= BPE - 2).astype(F32)
    ext_ref[0:HALO, :] = prev_ref[TB - HALO:TB, :].astype(F32) * has_prev
    ext_ref[HALO:HALO + TB, :] = cur_ref[...].astype(F32)
    ext_ref[HALO + TB:2 * HALO + TB, :] = next_ref[0:HALO, :].astype(F32) * has_next


def _ln_silu(z, g, b):
    mu = jnp.mean(z, axis=-1, keepdims=True)
    zc = z - mu
    var = jnp.mean(zc * zc, axis=-1, keepdims=True)
    return _silu(zc * lax.rsqrt(var + EPS) * g + b)


def conf_fwd(y, cw, cb, lg, lb):
    def body(i, cur_ref, cw_ref, cb_ref, lg_ref, lb_ref, prev_ref, next_ref, z_ref, c_ref, ext_ref):
        _fill_ext(ext_ref, prev_ref, cur_ref, next_ref, i)
        acc = ext_ref[1:1 + TB, :] * cw_ref[0:1, :]
        for k in range(1, CONV_K):
            acc = acc + ext_ref[k + 1:k + 1 + TB, :] * cw_ref[k:k + 1, :]
        z = acc + cb_ref[...]
        z_ref[...] = z.astype(BF)
        c_ref[...] = _ln_silu(z, lg_ref[...], lb_ref[...]).astype(BF)

    return _rb_call("conf_fwd", body, row_in=(y,), full_in=(cw, cb, lg, lb), shift_in=((y, -1), (y, 1)),
                    row_out=((512, BF), (512, BF)), scratch=(pltpu.VMEM((TB + 2 * HALO, 512), F32),))


def conf_bwd_ln(z, dcat, lg, lb):
    def body(i, z_ref, dcat_ref, lg_ref, lb_ref, dz_ref, dlg_ref, dlb_ref, dcb_ref):
        _, vjp = jax.vjp(_ln_silu, z_ref[...].astype(F32), lg_ref[...], lb_ref[...])
        dz, dlg, dlb = vjp(dcat_ref[:, 512:1024].astype(F32))
        dz_ref[...] = dz.astype(BF)
        _acc(dlg_ref, dlg, i == 0)
        _acc(dlb_ref, dlb, i == 0)
        _acc(dcb_ref, jnp.sum(dz, axis=0, keepdims=True), i == 0)

    return _rb_call("conf_bwd_ln", body, row_in=(z, dcat), full_in=(lg, lb), row_out=((512, BF),),
                    acc_out=((1, 512), (1, 512), (1, 512)))


def conf_bwd_conv(y, dz, cw):
    def body(i, y_ref, dz_ref, cw_ref, yp_ref, yn_ref, dzp_ref, dzn_ref, dy_ref, dcw_ref, exty_ref, extd_ref):
        _fill_ext(exty_ref, yp_ref, y_ref, yn_ref, i)
        _fill_ext(extd_ref, dzp_ref, dz_ref, dzn_ref, i)
        dzv = dz_ref[...].astype(F32)
        @pl.when(i == 0)
        def _():
            dcw_ref[...] = jnp.zeros_like(dcw_ref)

        acc = None
        for k in range(CONV_K):
            t = extd_ref[CONV_K - k:CONV_K - k + TB, :] * cw_ref[k:k + 1, :]
            acc = t if acc is None else acc + t
            dcw_ref[k:k + 1, :] += jnp.sum(dzv * exty_ref[k + 1:k + 1 + TB, :], axis=0, keepdims=True)
        dy_ref[...] = acc.astype(BF)

    return _rb_call("conf_bwd_conv", body, row_in=(y, dz), full_in=(cw,),
                    shift_in=((y, -1), (y, 1), (dz, -1), (dz, 1)), row_out=((512, BF),), acc_out=((32, 512),),
                    scratch=(pltpu.VMEM((TB + 2 * HALO, 512), F32), pltpu.VMEM((TB + 2 * HALO, 512), F32)))


def final_loss(x, target, fg):
    lpb = L // TB

    def kern(x_ref, t_ref, g_ref, dx_ref, loss_ref, dg_ref):
        i = pl.program_id(0)
        lat = (i % BPE) >= 1
        xv, tv = x_ref[...], t_ref[...]

        def f(x, g):
            err = _rmsn(x, g) - tv
            rowsum = jnp.sum(err * err, axis=-1, keepdims=True)
            return jnp.sum(rowsum, axis=0, keepdims=True) * (0.5 / D)

        lv, vjp = jax.vjp(f, xv, g_ref[...])
        dx, dg = vjp(jnp.ones((1, 1), F32))
        m = lat.astype(F32)
        dx_ref[...] = dx * m
        _acc(loss_ref, jnp.zeros((8, 128), F32) + lv * m, i == 0)
        _acc(dg_ref, dg * m, i == 0)

    return pl.pallas_call(
        kern, grid=(NBLK,),
        in_specs=[pl.BlockSpec((TB, D), lambda i: (i, 0)),
                  pl.BlockSpec((TB, D), lambda i: ((i // BPE) * lpb + jnp.maximum(i % BPE - 1, 0), 0)),
                  pl.BlockSpec((1, D), lambda i: (0, 0))],
        out_specs=[pl.BlockSpec((TB, D), lambda i: (i, 0)), pl.BlockSpec((8, 128), lambda i: (0, 0)),
                   pl.BlockSpec((1, D), lambda i: (0, 0))],
        out_shape=[jax.ShapeDtypeStruct((R, D), F32), jax.ShapeDtypeStruct((8, 128), F32),
                   jax.ShapeDtypeStruct((1, D), F32)],
        compiler_params=_cparams(("arbitrary",)), name="final_loss")(x, target, fg)


NC = 24


def mods_fwd(call, ada_w, ada_b):
    cols = ada_w.shape[2]

    def kern(c_ref, w_ref, b_ref, o_ref):
        o_ref[...] = jnp.dot(_silu(c_ref[...]), w_ref[...], precision=HI, preferred_element_type=F32) + b_ref[...]

    return pl.pallas_call(
        kern, grid=(2,),
        in_specs=[pl.BlockSpec((NC, D), lambda l: (0, 0)), pl.BlockSpec((None, D, cols), lambda l: (l, 0, 0)),
                  pl.BlockSpec((None, 1, cols), lambda l: (l, 0, 0))],
        out_specs=pl.BlockSpec((None, NC, cols), lambda l: (l, 0, 0)),
        out_shape=jax.ShapeDtypeStruct((2, NC, cols), F32),
        compiler_params=_cparams(("parallel",)), name="mods_fwd")(call, ada_w, ada_b)


def ada_bwd(call, ada_w, dm):
    cols = ada_w.shape[2]

    def kern(c_ref, w_ref, dm_ref, gw_ref, dc_ref):
        l = pl.program_id(0)
        gw_ref[...] = lax.dot_general(_silu(c_ref[...]), dm_ref[...], TN, precision=HI, preferred_element_type=F32)
        part = lax.dot_general(dm_ref[16:24, :], w_ref[...], NT, precision=HI, preferred_element_type=F32)
        cc = c_ref[16:17, :]
        sg = jax.nn.sigmoid(cc)
        _acc(dc_ref, part * (sg * (1.0 + cc * (1.0 - sg))), l == 0)

    return pl.pallas_call(
        kern, grid=(2,),
        in_specs=[pl.BlockSpec((NC, D), lambda l: (0, 0)), pl.BlockSpec((None, D, cols), lambda l: (l, 0, 0)),
                  pl.BlockSpec((None, NC, cols), lambda l: (l, 0, 0))],
        out_specs=[pl.BlockSpec((None, D, cols), lambda l: (l, 0, 0)), pl.BlockSpec((8, D), lambda l: (0, 0))],
        out_shape=[jax.ShapeDtypeStruct((2, D, cols), F32), jax.ShapeDtypeStruct((8, D), F32)],
        compiler_params=_cparams(("arbitrary",)), name="ada_bwd")(call, ada_w, dm)


def sum_lead(name, a, scale_last=None):
    n, r, c = a.shape
    tr = r
    for cand in (512, 256, 128, 64, 32, 16, 8):
        if r % cand == 0 and cand * c * 4 * n <= 8 * 1024 * 1024:
            tr = cand
            break

    def kern(a_ref, o_ref):
        acc = a_ref[0].astype(F32)
        for k in range(1, n):
            acc = acc + a_ref[k].astype(F32)
        o_ref[...] = acc

    return pl.pallas_call(kern, grid=(r // tr,), in_specs=[pl.BlockSpec((n, tr, c), lambda i: (0, i, 0))],
                          out_specs=pl.BlockSpec((tr, c), lambda i: (i, 0)),
                          out_shape=jax.ShapeDtypeStruct((r, c), F32),
                          compiler_params=_cparams(("parallel",)), name=name)(a)


def add_pairs(name, a, b):
    _, r, c = a.shape

    def kern(a_ref, b_ref, o_ref):
        o_ref[...] = (a_ref[...].astype(F32) + b_ref[...].astype(F32)).astype(BF)

    spec = pl.BlockSpec((None, r, c), lambda i: (i, 0, 0))
    return pl.pallas_call(kern, grid=(4,), in_specs=[spec, spec], out_specs=spec,
                          out_shape=jax.ShapeDtypeStruct(a.shape, BF),
                          compiler_params=_cparams(("parallel",)), name=name)(a, b)


def adamw(name, w, g, m, v):
    r, c = w.shape
    tr = r
    for cand in (512, 256, 128, 64, 32, 16, 8):
        if r % cand == 0 and cand * c * 4 <= 2 * 1024 * 1024:
            tr = cand
            break
    c1 = 1.0 / (1.0 - ADAM_B1 ** ADAM_STEP)
    c2 = 1.0 / (1.0 - ADAM_B2 ** ADAM_STEP)

    def kern(w_ref, g_ref, m_ref, v_ref, d_ref, mo_ref, vo_ref):
        gv = g_ref[...]
        mn = ADAM_B1 * m_ref[...] + (1.0 - ADAM_B1) * gv
        vn = ADAM_B2 * v_ref[...] + (1.0 - ADAM_B2) * (gv * gv)
        d_ref[...] = -ADAM_LR * ((mn * c1) / (jnp.sqrt(vn * c2) + ADAM_EPS) + ADAM_WD * w_ref[...])
        mo_ref[...] = mn
        vo_ref[...] = vn

    spec = pl.BlockSpec((tr, c), lambda i: (i, 0))
    shp = jax.ShapeDtypeStruct((r, c), F32)
    return pl.pallas_call(kern, grid=(r // tr,), in_specs=[spec] * 4, out_specs=[spec] * 3, out_shape=[shp] * 3,
                          compiler_params=_cparams(("parallel",)), name=name)(w, g, m, v)


def all_gather8(name, xs):
    m_per, n = xs.shape

    def body(x_ref, out_ref, send_sems, recv_sems, local_sem):
        x, y, c = lax.axis_index("x"), lax.axis_index("y"), lax.axis_index("c")
        me, sibling = (x, y, c), (x, y, 1 - c)
        chips = [(1 - x, y), (x, 1 - y), (1 - x, 1 - y)]

        def rows(px, py, pc):
            return out_ref.at[pl.ds((4 * px + 2 * py + pc) * m_per, m_per), :]

        def copy(k, block, to, src=None):
            return pltpu.make_async_remote_copy(
                src_ref=rows(*block) if src is None else src, dst_ref=rows(*block),
                send_sem=send_sems.at[k], recv_sem=recv_sems.at[k], device_id=to, device_id_type=MESH)

        mine = pltpu.make_async_copy(x_ref, rows(*me), local_sem)
        mine.start()
        first = [copy(0, me, sibling, src=x_ref)]
        first += [copy(1 + j, me, (*chip, c), src=x_ref) for j, chip in enumerate(chips)]
        for cp in first:
            cp.start()
        passed = [copy(4 + j, (*chip, c), sibling) for j, chip in enumerate(chips)]
        for j, chip in enumerate(chips):
            copy(1 + j, (*chip, c), me).wait_recv()
            passed[j].start()
        copy(0, sibling, me).wait_recv()
        for j, chip in enumerate(chips):
            copy(4 + j, (*chip, 1 - c), me).wait_recv()
        for cp in first + passed:
            cp.wait_send()
        mine.wait()

    return pl.pallas_call(
        body, out_shape=jax.ShapeDtypeStruct((8 * m_per, n), xs.dtype),
        in_specs=[pl.BlockSpec(memory_space=pltpu.VMEM)], out_specs=pl.BlockSpec(memory_space=pltpu.VMEM),
        scratch_shapes=[pltpu.SemaphoreType.DMA((7,)), pltpu.SemaphoreType.DMA((7,)), pltpu.SemaphoreType.DMA],
        compiler_params=pltpu.CompilerParams(vmem_limit_bytes=VMEM_LIMIT), name=name)(xs)


def sibling_exchange(name, srcs):
    n = len(srcs)

    def body(*refs):
        src, dst = refs[:n], refs[n:2 * n]
        send_sems, recv_sems = refs[2 * n], refs[2 * n + 1]
        sibling = (lax.axis_index("x"), lax.axis_index("y"), 1 - lax.axis_index("c"))
        cps = [pltpu.make_async_remote_copy(src_ref=src[k], dst_ref=dst[k], send_sem=send_sems.at[k],
                                            recv_sem=recv_sems.at[k], device_id=sibling, device_id_type=MESH)
               for k in range(n)]
        for cp in cps:
            cp.start()
        for cp in cps:
            cp.wait_recv()
        for cp in cps:
            cp.wait_send()

    anyspec = pl.BlockSpec(memory_space=pl.ANY)
    return pl.pallas_call(
        body, out_shape=[jax.ShapeDtypeStruct(s.shape, s.dtype) for s in srcs],
        in_specs=[anyspec] * n, out_specs=[anyspec] * n,
        scratch_shapes=[pltpu.SemaphoreType.DMA((n,)), pltpu.SemaphoreType.DMA((n,))], name=name)(*srcs)


def chip_exchange(name, srcs, scatter):
    n = len(srcs)
    shapes = [s.shape[1:] if scatter else s.shape for s in srcs]

    def body(*refs):
        src, dst = refs[:n], refs[n:2 * n]
        send_sems, recv_sems, local_sems = refs[2 * n], refs[2 * n + 1], refs[2 * n + 2]
        x, y, c = lax.axis_index("x"), lax.axis_index("y"), lax.axis_index("c")
        me = 2 * x + y
        chips = [(1 - x, y), (x, 1 - y), (1 - x, 1 - y)]
        local, sends, recvs = [], [], []
        for a in range(n):
            own = src[a].at[me] if scatter else src[a]
            local.append(pltpu.make_async_copy(own, dst[a].at[me], local_sems.at[a]))
            for j, (px, py) in enumerate(chips):
                to = 2 * px + py
                sends.append(pltpu.make_async_remote_copy(
                    src_ref=src[a].at[to] if scatter else src[a], dst_ref=dst[a].at[me],
                    send_sem=send_sems.at[3 * a + j], recv_sem=recv_sems.at[3 * a + j],
                    device_id=(px, py, c), device_id_type=MESH))
                recvs.append(pltpu.make_async_remote_copy(
                    src_ref=own, dst_ref=dst[a].at[to],
                    send_sem=send_sems.at[3 * a + j], recv_sem=recv_sems.at[3 * a + j],
                    device_id=(px, py, c), device_id_type=MESH))
        for cp in local + sends:
            cp.start()
        for cp in recvs:
            cp.wait_recv()
        for cp in sends:
            cp.wait_send()
        for cp in local:
            cp.wait()

    anyspec = pl.BlockSpec(memory_space=pl.ANY)
    return pl.pallas_call(
        body, out_shape=[jax.ShapeDtypeStruct((4,) + tuple(shp), s.dtype) for shp, s in zip(shapes, srcs)],
        in_specs=[anyspec] * n, out_specs=[anyspec] * n,
        scratch_shapes=[pltpu.SemaphoreType.DMA((3 * n,)), pltpu.SemaphoreType.DMA((3 * n,)),
                        pltpu.SemaphoreType.DMA((n,))], name=name)(*srcs)


def _chip_copies(src, land, send_sems, recv_sems, scatter):
    x, y, c = lax.axis_index("x"), lax.axis_index("y"), lax.axis_index("c")
    me = 2 * x + y
    pairs = []
    for a in range(len(src)):
        for j, (px, py) in enumerate([(1 - x, y), (x, 1 - y), (1 - x, 1 - y)]):
            to = 2 * px + py
            out = src[a].at[to] if scatter else src[a]
            kw = dict(send_sem=send_sems.at[3 * a + j], recv_sem=recv_sems.at[3 * a + j], device_id=(px, py, c),
                      device_id_type=MESH)
            pairs.append((pltpu.make_async_remote_copy(src_ref=out, dst_ref=land[a].at[me], **kw),
                          pltpu.make_async_remote_copy(src_ref=out, dst_ref=land[a].at[to], **kw)))
    return pairs


_HBM = pl.BlockSpec(memory_space=pltpu.HBM)
_SEM = pl.BlockSpec(memory_space=pltpu.SEMAPHORE)


def chip_exchange_start(name, groups, scatter):
    sizes = [len(s) for s, _ in groups]
    flat = [a for s, l in groups for a in list(s) + list(l)]
    ng = len(groups)

    def body(*refs):
        ins, outs = refs[:len(flat)], refs[len(flat):]
        off = 0
        for g, n in enumerate(sizes):
            src, land = ins[off:off + n], ins[off + n:off + 2 * n]
            off += 2 * n
            for send, _ in _chip_copies(src, land, outs[2 * g], outs[2 * g + 1], scatter):
                send.start()
        outs[-1][...] = jnp.zeros_like(outs[-1])

    out_shape = []
    for n in sizes:
        out_shape += [pltpu.SemaphoreType.DMA((3 * n,)), pltpu.SemaphoreType.DMA((3 * n,))]
    out_shape += [pltpu.HBM(a.shape, a.dtype) for a in flat] + [jax.ShapeDtypeStruct((8, 128), F32)]
    res = pl.pallas_call(
        body, out_shape=tuple(out_shape), in_specs=[_HBM] * len(flat),
        out_specs=tuple([_SEM] * (2 * ng) + [_HBM] * len(flat) + [pl.BlockSpec(memory_space=pltpu.VMEM)]),
        input_output_aliases={k: 2 * ng + k for k in range(len(flat))},
        compiler_params=pltpu.CompilerParams(has_side_effects=pltpu.SideEffectType.DATAFLOW_SIDE_EFFECTING),
        name=name)(*[pltpu.with_memory_space_constraint(a, pltpu.HBM) for a in flat])
    handles, off = [], 2 * ng
    for g, n in enumerate(sizes):
        handles.append((res[2 * g], res[2 * g + 1], list(res[off:off + n]), list(res[off + n:off + 2 * n])))
        off += 2 * n
    return handles, res[-1]


def chip_exchange_wait(name, handle, after, scatter):
    send_sems, recv_sems, srcs, lands = handle
    n = len(srcs)

    def body(*refs):
        src, land = refs[:n], refs[n:2 * n]
        for send, recv in _chip_copies(src, land, refs[2 * n], refs[2 * n + 1], scatter):
            send.wait_send()
            recv.wait_recv()

    res = pl.pallas_call(
        body, out_shape=tuple(pltpu.HBM(a.shape, a.dtype) for a in srcs + lands),
        in_specs=[_HBM] * (2 * n) + [_SEM, _SEM, pl.BlockSpec(memory_space=pl.ANY)],
        out_specs=tuple([_HBM] * (2 * n)), input_output_aliases={k: k for k in range(2 * n)},
        compiler_params=pltpu.CompilerParams(has_side_effects=pltpu.SideEffectType.DATAFLOW_SIDE_EFFECTING),
        name=name)(*srcs, *lands, send_sems, recv_sems, after)
    return list(res[n:])


def _own_slab(slab, chip):
    return lax.dynamic_update_index_in_dim(jnp.zeros((4,) + slab.shape, slab.dtype), slab, chip, 0)


def _rope_tables(d_rot, reps):
    rows = L // GRID_W
    row = np.repeat(np.arange(rows), GRID_W).astype(np.float32)
    col = np.tile(np.arange(GRID_W), rows).astype(np.float32)
    d_axis = d_rot // 2
    inv = (ROPE_THETA ** (-np.arange(0, d_axis, 2, dtype=np.float32) / d_axis)).astype(np.float32)
    ang = jnp.asarray(np.concatenate([row[:, None] * inv, col[:, None] * inv], axis=-1))
    cos, sin = jnp.cos(ang), jnp.sin(ang)
    c = jnp.repeat(cos, 2, axis=-1)
    s = jnp.stack([-sin, sin], axis=-1).reshape(L, d_rot)
    c = jnp.concatenate([jnp.ones((LC, d_rot), F32), c], axis=0)
    s = jnp.concatenate([jnp.zeros((LC, d_rot), F32), s], axis=0)
    return jnp.tile(c, (1, reps)), jnp.tile(s, (1, reps))


def _group_consts():
    g = np.arange(512) // 64
    avg = (g[:, None] == g[None, :]).astype(np.float32) / 64.0
    masks = (np.arange(8)[:, None] == g[None, :]).astype(np.float32).reshape(8, 1, 512)
    return jnp.asarray(avg), jnp.asarray(masks)


def _pack(items):
    flat = jnp.concatenate([a.reshape(-1).astype(F32) for a in items])
    n = flat.shape[0]
    rows = -(-n // D)
    rows = -(-rows // 8) * 8
    return jnp.pad(flat, (0, rows * D - n)).reshape(rows, D)


def _unpack(buf, shapes):
    lead = buf.shape[:-2]
    flat = buf.reshape(lead + (-1,))
    out, off = [], 0
    for shp in shapes:
        n = int(np.prod(shp))
        out.append(flat[..., off:off + n].reshape(lead + tuple(shp)))
        off += n
    return out


def _to_hs(g, rows_sharded):
    k, n = g.shape
    if rows_sharded:
        return g.reshape(4, 2, k // 8, n).transpose(1, 0, 2, 3)
    return g.reshape(2, k // 2, 4, n // 4).transpose(0, 2, 1, 3)


def _arrive(prm, key, after):
    if callable(prm[key]):
        prm[key](after)
    return prm[key]


def _layer_fwd(i, x, mods, prm, consts):
    sv = {}
    sv["x0"] = x
    h = modnorm_fwd(f"norm1_fwd_{i}", x, mods, prm["norm1_g"], 0, 1)
    sv["h"] = h
    p = proj_in(f"proj_in_{i}", h, _arrive(prm, "w_in", h))
    sv["p"] = p
    if i == 0:
        q, kv, m2 = even_tok_fwd(p, consts["cos_e"], consts["sin_e"], prm["gq"], prm["gk"], prm["gs"],
                                 prm["sgu_w"], prm["sgu_b"], consts["avg"], consts["masks"])
        o, lse = attn_fwd("attn_fwd_0", q, kv, GQA_HEADS)
        sv.update(q=q, kv=kv)
    else:
        q, kv, y = odd_tok_fwd(p, consts["cos_o"], consts["sin_o"], prm["gq"], prm["gkv"], prm["wq"], prm["wkk"],
                               prm["wkv"], consts["spread"])
        o, lse = attn_fwd("attn_fwd_1", q, kv, MLA_HEADS)
        z, m2 = conf_fwd(y, prm["conv_w"], prm["conv_b"], prm["ln_g"], prm["ln_b"])
        sv.update(q=q, kv=kv, y=y, z=z)
    sv.update(o=o, lse=lse, m2=m2)
    x1, y1 = proj_out(f"proj_out_{i}", o, m2, _arrive(prm, "w_out", o), x, mods, 2)
    sv.update(x1=x1, y1=y1)
    h2 = modnorm_fwd(f"norm2_fwd_{i}", x1, mods, prm["norm2_g"], 3, 4)
    a, f = mlp_up(f"mlp_up_{i}", h2, prm["w1"])
    x2, y2 = mlp_down(f"mlp_down_{i}", f, prm["w2"], x1, mods, 5)
    sv.update(h2=h2, a=a, f=f, y2=y2)
    return x2, sv


def _layer_bwd(i, dx, sv, mods, prm, consts, on_grads=None):
    gr = {}
    dy2, dg2 = gate_bwd(f"gate2_bwd_{i}", dx, sv["y2"], mods, 5)
    da = mlp_bwd_da(f"mlp_bwd_da_{i}", dy2, prm["w2"], sv["a"])
    tiles8 = [(h, j) for h in range(2) for j in range(4)]
    gr["w1"] = mm_tn(f"grad_w1_{i}", sv["h2"], da, tiles8, 512, D).reshape(2, 4, 512, D)
    gr["w2"] = mm_tn(f"grad_w2_{i}", sv["f"], dy2, [(2 * j + h, 0) for h in range(2) for j in range(4)],
                     512, D).reshape(2, 4, 512, D)
    dh2 = mlp_bwd_dh(f"mlp_bwd_dh_{i}", da, prm["w1"])
    if on_grads is not None:
        on_grads(i, "mlp", gr, dh2)
    dx1, dsh2, dsc2, gr["norm2_g"] = modnorm_bwd(f"norm2_bwd_{i}", sv["x1"], dh2, dx, mods, prm["norm2_g"], 3, 4)
    dy1, dg1 = gate_bwd(f"gate1_bwd_{i}", dx1, sv["y1"], mods, 2)
    dcat = mm_nt(f"proj_out_bwd_{i}", dy1, prm["w_out"])
    t4 = [(2 * j + h, 0) for h in range(2) for j in range(2)]
    go = mm_tn(f"grad_wout_a_{i}", sv["o"], dy1, t4, 128, D).reshape(2, 2, 128, D)
    gm = mm_tn(f"grad_wout_b_{i}", sv["m2"], dy1, t4, 128, D).reshape(2, 2, 128, D)
    gr["w_out"] = jnp.concatenate([go, gm], axis=1)
    if i == 0:
        dq, dkv = attn_bwd("attn_bwd_0", sv["q"], sv["kv"], sv["o"], dcat, sv["lse"], GQA_HEADS)
        dp, gr["gq"], gr["gk"], gr["gs"], gr["sgu_w"], gr["sgu_b"] = even_tok_bwd(
            sv["p"], dq, dkv, dcat, consts["cos_e"], consts["sin_e"], prm["gq"], prm["gk"],
            prm["gs"], prm["sgu_w"], prm["sgu_b"], consts["avg"], consts["masks"])
    else:
        dq, dkv = attn_bwd("attn_bwd_1", sv["q"], sv["kv"], sv["o"], dcat, sv["lse"], MLA_HEADS)
        dz, gr["ln_g"], gr["ln_b"], gr["conv_b"] = conf_bwd_ln(sv["z"], dcat, prm["ln_g"], prm["ln_b"])
        dyc, gr["conv_w"] = conf_bwd_conv(sv["y"], dz, prm["conv_w"])
        dp, gr["gq"], gr["gkv"], gr["wq"], gr["wkk"], gr["wkv"] = odd_tok_bwd(
            sv["p"], dq, dkv, dyc, consts["cos_o"], consts["sin_o"], prm["gq"], prm["gkv"], prm["wq"], prm["wkk"],
            prm["wkv"], consts["spread"])
    n_in = prm["w_in"].shape[1]
    gr["w_in"] = mm_tn(f"grad_win_{i}", sv["h"], dp, [(0, 0), (1, 0)], 512, n_in)
    dh = mm_nt(f"proj_in_bwd_{i}", dp, prm["w_in"])
    dx0, dsh1, dsc1, gr["norm1_g"] = modnorm_bwd(f"norm1_bwd_{i}", sv["x0"], dh, dx1, mods, prm["norm1_g"], 0, 1)
    dmods = jnp.concatenate([dsh1, dsc1, dg1, dsh2, dsc2, dg2], axis=1)
    return dx0, dmods, gr


def local_step(xcat, target, mods, prms, final_g, on_grads=None):
    avg, masks = _group_consts()
    cos_e, sin_e = _rope_tables(64, 8)
    ck, sk = _rope_tables(32, 1)
    one64, zero64 = jnp.ones((SEQ, 64), F32), jnp.zeros((SEQ, 64), F32)
    one96, zero96 = jnp.ones((SEQ, 96), F32), jnp.zeros((SEQ, 96), F32)
    cos_o = jnp.concatenate([jnp.tile(jnp.concatenate([one64, ck], axis=1), (1, 8)), ck, one96], axis=1)
    sin_o = jnp.concatenate([jnp.tile(jnp.concatenate([zero64, sk], axis=1), (1, 8)), sk, zero96], axis=1)
    lane = np.arange(768)
    spread = np.zeros((128, 768), np.float32)
    spread[lane % 96 - 64, lane] = (lane % 96 >= 64)
    consts = dict(avg=avg, masks=masks, cos_e=cos_e, sin_e=sin_e, cos_o=cos_o, sin_o=sin_o,
                  spread=jnp.asarray(spread, BF))
    x = xcat
    saved = []
    for i in range(2):
        x, sv = _layer_fwd(i, x, mods[i], prms[i], consts)
        saved.append(sv)
    dx, loss, dfg = final_loss(x, target, final_g)
    dmods, grads = [None, None], [None, None]
    for i in (1, 0):
        dx, dmods[i], grads[i] = _layer_bwd(i, dx, saved[i], mods[i], prms[i], consts, on_grads)
        if on_grads is not None:
            on_grads(i, "rest", grads[i], dx)
    return loss, dx, dmods, grads, dfg


def _row(v):
    return v.reshape(1, -1).astype(F32)


def odd_in_params(od_w_in, w_uq, w_ukv):
    od = jnp.concatenate([od_w_in[:, 0:416], jnp.zeros((D, 96), od_w_in.dtype), od_w_in[:, 416:OD_IN]], axis=1)
    ukv = w_ukv.reshape(128, 8, 128)
    wkk = jnp.pad(ukv[:, :, :64], ((0, 0), (0, 0), (0, 32))).reshape(128, 768)
    return dict(w_in=od, wq=w_uq, wkk=wkk, wkv=ukv[:, :, 64:].reshape(128, 512))


def small_params(small):
    p0 = dict(norm1_g=_row(small["norm1_g"][0]), norm2_g=_row(small["norm2_g"][0]),
              gq=jnp.tile(_row(small["ev_q_norm_g"]), (1, 8)), gk=jnp.tile(_row(small["ev_k_norm_g"]), (1, 2)),
              gs=_row(small["ev_sgu_norm_g"]), sgu_w=small["ev_sgu_w"].reshape(8, 128, 128).astype(F32),
              sgu_b=small["ev_sgu_b"].reshape(8, 128, 1).astype(F32))
    p1 = dict(norm1_g=_row(small["norm1_g"][1]), norm2_g=_row(small["norm2_g"][1]),
              gq=_row(small["od_q_norm_g"]), gkv=_row(small["od_kv_norm_g"]),
              conv_w=jnp.pad(small["od_conv_w"].reshape(CONV_K, 512).astype(F32), ((0, 1), (0, 0))),
              conv_b=_row(small["od_conv_b"]), ln_g=_row(small["od_ln_g"]), ln_b=_row(small["od_ln_b"]))
    return [p0, p1]


def prep_params(ev_w_in, od_w_in, w_out, w1, w2, w_uq, w_ukv, small):
    p0, p1 = small_params(small)
    p0.update(w_in=ev_w_in, w_out=w_out[0], w1=w1[0], w2=w2[0])
    p1.update(odd_in_params(od_w_in, w_uq, w_ukv), w_out=w_out[1], w1=w1[1], w2=w2[1])
    return [p0, p1]


def small_grads_natural(grads, dfg):
    g0, g1 = grads
    return dict(
        norm1_g=jnp.concatenate([g0["norm1_g"], g1["norm1_g"]], axis=0),
        norm2_g=jnp.concatenate([g0["norm2_g"], g1["norm2_g"]], axis=0),
        ev_q_norm_g=g0["gq"].reshape(8, 64).sum(0).reshape(1, 64),
        ev_k_norm_g=g0["gk"].reshape(2, 64).sum(0).reshape(1, 64),
        ev_sgu_norm_g=g0["gs"].reshape(1, 8, 64),
        ev_sgu_w=g0["sgu_w"].reshape(1, 8, 128, 128),
        ev_sgu_b=g0["sgu_b"].reshape(1, 8, 128),
        od_q_norm_g=g1["gq"].reshape(1, 256),
        od_kv_norm_g=g1["gkv"].reshape(1, 128),
        od_conv_w=g1["conv_w"][0:CONV_K].reshape(1, CONV_K, 512),
        od_conv_b=g1["conv_b"].reshape(1, 512),
        od_ln_g=g1["ln_g"].reshape(1, 512),
        od_ln_b=g1["ln_b"].reshape(1, 512),
        final_g=dfg.reshape(D))


def layer_grads_hs(i, g, part="all"):
    def cols(a):
        k, n = a.shape
        return a.reshape(2, k // 2, 4, n // 4).transpose(0, 2, 1, 3).astype(BF)

    mlp = [(("mlp_w1", i), g["w1"]), (("mlp_w2", i), g["w2"])]
    if part == "mlp":
        return mlp
    rest = [(("w_out", i), g["w_out"])]
    if i == 0:
        rest.append((("ev_w_in", 0), cols(g["w_in"].reshape(D, EV_IN))))
    else:
        od = g["w_in"].reshape(D, OD_PAD)
        od = jnp.concatenate([od[:, 0:416], od[:, 512:OD_PAD]], axis=1)
        ukv = jnp.concatenate([g["wkk"].reshape(128, 8, 96)[:, :, :64], g["wkv"].reshape(128, 8, 64)], axis=2)
        rest += [(("od_w_in", 0), cols(od)), (("od_w_uq", 0), cols(g["wq"])),
                 (("od_w_ukv", 0), cols(ukv.reshape(128, 1024)))]
    return rest if part == "rest" else mlp + rest


def big_grads_hs(grads):
    d = dict(layer_grads_hs(0, grads[0]) + layer_grads_hs(1, grads[1]))
    return dict(ev_w_in=d[("ev_w_in", 0)], od_w_in=d[("od_w_in", 0)], od_w_uq=d[("od_w_uq", 0)],
                od_w_ukv=d[("od_w_ukv", 0)], w_out=[d[("w_out", 0)], d[("w_out", 1)]],
                mlp_w1=[d[("mlp_w1", 0)], d[("mlp_w1", 1)]], mlp_w2=[d[("mlp_w2", 0)], d[("mlp_w2", 1)]])


def grads_to_natural(grads, dfg):
    out = small_grads_natural(grads, dfg)
    hs = big_grads_hs(grads)

    def from_cols(a):
        return a.transpose(0, 2, 1, 3).reshape(2 * a.shape[2], 4 * a.shape[3])

    def from_rows(a):
        return a.transpose(1, 0, 2, 3).reshape(8 * a.shape[2], a.shape[3])

    out["ev_w_in"] = from_cols(hs["ev_w_in"])[None]
    out["od_w_in"] = from_cols(hs["od_w_in"])[None]
    out["od_w_uq"] = from_cols(hs["od_w_uq"])[None]
    out["od_w_ukv"] = from_cols(hs["od_w_ukv"])[None]
    out["w_out"] = jnp.stack([from_rows(a) for a in hs["w_out"]])
    out["mlp_w1"] = jnp.stack([from_cols(a) for a in hs["mlp_w1"]])
    out["mlp_w2"] = jnp.stack([from_rows(a) for a in hs["mlp_w2"]])
    return out


WEIGHT_NAMES = ['c_ctx', 'ada_w', 'ada_b', 'norm1_g', 'norm2_g', 'w_out', 'mlp_w1', 'mlp_w2', 'ev_w_in',
                'ev_q_norm_g', 'ev_k_norm_g', 'ev_sgu_norm_g', 'ev_sgu_w', 'ev_sgu_b', 'od_w_in', 'od_q_norm_g',
                'od_kv_norm_g', 'od_w_uq', 'od_w_ukv', 'od_conv_w', 'od_conv_b', 'od_ln_g', 'od_ln_b', 'final_g']
REPL_SMALL = ['norm1_g', 'norm2_g', 'ev_q_norm_g', 'ev_k_norm_g', 'ev_sgu_norm_g', 'ev_sgu_w', 'ev_sgu_b',
              'od_kv_norm_g', 'final_g']
SHARD_SMALL = ['od_q_norm_g', 'od_conv_w', 'od_conv_b', 'od_ln_g', 'od_ln_b']
BIG = ['w_out', 'mlp_w1', 'mlp_w2', 'ev_w_in', 'od_w_in', 'od_w_uq', 'od_w_ukv']


def _gather_last(parts):
    return jnp.concatenate([parts[k] for k in range(4)], axis=-1)


def _reduce_begin(tag, hs_list, ci, chip):
    mine = [lax.dynamic_index_in_dim(a, ci, 0, keepdims=False) for a in hs_list]
    other = [lax.dynamic_index_in_dim(a, 1 - ci, 0, keepdims=False) for a in hs_list]
    got = sibling_exchange(f"rs_sibling_in_{tag}", other)
    pair = [add_pairs(f"rs_add_{tag}_{k}", a, b) for k, (a, b) in enumerate(zip(mine, got))]
    lands = [_own_slab(lax.dynamic_index_in_dim(p, chip, 0, keepdims=False), chip) for p in pair]
    (handle,), token = chip_exchange_start(f"rs_chips_start_{tag}", [(pair, lands)], scatter=True)
    return handle, token


def _reduce_end(tag, handle, after):
    land = chip_exchange_wait(f"rs_chips_wait_{tag}", handle, after, scatter=True)
    return [sum_lead(f"rs_sum_{tag}_{k}", a) for k, a in enumerate(land)]


def _reduce_finish(halves, ci):
    sib = sibling_exchange("rs_sibling_out", halves)
    return [jnp.where(ci == 0, jnp.concatenate([h, s], axis=0), jnp.concatenate([s, h], axis=0))
            for h, s in zip(halves, sib)]


def kernel(x, c, ctx, c_ctx, ada_w, ada_b, norm1_g, norm2_g, w_out, mlp_w1, mlp_w2, ev_w_in, ev_q_norm_g, ev_k_norm_g, ev_sgu_norm_g, ev_sgu_w, ev_sgu_b, od_w_in, od_q_norm_g, od_kv_norm_g, od_w_uq, od_w_ukv, od_conv_w, od_conv_b, od_ln_g, od_ln_b, final_g, loss_target, m_c_ctx, m_ada_w, m_ada_b, m_norm1_g, m_norm2_g, m_w_out, m_mlp_w1, m_mlp_w2, m_ev_w_in, m_ev_q_norm_g, m_ev_k_norm_g, m_ev_sgu_norm_g, m_ev_sgu_w, m_ev_sgu_b, m_od_w_in, m_od_q_norm_g, m_od_kv_norm_g, m_od_w_uq, m_od_w_ukv, m_od_conv_w, m_od_conv_b, m_od_ln_g, m_od_ln_b, m_final_g, v_c_ctx, v_ada_w, v_ada_b, v_norm1_g, v_norm2_g, v_w_out, v_mlp_w1, v_mlp_w2, v_ev_w_in, v_ev_q_norm_g, v_ev_k_norm_g, v_ev_sgu_norm_g, v_ev_sgu_w, v_ev_sgu_b, v_od_w_in, v_od_q_norm_g, v_od_kv_norm_g, v_od_w_uq, v_od_w_ukv, v_od_conv_w, v_od_conv_b, v_od_ln_g, v_od_ln_b, v_final_g):
    w = dict(c_ctx=c_ctx, ada_w=ada_w, ada_b=ada_b, norm1_g=norm1_g, norm2_g=norm2_g, w_out=w_out, mlp_w1=mlp_w1,
             mlp_w2=mlp_w2, ev_w_in=ev_w_in, ev_q_norm_g=ev_q_norm_g, ev_k_norm_g=ev_k_norm_g,
             ev_sgu_norm_g=ev_sgu_norm_g, ev_sgu_w=ev_sgu_w, ev_sgu_b=ev_sgu_b, od_w_in=od_w_in,
             od_q_norm_g=od_q_norm_g, od_kv_norm_g=od_kv_norm_g, od_w_uq=od_w_uq, od_w_ukv=od_w_ukv,
             od_conv_w=od_conv_w, od_conv_b=od_conv_b, od_ln_g=od_ln_g, od_ln_b=od_ln_b, final_g=final_g)
    mom = dict(c_ctx=m_c_ctx, ada_w=m_ada_w, ada_b=m_ada_b, norm1_g=m_norm1_g, norm2_g=m_norm2_g, w_out=m_w_out,
               mlp_w1=m_mlp_w1, mlp_w2=m_mlp_w2, ev_w_in=m_ev_w_in, ev_q_norm_g=m_ev_q_norm_g,
               ev_k_norm_g=m_ev_k_norm_g, ev_sgu_norm_g=m_ev_sgu_norm_g, ev_sgu_w=m_ev_sgu_w, ev_sgu_b=m_ev_sgu_b,
               od_w_in=m_od_w_in, od_q_norm_g=m_od_q_norm_g, od_kv_norm_g=m_od_kv_norm_g, od_w_uq=m_od_w_uq,
               od_w_ukv=m_od_w_ukv, od_conv_w=m_od_conv_w, od_conv_b=m_od_conv_b, od_ln_g=m_od_ln_g,
               od_ln_b=m_od_ln_b, final_g=m_final_g)
    var = dict(c_ctx=v_c_ctx, ada_w=v_ada_w, ada_b=v_ada_b, norm1_g=v_norm1_g, norm2_g=v_norm2_g, w_out=v_w_out,
               mlp_w1=v_mlp_w1, mlp_w2=v_mlp_w2, ev_w_in=v_ev_w_in, ev_q_norm_g=v_ev_q_norm_g,
               ev_k_norm_g=v_ev_k_norm_g, ev_sgu_norm_g=v_ev_sgu_norm_g, ev_sgu_w=v_ev_sgu_w, ev_sgu_b=v_ev_sgu_b,
               od_w_in=v_od_w_in, od_q_norm_g=v_od_q_norm_g, od_kv_norm_g=v_od_kv_norm_g, od_w_uq=v_od_w_uq,
               od_w_ukv=v_od_w_ukv, od_conv_w=v_od_conv_w, od_conv_b=v_od_conv_b, od_ln_g=v_od_ln_g,
               od_ln_b=v_od_ln_b, final_g=v_final_g)
    xi, yi, ci = lax.axis_index("x"), lax.axis_index("y"), lax.axis_index("c")
    chip = 2 * xi + yi
    dev = 2 * chip + ci

    shard_shapes = [w[n].shape for n in SHARD_SMALL]
    g0 = all_gather8("ag_small", _pack([c] + [w[n] for n in SHARD_SMALL]))
    g0 = g0.reshape(8, -1, D)
    parts = _unpack(g0, [c.shape] + shard_shapes)
    c_all = parts[0].reshape(16, D)
    small_full = {n: _gather_last(p[0::2]) for n, p in zip(SHARD_SMALL, parts[1:])}
    call = jnp.concatenate([c_all, c_ctx.reshape(1, D), jnp.zeros((NC - 17, D), F32)], axis=0)

    cols = ada_w.shape[2]
    ada_b_sh = lax.dynamic_slice(ada_b, (0, chip * cols), (2, cols)).reshape(2, 1, cols)
    mt = mods_fwd(call, ada_w, ada_b_sh)
    mt = all_gather8("ag_mods", mt.reshape(2 * NC, cols)).reshape(8, 2, NC, cols)
    table = mt[0::2].transpose(1, 2, 0, 3).reshape(2, NC, 4 * cols)
    mods = []
    for i in range(2):
        lat = lax.dynamic_slice(table[i], (2 * dev, 0), (2, 4 * cols))
        mc = table[i, 16]
        mods.append(jnp.stack([mc, lat[0], mc, lat[1]]).reshape(4 * N_MOD, 1, D))

    order = [[("ev_w_in", 0)], [("w_out", 0), ("mlp_w1", 0), ("mlp_w2", 0)],
             [("od_w_in", 0), ("od_w_uq", 0), ("od_w_ukv", 0), ("w_out", 1), ("mlp_w1", 1), ("mlp_w2", 1)]]
    groups = []
    for names in order:
        srcs = [w[n][i].astype(BF) for n, i in names]
        groups.append((srcs, [_own_slab(s, chip) for s in srcs]))
    groups[0][1][0], table = lax.optimization_barrier((groups[0][1][0], table))
    handles, token = chip_exchange_start("gather_start", groups, scatter=False)
    mods[0] = mods[0] + token[0, 0]
    small = {n: w[n] for n in REPL_SMALL}
    small.update(small_full)
    prms = small_params(small)

    def arrive_ev_in(after):
        (ev,) = chip_exchange_wait("gather_wait_0", handles[0], after, scatter=False)
        prms[0]["w_in"] = _gather_last(ev)

    def arrive_ev_rest(after):
        wo, w1, w2 = chip_exchange_wait("gather_wait_1", handles[1], after, scatter=False)
        prms[0].update(w_out=wo.reshape(D, D), w1=w1, w2=w2)

    def arrive_od(after):
        od, uq, ukv, wo, w1, w2 = chip_exchange_wait("gather_wait_2", handles[2], after, scatter=False)
        prms[1].update(odd_in_params(_gather_last(od), _gather_last(uq), _gather_last(ukv)),
                       w_out=wo.reshape(D, D), w1=w1, w2=w2)

    prms[0]["w_in"] = arrive_ev_in
    prms[0]["w_out"] = arrive_ev_rest
    prms[1]["w_in"] = arrive_od

    in_flight = []

    def on_grads(i, part, g, after):
        sel = {(1, "rest"): "all", (0, "mlp"): "mlp", (0, "rest"): "rest"}.get((i, part))
        if sel is not None:
            names, arrs = zip(*layer_grads_hs(i, g, sel))
            handle, token = _reduce_begin(f"l{i}_{sel}", list(arrs), ci, chip)
            in_flight.append((f"l{i}_{sel}", names, handle, token))

    xcat = jnp.concatenate([ctx, x], axis=1).reshape(R, D)
    loss_p, dx, dmods, grads, dfg = local_step(xcat, loss_target.reshape(NEX * L, D), mods, prms,
                                               final_g.reshape(1, D), on_grads)
    grad_x = dx.reshape(NEX, SEQ, D)[:, LC:]
    names, halves = (), []
    for tag, nm, handle, token in in_flight:
        names += nm
        halves += _reduce_end(tag, handle, token if tag == "l0_rest" else dx)
    red = dict(zip(names, _reduce_finish(halves, ci)))

    sg = small_grads_natural(grads, dfg)
    dm = jnp.stack([d.reshape(4, N_MOD * D) for d in dmods])
    small_names = REPL_SMALL + SHARD_SMALL
    items = [dm[:, 1::2], dm[:, 0] + dm[:, 2]] + [sg[n] for n in small_names] + [loss_p[0:1, 0:1]]
    shapes = [a.shape for a in items]
    g1 = all_gather8("ag_grads", _pack(items))
    rows1 = g1.shape[0] // 8
    g1 = g1.reshape(8, rows1, D)
    tot = _unpack(sum_lead("sum_small", g1), shapes)
    dm_lat = _unpack(g1, shapes[:1])[0]
    dm_lat = dm_lat.transpose(1, 0, 2, 3).reshape(2, 16, N_MOD * D)
    dm_all = jnp.concatenate([dm_lat, tot[1][:, None], jnp.zeros((2, NC - 17, N_MOD * D), F32)], axis=1)
    gsum = dict(zip(small_names, tot[2:2 + len(small_names)]))
    loss = tot[-1].reshape(())
    grad = {n: gsum[n].reshape(w[n].shape) for n in REPL_SMALL}
    for n in SHARD_SMALL:
        k = w[n].shape[-1]
        grad[n] = lax.dynamic_slice_in_dim(gsum[n], chip * k, k, axis=gsum[n].ndim - 1)
    grad["ada_b"] = sum_lead("sum_ada_b", dm_all.transpose(1, 0, 2).reshape(NC, 2 * N_MOD, D)).reshape(2, N_MOD * D)

    dm_sh = lax.dynamic_slice(dm_all, (0, 0, chip * cols), (2, NC, cols))
    grad["ada_w"], dcc = ada_bwd(call, ada_w, dm_sh)
    dcc = all_gather8("ag_cctx", dcc).reshape(8, 8, D)
    grad["c_ctx"] = sum_lead("sum_cctx", dcc[0::2])[0]

    for n in BIG:
        grad[n] = jnp.stack([red[(n, i)] for i in range(w[n].shape[0])]).reshape(w[n].shape)

    delta, new_m, new_v = {}, {}, {}
    for n in ['ada_w'] + BIG:
        shp = w[n].shape
        two_d = (shp[0] * shp[1], shp[2])
        d_, m_, v_ = adamw(f"adamw_{n}", w[n].reshape(two_d), grad[n].reshape(two_d), mom[n].reshape(two_d),
                           var[n].reshape(two_d))
        delta[n], new_m[n], new_v[n] = d_.reshape(shp), m_.reshape(shp), v_.reshape(shp)
    rest = [n for n in WEIGHT_NAMES if n not in ['ada_w'] + BIG]
    rshapes = [w[n].shape for n in rest]
    d_, m_, v_ = adamw("adamw_small", _pack([w[n] for n in rest]), _pack([grad[n] for n in rest]),
                       _pack([mom[n] for n in rest]), _pack([var[n] for n in rest]))
    for dst, buf in ((delta, d_), (new_m, m_), (new_v, v_)):
        dst.update(zip(rest, _unpack(buf, rshapes)))

    return (loss, grad_x, *[grad[n] for n in WEIGHT_NAMES], *[delta[n] for n in WEIGHT_NAMES],
            *[new_m[n] for n in WEIGHT_NAMES], *[new_v[n] for n in WEIGHT_NAMES])
```

```python
import functools
import math

import numpy as np
import jax
import jax.numpy as jnp
from jax import lax
from jax.experimental import pallas as pl
from jax.experimental.pallas import tpu as pltpu

F32 = jnp.float32
BF = jnp.bfloat16
HI = lax.Precision.HIGHEST
MESH = pl.DeviceIdType.MESH

D = 1024
L = 2048
LC = 256
SEQ = L + LC
NEX = 2
R = NEX * SEQ
TB = 256
BPE = SEQ // TB
NBLK = R // TB
GRID_W = 64
FF = 4 * D
EPS = 1e-6
ROPE_THETA = 10000.0
N_MOD = 6
EV_IN = 1792
OD_IN = 1440
OD_PAD = 1536
VMEM_LIMIT = 60 * 1024 * 1024

ADAM_LR = 0.001
ADAM_B1 = 0.9
ADAM_B2 = 0.999
ADAM_EPS = 1e-08
ADAM_WD = 0.01
ADAM_STEP = 10

NT = (((1,), (1,)), ((), ()))
TN = (((0,), (0,)), ((), ()))


def _cparams(sem=None):
    return pltpu.CompilerParams(dimension_semantics=sem, vmem_limit_bytes=VMEM_LIMIT)


@jax.custom_vjp
def _mm(a, b):
    return jnp.dot(a.astype(BF), b.astype(BF), preferred_element_type=F32)


def _mm_fwd(a, b):
    return _mm(a, b), (a, b)


def _mm_bwd(res, g):
    a, b = res
    gb = g.astype(BF)
    da = lax.dot_general(gb, b.astype(BF), NT, preferred_element_type=F32)
    db = lax.dot_general(a.astype(BF), gb, TN, preferred_element_type=F32)
    return da, db


_mm.defvjp(_mm_fwd, _mm_bwd)


@jax.custom_vjp
def _swap(x):
    n = x.shape[-1]
    ax = x.ndim - 1
    lane = lax.broadcasted_iota(jnp.int32, x.shape, ax)
    return jnp.where(lane % 2 == 0, pltpu.roll(x, n - 1, ax), pltpu.roll(x, 1, ax))


_swap.defvjp(lambda x: (_swap(x), None), lambda _, g: (_swap(g),))


def _rope(x, cos, sin):
    return x * cos + _swap(x) * sin


def _rmsn(x, g):
    return x * lax.rsqrt(jnp.mean(x * x, axis=-1, keepdims=True) + EPS) * g


def _grmsn(x, g, avg):
    ms = jnp.dot(x * x, avg, precision=HI, preferred_element_type=F32)
    return x * lax.rsqrt(ms + EPS) * g


def _modnorm(x, g, sh, sc):
    return _rmsn(x, g) * (1.0 + sc) + sh


def _gelu(x):
    return 0.5 * x * (1.0 + jnp.tanh(0.7978845608028654 * (x + 0.044715 * (x * x * x))))


def _silu(x):
    return x * jax.nn.sigmoid(x)


def _acc(ref, val, first):
    @pl.when(first)
    def _():
        ref[...] = val

    @pl.when(jnp.logical_not(first))
    def _():
        ref[...] += val


def _seg(i):
    return 2 * (i // BPE) + jnp.minimum(i % BPE, 1)


def _seg_first(i):
    return (i % BPE) <= 1


def _rb_call(name, body, row_in=(), mod_in=(), pos_in=(), full_in=(), shift_in=(),
             row_out=(), seg_out=(), acc_out=(), scratch=(), after=None):
    in_specs, args = [], []
    for a in row_in:
        in_specs.append(pl.BlockSpec((TB, a.shape[1]), lambda i: (i, 0)))
        args.append(a)
    for tab, m in mod_in:
        in_specs.append(pl.BlockSpec((1, 1, D), lambda i, m=m: (_seg(i) * N_MOD + m, 0, 0)))
        args.append(tab)
    for a in pos_in:
        in_specs.append(pl.BlockSpec((TB, a.shape[1]), lambda i: (i % BPE, 0)))
        args.append(a)
    for a in full_in:
        in_specs.append(pl.BlockSpec(a.shape, lambda i, n=a.ndim: (0,) * n))
        args.append(a)
    for a, d in shift_in:
        in_specs.append(pl.BlockSpec((TB, a.shape[1]), lambda i, d=d: (jnp.clip(i + d, 0, NBLK - 1), 0)))
        args.append(a)
    n_in = len(args)
    if after is not None:
        in_specs.append(pl.BlockSpec(after.shape, lambda i, n=after.ndim: (0,) * n))
        args.append(after)
    out_specs, out_shape = [], []
    for w, dt in row_out:
        out_specs.append(pl.BlockSpec((TB, w), lambda i: (i, 0)))
        out_shape.append(jax.ShapeDtypeStruct((R, w), dt))
    for w in seg_out:
        out_specs.append(pl.BlockSpec((1, 1, w), lambda i: (_seg(i), 0, 0)))
        out_shape.append(jax.ShapeDtypeStruct((4, 1, w), F32))
    for shp in acc_out:
        out_specs.append(pl.BlockSpec(shp, lambda i, n=len(shp): (0,) * n))
        out_shape.append(jax.ShapeDtypeStruct(shp, F32))

    def kern(*refs):
        body(pl.program_id(0), *refs[:n_in], *refs[len(args):])

    sem = ("arbitrary",) if (seg_out or acc_out) else ("parallel",)
    return pl.pallas_call(kern, grid=(NBLK,), in_specs=in_specs, out_specs=out_specs, out_shape=out_shape,
                          scratch_shapes=list(scratch), compiler_params=_cparams(sem), name=name)(*args)


def modnorm_fwd(name, x, mods, g, m_sh, m_sc):
    def body(i, x_ref, sh_ref, sc_ref, g_ref, h_ref):
        h_ref[...] = _modnorm(x_ref[...], g_ref[...], sh_ref[0], sc_ref[0]).astype(BF)

    return _rb_call(name, body, row_in=(x,), mod_in=((mods, m_sh), (mods, m_sc)), full_in=(g,),
                    row_out=((D, BF),))[0]


def modnorm_bwd(name, x, dh, dx_in, mods, g, m_sh, m_sc, after=None):
    def body(i, x_ref, dh_ref, dxin_ref, sh_ref, sc_ref, g_ref, dx_ref, dsh_ref, dsc_ref, dg_ref):
        _, vjp = jax.vjp(_modnorm, x_ref[...], g_ref[...], sh_ref[0], sc_ref[0])
        dx, dg, dsh, dsc = vjp(dh_ref[...].astype(F32))
        dx_ref[...] = dxin_ref[...] + dx
        _acc(dsh_ref, dsh[None], _seg_first(i))
        _acc(dsc_ref, dsc[None], _seg_first(i))
        _acc(dg_ref, dg, i == 0)

    return _rb_call(name, body, row_in=(x, dh, dx_in), mod_in=((mods, m_sh), (mods, m_sc)), full_in=(g,),
                    row_out=((D, F32),), seg_out=(D, D), acc_out=((1, D),), after=after)


def gate_bwd(name, dx, y, mods, m_gate, after=None):
    def body(i, dx_ref, y_ref, gt_ref, dy_ref, dgt_ref):
        dxv = dx_ref[...]
        dy_ref[...] = (dxv * gt_ref[0]).astype(BF)
        _acc(dgt_ref, jnp.sum(dxv * y_ref[...].astype(F32), axis=0, keepdims=True)[None], _seg_first(i))

    return _rb_call(name, body, row_in=(dx, y), mod_in=((mods, m_gate),), row_out=((D, BF),), seg_out=(D,),
                    after=after)


def proj_in(name, h, w):
    n = w.shape[1]

    def body(i, h_ref, w_ref, o_ref):
        o_ref[...] = jnp.dot(h_ref[...], w_ref[...], preferred_element_type=F32).astype(BF)

    return _rb_call(name, body, row_in=(h,), full_in=(w,), row_out=((n, BF),))[0]


def proj_out(name, a1, a2, w, x, mods, m_gate):
    k1 = a1.shape[1]

    def body(i, a1_ref, a2_ref, x_ref, gt_ref, w_ref, xo_ref, y_ref):
        y = jnp.dot(a1_ref[...], w_ref[:k1, :], preferred_element_type=F32)
        y = y + jnp.dot(a2_ref[...], w_ref[k1:, :], preferred_element_type=F32)
        y_ref[...] = y.astype(BF)
        xo_ref[...] = x_ref[...] + gt_ref[0] * y

    return _rb_call(name, body, row_in=(a1, a2, x), mod_in=((mods, m_gate),), full_in=(w,),
                    row_out=((D, F32), (D, BF)))


def mlp_up(name, h, w1):
    def body(i, h_ref, w_ref, a_ref, f_ref):
        hv = h_ref[...]
        for n in range(4):
            a = jnp.dot(hv, w_ref[n], preferred_element_type=F32)
            a_ref[:, n * D:(n + 1) * D] = a.astype(BF)
            r = jnp.maximum(a, 0.0)
            f_ref[:, n * D:(n + 1) * D] = (r * r).astype(BF)

    return _rb_call(name, body, row_in=(h,), full_in=(w1,), row_out=((FF, BF), (FF, BF)))


def mlp_down(name, f, w2, x, mods, m_gate):
    def body(i, f_ref, x_ref, gt_ref, w_ref, xo_ref, y_ref):
        y = jnp.dot(f_ref[:, 0:D], w_ref[0], preferred_element_type=F32)
        for n in range(1, 4):
            y = y + jnp.dot(f_ref[:, n * D:(n + 1) * D], w_ref[n], preferred_element_type=F32)
        y_ref[...] = y.astype(BF)
        xo_ref[...] = x_ref[...] + gt_ref[0] * y

    return _rb_call(name, body, row_in=(f, x), mod_in=((mods, m_gate),), full_in=(w2,),
                    row_out=((D, F32), (D, BF)))


def mm_nt(name, g, w, after=None):
    k = w.shape[0]

    def body(i, g_ref, w_ref, o_ref):
        o_ref[...] = lax.dot_general(g_ref[...], w_ref[...], NT, preferred_element_type=F32).astype(BF)

    return _rb_call(name, body, row_in=(g,), full_in=(w,), row_out=((k, BF),), after=after)[0]


def mlp_bwd_da(name, dy, w2, a):
    def body(i, dy_ref, a_ref, w_ref, da_ref):
        dyv = dy_ref[...]
        for n in range(4):
            df = lax.dot_general(dyv, w_ref[n], NT, preferred_element_type=F32)
            av = a_ref[:, n * D:(n + 1) * D].astype(F32)
            da_ref[:, n * D:(n + 1) * D] = (df * (2.0 * jnp.maximum(av, 0.0))).astype(BF)

    return _rb_call(name, body, row_in=(dy, a), full_in=(w2,), row_out=((FF, BF),))[0]


def mlp_bwd_dh(name, da, w1):
    def body(i, da_ref, w_ref, dh_ref):
        acc = lax.dot_general(da_ref[:, 0:D], w_ref[0], NT, preferred_element_type=F32)
        for n in range(1, 4):
            acc = acc + lax.dot_general(da_ref[:, n * D:(n + 1) * D], w_ref[n], NT, preferred_element_type=F32)
        dh_ref[...] = acc.astype(BF)

    return _rb_call(name, body, row_in=(da,), full_in=(w1,), row_out=((D, BF),))[0]


TN_ROWS = 512


def mm_tn(name, a, g, tiles, th, tw):
    nt = len(tiles)
    acs = jnp.asarray([t[0] for t in tiles], jnp.int32)
    gcs = jnp.asarray([t[1] for t in tiles], jnp.int32)
    nr = R // TN_ROWS

    def kern(ac_ref, gc_ref, a_ref, g_ref, o_ref, acc_ref):
        r = pl.program_id(1)

        @pl.when(r == 0)
        def _():
            acc_ref[...] = jnp.zeros_like(acc_ref)

        acc_ref[...] += lax.dot_general(a_ref[...], g_ref[...], TN, preferred_element_type=F32)

        @pl.when(r == nr - 1)
        def _():
            o_ref[...] = acc_ref[...].astype(BF)

    grid_spec = pltpu.PrefetchScalarGridSpec(
        num_scalar_prefetch=2, grid=(nt, nr),
        in_specs=[pl.BlockSpec((TN_ROWS, th), lambda t, r, ac, gc: (r, ac[t])),
                  pl.BlockSpec((TN_ROWS, tw), lambda t, r, ac, gc: (r, gc[t]))],
        out_specs=pl.BlockSpec((None, th, tw), lambda t, r, ac, gc: (t, 0, 0)),
        scratch_shapes=[pltpu.VMEM((th, tw), F32)])
    return pl.pallas_call(kern, grid_spec=grid_spec, out_shape=jax.ShapeDtypeStruct((nt, th, tw), BF),
                          compiler_params=_cparams(("parallel", "arbitrary")), name=name)(acs, gcs, a, g)


def _even_tok(q, k, zu, zv, gq, gk, gs, ws, bs, cq, sq, ck, sk, avg, masks):
    qr = _rope(_grmsn(q, gq, avg), cq, sq) * GQA_SCALE
    kr = _rope(_grmsn(k, gk, avg[:128, :128]), ck, sk)
    u = _gelu(zu)
    v = _grmsn(_gelu(zv), gs, avg)
    sv = None
    for g in range(8):
        t = masks[g] * (_mm(ws[g], v) + bs[g])
        sv = t if sv is None else sv + t
    return qr, kr, u * sv


def even_tok_fwd(p, cos, sin, gq, gk, gs, sgu_w, sgu_b, avg, masks):
    def body(i, p_ref, cos_ref, sin_ref, gq_ref, gk_ref, gs_ref, w_ref, b_ref, avg_ref, mk_ref, q_ref, kv_ref, m_ref):
        avgv = avg_ref[...]
        ws = [w_ref[g] for g in range(8)]
        bs = [b_ref[g] for g in range(8)]
        mks = [mk_ref[g] for g in range(8)]
        for c in range(2):
            rs = pl.ds(c * 128, 128)
            qr, kr, m = _even_tok(
                p_ref[rs, 0:512].astype(F32), p_ref[rs, 512:640].astype(F32),
                p_ref[rs, 768:1280].astype(F32), p_ref[rs, 1280:1792].astype(F32),
                gq_ref[...], gk_ref[...], gs_ref[...], ws, bs,
                cos_ref[rs, :], sin_ref[rs, :], cos_ref[rs, 0:128], sin_ref[rs, 0:128], avgv, mks)
            q_ref[rs, :] = qr.astype(BF)
            kv_ref[rs, 0:128] = kr.astype(BF)
            kv_ref[rs, 128:256] = p_ref[rs, 640:768]
            m_ref[rs, :] = m.astype(BF)

    return _rb_call("even_tok_fwd", body, row_in=(p,), pos_in=(cos, sin),
                    full_in=(gq, gk, gs, sgu_w, sgu_b, avg, masks), row_out=((512, BF), (256, BF), (512, BF)))


def even_tok_bwd(p, dq, dkv, dcat, cos, sin, gq, gk, gs, sgu_w, sgu_b, avg, masks):
    def body(i, p_ref, dq_ref, dkv_ref, dcat_ref, cos_ref, sin_ref, gq_ref, gk_ref, gs_ref, w_ref, b_ref,
             avg_ref, mk_ref, dp_ref, dgq_ref, dgk_ref, dgs_ref, dw_ref, db_ref):
        avgv = avg_ref[...]
        ws = [w_ref[g] for g in range(8)]
        bs = [b_ref[g] for g in range(8)]
        mks = [mk_ref[g] for g in range(8)]
        tot = None
        for c in range(2):
            rs = pl.ds(c * 128, 128)
            cq, sq, ck, sk = cos_ref[rs, :], sin_ref[rs, :], cos_ref[rs, 0:128], sin_ref[rs, 0:128]

            def f(q, k, zu, zv, gq, gk, gs, ws, bs):
                return _even_tok(q, k, zu, zv, gq, gk, gs, ws, bs, cq, sq, ck, sk, avgv, mks)

            _, vjp = jax.vjp(f, p_ref[rs, 0:512].astype(F32), p_ref[rs, 512:640].astype(F32),
                             p_ref[rs, 768:1280].astype(F32), p_ref[rs, 1280:1792].astype(F32),
                             gq_ref[...], gk_ref[...], gs_ref[...], ws, bs)
            d = vjp((dq_ref[rs, :].astype(F32), dkv_ref[rs, 0:128], dcat_ref[rs, 512:1024].astype(F32)))
            dp_ref[rs, 0:512] = d[0].astype(BF)
            dp_ref[rs, 512:640] = d[1].astype(BF)
            dp_ref[rs, 640:768] = dkv_ref[rs, 128:256].astype(BF)
            dp_ref[rs, 768:1280] = d[2].astype(BF)
            dp_ref[rs, 1280:1792] = d[3].astype(BF)
            part = [d[4], d[5], d[6]] + list(d[7]) + list(d[8])
            tot = part if tot is None else [a + b for a, b in zip(tot, part)]
        refs = [dgq_ref, dgk_ref, dgs_ref] + [dw_ref.at[g] for g in range(8)] + [db_ref.at[g] for g in range(8)]
        for ref, val in zip(refs, tot):
            _acc(ref, val, i == 0)

    return _rb_call("even_tok_bwd", body, row_in=(p, dq, dkv, dcat), pos_in=(cos, sin),
                    full_in=(gq, gk, gs, sgu_w, sgu_b, avg, masks), row_out=((EV_IN, BF),),
                    acc_out=((1, 512), (1, 128), (1, 512), (8, 128, 128), (8, 128, 1)))


MLA_SCALE = 96 ** -0.5
GQA_SCALE = 64 ** -0.5


def _odd_tok(cq, ckv, kr, za, zg, gq, gkv, wq, wkk, wkv, spread, cr, sr, ck, sk):
    cqn = _rmsn(cq, gq)
    q = _rope(_mm(cqn, wq), cr, sr) * MLA_SCALE
    ckn = _rmsn(ckv, gkv)
    k = _mm(ckn, wkk) + _mm(_rope(kr, ck, sk), spread)
    v = _mm(ckn, wkv)
    y = za * jax.nn.sigmoid(zg)
    return q, k, v, y


def odd_tok_fwd(p, cos, sin, gq, gkv, wq, wkk, wkv, spread):
    def body(i, p_ref, cos_ref, sin_ref, gq_ref, gkv_ref, wq_ref, wkk_ref, wkv_ref, sp_ref, q_ref, kv_ref, y_ref):
        q, k, v, y = _odd_tok(
            p_ref[:, 0:256].astype(F32), p_ref[:, 256:384].astype(F32), p_ref[:, 384:512].astype(F32),
            p_ref[:, 512:1024].astype(F32), p_ref[:, 1024:1536].astype(F32),
            gq_ref[...], gkv_ref[...], wq_ref[...], wkk_ref[...], wkv_ref[...], sp_ref[...],
            cos_ref[:, 0:768], sin_ref[:, 0:768], cos_ref[:, 768:896], sin_ref[:, 768:896])
        q_ref[...] = q.astype(BF)
        kv_ref[:, 0:768] = k.astype(BF)
        kv_ref[:, 768:1280] = v.astype(BF)
        y_ref[...] = y.astype(BF)

    return _rb_call("odd_tok_fwd", body, row_in=(p,), pos_in=(cos, sin), full_in=(gq, gkv, wq, wkk, wkv, spread),
                    row_out=((768, BF), (1280, BF), (512, BF)))


def odd_tok_bwd(p, dq, dkv, dy, cos, sin, gq, gkv, wq, wkk, wkv, spread):
    def body(i, p_ref, dq_ref, dkv_ref, dy_ref, cos_ref, sin_ref, gq_ref, gkv_ref, wq_ref, wkk_ref, wkv_ref, sp_ref,
             dp_ref, dgq_ref, dgkv_ref, dwq_ref, dwkk_ref, dwkv_ref):
        cr, sr, ck, sk = cos_ref[:, 0:768], sin_ref[:, 0:768], cos_ref[:, 768:896], sin_ref[:, 768:896]
        spread_v = sp_ref[...]

        def f(cq, ckv, kr, za, zg, gq, gkv, wq, wkk, wkv):
            return _odd_tok(cq, ckv, kr, za, zg, gq, gkv, wq, wkk, wkv, spread_v, cr, sr, ck, sk)

        _, vjp = jax.vjp(f, p_ref[:, 0:256].astype(F32), p_ref[:, 256:384].astype(F32),
                         p_ref[:, 384:512].astype(F32), p_ref[:, 512:1024].astype(F32),
                         p_ref[:, 1024:1536].astype(F32), gq_ref[...], gkv_ref[...], wq_ref[...],
                         wkk_ref[...], wkv_ref[...])
        d = vjp((dq_ref[...].astype(F32), dkv_ref[:, 0:768], dkv_ref[:, 768:1280], dy_ref[...].astype(F32)))
        dp_ref[:, 0:256] = d[0].astype(BF)
        dp_ref[:, 256:384] = d[1].astype(BF)
        dp_ref[:, 384:512] = d[2].astype(BF)
        dp_ref[:, 512:1024] = d[3].astype(BF)
        dp_ref[:, 1024:1536] = d[4].astype(BF)
        for ref, val in zip((dgq_ref, dgkv_ref, dwq_ref, dwkk_ref, dwkv_ref), d[5:]):
            _acc(ref, val, i == 0)

    return _rb_call("odd_tok_bwd", body, row_in=(p, dq, dkv, dy), pos_in=(cos, sin),
                    full_in=(gq, gkv, wq, wkk, wkv, spread), row_out=((OD_PAD, BF),),
                    acc_out=((1, 256), (1, 128), (256, 768), (128, 768), (128, 512)))


GQA_HEADS = [(64 * h, 64 * (h // 4), 64, 128 + 64 * (h // 4)) for h in range(8)]
MLA_HEADS = [(96 * h, 96 * h, 96, 768 + 64 * h) for h in range(8)]


def _by_block(j, run):
    @pl.when(j == 0)
    def _():
        run(LC)

    @pl.when(j > 0)
    def _():
        run(SEQ)


def attn_fwd(name, q, kv, heads):
    qw, kvw = q.shape[1], kv.shape[1]

    def kern(q_ref, kv_ref, o_ref, lse_ref):
        def run(nk):
            for h, (qo, ko, w, vo) in enumerate(heads):
                s = lax.dot_general(q_ref[:, qo:qo + w], kv_ref[0:nk, ko:ko + w], NT, preferred_element_type=F32)
                m = jnp.max(s, axis=-1, keepdims=True)
                p = jnp.exp(s - m)
                l = jnp.sum(p, axis=-1, keepdims=True)
                o = jnp.dot(p.astype(BF), kv_ref[0:nk, vo:vo + 64], preferred_element_type=F32) / l
                o_ref[:, 64 * h:64 * h + 64] = o.astype(BF)
                lse_ref[:, h:h + 1] = m + jnp.log(l)

        _by_block(pl.program_id(1), run)

    return pl.pallas_call(
        kern, grid=(NEX, BPE),
        in_specs=[pl.BlockSpec((TB, qw), lambda e, j: (e * BPE + j, 0)),
                  pl.BlockSpec((SEQ, kvw), lambda e, j: (e, 0))],
        out_specs=[pl.BlockSpec((TB, 512), lambda e, j: (e * BPE + j, 0)),
                   pl.BlockSpec((TB, 8), lambda e, j: (e * BPE + j, 0))],
        out_shape=[jax.ShapeDtypeStruct((R, 512), BF), jax.ShapeDtypeStruct((R, 8), F32)],
        compiler_params=_cparams(("parallel", "arbitrary")), name=name)(q, kv)


def attn_bwd(name, q, kv, o, dcat, lse, heads):
    qw, kvw = q.shape[1], kv.shape[1]

    def kern(q_ref, kv_ref, o_ref, do_ref, lse_ref, dq_ref, dkv_ref):
        j = pl.program_id(1)

        @pl.when(j == 0)
        def _():
            dkv_ref[...] = jnp.zeros_like(dkv_ref)

        def run(nk):
            for h, (qo, ko, w, vo) in enumerate(heads):
                qh = q_ref[:, qo:qo + w]
                kh = kv_ref[0:nk, ko:ko + w]
                s = lax.dot_general(qh, kh, NT, preferred_element_type=F32)
                p = jnp.exp(s - lse_ref[:, h:h + 1])
                do = do_ref[:, 64 * h:64 * h + 64]
                dsum = jnp.sum(do.astype(F32) * o_ref[:, 64 * h:64 * h + 64].astype(F32), axis=-1, keepdims=True)
                dp = lax.dot_general(do, kv_ref[0:nk, vo:vo + 64], NT, preferred_element_type=F32)
                ds = (p * (dp - dsum)).astype(BF)
                dkv_ref[0:nk, vo:vo + 64] += lax.dot_general(p.astype(BF), do, TN, preferred_element_type=F32)
                dq_ref[:, qo:qo + w] = jnp.dot(ds, kh, preferred_element_type=F32).astype(BF)
                dkv_ref[0:nk, ko:ko + w] += lax.dot_general(ds, qh, TN, preferred_element_type=F32)

        _by_block(j, run)

    return pl.pallas_call(
        kern, grid=(NEX, BPE),
        in_specs=[pl.BlockSpec((TB, qw), lambda e, j: (e * BPE + j, 0)),
                  pl.BlockSpec((SEQ, kvw), lambda e, j: (e, 0)),
                  pl.BlockSpec((TB, 512), lambda e, j: (e * BPE + j, 0)),
                  pl.BlockSpec((TB, 512), lambda e, j: (e * BPE + j, 0)),
                  pl.BlockSpec((TB, 8), lambda e, j: (e * BPE + j, 0))],
        out_specs=[pl.BlockSpec((TB, qw), lambda e, j: (e * BPE + j, 0)),
                   pl.BlockSpec((SEQ, kvw), lambda e, j: (e, 0))],
        out_shape=[jax.ShapeDtypeStruct((R, qw), BF), jax.ShapeDtypeStruct((R, kvw), F32)],
        compiler_params=_cparams(("parallel", "arbitrary")), name=name)(q, kv, o, dcat, lse)


HALO = 16
CONV_K = 31


def _fill_ext(ext_ref, prev_ref, cur_ref, next_ref, i):
    j = i % BPE
    has_prev = (j >= 2).astype(F32)
    has_next = jnp.logical_and(j >= 1, j <= BPE - 2).astype(F32)
    ext_ref[0:HALO, :] = prev_ref[TB - HALO:TB, :].astype(F32) * has_prev
    ext_ref[HALO:HALO + TB, :] = cur_ref[...].astype(F32)
    ext_ref[HALO + TB:2 * HALO + TB, :] = next_ref[0:HALO, :].astype(F32) * has_next


def _ln_silu(z, g, b):
    mu = jnp.mean(z, axis=-1, keepdims=True)
    zc = z - mu
    var = jnp.mean(zc * zc, axis=-1, keepdims=True)
    return _silu(zc * lax.rsqrt(var + EPS) * g + b)


def conf_fwd(y, cw, cb, lg, lb):
    def body(i, cur_ref, cw_ref, cb_ref, lg_ref, lb_ref, prev_ref, next_ref, z_ref, c_ref, ext_ref):
        _fill_ext(ext_ref, prev_ref, cur_ref, next_ref, i)
        acc = ext_ref[1:1 + TB, :] * cw_ref[0:1, :]
        for k in range(1, CONV_K):
            acc = acc + ext_ref[k + 1:k + 1 + TB, :] * cw_ref[k:k + 1, :]
        z = acc + cb_ref[...]
        z_ref[...] = z.astype(BF)
        c_ref[...] = _ln_silu(z, lg_ref[...], lb_ref[...]).astype(BF)

    return _rb_call("conf_fwd", body, row_in=(y,), full_in=(cw, cb, lg, lb), shift_in=((y, -1), (y, 1)),
                    row_out=((512, BF), (512, BF)), scratch=(pltpu.VMEM((TB + 2 * HALO, 512), F32),))


def conf_bwd_ln(z, dcat, lg, lb):
    def body(i, z_ref, dcat_ref, lg_ref, lb_ref, dz_ref, dlg_ref, dlb_ref, dcb_ref):
        _, vjp = jax.vjp(_ln_silu, z_ref[...].astype(F32), lg_ref[...], lb_ref[...])
        dz, dlg, dlb = vjp(dcat_ref[:, 512:1024].astype(F32))
        dz_ref[...] = dz.astype(BF)
        _acc(dlg_ref, dlg, i == 0)
        _acc(dlb_ref, dlb, i == 0)
        _acc(dcb_ref, jnp.sum(dz, axis=0, keepdims=True), i == 0)

    return _rb_call("conf_bwd_ln", body, row_in=(z, dcat), full_in=(lg, lb), row_out=((512, BF),),
                    acc_out=((1, 512), (1, 512), (1, 512)))


def conf_bwd_conv(y, dz, cw):
    def body(i, y_ref, dz_ref, cw_ref, yp_ref, yn_ref, dzp_ref, dzn_ref, dy_ref, dcw_ref, exty_ref, extd_ref):
        _fill_ext(exty_ref, yp_ref, y_ref, yn_ref, i)
        _fill_ext(extd_ref, dzp_ref, dz_ref, dzn_ref, i)
        dzv = dz_ref[...].astype(F32)
        @pl.when(i == 0)
        def _():
            dcw_ref[...] = jnp.zeros_like(dcw_ref)

        acc = None
        for k in range(CONV_K):
            t = extd_ref[CONV_K - k:CONV_K - k + TB, :] * cw_ref[k:k + 1, :]
            acc = t if acc is None else acc + t
            dcw_ref[k:k + 1, :] += jnp.sum(dzv * exty_ref[k + 1:k + 1 + TB, :], axis=0, keepdims=True)
        dy_ref[...] = acc.astype(BF)

    return _rb_call("conf_bwd_conv", body, row_in=(y, dz), full_in=(cw,),
                    shift_in=((y, -1), (y, 1), (dz, -1), (dz, 1)), row_out=((512, BF),), acc_out=((32, 512),),
                    scratch=(pltpu.VMEM((TB + 2 * HALO, 512), F32), pltpu.VMEM((TB + 2 * HALO, 512), F32)))


def final_loss(x, target, fg):
    lpb = L // TB

    def kern(x_ref, t_ref, g_ref, dx_ref, loss_ref, dg_ref):
        i = pl.program_id(0)
        lat = (i % BPE) >= 1
        xv, tv = x_ref[...], t_ref[...]

        def f(x, g):
            err = _rmsn(x, g) - tv
            rowsum = jnp.sum(err * err, axis=-1, keepdims=True)
            return jnp.sum(rowsum, axis=0, keepdims=True) * (0.5 / D)

        lv, vjp = jax.vjp(f, xv, g_ref[...])
        dx, dg = vjp(jnp.ones((1, 1), F32))
        m = lat.astype(F32)
        dx_ref[...] = dx * m
        _acc(loss_ref, jnp.zeros((8, 128), F32) + lv * m, i == 0)
        _acc(dg_ref, dg * m, i == 0)

    return pl.pallas_call(
        kern, grid=(NBLK,),
        in_specs=[pl.BlockSpec((TB, D), lambda i: (i, 0)),
                  pl.BlockSpec((TB, D), lambda i: ((i // BPE) * lpb + jnp.maximum(i % BPE - 1, 0), 0)),
                  pl.BlockSpec((1, D), lambda i: (0, 0))],
        out_specs=[pl.BlockSpec((TB, D), lambda i: (i, 0)), pl.BlockSpec((8, 128), lambda i: (0, 0)),
                   pl.BlockSpec((1, D), lambda i: (0, 0))],
        out_shape=[jax.ShapeDtypeStruct((R, D), F32), jax.ShapeDtypeStruct((8, 128), F32),
                   jax.ShapeDtypeStruct((1, D), F32)],
        compiler_params=_cparams(("arbitrary",)), name="final_loss")(x, target, fg)


NC = 24


def mods_fwd(call, ada_w, ada_b):
    cols = ada_w.shape[2]

    def kern(c_ref, w_ref, b_ref, o_ref):
        o_ref[...] = jnp.dot(_silu(c_ref[...]), w_ref[...], precision=HI, preferred_element_type=F32) + b_ref[...]

    return pl.pallas_call(
        kern, grid=(2,),
        in_specs=[pl.BlockSpec((NC, D), lambda l: (0, 0)), pl.BlockSpec((None, D, cols), lambda l: (l, 0, 0)),
                  pl.BlockSpec((None, 1, cols), lambda l: (l, 0, 0))],
        out_specs=pl.BlockSpec((None, NC, cols), lambda l: (l, 0, 0)),
        out_shape=jax.ShapeDtypeStruct((2, NC, cols), F32),
        compiler_params=_cparams(("parallel",)), name="mods_fwd")(call, ada_w, ada_b)


def ada_bwd(call, ada_w, dm):
    cols = ada_w.shape[2]

    def kern(c_ref, w_ref, dm_ref, gw_ref, dc_ref):
        l = pl.program_id(0)
        gw_ref[...] = lax.dot_general(_silu(c_ref[...]), dm_ref[...], TN, precision=HI, preferred_element_type=F32)
        part = lax.dot_general(dm_ref[16:24, :], w_ref[...], NT, precision=HI, preferred_element_type=F32)
        cc = c_ref[16:17, :]
        sg = jax.nn.sigmoid(cc)
        _acc(dc_ref, part * (sg * (1.0 + cc * (1.0 - sg))), l == 0)

    return pl.pallas_call(
        kern, grid=(2,),
        in_specs=[pl.BlockSpec((NC, D), lambda l: (0, 0)), pl.BlockSpec((None, D, cols), lambda l: (l, 0, 0)),
                  pl.BlockSpec((None, NC, cols), lambda l: (l, 0, 0))],
        out_specs=[pl.BlockSpec((None, D, cols), lambda l: (l, 0, 0)), pl.BlockSpec((8, D), lambda l: (0, 0))],
        out_shape=[jax.ShapeDtypeStruct((2, D, cols), F32), jax.ShapeDtypeStruct((8, D), F32)],
        compiler_params=_cparams(("arbitrary",)), name="ada_bwd")(call, ada_w, dm)


def sum_lead(name, a, scale_last=None):
    n, r, c = a.shape
    tr = r
    for cand in (512, 256, 128, 64, 32, 16, 8):
        if r % cand == 0 and cand * c * 4 * n <= 8 * 1024 * 1024:
            tr = cand
            break

    def kern(a_ref, o_ref):
        acc = a_ref[0].astype(F32)
        for k in range(1, n):
            acc = acc + a_ref[k].astype(F32)
        o_ref[...] = acc

    return pl.pallas_call(kern, grid=(r // tr,), in_specs=[pl.BlockSpec((n, tr, c), lambda i: (0, i, 0))],
                          out_specs=pl.BlockSpec((tr, c), lambda i: (i, 0)),
                          out_shape=jax.ShapeDtypeStruct((r, c), F32),
                          compiler_params=_cparams(("parallel",)), name=name)(a)


def add_pairs(name, hs, got, half):
    _, _, r, c = hs.shape

    def kern(half_ref, a_ref, b_ref, o_ref):
        o_ref[...] = (a_ref[...].astype(F32) + b_ref[...].astype(F32)).astype(BF)

    spec = pl.BlockSpec((None, r, c), lambda j, h: (j, 0, 0))
    grid_spec = pltpu.PrefetchScalarGridSpec(
        num_scalar_prefetch=1, grid=(4,),
        in_specs=[pl.BlockSpec((None, None, r, c), lambda j, h: (h[0], j, 0, 0)), spec], out_specs=spec)
    return pl.pallas_call(kern, grid_spec=grid_spec, out_shape=jax.ShapeDtypeStruct(got.shape, BF),
                          compiler_params=_cparams(("parallel",)), name=name)(half, hs, got)


def sum_slabs(name, land, own, chip):
    _, r, c = land.shape
    tr = r
    for cand in (512, 256, 128, 64, 32, 16):
        if r % cand == 0 and cand * c * 16 <= 4 * 1024 * 1024:
            tr = cand
            break

    def kern(chip_ref, land_ref, own_ref, o_ref):
        me = chip_ref[0]
        acc = None
        for k in range(4):
            t = jnp.where(me == k, own_ref[k], land_ref[k]).astype(F32)
            acc = t if acc is None else acc + t
        o_ref[...] = acc

    spec = pl.BlockSpec((4, tr, c), lambda i, m: (0, i, 0))
    grid_spec = pltpu.PrefetchScalarGridSpec(num_scalar_prefetch=1, grid=(r // tr,), in_specs=[spec, spec],
                                             out_specs=pl.BlockSpec((tr, c), lambda i, m: (i, 0)))
    return pl.pallas_call(kern, grid_spec=grid_spec, out_shape=jax.ShapeDtypeStruct((r, c), F32),
                          compiler_params=_cparams(("parallel",)), name=name)(chip, land, own)


def adamw(name, w, g, m, v):
    r, c = w.shape
    tr = r
    for cand in (512, 256, 128, 64, 32, 16, 8):
        if r % cand == 0 and cand * c * 4 <= 2 * 1024 * 1024:
            tr = cand
            break
    c1 = 1.0 / (1.0 - ADAM_B1 ** ADAM_STEP)
    c2 = 1.0 / (1.0 - ADAM_B2 ** ADAM_STEP)

    def kern(w_ref, g_ref, m_ref, v_ref, d_ref, mo_ref, vo_ref):
        gv = g_ref[...]
        mn = ADAM_B1 * m_ref[...] + (1.0 - ADAM_B1) * gv
        vn = ADAM_B2 * v_ref[...] + (1.0 - ADAM_B2) * (gv * gv)
        d_ref[...] = -ADAM_LR * ((mn * c1) / (jnp.sqrt(vn * c2) + ADAM_EPS) + ADAM_WD * w_ref[...])
        mo_ref[...] = mn
        vo_ref[...] = vn

    spec = pl.BlockSpec((tr, c), lambda i: (i, 0))
    shp = jax.ShapeDtypeStruct((r, c), F32)
    return pl.pallas_call(kern, grid=(r // tr,), in_specs=[spec] * 4, out_specs=[spec] * 3, out_shape=[shp] * 3,
                          compiler_params=_cparams(("parallel",)), name=name)(w, g, m, v)


def all_gather8(name, xs):
    m_per, n = xs.shape

    def body(x_ref, out_ref, send_sems, recv_sems, local_sem):
        x, y, c = lax.axis_index("x"), lax.axis_index("y"), lax.axis_index("c")
        me, sibling = (x, y, c), (x, y, 1 - c)
        chips = [(1 - x, y), (x, 1 - y), (1 - x, 1 - y)]

        def rows(px, py, pc):
            return out_ref.at[pl.ds((4 * px + 2 * py + pc) * m_per, m_per), :]

        def copy(k, block, to, src=None):
            return pltpu.make_async_remote_copy(
                src_ref=rows(*block) if src is None else src, dst_ref=rows(*block),
                send_sem=send_sems.at[k], recv_sem=recv_sems.at[k], device_id=to, device_id_type=MESH)

        mine = pltpu.make_async_copy(x_ref, rows(*me), local_sem)
        mine.start()
        first = [copy(0, me, sibling, src=x_ref)]
        first += [copy(1 + j, me, (*chip, c), src=x_ref) for j, chip in enumerate(chips)]
        for cp in first:
            cp.start()
        passed = [copy(4 + j, (*chip, c), sibling) for j, chip in enumerate(chips)]
        for j, chip in enumerate(chips):
            copy(1 + j, (*chip, c), me).wait_recv()
            passed[j].start()
        copy(0, sibling, me).wait_recv()
        for j, chip in enumerate(chips):
            copy(4 + j, (*chip, 1 - c), me).wait_recv()
        for cp in first + passed:
            cp.wait_send()
        mine.wait()

    return pl.pallas_call(
        body, out_shape=jax.ShapeDtypeStruct((8 * m_per, n), xs.dtype),
        in_specs=[pl.BlockSpec(memory_space=pltpu.VMEM)], out_specs=pl.BlockSpec(memory_space=pltpu.VMEM),
        scratch_shapes=[pltpu.SemaphoreType.DMA((7,)), pltpu.SemaphoreType.DMA((7,)), pltpu.SemaphoreType.DMA],
        compiler_params=pltpu.CompilerParams(vmem_limit_bytes=VMEM_LIMIT), name=name)(xs)


def sibling_merge(name, halves):
    n = len(halves)

    def body(*refs):
        src, dst = refs[:n], refs[n:2 * n]
        send_sems, recv_sems, local_sems = refs[2 * n], refs[2 * n + 1], refs[2 * n + 2]
        c = lax.axis_index("c")
        sibling = (lax.axis_index("x"), lax.axis_index("y"), 1 - c)
        local = [pltpu.make_async_copy(src[a], dst[a].at[c], local_sems.at[a]) for a in range(n)]
        sends = [pltpu.make_async_remote_copy(src_ref=src[a], dst_ref=dst[a].at[c], send_sem=send_sems.at[a],
                                              recv_sem=recv_sems.at[a], device_id=sibling, device_id_type=MESH)
                 for a in range(n)]
        recvs = [pltpu.make_async_remote_copy(src_ref=src[a], dst_ref=dst[a].at[1 - c], send_sem=send_sems.at[a],
                                              recv_sem=recv_sems.at[a], device_id=sibling, device_id_type=MESH)
                 for a in range(n)]
        for cp in local + sends:
            cp.start()
        for cp in recvs:
            cp.wait_recv()
        for cp in sends:
            cp.wait_send()
        for cp in local:
            cp.wait()

    anyspec = pl.BlockSpec(memory_space=pl.ANY)
    return pl.pallas_call(
        body, out_shape=[jax.ShapeDtypeStruct((2,) + s.shape, s.dtype) for s in halves],
        in_specs=[anyspec] * n, out_specs=[anyspec] * n,
        scratch_shapes=[pltpu.SemaphoreType.DMA((n,)), pltpu.SemaphoreType.DMA((n,)),
                        pltpu.SemaphoreType.DMA((n,))], name=name)(*halves)


def place_own(name, lands, srcs):
    n = len(srcs)

    def body(*refs):
        src, out, sems = refs[n:2 * n], refs[2 * n:3 * n], refs[3 * n]
        me = 2 * lax.axis_index("x") + lax.axis_index("y")
        cps = [pltpu.make_async_copy(src[a], out[a].at[me], sems.at[a]) for a in range(n)]
        for cp in cps:
            cp.start()
        for cp in cps:
            cp.wait()

    anyspec = pl.BlockSpec(memory_space=pl.ANY)
    return pl.pallas_call(
        body, out_shape=[jax.ShapeDtypeStruct(a.shape, a.dtype) for a in lands],
        in_specs=[anyspec] * (2 * n), out_specs=[anyspec] * n, input_output_aliases={a: a for a in range(n)},
        scratch_shapes=[pltpu.SemaphoreType.DMA((n,))], name=name)(*lands, *srcs)


def _half_copies(src, land, send_sems, recv_sems):
    c = lax.axis_index("c")
    sibling = (lax.axis_index("x"), lax.axis_index("y"), 1 - c)
    pairs = []
    for a in range(len(src)):
        cp = pltpu.make_async_remote_copy(src_ref=src[a].at[1 - c], dst_ref=land[a], send_sem=send_sems.at[a],
                                          recv_sem=recv_sems.at[a], device_id=sibling, device_id_type=MESH)
        pairs.append((cp, cp))
    return pairs


def _chip_copies(src, land, send_sems, recv_sems, scatter):
    x, y, c = lax.axis_index("x"), lax.axis_index("y"), lax.axis_index("c")
    me = 2 * x + y
    pairs = []
    for a in range(len(src)):
        for j, (px, py) in enumerate([(1 - x, y), (x, 1 - y), (1 - x, 1 - y)]):
            to = 2 * px + py
            out = src[a].at[to] if scatter else src[a]
            kw = dict(send_sem=send_sems.at[3 * a + j], recv_sem=recv_sems.at[3 * a + j], device_id=(px, py, c),
                      device_id_type=MESH)
            pairs.append((pltpu.make_async_remote_copy(src_ref=out, dst_ref=land[a].at[me], **kw),
                          pltpu.make_async_remote_copy(src_ref=out, dst_ref=land[a].at[to], **kw)))
    return pairs


_HBM = pl.BlockSpec(memory_space=pltpu.HBM)
_SEM = pl.BlockSpec(memory_space=pltpu.SEMAPHORE)


GATHER = (functools.partial(_chip_copies, scatter=False), 3)
SCATTER = (functools.partial(_chip_copies, scatter=True), 3)
TO_SIBLING = (_half_copies, 1)


def _landing(shapes, dtype):
    return [lax.empty(tuple(s), dtype) for s in shapes]


def exchange_start(name, groups, plan):
    copies, per = plan
    sizes = [len(s) for s, _ in groups]
    flat = [a for s, l in groups for a in list(s) + list(l)]
    ng = len(groups)

    def body(*refs):
        ins, outs = refs[:len(flat)], refs[len(flat):]
        off = 0
        for g, n in enumerate(sizes):
            src, land = ins[off:off + n], ins[off + n:off + 2 * n]
            off += 2 * n
            for send, _ in copies(src, land, outs[2 * g], outs[2 * g + 1]):
                send.start()
        outs[-1][...] = jnp.zeros_like(outs[-1])

    out_shape = []
    for n in sizes:
        out_shape += [pltpu.SemaphoreType.DMA((per * n,)), pltpu.SemaphoreType.DMA((per * n,))]
    out_shape += [pltpu.HBM(a.shape, a.dtype) for a in flat] + [jax.ShapeDtypeStruct((8, 128), F32)]
    res = pl.pallas_call(
        body, out_shape=tuple(out_shape), in_specs=[_HBM] * len(flat),
        out_specs=tuple([_SEM] * (2 * ng) + [_HBM] * len(flat) + [pl.BlockSpec(memory_space=pltpu.VMEM)]),
        input_output_aliases={k: 2 * ng + k for k in range(len(flat))},
        compiler_params=pltpu.CompilerParams(has_side_effects=pltpu.SideEffectType.DATAFLOW_SIDE_EFFECTING),
        name=name)(*[pltpu.with_memory_space_constraint(a, pltpu.HBM) for a in flat])
    handles, off = [], 2 * ng
    for g, n in enumerate(sizes):
        handles.append((res[2 * g], res[2 * g + 1], list(res[off:off + n]), list(res[off + n:off + 2 * n])))
        off += 2 * n
    return handles, res[-1]


def exchange_wait(name, handle, after, plan):
    send_sems, recv_sems, srcs, lands = handle
    n = len(srcs)

    def body(*refs):
        src, land = refs[:n], refs[n:2 * n]
        for send, recv in plan[0](src, land, refs[2 * n], refs[2 * n + 1]):
            send.wait_send()
            recv.wait_recv()

    res = pl.pallas_call(
        body, out_shape=tuple(pltpu.HBM(a.shape, a.dtype) for a in srcs + lands),
        in_specs=[_HBM] * (2 * n) + [_SEM, _SEM, pl.BlockSpec(memory_space=pl.ANY)],
        out_specs=tuple([_HBM] * (2 * n)), input_output_aliases={k: k for k in range(2 * n)},
        compiler_params=pltpu.CompilerParams(has_side_effects=pltpu.SideEffectType.DATAFLOW_SIDE_EFFECTING),
        name=name)(*srcs, *lands, send_sems, recv_sems, after)
    return list(res[:n]), list(res[n:])


def _rope_tables(d_rot, reps):
    rows = L // GRID_W
    row = np.repeat(np.arange(rows), GRID_W).astype(np.float32)
    col = np.tile(np.arange(GRID_W), rows).astype(np.float32)
    d_axis = d_rot // 2
    inv = (ROPE_THETA ** (-np.arange(0, d_axis, 2, dtype=np.float32) / d_axis)).astype(np.float32)
    ang = np.concatenate([row[:, None] * inv, col[:, None] * inv], axis=-1).astype(np.float32)
    cos, sin = np.cos(ang).astype(np.float32), np.sin(ang).astype(np.float32)
    c = np.repeat(cos, 2, axis=-1)
    s = np.stack([-sin, sin], axis=-1).reshape(L, d_rot)
    c = np.concatenate([np.ones((LC, d_rot), np.float32), c], axis=0)
    s = np.concatenate([np.zeros((LC, d_rot), np.float32), s], axis=0)
    return np.tile(c, (1, reps)), np.tile(s, (1, reps))


def _group_consts():
    g = np.arange(512) // 64
    avg = (g[:, None] == g[None, :]).astype(np.float32) / 64.0
    masks = (np.arange(8)[:, None] == g[None, :]).astype(np.float32).reshape(8, 1, 512)
    return jnp.asarray(avg), jnp.asarray(masks)


def _pack(items):
    flat = jnp.concatenate([a.reshape(-1).astype(F32) for a in items])
    n = flat.shape[0]
    rows = -(-n // D)
    rows = -(-rows // 8) * 8
    return jnp.pad(flat, (0, rows * D - n)).reshape(rows, D)


def _unpack(buf, shapes):
    lead = buf.shape[:-2]
    flat = buf.reshape(lead + (-1,))
    out, off = [], 0
    for shp in shapes:
        n = int(np.prod(shp))
        out.append(flat[..., off:off + n].reshape(lead + tuple(shp)))
        off += n
    return out


def _arrive(prm, key, after):
    if callable(prm[key]):
        prm[key](after)
    return prm[key]


def _layer_fwd(i, x, mods, prm, consts):
    sv = {}
    sv["x0"] = x
    h = modnorm_fwd(f"norm1_fwd_{i}", x, mods, prm["norm1_g"], 0, 1)
    sv["h"] = h
    p = proj_in(f"proj_in_{i}", h, _arrive(prm, "w_in", h))
    sv["p"] = p
    if i == 0:
        q, kv, m2 = even_tok_fwd(p, consts["cos_e"], consts["sin_e"], prm["gq"], prm["gk"], prm["gs"],
                                 prm["sgu_w"], prm["sgu_b"], consts["avg"], consts["masks"])
        o, lse = attn_fwd("attn_fwd_0", q, kv, GQA_HEADS)
        sv.update(q=q, kv=kv)
    else:
        q, kv, y = odd_tok_fwd(p, consts["cos_o"], consts["sin_o"], prm["gq"], prm["gkv"], prm["wq"], prm["wkk"],
                               prm["wkv"], consts["spread"])
        o, lse = attn_fwd("attn_fwd_1", q, kv, MLA_HEADS)
        z, m2 = conf_fwd(y, prm["conv_w"], prm["conv_b"], prm["ln_g"], prm["ln_b"])
        sv.update(q=q, kv=kv, y=y, z=z)
    sv.update(o=o, lse=lse, m2=m2)
    x1, y1 = proj_out(f"proj_out_{i}", o, m2, _arrive(prm, "w_out", o), x, mods, 2)
    sv.update(x1=x1, y1=y1)
    h2 = modnorm_fwd(f"norm2_fwd_{i}", x1, mods, prm["norm2_g"], 3, 4)
    a, f = mlp_up(f"mlp_up_{i}", h2, prm["w1"])
    x2, y2 = mlp_down(f"mlp_down_{i}", f, prm["w2"], x1, mods, 5)
    sv.update(h2=h2, a=a, f=f, y2=y2)
    return x2, sv


def _layer_bwd(i, dx, sv, mods, prm, consts, hook, entry):
    gr = {}
    dy2, dg2 = gate_bwd(f"gate2_bwd_{i}", dx, sv["y2"], mods, 5, after=entry)
    da = mlp_bwd_da(f"mlp_bwd_da_{i}", dy2, prm["w2"], sv["a"])
    tiles8 = [(h, j) for h in range(2) for j in range(4)]
    gr["w1"] = mm_tn(f"grad_w1_{i}", sv["h2"], da, tiles8, 512, D).reshape(2, 4, 512, D)
    gr["w2"] = mm_tn(f"grad_w2_{i}", sv["f"], dy2, [(2 * j + h, 0) for h in range(2) for j in range(4)],
                     512, D).reshape(2, 4, 512, D)
    dh2 = mlp_bwd_dh(f"mlp_bwd_dh_{i}", da, prm["w1"])
    dx1, dsh2, dsc2, gr["norm2_g"] = modnorm_bwd(f"norm2_bwd_{i}", sv["x1"], dh2, dx, mods, prm["norm2_g"], 3, 4,
                                                 after=hook(f"{i}:mlp", gr, dh2))
    dy1, dg1 = gate_bwd(f"gate1_bwd_{i}", dx1, sv["y1"], mods, 2)
    dcat = mm_nt(f"proj_out_bwd_{i}", dy1, prm["w_out"], after=hook(f"{i}:mid", gr, dy1))
    t4 = [(2 * j + h, 0) for h in range(2) for j in range(2)]
    go = mm_tn(f"grad_wout_a_{i}", sv["o"], dy1, t4, 128, D).reshape(2, 2, 128, D)
    gm = mm_tn(f"grad_wout_b_{i}", sv["m2"], dy1, t4, 128, D).reshape(2, 2, 128, D)
    gr["w_out"] = jnp.concatenate([go, gm], axis=1)
    if i == 0:
        dq, dkv = attn_bwd("attn_bwd_0", sv["q"], sv["kv"], sv["o"], dcat, sv["lse"], GQA_HEADS)
        dp, gr["gq"], gr["gk"], gr["gs"], gr["sgu_w"], gr["sgu_b"] = even_tok_bwd(
            sv["p"], dq, dkv, dcat, consts["cos_e"], consts["sin_e"], prm["gq"], prm["gk"],
            prm["gs"], prm["sgu_w"], prm["sgu_b"], consts["avg"], consts["masks"])
    else:
        dq, dkv = attn_bwd("attn_bwd_1", sv["q"], sv["kv"], sv["o"], dcat, sv["lse"], MLA_HEADS)
        dz, gr["ln_g"], gr["ln_b"], gr["conv_b"] = conf_bwd_ln(sv["z"], dcat, prm["ln_g"], prm["ln_b"])
        dyc, gr["conv_w"] = conf_bwd_conv(sv["y"], dz, prm["conv_w"])
        dp, gr["gq"], gr["gkv"], gr["wq"], gr["wkk"], gr["wkv"] = odd_tok_bwd(
            sv["p"], dq, dkv, dyc, consts["cos_o"], consts["sin_o"], prm["gq"], prm["gkv"], prm["wq"], prm["wkk"],
            prm["wkv"], consts["spread"])
    n_in = prm["w_in"].shape[1]
    gr["w_in"] = mm_tn(f"grad_win_{i}", sv["h"], dp, [(0, 0), (1, 0)], 512, n_in)
    dh = mm_nt(f"proj_in_bwd_{i}", dp, prm["w_in"])
    dx0, dsh1, dsc1, gr["norm1_g"] = modnorm_bwd(f"norm1_bwd_{i}", sv["x0"], dh, dx1, mods, prm["norm1_g"], 0, 1)
    dmods = jnp.concatenate([dsh1, dsc1, dg1, dsh2, dsc2, dg2], axis=1)
    return dx0, dmods, gr, hook(f"{i}:end", gr, dx0)


def local_step(xcat, target, mods, prms, final_g, hook=lambda point, grads, fresh: None):
    avg, masks = _group_consts()
    cos_e, sin_e = _rope_tables(64, 8)
    ck, sk = _rope_tables(32, 1)
    one64, zero64 = np.ones((SEQ, 64), np.float32), np.zeros((SEQ, 64), np.float32)
    one96, zero96 = np.ones((SEQ, 96), np.float32), np.zeros((SEQ, 96), np.float32)
    cos_o = np.concatenate([np.tile(np.concatenate([one64, ck], axis=1), (1, 8)), ck, one96], axis=1)
    sin_o = np.concatenate([np.tile(np.concatenate([zero64, sk], axis=1), (1, 8)), sk, zero96], axis=1)
    lane = np.arange(768)
    spread = np.zeros((128, 768), np.float32)
    spread[lane % 96 - 64, lane] = (lane % 96 >= 64)
    consts = dict(avg=avg, masks=masks, cos_e=jnp.asarray(cos_e), sin_e=jnp.asarray(sin_e),
                  cos_o=jnp.asarray(cos_o), sin_o=jnp.asarray(sin_o), spread=jnp.asarray(spread, BF))
    x = xcat
    saved = []
    for i in range(2):
        x, sv = _layer_fwd(i, x, mods[i], prms[i], consts)
        saved.append(sv)
    dx, loss, dfg = final_loss(x, target, final_g)
    dmods, grads = [None, None], [None, None]
    entry = None
    for i in (1, 0):
        dx, dmods[i], grads[i], entry = _layer_bwd(i, dx, saved[i], mods[i], prms[i], consts, hook, entry)
    return loss, dx, dmods, grads, dfg


def _row(v):
    return v.reshape(1, -1).astype(F32)


def odd_in_params(od_w_in, w_uq, w_ukv):
    od = jnp.concatenate([od_w_in[:, 0:416], jnp.zeros((D, 96), od_w_in.dtype), od_w_in[:, 416:OD_IN]], axis=1)
    ukv = w_ukv.reshape(128, 8, 128)
    wkk = jnp.pad(ukv[:, :, :64], ((0, 0), (0, 0), (0, 32))).reshape(128, 768)
    return dict(w_in=od, wq=w_uq, wkk=wkk, wkv=ukv[:, :, 64:].reshape(128, 512))


def small_params(small):
    p0 = dict(norm1_g=_row(small["norm1_g"][0]), norm2_g=_row(small["norm2_g"][0]),
              gq=jnp.tile(_row(small["ev_q_norm_g"]), (1, 8)), gk=jnp.tile(_row(small["ev_k_norm_g"]), (1, 2)),
              gs=_row(small["ev_sgu_norm_g"]), sgu_w=small["ev_sgu_w"].reshape(8, 128, 128).astype(F32),
              sgu_b=small["ev_sgu_b"].reshape(8, 128, 1).astype(F32))
    p1 = dict(norm1_g=_row(small["norm1_g"][1]), norm2_g=_row(small["norm2_g"][1]),
              gq=_row(small["od_q_norm_g"]), gkv=_row(small["od_kv_norm_g"]),
              conv_w=jnp.pad(small["od_conv_w"].reshape(CONV_K, 512).astype(F32), ((0, 1), (0, 0))),
              conv_b=_row(small["od_conv_b"]), ln_g=_row(small["od_ln_g"]), ln_b=_row(small["od_ln_b"]))
    return [p0, p1]


def prep_params(ev_w_in, od_w_in, w_out, w1, w2, w_uq, w_ukv, small):
    p0, p1 = small_params(small)
    p0.update(w_in=ev_w_in, w_out=w_out[0], w1=w1[0], w2=w2[0])
    p1.update(odd_in_params(od_w_in, w_uq, w_ukv), w_out=w_out[1], w1=w1[1], w2=w2[1])
    return [p0, p1]


def small_grads_natural(grads, dfg):
    g0, g1 = grads
    return dict(
        norm1_g=jnp.concatenate([g0["norm1_g"], g1["norm1_g"]], axis=0),
        norm2_g=jnp.concatenate([g0["norm2_g"], g1["norm2_g"]], axis=0),
        ev_q_norm_g=g0["gq"].reshape(8, 64).sum(0).reshape(1, 64),
        ev_k_norm_g=g0["gk"].reshape(2, 64).sum(0).reshape(1, 64),
        ev_sgu_norm_g=g0["gs"].reshape(1, 8, 64),
        ev_sgu_w=g0["sgu_w"].reshape(1, 8, 128, 128),
        ev_sgu_b=g0["sgu_b"].reshape(1, 8, 128),
        od_q_norm_g=g1["gq"].reshape(1, 256),
        od_kv_norm_g=g1["gkv"].reshape(1, 128),
        od_conv_w=g1["conv_w"][0:CONV_K].reshape(1, CONV_K, 512),
        od_conv_b=g1["conv_b"].reshape(1, 512),
        od_ln_g=g1["ln_g"].reshape(1, 512),
        od_ln_b=g1["ln_b"].reshape(1, 512),
        final_g=dfg.reshape(D))


def layer_grads_hs(i, g, part="all"):
    def cols(a):
        k, n = a.shape
        return a.reshape(2, k // 2, 4, n // 4).transpose(0, 2, 1, 3).astype(BF)

    mlp = [(("mlp_w1", i), g["w1"]), (("mlp_w2", i), g["w2"])]
    if part == "mlp":
        return mlp
    rest = [(("w_out", i), g["w_out"])]
    if i == 0:
        rest.append((("ev_w_in", 0), cols(g["w_in"].reshape(D, EV_IN))))
    else:
        od = g["w_in"].reshape(D, OD_PAD)
        od = jnp.concatenate([od[:, 0:416], od[:, 512:OD_PAD]], axis=1)
        ukv = jnp.concatenate([g["wkk"].reshape(128, 8, 96)[:, :, :64], g["wkv"].reshape(128, 8, 64)], axis=2)
        rest += [(("od_w_in", 0), cols(od)), (("od_w_uq", 0), cols(g["wq"])),
                 (("od_w_ukv", 0), cols(ukv.reshape(128, 1024)))]
    return rest if part == "rest" else mlp + rest


def big_grads_hs(grads):
    d = dict(layer_grads_hs(0, grads[0]) + layer_grads_hs(1, grads[1]))
    return dict(ev_w_in=d[("ev_w_in", 0)], od_w_in=d[("od_w_in", 0)], od_w_uq=d[("od_w_uq", 0)],
                od_w_ukv=d[("od_w_ukv", 0)], w_out=[d[("w_out", 0)], d[("w_out", 1)]],
                mlp_w1=[d[("mlp_w1", 0)], d[("mlp_w1", 1)]], mlp_w2=[d[("mlp_w2", 0)], d[("mlp_w2", 1)]])


def grads_to_natural(grads, dfg):
    out = small_grads_natural(grads, dfg)
    hs = big_grads_hs(grads)

    def from_cols(a):
        return a.transpose(0, 2, 1, 3).reshape(2 * a.shape[2], 4 * a.shape[3])

    def from_rows(a):
        return a.transpose(1, 0, 2, 3).reshape(8 * a.shape[2], a.shape[3])

    out["ev_w_in"] = from_cols(hs["ev_w_in"])[None]
    out["od_w_in"] = from_cols(hs["od_w_in"])[None]
    out["od_w_uq"] = from_cols(hs["od_w_uq"])[None]
    out["od_w_ukv"] = from_cols(hs["od_w_ukv"])[None]
    out["w_out"] = jnp.stack([from_rows(a) for a in hs["w_out"]])
    out["mlp_w1"] = jnp.stack([from_cols(a) for a in hs["mlp_w1"]])
    out["mlp_w2"] = jnp.stack([from_rows(a) for a in hs["mlp_w2"]])
    return out


WEIGHT_NAMES = ['c_ctx', 'ada_w', 'ada_b', 'norm1_g', 'norm2_g', 'w_out', 'mlp_w1', 'mlp_w2', 'ev_w_in',
                'ev_q_norm_g', 'ev_k_norm_g', 'ev_sgu_norm_g', 'ev_sgu_w', 'ev_sgu_b', 'od_w_in', 'od_q_norm_g',
                'od_kv_norm_g', 'od_w_uq', 'od_w_ukv', 'od_conv_w', 'od_conv_b', 'od_ln_g', 'od_ln_b', 'final_g']
REPL_SMALL = ['norm1_g', 'norm2_g', 'ev_q_norm_g', 'ev_k_norm_g', 'ev_sgu_norm_g', 'ev_sgu_w', 'ev_sgu_b',
              'od_kv_norm_g', 'final_g']
SHARD_SMALL = ['od_q_norm_g', 'od_conv_w', 'od_conv_b', 'od_ln_g', 'od_ln_b']
BIG = ['w_out', 'mlp_w1', 'mlp_w2', 'ev_w_in', 'od_w_in', 'od_w_uq', 'od_w_ukv']


def _gather_last(parts):
    return jnp.concatenate([parts[k] for k in range(4)], axis=-1)


class _Reduce:
    def __init__(self, tag, named, half, chip):
        self.tag, self.half, self.chip = tag, half, chip
        self.names, self.hs = zip(*named)
        self.hs = list(self.hs)

    def to_sibling(self):
        lands = [lax.empty(a.shape[1:], BF) for a in self.hs]
        (self.h1,), token = exchange_start(f"rs_sibling_start_{self.tag}", [(self.hs, lands)], TO_SIBLING)
        return token

    def to_chips(self, after):
        hs, got = exchange_wait(f"rs_sibling_wait_{self.tag}", self.h1, after, TO_SIBLING)
        pair = [add_pairs(f"rs_add_{self.tag}_{k}", a, b, self.half) for k, (a, b) in enumerate(zip(hs, got))]
        lands = [lax.empty(p.shape, BF) for p in pair]
        (self.h2,), token = exchange_start(f"rs_chips_start_{self.tag}", [(pair, lands)], SCATTER)
        return token

    def finish(self, after):
        pair, land = exchange_wait(f"rs_chips_wait_{self.tag}", self.h2, after, SCATTER)
        return [sum_slabs(f"rs_sum_{self.tag}_{k}", l, p, self.chip) for k, (l, p) in enumerate(zip(land, pair))]


def kernel(x, c, ctx, c_ctx, ada_w, ada_b, norm1_g, norm2_g, w_out, mlp_w1, mlp_w2, ev_w_in, ev_q_norm_g, ev_k_norm_g, ev_sgu_norm_g, ev_sgu_w, ev_sgu_b, od_w_in, od_q_norm_g, od_kv_norm_g, od_w_uq, od_w_ukv, od_conv_w, od_conv_b, od_ln_g, od_ln_b, final_g, loss_target, m_c_ctx, m_ada_w, m_ada_b, m_norm1_g, m_norm2_g, m_w_out, m_mlp_w1, m_mlp_w2, m_ev_w_in, m_ev_q_norm_g, m_ev_k_norm_g, m_ev_sgu_norm_g, m_ev_sgu_w, m_ev_sgu_b, m_od_w_in, m_od_q_norm_g, m_od_kv_norm_g, m_od_w_uq, m_od_w_ukv, m_od_conv_w, m_od_conv_b, m_od_ln_g, m_od_ln_b, m_final_g, v_c_ctx, v_ada_w, v_ada_b, v_norm1_g, v_norm2_g, v_w_out, v_mlp_w1, v_mlp_w2, v_ev_w_in, v_ev_q_norm_g, v_ev_k_norm_g, v_ev_sgu_norm_g, v_ev_sgu_w, v_ev_sgu_b, v_od_w_in, v_od_q_norm_g, v_od_kv_norm_g, v_od_w_uq, v_od_w_ukv, v_od_conv_w, v_od_conv_b, v_od_ln_g, v_od_ln_b, v_final_g):
    w = dict(c_ctx=c_ctx, ada_w=ada_w, ada_b=ada_b, norm1_g=norm1_g, norm2_g=norm2_g, w_out=w_out, mlp_w1=mlp_w1,
             mlp_w2=mlp_w2, ev_w_in=ev_w_in, ev_q_norm_g=ev_q_norm_g, ev_k_norm_g=ev_k_norm_g,
             ev_sgu_norm_g=ev_sgu_norm_g, ev_sgu_w=ev_sgu_w, ev_sgu_b=ev_sgu_b, od_w_in=od_w_in,
             od_q_norm_g=od_q_norm_g, od_kv_norm_g=od_kv_norm_g, od_w_uq=od_w_uq, od_w_ukv=od_w_ukv,
             od_conv_w=od_conv_w, od_conv_b=od_conv_b, od_ln_g=od_ln_g, od_ln_b=od_ln_b, final_g=final_g)
    mom = dict(c_ctx=m_c_ctx, ada_w=m_ada_w, ada_b=m_ada_b, norm1_g=m_norm1_g, norm2_g=m_norm2_g, w_out=m_w_out,
               mlp_w1=m_mlp_w1, mlp_w2=m_mlp_w2, ev_w_in=m_ev_w_in, ev_q_norm_g=m_ev_q_norm_g,
               ev_k_norm_g=m_ev_k_norm_g, ev_sgu_norm_g=m_ev_sgu_norm_g, ev_sgu_w=m_ev_sgu_w, ev_sgu_b=m_ev_sgu_b,
               od_w_in=m_od_w_in, od_q_norm_g=m_od_q_norm_g, od_kv_norm_g=m_od_kv_norm_g, od_w_uq=m_od_w_uq,
               od_w_ukv=m_od_w_ukv, od_conv_w=m_od_conv_w, od_conv_b=m_od_conv_b, od_ln_g=m_od_ln_g,
               od_ln_b=m_od_ln_b, final_g=m_final_g)
    var = dict(c_ctx=v_c_ctx, ada_w=v_ada_w, ada_b=v_ada_b, norm1_g=v_norm1_g, norm2_g=v_norm2_g, w_out=v_w_out,
               mlp_w1=v_mlp_w1, mlp_w2=v_mlp_w2, ev_w_in=v_ev_w_in, ev_q_norm_g=v_ev_q_norm_g,
               ev_k_norm_g=v_ev_k_norm_g, ev_sgu_norm_g=v_ev_sgu_norm_g, ev_sgu_w=v_ev_sgu_w, ev_sgu_b=v_ev_sgu_b,
               od_w_in=v_od_w_in, od_q_norm_g=v_od_q_norm_g, od_kv_norm_g=v_od_kv_norm_g, od_w_uq=v_od_w_uq,
               od_w_ukv=v_od_w_ukv, od_conv_w=v_od_conv_w, od_conv_b=v_od_conv_b, od_ln_g=v_od_ln_g,
               od_ln_b=v_od_ln_b, final_g=v_final_g)
    xi, yi, ci = lax.axis_index("x"), lax.axis_index("y"), lax.axis_index("c")
    chip = 2 * xi + yi
    dev = 2 * chip + ci

    shard_shapes = [w[n].shape for n in SHARD_SMALL]
    g0 = all_gather8("ag_small", _pack([c] + [w[n] for n in SHARD_SMALL]))
    g0 = g0.reshape(8, -1, D)
    parts = _unpack(g0, [c.shape] + shard_shapes)
    c_all = parts[0].reshape(16, D)
    small_full = {n: _gather_last(p[0::2]) for n, p in zip(SHARD_SMALL, parts[1:])}
    call = jnp.concatenate([c_all, c_ctx.reshape(1, D), jnp.zeros((NC - 17, D), F32)], axis=0)

    cols = ada_w.shape[2]
    ada_b_sh = lax.dynamic_slice(ada_b, (0, chip * cols), (2, cols)).reshape(2, 1, cols)
    mt = mods_fwd(call, ada_w, ada_b_sh)
    mt = all_gather8("ag_mods", mt.reshape(2 * NC, cols)).reshape(8, 2, NC, cols)
    table = mt[0::2].transpose(1, 2, 0, 3).reshape(2, NC, 4 * cols)
    mods = []
    for i in range(2):
        lat = lax.dynamic_slice(table[i], (2 * dev, 0), (2, 4 * cols))
        mc = table[i, 16]
        mods.append(jnp.stack([mc, lat[0], mc, lat[1]]).reshape(4 * N_MOD, 1, D))

    order = [[("ev_w_in", 0)], [("w_out", 0), ("mlp_w1", 0), ("mlp_w2", 0)],
             [("od_w_in", 0), ("od_w_uq", 0), ("od_w_ukv", 0), ("w_out", 1), ("mlp_w1", 1), ("mlp_w2", 1)]]
    groups = []
    for names in order:
        srcs = [w[n][i].astype(BF) for n, i in names]
        groups.append((srcs, [lax.empty((4,) + s.shape, BF) for s in srcs]))
    groups[0][0][0], table = lax.optimization_barrier((groups[0][0][0], table))
    handles, token = exchange_start("gather_start", groups, GATHER)
    mods[0] = mods[0] + token[0, 0]
    small = {n: w[n] for n in REPL_SMALL}
    small.update(small_full)
    prms = small_params(small)

    def arrived(k, after):
        srcs, lands = exchange_wait(f"gather_wait_{k}", handles[k], after, GATHER)
        return place_own(f"gather_own_{k}", lands, srcs)

    def arrive_ev_in(after):
        (ev,) = arrived(0, after)
        prms[0]["w_in"] = _gather_last(ev)

    def arrive_ev_rest(after):
        wo, w1, w2 = arrived(1, after)
        prms[0].update(w_out=wo.reshape(D, D), w1=w1, w2=w2)

    def arrive_od(after):
        od, uq, ukv, wo, w1, w2 = arrived(2, after)
        prms[1].update(odd_in_params(_gather_last(od), _gather_last(uq), _gather_last(ukv)),
                       w_out=wo.reshape(D, D), w1=w1, w2=w2)

    prms[0]["w_in"] = arrive_ev_in
    prms[0]["w_out"] = arrive_ev_rest
    prms[1]["w_in"] = arrive_od

    half, chip1 = ci.reshape(1).astype(jnp.int32), chip.reshape(1).astype(jnp.int32)
    red = {}

    def hook(point, g, fresh):
        if point == "1:end":
            red["l1"] = _Reduce("l1", layer_grads_hs(1, g, "all"), half, chip1)
            return red["l1"].to_sibling()
        if point == "0:mlp":
            red["l0_mlp"] = _Reduce("l0_mlp", layer_grads_hs(0, g, "mlp"), half, chip1)
            return red["l1"].to_chips(fresh) + red["l0_mlp"].to_sibling()
        if point == "0:mid":
            return red["l0_mlp"].to_chips(fresh)
        if point == "0:end":
            red["l0_rest"] = _Reduce("l0_rest", layer_grads_hs(0, g, "rest"), half, chip1)
            return red["l0_rest"].to_sibling()
        return None

    xcat = jnp.concatenate([ctx, x], axis=1).reshape(R, D)
    loss_p, dx, dmods, grads, dfg = local_step(xcat, loss_target.reshape(NEX * L, D), mods, prms,
                                               final_g.reshape(1, D), hook)
    grad_x = dx.reshape(NEX, SEQ, D)[:, LC:]

    sg = small_grads_natural(grads, dfg)
    dm = jnp.stack([d.reshape(4, N_MOD * D) for d in dmods])
    small_names = REPL_SMALL + SHARD_SMALL
    items = [dm[:, 1::2], dm[:, 0] + dm[:, 2]] + [sg[n] for n in small_names] + [loss_p[0:1, 0:1]]
    shapes = [a.shape for a in items]
    g1 = all_gather8("ag_grads", _pack(items))
    red["l0_rest"].to_chips(g1)
    rows1 = g1.shape[0] // 8
    g1 = g1.reshape(8, rows1, D)
    tot = _unpack(sum_lead("sum_small", g1), shapes)
    dm_lat = _unpack(g1, shapes[:1])[0]
    dm_lat = dm_lat.transpose(1, 0, 2, 3).reshape(2, 16, N_MOD * D)
    dm_all = jnp.concatenate([dm_lat, tot[1][:, None], jnp.zeros((2, NC - 17, N_MOD * D), F32)], axis=1)
    gsum = dict(zip(small_names, tot[2:2 + len(small_names)]))
    loss = tot[-1].reshape(())
    grad = {n: gsum[n].reshape(w[n].shape) for n in REPL_SMALL}
    for n in SHARD_SMALL:
        k = w[n].shape[-1]
        grad[n] = lax.dynamic_slice_in_dim(gsum[n], chip * k, k, axis=gsum[n].ndim - 1)
    grad["ada_b"] = sum_lead("sum_ada_b", dm_all.transpose(1, 0, 2).reshape(NC, 2 * N_MOD, D)).reshape(2, N_MOD * D)

    dm_sh = lax.dynamic_slice(dm_all, (0, 0, chip * cols), (2, NC, cols))
    grad["ada_w"], dcc = ada_bwd(call, ada_w, dm_sh)
    dcc = all_gather8("ag_cctx", dcc).reshape(8, 8, D)
    grad["c_ctx"] = sum_lead("sum_cctx", dcc[0::2])[0]

    names, halves = (), []
    for tag in ("l1", "l0_mlp", "l0_rest"):
        names += red[tag].names
        halves += red[tag].finish(dcc)
    full = dict(zip(names, sibling_merge("rs_sibling_merge", halves)))
    for n in BIG:
        grad[n] = jnp.stack([full[(n, i)] for i in range(w[n].shape[0])]).reshape(w[n].shape)

    delta, new_m, new_v = {}, {}, {}
    for n in ['ada_w'] + BIG:
        shp = w[n].shape
        two_d = (shp[0] * shp[1], shp[2])
        d_, m_, v_ = adamw(f"adamw_{n}", w[n].reshape(two_d), grad[n].reshape(two_d), mom[n].reshape(two_d),
                           var[n].reshape(two_d))
        delta[n], new_m[n], new_v[n] = d_.reshape(shp), m_.reshape(shp), v_.reshape(shp)
    rest = [n for n in WEIGHT_NAMES if n not in ['ada_w'] + BIG]
    rshapes = [w[n].shape for n in rest]
    d_, m_, v_ = adamw("adamw_small", _pack([w[n] for n in rest]), _pack([grad[n] for n in rest]),
                       _pack([mom[n] for n in rest]), _pack([var[n] for n in rest]))
    for dst, buf in ((delta, d_), (new_m, m_), (new_v, v_)):
        dst.update(zip(rest, _unpack(buf, rshapes)))

    return (loss, grad_x, *[grad[n] for n in WEIGHT_NAMES], *[delta[n] for n in WEIGHT_NAMES],
            *[new_m[n] for n in WEIGHT_NAMES], *[new_v[n] for n in WEIGHT_NAMES])
```

```python
import functools
import math

import numpy as np
import jax
import jax.numpy as jnp
from jax import lax
from jax.experimental import pallas as pl
from jax.experimental.pallas import tpu as pltpu

F32 = jnp.float32
BF = jnp.bfloat16
HI = lax.Precision.HIGHEST
MESH = pl.DeviceIdType.MESH

D = 1024
L = 2048
LC = 256
SEQ = L + LC
NEX = 2
R = NEX * SEQ
TB = 256
BPE = SEQ // TB
NBLK = R // TB
GRID_W = 64
FF = 4 * D
EPS = 1e-6
ROPE_THETA = 10000.0
N_MOD = 6
EV_IN = 1792
OD_IN = 1440
OD_PAD = 1536
VMEM_LIMIT = 60 * 1024 * 1024

ADAM_LR = 0.001
ADAM_B1 = 0.9
ADAM_B2 = 0.999
ADAM_EPS = 1e-08
ADAM_WD = 0.01
ADAM_STEP = 10

NT = (((1,), (1,)), ((), ()))
TN = (((0,), (0,)), ((), ()))


def _cparams(sem=None):
    return pltpu.CompilerParams(dimension_semantics=sem, vmem_limit_bytes=VMEM_LIMIT)


@jax.custom_vjp
def _mm(a, b):
    return jnp.dot(a.astype(BF), b.astype(BF), preferred_element_type=F32)


def _mm_fwd(a, b):
    return _mm(a, b), (a, b)


def _mm_bwd(res, g):
    a, b = res
    gb = g.astype(BF)
    da = lax.dot_general(gb, b.astype(BF), NT, preferred_element_type=F32)
    db = lax.dot_general(a.astype(BF), gb, TN, preferred_element_type=F32)
    return da, db


_mm.defvjp(_mm_fwd, _mm_bwd)


@jax.custom_vjp
def _swap(x):
    n = x.shape[-1]
    ax = x.ndim - 1
    lane = lax.broadcasted_iota(jnp.int32, x.shape, ax)
    return jnp.where(lane % 2 == 0, pltpu.roll(x, n - 1, ax), pltpu.roll(x, 1, ax))


_swap.defvjp(lambda x: (_swap(x), None), lambda _, g: (_swap(g),))


def _rope(x, cos, sin):
    return x * cos + _swap(x) * sin


def _rmsn(x, g):
    return x * lax.rsqrt(jnp.mean(x * x, axis=-1, keepdims=True) + EPS) * g


def _split_dot(a, m):
    hi = a.astype(BF)
    lo = (a - hi.astype(F32)).astype(BF)
    return jnp.dot(hi, m, preferred_element_type=F32) + jnp.dot(lo, m, preferred_element_type=F32)


@jax.custom_vjp
def _group_mean(a, avg):
    return _split_dot(a, avg)


_group_mean.defvjp(lambda a, avg: (_split_dot(a, avg), avg),
                   lambda avg, g: (_split_dot(g, avg), jnp.zeros_like(avg)))


def _grmsn(x, g, avg):
    return x * lax.rsqrt(_group_mean(x * x, avg) + EPS) * g


def _modnorm(x, g, sh, sc):
    return _rmsn(x, g) * (1.0 + sc) + sh


def _gelu(x):
    return 0.5 * x * (1.0 + jnp.tanh(0.7978845608028654 * (x + 0.044715 * (x * x * x))))


def _silu(x):
    return x * jax.nn.sigmoid(x)


def _acc(ref, val, first):
    @pl.when(first)
    def _():
        ref[...] = val

    @pl.when(jnp.logical_not(first))
    def _():
        ref[...] += val


def _seg(i):
    return 2 * (i // BPE) + jnp.minimum(i % BPE, 1)


def _seg_first(i):
    return (i % BPE) <= 1


def _rb_call(name, body, row_in=(), mod_in=(), pos_in=(), full_in=(), shift_in=(),
             row_out=(), seg_out=(), acc_out=(), scratch=(), after=None):
    in_specs, args = [], []
    for a in row_in:
        in_specs.append(pl.BlockSpec((TB, a.shape[1]), lambda i: (i, 0)))
        args.append(a)
    for tab, m in mod_in:
        in_specs.append(pl.BlockSpec((1, 1, D), lambda i, m=m: (_seg(i) * N_MOD + m, 0, 0)))
        args.append(tab)
    for a in pos_in:
        in_specs.append(pl.BlockSpec((TB, a.shape[1]), lambda i: (i % BPE, 0)))
        args.append(a)
    for a in full_in:
        in_specs.append(pl.BlockSpec(a.shape, lambda i, n=a.ndim: (0,) * n))
        args.append(a)
    for a, d in shift_in:
        in_specs.append(pl.BlockSpec((TB, a.shape[1]), lambda i, d=d: (jnp.clip(i + d, 0, NBLK - 1), 0)))
        args.append(a)
    n_in = len(args)
    if after is not None:
        in_specs.append(pl.BlockSpec(after.shape, lambda i, n=after.ndim: (0,) * n))
        args.append(after)
    out_specs, out_shape = [], []
    for w, dt in row_out:
        out_specs.append(pl.BlockSpec((TB, w), lambda i: (i, 0)))
        out_shape.append(jax.ShapeDtypeStruct((R, w), dt))
    for w in seg_out:
        out_specs.append(pl.BlockSpec((1, 1, w), lambda i: (_seg(i), 0, 0)))
        out_shape.append(jax.ShapeDtypeStruct((4, 1, w), F32))
    for shp in acc_out:
        out_specs.append(pl.BlockSpec(shp, lambda i, n=len(shp): (0,) * n))
        out_shape.append(jax.ShapeDtypeStruct(shp, F32))

    def kern(*refs):
        body(pl.program_id(0), *refs[:n_in], *refs[len(args):])

    sem = ("arbitrary",) if (seg_out or acc_out) else ("parallel",)
    return pl.pallas_call(kern, grid=(NBLK,), in_specs=in_specs, out_specs=out_specs, out_shape=out_shape,
                          scratch_shapes=list(scratch), compiler_params=_cparams(sem), name=name)(*args)


def modnorm_fwd(name, x, mods, g, m_sh, m_sc):
    def body(i, x_ref, sh_ref, sc_ref, g_ref, h_ref):
        h_ref[...] = _modnorm(x_ref[...], g_ref[...], sh_ref[0], sc_ref[0]).astype(BF)

    return _rb_call(name, body, row_in=(x,), mod_in=((mods, m_sh), (mods, m_sc)), full_in=(g,),
                    row_out=((D, BF),))[0]


def modnorm_bwd(name, x, dh, dx_in, mods, g, m_sh, m_sc, after=None):
    def body(i, x_ref, dh_ref, dxin_ref, sh_ref, sc_ref, g_ref, dx_ref, dsh_ref, dsc_ref, dg_ref):
        _, vjp = jax.vjp(_modnorm, x_ref[...], g_ref[...], sh_ref[0], sc_ref[0])
        dx, dg, dsh, dsc = vjp(dh_ref[...].astype(F32))
        dx_ref[...] = dxin_ref[...] + dx
        _acc(dsh_ref, dsh[None], _seg_first(i))
        _acc(dsc_ref, dsc[None], _seg_first(i))
        _acc(dg_ref, dg, i == 0)

    return _rb_call(name, body, row_in=(x, dh, dx_in), mod_in=((mods, m_sh), (mods, m_sc)), full_in=(g,),
                    row_out=((D, F32),), seg_out=(D, D), acc_out=((1, D),), after=after)


def gate_bwd(name, dx, y, mods, m_gate, after=None):
    def body(i, dx_ref, y_ref, gt_ref, dy_ref, dgt_ref):
        dxv = dx_ref[...]
        dy_ref[...] = (dxv * gt_ref[0]).astype(BF)
        _acc(dgt_ref, jnp.sum(dxv * y_ref[...].astype(F32), axis=0, keepdims=True)[None], _seg_first(i))

    return _rb_call(name, body, row_in=(dx, y), mod_in=((mods, m_gate),), row_out=((D, BF),), seg_out=(D,),
                    after=after)


def proj_in(name, h, w):
    n = w.shape[1]

    def body(i, h_ref, w_ref, o_ref):
        o_ref[...] = jnp.dot(h_ref[...], w_ref[...], preferred_element_type=F32).astype(BF)

    return _rb_call(name, body, row_in=(h,), full_in=(w,), row_out=((n, BF),))[0]


def proj_out(name, a1, a2, w, x, mods, m_gate):
    k1 = a1.shape[1]

    def body(i, a1_ref, a2_ref, x_ref, gt_ref, w_ref, xo_ref, y_ref):
        y = jnp.dot(a1_ref[...], w_ref[:k1, :], preferred_element_type=F32)
        y = y + jnp.dot(a2_ref[...], w_ref[k1:, :], preferred_element_type=F32)
        y_ref[...] = y.astype(BF)
        xo_ref[...] = x_ref[...] + gt_ref[0] * y

    return _rb_call(name, body, row_in=(a1, a2, x), mod_in=((mods, m_gate),), full_in=(w,),
                    row_out=((D, F32), (D, BF)))


def mlp_up(name, h, w1):
    def body(i, h_ref, w_ref, a_ref, f_ref):
        hv = h_ref[...]
        for n in range(4):
            a = jnp.dot(hv, w_ref[n], preferred_element_type=F32)
            a_ref[:, n * D:(n + 1) * D] = a.astype(BF)
            r = jnp.maximum(a, 0.0)
            f_ref[:, n * D:(n + 1) * D] = (r * r).astype(BF)

    return _rb_call(name, body, row_in=(h,), full_in=(w1,), row_out=((FF, BF), (FF, BF)))


def mlp_down(name, f, w2, x, mods, m_gate):
    def body(i, f_ref, x_ref, gt_ref, w_ref, xo_ref, y_ref):
        y = jnp.dot(f_ref[:, 0:D], w_ref[0], preferred_element_type=F32)
        for n in range(1, 4):
            y = y + jnp.dot(f_ref[:, n * D:(n + 1) * D], w_ref[n], preferred_element_type=F32)
        y_ref[...] = y.astype(BF)
        xo_ref[...] = x_ref[...] + gt_ref[0] * y

    return _rb_call(name, body, row_in=(f, x), mod_in=((mods, m_gate),), full_in=(w2,),
                    row_out=((D, F32), (D, BF)))


def mm_nt(name, g, w, after=None):
    k = w.shape[0]

    def body(i, g_ref, w_ref, o_ref):
        o_ref[...] = lax.dot_general(g_ref[...], w_ref[...], NT, preferred_element_type=F32).astype(BF)

    return _rb_call(name, body, row_in=(g,), full_in=(w,), row_out=((k, BF),), after=after)[0]


def mlp_bwd_da(name, dy, w2, a):
    def body(i, dy_ref, a_ref, w_ref, da_ref):
        dyv = dy_ref[...]
        for n in range(4):
            df = lax.dot_general(dyv, w_ref[n], NT, preferred_element_type=F32)
            av = a_ref[:, n * D:(n + 1) * D].astype(F32)
            da_ref[:, n * D:(n + 1) * D] = (df * (2.0 * jnp.maximum(av, 0.0))).astype(BF)

    return _rb_call(name, body, row_in=(dy, a), full_in=(w2,), row_out=((FF, BF),))[0]


def mlp_bwd_dh(name, da, w1):
    def body(i, da_ref, w_ref, dh_ref):
        acc = lax.dot_general(da_ref[:, 0:D], w_ref[0], NT, preferred_element_type=F32)
        for n in range(1, 4):
            acc = acc + lax.dot_general(da_ref[:, n * D:(n + 1) * D], w_ref[n], NT, preferred_element_type=F32)
        dh_ref[...] = acc.astype(BF)

    return _rb_call(name, body, row_in=(da,), full_in=(w1,), row_out=((D, BF),))[0]


TN_ROWS = 1536


def mm_tn(name, a, g, tiles, th, tw):
    nt = len(tiles)
    acs = jnp.asarray([t[0] for t in tiles], jnp.int32)
    gcs = jnp.asarray([t[1] for t in tiles], jnp.int32)
    nr = R // TN_ROWS

    def kern(ac_ref, gc_ref, a_ref, g_ref, o_ref, acc_ref):
        r = pl.program_id(1)

        @pl.when(r == 0)
        def _():
            acc_ref[...] = jnp.zeros_like(acc_ref)

        acc_ref[...] += lax.dot_general(a_ref[...], g_ref[...], TN, preferred_element_type=F32)

        @pl.when(r == nr - 1)
        def _():
            o_ref[...] = acc_ref[...].astype(BF)

    grid_spec = pltpu.PrefetchScalarGridSpec(
        num_scalar_prefetch=2, grid=(nt, nr),
        in_specs=[pl.BlockSpec((TN_ROWS, th), lambda t, r, ac, gc: (r, ac[t])),
                  pl.BlockSpec((TN_ROWS, tw), lambda t, r, ac, gc: (r, gc[t]))],
        out_specs=pl.BlockSpec((None, th, tw), lambda t, r, ac, gc: (t, 0, 0)),
        scratch_shapes=[pltpu.VMEM((th, tw), F32)])
    return pl.pallas_call(kern, grid_spec=grid_spec, out_shape=jax.ShapeDtypeStruct((nt, th, tw), BF),
                          compiler_params=_cparams(("parallel", "arbitrary")), name=name)(acs, gcs, a, g)


def _even_tok(q, k, zu, zv, gq, gk, gs, ws, bs, cq, sq, ck, sk, avg, masks):
    qr = _rope(_grmsn(q, gq, avg), cq, sq) * GQA_SCALE
    kr = _rope(_grmsn(k, gk, avg[:128, :128]), ck, sk)
    u = _gelu(zu)
    v = _grmsn(_gelu(zv), gs, avg)
    sv = None
    for g in range(8):
        t = masks[g] * (_mm(ws[g], v) + bs[g])
        sv = t if sv is None else sv + t
    return qr, kr, u * sv


def even_tok_fwd(p, cos, sin, gq, gk, gs, sgu_w, sgu_b, avg, masks):
    def body(i, p_ref, cos_ref, sin_ref, gq_ref, gk_ref, gs_ref, w_ref, b_ref, avg_ref, mk_ref, q_ref, kv_ref, m_ref):
        avgv = avg_ref[...]
        ws = [w_ref[g] for g in range(8)]
        bs = [b_ref[g] for g in range(8)]
        mks = [mk_ref[g] for g in range(8)]
        for c in range(2):
            rs = pl.ds(c * 128, 128)
            qr, kr, m = _even_tok(
                p_ref[rs, 0:512].astype(F32), p_ref[rs, 512:640].astype(F32),
                p_ref[rs, 768:1280].astype(F32), p_ref[rs, 1280:1792].astype(F32),
                gq_ref[...], gk_ref[...], gs_ref[...], ws, bs,
                cos_ref[rs, :], sin_ref[rs, :], cos_ref[rs, 0:128], sin_ref[rs, 0:128], avgv, mks)
            q_ref[rs, :] = qr.astype(BF)
            kv_ref[rs, 0:128] = kr.astype(BF)
            kv_ref[rs, 128:256] = p_ref[rs, 640:768]
            m_ref[rs, :] = m.astype(BF)

    return _rb_call("even_tok_fwd", body, row_in=(p,), pos_in=(cos, sin),
                    full_in=(gq, gk, gs, sgu_w, sgu_b, avg, masks), row_out=((512, BF), (256, BF), (512, BF)))


def even_tok_bwd(p, dq, dkv, dcat, cos, sin, gq, gk, gs, sgu_w, sgu_b, avg, masks):
    def body(i, p_ref, dq_ref, dkv_ref, dcat_ref, cos_ref, sin_ref, gq_ref, gk_ref, gs_ref, w_ref, b_ref,
             avg_ref, mk_ref, dp_ref, dgq_ref, dgk_ref, dgs_ref, dw_ref, db_ref):
        avgv = avg_ref[...]
        ws = [w_ref[g] for g in range(8)]
        bs = [b_ref[g] for g in range(8)]
        mks = [mk_ref[g] for g in range(8)]
        tot = None
        for c in range(2):
            rs = pl.ds(c * 128, 128)
            cq, sq, ck, sk = cos_ref[rs, :], sin_ref[rs, :], cos_ref[rs, 0:128], sin_ref[rs, 0:128]

            def f(q, k, zu, zv, gq, gk, gs, ws, bs):
                return _even_tok(q, k, zu, zv, gq, gk, gs, ws, bs, cq, sq, ck, sk, avgv, mks)

            _, vjp = jax.vjp(f, p_ref[rs, 0:512].astype(F32), p_ref[rs, 512:640].astype(F32),
                             p_ref[rs, 768:1280].astype(F32), p_ref[rs, 1280:1792].astype(F32),
                             gq_ref[...], gk_ref[...], gs_ref[...], ws, bs)
            d = vjp((dq_ref[rs, :].astype(F32), dkv_ref[rs, 0:128], dcat_ref[rs, 512:1024].astype(F32)))
            dp_ref[rs, 0:512] = d[0].astype(BF)
            dp_ref[rs, 512:640] = d[1].astype(BF)
            dp_ref[rs, 640:768] = dkv_ref[rs, 128:256].astype(BF)
            dp_ref[rs, 768:1280] = d[2].astype(BF)
            dp_ref[rs, 1280:1792] = d[3].astype(BF)
            part = [d[4], d[5], d[6]] + list(d[7]) + list(d[8])
            tot = part if tot is None else [a + b for a, b in zip(tot, part)]
        refs = [dgq_ref, dgk_ref, dgs_ref] + [dw_ref.at[g] for g in range(8)] + [db_ref.at[g] for g in range(8)]
        for ref, val in zip(refs, tot):
            _acc(ref, val, i == 0)

    return _rb_call("even_tok_bwd", body, row_in=(p, dq, dkv, dcat), pos_in=(cos, sin),
                    full_in=(gq, gk, gs, sgu_w, sgu_b, avg, masks), row_out=((EV_IN, BF),),
                    acc_out=((1, 512), (1, 128), (1, 512), (8, 128, 128), (8, 128, 1)))


MLA_SCALE = 96 ** -0.5
GQA_SCALE = 64 ** -0.5


def _odd_tok(cq, ckv, kr, za, zg, gq, gkv, wq, wkk, wkv, spread, cr, sr, ck, sk):
    cqn = _rmsn(cq, gq)
    q = _rope(_mm(cqn, wq), cr, sr) * MLA_SCALE
    ckn = _rmsn(ckv, gkv)
    k = _mm(ckn, wkk) + _mm(_rope(kr, ck, sk), spread)
    v = _mm(ckn, wkv)
    y = za * jax.nn.sigmoid(zg)
    return q, k, v, y


def odd_tok_fwd(p, cos, sin, gq, gkv, wq, wkk, wkv, spread):
    def body(i, p_ref, cos_ref, sin_ref, gq_ref, gkv_ref, wq_ref, wkk_ref, wkv_ref, sp_ref, q_ref, kv_ref, y_ref):
        q, k, v, y = _odd_tok(
            p_ref[:, 0:256].astype(F32), p_ref[:, 256:384].astype(F32), p_ref[:, 384:512].astype(F32),
            p_ref[:, 512:1024].astype(F32), p_ref[:, 1024:1536].astype(F32),
            gq_ref[...], gkv_ref[...], wq_ref[...], wkk_ref[...], wkv_ref[...], sp_ref[...],
            cos_ref[:, 0:768], sin_ref[:, 0:768], cos_ref[:, 768:896], sin_ref[:, 768:896])
        q_ref[...] = q.astype(BF)
        kv_ref[:, 0:768] = k.astype(BF)
        kv_ref[:, 768:1280] = v.astype(BF)
        y_ref[...] = y.astype(BF)

    return _rb_call("odd_tok_fwd", body, row_in=(p,), pos_in=(cos, sin), full_in=(gq, gkv, wq, wkk, wkv, spread),
                    row_out=((768, BF), (1280, BF), (512, BF)))


def odd_tok_bwd(p, dq, dkv, dy, cos, sin, gq, gkv, wq, wkk, wkv, spread):
    def body(i, p_ref, dq_ref, dkv_ref, dy_ref, cos_ref, sin_ref, gq_ref, gkv_ref, wq_ref, wkk_ref, wkv_ref, sp_ref,
             dp_ref, dgq_ref, dgkv_ref, dwq_ref, dwkk_ref, dwkv_ref):
        cr, sr, ck, sk = cos_ref[:, 0:768], sin_ref[:, 0:768], cos_ref[:, 768:896], sin_ref[:, 768:896]
        spread_v = sp_ref[...]

        def f(cq, ckv, kr, za, zg, gq, gkv, wq, wkk, wkv):
            return _odd_tok(cq, ckv, kr, za, zg, gq, gkv, wq, wkk, wkv, spread_v, cr, sr, ck, sk)

        _, vjp = jax.vjp(f, p_ref[:, 0:256].astype(F32), p_ref[:, 256:384].astype(F32),
                         p_ref[:, 384:512].astype(F32), p_ref[:, 512:1024].astype(F32),
                         p_ref[:, 1024:1536].astype(F32), gq_ref[...], gkv_ref[...], wq_ref[...],
                         wkk_ref[...], wkv_ref[...])
        d = vjp((dq_ref[...].astype(F32), dkv_ref[:, 0:768], dkv_ref[:, 768:1280], dy_ref[...].astype(F32)))
        dp_ref[:, 0:256] = d[0].astype(BF)
        dp_ref[:, 256:384] = d[1].astype(BF)
        dp_ref[:, 384:512] = d[2].astype(BF)
        dp_ref[:, 512:1024] = d[3].astype(BF)
        dp_ref[:, 1024:1536] = d[4].astype(BF)
        for ref, val in zip((dgq_ref, dgkv_ref, dwq_ref, dwkk_ref, dwkv_ref), d[5:]):
            _acc(ref, val, i == 0)

    return _rb_call("odd_tok_bwd", body, row_in=(p, dq, dkv, dy), pos_in=(cos, sin),
                    full_in=(gq, gkv, wq, wkk, wkv, spread), row_out=((OD_PAD, BF),),
                    acc_out=((1, 256), (1, 128), (256, 768), (128, 768), (128, 512)))


GQA_HEADS = [(64 * h, 64 * (h // 4), 64, 128 + 64 * (h // 4)) for h in range(8)]
MLA_HEADS = [(96 * h, 96 * h, 96, 768 + 64 * h) for h in range(8)]


def _by_block(j, run):
    @pl.when(j == 0)
    def _():
        run(LC)

    @pl.when(j > 0)
    def _():
        run(SEQ)


def attn_fwd(name, q, kv, heads):
    qw, kvw = q.shape[1], kv.shape[1]

    def kern(q_ref, kv_ref, o_ref, lse_ref):
        def run(nk):
            for h, (qo, ko, w, vo) in enumerate(heads):
                s = lax.dot_general(q_ref[:, qo:qo + w], kv_ref[0:nk, ko:ko + w], NT, preferred_element_type=F32)
                m = jnp.max(s, axis=-1, keepdims=True)
                p = jnp.exp(s - m)
                l = jnp.sum(p, axis=-1, keepdims=True)
                o = jnp.dot(p.astype(BF), kv_ref[0:nk, vo:vo + 64], preferred_element_type=F32) / l
                o_ref[:, 64 * h:64 * h + 64] = o.astype(BF)
                lse_ref[:, h:h + 1] = m + jnp.log(l)

        _by_block(pl.program_id(1), run)

    return pl.pallas_call(
        kern, grid=(NEX, BPE),
        in_specs=[pl.BlockSpec((TB, qw), lambda e, j: (e * BPE + j, 0)),
                  pl.BlockSpec((SEQ, kvw), lambda e, j: (e, 0))],
        out_specs=[pl.BlockSpec((TB, 512), lambda e, j: (e * BPE + j, 0)),
                   pl.BlockSpec((TB, 8), lambda e, j: (e * BPE + j, 0))],
        out_shape=[jax.ShapeDtypeStruct((R, 512), BF), jax.ShapeDtypeStruct((R, 8), F32)],
        compiler_params=_cparams(("parallel", "arbitrary")), name=name)(q, kv)


def attn_bwd(name, q, kv, o, dcat, lse, heads):
    qw, kvw = q.shape[1], kv.shape[1]

    def kern(q_ref, kv_ref, o_ref, do_ref, lse_ref, dq_ref, dkv_ref):
        j = pl.program_id(1)

        @pl.when(j == 0)
        def _():
            dkv_ref[...] = jnp.zeros_like(dkv_ref)

        def run(nk):
            for h, (qo, ko, w, vo) in enumerate(heads):
                qh = q_ref[:, qo:qo + w]
                kh = kv_ref[0:nk, ko:ko + w]
                s = lax.dot_general(qh, kh, NT, preferred_element_type=F32)
                p = jnp.exp(s - lse_ref[:, h:h + 1])
                do = do_ref[:, 64 * h:64 * h + 64]
                dsum = jnp.sum(do.astype(F32) * o_ref[:, 64 * h:64 * h + 64].astype(F32), axis=-1, keepdims=True)
                dp = lax.dot_general(do, kv_ref[0:nk, vo:vo + 64], NT, preferred_element_type=F32)
                ds = (p * (dp - dsum)).astype(BF)
                dkv_ref[0:nk, vo:vo + 64] += lax.dot_general(p.astype(BF), do, TN, preferred_element_type=F32)
                dq_ref[:, qo:qo + w] = jnp.dot(ds, kh, preferred_element_type=F32).astype(BF)
                dkv_ref[0:nk, ko:ko + w] += lax.dot_general(ds, qh, TN, preferred_element_type=F32)

        _by_block(j, run)

    return pl.pallas_call(
        kern, grid=(NEX, BPE),
        in_specs=[pl.BlockSpec((TB, qw), lambda e, j: (e * BPE + j, 0)),
                  pl.BlockSpec((SEQ, kvw), lambda e, j: (e, 0)),
                  pl.BlockSpec((TB, 512), lambda e, j: (e * BPE + j, 0)),
                  pl.BlockSpec((TB, 512), lambda e, j: (e * BPE + j, 0)),
                  pl.BlockSpec((TB, 8), lambda e, j: (e * BPE + j, 0))],
        out_specs=[pl.BlockSpec((TB, qw), lambda e, j: (e * BPE + j, 0)),
                   pl.BlockSpec((SEQ, kvw), lambda e, j: (e, 0))],
        out_shape=[jax.ShapeDtypeStruct((R, qw), BF), jax.ShapeDtypeStruct((R, kvw), F32)],
        compiler_params=_cparams(("parallel", "arbitrary")), name=name)(q, kv, o, dcat, lse)


HALO = 16
CONV_K = 31


def _fill_ext(ext_ref, prev_ref, cur_ref, next_ref, i):
    j = i % BPE
    has_prev = (j >= 2).astype(F32)
    has_next = jnp.logical_and(j >= 1, j <= BPE - 2).astype(F32)
    ext_ref[0:HALO, :] = prev_ref[TB - HALO:TB, :].astype(F32) * has_prev
    ext_ref[HALO:HALO + TB, :] = cur_ref[...].astype(F32)
    ext_ref[HALO + TB:2 * HALO + TB, :] = next_ref[0:HALO, :].astype(F32) * has_next


def _ln_silu(z, g, b):
    mu = jnp.mean(z, axis=-1, keepdims=True)
    zc = z - mu
    var = jnp.mean(zc * zc, axis=-1, keepdims=True)
    return _silu(zc * lax.rsqrt(var + EPS) * g + b)


def conf_fwd(y, cw, cb, lg, lb):
    def body(i, cur_ref, cw_ref, cb_ref, lg_ref, lb_ref, prev_ref, next_ref, z_ref, c_ref, ext_ref):
        _fill_ext(ext_ref, prev_ref, cur_ref, next_ref, i)
        acc = ext_ref[1:1 + TB, :] * cw_ref[0:1, :]
        for k in range(1, CONV_K):
            acc = acc + ext_ref[k + 1:k + 1 + TB, :] * cw_ref[k:k + 1, :]
        z = acc + cb_ref[...]
        z_ref[...] = z.astype(BF)
        c_ref[...] = _ln_silu(z, lg_ref[...], lb_ref[...]).astype(BF)

    return _rb_call("conf_fwd", body, row_in=(y,), full_in=(cw, cb, lg, lb), shift_in=((y, -1), (y, 1)),
                    row_out=((512, BF), (512, BF)), scratch=(pltpu.VMEM((TB + 2 * HALO, 512), F32),))


def conf_bwd_ln(z, dcat, lg, lb):
    def body(i, z_ref, dcat_ref, lg_ref, lb_ref, dz_ref, dlg_ref, dlb_ref, dcb_ref):
        _, vjp = jax.vjp(_ln_silu, z_ref[...].astype(F32), lg_ref[...], lb_ref[...])
        dz, dlg, dlb = vjp(dcat_ref[:, 512:1024].astype(F32))
        dz_ref[...] = dz.astype(BF)
        _acc(dlg_ref, dlg, i == 0)
        _acc(dlb_ref, dlb, i == 0)
        _acc(dcb_ref, jnp.sum(dz, axis=0, keepdims=True), i == 0)

    return _rb_call("conf_bwd_ln", body, row_in=(z, dcat), full_in=(lg, lb), row_out=((512, BF),),
                    acc_out=((1, 512), (1, 512), (1, 512)))


def conf_bwd_conv(y, dz, cw):
    def body(i, y_ref, dz_ref, cw_ref, yp_ref, yn_ref, dzp_ref, dzn_ref, dy_ref, dcw_ref, exty_ref, extd_ref):
        _fill_ext(exty_ref, yp_ref, y_ref, yn_ref, i)
        _fill_ext(extd_ref, dzp_ref, dz_ref, dzn_ref, i)
        dzv = dz_ref[...].astype(F32)
        @pl.when(i == 0)
        def _():
            dcw_ref[...] = jnp.zeros_like(dcw_ref)

        acc = None
        for k in range(CONV_K):
            t = extd_ref[CONV_K - k:CONV_K - k + TB, :] * cw_ref[k:k + 1, :]
            acc = t if acc is None else acc + t
            dcw_ref[k:k + 1, :] += jnp.sum(dzv * exty_ref[k + 1:k + 1 + TB, :], axis=0, keepdims=True)
        dy_ref[...] = acc.astype(BF)

    return _rb_call("conf_bwd_conv", body, row_in=(y, dz), full_in=(cw,),
                    shift_in=((y, -1), (y, 1), (dz, -1), (dz, 1)), row_out=((512, BF),), acc_out=((32, 512),),
                    scratch=(pltpu.VMEM((TB + 2 * HALO, 512), F32), pltpu.VMEM((TB + 2 * HALO, 512), F32)))


def final_loss(x, target, fg):
    lpb = L // TB

    def kern(x_ref, t_ref, g_ref, dx_ref, loss_ref, dg_ref):
        i = pl.program_id(0)
        lat = (i % BPE) >= 1
        xv, tv = x_ref[...], t_ref[...]

        def f(x, g):
            err = _rmsn(x, g) - tv
            rowsum = jnp.sum(err * err, axis=-1, keepdims=True)
            return jnp.sum(rowsum, axis=0, keepdims=True) * (0.5 / D)

        lv, vjp = jax.vjp(f, xv, g_ref[...])
        dx, dg = vjp(jnp.ones((1, 1), F32))
        m = lat.astype(F32)
        dx_ref[...] = dx * m
        _acc(loss_ref, jnp.zeros((8, 128), F32) + lv * m, i == 0)
        _acc(dg_ref, dg * m, i == 0)

    return pl.pallas_call(
        kern, grid=(NBLK,),
        in_specs=[pl.BlockSpec((TB, D), lambda i: (i, 0)),
                  pl.BlockSpec((TB, D), lambda i: ((i // BPE) * lpb + jnp.maximum(i % BPE - 1, 0), 0)),
                  pl.BlockSpec((1, D), lambda i: (0, 0))],
        out_specs=[pl.BlockSpec((TB, D), lambda i: (i, 0)), pl.BlockSpec((8, 128), lambda i: (0, 0)),
                   pl.BlockSpec((1, D), lambda i: (0, 0))],
        out_shape=[jax.ShapeDtypeStruct((R, D), F32), jax.ShapeDtypeStruct((8, 128), F32),
                   jax.ShapeDtypeStruct((1, D), F32)],
        compiler_params=_cparams(("arbitrary",)), name="final_loss")(x, target, fg)


NC = 24


def mods_fwd(call, ada_w, ada_b):
    cols = ada_w.shape[2]

    def kern(c_ref, w_ref, b_ref, o_ref):
        o_ref[...] = jnp.dot(_silu(c_ref[...]), w_ref[...], precision=HI, preferred_element_type=F32) + b_ref[...]

    return pl.pallas_call(
        kern, grid=(2,),
        in_specs=[pl.BlockSpec((NC, D), lambda l: (0, 0)), pl.BlockSpec((None, D, cols), lambda l: (l, 0, 0)),
                  pl.BlockSpec((None, 1, cols), lambda l: (l, 0, 0))],
        out_specs=pl.BlockSpec((None, NC, cols), lambda l: (l, 0, 0)),
        out_shape=jax.ShapeDtypeStruct((2, NC, cols), F32),
        compiler_params=_cparams(("parallel",)), name="mods_fwd")(call, ada_w, ada_b)


def ada_bwd(call, ada_w, dm):
    cols = ada_w.shape[2]

    def kern(c_ref, w_ref, dm_ref, gw_ref, dc_ref):
        l = pl.program_id(0)
        gw_ref[...] = lax.dot_general(_silu(c_ref[...]), dm_ref[...], TN, precision=HI, preferred_element_type=F32)
        part = lax.dot_general(dm_ref[16:24, :], w_ref[...], NT, precision=HI, preferred_element_type=F32)
        cc = c_ref[16:17, :]
        sg = jax.nn.sigmoid(cc)
        _acc(dc_ref, part * (sg * (1.0 + cc * (1.0 - sg))), l == 0)

    return pl.pallas_call(
        kern, grid=(2,),
        in_specs=[pl.BlockSpec((NC, D), lambda l: (0, 0)), pl.BlockSpec((None, D, cols), lambda l: (l, 0, 0)),
                  pl.BlockSpec((None, NC, cols), lambda l: (l, 0, 0))],
        out_specs=[pl.BlockSpec((None, D, cols), lambda l: (l, 0, 0)), pl.BlockSpec((8, D), lambda l: (0, 0))],
        out_shape=[jax.ShapeDtypeStruct((2, D, cols), F32), jax.ShapeDtypeStruct((8, D), F32)],
        compiler_params=_cparams(("arbitrary",)), name="ada_bwd")(call, ada_w, dm)


def sum_lead(name, a, scale_last=None):
    n, r, c = a.shape
    tr = r
    for cand in (512, 256, 128, 64, 32, 16, 8):
        if r % cand == 0 and cand * c * 4 * n <= 8 * 1024 * 1024:
            tr = cand
            break

    def kern(a_ref, o_ref):
        acc = a_ref[0].astype(F32)
        for k in range(1, n):
            acc = acc + a_ref[k].astype(F32)
        o_ref[...] = acc

    return pl.pallas_call(kern, grid=(r // tr,), in_specs=[pl.BlockSpec((n, tr, c), lambda i: (0, i, 0))],
                          out_specs=pl.BlockSpec((tr, c), lambda i: (i, 0)),
                          out_shape=jax.ShapeDtypeStruct((r, c), F32),
                          compiler_params=_cparams(("parallel",)), name=name)(a)


def add_pairs(name, hs, got, half):
    _, _, r, c = hs.shape

    def kern(half_ref, a_ref, b_ref, o_ref):
        o_ref[...] = (a_ref[...].astype(F32) + b_ref[...].astype(F32)).astype(BF)

    spec = pl.BlockSpec((None, r, c), lambda j, h: (j, 0, 0))
    grid_spec = pltpu.PrefetchScalarGridSpec(
        num_scalar_prefetch=1, grid=(4,),
        in_specs=[pl.BlockSpec((None, None, r, c), lambda j, h: (h[0], j, 0, 0)), spec], out_specs=spec)
    return pl.pallas_call(kern, grid_spec=grid_spec, out_shape=jax.ShapeDtypeStruct(got.shape, BF),
                          compiler_params=_cparams(("parallel",)), name=name)(half, hs, got)


def sum_slabs(name, land, own, where):
    _, r, c = land.shape
    tr = r
    for cand in (512, 256, 128, 64, 32, 16):
        if r % cand == 0 and cand * c * 16 <= 4 * 1024 * 1024:
            tr = cand
            break

    def kern(where_ref, full_ref, land_ref, own_ref, o_ref):
        me = where_ref[0]
        acc = None
        for k in range(4):
            t = jnp.where(me == k, own_ref[k], land_ref[k]).astype(F32)
            acc = t if acc is None else acc + t
        o_ref[...] = acc

    spec = pl.BlockSpec((4, tr, c), lambda i, m: (0, i, 0))
    grid_spec = pltpu.PrefetchScalarGridSpec(
        num_scalar_prefetch=1, grid=(r // tr,), in_specs=[pl.BlockSpec(memory_space=pl.ANY), spec, spec],
        out_specs=pl.BlockSpec((None, tr, c), lambda i, m: (m[1], i, 0)))
    return pl.pallas_call(kern, grid_spec=grid_spec, out_shape=jax.ShapeDtypeStruct((2, r, c), F32),
                          input_output_aliases={1: 0}, compiler_params=_cparams(("parallel",)),
                          name=name)(where, lax.empty((2, r, c), F32), land, own)


def adamw(name, w, g, m, v):
    r, c = w.shape
    tr = r
    for cand in (512, 256, 128, 64, 32, 16, 8):
        if r % cand == 0 and cand * c * 4 <= 2 * 1024 * 1024:
            tr = cand
            break
    c1 = 1.0 / (1.0 - ADAM_B1 ** ADAM_STEP)
    c2 = 1.0 / (1.0 - ADAM_B2 ** ADAM_STEP)

    def kern(w_ref, g_ref, m_ref, v_ref, d_ref, mo_ref, vo_ref):
        gv = g_ref[...]
        mn = ADAM_B1 * m_ref[...] + (1.0 - ADAM_B1) * gv
        vn = ADAM_B2 * v_ref[...] + (1.0 - ADAM_B2) * (gv * gv)
        d_ref[...] = -ADAM_LR * ((mn * c1) / (jnp.sqrt(vn * c2) + ADAM_EPS) + ADAM_WD * w_ref[...])
        mo_ref[...] = mn
        vo_ref[...] = vn

    spec = pl.BlockSpec((tr, c), lambda i: (i, 0))
    shp = jax.ShapeDtypeStruct((r, c), F32)
    return pl.pallas_call(kern, grid=(r // tr,), in_specs=[spec] * 4, out_specs=[spec] * 3, out_shape=[shp] * 3,
                          compiler_params=_cparams(("parallel",)), name=name)(w, g, m, v)


def all_gather8(name, xs):
    m_per, n = xs.shape

    def body(x_ref, out_ref, send_sems, recv_sems, local_sem):
        x, y, c = lax.axis_index("x"), lax.axis_index("y"), lax.axis_index("c")
        me, sibling = (x, y, c), (x, y, 1 - c)
        chips = [(1 - x, y), (x, 1 - y), (1 - x, 1 - y)]

        def rows(px, py, pc):
            return out_ref.at[pl.ds((4 * px + 2 * py + pc) * m_per, m_per), :]

        def copy(k, block, to, src=None):
            return pltpu.make_async_remote_copy(
                src_ref=rows(*block) if src is None else src, dst_ref=rows(*block),
                send_sem=send_sems.at[k], recv_sem=recv_sems.at[k], device_id=to, device_id_type=MESH)

        mine = pltpu.make_async_copy(x_ref, rows(*me), local_sem)
        mine.start()
        first = [copy(0, me, sibling, src=x_ref)]
        first += [copy(1 + j, me, (*chip, c), src=x_ref) for j, chip in enumerate(chips)]
        for cp in first:
            cp.start()
        passed = [copy(4 + j, (*chip, c), sibling) for j, chip in enumerate(chips)]
        for j, chip in enumerate(chips):
            copy(1 + j, (*chip, c), me).wait_recv()
            passed[j].start()
        copy(0, sibling, me).wait_recv()
        for j, chip in enumerate(chips):
            copy(4 + j, (*chip, 1 - c), me).wait_recv()
        for cp in first + passed:
            cp.wait_send()
        mine.wait()

    return pl.pallas_call(
        body, out_shape=jax.ShapeDtypeStruct((8 * m_per, n), xs.dtype),
        in_specs=[pl.BlockSpec(memory_space=pltpu.VMEM)], out_specs=pl.BlockSpec(memory_space=pltpu.VMEM),
        scratch_shapes=[pltpu.SemaphoreType.DMA((7,)), pltpu.SemaphoreType.DMA((7,)), pltpu.SemaphoreType.DMA],
        compiler_params=pltpu.CompilerParams(vmem_limit_bytes=VMEM_LIMIT), name=name)(xs)


def sibling_merge(name, fulls):
    n = len(fulls)

    def body(*refs):
        buf = refs[n:2 * n]
        send_sems, recv_sems = refs[2 * n], refs[2 * n + 1]
        c = lax.axis_index("c")
        sibling = (lax.axis_index("x"), lax.axis_index("y"), 1 - c)
        kw = dict(device_id=sibling, device_id_type=MESH)
        sends = [pltpu.make_async_remote_copy(src_ref=buf[a].at[c], dst_ref=buf[a].at[c], send_sem=send_sems.at[a],
                                              recv_sem=recv_sems.at[a], **kw) for a in range(n)]
        recvs = [pltpu.make_async_remote_copy(src_ref=buf[a].at[c], dst_ref=buf[a].at[1 - c],
                                              send_sem=send_sems.at[a], recv_sem=recv_sems.at[a], **kw)
                 for a in range(n)]
        for cp in sends:
            cp.start()
        for cp in recvs:
            cp.wait_recv()
        for cp in sends:
            cp.wait_send()

    anyspec = pl.BlockSpec(memory_space=pl.ANY)
    return pl.pallas_call(
        body, out_shape=[jax.ShapeDtypeStruct(s.shape, s.dtype) for s in fulls],
        in_specs=[anyspec] * n, out_specs=[anyspec] * n, input_output_aliases={a: a for a in range(n)},
        scratch_shapes=[pltpu.SemaphoreType.DMA((n,)), pltpu.SemaphoreType.DMA((n,))], name=name)(*fulls)


def place_own(name, land, src, chip):
    c = src.shape[-1]
    r = src.size // c
    tr = r
    for cand in (1024, 512, 256, 128, 64, 32, 16):
        if r % cand == 0 and cand * c * 2 <= 2 * 1024 * 1024:
            tr = cand
            break

    def kern(chip_ref, land_ref, src_ref, out_ref):
        out_ref[...] = src_ref[...]

    grid_spec = pltpu.PrefetchScalarGridSpec(
        num_scalar_prefetch=1, grid=(r // tr,),
        in_specs=[pl.BlockSpec(memory_space=pl.ANY), pl.BlockSpec((tr, c), lambda i, m: (i, 0))],
        out_specs=pl.BlockSpec((None, tr, c), lambda i, m: (m[0], i, 0)))
    out = pl.pallas_call(kern, grid_spec=grid_spec, out_shape=jax.ShapeDtypeStruct((4, r, c), land.dtype),
                         input_output_aliases={1: 0}, compiler_params=_cparams(("parallel",)),
                         name=name)(chip, land.reshape(4, r, c), src.reshape(r, c))
    return out.reshape(land.shape)


def _half_copies(src, land, send_sems, recv_sems):
    c = lax.axis_index("c")
    sibling = (lax.axis_index("x"), lax.axis_index("y"), 1 - c)
    pairs = []
    for a in range(len(src)):
        cp = pltpu.make_async_remote_copy(src_ref=src[a].at[1 - c], dst_ref=land[a], send_sem=send_sems.at[a],
                                          recv_sem=recv_sems.at[a], device_id=sibling, device_id_type=MESH)
        pairs.append((cp, cp))
    return pairs


def _chip_copies(src, land, send_sems, recv_sems, scatter):
    x, y, c = lax.axis_index("x"), lax.axis_index("y"), lax.axis_index("c")
    me = 2 * x + y
    pairs = []
    for a in range(len(src)):
        for j, (px, py) in enumerate([(1 - x, y), (x, 1 - y), (1 - x, 1 - y)]):
            to = 2 * px + py
            out = src[a].at[to] if scatter else src[a]
            kw = dict(send_sem=send_sems.at[3 * a + j], recv_sem=recv_sems.at[3 * a + j], device_id=(px, py, c),
                      device_id_type=MESH)
            pairs.append((pltpu.make_async_remote_copy(src_ref=out, dst_ref=land[a].at[me], **kw),
                          pltpu.make_async_remote_copy(src_ref=out, dst_ref=land[a].at[to], **kw)))
    return pairs


_HBM = pl.BlockSpec(memory_space=pltpu.HBM)
_SEM = pl.BlockSpec(memory_space=pltpu.SEMAPHORE)


GATHER = (functools.partial(_chip_copies, scatter=False), 3)
SCATTER = (functools.partial(_chip_copies, scatter=True), 3)
TO_SIBLING = (_half_copies, 1)


def _landing(shapes, dtype):
    return [lax.empty(tuple(s), dtype) for s in shapes]


def exchange_start(name, groups, plan):
    copies, per = plan
    sizes = [len(s) for s, _ in groups]
    flat = [a for s, l in groups for a in list(s) + list(l)]
    ng = len(groups)

    def body(*refs):
        ins, outs = refs[:len(flat)], refs[len(flat):]
        off = 0
        for g, n in enumerate(sizes):
            src, land = ins[off:off + n], ins[off + n:off + 2 * n]
            off += 2 * n
            for send, _ in copies(src, land, outs[2 * g], outs[2 * g + 1]):
                send.start()
        outs[-1][...] = jnp.zeros_like(outs[-1])

    out_shape = []
    for n in sizes:
        out_shape += [pltpu.SemaphoreType.DMA((per * n,)), pltpu.SemaphoreType.DMA((per * n,))]
    out_shape += [pltpu.HBM(a.shape, a.dtype) for a in flat] + [jax.ShapeDtypeStruct((8, 128), F32)]
    res = pl.pallas_call(
        body, out_shape=tuple(out_shape), in_specs=[_HBM] * len(flat),
        out_specs=tuple([_SEM] * (2 * ng) + [_HBM] * len(flat) + [pl.BlockSpec(memory_space=pltpu.VMEM)]),
        input_output_aliases={k: 2 * ng + k for k in range(len(flat))},
        compiler_params=pltpu.CompilerParams(has_side_effects=pltpu.SideEffectType.DATAFLOW_SIDE_EFFECTING),
        name=name)(*[pltpu.with_memory_space_constraint(a, pltpu.HBM) for a in flat])
    handles, off = [], 2 * ng
    for g, n in enumerate(sizes):
        handles.append((res[2 * g], res[2 * g + 1], list(res[off:off + n]), list(res[off + n:off + 2 * n])))
        off += 2 * n
    return handles, res[-1]


def exchange_wait(name, handle, after, plan):
    send_sems, recv_sems, srcs, lands = handle
    n = len(srcs)

    def body(*refs):
        src, land = refs[:n], refs[n:2 * n]
        for send, recv in plan[0](src, land, refs[2 * n], refs[2 * n + 1]):
            send.wait_send()
            recv.wait_recv()

    res = pl.pallas_call(
        body, out_shape=tuple(pltpu.HBM(a.shape, a.dtype) for a in srcs + lands),
        in_specs=[_HBM] * (2 * n) + [_SEM, _SEM, pl.BlockSpec(memory_space=pl.ANY)],
        out_specs=tuple([_HBM] * (2 * n)), input_output_aliases={k: k for k in range(2 * n)},
        compiler_params=pltpu.CompilerParams(has_side_effects=pltpu.SideEffectType.DATAFLOW_SIDE_EFFECTING),
        name=name)(*srcs, *lands, send_sems, recv_sems, after)
    return list(res[:n]), list(res[n:])


def _rope_tables(d_rot, reps):
    rows = L // GRID_W
    row = np.repeat(np.arange(rows), GRID_W).astype(np.float32)
    col = np.tile(np.arange(GRID_W), rows).astype(np.float32)
    d_axis = d_rot // 2
    inv = (ROPE_THETA ** (-np.arange(0, d_axis, 2, dtype=np.float32) / d_axis)).astype(np.float32)
    ang = np.concatenate([row[:, None] * inv, col[:, None] * inv], axis=-1).astype(np.float32)
    cos, sin = np.cos(ang).astype(np.float32), np.sin(ang).astype(np.float32)
    c = np.repeat(cos, 2, axis=-1)
    s = np.stack([-sin, sin], axis=-1).reshape(L, d_rot)
    c = np.concatenate([np.ones((LC, d_rot), np.float32), c], axis=0)
    s = np.concatenate([np.zeros((LC, d_rot), np.float32), s], axis=0)
    return np.tile(c, (1, reps)), np.tile(s, (1, reps))


def _group_consts():
    g = np.arange(512) // 64
    avg = (g[:, None] == g[None, :]).astype(np.float32) / 64.0
    masks = (np.arange(8)[:, None] == g[None, :]).astype(np.float32).reshape(8, 1, 512)
    return jnp.asarray(avg, BF), jnp.asarray(masks)


def _pack(items):
    flat = jnp.concatenate([a.reshape(-1).astype(F32) for a in items])
    n = flat.shape[0]
    rows = -(-n // D)
    rows = -(-rows // 8) * 8
    return jnp.pad(flat, (0, rows * D - n)).reshape(rows, D)


def _unpack(buf, shapes):
    lead = buf.shape[:-2]
    flat = buf.reshape(lead + (-1,))
    out, off = [], 0
    for shp in shapes:
        n = int(np.prod(shp))
        out.append(flat[..., off:off + n].reshape(lead + tuple(shp)))
        off += n
    return out


def _arrive(prm, key, after):
    if callable(prm[key]):
        prm[key](after)
    return prm[key]


def _layer_fwd(i, x, mods, prm, consts):
    sv = {}
    sv["x0"] = x
    h = modnorm_fwd(f"norm1_fwd_{i}", x, mods, prm["norm1_g"], 0, 1)
    sv["h"] = h
    p = proj_in(f"proj_in_{i}", h, _arrive(prm, "w_in", h))
    sv["p"] = p
    if i == 0:
        q, kv, m2 = even_tok_fwd(p, consts["cos_e"], consts["sin_e"], prm["gq"], prm["gk"], prm["gs"],
                                 prm["sgu_w"], prm["sgu_b"], consts["avg"], consts["masks"])
        o, lse = attn_fwd("attn_fwd_0", q, kv, GQA_HEADS)
        sv.update(q=q, kv=kv)
    else:
        q, kv, y = odd_tok_fwd(p, consts["cos_o"], consts["sin_o"], prm["gq"], prm["gkv"], prm["wq"], prm["wkk"],
                               prm["wkv"], consts["spread"])
        o, lse = attn_fwd("attn_fwd_1", q, kv, MLA_HEADS)
        z, m2 = conf_fwd(y, prm["conv_w"], prm["conv_b"], prm["ln_g"], prm["ln_b"])
        sv.update(q=q, kv=kv, y=y, z=z)
    sv.update(o=o, lse=lse, m2=m2)
    x1, y1 = proj_out(f"proj_out_{i}", o, m2, _arrive(prm, "w_out", o), x, mods, 2)
    sv.update(x1=x1, y1=y1)
    h2 = modnorm_fwd(f"norm2_fwd_{i}", x1, mods, prm["norm2_g"], 3, 4)
    a, f = mlp_up(f"mlp_up_{i}", h2, prm["w1"])
    x2, y2 = mlp_down(f"mlp_down_{i}", f, prm["w2"], x1, mods, 5)
    sv.update(h2=h2, a=a, f=f, y2=y2)
    return x2, sv


def _layer_bwd(i, dx, sv, mods, prm, consts, hook, entry):
    gr = {}
    dy2, dg2 = gate_bwd(f"gate2_bwd_{i}", dx, sv["y2"], mods, 5, after=entry)
    da = mlp_bwd_da(f"mlp_bwd_da_{i}", dy2, prm["w2"], sv["a"])
    tiles8 = [(h, j) for h in range(2) for j in range(4)]
    gr["w1"] = mm_tn(f"grad_w1_{i}", sv["h2"], da, tiles8, 512, D).reshape(2, 4, 512, D)
    gr["w2"] = mm_tn(f"grad_w2_{i}", sv["f"], dy2, [(2 * j + h, 0) for h in range(2) for j in range(4)],
                     512, D).reshape(2, 4, 512, D)
    dh2 = mlp_bwd_dh(f"mlp_bwd_dh_{i}", da, prm["w1"])
    dx1, dsh2, dsc2, gr["norm2_g"] = modnorm_bwd(f"norm2_bwd_{i}", sv["x1"], dh2, dx, mods, prm["norm2_g"], 3, 4,
                                                 after=hook(f"{i}:mlp", gr, dh2))
    dy1, dg1 = gate_bwd(f"gate1_bwd_{i}", dx1, sv["y1"], mods, 2)
    dcat = mm_nt(f"proj_out_bwd_{i}", dy1, prm["w_out"], after=hook(f"{i}:mid", gr, dy1))
    t4 = [(2 * j + h, 0) for h in range(2) for j in range(2)]
    go = mm_tn(f"grad_wout_a_{i}", sv["o"], dy1, t4, 128, D).reshape(2, 2, 128, D)
    gm = mm_tn(f"grad_wout_b_{i}", sv["m2"], dy1, t4, 128, D).reshape(2, 2, 128, D)
    gr["w_out"] = jnp.concatenate([go, gm], axis=1)
    if i == 0:
        dq, dkv = attn_bwd("attn_bwd_0", sv["q"], sv["kv"], sv["o"], dcat, sv["lse"], GQA_HEADS)
        dp, gr["gq"], gr["gk"], gr["gs"], gr["sgu_w"], gr["sgu_b"] = even_tok_bwd(
            sv["p"], dq, dkv, dcat, consts["cos_e"], consts["sin_e"], prm["gq"], prm["gk"],
            prm["gs"], prm["sgu_w"], prm["sgu_b"], consts["avg"], consts["masks"])
    else:
        dq, dkv = attn_bwd("attn_bwd_1", sv["q"], sv["kv"], sv["o"], dcat, sv["lse"], MLA_HEADS)
        dz, gr["ln_g"], gr["ln_b"], gr["conv_b"] = conf_bwd_ln(sv["z"], dcat, prm["ln_g"], prm["ln_b"])
        dyc, gr["conv_w"] = conf_bwd_conv(sv["y"], dz, prm["conv_w"])
        dp, gr["gq"], gr["gkv"], gr["wq"], gr["wkk"], gr["wkv"] = odd_tok_bwd(
            sv["p"], dq, dkv, dyc, consts["cos_o"], consts["sin_o"], prm["gq"], prm["gkv"], prm["wq"], prm["wkk"],
            prm["wkv"], consts["spread"])
    n_in = prm["w_in"].shape[1]
    gr["w_in"] = mm_tn(f"grad_win_{i}", sv["h"], dp, [(0, 0), (1, 0)], 512, n_in)
    dh = mm_nt(f"proj_in_bwd_{i}", dp, prm["w_in"])
    dx0, dsh1, dsc1, gr["norm1_g"] = modnorm_bwd(f"norm1_bwd_{i}", sv["x0"], dh, dx1, mods, prm["norm1_g"], 0, 1)
    dmods = jnp.concatenate([dsh1, dsc1, dg1, dsh2, dsc2, dg2], axis=1)
    return dx0, dmods, gr, hook(f"{i}:end", gr, dx0)


def local_step(xcat, target, mods, prms, final_g, hook=lambda point, grads, fresh: None):
    avg, masks = _group_consts()
    cos_e, sin_e = _rope_tables(64, 8)
    ck, sk = _rope_tables(32, 1)
    one64, zero64 = np.ones((SEQ, 64), np.float32), np.zeros((SEQ, 64), np.float32)
    one96, zero96 = np.ones((SEQ, 96), np.float32), np.zeros((SEQ, 96), np.float32)
    cos_o = np.concatenate([np.tile(np.concatenate([one64, ck], axis=1), (1, 8)), ck, one96], axis=1)
    sin_o = np.concatenate([np.tile(np.concatenate([zero64, sk], axis=1), (1, 8)), sk, zero96], axis=1)
    lane = np.arange(768)
    spread = np.zeros((128, 768), np.float32)
    spread[lane % 96 - 64, lane] = (lane % 96 >= 64)
    consts = dict(avg=avg, masks=masks, cos_e=jnp.asarray(cos_e), sin_e=jnp.asarray(sin_e),
                  cos_o=jnp.asarray(cos_o), sin_o=jnp.asarray(sin_o), spread=jnp.asarray(spread, BF))
    x = xcat
    saved = []
    for i in range(2):
        x, sv = _layer_fwd(i, x, mods[i], prms[i], consts)
        saved.append(sv)
    dx, loss, dfg = final_loss(x, target, final_g)
    dmods, grads = [None, None], [None, None]
    entry = None
    for i in (1, 0):
        dx, dmods[i], grads[i], entry = _layer_bwd(i, dx, saved[i], mods[i], prms[i], consts, hook, entry)
    return loss, dx, dmods, grads, dfg


def _row(v):
    return v.reshape(1, -1).astype(F32)


def odd_in_params(od_w_in, w_uq, w_ukv):
    od = jnp.concatenate([od_w_in[:, 0:416], jnp.zeros((D, 96), od_w_in.dtype), od_w_in[:, 416:OD_IN]], axis=1)
    ukv = w_ukv.reshape(128, 8, 128)
    wkk = jnp.pad(ukv[:, :, :64], ((0, 0), (0, 0), (0, 32))).reshape(128, 768)
    return dict(w_in=od, wq=w_uq, wkk=wkk, wkv=ukv[:, :, 64:].reshape(128, 512))


def small_params(small):
    p0 = dict(norm1_g=_row(small["norm1_g"][0]), norm2_g=_row(small["norm2_g"][0]),
              gq=jnp.tile(_row(small["ev_q_norm_g"]), (1, 8)), gk=jnp.tile(_row(small["ev_k_norm_g"]), (1, 2)),
              gs=_row(small["ev_sgu_norm_g"]), sgu_w=small["ev_sgu_w"].reshape(8, 128, 128).astype(F32),
              sgu_b=small["ev_sgu_b"].reshape(8, 128, 1).astype(F32))
    p1 = dict(norm1_g=_row(small["norm1_g"][1]), norm2_g=_row(small["norm2_g"][1]),
              gq=_row(small["od_q_norm_g"]), gkv=_row(small["od_kv_norm_g"]),
              conv_w=jnp.pad(small["od_conv_w"].reshape(CONV_K, 512).astype(F32), ((0, 1), (0, 0))),
              conv_b=_row(small["od_conv_b"]), ln_g=_row(small["od_ln_g"]), ln_b=_row(small["od_ln_b"]))
    return [p0, p1]


def prep_params(ev_w_in, od_w_in, w_out, w1, w2, w_uq, w_ukv, small):
    p0, p1 = small_params(small)
    p0.update(w_in=ev_w_in, w_out=w_out[0], w1=w1[0], w2=w2[0])
    p1.update(odd_in_params(od_w_in, w_uq, w_ukv), w_out=w_out[1], w1=w1[1], w2=w2[1])
    return [p0, p1]


def small_grads_natural(grads, dfg):
    g0, g1 = grads
    return dict(
        norm1_g=jnp.concatenate([g0["norm1_g"], g1["norm1_g"]], axis=0),
        norm2_g=jnp.concatenate([g0["norm2_g"], g1["norm2_g"]], axis=0),
        ev_q_norm_g=g0["gq"].reshape(8, 64).sum(0).reshape(1, 64),
        ev_k_norm_g=g0["gk"].reshape(2, 64).sum(0).reshape(1, 64),
        ev_sgu_norm_g=g0["gs"].reshape(1, 8, 64),
        ev_sgu_w=g0["sgu_w"].reshape(1, 8, 128, 128),
        ev_sgu_b=g0["sgu_b"].reshape(1, 8, 128),
        od_q_norm_g=g1["gq"].reshape(1, 256),
        od_kv_norm_g=g1["gkv"].reshape(1, 128),
        od_conv_w=g1["conv_w"][0:CONV_K].reshape(1, CONV_K, 512),
        od_conv_b=g1["conv_b"].reshape(1, 512),
        od_ln_g=g1["ln_g"].reshape(1, 512),
        od_ln_b=g1["ln_b"].reshape(1, 512),
        final_g=dfg.reshape(D))


def layer_grads_hs(i, g, part="all"):
    def cols(a):
        k, n = a.shape
        return a.reshape(2, k // 2, 4, n // 4).transpose(0, 2, 1, 3).astype(BF)

    mlp = [(("mlp_w1", i), g["w1"]), (("mlp_w2", i), g["w2"])]
    if part == "mlp":
        return mlp
    rest = [(("w_out", i), g["w_out"])]
    if i == 0:
        rest.append((("ev_w_in", 0), cols(g["w_in"].reshape(D, EV_IN))))
    else:
        od = g["w_in"].reshape(D, OD_PAD)
        od = jnp.concatenate([od[:, 0:416], od[:, 512:OD_PAD]], axis=1)
        ukv = jnp.concatenate([g["wkk"].reshape(128, 8, 96)[:, :, :64], g["wkv"].reshape(128, 8, 64)], axis=2)
        rest += [(("od_w_in", 0), cols(od)), (("od_w_uq", 0), cols(g["wq"])),
                 (("od_w_ukv", 0), cols(ukv.reshape(128, 1024)))]
    return rest if part == "rest" else mlp + rest


def big_grads_hs(grads):
    d = dict(layer_grads_hs(0, grads[0]) + layer_grads_hs(1, grads[1]))
    return dict(ev_w_in=d[("ev_w_in", 0)], od_w_in=d[("od_w_in", 0)], od_w_uq=d[("od_w_uq", 0)],
                od_w_ukv=d[("od_w_ukv", 0)], w_out=[d[("w_out", 0)], d[("w_out", 1)]],
                mlp_w1=[d[("mlp_w1", 0)], d[("mlp_w1", 1)]], mlp_w2=[d[("mlp_w2", 0)], d[("mlp_w2", 1)]])


def grads_to_natural(grads, dfg):
    out = small_grads_natural(grads, dfg)
    hs = big_grads_hs(grads)

    def from_cols(a):
        return a.transpose(0, 2, 1, 3).reshape(2 * a.shape[2], 4 * a.shape[3])

    def from_rows(a):
        return a.transpose(1, 0, 2, 3).reshape(8 * a.shape[2], a.shape[3])

    out["ev_w_in"] = from_cols(hs["ev_w_in"])[None]
    out["od_w_in"] = from_cols(hs["od_w_in"])[None]
    out["od_w_uq"] = from_cols(hs["od_w_uq"])[None]
    out["od_w_ukv"] = from_cols(hs["od_w_ukv"])[None]
    out["w_out"] = jnp.stack([from_rows(a) for a in hs["w_out"]])
    out["mlp_w1"] = jnp.stack([from_cols(a) for a in hs["mlp_w1"]])
    out["mlp_w2"] = jnp.stack([from_rows(a) for a in hs["mlp_w2"]])
    return out


WEIGHT_NAMES = ['c_ctx', 'ada_w', 'ada_b', 'norm1_g', 'norm2_g', 'w_out', 'mlp_w1', 'mlp_w2', 'ev_w_in',
                'ev_q_norm_g', 'ev_k_norm_g', 'ev_sgu_norm_g', 'ev_sgu_w', 'ev_sgu_b', 'od_w_in', 'od_q_norm_g',
                'od_kv_norm_g', 'od_w_uq', 'od_w_ukv', 'od_conv_w', 'od_conv_b', 'od_ln_g', 'od_ln_b', 'final_g']
REPL_SMALL = ['norm1_g', 'norm2_g', 'ev_q_norm_g', 'ev_k_norm_g', 'ev_sgu_norm_g', 'ev_sgu_w', 'ev_sgu_b',
              'od_kv_norm_g', 'final_g']
SHARD_SMALL = ['od_q_norm_g', 'od_conv_w', 'od_conv_b', 'od_ln_g', 'od_ln_b']
BIG = ['w_out', 'mlp_w1', 'mlp_w2', 'ev_w_in', 'od_w_in', 'od_w_uq', 'od_w_ukv']


def _gather_last(parts):
    return jnp.concatenate([parts[k] for k in range(4)], axis=-1)


class _Reduce:
    def __init__(self, tag, named, half, where):
        self.tag, self.half, self.where = tag, half, where
        self.names, self.hs = zip(*named)
        self.hs = list(self.hs)

    def to_sibling(self):
        lands = [lax.empty(a.shape[1:], BF) for a in self.hs]
        (self.h1,), token = exchange_start(f"rs_sibling_start_{self.tag}", [(self.hs, lands)], TO_SIBLING)
        return token

    def to_chips(self, after):
        hs, got = exchange_wait(f"rs_sibling_wait_{self.tag}", self.h1, after, TO_SIBLING)
        pair = [add_pairs(f"rs_add_{self.tag}_{k}", a, b, self.half) for k, (a, b) in enumerate(zip(hs, got))]
        lands = [lax.empty(p.shape, BF) for p in pair]
        (self.h2,), token = exchange_start(f"rs_chips_start_{self.tag}", [(pair, lands)], SCATTER)
        return token

    def finish(self, after):
        pair, land = exchange_wait(f"rs_chips_wait_{self.tag}", self.h2, after, SCATTER)
        return [sum_slabs(f"rs_sum_{self.tag}_{k}", l, p, self.where) for k, (l, p) in enumerate(zip(land, pair))]


def kernel(x, c, ctx, c_ctx, ada_w, ada_b, norm1_g, norm2_g, w_out, mlp_w1, mlp_w2, ev_w_in, ev_q_norm_g, ev_k_norm_g, ev_sgu_norm_g, ev_sgu_w, ev_sgu_b, od_w_in, od_q_norm_g, od_kv_norm_g, od_w_uq, od_w_ukv, od_conv_w, od_conv_b, od_ln_g, od_ln_b, final_g, loss_target, m_c_ctx, m_ada_w, m_ada_b, m_norm1_g, m_norm2_g, m_w_out, m_mlp_w1, m_mlp_w2, m_ev_w_in, m_ev_q_norm_g, m_ev_k_norm_g, m_ev_sgu_norm_g, m_ev_sgu_w, m_ev_sgu_b, m_od_w_in, m_od_q_norm_g, m_od_kv_norm_g, m_od_w_uq, m_od_w_ukv, m_od_conv_w, m_od_conv_b, m_od_ln_g, m_od_ln_b, m_final_g, v_c_ctx, v_ada_w, v_ada_b, v_norm1_g, v_norm2_g, v_w_out, v_mlp_w1, v_mlp_w2, v_ev_w_in, v_ev_q_norm_g, v_ev_k_norm_g, v_ev_sgu_norm_g, v_ev_sgu_w, v_ev_sgu_b, v_od_w_in, v_od_q_norm_g, v_od_kv_norm_g, v_od_w_uq, v_od_w_ukv, v_od_conv_w, v_od_conv_b, v_od_ln_g, v_od_ln_b, v_final_g):
    w = dict(c_ctx=c_ctx, ada_w=ada_w, ada_b=ada_b, norm1_g=norm1_g, norm2_g=norm2_g, w_out=w_out, mlp_w1=mlp_w1,
             mlp_w2=mlp_w2, ev_w_in=ev_w_in, ev_q_norm_g=ev_q_norm_g, ev_k_norm_g=ev_k_norm_g,
             ev_sgu_norm_g=ev_sgu_norm_g, ev_sgu_w=ev_sgu_w, ev_sgu_b=ev_sgu_b, od_w_in=od_w_in,
             od_q_norm_g=od_q_norm_g, od_kv_norm_g=od_kv_norm_g, od_w_uq=od_w_uq, od_w_ukv=od_w_ukv,
             od_conv_w=od_conv_w, od_conv_b=od_conv_b, od_ln_g=od_ln_g, od_ln_b=od_ln_b, final_g=final_g)
    mom = dict(c_ctx=m_c_ctx, ada_w=m_ada_w, ada_b=m_ada_b, norm1_g=m_norm1_g, norm2_g=m_norm2_g, w_out=m_w_out,
               mlp_w1=m_mlp_w1, mlp_w2=m_mlp_w2, ev_w_in=m_ev_w_in, ev_q_norm_g=m_ev_q_norm_g,
               ev_k_norm_g=m_ev_k_norm_g, ev_sgu_norm_g=m_ev_sgu_norm_g, ev_sgu_w=m_ev_sgu_w, ev_sgu_b=m_ev_sgu_b,
               od_w_in=m_od_w_in, od_q_norm_g=m_od_q_norm_g, od_kv_norm_g=m_od_kv_norm_g, od_w_uq=m_od_w_uq,
               od_w_ukv=m_od_w_ukv, od_conv_w=m_od_conv_w, od_conv_b=m_od_conv_b, od_ln_g=m_od_ln_g,
               od_ln_b=m_od_ln_b, final_g=m_final_g)
    var = dict(c_ctx=v_c_ctx, ada_w=v_ada_w, ada_b=v_ada_b, norm1_g=v_norm1_g, norm2_g=v_norm2_g, w_out=v_w_out,
               mlp_w1=v_mlp_w1, mlp_w2=v_mlp_w2, ev_w_in=v_ev_w_in, ev_q_norm_g=v_ev_q_norm_g,
               ev_k_norm_g=v_ev_k_norm_g, ev_sgu_norm_g=v_ev_sgu_norm_g, ev_sgu_w=v_ev_sgu_w, ev_sgu_b=v_ev_sgu_b,
               od_w_in=v_od_w_in, od_q_norm_g=v_od_q_norm_g, od_kv_norm_g=v_od_kv_norm_g, od_w_uq=v_od_w_uq,
               od_w_ukv=v_od_w_ukv, od_conv_w=v_od_conv_w, od_conv_b=v_od_conv_b, od_ln_g=v_od_ln_g,
               od_ln_b=v_od_ln_b, final_g=v_final_g)
    xi, yi, ci = lax.axis_index("x"), lax.axis_index("y"), lax.axis_index("c")
    chip = 2 * xi + yi
    dev = 2 * chip + ci

    shard_shapes = [w[n].shape for n in SHARD_SMALL]
    g0 = all_gather8("ag_small", _pack([c] + [w[n] for n in SHARD_SMALL]))
    g0 = g0.reshape(8, -1, D)
    parts = _unpack(g0, [c.shape] + shard_shapes)
    c_all = parts[0].reshape(16, D)
    small_full = {n: _gather_last(p[0::2]) for n, p in zip(SHARD_SMALL, parts[1:])}
    call = jnp.concatenate([c_all, c_ctx.reshape(1, D), jnp.zeros((NC - 17, D), F32)], axis=0)

    cols = ada_w.shape[2]
    ada_b_sh = lax.dynamic_slice(ada_b, (0, chip * cols), (2, cols)).reshape(2, 1, cols)
    mt = mods_fwd(call, ada_w, ada_b_sh)
    mt = all_gather8("ag_mods", mt.reshape(2 * NC, cols)).reshape(8, 2, NC, cols)
    table = mt[0::2].transpose(1, 2, 0, 3).reshape(2, NC, 4 * cols)
    mods = []
    for i in range(2):
        lat = lax.dynamic_slice(table[i], (2 * dev, 0), (2, 4 * cols))
        mc = table[i, 16]
        mods.append(jnp.stack([mc, lat[0], mc, lat[1]]).reshape(4 * N_MOD, 1, D))

    order = [[("ev_w_in", 0)], [("w_out", 0), ("mlp_w1", 0), ("mlp_w2", 0)],
             [("od_w_in", 0), ("od_w_uq", 0), ("od_w_ukv", 0), ("w_out", 1), ("mlp_w1", 1), ("mlp_w2", 1)]]
    groups = []
    for names in order:
        srcs = [w[n][i].astype(BF) for n, i in names]
        groups.append((srcs, [lax.empty((4,) + s.shape, BF) for s in srcs]))
    groups[0][0][0], table = lax.optimization_barrier((groups[0][0][0], table))
    handles, token = exchange_start("gather_start", groups, GATHER)
    mods[0] = mods[0] + token[0, 0]
    small = {n: w[n] for n in REPL_SMALL}
    small.update(small_full)
    prms = small_params(small)

    chip1 = chip.reshape(1).astype(jnp.int32)

    def arrived(k, after):
        srcs, lands = exchange_wait(f"gather_wait_{k}", handles[k], after, GATHER)
        return [place_own(f"gather_own_{k}_{a}", l, s, chip1) for a, (l, s) in enumerate(zip(lands, srcs))]

    def arrive_ev_in(after):
        (ev,) = arrived(0, after)
        prms[0]["w_in"] = _gather_last(ev)

    def arrive_ev_rest(after):
        wo, w1, w2 = arrived(1, after)
        prms[0].update(w_out=wo.reshape(D, D), w1=w1, w2=w2)

    def arrive_od(after):
        od, uq, ukv, wo, w1, w2 = arrived(2, after)
        prms[1].update(odd_in_params(_gather_last(od), _gather_last(uq), _gather_last(ukv)),
                       w_out=wo.reshape(D, D), w1=w1, w2=w2)

    prms[0]["w_in"] = arrive_ev_in
    prms[0]["w_out"] = arrive_ev_rest
    prms[1]["w_in"] = arrive_od

    half = ci.reshape(1).astype(jnp.int32)
    where = jnp.stack([chip, ci]).astype(jnp.int32)
    red = {}

    def hook(point, g, fresh):
        if point == "1:end":
            red["l1"] = _Reduce("l1", layer_grads_hs(1, g, "all"), half, where)
            return red["l1"].to_sibling()
        if point == "0:mlp":
            red["l0_mlp"] = _Reduce("l0_mlp", layer_grads_hs(0, g, "mlp"), half, where)
            return red["l1"].to_chips(fresh) + red["l0_mlp"].to_sibling()
        if point == "0:mid":
            return red["l0_mlp"].to_chips(fresh)
        if point == "0:end":
            red["l0_rest"] = _Reduce("l0_rest", layer_grads_hs(0, g, "rest"), half, where)
            return red["l0_rest"].to_sibling()
        return None

    xcat = jnp.concatenate([ctx, x], axis=1).reshape(R, D)
    loss_p, dx, dmods, grads, dfg = local_step(xcat, loss_target.reshape(NEX * L, D), mods, prms,
                                               final_g.reshape(1, D), hook)
    grad_x = dx.reshape(NEX, SEQ, D)[:, LC:]

    sg = small_grads_natural(grads, dfg)
    dm = jnp.stack([d.reshape(4, N_MOD * D) for d in dmods])
    small_names = REPL_SMALL + SHARD_SMALL
    items = [dm[:, 1::2], dm[:, 0] + dm[:, 2]] + [sg[n] for n in small_names] + [loss_p[0:1, 0:1]]
    shapes = [a.shape for a in items]
    g1 = all_gather8("ag_grads", _pack(items))
    red["l0_rest"].to_chips(g1)
    rows1 = g1.shape[0] // 8
    g1 = g1.reshape(8, rows1, D)
    tot = _unpack(sum_lead("sum_small", g1), shapes)
    dm_lat = _unpack(g1, shapes[:1])[0]
    dm_lat = dm_lat.transpose(1, 0, 2, 3).reshape(2, 16, N_MOD * D)
    dm_all = jnp.concatenate([dm_lat, tot[1][:, None], jnp.zeros((2, NC - 17, N_MOD * D), F32)], axis=1)
    gsum = dict(zip(small_names, tot[2:2 + len(small_names)]))
    loss = tot[-1].reshape(())
    grad = {n: gsum[n].reshape(w[n].shape) for n in REPL_SMALL}
    for n in SHARD_SMALL:
        k = w[n].shape[-1]
        grad[n] = lax.dynamic_slice_in_dim(gsum[n], chip * k, k, axis=gsum[n].ndim - 1)
    grad["ada_b"] = sum_lead("sum_ada_b", dm_all.transpose(1, 0, 2).reshape(NC, 2 * N_MOD, D)).reshape(2, N_MOD * D)

    dm_sh = lax.dynamic_slice(dm_all, (0, 0, chip * cols), (2, NC, cols))
    grad["ada_w"], dcc = ada_bwd(call, ada_w, dm_sh)
    dcc = all_gather8("ag_cctx", dcc).reshape(8, 8, D)
    grad["c_ctx"] = sum_lead("sum_cctx", dcc[0::2])[0]

    names, halves = (), []
    for tag in ("l1", "l0_mlp", "l0_rest"):
        names += red[tag].names
        halves += red[tag].finish(dcc)
    full = dict(zip(names, sibling_merge("rs_sibling_merge", halves)))
    for n in BIG:
        grad[n] = jnp.stack([full[(n, i)] for i in range(w[n].shape[0])]).reshape(w[n].shape)

    delta, new_m, new_v = {}, {}, {}
    for n in ['ada_w'] + BIG:
        shp = w[n].shape
        two_d = (shp[0] * shp[1], shp[2])
        d_, m_, v_ = adamw(f"adamw_{n}", w[n].reshape(two_d), grad[n].reshape(two_d), mom[n].reshape(two_d),
                           var[n].reshape(two_d))
        delta[n], new_m[n], new_v[n] = d_.reshape(shp), m_.reshape(shp), v_.reshape(shp)
    rest = [n for n in WEIGHT_NAMES if n not in ['ada_w'] + BIG]
    rshapes = [w[n].shape for n in rest]
    d_, m_, v_ = adamw("adamw_small", _pack([w[n] for n in rest]), _pack([grad[n] for n in rest]),
                       _pack([mom[n] for n in rest]), _pack([var[n] for n in rest]))
    for dst, buf in ((delta, d_), (new_m, m_), (new_v, v_)):
        dst.update(zip(rest, _unpack(buf, rshapes)))

    return (loss, grad_x, *[grad[n] for n in WEIGHT_NAMES], *[delta[n] for n in WEIGHT_NAMES],
            *[new_m[n] for n in WEIGHT_NAMES], *[new_v[n] for n in WEIGHT_NAMES])
```

```python
import functools
import math

import numpy as np
import jax
import jax.numpy as jnp
from jax import lax
from jax.experimental import pallas as pl
from jax.experimental.pallas import tpu as pltpu

F32 = jnp.float32
BF = jnp.bfloat16
HI = lax.Precision.HIGHEST
MESH = pl.DeviceIdType.MESH

D = 1024
L = 2048
LC = 256
SEQ = L + LC
NEX = 2
R = NEX * SEQ
TB = 256
BPE = SEQ // TB
NBLK = R // TB
GRID_W = 64
FF = 4 * D
EPS = 1e-6
ROPE_THETA = 10000.0
N_MOD = 6
EV_IN = 1792
OD_IN = 1440
OD_PAD = 1536
VMEM_LIMIT = 60 * 1024 * 1024

ADAM_LR = 0.001
ADAM_B1 = 0.9
ADAM_B2 = 0.999
ADAM_EPS = 1e-08
ADAM_WD = 0.01
ADAM_STEP = 10

NT = (((1,), (1,)), ((), ()))
TN = (((0,), (0,)), ((), ()))


def _cparams(sem=None):
    return pltpu.CompilerParams(dimension_semantics=sem, vmem_limit_bytes=VMEM_LIMIT)


@jax.custom_vjp
def _mm(a, b):
    return jnp.dot(a.astype(BF), b.astype(BF), preferred_element_type=F32)


def _mm_fwd(a, b):
    return _mm(a, b), (a, b)


def _mm_bwd(res, g):
    a, b = res
    gb = g.astype(BF)
    da = lax.dot_general(gb, b.astype(BF), NT, preferred_element_type=F32)
    db = lax.dot_general(a.astype(BF), gb, TN, preferred_element_type=F32)
    return da, db


_mm.defvjp(_mm_fwd, _mm_bwd)


@jax.custom_vjp
def _swap(x):
    n = x.shape[-1]
    ax = x.ndim - 1
    lane = lax.broadcasted_iota(jnp.int32, x.shape, ax)
    return jnp.where(lane % 2 == 0, pltpu.roll(x, n - 1, ax), pltpu.roll(x, 1, ax))


_swap.defvjp(lambda x: (_swap(x), None), lambda _, g: (_swap(g),))


def _rope(x, cos, sin):
    return x * cos + _swap(x) * sin


def _rmsn(x, g):
    return x * lax.rsqrt(jnp.mean(x * x, axis=-1, keepdims=True) + EPS) * g


def _split_dot(a, m):
    hi = a.astype(BF)
    lo = (a - hi.astype(F32)).astype(BF)
    return jnp.dot(hi, m, preferred_element_type=F32) + jnp.dot(lo, m, preferred_element_type=F32)


@jax.custom_vjp
def _group_mean(a, avg):
    return _split_dot(a, avg)


_group_mean.defvjp(lambda a, avg: (_split_dot(a, avg), avg),
                   lambda avg, g: (_split_dot(g, avg), jnp.zeros_like(avg)))


def _grmsn(x, g, avg):
    return x * lax.rsqrt(_group_mean(x * x, avg) + EPS) * g


def _modnorm(x, g, sh, sc):
    return _rmsn(x, g) * (1.0 + sc) + sh


def _gelu(x):
    return 0.5 * x * (1.0 + jnp.tanh(0.7978845608028654 * (x + 0.044715 * (x * x * x))))


def _silu(x):
    return x * jax.nn.sigmoid(x)


def _acc(ref, val, first):
    @pl.when(first)
    def _():
        ref[...] = val

    @pl.when(jnp.logical_not(first))
    def _():
        ref[...] += val


def _seg(i):
    return 2 * (i // BPE) + jnp.minimum(i % BPE, 1)


def _seg_first(i):
    return (i % BPE) <= 1


def _rb_call(name, body, row_in=(), mod_in=(), pos_in=(), full_in=(), shift_in=(),
             row_out=(), seg_out=(), acc_out=(), scratch=(), after=None):
    in_specs, args = [], []
    for a in row_in:
        in_specs.append(pl.BlockSpec((TB, a.shape[1]), lambda i: (i, 0)))
        args.append(a)
    for tab, m in mod_in:
        in_specs.append(pl.BlockSpec((1, 1, D), lambda i, m=m: (_seg(i) * N_MOD + m, 0, 0)))
        args.append(tab)
    for a in pos_in:
        in_specs.append(pl.BlockSpec((TB, a.shape[1]), lambda i: (i % BPE, 0)))
        args.append(a)
    for a in full_in:
        in_specs.append(pl.BlockSpec(a.shape, lambda i, n=a.ndim: (0,) * n))
        args.append(a)
    for a, d in shift_in:
        in_specs.append(pl.BlockSpec((TB, a.shape[1]), lambda i, d=d: (jnp.clip(i + d, 0, NBLK - 1), 0)))
        args.append(a)
    n_in = len(args)
    if after is not None:
        in_specs.append(pl.BlockSpec(after.shape, lambda i, n=after.ndim: (0,) * n))
        args.append(after)
    out_specs, out_shape = [], []
    for w, dt, *lat in row_out:
        if lat:
            out_specs.append(pl.BlockSpec(
                (TB, w), lambda i: ((i // BPE) * (L // TB) + jnp.maximum(i % BPE - 1, 0), 0)))
            out_shape.append(jax.ShapeDtypeStruct((NEX * L, w), dt))
        else:
            out_specs.append(pl.BlockSpec((TB, w), lambda i: (i, 0)))
            out_shape.append(jax.ShapeDtypeStruct((R, w), dt))
    for w in seg_out:
        out_specs.append(pl.BlockSpec((1, 1, w), lambda i: (_seg(i), 0, 0)))
        out_shape.append(jax.ShapeDtypeStruct((4, 1, w), F32))
    for shp in acc_out:
        out_specs.append(pl.BlockSpec(shp, lambda i, n=len(shp): (0,) * n))
        out_shape.append(jax.ShapeDtypeStruct(shp, F32))

    def kern(*refs):
        body(pl.program_id(0), *refs[:n_in], *refs[len(args):])

    sem = ("arbitrary",) if (seg_out or acc_out or any(len(r) > 2 for r in row_out)) else ("parallel",)
    return pl.pallas_call(kern, grid=(NBLK,), in_specs=in_specs, out_specs=out_specs, out_shape=out_shape,
                          scratch_shapes=list(scratch), compiler_params=_cparams(sem), name=name)(*args)


def modnorm_fwd(name, x, mods, g, m_sh, m_sc):
    def body(i, x_ref, sh_ref, sc_ref, g_ref, h_ref):
        h_ref[...] = _modnorm(x_ref[...], g_ref[...], sh_ref[0], sc_ref[0]).astype(BF)

    return _rb_call(name, body, row_in=(x,), mod_in=((mods, m_sh), (mods, m_sc)), full_in=(g,),
                    row_out=((D, BF),))[0]


def _gate_grads(dx, y_ref, gt_ref, dy_ref, dgt_ref, i):
    dy_ref[...] = (dx * gt_ref[0]).astype(BF)
    _acc(dgt_ref, jnp.sum(dx * y_ref[...].astype(F32), axis=0, keepdims=True)[None], _seg_first(i))


def modnorm_bwd(name, x, dh, dx_in, mods, g, m_sh, m_sc, gate=None, after=None, lat_only=False):
    def body(i, x_ref, dh_ref, dxin_ref, *rest):
        if gate:
            y_ref, sh_ref, sc_ref, gt_ref, g_ref, dx_ref, dy_ref, dgt_ref, dsh_ref, dsc_ref, dg_ref = rest
        else:
            sh_ref, sc_ref, g_ref, dx_ref, dsh_ref, dsc_ref, dg_ref = rest
        _, vjp = jax.vjp(_modnorm, x_ref[...], g_ref[...], sh_ref[0], sc_ref[0])
        dx, dg, dsh, dsc = vjp(dh_ref[...].astype(F32))
        dx = dxin_ref[...] + dx
        dx_ref[...] = dx
        if gate:
            _gate_grads(dx, y_ref, gt_ref, dy_ref, dgt_ref, i)
        _acc(dsh_ref, dsh[None], _seg_first(i))
        _acc(dsc_ref, dsc[None], _seg_first(i))
        _acc(dg_ref, dg, i == 0)

    if gate:
        y, gmods, m = gate
        return _rb_call(name, body, row_in=(x, dh, dx_in, y), mod_in=((mods, m_sh), (mods, m_sc), (gmods, m)),
                        full_in=(g,), row_out=((D, F32), (D, BF)), seg_out=(D, D, D), acc_out=((1, D),), after=after)
    return _rb_call(name, body, row_in=(x, dh, dx_in), mod_in=((mods, m_sh), (mods, m_sc)), full_in=(g,),
                    row_out=((D, F32, "lat") if lat_only else (D, F32),), seg_out=(D, D), acc_out=((1, D),),
                    after=after)


def proj_in(name, h, w):
    n = w.shape[1]

    def body(i, h_ref, w_ref, o_ref):
        o_ref[...] = jnp.dot(h_ref[...], w_ref[...], preferred_element_type=F32).astype(BF)

    return _rb_call(name, body, row_in=(h,), full_in=(w,), row_out=((n, BF),))[0]


def proj_out(name, a1, a2, w, x, mods, m_gate, g_next, m_sh, m_sc):
    k1 = a1.shape[1]

    def body(i, a1_ref, a2_ref, x_ref, gt_ref, sh_ref, sc_ref, w_ref, g_ref, xo_ref, y_ref, h_ref):
        y = jnp.dot(a1_ref[...], w_ref[:k1, :], preferred_element_type=F32)
        y = y + jnp.dot(a2_ref[...], w_ref[k1:, :], preferred_element_type=F32)
        y_ref[...] = y.astype(BF)
        xn = x_ref[...] + gt_ref[0] * y
        xo_ref[...] = xn
        h_ref[...] = _modnorm(xn, g_ref[...], sh_ref[0], sc_ref[0]).astype(BF)

    return _rb_call(name, body, row_in=(a1, a2, x), mod_in=((mods, m_gate), (mods, m_sh), (mods, m_sc)),
                    full_in=(w, g_next), row_out=((D, F32), (D, BF), (D, BF)))


def mlp_up(name, h, w1):
    def body(i, h_ref, w_ref, a_ref, f_ref):
        hv = h_ref[...]
        for n in range(4):
            a = jnp.dot(hv, w_ref[n], preferred_element_type=F32)
            a_ref[:, n * D:(n + 1) * D] = a.astype(BF)
            r = jnp.maximum(a, 0.0)
            f_ref[:, n * D:(n + 1) * D] = (r * r).astype(BF)

    return _rb_call(name, body, row_in=(h,), full_in=(w1,), row_out=((FF, BF), (FF, BF)))


def mlp_down(name, f, w2, x, mods, m_gate, nxt=None):
    def body(i, f_ref, x_ref, gt_ref, *rest):
        if nxt:
            sh_ref, sc_ref, w_ref, g_ref, xo_ref, y_ref, h_ref = rest
        else:
            w_ref, xo_ref, y_ref = rest
        y = jnp.dot(f_ref[:, 0:D], w_ref[0], preferred_element_type=F32)
        for n in range(1, 4):
            y = y + jnp.dot(f_ref[:, n * D:(n + 1) * D], w_ref[n], preferred_element_type=F32)
        xn = x_ref[...] + gt_ref[0] * y
        y_ref[...] = y.astype(BF)
        xo_ref[...] = xn
        if nxt:
            h_ref[...] = _modnorm(xn, g_ref[...], sh_ref[0], sc_ref[0]).astype(BF)

    if nxt:
        return _rb_call(name, body, row_in=(f, x), mod_in=((mods, m_gate), (nxt[0], 0), (nxt[0], 1)),
                        full_in=(w2, nxt[1]), row_out=((D, F32), (D, BF), (D, BF)))
    return _rb_call(name, body, row_in=(f, x), mod_in=((mods, m_gate),), full_in=(w2,),
                    row_out=((D, F32), (D, BF)))


def mm_nt(name, g, w, after=None):
    k = w.shape[0]

    def body(i, g_ref, w_ref, o_ref):
        o_ref[...] = lax.dot_general(g_ref[...], w_ref[...], NT, preferred_element_type=F32).astype(BF)

    return _rb_call(name, body, row_in=(g,), full_in=(w,), row_out=((k, BF),), after=after)[0]


def mlp_bwd_da(name, dy, w2, a, after=None):
    def body(i, dy_ref, a_ref, w_ref, da_ref):
        dyv = dy_ref[...]
        for n in range(4):
            df = lax.dot_general(dyv, w_ref[n], NT, preferred_element_type=F32)
            av = a_ref[:, n * D:(n + 1) * D].astype(F32)
            da_ref[:, n * D:(n + 1) * D] = (df * (2.0 * jnp.maximum(av, 0.0))).astype(BF)

    return _rb_call(name, body, row_in=(dy, a), full_in=(w2,), row_out=((FF, BF),), after=after)[0]


def mlp_bwd_dh(name, da, w1):
    def body(i, da_ref, w_ref, dh_ref):
        acc = lax.dot_general(da_ref[:, 0:D], w_ref[0], NT, preferred_element_type=F32)
        for n in range(1, 4):
            acc = acc + lax.dot_general(da_ref[:, n * D:(n + 1) * D], w_ref[n], NT, preferred_element_type=F32)
        dh_ref[...] = acc.astype(BF)

    return _rb_call(name, body, row_in=(da,), full_in=(w1,), row_out=((D, BF),))[0]


TN_ROWS = 1536


def mm_tn(name, a, g, tiles, th, tw):
    nt = len(tiles)
    acs = jnp.asarray([t[0] for t in tiles], jnp.int32)
    gcs = jnp.asarray([t[1] for t in tiles], jnp.int32)
    nr = R // TN_ROWS

    def kern(ac_ref, gc_ref, a_ref, g_ref, o_ref, acc_ref):
        r = pl.program_id(1)

        @pl.when(r == 0)
        def _():
            acc_ref[...] = jnp.zeros_like(acc_ref)

        acc_ref[...] += lax.dot_general(a_ref[...], g_ref[...], TN, preferred_element_type=F32)

        @pl.when(r == nr - 1)
        def _():
            o_ref[...] = acc_ref[...].astype(BF)

    grid_spec = pltpu.PrefetchScalarGridSpec(
        num_scalar_prefetch=2, grid=(nt, nr),
        in_specs=[pl.BlockSpec((TN_ROWS, th), lambda t, r, ac, gc: (r, ac[t])),
                  pl.BlockSpec((TN_ROWS, tw), lambda t, r, ac, gc: (r, gc[t]))],
        out_specs=pl.BlockSpec((None, th, tw), lambda t, r, ac, gc: (t, 0, 0)),
        scratch_shapes=[pltpu.VMEM((th, tw), F32)])
    return pl.pallas_call(kern, grid_spec=grid_spec, out_shape=jax.ShapeDtypeStruct((nt, th, tw), BF),
                          compiler_params=_cparams(("parallel", "arbitrary")), name=name)(acs, gcs, a, g)


def _even_tok(q, k, zu, zv, gq, gk, gs, ws, bs, cq, sq, ck, sk, avg, masks):
    qr = _rope(_grmsn(q, gq, avg), cq, sq) * GQA_SCALE
    kr = _rope(_grmsn(k, gk, avg[:128, :128]), ck, sk)
    u = _gelu(zu)
    v = _grmsn(_gelu(zv), gs, avg)
    sv = None
    for g in range(8):
        t = masks[g] * (_mm(ws[g], v) + bs[g])
        sv = t if sv is None else sv + t
    return qr, kr, u * sv


def even_tok_fwd(p, cos, sin, gq, gk, gs, sgu_w, sgu_b, avg, masks):
    def body(i, p_ref, cos_ref, sin_ref, gq_ref, gk_ref, gs_ref, w_ref, b_ref, avg_ref, mk_ref, q_ref, kv_ref, m_ref):
        avgv = avg_ref[...]
        ws = [w_ref[g] for g in range(8)]
        bs = [b_ref[g] for g in range(8)]
        mks = [mk_ref[g] for g in range(8)]
        for c in range(2):
            rs = pl.ds(c * 128, 128)
            qr, kr, m = _even_tok(
                p_ref[rs, 0:512].astype(F32), p_ref[rs, 512:640].astype(F32),
                p_ref[rs, 768:1280].astype(F32), p_ref[rs, 1280:1792].astype(F32),
                gq_ref[...], gk_ref[...], gs_ref[...], ws, bs,
                cos_ref[rs, :], sin_ref[rs, :], cos_ref[rs, 0:128], sin_ref[rs, 0:128], avgv, mks)
            q_ref[rs, :] = qr.astype(BF)
            kv_ref[rs, 0:128] = kr.astype(BF)
            kv_ref[rs, 128:256] = p_ref[rs, 640:768]
            m_ref[rs, :] = m.astype(BF)

    return _rb_call("even_tok_fwd", body, row_in=(p,), pos_in=(cos, sin),
                    full_in=(gq, gk, gs, sgu_w, sgu_b, avg, masks), row_out=((512, BF), (256, BF), (512, BF)))


def even_tok_bwd(p, dq, dkv, dcat, cos, sin, gq, gk, gs, sgu_w, sgu_b, avg, masks):
    def body(i, p_ref, dq_ref, dkv_ref, dcat_ref, cos_ref, sin_ref, gq_ref, gk_ref, gs_ref, w_ref, b_ref,
             avg_ref, mk_ref, dp_ref, dgq_ref, dgk_ref, dgs_ref, dw_ref, db_ref):
        avgv = avg_ref[...]
        ws = [w_ref[g] for g in range(8)]
        bs = [b_ref[g] for g in range(8)]
        mks = [mk_ref[g] for g in range(8)]
        tot = None
        for c in range(2):
            rs = pl.ds(c * 128, 128)
            cq, sq, ck, sk = cos_ref[rs, :], sin_ref[rs, :], cos_ref[rs, 0:128], sin_ref[rs, 0:128]

            def f(q, k, zu, zv, gq, gk, gs, ws, bs):
                return _even_tok(q, k, zu, zv, gq, gk, gs, ws, bs, cq, sq, ck, sk, avgv, mks)

            _, vjp = jax.vjp(f, p_ref[rs, 0:512].astype(F32), p_ref[rs, 512:640].astype(F32),
                             p_ref[rs, 768:1280].astype(F32), p_ref[rs, 1280:1792].astype(F32),
                             gq_ref[...], gk_ref[...], gs_ref[...], ws, bs)
            d = vjp((dq_ref[rs, :].astype(F32), dkv_ref[rs, 0:128], dcat_ref[rs, 512:1024].astype(F32)))
            dp_ref[rs, 0:512] = d[0].astype(BF)
            dp_ref[rs, 512:640] = d[1].astype(BF)
            dp_ref[rs, 640:768] = dkv_ref[rs, 128:256].astype(BF)
            dp_ref[rs, 768:1280] = d[2].astype(BF)
            dp_ref[rs, 1280:1792] = d[3].astype(BF)
            part = [d[4], d[5], d[6]] + list(d[7]) + list(d[8])
            tot = part if tot is None else [a + b for a, b in zip(tot, part)]
        refs = [dgq_ref, dgk_ref, dgs_ref] + [dw_ref.at[g] for g in range(8)] + [db_ref.at[g] for g in range(8)]
        for ref, val in zip(refs, tot):
            _acc(ref, val, i == 0)

    return _rb_call("even_tok_bwd", body, row_in=(p, dq, dkv, dcat), pos_in=(cos, sin),
                    full_in=(gq, gk, gs, sgu_w, sgu_b, avg, masks), row_out=((EV_IN, BF),),
                    acc_out=((1, 512), (1, 128), (1, 512), (8, 128, 128), (8, 128, 1)))


MLA_SCALE = 96 ** -0.5
GQA_SCALE = 64 ** -0.5


def _odd_tok(cq, ckv, kr, za, zg, gq, gkv, wq, wkk, wkv, spread, cr, sr, ck, sk):
    cqn = _rmsn(cq, gq)
    q = _rope(_mm(cqn, wq), cr, sr) * MLA_SCALE
    ckn = _rmsn(ckv, gkv)
    k = _mm(ckn, wkk) + _mm(_rope(kr, ck, sk), spread)
    v = _mm(ckn, wkv)
    y = za * jax.nn.sigmoid(zg)
    return q, k, v, y


def odd_tok_fwd(p, cos, sin, gq, gkv, wq, wkk, wkv, spread):
    def body(i, p_ref, cos_ref, sin_ref, gq_ref, gkv_ref, wq_ref, wkk_ref, wkv_ref, sp_ref, q_ref, kv_ref, y_ref):
        q, k, v, y = _odd_tok(
            p_ref[:, 0:256].astype(F32), p_ref[:, 256:384].astype(F32), p_ref[:, 384:512].astype(F32),
            p_ref[:, 512:1024].astype(F32), p_ref[:, 1024:1536].astype(F32),
            gq_ref[...], gkv_ref[...], wq_ref[...], wkk_ref[...], wkv_ref[...], sp_ref[...],
            cos_ref[:, 0:768], sin_ref[:, 0:768], cos_ref[:, 768:896], sin_ref[:, 768:896])
        q_ref[...] = q.astype(BF)
        kv_ref[:, 0:768] = k.astype(BF)
        kv_ref[:, 768:1280] = v.astype(BF)
        y_ref[...] = y.astype(BF)

    return _rb_call("odd_tok_fwd", body, row_in=(p,), pos_in=(cos, sin), full_in=(gq, gkv, wq, wkk, wkv, spread),
                    row_out=((768, BF), (1280, BF), (512, BF)))


def odd_tok_bwd(p, dq, dkv, dy, cos, sin, gq, gkv, wq, wkk, wkv, spread):
    def body(i, p_ref, dq_ref, dkv_ref, dy_ref, cos_ref, sin_ref, gq_ref, gkv_ref, wq_ref, wkk_ref, wkv_ref, sp_ref,
             dp_ref, dgq_ref, dgkv_ref, dwq_ref, dwkk_ref, dwkv_ref):
        cr, sr, ck, sk = cos_ref[:, 0:768], sin_ref[:, 0:768], cos_ref[:, 768:896], sin_ref[:, 768:896]
        spread_v = sp_ref[...]

        def f(cq, ckv, kr, za, zg, gq, gkv, wq, wkk, wkv):
            return _odd_tok(cq, ckv, kr, za, zg, gq, gkv, wq, wkk, wkv, spread_v, cr, sr, ck, sk)

        _, vjp = jax.vjp(f, p_ref[:, 0:256].astype(F32), p_ref[:, 256:384].astype(F32),
                         p_ref[:, 384:512].astype(F32), p_ref[:, 512:1024].astype(F32),
                         p_ref[:, 1024:1536].astype(F32), gq_ref[...], gkv_ref[...], wq_ref[...],
                         wkk_ref[...], wkv_ref[...])
        d = vjp((dq_ref[...].astype(F32), dkv_ref[:, 0:768], dkv_ref[:, 768:1280], dy_ref[...].astype(F32)))
        dp_ref[:, 0:256] = d[0].astype(BF)
        dp_ref[:, 256:384] = d[1].astype(BF)
        dp_ref[:, 384:512] = d[2].astype(BF)
        dp_ref[:, 512:1024] = d[3].astype(BF)
        dp_ref[:, 1024:1536] = d[4].astype(BF)
        for ref, val in zip((dgq_ref, dgkv_ref, dwq_ref, dwkk_ref, dwkv_ref), d[5:]):
            _acc(ref, val, i == 0)

    return _rb_call("odd_tok_bwd", body, row_in=(p, dq, dkv, dy), pos_in=(cos, sin),
                    full_in=(gq, gkv, wq, wkk, wkv, spread), row_out=((OD_PAD, BF),),
                    acc_out=((1, 256), (1, 128), (256, 768), (128, 768), (128, 512)))


GQA_HEADS = [(64 * h, 64 * (h // 4), 64, 128 + 64 * (h // 4)) for h in range(8)]
MLA_HEADS = [(96 * h, 96 * h, 96, 768 + 64 * h) for h in range(8)]


def _by_block(j, run):
    @pl.when(j == 0)
    def _():
        run(LC)

    @pl.when(j > 0)
    def _():
        run(SEQ)


def attn_fwd(name, q, kv, heads):
    qw, kvw = q.shape[1], kv.shape[1]

    def kern(q_ref, kv_ref, o_ref, lse_ref):
        def run(nk):
            for h, (qo, ko, w, vo) in enumerate(heads):
                s = lax.dot_general(q_ref[:, qo:qo + w], kv_ref[0:nk, ko:ko + w], NT, preferred_element_type=F32)
                m = jnp.max(s, axis=-1, keepdims=True)
                p = jnp.exp(s - m)
                l = jnp.sum(p, axis=-1, keepdims=True)
                o = jnp.dot(p.astype(BF), kv_ref[0:nk, vo:vo + 64], preferred_element_type=F32) / l
                o_ref[:, 64 * h:64 * h + 64] = o.astype(BF)
                lse_ref[:, h:h + 1] = m + jnp.log(l)

        _by_block(pl.program_id(1), run)

    return pl.pallas_call(
        kern, grid=(NEX, BPE),
        in_specs=[pl.BlockSpec((TB, qw), lambda e, j: (e * BPE + j, 0)),
                  pl.BlockSpec((SEQ, kvw), lambda e, j: (e, 0))],
        out_specs=[pl.BlockSpec((TB, 512), lambda e, j: (e * BPE + j, 0)),
                   pl.BlockSpec((TB, 8), lambda e, j: (e * BPE + j, 0))],
        out_shape=[jax.ShapeDtypeStruct((R, 512), BF), jax.ShapeDtypeStruct((R, 8), F32)],
        compiler_params=_cparams(("parallel", "arbitrary")), name=name)(q, kv)


def attn_bwd(name, q, kv, o, dcat, lse, heads):
    qw, kvw = q.shape[1], kv.shape[1]

    def kern(q_ref, kv_ref, o_ref, do_ref, lse_ref, dq_ref, dkv_ref):
        j = pl.program_id(1)

        @pl.when(j == 0)
        def _():
            dkv_ref[...] = jnp.zeros_like(dkv_ref)

        def run(nk):
            for h, (qo, ko, w, vo) in enumerate(heads):
                qh = q_ref[:, qo:qo + w]
                kh = kv_ref[0:nk, ko:ko + w]
                s = lax.dot_general(qh, kh, NT, preferred_element_type=F32)
                p = jnp.exp(s - lse_ref[:, h:h + 1])
                do = do_ref[:, 64 * h:64 * h + 64]
                dsum = jnp.sum(do.astype(F32) * o_ref[:, 64 * h:64 * h + 64].astype(F32), axis=-1, keepdims=True)
                dp = lax.dot_general(do, kv_ref[0:nk, vo:vo + 64], NT, preferred_element_type=F32)
                ds = (p * (dp - dsum)).astype(BF)
                dkv_ref[0:nk, vo:vo + 64] += lax.dot_general(p.astype(BF), do, TN, preferred_element_type=F32)
                dq_ref[:, qo:qo + w] = jnp.dot(ds, kh, preferred_element_type=F32).astype(BF)
                dkv_ref[0:nk, ko:ko + w] += lax.dot_general(ds, qh, TN, preferred_element_type=F32)

        _by_block(j, run)

    return pl.pallas_call(
        kern, grid=(NEX, BPE),
        in_specs=[pl.BlockSpec((TB, qw), lambda e, j: (e * BPE + j, 0)),
                  pl.BlockSpec((SEQ, kvw), lambda e, j: (e, 0)),
                  pl.BlockSpec((TB, 512), lambda e, j: (e * BPE + j, 0)),
                  pl.BlockSpec((TB, 512), lambda e, j: (e * BPE + j, 0)),
                  pl.BlockSpec((TB, 8), lambda e, j: (e * BPE + j, 0))],
        out_specs=[pl.BlockSpec((TB, qw), lambda e, j: (e * BPE + j, 0)),
                   pl.BlockSpec((SEQ, kvw), lambda e, j: (e, 0))],
        out_shape=[jax.ShapeDtypeStruct((R, qw), BF), jax.ShapeDtypeStruct((R, kvw), F32)],
        compiler_params=_cparams(("parallel", "arbitrary")), name=name)(q, kv, o, dcat, lse)


HALO = 16
CONV_K = 31


def _fill_ext(ext_ref, prev_ref, cur_ref, next_ref, i):
    j = i % BPE
    has_prev = (j >= 2).astype(F32)
    has_next = jnp.logical_and(j >= 1, j <= BPE - 2).astype(F32)
    ext_ref[0:HALO, :] = prev_ref[TB - HALO:TB, :].astype(F32) * has_prev
    ext_ref[HALO:HALO + TB, :] = cur_ref[...].astype(F32)
    ext_ref[HALO + TB:2 * HALO + TB, :] = next_ref[0:HALO, :].astype(F32) * has_next


def _ln_silu(z, g, b):
    mu = jnp.mean(z, axis=-1, keepdims=True)
    zc = z - mu
    var = jnp.mean(zc * zc, axis=-1, keepdims=True)
    return _silu(zc * lax.rsqrt(var + EPS) * g + b)


def conf_fwd(y, cw, cb, lg, lb):
    def body(i, cur_ref, cw_ref, cb_ref, lg_ref, lb_ref, prev_ref, next_ref, z_ref, c_ref, ext_ref):
        _fill_ext(ext_ref, prev_ref, cur_ref, next_ref, i)
        acc = ext_ref[1:1 + TB, :] * cw_ref[0:1, :]
        for k in range(1, CONV_K):
            acc = acc + ext_ref[k + 1:k + 1 + TB, :] * cw_ref[k:k + 1, :]
        z = acc + cb_ref[...]
        z_ref[...] = z.astype(BF)
        c_ref[...] = _ln_silu(z, lg_ref[...], lb_ref[...]).astype(BF)

    return _rb_call("conf_fwd", body, row_in=(y,), full_in=(cw, cb, lg, lb), shift_in=((y, -1), (y, 1)),
                    row_out=((512, BF), (512, BF)), scratch=(pltpu.VMEM((TB + 2 * HALO, 512), F32),))


def conf_bwd_ln(z, dcat, lg, lb):
    def body(i, z_ref, dcat_ref, lg_ref, lb_ref, dz_ref, dlg_ref, dlb_ref, dcb_ref):
        _, vjp = jax.vjp(_ln_silu, z_ref[...].astype(F32), lg_ref[...], lb_ref[...])
        dz, dlg, dlb = vjp(dcat_ref[:, 512:1024].astype(F32))
        dz_ref[...] = dz.astype(BF)
        _acc(dlg_ref, dlg, i == 0)
        _acc(dlb_ref, dlb, i == 0)
        _acc(dcb_ref, jnp.sum(dz, axis=0, keepdims=True), i == 0)

    return _rb_call("conf_bwd_ln", body, row_in=(z, dcat), full_in=(lg, lb), row_out=((512, BF),),
                    acc_out=((1, 512), (1, 512), (1, 512)))


def conf_bwd_conv(y, dz, cw):
    def body(i, y_ref, dz_ref, cw_ref, yp_ref, yn_ref, dzp_ref, dzn_ref, dy_ref, dcw_ref, exty_ref, extd_ref):
        _fill_ext(exty_ref, yp_ref, y_ref, yn_ref, i)
        _fill_ext(extd_ref, dzp_ref, dz_ref, dzn_ref, i)
        dzv = dz_ref[...].astype(F32)
        @pl.when(i == 0)
        def _():
            dcw_ref[...] = jnp.zeros_like(dcw_ref)

        acc = None
        for k in range(CONV_K):
            t = extd_ref[CONV_K - k:CONV_K - k + TB, :] * cw_ref[k:k + 1, :]
            acc = t if acc is None else acc + t
            dcw_ref[k:k + 1, :] += jnp.sum(dzv * exty_ref[k + 1:k + 1 + TB, :], axis=0, keepdims=True)
        dy_ref[...] = acc.astype(BF)

    return _rb_call("conf_bwd_conv", body, row_in=(y, dz), full_in=(cw,),
                    shift_in=((y, -1), (y, 1), (dz, -1), (dz, 1)), row_out=((512, BF),), acc_out=((32, 512),),
                    scratch=(pltpu.VMEM((TB + 2 * HALO, 512), F32), pltpu.VMEM((TB + 2 * HALO, 512), F32)))


def final_loss(x, target, fg, y, mods, m_gate):
    lpb = L // TB

    def kern(x_ref, t_ref, g_ref, y_ref, gt_ref, dx_ref, dy_ref, dgt_ref, loss_ref, dg_ref):
        i = pl.program_id(0)
        lat = (i % BPE) >= 1
        xv, tv = x_ref[...], t_ref[...]

        def f(x, g):
            err = _rmsn(x, g) - tv
            rowsum = jnp.sum(err * err, axis=-1, keepdims=True)
            return jnp.sum(rowsum, axis=0, keepdims=True) * (0.5 / D)

        lv, vjp = jax.vjp(f, xv, g_ref[...])
        dx, dg = vjp(jnp.ones((1, 1), F32))
        m = lat.astype(F32)
        dx = dx * m
        dx_ref[...] = dx
        _gate_grads(dx, y_ref, gt_ref, dy_ref, dgt_ref, i)
        _acc(loss_ref, jnp.zeros((8, 128), F32) + lv * m, i == 0)
        _acc(dg_ref, dg * m, i == 0)

    row = pl.BlockSpec((TB, D), lambda i: (i, 0))
    return pl.pallas_call(
        kern, grid=(NBLK,),
        in_specs=[row, pl.BlockSpec((TB, D), lambda i: ((i // BPE) * lpb + jnp.maximum(i % BPE - 1, 0), 0)),
                  pl.BlockSpec((1, D), lambda i: (0, 0)), row,
                  pl.BlockSpec((1, 1, D), lambda i: (_seg(i) * N_MOD + m_gate, 0, 0))],
        out_specs=[row, row, pl.BlockSpec((1, 1, D), lambda i: (_seg(i), 0, 0)),
                   pl.BlockSpec((8, 128), lambda i: (0, 0)), pl.BlockSpec((1, D), lambda i: (0, 0))],
        out_shape=[jax.ShapeDtypeStruct((R, D), F32), jax.ShapeDtypeStruct((R, D), BF),
                   jax.ShapeDtypeStruct((4, 1, D), F32), jax.ShapeDtypeStruct((8, 128), F32),
                   jax.ShapeDtypeStruct((1, D), F32)],
        compiler_params=_cparams(("arbitrary",)), name="final_loss")(x, target, fg, y, mods)


NC = 24


def mods_fwd(call, ada_w, ada_b):
    cols = ada_w.shape[2]

    def kern(c_ref, w_ref, b_ref, o_ref):
        o_ref[...] = jnp.dot(_silu(c_ref[...]), w_ref[...], precision=HI, preferred_element_type=F32) + b_ref[...]

    return pl.pallas_call(
        kern, grid=(2,),
        in_specs=[pl.BlockSpec((NC, D), lambda l: (0, 0)), pl.BlockSpec((None, D, cols), lambda l: (l, 0, 0)),
                  pl.BlockSpec((None, 1, cols), lambda l: (l, 0, 0))],
        out_specs=pl.BlockSpec((None, NC, cols), lambda l: (l, 0, 0)),
        out_shape=jax.ShapeDtypeStruct((2, NC, cols), F32),
        compiler_params=_cparams(("parallel",)), name="mods_fwd")(call, ada_w, ada_b)


def ada_bwd(call, ada_w, dm):
    cols = ada_w.shape[2]

    def kern(c_ref, w_ref, dm_ref, gw_ref, dc_ref):
        l = pl.program_id(0)
        gw_ref[...] = lax.dot_general(_silu(c_ref[...]), dm_ref[...], TN, precision=HI, preferred_element_type=F32)
        part = lax.dot_general(dm_ref[16:24, :], w_ref[...], NT, precision=HI, preferred_element_type=F32)
        cc = c_ref[16:17, :]
        sg = jax.nn.sigmoid(cc)
        _acc(dc_ref, part * (sg * (1.0 + cc * (1.0 - sg))), l == 0)

    return pl.pallas_call(
        kern, grid=(2,),
        in_specs=[pl.BlockSpec((NC, D), lambda l: (0, 0)), pl.BlockSpec((None, D, cols), lambda l: (l, 0, 0)),
                  pl.BlockSpec((None, NC, cols), lambda l: (l, 0, 0))],
        out_specs=[pl.BlockSpec((None, D, cols), lambda l: (l, 0, 0)), pl.BlockSpec((8, D), lambda l: (0, 0))],
        out_shape=[jax.ShapeDtypeStruct((2, D, cols), F32), jax.ShapeDtypeStruct((8, D), F32)],
        compiler_params=_cparams(("arbitrary",)), name="ada_bwd")(call, ada_w, dm)


def sum_lead(name, a, scale_last=None):
    n, r, c = a.shape
    tr = r
    for cand in (512, 256, 128, 64, 32, 16, 8):
        if r % cand == 0 and cand * c * 4 * n <= 8 * 1024 * 1024:
            tr = cand
            break

    def kern(a_ref, o_ref):
        acc = a_ref[0].astype(F32)
        for k in range(1, n):
            acc = acc + a_ref[k].astype(F32)
        o_ref[...] = acc

    return pl.pallas_call(kern, grid=(r // tr,), in_specs=[pl.BlockSpec((n, tr, c), lambda i: (0, i, 0))],
                          out_specs=pl.BlockSpec((tr, c), lambda i: (i, 0)),
                          out_shape=jax.ShapeDtypeStruct((r, c), F32),
                          compiler_params=_cparams(("parallel",)), name=name)(a)


def add_pairs(name, hs, got, half):
    _, _, r, c = hs.shape

    def kern(half_ref, a_ref, b_ref, o_ref):
        o_ref[...] = (a_ref[...].astype(F32) + b_ref[...].astype(F32)).astype(BF)

    spec = pl.BlockSpec((None, r, c), lambda j, h: (j, 0, 0))
    grid_spec = pltpu.PrefetchScalarGridSpec(
        num_scalar_prefetch=1, grid=(4,),
        in_specs=[pl.BlockSpec((None, None, r, c), lambda j, h: (h[0], j, 0, 0)), spec], out_specs=spec)
    return pl.pallas_call(kern, grid_spec=grid_spec, out_shape=jax.ShapeDtypeStruct(got.shape, BF),
                          compiler_params=_cparams(("parallel",)), name=name)(half, hs, got)


def sum_slabs(name, land, own, where, full, lead):
    _, r, c = land.shape
    tr = r
    for cand in (512, 256, 128, 64, 32, 16):
        if r % cand == 0 and cand * c * 16 <= 4 * 1024 * 1024:
            tr = cand
            break

    def kern(where_ref, full_ref, land_ref, own_ref, o_ref):
        me = where_ref[0]
        acc = None
        for k in range(4):
            t = jnp.where(me == k, own_ref[k], land_ref[k]).astype(F32)
            acc = t if acc is None else acc + t
        o_ref[...] = acc

    spec = pl.BlockSpec((4, tr, c), lambda i, m: (0, i, 0))
    grid_spec = pltpu.PrefetchScalarGridSpec(
        num_scalar_prefetch=1, grid=(r // tr,), in_specs=[pl.BlockSpec(memory_space=pl.ANY), spec, spec],
        out_specs=pl.BlockSpec((None, None, tr, c), lambda i, m: (lead, m[1], i, 0)))
    return pl.pallas_call(kern, grid_spec=grid_spec, out_shape=jax.ShapeDtypeStruct(full.shape, F32),
                          input_output_aliases={1: 0}, compiler_params=_cparams(("parallel",)),
                          name=name)(where, full, land, own)


def adamw(name, w, g, m, v):
    r, c = w.shape
    tr = r
    for cand in (512, 256, 128, 64, 32, 16, 8):
        if r % cand == 0 and cand * c * 4 <= 2 * 1024 * 1024:
            tr = cand
            break
    c1 = 1.0 / (1.0 - ADAM_B1 ** ADAM_STEP)
    c2 = 1.0 / (1.0 - ADAM_B2 ** ADAM_STEP)

    def kern(w_ref, g_ref, m_ref, v_ref, d_ref, mo_ref, vo_ref):
        gv = g_ref[...]
        mn = ADAM_B1 * m_ref[...] + (1.0 - ADAM_B1) * gv
        vn = ADAM_B2 * v_ref[...] + (1.0 - ADAM_B2) * (gv * gv)
        d_ref[...] = -ADAM_LR * ((mn * c1) / (jnp.sqrt(vn * c2) + ADAM_EPS) + ADAM_WD * w_ref[...])
        mo_ref[...] = mn
        vo_ref[...] = vn

    spec = pl.BlockSpec((tr, c), lambda i: (i, 0))
    shp = jax.ShapeDtypeStruct((r, c), F32)
    return pl.pallas_call(kern, grid=(r // tr,), in_specs=[spec] * 4, out_specs=[spec] * 3, out_shape=[shp] * 3,
                          compiler_params=_cparams(("parallel",)), name=name)(w, g, m, v)


def all_gather8(name, xs):
    m_per, n = xs.shape

    def body(x_ref, out_ref, send_sems, recv_sems, local_sem):
        x, y, c = lax.axis_index("x"), lax.axis_index("y"), lax.axis_index("c")
        me, sibling = (x, y, c), (x, y, 1 - c)
        chips = [(1 - x, y), (x, 1 - y), (1 - x, 1 - y)]

        def rows(px, py, pc):
            return out_ref.at[pl.ds((4 * px + 2 * py + pc) * m_per, m_per), :]

        def copy(k, block, to, src=None):
            return pltpu.make_async_remote_copy(
                src_ref=rows(*block) if src is None else src, dst_ref=rows(*block),
                send_sem=send_sems.at[k], recv_sem=recv_sems.at[k], device_id=to, device_id_type=MESH)

        mine = pltpu.make_async_copy(x_ref, rows(*me), local_sem)
        mine.start()
        first = [copy(0, me, sibling, src=x_ref)]
        first += [copy(1 + j, me, (*chip, c), src=x_ref) for j, chip in enumerate(chips)]
        for cp in first:
            cp.start()
        passed = [copy(4 + j, (*chip, c), sibling) for j, chip in enumerate(chips)]
        for j, chip in enumerate(chips):
            copy(1 + j, (*chip, c), me).wait_recv()
            passed[j].start()
        copy(0, sibling, me).wait_recv()
        for j, chip in enumerate(chips):
            copy(4 + j, (*chip, 1 - c), me).wait_recv()
        for cp in first + passed:
            cp.wait_send()
        mine.wait()

    return pl.pallas_call(
        body, out_shape=jax.ShapeDtypeStruct((8 * m_per, n), xs.dtype),
        in_specs=[pl.BlockSpec(memory_space=pltpu.VMEM)], out_specs=pl.BlockSpec(memory_space=pltpu.VMEM),
        scratch_shapes=[pltpu.SemaphoreType.DMA((7,)), pltpu.SemaphoreType.DMA((7,)), pltpu.SemaphoreType.DMA],
        compiler_params=pltpu.CompilerParams(vmem_limit_bytes=VMEM_LIMIT), name=name)(xs)


def sibling_merge(name, fulls):
    n = len(fulls)
    slots = [(a, l) for a in range(n) for l in range(fulls[a].shape[0])]

    def body(*refs):
        buf = refs[n:2 * n]
        send_sems, recv_sems = refs[2 * n], refs[2 * n + 1]
        c = lax.axis_index("c")
        sibling = (lax.axis_index("x"), lax.axis_index("y"), 1 - c)
        sends, recvs = [], []
        for k, (a, l) in enumerate(slots):
            kw = dict(send_sem=send_sems.at[k], recv_sem=recv_sems.at[k], device_id=sibling, device_id_type=MESH)
            sends.append(pltpu.make_async_remote_copy(src_ref=buf[a].at[l, c], dst_ref=buf[a].at[l, c], **kw))
            recvs.append(pltpu.make_async_remote_copy(src_ref=buf[a].at[l, c], dst_ref=buf[a].at[l, 1 - c], **kw))
        for cp in sends:
            cp.start()
        for cp in recvs:
            cp.wait_recv()
        for cp in sends:
            cp.wait_send()

    anyspec = pl.BlockSpec(memory_space=pl.ANY)
    return pl.pallas_call(
        body, out_shape=[jax.ShapeDtypeStruct(s.shape, s.dtype) for s in fulls],
        in_specs=[anyspec] * n, out_specs=[anyspec] * n, input_output_aliases={a: a for a in range(n)},
        scratch_shapes=[pltpu.SemaphoreType.DMA((len(slots),)), pltpu.SemaphoreType.DMA((len(slots),))],
        name=name)(*fulls)


def place_own(name, land, src, chip):
    c = src.shape[-1]
    r = src.size // c
    tr = r
    for cand in (1024, 512, 256, 128, 64, 32, 16):
        if r % cand == 0 and cand * c * 2 <= 2 * 1024 * 1024:
            tr = cand
            break

    def kern(chip_ref, land_ref, src_ref, out_ref):
        out_ref[...] = src_ref[...]

    grid_spec = pltpu.PrefetchScalarGridSpec(
        num_scalar_prefetch=1, grid=(r // tr,),
        in_specs=[pl.BlockSpec(memory_space=pl.ANY), pl.BlockSpec((tr, c), lambda i, m: (i, 0))],
        out_specs=pl.BlockSpec((None, tr, c), lambda i, m: (m[0], i, 0)))
    out = pl.pallas_call(kern, grid_spec=grid_spec, out_shape=jax.ShapeDtypeStruct((4, r, c), land.dtype),
                         input_output_aliases={1: 0}, compiler_params=_cparams(("parallel",)),
                         name=name)(chip, land.reshape(4, r, c), src.reshape(r, c))
    return out.reshape(land.shape)


def _half_copies(src, land, send_sems, recv_sems):
    c = lax.axis_index("c")
    sibling = (lax.axis_index("x"), lax.axis_index("y"), 1 - c)
    pairs = []
    for a in range(len(src)):
        cp = pltpu.make_async_remote_copy(src_ref=src[a].at[1 - c], dst_ref=land[a], send_sem=send_sems.at[a],
                                          recv_sem=recv_sems.at[a], device_id=sibling, device_id_type=MESH)
        pairs.append((cp, cp))
    return pairs


def _chip_copies(src, land, send_sems, recv_sems, scatter):
    x, y, c = lax.axis_index("x"), lax.axis_index("y"), lax.axis_index("c")
    me = 2 * x + y
    pairs = []
    for a in range(len(src)):
        for j, (px, py) in enumerate([(1 - x, y), (x, 1 - y), (1 - x, 1 - y)]):
            to = 2 * px + py
            out = src[a].at[to] if scatter else src[a]
            kw = dict(send_sem=send_sems.at[3 * a + j], recv_sem=recv_sems.at[3 * a + j], device_id=(px, py, c),
                      device_id_type=MESH)
            pairs.append((pltpu.make_async_remote_copy(src_ref=out, dst_ref=land[a].at[me], **kw),
                          pltpu.make_async_remote_copy(src_ref=out, dst_ref=land[a].at[to], **kw)))
    return pairs


_HBM = pl.BlockSpec(memory_space=pltpu.HBM)
_SEM = pl.BlockSpec(memory_space=pltpu.SEMAPHORE)


GATHER = (functools.partial(_chip_copies, scatter=False), 3)
SCATTER = (functools.partial(_chip_copies, scatter=True), 3)
TO_SIBLING = (_half_copies, 1)


def _landing(shapes, dtype):
    return [lax.empty(tuple(s), dtype) for s in shapes]


def exchange_start(name, groups, plan):
    copies, per = plan
    sizes = [len(s) for s, _ in groups]
    flat = [a for s, l in groups for a in list(s) + list(l)]
    ng = len(groups)

    def body(*refs):
        ins, outs = refs[:len(flat)], refs[len(flat):]
        off = 0
        for g, n in enumerate(sizes):
            src, land = ins[off:off + n], ins[off + n:off + 2 * n]
            off += 2 * n
            for send, _ in copies(src, land, outs[2 * g], outs[2 * g + 1]):
                send.start()
        outs[-1][...] = jnp.zeros_like(outs[-1])

    out_shape = []
    for n in sizes:
        out_shape += [pltpu.SemaphoreType.DMA((per * n,)), pltpu.SemaphoreType.DMA((per * n,))]
    out_shape += [pltpu.HBM(a.shape, a.dtype) for a in flat] + [jax.ShapeDtypeStruct((8, 128), F32)]
    res = pl.pallas_call(
        body, out_shape=tuple(out_shape), in_specs=[_HBM] * len(flat),
        out_specs=tuple([_SEM] * (2 * ng) + [_HBM] * len(flat) + [pl.BlockSpec(memory_space=pltpu.VMEM)]),
        input_output_aliases={k: 2 * ng + k for k in range(len(flat))},
        compiler_params=pltpu.CompilerParams(has_side_effects=pltpu.SideEffectType.DATAFLOW_SIDE_EFFECTING),
        name=name)(*[pltpu.with_memory_space_constraint(a, pltpu.HBM) for a in flat])
    handles, off = [], 2 * ng
    for g, n in enumerate(sizes):
        handles.append((res[2 * g], res[2 * g + 1], list(res[off:off + n]), list(res[off + n:off + 2 * n])))
        off += 2 * n
    return handles, res[-1]


def exchange_wait(name, handle, after, plan):
    send_sems, recv_sems, srcs, lands = handle
    n = len(srcs)

    def body(*refs):
        src, land = refs[:n], refs[n:2 * n]
        for send, recv in plan[0](src, land, refs[2 * n], refs[2 * n + 1]):
            send.wait_send()
            recv.wait_recv()

    res = pl.pallas_call(
        body, out_shape=tuple(pltpu.HBM(a.shape, a.dtype) for a in srcs + lands),
        in_specs=[_HBM] * (2 * n) + [_SEM, _SEM, pl.BlockSpec(memory_space=pl.ANY)],
        out_specs=tuple([_HBM] * (2 * n)), input_output_aliases={k: k for k in range(2 * n)},
        compiler_params=pltpu.CompilerParams(has_side_effects=pltpu.SideEffectType.DATAFLOW_SIDE_EFFECTING),
        name=name)(*srcs, *lands, send_sems, recv_sems, after)
    return list(res[:n]), list(res[n:])


def _rope_tables(d_rot, reps):
    rows = L // GRID_W
    row = np.repeat(np.arange(rows), GRID_W).astype(np.float32)
    col = np.tile(np.arange(GRID_W), rows).astype(np.float32)
    d_axis = d_rot // 2
    inv = (ROPE_THETA ** (-np.arange(0, d_axis, 2, dtype=np.float32) / d_axis)).astype(np.float32)
    ang = np.concatenate([row[:, None] * inv, col[:, None] * inv], axis=-1).astype(np.float32)
    cos, sin = np.cos(ang).astype(np.float32), np.sin(ang).astype(np.float32)
    c = np.repeat(cos, 2, axis=-1)
    s = np.stack([-sin, sin], axis=-1).reshape(L, d_rot)
    c = np.concatenate([np.ones((LC, d_rot), np.float32), c], axis=0)
    s = np.concatenate([np.zeros((LC, d_rot), np.float32), s], axis=0)
    return np.tile(c, (1, reps)), np.tile(s, (1, reps))


def _group_consts():
    g = np.arange(512) // 64
    avg = (g[:, None] == g[None, :]).astype(np.float32) / 64.0
    masks = (np.arange(8)[:, None] == g[None, :]).astype(np.float32).reshape(8, 1, 512)
    return jnp.asarray(avg, BF), jnp.asarray(masks)


def _pack(items):
    flat = jnp.concatenate([a.reshape(-1).astype(F32) for a in items])
    n = flat.shape[0]
    rows = -(-n // D)
    rows = -(-rows // 8) * 8
    return jnp.pad(flat, (0, rows * D - n)).reshape(rows, D)


def _unpack(buf, shapes):
    lead = buf.shape[:-2]
    flat = buf.reshape(lead + (-1,))
    out, off = [], 0
    for shp in shapes:
        n = int(np.prod(shp))
        out.append(flat[..., off:off + n].reshape(lead + tuple(shp)))
        off += n
    return out


def _arrive(prm, key, after):
    if callable(prm[key]):
        prm[key](after)
    return prm[key]


def _layer_fwd(i, x, h, mods, prm, consts, nxt):
    sv = {}
    sv["x0"] = x
    sv["h"] = h
    p = proj_in(f"proj_in_{i}", h, _arrive(prm, "w_in", h))
    sv["p"] = p
    if i == 0:
        q, kv, m2 = even_tok_fwd(p, consts["cos_e"], consts["sin_e"], prm["gq"], prm["gk"], prm["gs"],
                                 prm["sgu_w"], prm["sgu_b"], consts["avg"], consts["masks"])
        o, lse = attn_fwd("attn_fwd_0", q, kv, GQA_HEADS)
        sv.update(q=q, kv=kv)
    else:
        q, kv, y = odd_tok_fwd(p, consts["cos_o"], consts["sin_o"], prm["gq"], prm["gkv"], prm["wq"], prm["wkk"],
                               prm["wkv"], consts["spread"])
        o, lse = attn_fwd("attn_fwd_1", q, kv, MLA_HEADS)
        z, m2 = conf_fwd(y, prm["conv_w"], prm["conv_b"], prm["ln_g"], prm["ln_b"])
        sv.update(q=q, kv=kv, y=y, z=z)
    sv.update(o=o, lse=lse, m2=m2)
    x1, y1, h2 = proj_out(f"proj_out_{i}", o, m2, _arrive(prm, "w_out", o), x, mods, 2, prm["norm2_g"], 3, 4)
    sv.update(x1=x1, y1=y1)
    a, f = mlp_up(f"mlp_up_{i}", h2, prm["w1"])
    x2, y2, *h_next = mlp_down(f"mlp_down_{i}", f, prm["w2"], x1, mods, 5, nxt)
    sv.update(h2=h2, a=a, f=f, y2=y2)
    return x2, (h_next[0] if h_next else None), sv


def _layer_bwd(i, dx, dy2, dg2, sv, mods, prm, consts, hook, entry, below):
    gr = {}
    da = mlp_bwd_da(f"mlp_bwd_da_{i}", dy2, prm["w2"], sv["a"], after=entry)
    tiles8 = [(h, j) for h in range(2) for j in range(4)]
    gr["w1"] = mm_tn(f"grad_w1_{i}", sv["h2"], da, tiles8, 512, D).reshape(2, 4, 512, D)
    gr["w2"] = mm_tn(f"grad_w2_{i}", sv["f"], dy2, [(2 * j + h, 0) for h in range(2) for j in range(4)],
                     512, D).reshape(2, 4, 512, D)
    dh2 = mlp_bwd_dh(f"mlp_bwd_dh_{i}", da, prm["w1"])
    dx1, dy1, dg1, dsh2, dsc2, gr["norm2_g"] = modnorm_bwd(
        f"norm2_bwd_{i}", sv["x1"], dh2, dx, mods, prm["norm2_g"], 3, 4, gate=(sv["y1"], mods, 2),
        after=hook(f"{i}:mlp", gr, dh2))
    dcat = mm_nt(f"proj_out_bwd_{i}", dy1, prm["w_out"], after=hook(f"{i}:mid", gr, dy1))
    t4 = [(2 * j + h, 0) for h in range(2) for j in range(2)]
    go = mm_tn(f"grad_wout_a_{i}", sv["o"], dy1, t4, 128, D).reshape(2, 2, 128, D)
    gm = mm_tn(f"grad_wout_b_{i}", sv["m2"], dy1, t4, 128, D).reshape(2, 2, 128, D)
    gr["w_out"] = jnp.concatenate([go, gm], axis=1)
    if i == 0:
        dq, dkv = attn_bwd("attn_bwd_0", sv["q"], sv["kv"], sv["o"], dcat, sv["lse"], GQA_HEADS)
        dp, gr["gq"], gr["gk"], gr["gs"], gr["sgu_w"], gr["sgu_b"] = even_tok_bwd(
            sv["p"], dq, dkv, dcat, consts["cos_e"], consts["sin_e"], prm["gq"], prm["gk"],
            prm["gs"], prm["sgu_w"], prm["sgu_b"], consts["avg"], consts["masks"])
    else:
        dq, dkv = attn_bwd("attn_bwd_1", sv["q"], sv["kv"], sv["o"], dcat, sv["lse"], MLA_HEADS)
        dz, gr["ln_g"], gr["ln_b"], gr["conv_b"] = conf_bwd_ln(sv["z"], dcat, prm["ln_g"], prm["ln_b"])
        dyc, gr["conv_w"] = conf_bwd_conv(sv["y"], dz, prm["conv_w"])
        dp, gr["gq"], gr["gkv"], gr["wq"], gr["wkk"], gr["wkv"] = odd_tok_bwd(
            sv["p"], dq, dkv, dyc, consts["cos_o"], consts["sin_o"], prm["gq"], prm["gkv"], prm["wq"], prm["wkk"],
            prm["wkv"], consts["spread"])
    n_in = prm["w_in"].shape[1]
    gr["w_in"] = mm_tn(f"grad_win_{i}", sv["h"], dp, [(0, 0), (1, 0)], 512, n_in)
    dh = mm_nt(f"proj_in_bwd_{i}", dp, prm["w_in"])
    if below:
        dx0, dy2b, dg2b, dsh1, dsc1, gr["norm1_g"] = modnorm_bwd(
            f"norm1_bwd_{i}", sv["x0"], dh, dx1, mods, prm["norm1_g"], 0, 1, gate=(below[0], below[1], 5))
        down = (dy2b, dg2b)
    else:
        dx0, dsh1, dsc1, gr["norm1_g"] = modnorm_bwd(f"norm1_bwd_{i}", sv["x0"], dh, dx1, mods, prm["norm1_g"], 0, 1,
                                                     lat_only=True)
        down = None
    dmods = jnp.concatenate([dsh1, dsc1, dg1, dsh2, dsc2, dg2], axis=1)
    return dx0, down, dmods, gr, hook(f"{i}:end", gr, dx0)


def local_step(xcat, target, mods, prms, final_g, hook=lambda point, grads, fresh: None):
    avg, masks = _group_consts()
    cos_e, sin_e = _rope_tables(64, 8)
    ck, sk = _rope_tables(32, 1)
    one64, zero64 = np.ones((SEQ, 64), np.float32), np.zeros((SEQ, 64), np.float32)
    one96, zero96 = np.ones((SEQ, 96), np.float32), np.zeros((SEQ, 96), np.float32)
    cos_o = np.concatenate([np.tile(np.concatenate([one64, ck], axis=1), (1, 8)), ck, one96], axis=1)
    sin_o = np.concatenate([np.tile(np.concatenate([zero64, sk], axis=1), (1, 8)), sk, zero96], axis=1)
    lane = np.arange(768)
    spread = np.zeros((128, 768), np.float32)
    spread[lane % 96 - 64, lane] = (lane % 96 >= 64)
    consts = dict(avg=avg, masks=masks, cos_e=jnp.asarray(cos_e), sin_e=jnp.asarray(sin_e),
                  cos_o=jnp.asarray(cos_o), sin_o=jnp.asarray(sin_o), spread=jnp.asarray(spread, BF))
    x = xcat
    h = modnorm_fwd("norm1_fwd_0", x, mods[0], prms[0]["norm1_g"], 0, 1)
    saved = []
    for i in range(2):
        x, h, sv = _layer_fwd(i, x, h, mods[i], prms[i], consts, (mods[1], prms[1]["norm1_g"]) if i == 0 else None)
        saved.append(sv)
    dx, dy2, dg2, loss, dfg = final_loss(x, target, final_g, saved[1]["y2"], mods[1], 5)
    dmods, grads = [None, None], [None, None]
    entry, down = None, (dy2, dg2)
    for i in (1, 0):
        below = (saved[0]["y2"], mods[0]) if i == 1 else None
        dx, down, dmods[i], grads[i], entry = _layer_bwd(i, dx, down[0], down[1], saved[i], mods[i], prms[i], consts,
                                                         hook, entry, below)
    return loss, dx, dmods, grads, dfg


def _row(v):
    return v.reshape(1, -1).astype(F32)


def odd_in_params(od_w_in, w_uq, w_ukv):
    od = jnp.concatenate([od_w_in[:, 0:416], jnp.zeros((D, 96), od_w_in.dtype), od_w_in[:, 416:OD_IN]], axis=1)
    ukv = w_ukv.reshape(128, 8, 128)
    wkk = jnp.pad(ukv[:, :, :64], ((0, 0), (0, 0), (0, 32))).reshape(128, 768)
    return dict(w_in=od, wq=w_uq, wkk=wkk, wkv=ukv[:, :, 64:].reshape(128, 512))


def small_params(small):
    p0 = dict(norm1_g=_row(small["norm1_g"][0]), norm2_g=_row(small["norm2_g"][0]),
              gq=jnp.tile(_row(small["ev_q_norm_g"]), (1, 8)), gk=jnp.tile(_row(small["ev_k_norm_g"]), (1, 2)),
              gs=_row(small["ev_sgu_norm_g"]), sgu_w=small["ev_sgu_w"].reshape(8, 128, 128).astype(F32),
              sgu_b=small["ev_sgu_b"].reshape(8, 128, 1).astype(F32))
    p1 = dict(norm1_g=_row(small["norm1_g"][1]), norm2_g=_row(small["norm2_g"][1]),
              gq=_row(small["od_q_norm_g"]), gkv=_row(small["od_kv_norm_g"]),
              conv_w=jnp.pad(small["od_conv_w"].reshape(CONV_K, 512).astype(F32), ((0, 1), (0, 0))),
              conv_b=_row(small["od_conv_b"]), ln_g=_row(small["od_ln_g"]), ln_b=_row(small["od_ln_b"]))
    return [p0, p1]


def prep_params(ev_w_in, od_w_in, w_out, w1, w2, w_uq, w_ukv, small):
    p0, p1 = small_params(small)
    p0.update(w_in=ev_w_in, w_out=w_out[0], w1=w1[0], w2=w2[0])
    p1.update(odd_in_params(od_w_in, w_uq, w_ukv), w_out=w_out[1], w1=w1[1], w2=w2[1])
    return [p0, p1]


def small_grads_natural(grads, dfg):
    g0, g1 = grads
    return dict(
        norm1_g=jnp.concatenate([g0["norm1_g"], g1["norm1_g"]], axis=0),
        norm2_g=jnp.concatenate([g0["norm2_g"], g1["norm2_g"]], axis=0),
        ev_q_norm_g=g0["gq"].reshape(8, 64).sum(0).reshape(1, 64),
        ev_k_norm_g=g0["gk"].reshape(2, 64).sum(0).reshape(1, 64),
        ev_sgu_norm_g=g0["gs"].reshape(1, 8, 64),
        ev_sgu_w=g0["sgu_w"].reshape(1, 8, 128, 128),
        ev_sgu_b=g0["sgu_b"].reshape(1, 8, 128),
        od_q_norm_g=g1["gq"].reshape(1, 256),
        od_kv_norm_g=g1["gkv"].reshape(1, 128),
        od_conv_w=g1["conv_w"][0:CONV_K].reshape(1, CONV_K, 512),
        od_conv_b=g1["conv_b"].reshape(1, 512),
        od_ln_g=g1["ln_g"].reshape(1, 512),
        od_ln_b=g1["ln_b"].reshape(1, 512),
        final_g=dfg.reshape(D))


def layer_grads_hs(i, g, part="all"):
    def cols(a):
        k, n = a.shape
        return a.reshape(2, k // 2, 4, n // 4).transpose(0, 2, 1, 3).astype(BF)

    mlp = [(("mlp_w1", i), g["w1"]), (("mlp_w2", i), g["w2"])]
    if part == "mlp":
        return mlp
    rest = [(("w_out", i), g["w_out"])]
    if i == 0:
        rest.append((("ev_w_in", 0), cols(g["w_in"].reshape(D, EV_IN))))
    else:
        od = g["w_in"].reshape(D, OD_PAD)
        od = jnp.concatenate([od[:, 0:416], od[:, 512:OD_PAD]], axis=1)
        ukv = jnp.concatenate([g["wkk"].reshape(128, 8, 96)[:, :, :64], g["wkv"].reshape(128, 8, 64)], axis=2)
        rest += [(("od_w_in", 0), cols(od)), (("od_w_uq", 0), cols(g["wq"])),
                 (("od_w_ukv", 0), cols(ukv.reshape(128, 1024)))]
    return rest if part == "rest" else mlp + rest


def big_grads_hs(grads):
    d = dict(layer_grads_hs(0, grads[0]) + layer_grads_hs(1, grads[1]))
    return dict(ev_w_in=d[("ev_w_in", 0)], od_w_in=d[("od_w_in", 0)], od_w_uq=d[("od_w_uq", 0)],
                od_w_ukv=d[("od_w_ukv", 0)], w_out=[d[("w_out", 0)], d[("w_out", 1)]],
                mlp_w1=[d[("mlp_w1", 0)], d[("mlp_w1", 1)]], mlp_w2=[d[("mlp_w2", 0)], d[("mlp_w2", 1)]])


def grads_to_natural(grads, dfg):
    out = small_grads_natural(grads, dfg)
    hs = big_grads_hs(grads)

    def from_cols(a):
        return a.transpose(0, 2, 1, 3).reshape(2 * a.shape[2], 4 * a.shape[3])

    def from_rows(a):
        return a.transpose(1, 0, 2, 3).reshape(8 * a.shape[2], a.shape[3])

    out["ev_w_in"] = from_cols(hs["ev_w_in"])[None]
    out["od_w_in"] = from_cols(hs["od_w_in"])[None]
    out["od_w_uq"] = from_cols(hs["od_w_uq"])[None]
    out["od_w_ukv"] = from_cols(hs["od_w_ukv"])[None]
    out["w_out"] = jnp.stack([from_rows(a) for a in hs["w_out"]])
    out["mlp_w1"] = jnp.stack([from_cols(a) for a in hs["mlp_w1"]])
    out["mlp_w2"] = jnp.stack([from_rows(a) for a in hs["mlp_w2"]])
    return out


WEIGHT_NAMES = ['c_ctx', 'ada_w', 'ada_b', 'norm1_g', 'norm2_g', 'w_out', 'mlp_w1', 'mlp_w2', 'ev_w_in',
                'ev_q_norm_g', 'ev_k_norm_g', 'ev_sgu_norm_g', 'ev_sgu_w', 'ev_sgu_b', 'od_w_in', 'od_q_norm_g',
                'od_kv_norm_g', 'od_w_uq', 'od_w_ukv', 'od_conv_w', 'od_conv_b', 'od_ln_g', 'od_ln_b', 'final_g']
REPL_SMALL = ['norm1_g', 'norm2_g', 'ev_q_norm_g', 'ev_k_norm_g', 'ev_sgu_norm_g', 'ev_sgu_w', 'ev_sgu_b',
              'od_kv_norm_g', 'final_g']
SHARD_SMALL = ['od_q_norm_g', 'od_conv_w', 'od_conv_b', 'od_ln_g', 'od_ln_b']
BIG = ['w_out', 'mlp_w1', 'mlp_w2', 'ev_w_in', 'od_w_in', 'od_w_uq', 'od_w_ukv']


def _gather_last(parts):
    return jnp.concatenate([parts[k] for k in range(4)], axis=-1)


class _Reduce:
    def __init__(self, tag, named, half, where):
        self.tag, self.half, self.where = tag, half, where
        self.names, self.hs = zip(*named)
        self.hs = list(self.hs)

    def to_sibling(self):
        lands = [lax.empty(a.shape[1:], BF) for a in self.hs]
        (self.h1,), token = exchange_start(f"rs_sibling_start_{self.tag}", [(self.hs, lands)], TO_SIBLING)
        return token

    def to_chips(self, after):
        hs, got = exchange_wait(f"rs_sibling_wait_{self.tag}", self.h1, after, TO_SIBLING)
        pair = [add_pairs(f"rs_add_{self.tag}_{k}", a, b, self.half) for k, (a, b) in enumerate(zip(hs, got))]
        lands = [lax.empty(p.shape, BF) for p in pair]
        (self.h2,), token = exchange_start(f"rs_chips_start_{self.tag}", [(pair, lands)], SCATTER)
        return token

    def finish(self, after, bufs):
        pair, land = exchange_wait(f"rs_chips_wait_{self.tag}", self.h2, after, SCATTER)
        for k, ((n, idx), l, p) in enumerate(zip(self.names, land, pair)):
            bufs[n] = sum_slabs(f"rs_sum_{self.tag}_{k}", l, p, self.where, bufs[n], idx)


def kernel(x, c, ctx, c_ctx, ada_w, ada_b, norm1_g, norm2_g, w_out, mlp_w1, mlp_w2, ev_w_in, ev_q_norm_g, ev_k_norm_g, ev_sgu_norm_g, ev_sgu_w, ev_sgu_b, od_w_in, od_q_norm_g, od_kv_norm_g, od_w_uq, od_w_ukv, od_conv_w, od_conv_b, od_ln_g, od_ln_b, final_g, loss_target, m_c_ctx, m_ada_w, m_ada_b, m_norm1_g, m_norm2_g, m_w_out, m_mlp_w1, m_mlp_w2, m_ev_w_in, m_ev_q_norm_g, m_ev_k_norm_g, m_ev_sgu_norm_g, m_ev_sgu_w, m_ev_sgu_b, m_od_w_in, m_od_q_norm_g, m_od_kv_norm_g, m_od_w_uq, m_od_w_ukv, m_od_conv_w, m_od_conv_b, m_od_ln_g, m_od_ln_b, m_final_g, v_c_ctx, v_ada_w, v_ada_b, v_norm1_g, v_norm2_g, v_w_out, v_mlp_w1, v_mlp_w2, v_ev_w_in, v_ev_q_norm_g, v_ev_k_norm_g, v_ev_sgu_norm_g, v_ev_sgu_w, v_ev_sgu_b, v_od_w_in, v_od_q_norm_g, v_od_kv_norm_g, v_od_w_uq, v_od_w_ukv, v_od_conv_w, v_od_conv_b, v_od_ln_g, v_od_ln_b, v_final_g):
    w = dict(c_ctx=c_ctx, ada_w=ada_w, ada_b=ada_b, norm1_g=norm1_g, norm2_g=norm2_g, w_out=w_out, mlp_w1=mlp_w1,
             mlp_w2=mlp_w2, ev_w_in=ev_w_in, ev_q_norm_g=ev_q_norm_g, ev_k_norm_g=ev_k_norm_g,
             ev_sgu_norm_g=ev_sgu_norm_g, ev_sgu_w=ev_sgu_w, ev_sgu_b=ev_sgu_b, od_w_in=od_w_in,
             od_q_norm_g=od_q_norm_g, od_kv_norm_g=od_kv_norm_g, od_w_uq=od_w_uq, od_w_ukv=od_w_ukv,
             od_conv_w=od_conv_w, od_conv_b=od_conv_b, od_ln_g=od_ln_g, od_ln_b=od_ln_b, final_g=final_g)
    mom = dict(c_ctx=m_c_ctx, ada_w=m_ada_w, ada_b=m_ada_b, norm1_g=m_norm1_g, norm2_g=m_norm2_g, w_out=m_w_out,
               mlp_w1=m_mlp_w1, mlp_w2=m_mlp_w2, ev_w_in=m_ev_w_in, ev_q_norm_g=m_ev_q_norm_g,
               ev_k_norm_g=m_ev_k_norm_g, ev_sgu_norm_g=m_ev_sgu_norm_g, ev_sgu_w=m_ev_sgu_w, ev_sgu_b=m_ev_sgu_b,
               od_w_in=m_od_w_in, od_q_norm_g=m_od_q_norm_g, od_kv_norm_g=m_od_kv_norm_g, od_w_uq=m_od_w_uq,
               od_w_ukv=m_od_w_ukv, od_conv_w=m_od_conv_w, od_conv_b=m_od_conv_b, od_ln_g=m_od_ln_g,
               od_ln_b=m_od_ln_b, final_g=m_final_g)
    var = dict(c_ctx=v_c_ctx, ada_w=v_ada_w, ada_b=v_ada_b, norm1_g=v_norm1_g, norm2_g=v_norm2_g, w_out=v_w_out,
               mlp_w1=v_mlp_w1, mlp_w2=v_mlp_w2, ev_w_in=v_ev_w_in, ev_q_norm_g=v_ev_q_norm_g,
               ev_k_norm_g=v_ev_k_norm_g, ev_sgu_norm_g=v_ev_sgu_norm_g, ev_sgu_w=v_ev_sgu_w, ev_sgu_b=v_ev_sgu_b,
               od_w_in=v_od_w_in, od_q_norm_g=v_od_q_norm_g, od_kv_norm_g=v_od_kv_norm_g, od_w_uq=v_od_w_uq,
               od_w_ukv=v_od_w_ukv, od_conv_w=v_od_conv_w, od_conv_b=v_od_conv_b, od_ln_g=v_od_ln_g,
               od_ln_b=v_od_ln_b, final_g=v_final_g)
    xi, yi, ci = lax.axis_index("x"), lax.axis_index("y"), lax.axis_index("c")
    chip = 2 * xi + yi
    dev = 2 * chip + ci

    shard_shapes = [w[n].shape for n in SHARD_SMALL]
    g0 = all_gather8("ag_small", _pack([c] + [w[n] for n in SHARD_SMALL]))
    g0 = g0.reshape(8, -1, D)
    parts = _unpack(g0, [c.shape] + shard_shapes)
    c_all = parts[0].reshape(16, D)
    small_full = {n: _gather_last(p[0::2]) for n, p in zip(SHARD_SMALL, parts[1:])}
    call = jnp.concatenate([c_all, c_ctx.reshape(1, D), jnp.zeros((NC - 17, D), F32)], axis=0)

    cols = ada_w.shape[2]
    ada_b_sh = lax.dynamic_slice(ada_b, (0, chip * cols), (2, cols)).reshape(2, 1, cols)
    mt = mods_fwd(call, ada_w, ada_b_sh)
    mt = all_gather8("ag_mods", mt.reshape(2 * NC, cols)).reshape(8, 2, NC, cols)
    table = mt[0::2].transpose(1, 2, 0, 3).reshape(2, NC, 4 * cols)
    mods = []
    for i in range(2):
        lat = lax.dynamic_slice(table[i], (2 * dev, 0), (2, 4 * cols))
        mc = table[i, 16]
        mods.append(jnp.stack([mc, lat[0], mc, lat[1]]).reshape(4 * N_MOD, 1, D))

    order = [[("ev_w_in", 0)], [("w_out", 0), ("mlp_w1", 0), ("mlp_w2", 0)],
             [("od_w_in", 0), ("od_w_uq", 0), ("od_w_ukv", 0), ("w_out", 1), ("mlp_w1", 1), ("mlp_w2", 1)]]
    groups = []
    for names in order:
        srcs = [w[n][i].astype(BF) for n, i in names]
        groups.append((srcs, [lax.empty((4,) + s.shape, BF) for s in srcs]))
    groups[0][0][0], table = lax.optimization_barrier((groups[0][0][0], table))
    handles, token = exchange_start("gather_start", groups, GATHER)
    mods[0] = mods[0] + token[0, 0]
    small = {n: w[n] for n in REPL_SMALL}
    small.update(small_full)
    prms = small_params(small)

    chip1 = chip.reshape(1).astype(jnp.int32)

    def arrived(k, after):
        srcs, lands = exchange_wait(f"gather_wait_{k}", handles[k], after, GATHER)
        return [place_own(f"gather_own_{k}_{a}", l, s, chip1) for a, (l, s) in enumerate(zip(lands, srcs))]

    def arrive_ev_in(after):
        (ev,) = arrived(0, after)
        prms[0]["w_in"] = _gather_last(ev)

    def arrive_ev_rest(after):
        wo, w1, w2 = arrived(1, after)
        prms[0].update(w_out=wo.reshape(D, D), w1=w1, w2=w2)

    def arrive_od(after):
        od, uq, ukv, wo, w1, w2 = arrived(2, after)
        prms[1].update(odd_in_params(_gather_last(od), _gather_last(uq), _gather_last(ukv)),
                       w_out=wo.reshape(D, D), w1=w1, w2=w2)

    prms[0]["w_in"] = arrive_ev_in
    prms[0]["w_out"] = arrive_ev_rest
    prms[1]["w_in"] = arrive_od

    half = ci.reshape(1).astype(jnp.int32)
    where = jnp.stack([chip, ci]).astype(jnp.int32)
    red = {}

    def hook(point, g, fresh):
        if point == "1:end":
            red["l1"] = _Reduce("l1", layer_grads_hs(1, g, "all"), half, where)
            return red["l1"].to_sibling()
        if point == "0:mlp":
            red["l0_mlp"] = _Reduce("l0_mlp", layer_grads_hs(0, g, "mlp"), half, where)
            return red["l1"].to_chips(fresh) + red["l0_mlp"].to_sibling()
        if point == "0:mid":
            return red["l0_mlp"].to_chips(fresh)
        if point == "0:end":
            red["l0_rest"] = _Reduce("l0_rest", layer_grads_hs(0, g, "rest"), half, where)
            return red["l0_rest"].to_sibling()
        return None

    xcat = jnp.concatenate([ctx, x], axis=1).reshape(R, D)
    loss_p, dx, dmods, grads, dfg = local_step(xcat, loss_target.reshape(NEX * L, D), mods, prms,
                                               final_g.reshape(1, D), hook)
    grad_x = dx.reshape(NEX, L, D)

    sg = small_grads_natural(grads, dfg)
    dm = jnp.stack([d.reshape(4, N_MOD * D) for d in dmods])
    small_names = REPL_SMALL + SHARD_SMALL
    items = [dm[:, 1::2], dm[:, 0] + dm[:, 2]] + [sg[n] for n in small_names] + [loss_p[0:1, 0:1]]
    shapes = [a.shape for a in items]
    g1 = all_gather8("ag_grads", _pack(items))
    red["l0_rest"].to_chips(g1)
    rows1 = g1.shape[0] // 8
    g1 = g1.reshape(8, rows1, D)
    tot = _unpack(sum_lead("sum_small", g1), shapes)
    dm_lat = _unpack(g1, shapes[:1])[0]
    dm_lat = dm_lat.transpose(1, 0, 2, 3).reshape(2, 16, N_MOD * D)
    dm_all = jnp.concatenate([dm_lat, tot[1][:, None], jnp.zeros((2, NC - 17, N_MOD * D), F32)], axis=1)
    gsum = dict(zip(small_names, tot[2:2 + len(small_names)]))
    loss = tot[-1].reshape(())
    grad = {n: gsum[n].reshape(w[n].shape) for n in REPL_SMALL}
    for n in SHARD_SMALL:
        k = w[n].shape[-1]
        grad[n] = lax.dynamic_slice_in_dim(gsum[n], chip * k, k, axis=gsum[n].ndim - 1)
    grad["ada_b"] = sum_lead("sum_ada_b", dm_all.transpose(1, 0, 2).reshape(NC, 2 * N_MOD, D)).reshape(2, N_MOD * D)

    dm_sh = lax.dynamic_slice(dm_all, (0, 0, chip * cols), (2, NC, cols))
    grad["ada_w"], dcc = ada_bwd(call, ada_w, dm_sh)
    dcc = all_gather8("ag_cctx", dcc).reshape(8, 8, D)
    grad["c_ctx"] = sum_lead("sum_cctx", dcc[0::2])[0]

    bufs = {n: lax.empty((w[n].shape[0], 2, w[n].shape[1] // 2, w[n].shape[2]), F32) for n in BIG}
    for tag in ("l1", "l0_mlp", "l0_rest"):
        red[tag].finish(dcc, bufs)
    for n, full in zip(BIG, sibling_merge("rs_sibling_merge", [bufs[n] for n in BIG])):
        grad[n] = full.reshape(w[n].shape)

    delta, new_m, new_v = {}, {}, {}
    for n in ['ada_w'] + BIG:
        shp = w[n].shape
        two_d = (shp[0] * shp[1], shp[2])
        d_, m_, v_ = adamw(f"adamw_{n}", w[n].reshape(two_d), grad[n].reshape(two_d), mom[n].reshape(two_d),
                           var[n].reshape(two_d))
        delta[n], new_m[n], new_v[n] = d_.reshape(shp), m_.reshape(shp), v_.reshape(shp)
    rest = [n for n in WEIGHT_NAMES if n not in ['ada_w'] + BIG]
    rshapes = [w[n].shape for n in rest]
    d_, m_, v_ = adamw("adamw_small", _pack([w[n] for n in rest]), _pack([grad[n] for n in rest]),
                       _pack([mom[n] for n in rest]), _pack([var[n] for n in rest]))
    for dst, buf in ((delta, d_), (new_m, m_), (new_v, v_)):
        dst.update(zip(rest, _unpack(buf, rshapes)))

    return (loss, grad_x, *[grad[n] for n in WEIGHT_NAMES], *[delta[n] for n in WEIGHT_NAMES],
            *[new_m[n] for n in WEIGHT_NAMES], *[new_v[n] for n in WEIGHT_NAMES])
```

```python
import functools
import math

import numpy as np
import jax
import jax.numpy as jnp
from jax import lax
from jax.experimental import pallas as pl
from jax.experimental.pallas import tpu as pltpu

F32 = jnp.float32
BF = jnp.bfloat16
HI = lax.Precision.HIGHEST
MESH = pl.DeviceIdType.MESH

D = 1024
L = 2048
LC = 256
SEQ = L + LC
NEX = 2
R = NEX * SEQ
TB = 256
BPE = SEQ // TB
NBLK = R // TB
GRID_W = 64
FF = 4 * D
EPS = 1e-6
ROPE_THETA = 10000.0
N_MOD = 6
EV_IN = 1792
OD_IN = 1440
OD_PAD = 1536
VMEM_LIMIT = 60 * 1024 * 1024

ADAM_LR = 0.001
ADAM_B1 = 0.9
ADAM_B2 = 0.999
ADAM_EPS = 1e-08
ADAM_WD = 0.01
ADAM_STEP = 10

NT = (((1,), (1,)), ((), ()))
TN = (((0,), (0,)), ((), ()))


def _cparams(sem=None):
    return pltpu.CompilerParams(dimension_semantics=sem, vmem_limit_bytes=VMEM_LIMIT)


@jax.custom_vjp
def _mm(a, b):
    return jnp.dot(a.astype(BF), b.astype(BF), preferred_element_type=F32)


def _mm_fwd(a, b):
    return _mm(a, b), (a, b)


def _mm_bwd(res, g):
    a, b = res
    gb = g.astype(BF)
    da = lax.dot_general(gb, b.astype(BF), NT, preferred_element_type=F32)
    db = lax.dot_general(a.astype(BF), gb, TN, preferred_element_type=F32)
    return da, db


_mm.defvjp(_mm_fwd, _mm_bwd)


@jax.custom_vjp
def _swap(x):
    n = x.shape[-1]
    ax = x.ndim - 1
    lane = lax.broadcasted_iota(jnp.int32, x.shape, ax)
    return jnp.where(lane % 2 == 0, pltpu.roll(x, n - 1, ax), pltpu.roll(x, 1, ax))


_swap.defvjp(lambda x: (_swap(x), None), lambda _, g: (_swap(g),))


def _rope(x, cos, sin):
    return x * cos + _swap(x) * sin


def _rmsn(x, g):
    return x * lax.rsqrt(jnp.mean(x * x, axis=-1, keepdims=True) + EPS) * g


def _split_dot(a, m):
    hi = a.astype(BF)
    lo = (a - hi.astype(F32)).astype(BF)
    return jnp.dot(hi, m, preferred_element_type=F32) + jnp.dot(lo, m, preferred_element_type=F32)


@jax.custom_vjp
def _group_mean(a, avg):
    return _split_dot(a, avg)


_group_mean.defvjp(lambda a, avg: (_split_dot(a, avg), avg),
                   lambda avg, g: (_split_dot(g, avg), jnp.zeros_like(avg)))


def _grmsn(x, g, avg):
    return x * lax.rsqrt(_group_mean(x * x, avg) + EPS) * g


def _modnorm(x, g, sh, sc):
    return _rmsn(x, g) * (1.0 + sc) + sh


def _gelu(x):
    return 0.5 * x * (1.0 + jnp.tanh(0.7978845608028654 * (x + 0.044715 * (x * x * x))))


def _silu(x):
    return x * jax.nn.sigmoid(x)


def _acc(ref, val, first):
    @pl.when(first)
    def _():
        ref[...] = val

    @pl.when(jnp.logical_not(first))
    def _():
        ref[...] += val


def _seg(i):
    return 2 * (i // BPE) + jnp.minimum(i % BPE, 1)


def _seg_first(i):
    return (i % BPE) <= 1


def _rb_call(name, body, row_in=(), mod_in=(), pos_in=(), full_in=(), shift_in=(),
             row_out=(), seg_out=(), acc_out=(), scratch=(), after=None):
    in_specs, args = [], []
    for a in row_in:
        in_specs.append(pl.BlockSpec((TB, a.shape[1]), lambda i: (i, 0)))
        args.append(a)
    for tab, m in mod_in:
        in_specs.append(pl.BlockSpec((1, 1, D), lambda i, m=m: (_seg(i) * N_MOD + m, 0, 0)))
        args.append(tab)
    for a in pos_in:
        in_specs.append(pl.BlockSpec((TB, a.shape[1]), lambda i: (i % BPE, 0)))
        args.append(a)
    for a in full_in:
        in_specs.append(pl.BlockSpec(a.shape, lambda i, n=a.ndim: (0,) * n))
        args.append(a)
    for a, d in shift_in:
        in_specs.append(pl.BlockSpec((TB, a.shape[1]), lambda i, d=d: (jnp.clip(i + d, 0, NBLK - 1), 0)))
        args.append(a)
    n_in = len(args)
    if after is not None:
        in_specs.append(pl.BlockSpec(after.shape, lambda i, n=after.ndim: (0,) * n))
        args.append(after)
    out_specs, out_shape = [], []
    for w, dt, *lat in row_out:
        if lat:
            out_specs.append(pl.BlockSpec(
                (TB, w), lambda i: ((i // BPE) * (L // TB) + jnp.maximum(i % BPE - 1, 0), 0)))
            out_shape.append(jax.ShapeDtypeStruct((NEX * L, w), dt))
        else:
            out_specs.append(pl.BlockSpec((TB, w), lambda i: (i, 0)))
            out_shape.append(jax.ShapeDtypeStruct((R, w), dt))
    for w in seg_out:
        out_specs.append(pl.BlockSpec((1, 1, w), lambda i: (_seg(i), 0, 0)))
        out_shape.append(jax.ShapeDtypeStruct((4, 1, w), F32))
    for shp in acc_out:
        out_specs.append(pl.BlockSpec(shp, lambda i, n=len(shp): (0,) * n))
        out_shape.append(jax.ShapeDtypeStruct(shp, F32))

    def kern(*refs):
        body(pl.program_id(0), *refs[:n_in], *refs[len(args):])

    sem = ("arbitrary",) if (seg_out or acc_out or any(len(r) > 2 for r in row_out)) else ("parallel",)
    return pl.pallas_call(kern, grid=(NBLK,), in_specs=in_specs, out_specs=out_specs, out_shape=out_shape,
                          scratch_shapes=list(scratch), compiler_params=_cparams(sem), name=name)(*args)


def modnorm_fwd(name, x, mods, g, m_sh, m_sc):
    def body(i, x_ref, sh_ref, sc_ref, g_ref, h_ref):
        h_ref[...] = _modnorm(x_ref[...], g_ref[...], sh_ref[0], sc_ref[0]).astype(BF)

    return _rb_call(name, body, row_in=(x,), mod_in=((mods, m_sh), (mods, m_sc)), full_in=(g,),
                    row_out=((D, BF),))[0]


def _gate_grads(dx, y_ref, gt_ref, dy_ref, dgt_ref, i):
    dy_ref[...] = (dx * gt_ref[0]).astype(BF)
    _acc(dgt_ref, jnp.sum(dx * y_ref[...].astype(F32), axis=0, keepdims=True)[None], _seg_first(i))


def modnorm_bwd(name, x, dh, dx_in, mods, g, m_sh, m_sc, gate=None, after=None, lat_only=False):
    def body(i, x_ref, dh_ref, dxin_ref, *rest):
        if gate:
            y_ref, sh_ref, sc_ref, gt_ref, g_ref, dx_ref, dy_ref, dgt_ref, dsh_ref, dsc_ref, dg_ref = rest
        else:
            sh_ref, sc_ref, g_ref, dx_ref, dsh_ref, dsc_ref, dg_ref = rest
        _, vjp = jax.vjp(_modnorm, x_ref[...], g_ref[...], sh_ref[0], sc_ref[0])
        dx, dg, dsh, dsc = vjp(dh_ref[...].astype(F32))
        dx = dxin_ref[...] + dx
        dx_ref[...] = dx
        if gate:
            _gate_grads(dx, y_ref, gt_ref, dy_ref, dgt_ref, i)
        _acc(dsh_ref, dsh[None], _seg_first(i))
        _acc(dsc_ref, dsc[None], _seg_first(i))
        _acc(dg_ref, dg, i == 0)

    if gate:
        y, gmods, m = gate
        return _rb_call(name, body, row_in=(x, dh, dx_in, y), mod_in=((mods, m_sh), (mods, m_sc), (gmods, m)),
                        full_in=(g,), row_out=((D, F32), (D, BF)), seg_out=(D, D, D), acc_out=((1, D),), after=after)
    return _rb_call(name, body, row_in=(x, dh, dx_in), mod_in=((mods, m_sh), (mods, m_sc)), full_in=(g,),
                    row_out=((D, F32, "lat") if lat_only else (D, F32),), seg_out=(D, D), acc_out=((1, D),),
                    after=after)


def proj_in(name, h, w):
    n = w.shape[1]

    def body(i, h_ref, w_ref, o_ref):
        o_ref[...] = jnp.dot(h_ref[...], w_ref[...], preferred_element_type=F32).astype(BF)

    return _rb_call(name, body, row_in=(h,), full_in=(w,), row_out=((n, BF),))[0]


def proj_out(name, a1, a2, w, x, mods, m_gate, g_next, m_sh, m_sc):
    k1 = a1.shape[1]

    def body(i, a1_ref, a2_ref, x_ref, gt_ref, sh_ref, sc_ref, w_ref, g_ref, xo_ref, y_ref, h_ref):
        y = jnp.dot(a1_ref[...], w_ref[:k1, :], preferred_element_type=F32)
        y = y + jnp.dot(a2_ref[...], w_ref[k1:, :], preferred_element_type=F32)
        y_ref[...] = y.astype(BF)
        xn = x_ref[...] + gt_ref[0] * y
        xo_ref[...] = xn
        h_ref[...] = _modnorm(xn, g_ref[...], sh_ref[0], sc_ref[0]).astype(BF)

    return _rb_call(name, body, row_in=(a1, a2, x), mod_in=((mods, m_gate), (mods, m_sh), (mods, m_sc)),
                    full_in=(w, g_next), row_out=((D, F32), (D, BF), (D, BF)))


def mlp_up(name, h, w1):
    def body(i, h_ref, w_ref, a_ref, f_ref):
        hv = h_ref[...]
        for n in range(4):
            a = jnp.dot(hv, w_ref[n], preferred_element_type=F32)
            a_ref[:, n * D:(n + 1) * D] = a.astype(BF)
            r = jnp.maximum(a, 0.0)
            f_ref[:, n * D:(n + 1) * D] = (r * r).astype(BF)

    return _rb_call(name, body, row_in=(h,), full_in=(w1,), row_out=((FF, BF), (FF, BF)))


def mlp_down(name, f, w2, x, mods, m_gate, nxt=None):
    def body(i, f_ref, x_ref, gt_ref, *rest):
        if nxt:
            sh_ref, sc_ref, w_ref, g_ref, xo_ref, y_ref, h_ref = rest
        else:
            w_ref, xo_ref, y_ref = rest
        y = jnp.dot(f_ref[:, 0:D], w_ref[0], preferred_element_type=F32)
        for n in range(1, 4):
            y = y + jnp.dot(f_ref[:, n * D:(n + 1) * D], w_ref[n], preferred_element_type=F32)
        xn = x_ref[...] + gt_ref[0] * y
        y_ref[...] = y.astype(BF)
        xo_ref[...] = xn
        if nxt:
            h_ref[...] = _modnorm(xn, g_ref[...], sh_ref[0], sc_ref[0]).astype(BF)

    if nxt:
        return _rb_call(name, body, row_in=(f, x), mod_in=((mods, m_gate), (nxt[0], 0), (nxt[0], 1)),
                        full_in=(w2, nxt[1]), row_out=((D, F32), (D, BF), (D, BF)))
    return _rb_call(name, body, row_in=(f, x), mod_in=((mods, m_gate),), full_in=(w2,),
                    row_out=((D, F32), (D, BF)))


def mm_nt(name, g, w, after=None):
    k = w.shape[0]

    def body(i, g_ref, w_ref, o_ref):
        o_ref[...] = lax.dot_general(g_ref[...], w_ref[...], NT, preferred_element_type=F32).astype(BF)

    return _rb_call(name, body, row_in=(g,), full_in=(w,), row_out=((k, BF),), after=after)[0]


def mlp_bwd_da(name, dy, w2, a, after=None):
    def body(i, dy_ref, a_ref, w_ref, da_ref):
        dyv = dy_ref[...]
        for n in range(4):
            df = lax.dot_general(dyv, w_ref[n], NT, preferred_element_type=F32)
            av = a_ref[:, n * D:(n + 1) * D].astype(F32)
            da_ref[:, n * D:(n + 1) * D] = (df * (2.0 * jnp.maximum(av, 0.0))).astype(BF)

    return _rb_call(name, body, row_in=(dy, a), full_in=(w2,), row_out=((FF, BF),), after=after)[0]


def mlp_bwd_dh(name, da, w1):
    def body(i, da_ref, w_ref, dh_ref):
        acc = lax.dot_general(da_ref[:, 0:D], w_ref[0], NT, preferred_element_type=F32)
        for n in range(1, 4):
            acc = acc + lax.dot_general(da_ref[:, n * D:(n + 1) * D], w_ref[n], NT, preferred_element_type=F32)
        dh_ref[...] = acc.astype(BF)

    return _rb_call(name, body, row_in=(da,), full_in=(w1,), row_out=((D, BF),))[0]


TN_ROWS = 1536


def mm_tn(name, a, g, tiles, th, tw):
    nt = len(tiles)
    acs = jnp.asarray([t[0] for t in tiles], jnp.int32)
    gcs = jnp.asarray([t[1] for t in tiles], jnp.int32)
    nr = R // TN_ROWS

    def kern(ac_ref, gc_ref, a_ref, g_ref, o_ref, acc_ref):
        r = pl.program_id(1)

        @pl.when(r == 0)
        def _():
            acc_ref[...] = jnp.zeros_like(acc_ref)

        acc_ref[...] += lax.dot_general(a_ref[...], g_ref[...], TN, preferred_element_type=F32)

        @pl.when(r == nr - 1)
        def _():
            o_ref[...] = acc_ref[...].astype(BF)

    grid_spec = pltpu.PrefetchScalarGridSpec(
        num_scalar_prefetch=2, grid=(nt, nr),
        in_specs=[pl.BlockSpec((TN_ROWS, th), lambda t, r, ac, gc: (r, ac[t])),
                  pl.BlockSpec((TN_ROWS, tw), lambda t, r, ac, gc: (r, gc[t]))],
        out_specs=pl.BlockSpec((None, th, tw), lambda t, r, ac, gc: (t, 0, 0)),
        scratch_shapes=[pltpu.VMEM((th, tw), F32)])
    return pl.pallas_call(kern, grid_spec=grid_spec, out_shape=jax.ShapeDtypeStruct((nt, th, tw), BF),
                          compiler_params=_cparams(("parallel", "arbitrary")), name=name)(acs, gcs, a, g)


def _even_tok(q, k, zu, zv, gq, gk, gs, ws, bs, cq, sq, ck, sk, avg, masks):
    qr = _rope(_grmsn(q, gq, avg), cq, sq) * GQA_SCALE
    kr = _rope(_grmsn(k, gk, avg[:128, :128]), ck, sk)
    u = _gelu(zu)
    v = _grmsn(_gelu(zv), gs, avg)
    sv = None
    for g in range(8):
        t = masks[g] * (_mm(ws[g], v) + bs[g])
        sv = t if sv is None else sv + t
    return qr, kr, u * sv


def even_tok_fwd(p, cos, sin, gq, gk, gs, sgu_w, sgu_b, avg, masks):
    def body(i, p_ref, cos_ref, sin_ref, gq_ref, gk_ref, gs_ref, w_ref, b_ref, avg_ref, mk_ref, q_ref, kv_ref, m_ref):
        avgv = avg_ref[...]
        ws = [w_ref[g] for g in range(8)]
        bs = [b_ref[g] for g in range(8)]
        mks = [mk_ref[g] for g in range(8)]
        for c in range(2):
            rs = pl.ds(c * 128, 128)
            qr, kr, m = _even_tok(
                p_ref[rs, 0:512].astype(F32), p_ref[rs, 512:640].astype(F32),
                p_ref[rs, 768:1280].astype(F32), p_ref[rs, 1280:1792].astype(F32),
                gq_ref[...], gk_ref[...], gs_ref[...], ws, bs,
                cos_ref[rs, :], sin_ref[rs, :], cos_ref[rs, 0:128], sin_ref[rs, 0:128], avgv, mks)
            q_ref[rs, :] = qr.astype(BF)
            kv_ref[rs, 0:128] = kr.astype(BF)
            kv_ref[rs, 128:256] = p_ref[rs, 640:768]
            m_ref[rs, :] = m.astype(BF)

    return _rb_call("even_tok_fwd", body, row_in=(p,), pos_in=(cos, sin),
                    full_in=(gq, gk, gs, sgu_w, sgu_b, avg, masks), row_out=((512, BF), (256, BF), (512, BF)))


def even_tok_bwd(p, dq, dkv, dcat, cos, sin, gq, gk, gs, sgu_w, sgu_b, avg, masks):
    def body(i, p_ref, dq_ref, dkv_ref, dcat_ref, cos_ref, sin_ref, gq_ref, gk_ref, gs_ref, w_ref, b_ref,
             avg_ref, mk_ref, dp_ref, dgq_ref, dgk_ref, dgs_ref, dw_ref, db_ref):
        avgv = avg_ref[...]
        ws = [w_ref[g] for g in range(8)]
        bs = [b_ref[g] for g in range(8)]
        mks = [mk_ref[g] for g in range(8)]
        tot = None
        for c in range(2):
            rs = pl.ds(c * 128, 128)
            cq, sq, ck, sk = cos_ref[rs, :], sin_ref[rs, :], cos_ref[rs, 0:128], sin_ref[rs, 0:128]

            def f(q, k, zu, zv, gq, gk, gs, ws, bs):
                return _even_tok(q, k, zu, zv, gq, gk, gs, ws, bs, cq, sq, ck, sk, avgv, mks)

            _, vjp = jax.vjp(f, p_ref[rs, 0:512].astype(F32), p_ref[rs, 512:640].astype(F32),
                             p_ref[rs, 768:1280].astype(F32), p_ref[rs, 1280:1792].astype(F32),
                             gq_ref[...], gk_ref[...], gs_ref[...], ws, bs)
            d = vjp((dq_ref[rs, :].astype(F32), dkv_ref[rs, 0:128], dcat_ref[rs, 512:1024].astype(F32)))
            dp_ref[rs, 0:512] = d[0].astype(BF)
            dp_ref[rs, 512:640] = d[1].astype(BF)
            dp_ref[rs, 640:768] = dkv_ref[rs, 128:256].astype(BF)
            dp_ref[rs, 768:1280] = d[2].astype(BF)
            dp_ref[rs, 1280:1792] = d[3].astype(BF)
            part = [d[4], d[5], d[6]] + list(d[7]) + list(d[8])
            tot = part if tot is None else [a + b for a, b in zip(tot, part)]
        refs = [dgq_ref, dgk_ref, dgs_ref] + [dw_ref.at[g] for g in range(8)] + [db_ref.at[g] for g in range(8)]
        for ref, val in zip(refs, tot):
            _acc(ref, val, i == 0)

    return _rb_call("even_tok_bwd", body, row_in=(p, dq, dkv, dcat), pos_in=(cos, sin),
                    full_in=(gq, gk, gs, sgu_w, sgu_b, avg, masks), row_out=((EV_IN, BF),),
                    acc_out=((1, 512), (1, 128), (1, 512), (8, 128, 128), (8, 128, 1)))


MLA_SCALE = 96 ** -0.5
GQA_SCALE = 64 ** -0.5


def _odd_tok(cq, ckv, kr, za, zg, gq, gkv, wq, wkk, wkv, spread, cr, sr, ck, sk):
    cqn = _rmsn(cq, gq)
    q = _rope(_mm(cqn, wq), cr, sr) * MLA_SCALE
    ckn = _rmsn(ckv, gkv)
    k = _mm(ckn, wkk) + _mm(_rope(kr, ck, sk), spread)
    v = _mm(ckn, wkv)
    y = za * jax.nn.sigmoid(zg)
    return q, k, v, y


def odd_tok_fwd(p, cos, sin, gq, gkv, wq, wkk, wkv, spread):
    def body(i, p_ref, cos_ref, sin_ref, gq_ref, gkv_ref, wq_ref, wkk_ref, wkv_ref, sp_ref, q_ref, kv_ref, y_ref):
        q, k, v, y = _odd_tok(
            p_ref[:, 0:256].astype(F32), p_ref[:, 256:384].astype(F32), p_ref[:, 384:512].astype(F32),
            p_ref[:, 512:1024].astype(F32), p_ref[:, 1024:1536].astype(F32),
            gq_ref[...], gkv_ref[...], wq_ref[...], wkk_ref[...], wkv_ref[...], sp_ref[...],
            cos_ref[:, 0:768], sin_ref[:, 0:768], cos_ref[:, 768:896], sin_ref[:, 768:896])
        q_ref[...] = q.astype(BF)
        kv_ref[:, 0:768] = k.astype(BF)
        kv_ref[:, 768:1280] = v.astype(BF)
        y_ref[...] = y.astype(BF)

    return _rb_call("odd_tok_fwd", body, row_in=(p,), pos_in=(cos, sin), full_in=(gq, gkv, wq, wkk, wkv, spread),
                    row_out=((768, BF), (1280, BF), (512, BF)))


def odd_tok_bwd(p, dq, dkv, dy, cos, sin, gq, gkv, wq, wkk, wkv, spread):
    def body(i, p_ref, dq_ref, dkv_ref, dy_ref, cos_ref, sin_ref, gq_ref, gkv_ref, wq_ref, wkk_ref, wkv_ref, sp_ref,
             dp_ref, dgq_ref, dgkv_ref, dwq_ref, dwkk_ref, dwkv_ref):
        cr, sr, ck, sk = cos_ref[:, 0:768], sin_ref[:, 0:768], cos_ref[:, 768:896], sin_ref[:, 768:896]
        spread_v = sp_ref[...]

        def f(cq, ckv, kr, za, zg, gq, gkv, wq, wkk, wkv):
            return _odd_tok(cq, ckv, kr, za, zg, gq, gkv, wq, wkk, wkv, spread_v, cr, sr, ck, sk)

        _, vjp = jax.vjp(f, p_ref[:, 0:256].astype(F32), p_ref[:, 256:384].astype(F32),
                         p_ref[:, 384:512].astype(F32), p_ref[:, 512:1024].astype(F32),
                         p_ref[:, 1024:1536].astype(F32), gq_ref[...], gkv_ref[...], wq_ref[...],
                         wkk_ref[...], wkv_ref[...])
        d = vjp((dq_ref[...].astype(F32), dkv_ref[:, 0:768], dkv_ref[:, 768:1280], dy_ref[...].astype(F32)))
        dp_ref[:, 0:256] = d[0].astype(BF)
        dp_ref[:, 256:384] = d[1].astype(BF)
        dp_ref[:, 384:512] = d[2].astype(BF)
        dp_ref[:, 512:1024] = d[3].astype(BF)
        dp_ref[:, 1024:1536] = d[4].astype(BF)
        for ref, val in zip((dgq_ref, dgkv_ref, dwq_ref, dwkk_ref, dwkv_ref), d[5:]):
            _acc(ref, val, i == 0)

    return _rb_call("odd_tok_bwd", body, row_in=(p, dq, dkv, dy), pos_in=(cos, sin),
                    full_in=(gq, gkv, wq, wkk, wkv, spread), row_out=((OD_PAD, BF),),
                    acc_out=((1, 256), (1, 128), (256, 768), (128, 768), (128, 512)))


GQA_HEADS = [(64 * h, 64 * (h // 4), 64, 128 + 64 * (h // 4)) for h in range(8)]
MLA_HEADS = [(96 * h, 96 * h, 96, 768 + 64 * h) for h in range(8)]


def _by_block(j, run):
    @pl.when(j == 0)
    def _():
        run(LC)

    @pl.when(j > 0)
    def _():
        run(SEQ)


def attn_fwd(name, q, kv, heads):
    qw, kvw = q.shape[1], kv.shape[1]

    def kern(q_ref, kv_ref, o_ref, lse_ref):
        def run(nk):
            for h, (qo, ko, w, vo) in enumerate(heads):
                s = lax.dot_general(q_ref[:, qo:qo + w], kv_ref[0:nk, ko:ko + w], NT, preferred_element_type=F32)
                m = jnp.max(s, axis=-1, keepdims=True)
                p = jnp.exp(s - m)
                l = jnp.sum(p, axis=-1, keepdims=True)
                o = jnp.dot(p.astype(BF), kv_ref[0:nk, vo:vo + 64], preferred_element_type=F32) / l
                o_ref[:, 64 * h:64 * h + 64] = o.astype(BF)
                lse_ref[:, h:h + 1] = m + jnp.log(l)

        _by_block(pl.program_id(1), run)

    return pl.pallas_call(
        kern, grid=(NEX, BPE),
        in_specs=[pl.BlockSpec((TB, qw), lambda e, j: (e * BPE + j, 0)),
                  pl.BlockSpec((SEQ, kvw), lambda e, j: (e, 0))],
        out_specs=[pl.BlockSpec((TB, 512), lambda e, j: (e * BPE + j, 0)),
                   pl.BlockSpec((TB, 8), lambda e, j: (e * BPE + j, 0))],
        out_shape=[jax.ShapeDtypeStruct((R, 512), BF), jax.ShapeDtypeStruct((R, 8), F32)],
        compiler_params=_cparams(("parallel", "arbitrary")), name=name)(q, kv)


def attn_bwd(name, q, kv, o, dcat, lse, heads):
    qw, kvw = q.shape[1], kv.shape[1]

    def kern(q_ref, kv_ref, o_ref, do_ref, lse_ref, dq_ref, dkv_ref):
        j = pl.program_id(1)

        @pl.when(j == 0)
        def _():
            dkv_ref[...] = jnp.zeros_like(dkv_ref)

        def run(nk):
            for h, (qo, ko, w, vo) in enumerate(heads):
                qh = q_ref[:, qo:qo + w]
                kh = kv_ref[0:nk, ko:ko + w]
                s = lax.dot_general(qh, kh, NT, preferred_element_type=F32)
                p = jnp.exp(s - lse_ref[:, h:h + 1])
                do = do_ref[:, 64 * h:64 * h + 64]
                dsum = jnp.sum(do.astype(F32) * o_ref[:, 64 * h:64 * h + 64].astype(F32), axis=-1, keepdims=True)
                dp = lax.dot_general(do, kv_ref[0:nk, vo:vo + 64], NT, preferred_element_type=F32)
                ds = (p * (dp - dsum)).astype(BF)
                dkv_ref[0:nk, vo:vo + 64] += lax.dot_general(p.astype(BF), do, TN, preferred_element_type=F32)
                dq_ref[:, qo:qo + w] = jnp.dot(ds, kh, preferred_element_type=F32).astype(BF)
                dkv_ref[0:nk, ko:ko + w] += lax.dot_general(ds, qh, TN, preferred_element_type=F32)

        _by_block(j, run)

    return pl.pallas_call(
        kern, grid=(NEX, BPE),
        in_specs=[pl.BlockSpec((TB, qw), lambda e, j: (e * BPE + j, 0)),
                  pl.BlockSpec((SEQ, kvw), lambda e, j: (e, 0)),
                  pl.BlockSpec((TB, 512), lambda e, j: (e * BPE + j, 0)),
                  pl.BlockSpec((TB, 512), lambda e, j: (e * BPE + j, 0)),
                  pl.BlockSpec((TB, 8), lambda e, j: (e * BPE + j, 0))],
        out_specs=[pl.BlockSpec((TB, qw), lambda e, j: (e * BPE + j, 0)),
                   pl.BlockSpec((SEQ, kvw), lambda e, j: (e, 0))],
        out_shape=[jax.ShapeDtypeStruct((R, qw), BF), jax.ShapeDtypeStruct((R, kvw), F32)],
        compiler_params=_cparams(("parallel", "arbitrary")), name=name)(q, kv, o, dcat, lse)


HALO = 16
CONV_K = 31


def _fill_ext(ext_ref, prev_ref, cur_ref, next_ref, i):
    j = i % BPE
    has_prev = (j >= 2).astype(F32)
    has_next = jnp.logical_and(j >= 1, j <= BPE - 2).astype(F32)
    ext_ref[0:HALO, :] = prev_ref[TB - HALO:TB, :].astype(F32) * has_prev
    ext_ref[HALO:HALO + TB, :] = cur_ref[...].astype(F32)
    ext_ref[HALO + TB:2 * HALO + TB, :] = next_ref[0:HALO, :].astype(F32) * has_next


PHASE_ROWS = TB + 24


def _phases(ext_ref, ph_ref):
    for r in range(8):
        ph_ref[r] = ext_ref[r:r + PHASE_ROWS, :]


def _window(ph_ref, off):
    return ph_ref[off % 8, 8 * (off // 8):8 * (off // 8) + TB, :]


def _ln_silu(z, g, b):
    mu = jnp.mean(z, axis=-1, keepdims=True)
    zc = z - mu
    var = jnp.mean(zc * zc, axis=-1, keepdims=True)
    return _silu(zc * lax.rsqrt(var + EPS) * g + b)


def conf_fwd(y, cw, cb, lg, lb):
    def body(i, cur_ref, cw_ref, cb_ref, lg_ref, lb_ref, prev_ref, next_ref, z_ref, c_ref, ext_ref, ph_ref):
        _fill_ext(ext_ref, prev_ref, cur_ref, next_ref, i)
        _phases(ext_ref, ph_ref)
        acc = _window(ph_ref, 1) * cw_ref[0:1, :]
        for k in range(1, CONV_K):
            acc = acc + _window(ph_ref, k + 1) * cw_ref[k:k + 1, :]
        z = acc + cb_ref[...]
        z_ref[...] = z.astype(BF)
        c_ref[...] = _ln_silu(z, lg_ref[...], lb_ref[...]).astype(BF)

    return _rb_call("conf_fwd", body, row_in=(y,), full_in=(cw, cb, lg, lb), shift_in=((y, -1), (y, 1)),
                    row_out=((512, BF), (512, BF)),
                    scratch=(pltpu.VMEM((TB + 2 * HALO, 512), F32), pltpu.VMEM((8, PHASE_ROWS, 512), F32)))


def conf_bwd_ln(z, dcat, lg, lb):
    def body(i, z_ref, dcat_ref, lg_ref, lb_ref, dz_ref, dlg_ref, dlb_ref, dcb_ref):
        _, vjp = jax.vjp(_ln_silu, z_ref[...].astype(F32), lg_ref[...], lb_ref[...])
        dz, dlg, dlb = vjp(dcat_ref[:, 512:1024].astype(F32))
        dz_ref[...] = dz.astype(BF)
        _acc(dlg_ref, dlg, i == 0)
        _acc(dlb_ref, dlb, i == 0)
        _acc(dcb_ref, jnp.sum(dz, axis=0, keepdims=True), i == 0)

    return _rb_call("conf_bwd_ln", body, row_in=(z, dcat), full_in=(lg, lb), row_out=((512, BF),),
                    acc_out=((1, 512), (1, 512), (1, 512)))


def conf_bwd_conv(y, dz, cw):
    def body(i, y_ref, dz_ref, cw_ref, yp_ref, yn_ref, dzp_ref, dzn_ref, dy_ref, dcw_ref, ext_ref, phy_ref, phd_ref):
        _fill_ext(ext_ref, yp_ref, y_ref, yn_ref, i)
        _phases(ext_ref, phy_ref)
        _fill_ext(ext_ref, dzp_ref, dz_ref, dzn_ref, i)
        _phases(ext_ref, phd_ref)
        dzv = dz_ref[...].astype(F32)

        @pl.when(i == 0)
        def _():
            dcw_ref[...] = jnp.zeros_like(dcw_ref)

        acc = None
        for k in range(CONV_K):
            t = _window(phd_ref, CONV_K - k) * cw_ref[k:k + 1, :]
            acc = t if acc is None else acc + t
            dcw_ref[k:k + 1, :] += jnp.sum(dzv * _window(phy_ref, k + 1), axis=0, keepdims=True)
        dy_ref[...] = acc.astype(BF)

    return _rb_call("conf_bwd_conv", body, row_in=(y, dz), full_in=(cw,),
                    shift_in=((y, -1), (y, 1), (dz, -1), (dz, 1)), row_out=((512, BF),), acc_out=((32, 512),),
                    scratch=(pltpu.VMEM((TB + 2 * HALO, 512), F32), pltpu.VMEM((8, PHASE_ROWS, 512), F32),
                             pltpu.VMEM((8, PHASE_ROWS, 512), F32)))


def final_loss(x, target, fg, y, mods, m_gate):
    lpb = L // TB

    def kern(x_ref, t_ref, g_ref, y_ref, gt_ref, dx_ref, dy_ref, dgt_ref, loss_ref, dg_ref):
        i = pl.program_id(0)
        lat = (i % BPE) >= 1
        xv, tv = x_ref[...], t_ref[...]

        def f(x, g):
            err = _rmsn(x, g) - tv
            rowsum = jnp.sum(err * err, axis=-1, keepdims=True)
            return jnp.sum(rowsum, axis=0, keepdims=True) * (0.5 / D)

        lv, vjp = jax.vjp(f, xv, g_ref[...])
        dx, dg = vjp(jnp.ones((1, 1), F32))
        m = lat.astype(F32)
        dx = dx * m
        dx_ref[...] = dx
        _gate_grads(dx, y_ref, gt_ref, dy_ref, dgt_ref, i)
        _acc(loss_ref, jnp.zeros((8, 128), F32) + lv * m, i == 0)
        _acc(dg_ref, dg * m, i == 0)

    row = pl.BlockSpec((TB, D), lambda i: (i, 0))
    return pl.pallas_call(
        kern, grid=(NBLK,),
        in_specs=[row, pl.BlockSpec((TB, D), lambda i: ((i // BPE) * lpb + jnp.maximum(i % BPE - 1, 0), 0)),
                  pl.BlockSpec((1, D), lambda i: (0, 0)), row,
                  pl.BlockSpec((1, 1, D), lambda i: (_seg(i) * N_MOD + m_gate, 0, 0))],
        out_specs=[row, row, pl.BlockSpec((1, 1, D), lambda i: (_seg(i), 0, 0)),
                   pl.BlockSpec((8, 128), lambda i: (0, 0)), pl.BlockSpec((1, D), lambda i: (0, 0))],
        out_shape=[jax.ShapeDtypeStruct((R, D), F32), jax.ShapeDtypeStruct((R, D), BF),
                   jax.ShapeDtypeStruct((4, 1, D), F32), jax.ShapeDtypeStruct((8, 128), F32),
                   jax.ShapeDtypeStruct((1, D), F32)],
        compiler_params=_cparams(("arbitrary",)), name="final_loss")(x, target, fg, y, mods)


NC = 24


def mods_fwd(call, ada_w, ada_b):
    cols = ada_w.shape[2]

    def kern(c_ref, w_ref, b_ref, o_ref):
        o_ref[...] = jnp.dot(_silu(c_ref[...]), w_ref[...], precision=HI, preferred_element_type=F32) + b_ref[...]

    return pl.pallas_call(
        kern, grid=(2,),
        in_specs=[pl.BlockSpec((NC, D), lambda l: (0, 0)), pl.BlockSpec((None, D, cols), lambda l: (l, 0, 0)),
                  pl.BlockSpec((None, 1, cols), lambda l: (l, 0, 0))],
        out_specs=pl.BlockSpec((None, NC, cols), lambda l: (l, 0, 0)),
        out_shape=jax.ShapeDtypeStruct((2, NC, cols), F32),
        compiler_params=_cparams(("parallel",)), name="mods_fwd")(call, ada_w, ada_b)


def ada_bwd(call, ada_w, dm):
    cols = ada_w.shape[2]

    def kern(c_ref, w_ref, dm_ref, gw_ref, dc_ref):
        l = pl.program_id(0)
        gw_ref[...] = lax.dot_general(_silu(c_ref[...]), dm_ref[...], TN, precision=HI, preferred_element_type=F32)
        part = lax.dot_general(dm_ref[16:24, :], w_ref[...], NT, precision=HI, preferred_element_type=F32)
        cc = c_ref[16:17, :]
        sg = jax.nn.sigmoid(cc)
        _acc(dc_ref, part * (sg * (1.0 + cc * (1.0 - sg))), l == 0)

    return pl.pallas_call(
        kern, grid=(2,),
        in_specs=[pl.BlockSpec((NC, D), lambda l: (0, 0)), pl.BlockSpec((None, D, cols), lambda l: (l, 0, 0)),
                  pl.BlockSpec((None, NC, cols), lambda l: (l, 0, 0))],
        out_specs=[pl.BlockSpec((None, D, cols), lambda l: (l, 0, 0)), pl.BlockSpec((8, D), lambda l: (0, 0))],
        out_shape=[jax.ShapeDtypeStruct((2, D, cols), F32), jax.ShapeDtypeStruct((8, D), F32)],
        compiler_params=_cparams(("arbitrary",)), name="ada_bwd")(call, ada_w, dm)


def sum_lead(name, a, scale_last=None):
    n, r, c = a.shape
    tr = r
    for cand in (512, 256, 128, 64, 32, 16, 8):
        if r % cand == 0 and cand * c * 4 * n <= 8 * 1024 * 1024:
            tr = cand
            break

    def kern(a_ref, o_ref):
        acc = a_ref[0].astype(F32)
        for k in range(1, n):
            acc = acc + a_ref[k].astype(F32)
        o_ref[...] = acc

    return pl.pallas_call(kern, grid=(r // tr,), in_specs=[pl.BlockSpec((n, tr, c), lambda i: (0, i, 0))],
                          out_specs=pl.BlockSpec((tr, c), lambda i: (i, 0)),
                          out_shape=jax.ShapeDtypeStruct((r, c), F32),
                          compiler_params=_cparams(("parallel",)), name=name)(a)


def add_pairs(name, hs, got, half):
    _, _, r, c = hs.shape

    def kern(half_ref, a_ref, b_ref, o_ref):
        o_ref[...] = (a_ref[...].astype(F32) + b_ref[...].astype(F32)).astype(BF)

    spec = pl.BlockSpec((None, r, c), lambda j, h: (j, 0, 0))
    grid_spec = pltpu.PrefetchScalarGridSpec(
        num_scalar_prefetch=1, grid=(4,),
        in_specs=[pl.BlockSpec((None, None, r, c), lambda j, h: (h[0], j, 0, 0)), spec], out_specs=spec)
    return pl.pallas_call(kern, grid_spec=grid_spec, out_shape=jax.ShapeDtypeStruct(got.shape, BF),
                          compiler_params=_cparams(("parallel",)), name=name)(half, hs, got)


def sum_slabs(name, land, own, where, full, lead):
    _, r, c = land.shape
    tr = r
    for cand in (512, 256, 128, 64, 32, 16):
        if r % cand == 0 and cand * c * 16 <= 4 * 1024 * 1024:
            tr = cand
            break

    def kern(where_ref, full_ref, land_ref, own_ref, o_ref):
        me = where_ref[0]
        acc = None
        for k in range(4):
            t = jnp.where(me == k, own_ref[k], land_ref[k]).astype(F32)
            acc = t if acc is None else acc + t
        o_ref[...] = acc

    spec = pl.BlockSpec((4, tr, c), lambda i, m: (0, i, 0))
    grid_spec = pltpu.PrefetchScalarGridSpec(
        num_scalar_prefetch=1, grid=(r // tr,), in_specs=[pl.BlockSpec(memory_space=pl.ANY), spec, spec],
        out_specs=pl.BlockSpec((None, None, tr, c), lambda i, m: (lead, m[1], i, 0)))
    return pl.pallas_call(kern, grid_spec=grid_spec, out_shape=jax.ShapeDtypeStruct(full.shape, F32),
                          input_output_aliases={1: 0}, compiler_params=_cparams(("parallel",)),
                          name=name)(where, full, land, own)


def adamw(name, w, g, m, v):
    r, c = w.shape
    tr = r
    for cand in (512, 256, 128, 64, 32, 16, 8):
        if r % cand == 0 and cand * c * 4 <= 2 * 1024 * 1024:
            tr = cand
            break
    c1 = 1.0 / (1.0 - ADAM_B1 ** ADAM_STEP)
    c2 = 1.0 / (1.0 - ADAM_B2 ** ADAM_STEP)

    def kern(w_ref, g_ref, m_ref, v_ref, d_ref, mo_ref, vo_ref):
        gv = g_ref[...]
        mn = ADAM_B1 * m_ref[...] + (1.0 - ADAM_B1) * gv
        vn = ADAM_B2 * v_ref[...] + (1.0 - ADAM_B2) * (gv * gv)
        d_ref[...] = -ADAM_LR * ((mn * c1) / (jnp.sqrt(vn * c2) + ADAM_EPS) + ADAM_WD * w_ref[...])
        mo_ref[...] = mn
        vo_ref[...] = vn

    spec = pl.BlockSpec((tr, c), lambda i: (i, 0))
    shp = jax.ShapeDtypeStruct((r, c), F32)
    return pl.pallas_call(kern, grid=(r // tr,), in_specs=[spec] * 4, out_specs=[spec] * 3, out_shape=[shp] * 3,
                          compiler_params=_cparams(("parallel",)), name=name)(w, g, m, v)


def all_gather8(name, xs, after=None):
    m_per, n = xs.shape
    extra = [] if after is None else [after]

    def body(x_ref, *rest):
        out_ref, send_sems, recv_sems, local_sem = rest[len(extra):]
        x, y, c = lax.axis_index("x"), lax.axis_index("y"), lax.axis_index("c")
        me, sibling = (x, y, c), (x, y, 1 - c)
        chips = [(1 - x, y), (x, 1 - y), (1 - x, 1 - y)]

        def rows(px, py, pc):
            return out_ref.at[pl.ds((4 * px + 2 * py + pc) * m_per, m_per), :]

        def copy(k, block, to, src=None):
            return pltpu.make_async_remote_copy(
                src_ref=rows(*block) if src is None else src, dst_ref=rows(*block),
                send_sem=send_sems.at[k], recv_sem=recv_sems.at[k], device_id=to, device_id_type=MESH)

        mine = pltpu.make_async_copy(x_ref, rows(*me), local_sem)
        mine.start()
        first = [copy(0, me, sibling, src=x_ref)]
        first += [copy(1 + j, me, (*chip, c), src=x_ref) for j, chip in enumerate(chips)]
        for cp in first:
            cp.start()
        passed = [copy(4 + j, (*chip, c), sibling) for j, chip in enumerate(chips)]
        for j, chip in enumerate(chips):
            copy(1 + j, (*chip, c), me).wait_recv()
            passed[j].start()
        copy(0, sibling, me).wait_recv()
        for j, chip in enumerate(chips):
            copy(4 + j, (*chip, 1 - c), me).wait_recv()
        for cp in first + passed:
            cp.wait_send()
        mine.wait()

    return pl.pallas_call(
        body, out_shape=jax.ShapeDtypeStruct((8 * m_per, n), xs.dtype),
        in_specs=[pl.BlockSpec(memory_space=pltpu.VMEM)] * (1 + len(extra)),
        out_specs=pl.BlockSpec(memory_space=pltpu.VMEM),
        scratch_shapes=[pltpu.SemaphoreType.DMA((7,)), pltpu.SemaphoreType.DMA((7,)), pltpu.SemaphoreType.DMA],
        compiler_params=pltpu.CompilerParams(vmem_limit_bytes=VMEM_LIMIT), name=name)(xs, *extra)


def sibling_merge(name, fulls):
    n = len(fulls)
    slots = [(a, l) for a in range(n) for l in range(fulls[a].shape[0])]

    def body(*refs):
        buf = refs[n:2 * n]
        send_sems, recv_sems = refs[2 * n], refs[2 * n + 1]
        c = lax.axis_index("c")
        sibling = (lax.axis_index("x"), lax.axis_index("y"), 1 - c)
        sends, recvs = [], []
        for k, (a, l) in enumerate(slots):
            kw = dict(send_sem=send_sems.at[k], recv_sem=recv_sems.at[k], device_id=sibling, device_id_type=MESH)
            sends.append(pltpu.make_async_remote_copy(src_ref=buf[a].at[l, c], dst_ref=buf[a].at[l, c], **kw))
            recvs.append(pltpu.make_async_remote_copy(src_ref=buf[a].at[l, c], dst_ref=buf[a].at[l, 1 - c], **kw))
        for cp in sends:
            cp.start()
        for cp in recvs:
            cp.wait_recv()
        for cp in sends:
            cp.wait_send()

    anyspec = pl.BlockSpec(memory_space=pl.ANY)
    return pl.pallas_call(
        body, out_shape=[jax.ShapeDtypeStruct(s.shape, s.dtype) for s in fulls],
        in_specs=[anyspec] * n, out_specs=[anyspec] * n, input_output_aliases={a: a for a in range(n)},
        scratch_shapes=[pltpu.SemaphoreType.DMA((len(slots),)), pltpu.SemaphoreType.DMA((len(slots),))],
        name=name)(*fulls)


def place_own(name, land, src, chip):
    c = src.shape[-1]
    r = src.size // c
    tr = r
    for cand in (1024, 512, 256, 128, 64, 32, 16):
        if r % cand == 0 and cand * c * 2 <= 2 * 1024 * 1024:
            tr = cand
            break

    def kern(chip_ref, land_ref, src_ref, out_ref):
        out_ref[...] = src_ref[...]

    grid_spec = pltpu.PrefetchScalarGridSpec(
        num_scalar_prefetch=1, grid=(r // tr,),
        in_specs=[pl.BlockSpec(memory_space=pl.ANY), pl.BlockSpec((tr, c), lambda i, m: (i, 0))],
        out_specs=pl.BlockSpec((None, tr, c), lambda i, m: (m[0], i, 0)))
    out = pl.pallas_call(kern, grid_spec=grid_spec, out_shape=jax.ShapeDtypeStruct((4, r, c), land.dtype),
                         input_output_aliases={1: 0}, compiler_params=_cparams(("parallel",)),
                         name=name)(chip, land.reshape(4, r, c), src.reshape(r, c))
    return out.reshape(land.shape)


def _half_copies(src, land, send_sems, recv_sems):
    c = lax.axis_index("c")
    sibling = (lax.axis_index("x"), lax.axis_index("y"), 1 - c)
    pairs = []
    for a in range(len(src)):
        cp = pltpu.make_async_remote_copy(src_ref=src[a].at[1 - c], dst_ref=land[a], send_sem=send_sems.at[a],
                                          recv_sem=recv_sems.at[a], device_id=sibling, device_id_type=MESH)
        pairs.append((cp, cp))
    return pairs


def _chip_copies(src, land, send_sems, recv_sems, scatter):
    x, y, c = lax.axis_index("x"), lax.axis_index("y"), lax.axis_index("c")
    me = 2 * x + y
    pairs = []
    for a in range(len(src)):
        for j, (px, py) in enumerate([(1 - x, y), (x, 1 - y), (1 - x, 1 - y)]):
            to = 2 * px + py
            out = src[a].at[to] if scatter else src[a]
            kw = dict(send_sem=send_sems.at[3 * a + j], recv_sem=recv_sems.at[3 * a + j], device_id=(px, py, c),
                      device_id_type=MESH)
            pairs.append((pltpu.make_async_remote_copy(src_ref=out, dst_ref=land[a].at[me], **kw),
                          pltpu.make_async_remote_copy(src_ref=out, dst_ref=land[a].at[to], **kw)))
    return pairs


_HBM = pl.BlockSpec(memory_space=pltpu.HBM)
_SEM = pl.BlockSpec(memory_space=pltpu.SEMAPHORE)


GATHER = (functools.partial(_chip_copies, scatter=False), 3)
SCATTER = (functools.partial(_chip_copies, scatter=True), 3)
TO_SIBLING = (_half_copies, 1)


def _landing(shapes, dtype):
    return [lax.empty(tuple(s), dtype) for s in shapes]


def exchange_start(name, groups, plan):
    copies, per = plan
    sizes = [len(s) for s, _ in groups]
    flat = [a for s, l in groups for a in list(s) + list(l)]
    ng = len(groups)

    def body(*refs):
        ins, outs = refs[:len(flat)], refs[len(flat):]
        off = 0
        for g, n in enumerate(sizes):
            src, land = ins[off:off + n], ins[off + n:off + 2 * n]
            off += 2 * n
            for send, _ in copies(src, land, outs[2 * g], outs[2 * g + 1]):
                send.start()
        outs[-1][...] = jnp.zeros_like(outs[-1])

    out_shape = []
    for n in sizes:
        out_shape += [pltpu.SemaphoreType.DMA((per * n,)), pltpu.SemaphoreType.DMA((per * n,))]
    out_shape += [pltpu.HBM(a.shape, a.dtype) for a in flat] + [jax.ShapeDtypeStruct((8, 128), F32)]
    res = pl.pallas_call(
        body, out_shape=tuple(out_shape), in_specs=[_HBM] * len(flat),
        out_specs=tuple([_SEM] * (2 * ng) + [_HBM] * len(flat) + [pl.BlockSpec(memory_space=pltpu.VMEM)]),
        input_output_aliases={k: 2 * ng + k for k in range(len(flat))},
        compiler_params=pltpu.CompilerParams(has_side_effects=pltpu.SideEffectType.DATAFLOW_SIDE_EFFECTING),
        name=name)(*[pltpu.with_memory_space_constraint(a, pltpu.HBM) for a in flat])
    handles, off = [], 2 * ng
    for g, n in enumerate(sizes):
        handles.append((res[2 * g], res[2 * g + 1], list(res[off:off + n]), list(res[off + n:off + 2 * n])))
        off += 2 * n
    return handles, res[-1]


def exchange_wait(name, handle, after, plan):
    send_sems, recv_sems, srcs, lands = handle
    n = len(srcs)

    def body(*refs):
        src, land = refs[:n], refs[n:2 * n]
        for send, recv in plan[0](src, land, refs[2 * n], refs[2 * n + 1]):
            send.wait_send()
            recv.wait_recv()

    res = pl.pallas_call(
        body, out_shape=tuple(pltpu.HBM(a.shape, a.dtype) for a in srcs + lands),
        in_specs=[_HBM] * (2 * n) + [_SEM, _SEM, pl.BlockSpec(memory_space=pl.ANY)],
        out_specs=tuple([_HBM] * (2 * n)), input_output_aliases={k: k for k in range(2 * n)},
        compiler_params=pltpu.CompilerParams(has_side_effects=pltpu.SideEffectType.DATAFLOW_SIDE_EFFECTING),
        name=name)(*srcs, *lands, send_sems, recv_sems, after)
    return list(res[:n]), list(res[n:])


def _rope_tables(d_rot, reps):
    rows = L // GRID_W
    row = np.repeat(np.arange(rows), GRID_W).astype(np.float32)
    col = np.tile(np.arange(GRID_W), rows).astype(np.float32)
    d_axis = d_rot // 2
    inv = (ROPE_THETA ** (-np.arange(0, d_axis, 2, dtype=np.float32) / d_axis)).astype(np.float32)
    ang = np.concatenate([row[:, None] * inv, col[:, None] * inv], axis=-1).astype(np.float32)
    cos, sin = np.cos(ang).astype(np.float32), np.sin(ang).astype(np.float32)
    c = np.repeat(cos, 2, axis=-1)
    s = np.stack([-sin, sin], axis=-1).reshape(L, d_rot)
    c = np.concatenate([np.ones((LC, d_rot), np.float32), c], axis=0)
    s = np.concatenate([np.zeros((LC, d_rot), np.float32), s], axis=0)
    return np.tile(c, (1, reps)), np.tile(s, (1, reps))


def _group_consts():
    g = np.arange(512) // 64
    avg = (g[:, None] == g[None, :]).astype(np.float32) / 64.0
    masks = (np.arange(8)[:, None] == g[None, :]).astype(np.float32).reshape(8, 1, 512)
    return jnp.asarray(avg, BF), jnp.asarray(masks)


def _pack(items):
    flat = jnp.concatenate([a.reshape(-1).astype(F32) for a in items])
    n = flat.shape[0]
    rows = -(-n // D)
    rows = -(-rows // 8) * 8
    return jnp.pad(flat, (0, rows * D - n)).reshape(rows, D)


def _unpack(buf, shapes):
    lead = buf.shape[:-2]
    flat = buf.reshape(lead + (-1,))
    out, off = [], 0
    for shp in shapes:
        n = int(np.prod(shp))
        out.append(flat[..., off:off + n].reshape(lead + tuple(shp)))
        off += n
    return out


def _arrive(prm, key, after):
    if callable(prm[key]):
        prm[key](after)
    return prm[key]


def _layer_fwd(i, x, h, mods, prm, consts, nxt):
    sv = {}
    sv["x0"] = x
    sv["h"] = h
    p = proj_in(f"proj_in_{i}", h, _arrive(prm, "w_in", h))
    sv["p"] = p
    if i == 0:
        q, kv, m2 = even_tok_fwd(p, consts["cos_e"], consts["sin_e"], prm["gq"], prm["gk"], prm["gs"],
                                 prm["sgu_w"], prm["sgu_b"], consts["avg"], consts["masks"])
        o, lse = attn_fwd("attn_fwd_0", q, kv, GQA_HEADS)
        sv.update(q=q, kv=kv)
    else:
        q, kv, y = odd_tok_fwd(p, consts["cos_o"], consts["sin_o"], prm["gq"], prm["gkv"], prm["wq"], prm["wkk"],
                               prm["wkv"], consts["spread"])
        o, lse = attn_fwd("attn_fwd_1", q, kv, MLA_HEADS)
        z, m2 = conf_fwd(y, prm["conv_w"], prm["conv_b"], prm["ln_g"], prm["ln_b"])
        sv.update(q=q, kv=kv, y=y, z=z)
    sv.update(o=o, lse=lse, m2=m2)
    x1, y1, h2 = proj_out(f"proj_out_{i}", o, m2, _arrive(prm, "w_out", o), x, mods, 2, prm["norm2_g"], 3, 4)
    sv.update(x1=x1, y1=y1)
    a, f = mlp_up(f"mlp_up_{i}", h2, _arrive(prm, "w1", h2))
    x2, y2, *h_next = mlp_down(f"mlp_down_{i}", f, prm["w2"], x1, mods, 5, nxt)
    sv.update(h2=h2, a=a, f=f, y2=y2)
    return x2, (h_next[0] if h_next else None), sv


def _layer_bwd(i, dx, dy2, dg2, sv, mods, prm, consts, hook, entry, below):
    gr = {}
    da = mlp_bwd_da(f"mlp_bwd_da_{i}", dy2, prm["w2"], sv["a"], after=entry)
    tiles8 = [(h, j) for h in range(2) for j in range(4)]
    gr["w1"] = mm_tn(f"grad_w1_{i}", sv["h2"], da, tiles8, 512, D).reshape(2, 4, 512, D)
    gr["w2"] = mm_tn(f"grad_w2_{i}", sv["f"], dy2, [(2 * j + h, 0) for h in range(2) for j in range(4)],
                     512, D).reshape(2, 4, 512, D)
    dh2 = mlp_bwd_dh(f"mlp_bwd_dh_{i}", da, prm["w1"])
    dx1, dy1, dg1, dsh2, dsc2, gr["norm2_g"] = modnorm_bwd(
        f"norm2_bwd_{i}", sv["x1"], dh2, dx, mods, prm["norm2_g"], 3, 4, gate=(sv["y1"], mods, 2),
        after=hook(f"{i}:mlp", gr, dh2))
    dcat = mm_nt(f"proj_out_bwd_{i}", dy1, prm["w_out"], after=hook(f"{i}:mid", gr, dy1))
    t4 = [(2 * j + h, 0) for h in range(2) for j in range(2)]
    go = mm_tn(f"grad_wout_a_{i}", sv["o"], dy1, t4, 128, D).reshape(2, 2, 128, D)
    gm = mm_tn(f"grad_wout_b_{i}", sv["m2"], dy1, t4, 128, D).reshape(2, 2, 128, D)
    gr["w_out"] = jnp.concatenate([go, gm], axis=1)
    if i == 0:
        dq, dkv = attn_bwd("attn_bwd_0", sv["q"], sv["kv"], sv["o"], dcat, sv["lse"], GQA_HEADS)
        dp, gr["gq"], gr["gk"], gr["gs"], gr["sgu_w"], gr["sgu_b"] = even_tok_bwd(
            sv["p"], dq, dkv, dcat, consts["cos_e"], consts["sin_e"], prm["gq"], prm["gk"],
            prm["gs"], prm["sgu_w"], prm["sgu_b"], consts["avg"], consts["masks"])
    else:
        dq, dkv = attn_bwd("attn_bwd_1", sv["q"], sv["kv"], sv["o"], dcat, sv["lse"], MLA_HEADS)
        dz, gr["ln_g"], gr["ln_b"], gr["conv_b"] = conf_bwd_ln(sv["z"], dcat, prm["ln_g"], prm["ln_b"])
        dyc, gr["conv_w"] = conf_bwd_conv(sv["y"], dz, prm["conv_w"])
        dp, gr["gq"], gr["gkv"], gr["wq"], gr["wkk"], gr["wkv"] = odd_tok_bwd(
            sv["p"], dq, dkv, dyc, consts["cos_o"], consts["sin_o"], prm["gq"], prm["gkv"], prm["wq"], prm["wkk"],
            prm["wkv"], consts["spread"])
    n_in = prm["w_in"].shape[1]
    gr["w_in"] = mm_tn(f"grad_win_{i}", sv["h"], dp, [(0, 0), (1, 0)], 512, n_in)
    dh = mm_nt(f"proj_in_bwd_{i}", dp, prm["w_in"])
    if below:
        dx0, dy2b, dg2b, dsh1, dsc1, gr["norm1_g"] = modnorm_bwd(
            f"norm1_bwd_{i}", sv["x0"], dh, dx1, mods, prm["norm1_g"], 0, 1, gate=(below[0], below[1], 5))
        down = (dy2b, dg2b)
    else:
        dx0, dsh1, dsc1, gr["norm1_g"] = modnorm_bwd(f"norm1_bwd_{i}", sv["x0"], dh, dx1, mods, prm["norm1_g"], 0, 1,
                                                     lat_only=True)
        down = None
    dmods = jnp.concatenate([dsh1, dsc1, dg1, dsh2, dsc2, dg2], axis=1)
    return dx0, down, dmods, gr, hook(f"{i}:end", gr, dx0)


def local_step(xcat, target, mods, prms, final_g, hook=lambda point, grads, fresh: None):
    avg, masks = _group_consts()
    cos_e, sin_e = _rope_tables(64, 8)
    ck, sk = _rope_tables(32, 1)
    one64, zero64 = np.ones((SEQ, 64), np.float32), np.zeros((SEQ, 64), np.float32)
    one96, zero96 = np.ones((SEQ, 96), np.float32), np.zeros((SEQ, 96), np.float32)
    cos_o = np.concatenate([np.tile(np.concatenate([one64, ck], axis=1), (1, 8)), ck, one96], axis=1)
    sin_o = np.concatenate([np.tile(np.concatenate([zero64, sk], axis=1), (1, 8)), sk, zero96], axis=1)
    lane = np.arange(768)
    spread = np.zeros((128, 768), np.float32)
    spread[lane % 96 - 64, lane] = (lane % 96 >= 64)
    consts = dict(avg=avg, masks=masks, cos_e=jnp.asarray(cos_e), sin_e=jnp.asarray(sin_e),
                  cos_o=jnp.asarray(cos_o), sin_o=jnp.asarray(sin_o), spread=jnp.asarray(spread, BF))
    x = xcat
    h = modnorm_fwd("norm1_fwd_0", x, mods[0], prms[0]["norm1_g"], 0, 1)
    saved = []
    for i in range(2):
        x, h, sv = _layer_fwd(i, x, h, mods[i], prms[i], consts, (mods[1], prms[1]["norm1_g"]) if i == 0 else None)
        saved.append(sv)
    dx, dy2, dg2, loss, dfg = final_loss(x, target, final_g, saved[1]["y2"], mods[1], 5)
    dmods, grads = [None, None], [None, None]
    entry, down = None, (dy2, dg2)
    for i in (1, 0):
        below = (saved[0]["y2"], mods[0]) if i == 1 else None
        dx, down, dmods[i], grads[i], entry = _layer_bwd(i, dx, down[0], down[1], saved[i], mods[i], prms[i], consts,
                                                         hook, entry, below)
    return loss, dx, dmods, grads, dfg, entry


def _row(v):
    return v.reshape(1, -1).astype(F32)


def odd_in_params(od_w_in, w_uq, w_ukv):
    od = jnp.concatenate([od_w_in[:, 0:416], jnp.zeros((D, 96), od_w_in.dtype), od_w_in[:, 416:OD_IN]], axis=1)
    ukv = w_ukv.reshape(128, 8, 128)
    wkk = jnp.pad(ukv[:, :, :64], ((0, 0), (0, 0), (0, 32))).reshape(128, 768)
    return dict(w_in=od, wq=w_uq, wkk=wkk, wkv=ukv[:, :, 64:].reshape(128, 512))


def small_params(small):
    p0 = dict(norm1_g=_row(small["norm1_g"][0]), norm2_g=_row(small["norm2_g"][0]),
              gq=jnp.tile(_row(small["ev_q_norm_g"]), (1, 8)), gk=jnp.tile(_row(small["ev_k_norm_g"]), (1, 2)),
              gs=_row(small["ev_sgu_norm_g"]), sgu_w=small["ev_sgu_w"].reshape(8, 128, 128).astype(F32),
              sgu_b=small["ev_sgu_b"].reshape(8, 128, 1).astype(F32))
    p1 = dict(norm1_g=_row(small["norm1_g"][1]), norm2_g=_row(small["norm2_g"][1]),
              gq=_row(small["od_q_norm_g"]), gkv=_row(small["od_kv_norm_g"]),
              conv_w=jnp.pad(small["od_conv_w"].reshape(CONV_K, 512).astype(F32), ((0, 1), (0, 0))),
              conv_b=_row(small["od_conv_b"]), ln_g=_row(small["od_ln_g"]), ln_b=_row(small["od_ln_b"]))
    return [p0, p1]


def prep_params(ev_w_in, od_w_in, w_out, w1, w2, w_uq, w_ukv, small):
    p0, p1 = small_params(small)
    p0.update(w_in=ev_w_in, w_out=w_out[0], w1=w1[0], w2=w2[0])
    p1.update(odd_in_params(od_w_in, w_uq, w_ukv), w_out=w_out[1], w1=w1[1], w2=w2[1])
    return [p0, p1]


def small_grads_natural(grads, dfg):
    g0, g1 = grads
    return dict(
        norm1_g=jnp.concatenate([g0["norm1_g"], g1["norm1_g"]], axis=0),
        norm2_g=jnp.concatenate([g0["norm2_g"], g1["norm2_g"]], axis=0),
        ev_q_norm_g=g0["gq"].reshape(8, 64).sum(0).reshape(1, 64),
        ev_k_norm_g=g0["gk"].reshape(2, 64).sum(0).reshape(1, 64),
        ev_sgu_norm_g=g0["gs"].reshape(1, 8, 64),
        ev_sgu_w=g0["sgu_w"].reshape(1, 8, 128, 128),
        ev_sgu_b=g0["sgu_b"].reshape(1, 8, 128),
        od_q_norm_g=g1["gq"].reshape(1, 256),
        od_kv_norm_g=g1["gkv"].reshape(1, 128),
        od_conv_w=g1["conv_w"][0:CONV_K].reshape(1, CONV_K, 512),
        od_conv_b=g1["conv_b"].reshape(1, 512),
        od_ln_g=g1["ln_g"].reshape(1, 512),
        od_ln_b=g1["ln_b"].reshape(1, 512),
        final_g=dfg.reshape(D))


def layer_grads_hs(i, g, part="all"):
    def cols(a):
        k, n = a.shape
        return a.reshape(2, k // 2, 4, n // 4).transpose(0, 2, 1, 3).astype(BF)

    mlp = [(("mlp_w1", i), g["w1"]), (("mlp_w2", i), g["w2"])]
    if part == "mlp":
        return mlp
    rest = [(("w_out", i), g["w_out"])]
    if i == 0:
        rest.append((("ev_w_in", 0), cols(g["w_in"].reshape(D, EV_IN))))
    else:
        od = g["w_in"].reshape(D, OD_PAD)
        od = jnp.concatenate([od[:, 0:416], od[:, 512:OD_PAD]], axis=1)
        ukv = jnp.concatenate([g["wkk"].reshape(128, 8, 96)[:, :, :64], g["wkv"].reshape(128, 8, 64)], axis=2)
        rest += [(("od_w_in", 0), cols(od)), (("od_w_uq", 0), cols(g["wq"])),
                 (("od_w_ukv", 0), cols(ukv.reshape(128, 1024)))]
    return rest if part == "rest" else mlp + rest


def big_grads_hs(grads):
    d = dict(layer_grads_hs(0, grads[0]) + layer_grads_hs(1, grads[1]))
    return dict(ev_w_in=d[("ev_w_in", 0)], od_w_in=d[("od_w_in", 0)], od_w_uq=d[("od_w_uq", 0)],
                od_w_ukv=d[("od_w_ukv", 0)], w_out=[d[("w_out", 0)], d[("w_out", 1)]],
                mlp_w1=[d[("mlp_w1", 0)], d[("mlp_w1", 1)]], mlp_w2=[d[("mlp_w2", 0)], d[("mlp_w2", 1)]])


def grads_to_natural(grads, dfg):
    out = small_grads_natural(grads, dfg)
    hs = big_grads_hs(grads)

    def from_cols(a):
        return a.transpose(0, 2, 1, 3).reshape(2 * a.shape[2], 4 * a.shape[3])

    def from_rows(a):
        return a.transpose(1, 0, 2, 3).reshape(8 * a.shape[2], a.shape[3])

    out["ev_w_in"] = from_cols(hs["ev_w_in"])[None]
    out["od_w_in"] = from_cols(hs["od_w_in"])[None]
    out["od_w_uq"] = from_cols(hs["od_w_uq"])[None]
    out["od_w_ukv"] = from_cols(hs["od_w_ukv"])[None]
    out["w_out"] = jnp.stack([from_rows(a) for a in hs["w_out"]])
    out["mlp_w1"] = jnp.stack([from_cols(a) for a in hs["mlp_w1"]])
    out["mlp_w2"] = jnp.stack([from_rows(a) for a in hs["mlp_w2"]])
    return out


WEIGHT_NAMES = ['c_ctx', 'ada_w', 'ada_b', 'norm1_g', 'norm2_g', 'w_out', 'mlp_w1', 'mlp_w2', 'ev_w_in',
                'ev_q_norm_g', 'ev_k_norm_g', 'ev_sgu_norm_g', 'ev_sgu_w', 'ev_sgu_b', 'od_w_in', 'od_q_norm_g',
                'od_kv_norm_g', 'od_w_uq', 'od_w_ukv', 'od_conv_w', 'od_conv_b', 'od_ln_g', 'od_ln_b', 'final_g']
REPL_SMALL = ['norm1_g', 'norm2_g', 'ev_q_norm_g', 'ev_k_norm_g', 'ev_sgu_norm_g', 'ev_sgu_w', 'ev_sgu_b',
              'od_kv_norm_g', 'final_g']
SHARD_SMALL = ['od_q_norm_g', 'od_conv_w', 'od_conv_b', 'od_ln_g', 'od_ln_b']
BIG = ['w_out', 'mlp_w1', 'mlp_w2', 'ev_w_in', 'od_w_in', 'od_w_uq', 'od_w_ukv']


def _gather_last(parts):
    return jnp.concatenate([parts[k] for k in range(4)], axis=-1)


class _Reduce:
    def __init__(self, tag, named, half, where):
        self.tag, self.half, self.where = tag, half, where
        self.names, self.hs = zip(*named)
        self.hs = list(self.hs)

    def to_sibling(self):
        lands = [lax.empty(a.shape[1:], BF) for a in self.hs]
        (self.h1,), token = exchange_start(f"rs_sibling_start_{self.tag}", [(self.hs, lands)], TO_SIBLING)
        return token

    def to_chips(self, after):
        hs, got = exchange_wait(f"rs_sibling_wait_{self.tag}", self.h1, after, TO_SIBLING)
        pair = [add_pairs(f"rs_add_{self.tag}_{k}", a, b, self.half) for k, (a, b) in enumerate(zip(hs, got))]
        lands = [lax.empty(p.shape, BF) for p in pair]
        (self.h2,), token = exchange_start(f"rs_chips_start_{self.tag}", [(pair, lands)], SCATTER)
        return token

    def finish(self, after, bufs):
        pair, land = exchange_wait(f"rs_chips_wait_{self.tag}", self.h2, after, SCATTER)
        for k, ((n, idx), l, p) in enumerate(zip(self.names, land, pair)):
            bufs[n] = sum_slabs(f"rs_sum_{self.tag}_{k}", l, p, self.where, bufs[n], idx)


def kernel(x, c, ctx, c_ctx, ada_w, ada_b, norm1_g, norm2_g, w_out, mlp_w1, mlp_w2, ev_w_in, ev_q_norm_g, ev_k_norm_g, ev_sgu_norm_g, ev_sgu_w, ev_sgu_b, od_w_in, od_q_norm_g, od_kv_norm_g, od_w_uq, od_w_ukv, od_conv_w, od_conv_b, od_ln_g, od_ln_b, final_g, loss_target, m_c_ctx, m_ada_w, m_ada_b, m_norm1_g, m_norm2_g, m_w_out, m_mlp_w1, m_mlp_w2, m_ev_w_in, m_ev_q_norm_g, m_ev_k_norm_g, m_ev_sgu_norm_g, m_ev_sgu_w, m_ev_sgu_b, m_od_w_in, m_od_q_norm_g, m_od_kv_norm_g, m_od_w_uq, m_od_w_ukv, m_od_conv_w, m_od_conv_b, m_od_ln_g, m_od_ln_b, m_final_g, v_c_ctx, v_ada_w, v_ada_b, v_norm1_g, v_norm2_g, v_w_out, v_mlp_w1, v_mlp_w2, v_ev_w_in, v_ev_q_norm_g, v_ev_k_norm_g, v_ev_sgu_norm_g, v_ev_sgu_w, v_ev_sgu_b, v_od_w_in, v_od_q_norm_g, v_od_kv_norm_g, v_od_w_uq, v_od_w_ukv, v_od_conv_w, v_od_conv_b, v_od_ln_g, v_od_ln_b, v_final_g):
    w = dict(c_ctx=c_ctx, ada_w=ada_w, ada_b=ada_b, norm1_g=norm1_g, norm2_g=norm2_g, w_out=w_out, mlp_w1=mlp_w1,
             mlp_w2=mlp_w2, ev_w_in=ev_w_in, ev_q_norm_g=ev_q_norm_g, ev_k_norm_g=ev_k_norm_g,
             ev_sgu_norm_g=ev_sgu_norm_g, ev_sgu_w=ev_sgu_w, ev_sgu_b=ev_sgu_b, od_w_in=od_w_in,
             od_q_norm_g=od_q_norm_g, od_kv_norm_g=od_kv_norm_g, od_w_uq=od_w_uq, od_w_ukv=od_w_ukv,
             od_conv_w=od_conv_w, od_conv_b=od_conv_b, od_ln_g=od_ln_g, od_ln_b=od_ln_b, final_g=final_g)
    mom = dict(c_ctx=m_c_ctx, ada_w=m_ada_w, ada_b=m_ada_b, norm1_g=m_norm1_g, norm2_g=m_norm2_g, w_out=m_w_out,
               mlp_w1=m_mlp_w1, mlp_w2=m_mlp_w2, ev_w_in=m_ev_w_in, ev_q_norm_g=m_ev_q_norm_g,
               ev_k_norm_g=m_ev_k_norm_g, ev_sgu_norm_g=m_ev_sgu_norm_g, ev_sgu_w=m_ev_sgu_w, ev_sgu_b=m_ev_sgu_b,
               od_w_in=m_od_w_in, od_q_norm_g=m_od_q_norm_g, od_kv_norm_g=m_od_kv_norm_g, od_w_uq=m_od_w_uq,
               od_w_ukv=m_od_w_ukv, od_conv_w=m_od_conv_w, od_conv_b=m_od_conv_b, od_ln_g=m_od_ln_g,
               od_ln_b=m_od_ln_b, final_g=m_final_g)
    var = dict(c_ctx=v_c_ctx, ada_w=v_ada_w, ada_b=v_ada_b, norm1_g=v_norm1_g, norm2_g=v_norm2_g, w_out=v_w_out,
               mlp_w1=v_mlp_w1, mlp_w2=v_mlp_w2, ev_w_in=v_ev_w_in, ev_q_norm_g=v_ev_q_norm_g,
               ev_k_norm_g=v_ev_k_norm_g, ev_sgu_norm_g=v_ev_sgu_norm_g, ev_sgu_w=v_ev_sgu_w, ev_sgu_b=v_ev_sgu_b,
               od_w_in=v_od_w_in, od_q_norm_g=v_od_q_norm_g, od_kv_norm_g=v_od_kv_norm_g, od_w_uq=v_od_w_uq,
               od_w_ukv=v_od_w_ukv, od_conv_w=v_od_conv_w, od_conv_b=v_od_conv_b, od_ln_g=v_od_ln_g,
               od_ln_b=v_od_ln_b, final_g=v_final_g)
    xi, yi, ci = lax.axis_index("x"), lax.axis_index("y"), lax.axis_index("c")
    chip = 2 * xi + yi
    dev = 2 * chip + ci

    shard_shapes = [w[n].shape for n in SHARD_SMALL]
    g0 = all_gather8("ag_small", _pack([c] + [w[n] for n in SHARD_SMALL]))
    g0 = g0.reshape(8, -1, D)
    parts = _unpack(g0, [c.shape] + shard_shapes)
    c_all = parts[0].reshape(16, D)
    small_full = {n: _gather_last(p[0::2]) for n, p in zip(SHARD_SMALL, parts[1:])}
    call = jnp.concatenate([c_all, c_ctx.reshape(1, D), jnp.zeros((NC - 17, D), F32)], axis=0)

    cols = ada_w.shape[2]
    ada_b_sh = lax.dynamic_slice(ada_b, (0, chip * cols), (2, cols)).reshape(2, 1, cols)
    mt = mods_fwd(call, ada_w, ada_b_sh)
    mt = all_gather8("ag_mods", mt.reshape(2 * NC, cols)).reshape(8, 2, NC, cols)
    table = mt[0::2].transpose(1, 2, 0, 3).reshape(2, NC, 4 * cols)
    mods = []
    for i in range(2):
        lat = lax.dynamic_slice(table[i], (2 * dev, 0), (2, 4 * cols))
        mc = table[i, 16]
        mods.append(jnp.stack([mc, lat[0], mc, lat[1]]).reshape(4 * N_MOD, 1, D))

    order = [[("ev_w_in", 0)], [("w_out", 0), ("mlp_w1", 0), ("mlp_w2", 0)],
             [("od_w_in", 0), ("od_w_uq", 0), ("od_w_ukv", 0), ("w_out", 1)], [("mlp_w1", 1), ("mlp_w2", 1)]]
    groups = []
    for names in order:
        srcs = [w[n][i].astype(BF) for n, i in names]
        groups.append((srcs, [lax.empty((4,) + s.shape, BF) for s in srcs]))
    groups[0][0][0], table = lax.optimization_barrier((groups[0][0][0], table))
    handles, token = exchange_start("gather_start", groups, GATHER)
    mods[0] = mods[0] + token[0, 0]
    small = {n: w[n] for n in REPL_SMALL}
    small.update(small_full)
    prms = small_params(small)

    chip1 = chip.reshape(1).astype(jnp.int32)

    def arrived(k, after):
        srcs, lands = exchange_wait(f"gather_wait_{k}", handles[k], after, GATHER)
        return [place_own(f"gather_own_{k}_{a}", l, s, chip1) for a, (l, s) in enumerate(zip(lands, srcs))]

    def arrive_ev_in(after):
        (ev,) = arrived(0, after)
        prms[0]["w_in"] = _gather_last(ev)

    def arrive_ev_rest(after):
        wo, w1, w2 = arrived(1, after)
        prms[0].update(w_out=wo.reshape(D, D), w1=w1, w2=w2)

    def arrive_od(after):
        od, uq, ukv, wo = arrived(2, after)
        prms[1].update(odd_in_params(_gather_last(od), _gather_last(uq), _gather_last(ukv)), w_out=wo.reshape(D, D))

    def arrive_od_mlp(after):
        w1, w2 = arrived(3, after)
        prms[1].update(w1=w1, w2=w2)

    prms[0]["w_in"] = arrive_ev_in
    prms[0]["w_out"] = arrive_ev_rest
    prms[1]["w_in"] = arrive_od
    prms[1]["w1"] = arrive_od_mlp

    half = ci.reshape(1).astype(jnp.int32)
    where = jnp.stack([chip, ci]).astype(jnp.int32)
    red = {}

    def hook(point, g, fresh):
        if point == "1:end":
            red["l1"] = _Reduce("l1", layer_grads_hs(1, g, "all"), half, where)
            return red["l1"].to_sibling()
        if point == "0:mlp":
            red["l0_mlp"] = _Reduce("l0_mlp", layer_grads_hs(0, g, "mlp"), half, where)
            return red["l1"].to_chips(fresh) + red["l0_mlp"].to_sibling()
        if point == "0:mid":
            return red["l0_mlp"].to_chips(fresh)
        if point == "0:end":
            red["l0_rest"] = _Reduce("l0_rest", layer_grads_hs(0, g, "rest"), half, where)
            return red["l0_rest"].to_sibling()
        return None

    xcat = jnp.concatenate([ctx, x], axis=1).reshape(R, D)
    loss_p, dx, dmods, grads, dfg, last = local_step(xcat, loss_target.reshape(NEX * L, D), mods, prms,
                                                     final_g.reshape(1, D), hook)
    grad_x = dx.reshape(NEX, L, D)

    sg = small_grads_natural(grads, dfg)
    dm = jnp.stack([d.reshape(4, N_MOD * D) for d in dmods])
    small_names = REPL_SMALL + SHARD_SMALL
    items = [dm[:, 1::2], dm[:, 0] + dm[:, 2]] + [sg[n] for n in small_names] + [loss_p[0:1, 0:1]]
    shapes = [a.shape for a in items]
    g1 = all_gather8("ag_grads", _pack(items), after=last)
    red["l0_rest"].to_chips(g1)
    rows1 = g1.shape[0] // 8
    g1 = g1.reshape(8, rows1, D)
    tot = _unpack(sum_lead("sum_small", g1), shapes)
    dm_lat = _unpack(g1, shapes[:1])[0]
    dm_lat = dm_lat.transpose(1, 0, 2, 3).reshape(2, 16, N_MOD * D)
    dm_all = jnp.concatenate([dm_lat, tot[1][:, None], jnp.zeros((2, NC - 17, N_MOD * D), F32)], axis=1)
    gsum = dict(zip(small_names, tot[2:2 + len(small_names)]))
    loss = tot[-1].reshape(())
    grad = {n: gsum[n].reshape(w[n].shape) for n in REPL_SMALL}
    for n in SHARD_SMALL:
        k = w[n].shape[-1]
        grad[n] = lax.dynamic_slice_in_dim(gsum[n], chip * k, k, axis=gsum[n].ndim - 1)
    grad["ada_b"] = sum_lead("sum_ada_b", dm_all.transpose(1, 0, 2).reshape(NC, 2 * N_MOD, D)).reshape(2, N_MOD * D)

    dm_sh = lax.dynamic_slice(dm_all, (0, 0, chip * cols), (2, NC, cols))
    grad["ada_w"], dcc = ada_bwd(call, ada_w, dm_sh)
    dcc = all_gather8("ag_cctx", dcc).reshape(8, 8, D)
    grad["c_ctx"] = sum_lead("sum_cctx", dcc[0::2])[0]

    delta, new_m, new_v = {}, {}, {}

    def adam_big(n):
        shp = w[n].shape
        two_d = (shp[0] * shp[1], shp[2])
        d_, m_, v_ = adamw(f"adamw_{n}", w[n].reshape(two_d), grad[n].reshape(two_d), mom[n].reshape(two_d),
                           var[n].reshape(two_d))
        delta[n], new_m[n], new_v[n] = d_.reshape(shp), m_.reshape(shp), v_.reshape(shp)

    adam_big('ada_w')
    rest = [n for n in WEIGHT_NAMES if n not in ['ada_w'] + BIG]
    rshapes = [w[n].shape for n in rest]
    d_, m_, v_ = adamw("adamw_small", _pack([w[n] for n in rest]), _pack([grad[n] for n in rest]),
                       _pack([mom[n] for n in rest]), _pack([var[n] for n in rest]))
    for dst, buf in ((delta, d_), (new_m, m_), (new_v, v_)):
        dst.update(zip(rest, _unpack(buf, rshapes)))

    bufs = {n: lax.empty((w[n].shape[0], 2, w[n].shape[1] // 2, w[n].shape[2]), F32) for n in BIG}
    for tag, behind in (("l1", dcc), ("l0_mlp", dcc), ("l0_rest", d_)):
        red[tag].finish(behind, bufs)
    for n, full in zip(BIG, sibling_merge("rs_sibling_merge", [bufs[n] for n in BIG])):
        grad[n] = full.reshape(w[n].shape)
    for n in BIG:
        adam_big(n)

    return (loss, grad_x, *[grad[n] for n in WEIGHT_NAMES], *[delta[n] for n in WEIGHT_NAMES],
            *[new_m[n] for n in WEIGHT_NAMES], *[new_v[n] for n in WEIGHT_NAMES])
```

```python
import functools
import math

import numpy as np
import jax
import jax.numpy as jnp
from jax import lax
from jax.experimental import pallas as pl
from jax.experimental.pallas import tpu as pltpu

F32 = jnp.float32
BF = jnp.bfloat16
HI = lax.Precision.HIGHEST
MESH = pl.DeviceIdType.MESH

D = 1024
L = 2048
LC = 256
SEQ = L + LC
NEX = 2
R = NEX * SEQ
TB = 256
BPE = SEQ // TB
NBLK = R // TB
GRID_W = 64
FF = 4 * D
EPS = 1e-6
ROPE_THETA = 10000.0
N_MOD = 6
EV_IN = 1792
OD_IN = 1440
OD_PAD = 1536
VMEM_LIMIT = 60 * 1024 * 1024

ADAM_LR = 0.001
ADAM_B1 = 0.9
ADAM_B2 = 0.999
ADAM_EPS = 1e-08
ADAM_WD = 0.01
ADAM_STEP = 10

NT = (((1,), (1,)), ((), ()))
TN = (((0,), (0,)), ((), ()))


def _cparams(sem=None):
    return pltpu.CompilerParams(dimension_semantics=sem, vmem_limit_bytes=VMEM_LIMIT)


@jax.custom_vjp
def _mm(a, b):
    return jnp.dot(a.astype(BF), b.astype(BF), preferred_element_type=F32)


def _mm_fwd(a, b):
    return _mm(a, b), (a, b)


def _mm_bwd(res, g):
    a, b = res
    gb = g.astype(BF)
    da = lax.dot_general(gb, b.astype(BF), NT, preferred_element_type=F32)
    db = lax.dot_general(a.astype(BF), gb, TN, preferred_element_type=F32)
    return da, db


_mm.defvjp(_mm_fwd, _mm_bwd)


@jax.custom_vjp
def _swap(x):
    n = x.shape[-1]
    ax = x.ndim - 1
    lane = lax.broadcasted_iota(jnp.int32, x.shape, ax)
    return jnp.where(lane % 2 == 0, pltpu.roll(x, n - 1, ax), pltpu.roll(x, 1, ax))


_swap.defvjp(lambda x: (_swap(x), None), lambda _, g: (_swap(g),))


def _rope(x, cos, sin):
    return x * cos + _swap(x) * sin


def _rmsn(x, g):
    return x * lax.rsqrt(jnp.mean(x * x, axis=-1, keepdims=True) + EPS) * g


def _split_dot(a, m):
    hi = a.astype(BF)
    lo = (a - hi.astype(F32)).astype(BF)
    return jnp.dot(hi, m, preferred_element_type=F32) + jnp.dot(lo, m, preferred_element_type=F32)


@jax.custom_vjp
def _group_mean(a, avg):
    return _split_dot(a, avg)


_group_mean.defvjp(lambda a, avg: (_split_dot(a, avg), avg),
                   lambda avg, g: (_split_dot(g, avg), jnp.zeros_like(avg)))


def _grmsn(x, g, avg):
    return x * lax.rsqrt(_group_mean(x * x, avg) + EPS) * g


def _modnorm(x, g, sh, sc):
    return _rmsn(x, g) * (1.0 + sc) + sh


def _gelu(x):
    return 0.5 * x * (1.0 + jnp.tanh(0.7978845608028654 * (x + 0.044715 * (x * x * x))))


def _silu(x):
    return x * jax.nn.sigmoid(x)


def _acc(ref, val, first):
    @pl.when(first)
    def _():
        ref[...] = val

    @pl.when(jnp.logical_not(first))
    def _():
        ref[...] += val


def _seg(i):
    return 2 * (i // BPE) + jnp.minimum(i % BPE, 1)


def _seg_first(i):
    return (i % BPE) <= 1


def _rb_call(name, body, row_in=(), mod_in=(), pos_in=(), full_in=(), shift_in=(),
             row_out=(), seg_out=(), acc_out=(), scratch=(), after=None, col_in=()):
    in_specs, args = [], []
    for a in row_in:
        in_specs.append(pl.BlockSpec((TB, a.shape[1]), lambda i: (i, 0)))
        args.append(a)
    for a in col_in:
        in_specs.append(pl.BlockSpec((a.shape[0], TB), lambda i: (0, i)))
        args.append(a)
    for tab, m in mod_in:
        in_specs.append(pl.BlockSpec((1, 1, D), lambda i, m=m: (_seg(i) * N_MOD + m, 0, 0)))
        args.append(tab)
    for a in pos_in:
        in_specs.append(pl.BlockSpec((TB, a.shape[1]), lambda i: (i % BPE, 0)))
        args.append(a)
    for a in full_in:
        in_specs.append(pl.BlockSpec(a.shape, lambda i, n=a.ndim: (0,) * n))
        args.append(a)
    for a, d in shift_in:
        in_specs.append(pl.BlockSpec((TB, a.shape[1]), lambda i, d=d: (jnp.clip(i + d, 0, NBLK - 1), 0)))
        args.append(a)
    n_in = len(args)
    if after is not None:
        in_specs.append(pl.BlockSpec(after.shape, lambda i, n=after.ndim: (0,) * n))
        args.append(after)
    out_specs, out_shape = [], []
    for w, dt, *lat in row_out:
        if lat:
            out_specs.append(pl.BlockSpec(
                (TB, w), lambda i: ((i // BPE) * (L // TB) + jnp.maximum(i % BPE - 1, 0), 0)))
            out_shape.append(jax.ShapeDtypeStruct((NEX * L, w), dt))
        else:
            out_specs.append(pl.BlockSpec((TB, w), lambda i: (i, 0)))
            out_shape.append(jax.ShapeDtypeStruct((R, w), dt))
    for w in seg_out:
        out_specs.append(pl.BlockSpec((1, 1, w), lambda i: (_seg(i), 0, 0)))
        out_shape.append(jax.ShapeDtypeStruct((4, 1, w), F32))
    for shp in acc_out:
        out_specs.append(pl.BlockSpec(shp, lambda i, n=len(shp): (0,) * n))
        out_shape.append(jax.ShapeDtypeStruct(shp, F32))

    def kern(*refs):
        body(pl.program_id(0), *refs[:n_in], *refs[len(args):])

    sem = ("arbitrary",) if (seg_out or acc_out or any(len(r) > 2 for r in row_out)) else ("parallel",)
    return pl.pallas_call(kern, grid=(NBLK,), in_specs=in_specs, out_specs=out_specs, out_shape=out_shape,
                          scratch_shapes=list(scratch), compiler_params=_cparams(sem), name=name)(*args)


def modnorm_fwd(name, x, mods, g, m_sh, m_sc):
    def body(i, x_ref, sh_ref, sc_ref, g_ref, h_ref):
        h_ref[...] = _modnorm(x_ref[...], g_ref[...], sh_ref[0], sc_ref[0]).astype(BF)

    return _rb_call(name, body, row_in=(x,), mod_in=((mods, m_sh), (mods, m_sc)), full_in=(g,),
                    row_out=((D, BF),))[0]


def _gate_grads(dx, y_ref, gt_ref, dy_ref, dgt_ref, i):
    dy_ref[...] = (dx * gt_ref[0]).astype(BF)
    _acc(dgt_ref, jnp.sum(dx * y_ref[...].astype(F32), axis=0, keepdims=True)[None], _seg_first(i))


def modnorm_bwd(name, x, dh, dx_in, mods, g, m_sh, m_sc, gate=None, after=None, lat_only=False):
    def body(i, x_ref, dh_ref, dxin_ref, *rest):
        if gate:
            y_ref, sh_ref, sc_ref, gt_ref, g_ref, dx_ref, dy_ref, dgt_ref, dsh_ref, dsc_ref, dg_ref = rest
        else:
            sh_ref, sc_ref, g_ref, dx_ref, dsh_ref, dsc_ref, dg_ref = rest
        _, vjp = jax.vjp(_modnorm, x_ref[...], g_ref[...], sh_ref[0], sc_ref[0])
        dx, dg, dsh, dsc = vjp(dh_ref[...].astype(F32))
        dx = dxin_ref[...] + dx
        dx_ref[...] = dx
        if gate:
            _gate_grads(dx, y_ref, gt_ref, dy_ref, dgt_ref, i)
        _acc(dsh_ref, dsh[None], _seg_first(i))
        _acc(dsc_ref, dsc[None], _seg_first(i))
        _acc(dg_ref, dg, i == 0)

    if gate:
        y, gmods, m = gate
        return _rb_call(name, body, row_in=(x, dh, dx_in, y), mod_in=((mods, m_sh), (mods, m_sc), (gmods, m)),
                        full_in=(g,), row_out=((D, F32), (D, BF)), seg_out=(D, D, D), acc_out=((1, D),), after=after)
    return _rb_call(name, body, row_in=(x, dh, dx_in), mod_in=((mods, m_sh), (mods, m_sc)), full_in=(g,),
                    row_out=((D, F32, "lat") if lat_only else (D, F32),), seg_out=(D, D), acc_out=((1, D),),
                    after=after)


def proj_in(name, h, w):
    n = w.shape[1]

    def body(i, h_ref, w_ref, o_ref):
        o_ref[...] = jnp.dot(h_ref[...], w_ref[...], preferred_element_type=F32).astype(BF)

    return _rb_call(name, body, row_in=(h,), full_in=(w,), row_out=((n, BF),))[0]


def proj_out(name, a1, a2, w, x, mods, m_gate, g_next, m_sh, m_sc):
    k1 = a1.shape[1]

    def body(i, a1_ref, a2_ref, x_ref, gt_ref, sh_ref, sc_ref, w_ref, g_ref, xo_ref, y_ref, h_ref):
        y = jnp.dot(a1_ref[...], w_ref[:k1, :], preferred_element_type=F32)
        y = y + jnp.dot(a2_ref[...], w_ref[k1:, :], preferred_element_type=F32)
        y_ref[...] = y.astype(BF)
        xn = x_ref[...] + gt_ref[0] * y
        xo_ref[...] = xn
        h_ref[...] = _modnorm(xn, g_ref[...], sh_ref[0], sc_ref[0]).astype(BF)

    return _rb_call(name, body, row_in=(a1, a2, x), mod_in=((mods, m_gate), (mods, m_sh), (mods, m_sc)),
                    full_in=(w, g_next), row_out=((D, F32), (D, BF), (D, BF)))


def mlp_up(name, h, w1):
    def body(i, h_ref, w_ref, a_ref, f_ref):
        hv = h_ref[...]
        for n in range(4):
            a = jnp.dot(hv, w_ref[n], preferred_element_type=F32)
            a_ref[:, n * D:(n + 1) * D] = a.astype(BF)
            r = jnp.maximum(a, 0.0)
            f_ref[:, n * D:(n + 1) * D] = (r * r).astype(BF)

    return _rb_call(name, body, row_in=(h,), full_in=(w1,), row_out=((FF, BF), (FF, BF)))


def mlp_down(name, f, w2, x, mods, m_gate, nxt=None):
    def body(i, f_ref, x_ref, gt_ref, *rest):
        if nxt:
            sh_ref, sc_ref, w_ref, g_ref, xo_ref, y_ref, h_ref = rest
        else:
            w_ref, xo_ref, y_ref = rest
        y = jnp.dot(f_ref[:, 0:D], w_ref[0], preferred_element_type=F32)
        for n in range(1, 4):
            y = y + jnp.dot(f_ref[:, n * D:(n + 1) * D], w_ref[n], preferred_element_type=F32)
        xn = x_ref[...] + gt_ref[0] * y
        y_ref[...] = y.astype(BF)
        xo_ref[...] = xn
        if nxt:
            h_ref[...] = _modnorm(xn, g_ref[...], sh_ref[0], sc_ref[0]).astype(BF)

    if nxt:
        return _rb_call(name, body, row_in=(f, x), mod_in=((mods, m_gate), (nxt[0], 0), (nxt[0], 1)),
                        full_in=(w2, nxt[1]), row_out=((D, F32), (D, BF), (D, BF)))
    return _rb_call(name, body, row_in=(f, x), mod_in=((mods, m_gate),), full_in=(w2,),
                    row_out=((D, F32), (D, BF)))


def mm_nt(name, g, w, after=None):
    k = w.shape[0]

    def body(i, g_ref, w_ref, o_ref):
        o_ref[...] = lax.dot_general(g_ref[...], w_ref[...], NT, preferred_element_type=F32).astype(BF)

    return _rb_call(name, body, row_in=(g,), full_in=(w,), row_out=((k, BF),), after=after)[0]


def mlp_bwd_da(name, dy, w2, a, after=None):
    def body(i, dy_ref, a_ref, w_ref, da_ref):
        dyv = dy_ref[...]
        for n in range(4):
            df = lax.dot_general(dyv, w_ref[n], NT, preferred_element_type=F32)
            av = a_ref[:, n * D:(n + 1) * D].astype(F32)
            da_ref[:, n * D:(n + 1) * D] = (df * (2.0 * jnp.maximum(av, 0.0))).astype(BF)

    return _rb_call(name, body, row_in=(dy, a), full_in=(w2,), row_out=((FF, BF),), after=after)[0]


def mlp_bwd_dh(name, da, w1):
    def body(i, da_ref, w_ref, dh_ref):
        acc = lax.dot_general(da_ref[:, 0:D], w_ref[0], NT, preferred_element_type=F32)
        for n in range(1, 4):
            acc = acc + lax.dot_general(da_ref[:, n * D:(n + 1) * D], w_ref[n], NT, preferred_element_type=F32)
        dh_ref[...] = acc.astype(BF)

    return _rb_call(name, body, row_in=(da,), full_in=(w1,), row_out=((D, BF),))[0]


TN_ROWS = 1536


def mm_tn(name, a, g, tiles, th, tw):
    nt = len(tiles)
    acs = jnp.asarray([t[0] for t in tiles], jnp.int32)
    gcs = jnp.asarray([t[1] for t in tiles], jnp.int32)
    nr = R // TN_ROWS

    def kern(ac_ref, gc_ref, a_ref, g_ref, o_ref, acc_ref):
        r = pl.program_id(1)

        @pl.when(r == 0)
        def _():
            acc_ref[...] = jnp.zeros_like(acc_ref)

        acc_ref[...] += lax.dot_general(a_ref[...], g_ref[...], TN, preferred_element_type=F32)

        @pl.when(r == nr - 1)
        def _():
            o_ref[...] = acc_ref[...].astype(BF)

    grid_spec = pltpu.PrefetchScalarGridSpec(
        num_scalar_prefetch=2, grid=(nt, nr),
        in_specs=[pl.BlockSpec((TN_ROWS, th), lambda t, r, ac, gc: (r, ac[t])),
                  pl.BlockSpec((TN_ROWS, tw), lambda t, r, ac, gc: (r, gc[t]))],
        out_specs=pl.BlockSpec((None, th, tw), lambda t, r, ac, gc: (t, 0, 0)),
        scratch_shapes=[pltpu.VMEM((th, tw), F32)])
    return pl.pallas_call(kern, grid_spec=grid_spec, out_shape=jax.ShapeDtypeStruct((nt, th, tw), BF),
                          compiler_params=_cparams(("parallel", "arbitrary")), name=name)(acs, gcs, a, g)


def _even_tok(q, k, zu, zv, gq, gk, gs, ws, bs, cq, sq, ck, sk, avg, masks):
    qr = _rope(_grmsn(q, gq, avg), cq, sq) * GQA_SCALE
    kr = _rope(_grmsn(k, gk, avg[:128, :128]), ck, sk)
    u = _gelu(zu)
    v = _grmsn(_gelu(zv), gs, avg)
    sv = None
    for g in range(8):
        t = masks[g] * (_mm(ws[g], v) + bs[g])
        sv = t if sv is None else sv + t
    return qr, kr, u * sv


def even_tok_fwd(p, cos, sin, gq, gk, gs, sgu_w, sgu_b, avg, masks):
    def body(i, p_ref, cos_ref, sin_ref, gq_ref, gk_ref, gs_ref, w_ref, b_ref, avg_ref, mk_ref, q_ref, kv_ref, m_ref):
        avgv = avg_ref[...]
        ws = [w_ref[g] for g in range(8)]
        bs = [b_ref[g] for g in range(8)]
        mks = [mk_ref[g] for g in range(8)]
        for c in range(2):
            rs = pl.ds(c * 128, 128)
            qr, kr, m = _even_tok(
                p_ref[rs, 0:512].astype(F32), p_ref[rs, 512:640].astype(F32),
                p_ref[rs, 768:1280].astype(F32), p_ref[rs, 1280:1792].astype(F32),
                gq_ref[...], gk_ref[...], gs_ref[...], ws, bs,
                cos_ref[rs, :], sin_ref[rs, :], cos_ref[rs, 0:128], sin_ref[rs, 0:128], avgv, mks)
            q_ref[rs, :] = qr.astype(BF)
            kv_ref[rs, 0:128] = kr.astype(BF)
            kv_ref[rs, 128:256] = p_ref[rs, 640:768]
            m_ref[rs, :] = m.astype(BF)

    return _rb_call("even_tok_fwd", body, row_in=(p,), pos_in=(cos, sin),
                    full_in=(gq, gk, gs, sgu_w, sgu_b, avg, masks), row_out=((512, BF), (256, BF), (512, BF)))


def even_tok_bwd(p, dq, dkvt, dcat, cos, sin, gq, gk, gs, sgu_w, sgu_b, avg, masks):
    def body(i, p_ref, dq_ref, dcat_ref, dkvt_ref, cos_ref, sin_ref, gq_ref, gk_ref, gs_ref, w_ref, b_ref,
             avg_ref, mk_ref, dp_ref, dgq_ref, dgk_ref, dgs_ref, dw_ref, db_ref):
        avgv = avg_ref[...]
        ws = [w_ref[g] for g in range(8)]
        bs = [b_ref[g] for g in range(8)]
        mks = [mk_ref[g] for g in range(8)]
        tot = None
        for c in range(2):
            rs = pl.ds(c * 128, 128)
            cq, sq, ck, sk = cos_ref[rs, :], sin_ref[rs, :], cos_ref[rs, 0:128], sin_ref[rs, 0:128]

            def f(q, k, zu, zv, gq, gk, gs, ws, bs):
                return _even_tok(q, k, zu, zv, gq, gk, gs, ws, bs, cq, sq, ck, sk, avgv, mks)

            _, vjp = jax.vjp(f, p_ref[rs, 0:512].astype(F32), p_ref[rs, 512:640].astype(F32),
                             p_ref[rs, 768:1280].astype(F32), p_ref[rs, 1280:1792].astype(F32),
                             gq_ref[...], gk_ref[...], gs_ref[...], ws, bs)
            dk = dkvt_ref[0:128, c * 128:(c + 1) * 128].T
            dv = dkvt_ref[128:256, c * 128:(c + 1) * 128].T
            d = vjp((dq_ref[rs, :].astype(F32), dk, dcat_ref[rs, 512:1024].astype(F32)))
            dp_ref[rs, 0:512] = d[0].astype(BF)
            dp_ref[rs, 512:640] = d[1].astype(BF)
            dp_ref[rs, 640:768] = dv.astype(BF)
            dp_ref[rs, 768:1280] = d[2].astype(BF)
            dp_ref[rs, 1280:1792] = d[3].astype(BF)
            part = [d[4], d[5], d[6]] + list(d[7]) + list(d[8])
            tot = part if tot is None else [a + b for a, b in zip(tot, part)]
        refs = [dgq_ref, dgk_ref, dgs_ref] + [dw_ref.at[g] for g in range(8)] + [db_ref.at[g] for g in range(8)]
        for ref, val in zip(refs, tot):
            _acc(ref, val, i == 0)

    return _rb_call("even_tok_bwd", body, row_in=(p, dq, dcat), col_in=(dkvt,), pos_in=(cos, sin),
                    full_in=(gq, gk, gs, sgu_w, sgu_b, avg, masks), row_out=((EV_IN, BF),),
                    acc_out=((1, 512), (1, 128), (1, 512), (8, 128, 128), (8, 128, 1)))


MLA_SCALE = 96 ** -0.5
GQA_SCALE = 64 ** -0.5


def _odd_tok(cq, ckv, kr, za, zg, gq, gkv, wq, wkk, wkv, spread, cr, sr, ck, sk):
    cqn = _rmsn(cq, gq)
    q = _rope(_mm(cqn, wq), cr, sr) * MLA_SCALE
    ckn = _rmsn(ckv, gkv)
    k = _mm(ckn, wkk) + _mm(_rope(kr, ck, sk), spread)
    v = _mm(ckn, wkv)
    y = za * jax.nn.sigmoid(zg)
    return q, k, v, y


def odd_tok_fwd(p, cos, sin, gq, gkv, wq, wkk, wkv, spread):
    def body(i, p_ref, cos_ref, sin_ref, gq_ref, gkv_ref, wq_ref, wkk_ref, wkv_ref, sp_ref, q_ref, kv_ref, y_ref):
        q, k, v, y = _odd_tok(
            p_ref[:, 0:256].astype(F32), p_ref[:, 256:384].astype(F32), p_ref[:, 384:512].astype(F32),
            p_ref[:, 512:1024].astype(F32), p_ref[:, 1024:1536].astype(F32),
            gq_ref[...], gkv_ref[...], wq_ref[...], wkk_ref[...], wkv_ref[...], sp_ref[...],
            cos_ref[:, 0:768], sin_ref[:, 0:768], cos_ref[:, 768:896], sin_ref[:, 768:896])
        q_ref[...] = q.astype(BF)
        kv_ref[:, 0:768] = k.astype(BF)
        kv_ref[:, 768:1280] = v.astype(BF)
        y_ref[...] = y.astype(BF)

    return _rb_call("odd_tok_fwd", body, row_in=(p,), pos_in=(cos, sin), full_in=(gq, gkv, wq, wkk, wkv, spread),
                    row_out=((768, BF), (1280, BF), (512, BF)))


def odd_tok_bwd(p, dq, dkvt, dy, cos, sin, gq, gkv, wq, wkk, wkv, spread):
    def body(i, p_ref, dq_ref, dy_ref, dkvt_ref, cos_ref, sin_ref, gq_ref, gkv_ref, wq_ref, wkk_ref, wkv_ref, sp_ref,
             dp_ref, dgq_ref, dgkv_ref, dwq_ref, dwkk_ref, dwkv_ref):
        cr, sr, ck, sk = cos_ref[:, 0:768], sin_ref[:, 0:768], cos_ref[:, 768:896], sin_ref[:, 768:896]
        spread_v = sp_ref[...]

        def f(cq, ckv, kr, za, zg, gq, gkv, wq, wkk, wkv):
            return _odd_tok(cq, ckv, kr, za, zg, gq, gkv, wq, wkk, wkv, spread_v, cr, sr, ck, sk)

        _, vjp = jax.vjp(f, p_ref[:, 0:256].astype(F32), p_ref[:, 256:384].astype(F32),
                         p_ref[:, 384:512].astype(F32), p_ref[:, 512:1024].astype(F32),
                         p_ref[:, 1024:1536].astype(F32), gq_ref[...], gkv_ref[...], wq_ref[...],
                         wkk_ref[...], wkv_ref[...])
        d = vjp((dq_ref[...].astype(F32), dkvt_ref[0:768, :].T, dkvt_ref[768:1280, :].T, dy_ref[...].astype(F32)))
        dp_ref[:, 0:256] = d[0].astype(BF)
        dp_ref[:, 256:384] = d[1].astype(BF)
        dp_ref[:, 384:512] = d[2].astype(BF)
        dp_ref[:, 512:1024] = d[3].astype(BF)
        dp_ref[:, 1024:1536] = d[4].astype(BF)
        for ref, val in zip((dgq_ref, dgkv_ref, dwq_ref, dwkk_ref, dwkv_ref), d[5:]):
            _acc(ref, val, i == 0)

    return _rb_call("odd_tok_bwd", body, row_in=(p, dq, dy), col_in=(dkvt,), pos_in=(cos, sin),
                    full_in=(gq, gkv, wq, wkk, wkv, spread), row_out=((OD_PAD, BF),),
                    acc_out=((1, 256), (1, 128), (256, 768), (128, 768), (128, 512)))


GQA_HEADS = [(64 * h, 64 * (h // 4), 64, 128 + 64 * (h // 4)) for h in range(8)]
MLA_HEADS = [(96 * h, 96 * h, 96, 768 + 64 * h) for h in range(8)]


def _by_block(j, run):
    @pl.when(j == 0)
    def _():
        run(LC)

    @pl.when(j > 0)
    def _():
        run(SEQ)


def attn_fwd(name, q, kv, heads):
    qw, kvw = q.shape[1], kv.shape[1]

    def kern(q_ref, kv_ref, o_ref, lse_ref):
        def run(nk):
            for h, (qo, ko, w, vo) in enumerate(heads):
                s = lax.dot_general(q_ref[:, qo:qo + w], kv_ref[0:nk, ko:ko + w], NT, preferred_element_type=F32)
                m = jnp.max(s, axis=-1, keepdims=True)
                p = jnp.exp(s - m)
                l = jnp.sum(p, axis=-1, keepdims=True)
                o = jnp.dot(p.astype(BF), kv_ref[0:nk, vo:vo + 64], preferred_element_type=F32) / l
                o_ref[:, 64 * h:64 * h + 64] = o.astype(BF)
                lse_ref[:, h:h + 1] = m + jnp.log(l)

        _by_block(pl.program_id(1), run)

    return pl.pallas_call(
        kern, grid=(NEX, BPE),
        in_specs=[pl.BlockSpec((TB, qw), lambda e, j: (e * BPE + j, 0)),
                  pl.BlockSpec((SEQ, kvw), lambda e, j: (e, 0))],
        out_specs=[pl.BlockSpec((TB, 512), lambda e, j: (e * BPE + j, 0)),
                   pl.BlockSpec((TB, 8), lambda e, j: (e * BPE + j, 0))],
        out_shape=[jax.ShapeDtypeStruct((R, 512), BF), jax.ShapeDtypeStruct((R, 8), F32)],
        compiler_params=_cparams(("parallel", "arbitrary")), name=name)(q, kv)


def attn_bwd(name, q, kv, o, dcat, lse, heads):
    qw, kvw = q.shape[1], kv.shape[1]

    def kern(q_ref, kv_ref, o_ref, do_ref, lse_ref, dq_ref, dkvt_ref):
        j = pl.program_id(1)

        @pl.when(j == 0)
        def _():
            dkvt_ref[...] = jnp.zeros_like(dkvt_ref)

        def run(nk):
            for h, (qo, ko, w, vo) in enumerate(heads):
                qh = q_ref[:, qo:qo + w]
                kh = kv_ref[0:nk, ko:ko + w]
                s = lax.dot_general(qh, kh, NT, preferred_element_type=F32)
                p = jnp.exp(s - lse_ref[:, h:h + 1])
                do = do_ref[:, 64 * h:64 * h + 64]
                dsum = jnp.sum(do.astype(F32) * o_ref[:, 64 * h:64 * h + 64].astype(F32), axis=-1, keepdims=True)
                dp = lax.dot_general(do, kv_ref[0:nk, vo:vo + 64], NT, preferred_element_type=F32)
                ds = (p * (dp - dsum)).astype(BF)
                dkvt_ref[vo:vo + 64, 0:nk] += lax.dot_general(do, p.astype(BF), TN, preferred_element_type=F32)
                dq_ref[:, qo:qo + w] = jnp.dot(ds, kh, preferred_element_type=F32).astype(BF)
                dkvt_ref[ko:ko + w, 0:nk] += lax.dot_general(qh, ds, TN, preferred_element_type=F32)

        _by_block(j, run)

    return pl.pallas_call(
        kern, grid=(NEX, BPE),
        in_specs=[pl.BlockSpec((TB, qw), lambda e, j: (e * BPE + j, 0)),
                  pl.BlockSpec((SEQ, kvw), lambda e, j: (e, 0)),
                  pl.BlockSpec((TB, 512), lambda e, j: (e * BPE + j, 0)),
                  pl.BlockSpec((TB, 512), lambda e, j: (e * BPE + j, 0)),
                  pl.BlockSpec((TB, 8), lambda e, j: (e * BPE + j, 0))],
        out_specs=[pl.BlockSpec((TB, qw), lambda e, j: (e * BPE + j, 0)),
                   pl.BlockSpec((kvw, SEQ), lambda e, j: (0, e))],
        out_shape=[jax.ShapeDtypeStruct((R, qw), BF), jax.ShapeDtypeStruct((kvw, R), F32)],
        compiler_params=_cparams(("parallel", "arbitrary")), name=name)(q, kv, o, dcat, lse)


HALO = 16
CONV_K = 31


def _fill_ext(ext_ref, prev_ref, cur_ref, next_ref, i):
    j = i % BPE
    has_prev = (j >= 2).astype(F32)
    has_next = jnp.logical_and(j >= 1, j <= BPE - 2).astype(F32)
    ext_ref[0:HALO, :] = prev_ref[TB - HALO:TB, :].astype(F32) * has_prev
    ext_ref[HALO:HALO + TB, :] = cur_ref[...].astype(F32)
    ext_ref[HALO + TB:2 * HALO + TB, :] = next_ref[0:HALO, :].astype(F32) * has_next


PHASE_ROWS = TB + 24


def _phases(ext_ref, ph_ref):
    for r in range(8):
        ph_ref[r] = ext_ref[r:r + PHASE_ROWS, :]


def _window(ph_ref, off):
    return ph_ref[off % 8, 8 * (off // 8):8 * (off // 8) + TB, :]


def _ln_silu(z, g, b):
    mu = jnp.mean(z, axis=-1, keepdims=True)
    zc = z - mu
    var = jnp.mean(zc * zc, axis=-1, keepdims=True)
    return _silu(zc * lax.rsqrt(var + EPS) * g + b)


def conf_fwd(y, cw, cb, lg, lb):
    def body(i, cur_ref, cw_ref, cb_ref, lg_ref, lb_ref, prev_ref, next_ref, z_ref, c_ref, ext_ref, ph_ref):
        _fill_ext(ext_ref, prev_ref, cur_ref, next_ref, i)
        _phases(ext_ref, ph_ref)
        acc = _window(ph_ref, 1) * cw_ref[0:1, :]
        for k in range(1, CONV_K):
            acc = acc + _window(ph_ref, k + 1) * cw_ref[k:k + 1, :]
        z = acc + cb_ref[...]
        z_ref[...] = z.astype(BF)
        c_ref[...] = _ln_silu(z, lg_ref[...], lb_ref[...]).astype(BF)

    return _rb_call("conf_fwd", body, row_in=(y,), full_in=(cw, cb, lg, lb), shift_in=((y, -1), (y, 1)),
                    row_out=((512, BF), (512, BF)),
                    scratch=(pltpu.VMEM((TB + 2 * HALO, 512), F32), pltpu.VMEM((8, PHASE_ROWS, 512), F32)))


def conf_bwd_ln(z, dcat, lg, lb):
    def body(i, z_ref, dcat_ref, lg_ref, lb_ref, dz_ref, dlg_ref, dlb_ref, dcb_ref):
        _, vjp = jax.vjp(_ln_silu, z_ref[...].astype(F32), lg_ref[...], lb_ref[...])
        dz, dlg, dlb = vjp(dcat_ref[:, 512:1024].astype(F32))
        dz_ref[...] = dz.astype(BF)
        _acc(dlg_ref, dlg, i == 0)
        _acc(dlb_ref, dlb, i == 0)
        _acc(dcb_ref, jnp.sum(dz, axis=0, keepdims=True), i == 0)

    return _rb_call("conf_bwd_ln", body, row_in=(z, dcat), full_in=(lg, lb), row_out=((512, BF),),
                    acc_out=((1, 512), (1, 512), (1, 512)))


def conf_bwd_conv(y, dz, cw):
    def body(i, y_ref, dz_ref, cw_ref, yp_ref, yn_ref, dzp_ref, dzn_ref, dy_ref, dcw_ref, ext_ref, phy_ref, phd_ref):
        _fill_ext(ext_ref, yp_ref, y_ref, yn_ref, i)
        _phases(ext_ref, phy_ref)
        _fill_ext(ext_ref, dzp_ref, dz_ref, dzn_ref, i)
        _phases(ext_ref, phd_ref)
        dzv = dz_ref[...].astype(F32)

        @pl.when(i == 0)
        def _():
            dcw_ref[...] = jnp.zeros_like(dcw_ref)

        acc = None
        for k in range(CONV_K):
            t = _window(phd_ref, CONV_K - k) * cw_ref[k:k + 1, :]
            acc = t if acc is None else acc + t
            dcw_ref[k:k + 1, :] += jnp.sum(dzv * _window(phy_ref, k + 1), axis=0, keepdims=True)
        dy_ref[...] = acc.astype(BF)

    return _rb_call("conf_bwd_conv", body, row_in=(y, dz), full_in=(cw,),
                    shift_in=((y, -1), (y, 1), (dz, -1), (dz, 1)), row_out=((512, BF),), acc_out=((32, 512),),
                    scratch=(pltpu.VMEM((TB + 2 * HALO, 512), F32), pltpu.VMEM((8, PHASE_ROWS, 512), F32),
                             pltpu.VMEM((8, PHASE_ROWS, 512), F32)))


def final_loss(x, target, fg, y, mods, m_gate):
    lpb = L // TB

    def kern(x_ref, t_ref, g_ref, y_ref, gt_ref, dx_ref, dy_ref, dgt_ref, loss_ref, dg_ref):
        i = pl.program_id(0)
        lat = (i % BPE) >= 1
        xv, tv = x_ref[...], t_ref[...]

        def f(x, g):
            err = _rmsn(x, g) - tv
            rowsum = jnp.sum(err * err, axis=-1, keepdims=True)
            return jnp.sum(rowsum, axis=0, keepdims=True) * (0.5 / D)

        lv, vjp = jax.vjp(f, xv, g_ref[...])
        dx, dg = vjp(jnp.ones((1, 1), F32))
        m = lat.astype(F32)
        dx = dx * m
        dx_ref[...] = dx
        _gate_grads(dx, y_ref, gt_ref, dy_ref, dgt_ref, i)
        _acc(loss_ref, jnp.zeros((8, 128), F32) + lv * m, i == 0)
        _acc(dg_ref, dg * m, i == 0)

    row = pl.BlockSpec((TB, D), lambda i: (i, 0))
    return pl.pallas_call(
        kern, grid=(NBLK,),
        in_specs=[row, pl.BlockSpec((TB, D), lambda i: ((i // BPE) * lpb + jnp.maximum(i % BPE - 1, 0), 0)),
                  pl.BlockSpec((1, D), lambda i: (0, 0)), row,
                  pl.BlockSpec((1, 1, D), lambda i: (_seg(i) * N_MOD + m_gate, 0, 0))],
        out_specs=[row, row, pl.BlockSpec((1, 1, D), lambda i: (_seg(i), 0, 0)),
                   pl.BlockSpec((8, 128), lambda i: (0, 0)), pl.BlockSpec((1, D), lambda i: (0, 0))],
        out_shape=[jax.ShapeDtypeStruct((R, D), F32), jax.ShapeDtypeStruct((R, D), BF),
                   jax.ShapeDtypeStruct((4, 1, D), F32), jax.ShapeDtypeStruct((8, 128), F32),
                   jax.ShapeDtypeStruct((1, D), F32)],
        compiler_params=_cparams(("arbitrary",)), name="final_loss")(x, target, fg, y, mods)


NC = 24


def mods_fwd(call, ada_w, ada_b):
    cols = ada_w.shape[2]

    def kern(c_ref, w_ref, b_ref, o_ref):
        o_ref[...] = jnp.dot(_silu(c_ref[...]), w_ref[...], precision=HI, preferred_element_type=F32) + b_ref[...]

    return pl.pallas_call(
        kern, grid=(2,),
        in_specs=[pl.BlockSpec((NC, D), lambda l: (0, 0)), pl.BlockSpec((None, D, cols), lambda l: (l, 0, 0)),
                  pl.BlockSpec((None, 1, cols), lambda l: (l, 0, 0))],
        out_specs=pl.BlockSpec((None, NC, cols), lambda l: (l, 0, 0)),
        out_shape=jax.ShapeDtypeStruct((2, NC, cols), F32),
        compiler_params=_cparams(("parallel",)), name="mods_fwd")(call, ada_w, ada_b)


def ada_bwd(call, ada_w, dm):
    cols = ada_w.shape[2]

    def kern(c_ref, w_ref, dm_ref, gw_ref, dc_ref):
        l = pl.program_id(0)
        gw_ref[...] = lax.dot_general(_silu(c_ref[...]), dm_ref[...], TN, precision=HI, preferred_element_type=F32)
        part = lax.dot_general(dm_ref[16:24, :], w_ref[...], NT, precision=HI, preferred_element_type=F32)
        cc = c_ref[16:17, :]
        sg = jax.nn.sigmoid(cc)
        _acc(dc_ref, part * (sg * (1.0 + cc * (1.0 - sg))), l == 0)

    return pl.pallas_call(
        kern, grid=(2,),
        in_specs=[pl.BlockSpec((NC, D), lambda l: (0, 0)), pl.BlockSpec((None, D, cols), lambda l: (l, 0, 0)),
                  pl.BlockSpec((None, NC, cols), lambda l: (l, 0, 0))],
        out_specs=[pl.BlockSpec((None, D, cols), lambda l: (l, 0, 0)), pl.BlockSpec((8, D), lambda l: (0, 0))],
        out_shape=[jax.ShapeDtypeStruct((2, D, cols), F32), jax.ShapeDtypeStruct((8, D), F32)],
        compiler_params=_cparams(("arbitrary",)), name="ada_bwd")(call, ada_w, dm)


def sum_lead(name, a, after=None):
    n, r, c = a.shape
    tr = r
    for cand in (512, 256, 128, 64, 32, 16, 8):
        if r % cand == 0 and cand * c * 4 * n <= 8 * 1024 * 1024:
            tr = cand
            break
    extra = [] if after is None else [after]

    def kern(a_ref, *rest):
        acc = a_ref[0].astype(F32)
        for k in range(1, n):
            acc = acc + a_ref[k].astype(F32)
        rest[-1][...] = acc

    return pl.pallas_call(
        kern, grid=(r // tr,),
        in_specs=[pl.BlockSpec((n, tr, c), lambda i: (0, i, 0))]
        + [pl.BlockSpec(e.shape, lambda i, k=e.ndim: (0,) * k) for e in extra],
        out_specs=pl.BlockSpec((tr, c), lambda i: (i, 0)), out_shape=jax.ShapeDtypeStruct((r, c), F32),
        compiler_params=_cparams(("parallel",)), name=name)(a, *extra)


def add_pairs(name, hs, got, half):
    _, _, r, c = hs.shape

    def kern(half_ref, a_ref, b_ref, o_ref):
        o_ref[...] = (a_ref[...].astype(F32) + b_ref[...].astype(F32)).astype(BF)

    spec = pl.BlockSpec((None, r, c), lambda j, h: (j, 0, 0))
    grid_spec = pltpu.PrefetchScalarGridSpec(
        num_scalar_prefetch=1, grid=(4,),
        in_specs=[pl.BlockSpec((None, None, r, c), lambda j, h: (h[0], j, 0, 0)), spec], out_specs=spec)
    return pl.pallas_call(kern, grid_spec=grid_spec, out_shape=jax.ShapeDtypeStruct(got.shape, BF),
                          compiler_params=_cparams(("parallel",)), name=name)(half, hs, got)


def sum_slabs(name, land, own, where, full, lead):
    _, r, c = land.shape
    tr = r
    for cand in (512, 256, 128, 64, 32, 16):
        if r % cand == 0 and cand * c * 16 <= 4 * 1024 * 1024:
            tr = cand
            break

    def kern(where_ref, full_ref, land_ref, own_ref, o_ref):
        me = where_ref[0]
        acc = None
        for k in range(4):
            t = jnp.where(me == k, own_ref[k], land_ref[k]).astype(F32)
            acc = t if acc is None else acc + t
        o_ref[...] = acc

    spec = pl.BlockSpec((4, tr, c), lambda i, m: (0, i, 0))
    grid_spec = pltpu.PrefetchScalarGridSpec(
        num_scalar_prefetch=1, grid=(r // tr,), in_specs=[pl.BlockSpec(memory_space=pl.ANY), spec, spec],
        out_specs=pl.BlockSpec((None, None, tr, c), lambda i, m: (lead, m[1], i, 0)))
    return pl.pallas_call(kern, grid_spec=grid_spec, out_shape=jax.ShapeDtypeStruct(full.shape, F32),
                          input_output_aliases={1: 0}, compiler_params=_cparams(("parallel",)),
                          name=name)(where, full, land, own)


def adamw(name, w, g, m, v, again=False):
    r, c = w.shape
    tr = r
    for cand in (512, 256, 128, 64, 32, 16, 8):
        if r % cand == 0 and cand * c * 4 <= 2 * 1024 * 1024:
            tr = cand
            break
    c1 = 1.0 / (1.0 - ADAM_B1 ** ADAM_STEP)
    c2 = 1.0 / (1.0 - ADAM_B2 ** ADAM_STEP)

    def kern(w_ref, g_ref, m_ref, v_ref, d_ref, mo_ref, vo_ref, *go_ref):
        gv = g_ref[...]
        mn = ADAM_B1 * m_ref[...] + (1.0 - ADAM_B1) * gv
        vn = ADAM_B2 * v_ref[...] + (1.0 - ADAM_B2) * (gv * gv)
        d_ref[...] = -ADAM_LR * ((mn * c1) / (jnp.sqrt(vn * c2) + ADAM_EPS) + ADAM_WD * w_ref[...])
        mo_ref[...] = mn
        vo_ref[...] = vn
        if again:
            go_ref[0][...] = gv

    spec = pl.BlockSpec((tr, c), lambda i: (i, 0))
    shp = jax.ShapeDtypeStruct((r, c), F32)
    n_out = 4 if again else 3
    return pl.pallas_call(kern, grid=(r // tr,), in_specs=[spec] * 4, out_specs=[spec] * n_out,
                          out_shape=[shp] * n_out, compiler_params=_cparams(("parallel",)), name=name)(w, g, m, v)


def all_gather8(name, xs, after=None):
    m_per, n = xs.shape
    extra = [] if after is None else [after]

    def body(x_ref, *rest):
        out_ref, send_sems, recv_sems, local_sem = rest[len(extra):]
        x, y, c = lax.axis_index("x"), lax.axis_index("y"), lax.axis_index("c")
        me, sibling = (x, y, c), (x, y, 1 - c)
        chips = [(1 - x, y), (x, 1 - y), (1 - x, 1 - y)]

        def rows(px, py, pc):
            return out_ref.at[pl.ds((4 * px + 2 * py + pc) * m_per, m_per), :]

        def copy(k, block, to, src=None):
            return pltpu.make_async_remote_copy(
                src_ref=rows(*block) if src is None else src, dst_ref=rows(*block),
                send_sem=send_sems.at[k], recv_sem=recv_sems.at[k], device_id=to, device_id_type=MESH)

        mine = pltpu.make_async_copy(x_ref, rows(*me), local_sem)
        mine.start()
        first = [copy(0, me, sibling, src=x_ref)]
        first += [copy(1 + j, me, (*chip, c), src=x_ref) for j, chip in enumerate(chips)]
        for cp in first:
            cp.start()
        passed = [copy(4 + j, (*chip, c), sibling) for j, chip in enumerate(chips)]
        for j, chip in enumerate(chips):
            copy(1 + j, (*chip, c), me).wait_recv()
            passed[j].start()
        copy(0, sibling, me).wait_recv()
        for j, chip in enumerate(chips):
            copy(4 + j, (*chip, 1 - c), me).wait_recv()
        for cp in first + passed:
            cp.wait_send()
        mine.wait()

    return pl.pallas_call(
        body, out_shape=jax.ShapeDtypeStruct((8 * m_per, n), xs.dtype),
        in_specs=[pl.BlockSpec(memory_space=pltpu.VMEM)] * (1 + len(extra)),
        out_specs=pl.BlockSpec(memory_space=pltpu.VMEM),
        scratch_shapes=[pltpu.SemaphoreType.DMA((7,)), pltpu.SemaphoreType.DMA((7,)), pltpu.SemaphoreType.DMA],
        compiler_params=pltpu.CompilerParams(vmem_limit_bytes=VMEM_LIMIT), name=name)(xs, *extra)


def sibling_merge(name, fulls):
    n = len(fulls)
    slots = [(a, l) for a in range(n) for l in range(fulls[a].shape[0])]

    def body(*refs):
        buf = refs[n:2 * n]
        send_sems, recv_sems = refs[2 * n], refs[2 * n + 1]
        c = lax.axis_index("c")
        sibling = (lax.axis_index("x"), lax.axis_index("y"), 1 - c)
        sends, recvs = [], []
        for k, (a, l) in enumerate(slots):
            kw = dict(send_sem=send_sems.at[k], recv_sem=recv_sems.at[k], device_id=sibling, device_id_type=MESH)
            sends.append(pltpu.make_async_remote_copy(src_ref=buf[a].at[l, c], dst_ref=buf[a].at[l, c], **kw))
            recvs.append(pltpu.make_async_remote_copy(src_ref=buf[a].at[l, c], dst_ref=buf[a].at[l, 1 - c], **kw))
        for cp in sends:
            cp.start()
        for cp in recvs:
            cp.wait_recv()
        for cp in sends:
            cp.wait_send()

    anyspec = pl.BlockSpec(memory_space=pl.ANY)
    return pl.pallas_call(
        body, out_shape=[jax.ShapeDtypeStruct(s.shape, s.dtype) for s in fulls],
        in_specs=[anyspec] * n, out_specs=[anyspec] * n, input_output_aliases={a: a for a in range(n)},
        scratch_shapes=[pltpu.SemaphoreType.DMA((len(slots),)), pltpu.SemaphoreType.DMA((len(slots),))],
        name=name)(*fulls)


def place_own(name, land, src, chip):
    c = src.shape[-1]
    r = src.size // c
    tr = r
    for cand in (1024, 512, 256, 128, 64, 32, 16):
        if r % cand == 0 and cand * c * 2 <= 2 * 1024 * 1024:
            tr = cand
            break

    def kern(chip_ref, land_ref, src_ref, out_ref):
        out_ref[...] = src_ref[...]

    grid_spec = pltpu.PrefetchScalarGridSpec(
        num_scalar_prefetch=1, grid=(r // tr,),
        in_specs=[pl.BlockSpec(memory_space=pl.ANY), pl.BlockSpec((tr, c), lambda i, m: (i, 0))],
        out_specs=pl.BlockSpec((None, tr, c), lambda i, m: (m[0], i, 0)))
    out = pl.pallas_call(kern, grid_spec=grid_spec, out_shape=jax.ShapeDtypeStruct((4, r, c), land.dtype),
                         input_output_aliases={1: 0}, compiler_params=_cparams(("parallel",)),
                         name=name)(chip, land.reshape(4, r, c), src.reshape(r, c))
    return out.reshape(land.shape)


def _half_copies(src, land, send_sems, recv_sems):
    c = lax.axis_index("c")
    sibling = (lax.axis_index("x"), lax.axis_index("y"), 1 - c)
    pairs = []
    for a in range(len(src)):
        cp = pltpu.make_async_remote_copy(src_ref=src[a].at[1 - c], dst_ref=land[a], send_sem=send_sems.at[a],
                                          recv_sem=recv_sems.at[a], device_id=sibling, device_id_type=MESH)
        pairs.append((cp, cp))
    return pairs


def _chip_copies(src, land, send_sems, recv_sems, scatter):
    x, y, c = lax.axis_index("x"), lax.axis_index("y"), lax.axis_index("c")
    me = 2 * x + y
    pairs = []
    for a in range(len(src)):
        for j, (px, py) in enumerate([(1 - x, y), (x, 1 - y), (1 - x, 1 - y)]):
            to = 2 * px + py
            out = src[a].at[to] if scatter else src[a]
            kw = dict(send_sem=send_sems.at[3 * a + j], recv_sem=recv_sems.at[3 * a + j], device_id=(px, py, c),
                      device_id_type=MESH)
            pairs.append((pltpu.make_async_remote_copy(src_ref=out, dst_ref=land[a].at[me], **kw),
                          pltpu.make_async_remote_copy(src_ref=out, dst_ref=land[a].at[to], **kw)))
    return pairs


_HBM = pl.BlockSpec(memory_space=pltpu.HBM)
_SEM = pl.BlockSpec(memory_space=pltpu.SEMAPHORE)


GATHER = (functools.partial(_chip_copies, scatter=False), 3)
SCATTER = (functools.partial(_chip_copies, scatter=True), 3)
TO_SIBLING = (_half_copies, 1)


def _landing(shapes, dtype):
    return [lax.empty(tuple(s), dtype) for s in shapes]


def exchange_start(name, groups, plan):
    copies, per = plan
    sizes = [len(s) for s, _ in groups]
    flat = [a for s, l in groups for a in list(s) + list(l)]
    ng = len(groups)

    def body(*refs):
        ins, outs = refs[:len(flat)], refs[len(flat):]
        off = 0
        for g, n in enumerate(sizes):
            src, land = ins[off:off + n], ins[off + n:off + 2 * n]
            off += 2 * n
            for send, _ in copies(src, land, outs[2 * g], outs[2 * g + 1]):
                send.start()
        outs[-1][...] = jnp.zeros_like(outs[-1])

    out_shape = []
    for n in sizes:
        out_shape += [pltpu.SemaphoreType.DMA((per * n,)), pltpu.SemaphoreType.DMA((per * n,))]
    out_shape += [pltpu.HBM(a.shape, a.dtype) for a in flat] + [jax.ShapeDtypeStruct((8, 128), F32)]
    res = pl.pallas_call(
        body, out_shape=tuple(out_shape), in_specs=[_HBM] * len(flat),
        out_specs=tuple([_SEM] * (2 * ng) + [_HBM] * len(flat) + [pl.BlockSpec(memory_space=pltpu.VMEM)]),
        input_output_aliases={k: 2 * ng + k for k in range(len(flat))},
        compiler_params=pltpu.CompilerParams(has_side_effects=pltpu.SideEffectType.DATAFLOW_SIDE_EFFECTING),
        name=name)(*[pltpu.with_memory_space_constraint(a, pltpu.HBM) for a in flat])
    handles, off = [], 2 * ng
    for g, n in enumerate(sizes):
        handles.append((res[2 * g], res[2 * g + 1], list(res[off:off + n]), list(res[off + n:off + 2 * n])))
        off += 2 * n
    return handles, res[-1]


def exchange_wait(name, handle, after, plan):
    send_sems, recv_sems, srcs, lands = handle
    n = len(srcs)

    def body(*refs):
        src, land = refs[:n], refs[n:2 * n]
        for send, recv in plan[0](src, land, refs[2 * n], refs[2 * n + 1]):
            send.wait_send()
            recv.wait_recv()

    res = pl.pallas_call(
        body, out_shape=tuple(pltpu.HBM(a.shape, a.dtype) for a in srcs + lands),
        in_specs=[_HBM] * (2 * n) + [_SEM, _SEM, pl.BlockSpec(memory_space=pl.ANY)],
        out_specs=tuple([_HBM] * (2 * n)), input_output_aliases={k: k for k in range(2 * n)},
        compiler_params=pltpu.CompilerParams(has_side_effects=pltpu.SideEffectType.DATAFLOW_SIDE_EFFECTING),
        name=name)(*srcs, *lands, send_sems, recv_sems, after)
    return list(res[:n]), list(res[n:])


def _rope_tables(d_rot, reps):
    rows = L // GRID_W
    row = np.repeat(np.arange(rows), GRID_W).astype(np.float32)
    col = np.tile(np.arange(GRID_W), rows).astype(np.float32)
    d_axis = d_rot // 2
    inv = (ROPE_THETA ** (-np.arange(0, d_axis, 2, dtype=np.float32) / d_axis)).astype(np.float32)
    ang = np.concatenate([row[:, None] * inv, col[:, None] * inv], axis=-1).astype(np.float32)
    cos, sin = np.cos(ang).astype(np.float32), np.sin(ang).astype(np.float32)
    c = np.repeat(cos, 2, axis=-1)
    s = np.stack([-sin, sin], axis=-1).reshape(L, d_rot)
    c = np.concatenate([np.ones((LC, d_rot), np.float32), c], axis=0)
    s = np.concatenate([np.zeros((LC, d_rot), np.float32), s], axis=0)
    return np.tile(c, (1, reps)), np.tile(s, (1, reps))


def _group_consts():
    g = np.arange(512) // 64
    avg = (g[:, None] == g[None, :]).astype(np.float32) / 64.0
    masks = (np.arange(8)[:, None] == g[None, :]).astype(np.float32).reshape(8, 1, 512)
    return jnp.asarray(avg, BF), jnp.asarray(masks)


def _pack(items):
    flat = jnp.concatenate([a.reshape(-1).astype(F32) for a in items])
    n = flat.shape[0]
    rows = -(-n // D)
    rows = -(-rows // 8) * 8
    return jnp.pad(flat, (0, rows * D - n)).reshape(rows, D)


def _unpack(buf, shapes):
    lead = buf.shape[:-2]
    flat = buf.reshape(lead + (-1,))
    out, off = [], 0
    for shp in shapes:
        n = int(np.prod(shp))
        out.append(flat[..., off:off + n].reshape(lead + tuple(shp)))
        off += n
    return out


def _arrive(prm, key, after):
    if callable(prm[key]):
        prm[key](after)
    return prm[key]


def _layer_fwd(i, x, h, mods, prm, consts, nxt):
    sv = {}
    sv["x0"] = x
    sv["h"] = h
    p = proj_in(f"proj_in_{i}", h, _arrive(prm, "w_in", h))
    sv["p"] = p
    if i == 0:
        q, kv, m2 = even_tok_fwd(p, consts["cos_e"], consts["sin_e"], prm["gq"], prm["gk"], prm["gs"],
                                 prm["sgu_w"], prm["sgu_b"], consts["avg"], consts["masks"])
        o, lse = attn_fwd("attn_fwd_0", q, kv, GQA_HEADS)
        sv.update(q=q, kv=kv)
    else:
        q, kv, y = odd_tok_fwd(p, consts["cos_o"], consts["sin_o"], prm["gq"], prm["gkv"], prm["wq"], prm["wkk"],
                               prm["wkv"], consts["spread"])
        o, lse = attn_fwd("attn_fwd_1", q, kv, MLA_HEADS)
        z, m2 = conf_fwd(y, prm["conv_w"], prm["conv_b"], prm["ln_g"], prm["ln_b"])
        sv.update(q=q, kv=kv, y=y, z=z)
    sv.update(o=o, lse=lse, m2=m2)
    x1, y1, h2 = proj_out(f"proj_out_{i}", o, m2, _arrive(prm, "w_out", o), x, mods, 2, prm["norm2_g"], 3, 4)
    sv.update(x1=x1, y1=y1)
    a, f = mlp_up(f"mlp_up_{i}", h2, _arrive(prm, "w1", h2))
    x2, y2, *h_next = mlp_down(f"mlp_down_{i}", f, prm["w2"], x1, mods, 5, nxt)
    sv.update(h2=h2, a=a, f=f, y2=y2)
    return x2, (h_next[0] if h_next else None), sv


def _layer_bwd(i, dx, dy2, dg2, sv, mods, prm, consts, hook, entry, below):
    gr = {}
    da = mlp_bwd_da(f"mlp_bwd_da_{i}", dy2, prm["w2"], sv["a"], after=entry)
    tiles8 = [(h, j) for h in range(2) for j in range(4)]
    gr["w1"] = mm_tn(f"grad_w1_{i}", sv["h2"], da, tiles8, 512, D).reshape(2, 4, 512, D)
    gr["w2"] = mm_tn(f"grad_w2_{i}", sv["f"], dy2, [(2 * j + h, 0) for h in range(2) for j in range(4)],
                     512, D).reshape(2, 4, 512, D)
    dh2 = mlp_bwd_dh(f"mlp_bwd_dh_{i}", da, prm["w1"])
    dx1, dy1, dg1, dsh2, dsc2, gr["norm2_g"] = modnorm_bwd(
        f"norm2_bwd_{i}", sv["x1"], dh2, dx, mods, prm["norm2_g"], 3, 4, gate=(sv["y1"], mods, 2),
        after=hook(f"{i}:mlp", gr, dh2))
    dcat = mm_nt(f"proj_out_bwd_{i}", dy1, prm["w_out"], after=hook(f"{i}:mid", gr, dy1))
    t4 = [(2 * j + h, 0) for h in range(2) for j in range(2)]
    go = mm_tn(f"grad_wout_a_{i}", sv["o"], dy1, t4, 128, D).reshape(2, 2, 128, D)
    gm = mm_tn(f"grad_wout_b_{i}", sv["m2"], dy1, t4, 128, D).reshape(2, 2, 128, D)
    gr["w_out"] = jnp.concatenate([go, gm], axis=1)
    if i == 0:
        dq, dkv = attn_bwd("attn_bwd_0", sv["q"], sv["kv"], sv["o"], dcat, sv["lse"], GQA_HEADS)
        dp, gr["gq"], gr["gk"], gr["gs"], gr["sgu_w"], gr["sgu_b"] = even_tok_bwd(
            sv["p"], dq, dkv, dcat, consts["cos_e"], consts["sin_e"], prm["gq"], prm["gk"],
            prm["gs"], prm["sgu_w"], prm["sgu_b"], consts["avg"], consts["masks"])
    else:
        dq, dkv = attn_bwd("attn_bwd_1", sv["q"], sv["kv"], sv["o"], dcat, sv["lse"], MLA_HEADS)
        dz, gr["ln_g"], gr["ln_b"], gr["conv_b"] = conf_bwd_ln(sv["z"], dcat, prm["ln_g"], prm["ln_b"])
        dyc, gr["conv_w"] = conf_bwd_conv(sv["y"], dz, prm["conv_w"])
        dp, gr["gq"], gr["gkv"], gr["wq"], gr["wkk"], gr["wkv"] = odd_tok_bwd(
            sv["p"], dq, dkv, dyc, consts["cos_o"], consts["sin_o"], prm["gq"], prm["gkv"], prm["wq"], prm["wkk"],
            prm["wkv"], consts["spread"])
    n_in = prm["w_in"].shape[1]
    gr["w_in"] = mm_tn(f"grad_win_{i}", sv["h"], dp, [(0, 0), (1, 0)], 512, n_in)
    dh = mm_nt(f"proj_in_bwd_{i}", dp, prm["w_in"])
    if below:
        dx0, dy2b, dg2b, dsh1, dsc1, gr["norm1_g"] = modnorm_bwd(
            f"norm1_bwd_{i}", sv["x0"], dh, dx1, mods, prm["norm1_g"], 0, 1, gate=(below[0], below[1], 5))
        down = (dy2b, dg2b)
    else:
        dx0, dsh1, dsc1, gr["norm1_g"] = modnorm_bwd(f"norm1_bwd_{i}", sv["x0"], dh, dx1, mods, prm["norm1_g"], 0, 1,
                                                     lat_only=True)
        down = None
    dmods = jnp.concatenate([dsh1, dsc1, dg1, dsh2, dsc2, dg2], axis=1)
    return dx0, down, dmods, gr, hook(f"{i}:end", gr, dx0)


def local_step(xcat, target, mods, prms, final_g, hook=lambda point, grads, fresh: None):
    avg, masks = _group_consts()
    cos_e, sin_e = _rope_tables(64, 8)
    ck, sk = _rope_tables(32, 1)
    one64, zero64 = np.ones((SEQ, 64), np.float32), np.zeros((SEQ, 64), np.float32)
    one96, zero96 = np.ones((SEQ, 96), np.float32), np.zeros((SEQ, 96), np.float32)
    cos_o = np.concatenate([np.tile(np.concatenate([one64, ck], axis=1), (1, 8)), ck, one96], axis=1)
    sin_o = np.concatenate([np.tile(np.concatenate([zero64, sk], axis=1), (1, 8)), sk, zero96], axis=1)
    lane = np.arange(768)
    spread = np.zeros((128, 768), np.float32)
    spread[lane % 96 - 64, lane] = (lane % 96 >= 64)
    consts = dict(avg=avg, masks=masks, cos_e=jnp.asarray(cos_e), sin_e=jnp.asarray(sin_e),
                  cos_o=jnp.asarray(cos_o), sin_o=jnp.asarray(sin_o), spread=jnp.asarray(spread, BF))
    x = xcat
    h = modnorm_fwd("norm1_fwd_0", x, mods[0], prms[0]["norm1_g"], 0, 1)
    saved = []
    for i in range(2):
        x, h, sv = _layer_fwd(i, x, h, mods[i], prms[i], consts, (mods[1], prms[1]["norm1_g"]) if i == 0 else None)
        saved.append(sv)
    dx, dy2, dg2, loss, dfg = final_loss(x, target, final_g, saved[1]["y2"], mods[1], 5)
    dmods, grads = [None, None], [None, None]
    entry, down = None, (dy2, dg2)
    for i in (1, 0):
        below = (saved[0]["y2"], mods[0]) if i == 1 else None
        dx, down, dmods[i], grads[i], entry = _layer_bwd(i, dx, down[0], down[1], saved[i], mods[i], prms[i], consts,
                                                         hook, entry, below)
    return loss, dx, dmods, grads, dfg, entry


def _row(v):
    return v.reshape(1, -1).astype(F32)


def odd_in_params(od_w_in, w_uq, w_ukv):
    od = jnp.concatenate([od_w_in[:, 0:416], jnp.zeros((D, 96), od_w_in.dtype), od_w_in[:, 416:OD_IN]], axis=1)
    ukv = w_ukv.reshape(128, 8, 128)
    wkk = jnp.pad(ukv[:, :, :64], ((0, 0), (0, 0), (0, 32))).reshape(128, 768)
    return dict(w_in=od, wq=w_uq, wkk=wkk, wkv=ukv[:, :, 64:].reshape(128, 512))


def small_params(small):
    p0 = dict(norm1_g=_row(small["norm1_g"][0]), norm2_g=_row(small["norm2_g"][0]),
              gq=jnp.tile(_row(small["ev_q_norm_g"]), (1, 8)), gk=jnp.tile(_row(small["ev_k_norm_g"]), (1, 2)),
              gs=_row(small["ev_sgu_norm_g"]), sgu_w=small["ev_sgu_w"].reshape(8, 128, 128).astype(F32),
              sgu_b=small["ev_sgu_b"].reshape(8, 128, 1).astype(F32))
    p1 = dict(norm1_g=_row(small["norm1_g"][1]), norm2_g=_row(small["norm2_g"][1]),
              gq=_row(small["od_q_norm_g"]), gkv=_row(small["od_kv_norm_g"]),
              conv_w=jnp.pad(small["od_conv_w"].reshape(CONV_K, 512).astype(F32), ((0, 1), (0, 0))),
              conv_b=_row(small["od_conv_b"]), ln_g=_row(small["od_ln_g"]), ln_b=_row(small["od_ln_b"]))
    return [p0, p1]


def prep_params(ev_w_in, od_w_in, w_out, w1, w2, w_uq, w_ukv, small):
    p0, p1 = small_params(small)
    p0.update(w_in=ev_w_in, w_out=w_out[0], w1=w1[0], w2=w2[0])
    p1.update(odd_in_params(od_w_in, w_uq, w_ukv), w_out=w_out[1], w1=w1[1], w2=w2[1])
    return [p0, p1]


def small_grads_natural(grads, dfg):
    g0, g1 = grads
    return dict(
        norm1_g=jnp.concatenate([g0["norm1_g"], g1["norm1_g"]], axis=0),
        norm2_g=jnp.concatenate([g0["norm2_g"], g1["norm2_g"]], axis=0),
        ev_q_norm_g=g0["gq"].reshape(8, 64).sum(0).reshape(1, 64),
        ev_k_norm_g=g0["gk"].reshape(2, 64).sum(0).reshape(1, 64),
        ev_sgu_norm_g=g0["gs"].reshape(1, 8, 64),
        ev_sgu_w=g0["sgu_w"].reshape(1, 8, 128, 128),
        ev_sgu_b=g0["sgu_b"].reshape(1, 8, 128),
        od_q_norm_g=g1["gq"].reshape(1, 256),
        od_kv_norm_g=g1["gkv"].reshape(1, 128),
        od_conv_w=g1["conv_w"][0:CONV_K].reshape(1, CONV_K, 512),
        od_conv_b=g1["conv_b"].reshape(1, 512),
        od_ln_g=g1["ln_g"].reshape(1, 512),
        od_ln_b=g1["ln_b"].reshape(1, 512),
        final_g=dfg.reshape(D))


def layer_grads_hs(i, g, part="all"):
    def cols(a):
        k, n = a.shape
        return a.reshape(2, k // 2, 4, n // 4).transpose(0, 2, 1, 3).astype(BF)

    mlp = [(("mlp_w1", i), g["w1"]), (("mlp_w2", i), g["w2"])]
    if part == "mlp":
        return mlp
    rest = [(("w_out", i), g["w_out"])]
    if i == 0:
        rest.append((("ev_w_in", 0), cols(g["w_in"].reshape(D, EV_IN))))
    else:
        od = g["w_in"].reshape(D, OD_PAD)
        od = jnp.concatenate([od[:, 0:416], od[:, 512:OD_PAD]], axis=1)
        ukv = jnp.concatenate([g["wkk"].reshape(128, 8, 96)[:, :, :64], g["wkv"].reshape(128, 8, 64)], axis=2)
        rest += [(("od_w_in", 0), cols(od)), (("od_w_uq", 0), cols(g["wq"])),
                 (("od_w_ukv", 0), cols(ukv.reshape(128, 1024)))]
    return rest if part == "rest" else mlp + rest


def big_grads_hs(grads):
    d = dict(layer_grads_hs(0, grads[0]) + layer_grads_hs(1, grads[1]))
    return dict(ev_w_in=d[("ev_w_in", 0)], od_w_in=d[("od_w_in", 0)], od_w_uq=d[("od_w_uq", 0)],
                od_w_ukv=d[("od_w_ukv", 0)], w_out=[d[("w_out", 0)], d[("w_out", 1)]],
                mlp_w1=[d[("mlp_w1", 0)], d[("mlp_w1", 1)]], mlp_w2=[d[("mlp_w2", 0)], d[("mlp_w2", 1)]])


def grads_to_natural(grads, dfg):
    out = small_grads_natural(grads, dfg)
    hs = big_grads_hs(grads)

    def from_cols(a):
        return a.transpose(0, 2, 1, 3).reshape(2 * a.shape[2], 4 * a.shape[3])

    def from_rows(a):
        return a.transpose(1, 0, 2, 3).reshape(8 * a.shape[2], a.shape[3])

    out["ev_w_in"] = from_cols(hs["ev_w_in"])[None]
    out["od_w_in"] = from_cols(hs["od_w_in"])[None]
    out["od_w_uq"] = from_cols(hs["od_w_uq"])[None]
    out["od_w_ukv"] = from_cols(hs["od_w_ukv"])[None]
    out["w_out"] = jnp.stack([from_rows(a) for a in hs["w_out"]])
    out["mlp_w1"] = jnp.stack([from_cols(a) for a in hs["mlp_w1"]])
    out["mlp_w2"] = jnp.stack([from_rows(a) for a in hs["mlp_w2"]])
    return out


WEIGHT_NAMES = ['c_ctx', 'ada_w', 'ada_b', 'norm1_g', 'norm2_g', 'w_out', 'mlp_w1', 'mlp_w2', 'ev_w_in',
                'ev_q_norm_g', 'ev_k_norm_g', 'ev_sgu_norm_g', 'ev_sgu_w', 'ev_sgu_b', 'od_w_in', 'od_q_norm_g',
                'od_kv_norm_g', 'od_w_uq', 'od_w_ukv', 'od_conv_w', 'od_conv_b', 'od_ln_g', 'od_ln_b', 'final_g']
REPL_SMALL = ['norm1_g', 'norm2_g', 'ev_q_norm_g', 'ev_k_norm_g', 'ev_sgu_norm_g', 'ev_sgu_w', 'ev_sgu_b',
              'od_kv_norm_g', 'final_g']
SHARD_SMALL = ['od_q_norm_g', 'od_conv_w', 'od_conv_b', 'od_ln_g', 'od_ln_b']
BIG = ['w_out', 'mlp_w1', 'mlp_w2', 'ev_w_in', 'od_w_in', 'od_w_uq', 'od_w_ukv']


def _gather_last(parts):
    return jnp.concatenate([parts[k] for k in range(4)], axis=-1)


class _Reduce:
    def __init__(self, tag, named, half, where):
        self.tag, self.half, self.where = tag, half, where
        self.names, self.hs = zip(*named)
        self.hs = list(self.hs)

    def to_sibling(self):
        lands = [lax.empty(a.shape[1:], BF) for a in self.hs]
        (self.h1,), token = exchange_start(f"rs_sibling_start_{self.tag}", [(self.hs, lands)], TO_SIBLING)
        return token

    def to_chips(self, after):
        hs, got = exchange_wait(f"rs_sibling_wait_{self.tag}", self.h1, after, TO_SIBLING)
        pair = [add_pairs(f"rs_add_{self.tag}_{k}", a, b, self.half) for k, (a, b) in enumerate(zip(hs, got))]
        lands = [lax.empty(p.shape, BF) for p in pair]
        (self.h2,), token = exchange_start(f"rs_chips_start_{self.tag}", [(pair, lands)], SCATTER)
        return token

    def finish(self, after, bufs):
        pair, land = exchange_wait(f"rs_chips_wait_{self.tag}", self.h2, after, SCATTER)
        for k, ((n, idx), l, p) in enumerate(zip(self.names, land, pair)):
            bufs[n] = sum_slabs(f"rs_sum_{self.tag}_{k}", l, p, self.where, bufs[n], idx)


def kernel(x, c, ctx, c_ctx, ada_w, ada_b, norm1_g, norm2_g, w_out, mlp_w1, mlp_w2, ev_w_in, ev_q_norm_g, ev_k_norm_g, ev_sgu_norm_g, ev_sgu_w, ev_sgu_b, od_w_in, od_q_norm_g, od_kv_norm_g, od_w_uq, od_w_ukv, od_conv_w, od_conv_b, od_ln_g, od_ln_b, final_g, loss_target, m_c_ctx, m_ada_w, m_ada_b, m_norm1_g, m_norm2_g, m_w_out, m_mlp_w1, m_mlp_w2, m_ev_w_in, m_ev_q_norm_g, m_ev_k_norm_g, m_ev_sgu_norm_g, m_ev_sgu_w, m_ev_sgu_b, m_od_w_in, m_od_q_norm_g, m_od_kv_norm_g, m_od_w_uq, m_od_w_ukv, m_od_conv_w, m_od_conv_b, m_od_ln_g, m_od_ln_b, m_final_g, v_c_ctx, v_ada_w, v_ada_b, v_norm1_g, v_norm2_g, v_w_out, v_mlp_w1, v_mlp_w2, v_ev_w_in, v_ev_q_norm_g, v_ev_k_norm_g, v_ev_sgu_norm_g, v_ev_sgu_w, v_ev_sgu_b, v_od_w_in, v_od_q_norm_g, v_od_kv_norm_g, v_od_w_uq, v_od_w_ukv, v_od_conv_w, v_od_conv_b, v_od_ln_g, v_od_ln_b, v_final_g):
    w = dict(c_ctx=c_ctx, ada_w=ada_w, ada_b=ada_b, norm1_g=norm1_g, norm2_g=norm2_g, w_out=w_out, mlp_w1=mlp_w1,
             mlp_w2=mlp_w2, ev_w_in=ev_w_in, ev_q_norm_g=ev_q_norm_g, ev_k_norm_g=ev_k_norm_g,
             ev_sgu_norm_g=ev_sgu_norm_g, ev_sgu_w=ev_sgu_w, ev_sgu_b=ev_sgu_b, od_w_in=od_w_in,
             od_q_norm_g=od_q_norm_g, od_kv_norm_g=od_kv_norm_g, od_w_uq=od_w_uq, od_w_ukv=od_w_ukv,
             od_conv_w=od_conv_w, od_conv_b=od_conv_b, od_ln_g=od_ln_g, od_ln_b=od_ln_b, final_g=final_g)
    mom = dict(c_ctx=m_c_ctx, ada_w=m_ada_w, ada_b=m_ada_b, norm1_g=m_norm1_g, norm2_g=m_norm2_g, w_out=m_w_out,
               mlp_w1=m_mlp_w1, mlp_w2=m_mlp_w2, ev_w_in=m_ev_w_in, ev_q_norm_g=m_ev_q_norm_g,
               ev_k_norm_g=m_ev_k_norm_g, ev_sgu_norm_g=m_ev_sgu_norm_g, ev_sgu_w=m_ev_sgu_w, ev_sgu_b=m_ev_sgu_b,
               od_w_in=m_od_w_in, od_q_norm_g=m_od_q_norm_g, od_kv_norm_g=m_od_kv_norm_g, od_w_uq=m_od_w_uq,
               od_w_ukv=m_od_w_ukv, od_conv_w=m_od_conv_w, od_conv_b=m_od_conv_b, od_ln_g=m_od_ln_g,
               od_ln_b=m_od_ln_b, final_g=m_final_g)
    var = dict(c_ctx=v_c_ctx, ada_w=v_ada_w, ada_b=v_ada_b, norm1_g=v_norm1_g, norm2_g=v_norm2_g, w_out=v_w_out,
               mlp_w1=v_mlp_w1, mlp_w2=v_mlp_w2, ev_w_in=v_ev_w_in, ev_q_norm_g=v_ev_q_norm_g,
               ev_k_norm_g=v_ev_k_norm_g, ev_sgu_norm_g=v_ev_sgu_norm_g, ev_sgu_w=v_ev_sgu_w, ev_sgu_b=v_ev_sgu_b,
               od_w_in=v_od_w_in, od_q_norm_g=v_od_q_norm_g, od_kv_norm_g=v_od_kv_norm_g, od_w_uq=v_od_w_uq,
               od_w_ukv=v_od_w_ukv, od_conv_w=v_od_conv_w, od_conv_b=v_od_conv_b, od_ln_g=v_od_ln_g,
               od_ln_b=v_od_ln_b, final_g=v_final_g)
    xi, yi, ci = lax.axis_index("x"), lax.axis_index("y"), lax.axis_index("c")
    chip = 2 * xi + yi
    dev = 2 * chip + ci

    shard_shapes = [w[n].shape for n in SHARD_SMALL]
    g0 = all_gather8("ag_small", _pack([c] + [w[n] for n in SHARD_SMALL]))
    g0 = g0.reshape(8, -1, D)
    parts = _unpack(g0, [c.shape] + shard_shapes)
    c_all = parts[0].reshape(16, D)
    small_full = {n: _gather_last(p[0::2]) for n, p in zip(SHARD_SMALL, parts[1:])}
    call = jnp.concatenate([c_all, c_ctx.reshape(1, D), jnp.zeros((NC - 17, D), F32)], axis=0)

    cols = ada_w.shape[2]
    ada_b_sh = lax.dynamic_slice(ada_b, (0, chip * cols), (2, cols)).reshape(2, 1, cols)
    mt = mods_fwd(call, ada_w, ada_b_sh)
    mt = all_gather8("ag_mods", mt.reshape(2 * NC, cols)).reshape(8, 2, NC, cols)
    table = mt[0::2].transpose(1, 2, 0, 3).reshape(2, NC, 4 * cols)
    mods = []
    for i in range(2):
        lat = lax.dynamic_slice(table[i], (2 * dev, 0), (2, 4 * cols))
        mc = table[i, 16]
        mods.append(jnp.stack([mc, lat[0], mc, lat[1]]).reshape(4 * N_MOD, 1, D))

    order = [[("ev_w_in", 0)], [("w_out", 0), ("mlp_w1", 0), ("mlp_w2", 0)],
             [("od_w_in", 0), ("od_w_uq", 0), ("od_w_ukv", 0), ("w_out", 1)], [("mlp_w1", 1), ("mlp_w2", 1)]]
    groups = []
    for names in order:
        srcs = [w[n][i].astype(BF) for n, i in names]
        groups.append((srcs, [lax.empty((4,) + s.shape, BF) for s in srcs]))
    groups[0][0][0], table = lax.optimization_barrier((groups[0][0][0], table))
    handles, token = exchange_start("gather_start", groups, GATHER)
    mods[0] = mods[0] + token[0, 0]
    small = {n: w[n] for n in REPL_SMALL}
    small.update(small_full)
    prms = small_params(small)

    chip1 = chip.reshape(1).astype(jnp.int32)

    def arrived(k, after):
        srcs, lands = exchange_wait(f"gather_wait_{k}", handles[k], after, GATHER)
        return [place_own(f"gather_own_{k}_{a}", l, s, chip1) for a, (l, s) in enumerate(zip(lands, srcs))]

    def arrive_ev_in(after):
        (ev,) = arrived(0, after)
        prms[0]["w_in"] = _gather_last(ev)

    def arrive_ev_rest(after):
        wo, w1, w2 = arrived(1, after)
        prms[0].update(w_out=wo.reshape(D, D), w1=w1, w2=w2)

    def arrive_od(after):
        od, uq, ukv, wo = arrived(2, after)
        prms[1].update(odd_in_params(_gather_last(od), _gather_last(uq), _gather_last(ukv)), w_out=wo.reshape(D, D))

    def arrive_od_mlp(after):
        w1, w2 = arrived(3, after)
        prms[1].update(w1=w1, w2=w2)

    prms[0]["w_in"] = arrive_ev_in
    prms[0]["w_out"] = arrive_ev_rest
    prms[1]["w_in"] = arrive_od
    prms[1]["w1"] = arrive_od_mlp

    half = ci.reshape(1).astype(jnp.int32)
    where = jnp.stack([chip, ci]).astype(jnp.int32)
    red = {}

    def hook(point, g, fresh):
        if point == "1:end":
            red["l1"] = _Reduce("l1", layer_grads_hs(1, g, "all"), half, where)
            return red["l1"].to_sibling()
        if point == "0:mlp":
            red["l0_mlp"] = _Reduce("l0_mlp", layer_grads_hs(0, g, "mlp"), half, where)
            return red["l1"].to_chips(fresh) + red["l0_mlp"].to_sibling()
        if point == "0:mid":
            return red["l0_mlp"].to_chips(fresh)
        if point == "0:end":
            red["l0_rest"] = _Reduce("l0_rest", layer_grads_hs(0, g, "rest"), half, where)
            return red["l0_rest"].to_sibling()
        return None

    xcat = jnp.concatenate([ctx, x], axis=1).reshape(R, D)
    loss_p, dx, dmods, grads, dfg, last = local_step(xcat, loss_target.reshape(NEX * L, D), mods, prms,
                                                     final_g.reshape(1, D), hook)
    grad_x = dx.reshape(NEX, L, D)

    sg = small_grads_natural(grads, dfg)
    dm = jnp.stack([d.reshape(4, N_MOD * D) for d in dmods])
    small_names = REPL_SMALL + SHARD_SMALL
    items = [dm[:, 1::2], dm[:, 0] + dm[:, 2]] + [sg[n] for n in small_names] + [loss_p[0:1, 0:1]]
    shapes = [a.shape for a in items]
    g1 = all_gather8("ag_grads", _pack(items), after=last)
    started = red["l0_rest"].to_chips(g1)
    rows1 = g1.shape[0] // 8
    g1 = g1.reshape(8, rows1, D)
    tot = _unpack(sum_lead("sum_small", g1, after=started), shapes)
    dm_lat = _unpack(g1, shapes[:1])[0]
    dm_lat = dm_lat.transpose(1, 0, 2, 3).reshape(2, 16, N_MOD * D)
    dm_all = jnp.concatenate([dm_lat, tot[1][:, None], jnp.zeros((2, NC - 17, N_MOD * D), F32)], axis=1)
    gsum = dict(zip(small_names, tot[2:2 + len(small_names)]))
    loss = tot[-1].reshape(())
    grad = {n: gsum[n].reshape(w[n].shape) for n in REPL_SMALL}
    for n in SHARD_SMALL:
        k = w[n].shape[-1]
        grad[n] = lax.dynamic_slice_in_dim(gsum[n], chip * k, k, axis=gsum[n].ndim - 1)
    grad["ada_b"] = sum_lead("sum_ada_b", dm_all.transpose(1, 0, 2).reshape(NC, 2 * N_MOD, D)).reshape(2, N_MOD * D)

    dm_sh = lax.dynamic_slice(dm_all, (0, 0, chip * cols), (2, NC, cols))
    grad["ada_w"], dcc = ada_bwd(call, ada_w, dm_sh)
    dcc = all_gather8("ag_cctx", dcc).reshape(8, 8, D)
    grad["c_ctx"] = sum_lead("sum_cctx", dcc[0::2])[0]

    delta, new_m, new_v = {}, {}, {}

    def adam_big(n, again):
        shp = w[n].shape
        two_d = (shp[0] * shp[1], shp[2])
        res = adamw(f"adamw_{n}", w[n].reshape(two_d), grad[n].reshape(two_d), mom[n].reshape(two_d),
                    var[n].reshape(two_d), again)
        delta[n], new_m[n], new_v[n] = [a.reshape(shp) for a in res[:3]]
        if again:
            grad[n] = res[3].reshape(shp)

    adam_big('ada_w', False)
    rest = [n for n in WEIGHT_NAMES if n not in ['ada_w'] + BIG]
    rshapes = [w[n].shape for n in rest]
    d_, m_, v_ = adamw("adamw_small", _pack([w[n] for n in rest]), _pack([grad[n] for n in rest]),
                       _pack([mom[n] for n in rest]), _pack([var[n] for n in rest]))
    for dst, buf in ((delta, d_), (new_m, m_), (new_v, v_)):
        dst.update(zip(rest, _unpack(buf, rshapes)))

    bufs = {n: lax.empty((w[n].shape[0], 2, w[n].shape[1] // 2, w[n].shape[2]), F32) for n in BIG}
    for tag, behind in (("l1", delta["ada_w"]), ("l0_mlp", d_), ("l0_rest", d_)):
        red[tag].finish(behind, bufs)
    for n, full in zip(BIG, sibling_merge("rs_sibling_merge", [bufs[n] for n in BIG])):
        grad[n] = full.reshape(w[n].shape)
    for n in BIG:
        adam_big(n, True)

    return (loss, grad_x, *[grad[n] for n in WEIGHT_NAMES], *[delta[n] for n in WEIGHT_NAMES],
            *[new_m[n] for n in WEIGHT_NAMES], *[new_v[n] for n in WEIGHT_NAMES])
```

```python
import functools
import math

import numpy as np
import jax
import jax.numpy as jnp
from jax import lax
from jax.experimental import pallas as pl
from jax.experimental.pallas import tpu as pltpu

F32 = jnp.float32
BF = jnp.bfloat16
HI = lax.Precision.HIGHEST
MESH = pl.DeviceIdType.MESH

D = 1024
L = 2048
LC = 256
SEQ = L + LC
NEX = 2
R = NEX * SEQ
TB = 256
WIDE = 512
BPE = SEQ // TB
NBLK = R // TB
GRID_W = 64
FF = 4 * D
EPS = 1e-6
ROPE_THETA = 10000.0
N_MOD = 6
EV_IN = 1792
OD_IN = 1440
OD_PAD = 1536
VMEM_LIMIT = 60 * 1024 * 1024

ADAM_LR = 0.001
ADAM_B1 = 0.9
ADAM_B2 = 0.999
ADAM_EPS = 1e-08
ADAM_WD = 0.01
ADAM_STEP = 10

NT = (((1,), (1,)), ((), ()))
TN = (((0,), (0,)), ((), ()))


def _cparams(sem=None):
    return pltpu.CompilerParams(dimension_semantics=sem, vmem_limit_bytes=VMEM_LIMIT)


@jax.custom_vjp
def _mm(a, b):
    return jnp.dot(a.astype(BF), b.astype(BF), preferred_element_type=F32)


def _mm_fwd(a, b):
    return _mm(a, b), (a, b)


def _mm_bwd(res, g):
    a, b = res
    gb = g.astype(BF)
    da = lax.dot_general(gb, b.astype(BF), NT, preferred_element_type=F32)
    db = lax.dot_general(a.astype(BF), gb, TN, preferred_element_type=F32)
    return da, db


_mm.defvjp(_mm_fwd, _mm_bwd)


@jax.custom_vjp
def _swap(x):
    n = x.shape[-1]
    ax = x.ndim - 1
    lane = lax.broadcasted_iota(jnp.int32, x.shape, ax)
    return jnp.where(lane % 2 == 0, pltpu.roll(x, n - 1, ax), pltpu.roll(x, 1, ax))


_swap.defvjp(lambda x: (_swap(x), None), lambda _, g: (_swap(g),))


def _rope(x, cos, sin):
    return x * cos + _swap(x) * sin


def _rmsn(x, g):
    return x * lax.rsqrt(jnp.mean(x * x, axis=-1, keepdims=True) + EPS) * g


def _split_dot(a, m):
    hi = a.astype(BF)
    lo = (a - hi.astype(F32)).astype(BF)
    return jnp.dot(hi, m, preferred_element_type=F32) + jnp.dot(lo, m, preferred_element_type=F32)


@jax.custom_vjp
def _group_mean(a, avg):
    return _split_dot(a, avg)


_group_mean.defvjp(lambda a, avg: (_split_dot(a, avg), avg),
                   lambda avg, g: (_split_dot(g, avg), jnp.zeros_like(avg)))


def _grmsn(x, g, avg):
    return x * lax.rsqrt(_group_mean(x * x, avg) + EPS) * g


def _modnorm(x, g, sh, sc):
    return _rmsn(x, g) * (1.0 + sc) + sh


def _gelu(x):
    return 0.5 * x * (1.0 + jnp.tanh(0.7978845608028654 * (x + 0.044715 * (x * x * x))))


def _silu(x):
    return x * jax.nn.sigmoid(x)


def _acc(ref, val, first):
    @pl.when(first)
    def _():
        ref[...] = val

    @pl.when(jnp.logical_not(first))
    def _():
        ref[...] += val


def _seg(i):
    return 2 * (i // BPE) + jnp.minimum(i % BPE, 1)


def _seg_first(i):
    return (i % BPE) <= 1


def _rb_call(name, body, row_in=(), mod_in=(), pos_in=(), full_in=(), shift_in=(),
             row_out=(), seg_out=(), acc_out=(), scratch=(), after=None, col_in=(), rows=TB):
    assert rows == TB or not (mod_in or pos_in or shift_in or seg_out or col_in)
    in_specs, args = [], []
    for a in row_in:
        in_specs.append(pl.BlockSpec((rows, a.shape[1]), lambda i: (i, 0)))
        args.append(a)
    for a in col_in:
        in_specs.append(pl.BlockSpec((a.shape[0], TB), lambda i: (0, i)))
        args.append(a)
    for tab, m in mod_in:
        in_specs.append(pl.BlockSpec((1, 1, D), lambda i, m=m: (_seg(i) * N_MOD + m, 0, 0)))
        args.append(tab)
    for a in pos_in:
        in_specs.append(pl.BlockSpec((TB, a.shape[1]), lambda i: (i % BPE, 0)))
        args.append(a)
    for a in full_in:
        in_specs.append(pl.BlockSpec(a.shape, lambda i, n=a.ndim: (0,) * n))
        args.append(a)
    for a, d in shift_in:
        in_specs.append(pl.BlockSpec((TB, a.shape[1]), lambda i, d=d: (jnp.clip(i + d, 0, NBLK - 1), 0)))
        args.append(a)
    n_in = len(args)
    if after is not None:
        in_specs.append(pl.BlockSpec(after.shape, lambda i, n=after.ndim: (0,) * n))
        args.append(after)
    out_specs, out_shape = [], []
    for w, dt, *lat in row_out:
        if lat:
            out_specs.append(pl.BlockSpec(
                (TB, w), lambda i: ((i // BPE) * (L // TB) + jnp.maximum(i % BPE - 1, 0), 0)))
            out_shape.append(jax.ShapeDtypeStruct((NEX * L, w), dt))
        else:
            out_specs.append(pl.BlockSpec((rows, w), lambda i: (i, 0)))
            out_shape.append(jax.ShapeDtypeStruct((R, w), dt))
    for w in seg_out:
        out_specs.append(pl.BlockSpec((1, 1, w), lambda i: (_seg(i), 0, 0)))
        out_shape.append(jax.ShapeDtypeStruct((4, 1, w), F32))
    for shp in acc_out:
        out_specs.append(pl.BlockSpec(shp, lambda i, n=len(shp): (0,) * n))
        out_shape.append(jax.ShapeDtypeStruct(shp, F32))

    def kern(*refs):
        body(pl.program_id(0), *refs[:n_in], *refs[len(args):])

    sem = ("arbitrary",) if (seg_out or acc_out or any(len(r) > 2 for r in row_out)) else ("parallel",)
    return pl.pallas_call(kern, grid=(R // rows,), in_specs=in_specs, out_specs=out_specs, out_shape=out_shape,
                          scratch_shapes=list(scratch), compiler_params=_cparams(sem), name=name)(*args)


def modnorm_fwd(name, x, mods, g, m_sh, m_sc):
    def body(i, x_ref, sh_ref, sc_ref, g_ref, h_ref):
        h_ref[...] = _modnorm(x_ref[...], g_ref[...], sh_ref[0], sc_ref[0]).astype(BF)

    return _rb_call(name, body, row_in=(x,), mod_in=((mods, m_sh), (mods, m_sc)), full_in=(g,),
                    row_out=((D, BF),))[0]


def _gate_grads(dx, y_ref, gt_ref, dy_ref, dgt_ref, i):
    dy_ref[...] = (dx * gt_ref[0]).astype(BF)
    _acc(dgt_ref, jnp.sum(dx * y_ref[...].astype(F32), axis=0, keepdims=True)[None], _seg_first(i))


def modnorm_bwd(name, x, dh, dx_in, mods, g, m_sh, m_sc, gate=None, after=None, lat_only=False):
    def body(i, x_ref, dh_ref, dxin_ref, *rest):
        if gate:
            y_ref, sh_ref, sc_ref, gt_ref, g_ref, dx_ref, dy_ref, dgt_ref, dsh_ref, dsc_ref, dg_ref = rest
        else:
            sh_ref, sc_ref, g_ref, dx_ref, dsh_ref, dsc_ref, dg_ref = rest
        _, vjp = jax.vjp(_modnorm, x_ref[...], g_ref[...], sh_ref[0], sc_ref[0])
        dx, dg, dsh, dsc = vjp(dh_ref[...].astype(F32))
        dx = dxin_ref[...] + dx
        dx_ref[...] = dx
        if gate:
            _gate_grads(dx, y_ref, gt_ref, dy_ref, dgt_ref, i)
        _acc(dsh_ref, dsh[None], _seg_first(i))
        _acc(dsc_ref, dsc[None], _seg_first(i))
        _acc(dg_ref, dg, i == 0)

    if gate:
        y, gmods, m = gate
        return _rb_call(name, body, row_in=(x, dh, dx_in, y), mod_in=((mods, m_sh), (mods, m_sc), (gmods, m)),
                        full_in=(g,), row_out=((D, F32), (D, BF)), seg_out=(D, D, D), acc_out=((1, D),), after=after)
    return _rb_call(name, body, row_in=(x, dh, dx_in), mod_in=((mods, m_sh), (mods, m_sc)), full_in=(g,),
                    row_out=((D, F32, "lat") if lat_only else (D, F32),), seg_out=(D, D), acc_out=((1, D),),
                    after=after)


def proj_in(name, h, w):
    n = w.shape[1]

    def body(i, h_ref, w_ref, o_ref):
        o_ref[...] = jnp.dot(h_ref[...], w_ref[...], preferred_element_type=F32).astype(BF)

    return _rb_call(name, body, row_in=(h,), full_in=(w,), row_out=((n, BF),), rows=WIDE)[0]


def proj_out(name, a1, a2, w, x, mods, m_gate, g_next, m_sh, m_sc):
    k1 = a1.shape[1]

    def body(i, a1_ref, a2_ref, x_ref, gt_ref, sh_ref, sc_ref, w_ref, g_ref, xo_ref, y_ref, h_ref):
        y = jnp.dot(a1_ref[...], w_ref[:k1, :], preferred_element_type=F32)
        y = y + jnp.dot(a2_ref[...], w_ref[k1:, :], preferred_element_type=F32)
        y_ref[...] = y.astype(BF)
        xn = x_ref[...] + gt_ref[0] * y
        xo_ref[...] = xn
        h_ref[...] = _modnorm(xn, g_ref[...], sh_ref[0], sc_ref[0]).astype(BF)

    return _rb_call(name, body, row_in=(a1, a2, x), mod_in=((mods, m_gate), (mods, m_sh), (mods, m_sc)),
                    full_in=(w, g_next), row_out=((D, F32), (D, BF), (D, BF)))


def mlp_up(name, h, w1):
    def body(i, h_ref, w_ref, a_ref, f_ref):
        hv = h_ref[...]
        for n in range(4):
            a = jnp.dot(hv, w_ref[n], preferred_element_type=F32)
            a_ref[:, n * D:(n + 1) * D] = a.astype(BF)
            r = jnp.maximum(a, 0.0)
            f_ref[:, n * D:(n + 1) * D] = (r * r).astype(BF)

    return _rb_call(name, body, row_in=(h,), full_in=(w1,), row_out=((FF, BF), (FF, BF)), rows=WIDE)


def mlp_down(name, f, w2, x, mods, m_gate, nxt=None):
    def body(i, f_ref, x_ref, gt_ref, *rest):
        if nxt:
            sh_ref, sc_ref, w_ref, g_ref, xo_ref, y_ref, h_ref = rest
        else:
            w_ref, xo_ref, y_ref = rest
        y = jnp.dot(f_ref[:, 0:D], w_ref[0], preferred_element_type=F32)
        for n in range(1, 4):
            y = y + jnp.dot(f_ref[:, n * D:(n + 1) * D], w_ref[n], preferred_element_type=F32)
        xn = x_ref[...] + gt_ref[0] * y
        y_ref[...] = y.astype(BF)
        xo_ref[...] = xn
        if nxt:
            h_ref[...] = _modnorm(xn, g_ref[...], sh_ref[0], sc_ref[0]).astype(BF)

    if nxt:
        return _rb_call(name, body, row_in=(f, x), mod_in=((mods, m_gate), (nxt[0], 0), (nxt[0], 1)),
                        full_in=(w2, nxt[1]), row_out=((D, F32), (D, BF), (D, BF)))
    return _rb_call(name, body, row_in=(f, x), mod_in=((mods, m_gate),), full_in=(w2,),
                    row_out=((D, F32), (D, BF)))


def mm_nt(name, g, w, after=None):
    k = w.shape[0]

    def body(i, g_ref, w_ref, o_ref):
        o_ref[...] = lax.dot_general(g_ref[...], w_ref[...], NT, preferred_element_type=F32).astype(BF)

    return _rb_call(name, body, row_in=(g,), full_in=(w,), row_out=((k, BF),), after=after, rows=WIDE)[0]


def mlp_bwd_da(name, dy, w2, a, after=None):
    def body(i, dy_ref, a_ref, w_ref, da_ref):
        dyv = dy_ref[...]
        for n in range(4):
            df = lax.dot_general(dyv, w_ref[n], NT, preferred_element_type=F32)
            av = a_ref[:, n * D:(n + 1) * D].astype(F32)
            da_ref[:, n * D:(n + 1) * D] = (df * (2.0 * jnp.maximum(av, 0.0))).astype(BF)

    return _rb_call(name, body, row_in=(dy, a), full_in=(w2,), row_out=((FF, BF),), after=after, rows=WIDE)[0]


def mlp_bwd_dh(name, da, w1):
    def body(i, da_ref, w_ref, dh_ref):
        acc = lax.dot_general(da_ref[:, 0:D], w_ref[0], NT, preferred_element_type=F32)
        for n in range(1, 4):
            acc = acc + lax.dot_general(da_ref[:, n * D:(n + 1) * D], w_ref[n], NT, preferred_element_type=F32)
        dh_ref[...] = acc.astype(BF)

    return _rb_call(name, body, row_in=(da,), full_in=(w1,), row_out=((D, BF),), rows=WIDE)[0]


TN_ROWS = 1536


def mm_tn(name, a, g, tiles, th, tw):
    nt = len(tiles)
    acs = jnp.asarray([t[0] for t in tiles], jnp.int32)
    gcs = jnp.asarray([t[1] for t in tiles], jnp.int32)
    nr = R // TN_ROWS

    def kern(ac_ref, gc_ref, a_ref, g_ref, o_ref, acc_ref):
        r = pl.program_id(1)

        @pl.when(r == 0)
        def _():
            acc_ref[...] = jnp.zeros_like(acc_ref)

        acc_ref[...] += lax.dot_general(a_ref[...], g_ref[...], TN, preferred_element_type=F32)

        @pl.when(r == nr - 1)
        def _():
            o_ref[...] = acc_ref[...].astype(BF)

    grid_spec = pltpu.PrefetchScalarGridSpec(
        num_scalar_prefetch=2, grid=(nt, nr),
        in_specs=[pl.BlockSpec((TN_ROWS, th), lambda t, r, ac, gc: (r, ac[t])),
                  pl.BlockSpec((TN_ROWS, tw), lambda t, r, ac, gc: (r, gc[t]))],
        out_specs=pl.BlockSpec((None, th, tw), lambda t, r, ac, gc: (t, 0, 0)),
        scratch_shapes=[pltpu.VMEM((th, tw), F32)])
    return pl.pallas_call(kern, grid_spec=grid_spec, out_shape=jax.ShapeDtypeStruct((nt, th, tw), BF),
                          compiler_params=_cparams(("parallel", "arbitrary")), name=name)(acs, gcs, a, g)


def _even_tok(q, k, zu, zv, gq, gk, gs, ws, bs, cq, sq, ck, sk, avg, masks):
    qr = _rope(_grmsn(q, gq, avg), cq, sq) * GQA_SCALE
    kr = _rope(_grmsn(k, gk, avg[:128, :128]), ck, sk)
    u = _gelu(zu)
    v = _grmsn(_gelu(zv), gs, avg)
    sv = None
    for g in range(8):
        t = masks[g] * (_mm(ws[g], v) + bs[g])
        sv = t if sv is None else sv + t
    return qr, kr, u * sv


def even_tok_fwd(p, cos, sin, gq, gk, gs, sgu_w, sgu_b, avg, masks):
    def body(i, p_ref, cos_ref, sin_ref, gq_ref, gk_ref, gs_ref, w_ref, b_ref, avg_ref, mk_ref, q_ref, kv_ref, m_ref):
        avgv = avg_ref[...]
        ws = [w_ref[g] for g in range(8)]
        bs = [b_ref[g] for g in range(8)]
        mks = [mk_ref[g] for g in range(8)]
        for c in range(2):
            rs = pl.ds(c * 128, 128)
            qr, kr, m = _even_tok(
                p_ref[rs, 0:512].astype(F32), p_ref[rs, 512:640].astype(F32),
                p_ref[rs, 768:1280].astype(F32), p_ref[rs, 1280:1792].astype(F32),
                gq_ref[...], gk_ref[...], gs_ref[...], ws, bs,
                cos_ref[rs, :], sin_ref[rs, :], cos_ref[rs, 0:128], sin_ref[rs, 0:128], avgv, mks)
            q_ref[rs, :] = qr.astype(BF)
            kv_ref[rs, 0:128] = kr.astype(BF)
            kv_ref[rs, 128:256] = p_ref[rs, 640:768]
            m_ref[rs, :] = m.astype(BF)

    return _rb_call("even_tok_fwd", body, row_in=(p,), pos_in=(cos, sin),
                    full_in=(gq, gk, gs, sgu_w, sgu_b, avg, masks), row_out=((512, BF), (256, BF), (512, BF)))


def even_tok_bwd(p, dq, dkvt, dcat, cos, sin, gq, gk, gs, sgu_w, sgu_b, avg, masks):
    def body(i, p_ref, dq_ref, dcat_ref, dkvt_ref, cos_ref, sin_ref, gq_ref, gk_ref, gs_ref, w_ref, b_ref,
             avg_ref, mk_ref, dp_ref, dgq_ref, dgk_ref, dgs_ref, dw_ref, db_ref):
        avgv = avg_ref[...]
        ws = [w_ref[g] for g in range(8)]
        bs = [b_ref[g] for g in range(8)]
        mks = [mk_ref[g] for g in range(8)]
        tot = None
        for c in range(2):
            rs = pl.ds(c * 128, 128)
            cq, sq, ck, sk = cos_ref[rs, :], sin_ref[rs, :], cos_ref[rs, 0:128], sin_ref[rs, 0:128]

            def f(q, k, zu, zv, gq, gk, gs, ws, bs):
                return _even_tok(q, k, zu, zv, gq, gk, gs, ws, bs, cq, sq, ck, sk, avgv, mks)

            _, vjp = jax.vjp(f, p_ref[rs, 0:512].astype(F32), p_ref[rs, 512:640].astype(F32),
                             p_ref[rs, 768:1280].astype(F32), p_ref[rs, 1280:1792].astype(F32),
                             gq_ref[...], gk_ref[...], gs_ref[...], ws, bs)
            dk = dkvt_ref[0:128, c * 128:(c + 1) * 128].T
            dv = dkvt_ref[128:256, c * 128:(c + 1) * 128].T
            d = vjp((dq_ref[rs, :].astype(F32), dk, dcat_ref[rs, 512:1024].astype(F32)))
            dp_ref[rs, 0:512] = d[0].astype(BF)
            dp_ref[rs, 512:640] = d[1].astype(BF)
            dp_ref[rs, 640:768] = dv.astype(BF)
            dp_ref[rs, 768:1280] = d[2].astype(BF)
            dp_ref[rs, 1280:1792] = d[3].astype(BF)
            part = [d[4], d[5], d[6]] + list(d[7]) + list(d[8])
            tot = part if tot is None else [a + b for a, b in zip(tot, part)]
        refs = [dgq_ref, dgk_ref, dgs_ref] + [dw_ref.at[g] for g in range(8)] + [db_ref.at[g] for g in range(8)]
        for ref, val in zip(refs, tot):
            _acc(ref, val, i == 0)

    return _rb_call("even_tok_bwd", body, row_in=(p, dq, dcat), col_in=(dkvt,), pos_in=(cos, sin),
                    full_in=(gq, gk, gs, sgu_w, sgu_b, avg, masks), row_out=((EV_IN, BF),),
                    acc_out=((1, 512), (1, 128), (1, 512), (8, 128, 128), (8, 128, 1)))


MLA_SCALE = 96 ** -0.5
GQA_SCALE = 64 ** -0.5


def _odd_tok(cq, ckv, kr, za, zg, gq, gkv, wq, wkk, wkv, spread, cr, sr, ck, sk):
    cqn = _rmsn(cq, gq)
    q = _rope(_mm(cqn, wq), cr, sr) * MLA_SCALE
    ckn = _rmsn(ckv, gkv)
    k = _mm(ckn, wkk) + _mm(_rope(kr, ck, sk), spread)
    v = _mm(ckn, wkv)
    y = za * jax.nn.sigmoid(zg)
    return q, k, v, y


def odd_tok_fwd(p, cos, sin, gq, gkv, wq, wkk, wkv, spread):
    def body(i, p_ref, cos_ref, sin_ref, gq_ref, gkv_ref, wq_ref, wkk_ref, wkv_ref, sp_ref, q_ref, kv_ref, y_ref):
        q, k, v, y = _odd_tok(
            p_ref[:, 0:256].astype(F32), p_ref[:, 256:384].astype(F32), p_ref[:, 384:512].astype(F32),
            p_ref[:, 512:1024].astype(F32), p_ref[:, 1024:1536].astype(F32),
            gq_ref[...], gkv_ref[...], wq_ref[...], wkk_ref[...], wkv_ref[...], sp_ref[...],
            cos_ref[:, 0:768], sin_ref[:, 0:768], cos_ref[:, 768:896], sin_ref[:, 768:896])
        q_ref[...] = q.astype(BF)
        kv_ref[:, 0:768] = k.astype(BF)
        kv_ref[:, 768:1280] = v.astype(BF)
        y_ref[...] = y.astype(BF)

    return _rb_call("odd_tok_fwd", body, row_in=(p,), pos_in=(cos, sin), full_in=(gq, gkv, wq, wkk, wkv, spread),
                    row_out=((768, BF), (1280, BF), (512, BF)))


def odd_tok_bwd(p, dq, dkvt, dy, cos, sin, gq, gkv, wq, wkk, wkv, spread):
    def body(i, p_ref, dq_ref, dy_ref, dkvt_ref, cos_ref, sin_ref, gq_ref, gkv_ref, wq_ref, wkk_ref, wkv_ref, sp_ref,
             dp_ref, dgq_ref, dgkv_ref, dwq_ref, dwkk_ref, dwkv_ref):
        cr, sr, ck, sk = cos_ref[:, 0:768], sin_ref[:, 0:768], cos_ref[:, 768:896], sin_ref[:, 768:896]
        spread_v = sp_ref[...]

        def f(cq, ckv, kr, za, zg, gq, gkv, wq, wkk, wkv):
            return _odd_tok(cq, ckv, kr, za, zg, gq, gkv, wq, wkk, wkv, spread_v, cr, sr, ck, sk)

        _, vjp = jax.vjp(f, p_ref[:, 0:256].astype(F32), p_ref[:, 256:384].astype(F32),
                         p_ref[:, 384:512].astype(F32), p_ref[:, 512:1024].astype(F32),
                         p_ref[:, 1024:1536].astype(F32), gq_ref[...], gkv_ref[...], wq_ref[...],
                         wkk_ref[...], wkv_ref[...])
        d = vjp((dq_ref[...].astype(F32), dkvt_ref[0:768, :].T, dkvt_ref[768:1280, :].T, dy_ref[...].astype(F32)))
        dp_ref[:, 0:256] = d[0].astype(BF)
        dp_ref[:, 256:384] = d[1].astype(BF)
        dp_ref[:, 384:512] = d[2].astype(BF)
        dp_ref[:, 512:1024] = d[3].astype(BF)
        dp_ref[:, 1024:1536] = d[4].astype(BF)
        for ref, val in zip((dgq_ref, dgkv_ref, dwq_ref, dwkk_ref, dwkv_ref), d[5:]):
            _acc(ref, val, i == 0)

    return _rb_call("odd_tok_bwd", body, row_in=(p, dq, dy), col_in=(dkvt,), pos_in=(cos, sin),
                    full_in=(gq, gkv, wq, wkk, wkv, spread), row_out=((OD_PAD, BF),),
                    acc_out=((1, 256), (1, 128), (256, 768), (128, 768), (128, 512)))


GQA_HEADS = [(64 * h, 64 * (h // 4), 64, 128 + 64 * (h // 4)) for h in range(8)]
MLA_HEADS = [(96 * h, 96 * h, 96, 768 + 64 * h) for h in range(8)]


def _by_block(j, run):
    @pl.when(j == 0)
    def _():
        run(LC)

    @pl.when(j > 0)
    def _():
        run(SEQ)


def attn_fwd(name, q, kv, heads):
    qw, kvw = q.shape[1], kv.shape[1]

    def kern(q_ref, kv_ref, o_ref, lse_ref):
        def run(nk):
            for h, (qo, ko, w, vo) in enumerate(heads):
                s = lax.dot_general(q_ref[:, qo:qo + w], kv_ref[0:nk, ko:ko + w], NT, preferred_element_type=F32)
                m = jnp.max(s, axis=-1, keepdims=True)
                p = jnp.exp(s - m)
                l = jnp.sum(p, axis=-1, keepdims=True)
                o = jnp.dot(p.astype(BF), kv_ref[0:nk, vo:vo + 64], preferred_element_type=F32) / l
                o_ref[:, 64 * h:64 * h + 64] = o.astype(BF)
                lse_ref[:, h:h + 1] = m + jnp.log(l)

        _by_block(pl.program_id(1), run)

    return pl.pallas_call(
        kern, grid=(NEX, BPE),
        in_specs=[pl.BlockSpec((TB, qw), lambda e, j: (e * BPE + j, 0)),
                  pl.BlockSpec((SEQ, kvw), lambda e, j: (e, 0))],
        out_specs=[pl.BlockSpec((TB, 512), lambda e, j: (e * BPE + j, 0)),
                   pl.BlockSpec((TB, 8), lambda e, j: (e * BPE + j, 0))],
        out_shape=[jax.ShapeDtypeStruct((R, 512), BF), jax.ShapeDtypeStruct((R, 8), F32)],
        compiler_params=_cparams(("parallel", "arbitrary")), name=name)(q, kv)


def attn_bwd(name, q, kv, o, dcat, lse, heads):
    qw, kvw = q.shape[1], kv.shape[1]

    def kern(q_ref, kv_ref, o_ref, do_ref, lse_ref, dq_ref, dkvt_ref):
        j = pl.program_id(1)

        @pl.when(j == 0)
        def _():
            dkvt_ref[...] = jnp.zeros_like(dkvt_ref)

        def run(nk):
            for h, (qo, ko, w, vo) in enumerate(heads):
                qh = q_ref[:, qo:qo + w]
                kh = kv_ref[0:nk, ko:ko + w]
                s = lax.dot_general(qh, kh, NT, preferred_element_type=F32)
                p = jnp.exp(s - lse_ref[:, h:h + 1])
                do = do_ref[:, 64 * h:64 * h + 64]
                dsum = jnp.sum(do.astype(F32) * o_ref[:, 64 * h:64 * h + 64].astype(F32), axis=-1, keepdims=True)
                dp = lax.dot_general(do, kv_ref[0:nk, vo:vo + 64], NT, preferred_element_type=F32)
                ds = (p * (dp - dsum)).astype(BF)
                dkvt_ref[vo:vo + 64, 0:nk] += lax.dot_general(do, p.astype(BF), TN, preferred_element_type=F32)
                dq_ref[:, qo:qo + w] = jnp.dot(ds, kh, preferred_element_type=F32).astype(BF)
                dkvt_ref[ko:ko + w, 0:nk] += lax.dot_general(qh, ds, TN, preferred_element_type=F32)

        _by_block(j, run)

    return pl.pallas_call(
        kern, grid=(NEX, BPE),
        in_specs=[pl.BlockSpec((TB, qw), lambda e, j: (e * BPE + j, 0)),
                  pl.BlockSpec((SEQ, kvw), lambda e, j: (e, 0)),
                  pl.BlockSpec((TB, 512), lambda e, j: (e * BPE + j, 0)),
                  pl.BlockSpec((TB, 512), lambda e, j: (e * BPE + j, 0)),
                  pl.BlockSpec((TB, 8), lambda e, j: (e * BPE + j, 0))],
        out_specs=[pl.BlockSpec((TB, qw), lambda e, j: (e * BPE + j, 0)),
                   pl.BlockSpec((kvw, SEQ), lambda e, j: (0, e))],
        out_shape=[jax.ShapeDtypeStruct((R, qw), BF), jax.ShapeDtypeStruct((kvw, R), F32)],
        compiler_params=_cparams(("parallel", "arbitrary")), name=name)(q, kv, o, dcat, lse)


HALO = 16
CONV_K = 31


def _fill_ext(ext_ref, prev_ref, cur_ref, next_ref, i):
    j = i % BPE
    has_prev = (j >= 2).astype(F32)
    has_next = jnp.logical_and(j >= 1, j <= BPE - 2).astype(F32)
    ext_ref[0:HALO, :] = prev_ref[TB - HALO:TB, :].astype(F32) * has_prev
    ext_ref[HALO:HALO + TB, :] = cur_ref[...].astype(F32)
    ext_ref[HALO + TB:2 * HALO + TB, :] = next_ref[0:HALO, :].astype(F32) * has_next


PHASE_ROWS = TB + 24


def _phases(ext_ref, ph_ref):
    for r in range(8):
        ph_ref[r] = ext_ref[r:r + PHASE_ROWS, :]


def _window(ph_ref, off):
    return ph_ref[off % 8, 8 * (off // 8):8 * (off // 8) + TB, :]


def _ln_silu(z, g, b):
    mu = jnp.mean(z, axis=-1, keepdims=True)
    zc = z - mu
    var = jnp.mean(zc * zc, axis=-1, keepdims=True)
    return _silu(zc * lax.rsqrt(var + EPS) * g + b)


def conf_fwd(y, cw, cb, lg, lb):
    def body(i, cur_ref, cw_ref, cb_ref, lg_ref, lb_ref, prev_ref, next_ref, z_ref, c_ref, ext_ref, ph_ref):
        _fill_ext(ext_ref, prev_ref, cur_ref, next_ref, i)
        _phases(ext_ref, ph_ref)
        acc = _window(ph_ref, 1) * cw_ref[0:1, :]
        for k in range(1, CONV_K):
            acc = acc + _window(ph_ref, k + 1) * cw_ref[k:k + 1, :]
        z = acc + cb_ref[...]
        z_ref[...] = z.astype(BF)
        c_ref[...] = _ln_silu(z, lg_ref[...], lb_ref[...]).astype(BF)

    return _rb_call("conf_fwd", body, row_in=(y,), full_in=(cw, cb, lg, lb), shift_in=((y, -1), (y, 1)),
                    row_out=((512, BF), (512, BF)),
                    scratch=(pltpu.VMEM((TB + 2 * HALO, 512), F32), pltpu.VMEM((8, PHASE_ROWS, 512), F32)))


def conf_bwd_ln(z, dcat, lg, lb):
    def body(i, z_ref, dcat_ref, lg_ref, lb_ref, dz_ref, dlg_ref, dlb_ref, dcb_ref):
        _, vjp = jax.vjp(_ln_silu, z_ref[...].astype(F32), lg_ref[...], lb_ref[...])
        dz, dlg, dlb = vjp(dcat_ref[:, 512:1024].astype(F32))
        dz_ref[...] = dz.astype(BF)
        _acc(dlg_ref, dlg, i == 0)
        _acc(dlb_ref, dlb, i == 0)
        _acc(dcb_ref, jnp.sum(dz, axis=0, keepdims=True), i == 0)

    return _rb_call("conf_bwd_ln", body, row_in=(z, dcat), full_in=(lg, lb), row_out=((512, BF),),
                    acc_out=((1, 512), (1, 512), (1, 512)))


def conf_bwd_conv(y, dz, cw):
    def body(i, y_ref, dz_ref, cw_ref, yp_ref, yn_ref, dzp_ref, dzn_ref, dy_ref, dcw_ref, ext_ref, phy_ref, phd_ref):
        _fill_ext(ext_ref, yp_ref, y_ref, yn_ref, i)
        _phases(ext_ref, phy_ref)
        _fill_ext(ext_ref, dzp_ref, dz_ref, dzn_ref, i)
        _phases(ext_ref, phd_ref)
        dzv = dz_ref[...].astype(F32)

        @pl.when(i == 0)
        def _():
            dcw_ref[...] = jnp.zeros_like(dcw_ref)

        acc = None
        for k in range(CONV_K):
            t = _window(phd_ref, CONV_K - k) * cw_ref[k:k + 1, :]
            acc = t if acc is None else acc + t
            dcw_ref[k:k + 1, :] += jnp.sum(dzv * _window(phy_ref, k + 1), axis=0, keepdims=True)
        dy_ref[...] = acc.astype(BF)

    return _rb_call("conf_bwd_conv", body, row_in=(y, dz), full_in=(cw,),
                    shift_in=((y, -1), (y, 1), (dz, -1), (dz, 1)), row_out=((512, BF),), acc_out=((32, 512),),
                    scratch=(pltpu.VMEM((TB + 2 * HALO, 512), F32), pltpu.VMEM((8, PHASE_ROWS, 512), F32),
                             pltpu.VMEM((8, PHASE_ROWS, 512), F32)))


def final_loss(x, target, fg, y, mods, m_gate):
    lpb = L // TB

    def kern(x_ref, t_ref, g_ref, y_ref, gt_ref, dx_ref, dy_ref, dgt_ref, loss_ref, dg_ref):
        i = pl.program_id(0)
        lat = (i % BPE) >= 1
        xv, tv = x_ref[...], t_ref[...]

        def f(x, g):
            err = _rmsn(x, g) - tv
            rowsum = jnp.sum(err * err, axis=-1, keepdims=True)
            return jnp.sum(rowsum, axis=0, keepdims=True) * (0.5 / D)

        lv, vjp = jax.vjp(f, xv, g_ref[...])
        dx, dg = vjp(jnp.ones((1, 1), F32))
        m = lat.astype(F32)
        dx = dx * m
        dx_ref[...] = dx
        _gate_grads(dx, y_ref, gt_ref, dy_ref, dgt_ref, i)
        _acc(loss_ref, jnp.zeros((8, 128), F32) + lv * m, i == 0)
        _acc(dg_ref, dg * m, i == 0)

    row = pl.BlockSpec((TB, D), lambda i: (i, 0))
    return pl.pallas_call(
        kern, grid=(NBLK,),
        in_specs=[row, pl.BlockSpec((TB, D), lambda i: ((i // BPE) * lpb + jnp.maximum(i % BPE - 1, 0), 0)),
                  pl.BlockSpec((1, D), lambda i: (0, 0)), row,
                  pl.BlockSpec((1, 1, D), lambda i: (_seg(i) * N_MOD + m_gate, 0, 0))],
        out_specs=[row, row, pl.BlockSpec((1, 1, D), lambda i: (_seg(i), 0, 0)),
                   pl.BlockSpec((8, 128), lambda i: (0, 0)), pl.BlockSpec((1, D), lambda i: (0, 0))],
        out_shape=[jax.ShapeDtypeStruct((R, D), F32), jax.ShapeDtypeStruct((R, D), BF),
                   jax.ShapeDtypeStruct((4, 1, D), F32), jax.ShapeDtypeStruct((8, 128), F32),
                   jax.ShapeDtypeStruct((1, D), F32)],
        compiler_params=_cparams(("arbitrary",)), name="final_loss")(x, target, fg, y, mods)


NC = 24


def mods_fwd(call, ada_w, ada_b):
    cols = ada_w.shape[2]

    def kern(c_ref, w_ref, b_ref, o_ref):
        o_ref[...] = jnp.dot(_silu(c_ref[...]), w_ref[...], precision=HI, preferred_element_type=F32) + b_ref[...]

    return pl.pallas_call(
        kern, grid=(2,),
        in_specs=[pl.BlockSpec((NC, D), lambda l: (0, 0)), pl.BlockSpec((None, D, cols), lambda l: (l, 0, 0)),
                  pl.BlockSpec((None, 1, cols), lambda l: (l, 0, 0))],
        out_specs=pl.BlockSpec((None, NC, cols), lambda l: (l, 0, 0)),
        out_shape=jax.ShapeDtypeStruct((2, NC, cols), F32),
        compiler_params=_cparams(("parallel",)), name="mods_fwd")(call, ada_w, ada_b)


def ada_bwd(call, ada_w, dm):
    cols = ada_w.shape[2]

    def kern(c_ref, w_ref, dm_ref, gw_ref, dc_ref):
        l = pl.program_id(0)
        gw_ref[...] = lax.dot_general(_silu(c_ref[...]), dm_ref[...], TN, precision=HI, preferred_element_type=F32)
        part = lax.dot_general(dm_ref[16:24, :], w_ref[...], NT, precision=HI, preferred_element_type=F32)
        cc = c_ref[16:17, :]
        sg = jax.nn.sigmoid(cc)
        _acc(dc_ref, part * (sg * (1.0 + cc * (1.0 - sg))), l == 0)

    return pl.pallas_call(
        kern, grid=(2,),
        in_specs=[pl.BlockSpec((NC, D), lambda l: (0, 0)), pl.BlockSpec((None, D, cols), lambda l: (l, 0, 0)),
                  pl.BlockSpec((None, NC, cols), lambda l: (l, 0, 0))],
        out_specs=[pl.BlockSpec((None, D, cols), lambda l: (l, 0, 0)), pl.BlockSpec((8, D), lambda l: (0, 0))],
        out_shape=[jax.ShapeDtypeStruct((2, D, cols), F32), jax.ShapeDtypeStruct((8, D), F32)],
        compiler_params=_cparams(("arbitrary",)), name="ada_bwd")(call, ada_w, dm)


def sum_lead(name, a, after=None):
    n, r, c = a.shape
    tr = r
    for cand in (512, 256, 128, 64, 32, 16, 8):
        if r % cand == 0 and cand * c * 4 * n <= 8 * 1024 * 1024:
            tr = cand
            break
    extra = [] if after is None else [after]

    def kern(a_ref, *rest):
        acc = a_ref[0].astype(F32)
        for k in range(1, n):
            acc = acc + a_ref[k].astype(F32)
        rest[-1][...] = acc

    return pl.pallas_call(
        kern, grid=(r // tr,),
        in_specs=[pl.BlockSpec((n, tr, c), lambda i: (0, i, 0))]
        + [pl.BlockSpec(e.shape, lambda i, k=e.ndim: (0,) * k) for e in extra],
        out_specs=pl.BlockSpec((tr, c), lambda i: (i, 0)), out_shape=jax.ShapeDtypeStruct((r, c), F32),
        compiler_params=_cparams(("parallel",)), name=name)(a, *extra)


def add_pairs(name, hs, got, half):
    _, _, r, c = hs.shape

    def kern(half_ref, a_ref, b_ref, o_ref):
        o_ref[...] = (a_ref[...].astype(F32) + b_ref[...].astype(F32)).astype(BF)

    spec = pl.BlockSpec((None, r, c), lambda j, h: (j, 0, 0))
    grid_spec = pltpu.PrefetchScalarGridSpec(
        num_scalar_prefetch=1, grid=(4,),
        in_specs=[pl.BlockSpec((None, None, r, c), lambda j, h: (h[0], j, 0, 0)), spec], out_specs=spec)
    return pl.pallas_call(kern, grid_spec=grid_spec, out_shape=jax.ShapeDtypeStruct(got.shape, BF),
                          compiler_params=_cparams(("parallel",)), name=name)(half, hs, got)


def sum_slabs(name, land, own, where, full, lead):
    _, r, c = land.shape
    tr = r
    for cand in (512, 256, 128, 64, 32, 16):
        if r % cand == 0 and cand * c * 16 <= 4 * 1024 * 1024:
            tr = cand
            break

    def kern(where_ref, full_ref, land_ref, own_ref, o_ref):
        me = where_ref[0]
        acc = None
        for k in range(4):
            t = jnp.where(me == k, own_ref[k], land_ref[k]).astype(F32)
            acc = t if acc is None else acc + t
        o_ref[...] = acc

    spec = pl.BlockSpec((4, tr, c), lambda i, m: (0, i, 0))
    grid_spec = pltpu.PrefetchScalarGridSpec(
        num_scalar_prefetch=1, grid=(r // tr,), in_specs=[pl.BlockSpec(memory_space=pl.ANY), spec, spec],
        out_specs=pl.BlockSpec((None, None, tr, c), lambda i, m: (lead, m[1], i, 0)))
    return pl.pallas_call(kern, grid_spec=grid_spec, out_shape=jax.ShapeDtypeStruct(full.shape, F32),
                          input_output_aliases={1: 0}, compiler_params=_cparams(("parallel",)),
                          name=name)(where, full, land, own)


def adamw(name, w, g, m, v, again=False):
    r, c = w.shape
    tr = r
    for cand in (512, 256, 128, 64, 32, 16, 8):
        if r % cand == 0 and cand * c * 4 <= 2 * 1024 * 1024:
            tr = cand
            break
    c1 = 1.0 / (1.0 - ADAM_B1 ** ADAM_STEP)
    c2 = 1.0 / (1.0 - ADAM_B2 ** ADAM_STEP)

    def kern(w_ref, g_ref, m_ref, v_ref, d_ref, mo_ref, vo_ref, *go_ref):
        gv = g_ref[...]
        mn = ADAM_B1 * m_ref[...] + (1.0 - ADAM_B1) * gv
        vn = ADAM_B2 * v_ref[...] + (1.0 - ADAM_B2) * (gv * gv)
        d_ref[...] = -ADAM_LR * ((mn * c1) / (jnp.sqrt(vn * c2) + ADAM_EPS) + ADAM_WD * w_ref[...])
        mo_ref[...] = mn
        vo_ref[...] = vn
        if again:
            go_ref[0][...] = gv

    spec = pl.BlockSpec((tr, c), lambda i: (i, 0))
    shp = jax.ShapeDtypeStruct((r, c), F32)
    n_out = 4 if again else 3
    return pl.pallas_call(kern, grid=(r // tr,), in_specs=[spec] * 4, out_specs=[spec] * n_out,
                          out_shape=[shp] * n_out, compiler_params=_cparams(("parallel",)), name=name)(w, g, m, v)


def all_gather8(name, xs, after=None):
    m_per, n = xs.shape
    extra = [] if after is None else [after]

    def body(x_ref, *rest):
        out_ref, send_sems, recv_sems, local_sem = rest[len(extra):]
        x, y, c = lax.axis_index("x"), lax.axis_index("y"), lax.axis_index("c")
        me, sibling = (x, y, c), (x, y, 1 - c)
        chips = [(1 - x, y), (x, 1 - y), (1 - x, 1 - y)]

        def rows(px, py, pc):
            return out_ref.at[pl.ds((4 * px + 2 * py + pc) * m_per, m_per), :]

        def copy(k, block, to, src=None):
            return pltpu.make_async_remote_copy(
                src_ref=rows(*block) if src is None else src, dst_ref=rows(*block),
                send_sem=send_sems.at[k], recv_sem=recv_sems.at[k], device_id=to, device_id_type=MESH)

        mine = pltpu.make_async_copy(x_ref, rows(*me), local_sem)
        mine.start()
        first = [copy(0, me, sibling, src=x_ref)]
        first += [copy(1 + j, me, (*chip, c), src=x_ref) for j, chip in enumerate(chips)]
        for cp in first:
            cp.start()
        passed = [copy(4 + j, (*chip, c), sibling) for j, chip in enumerate(chips)]
        for j, chip in enumerate(chips):
            copy(1 + j, (*chip, c), me).wait_recv()
            passed[j].start()
        copy(0, sibling, me).wait_recv()
        for j, chip in enumerate(chips):
            copy(4 + j, (*chip, 1 - c), me).wait_recv()
        for cp in first + passed:
            cp.wait_send()
        mine.wait()

    return pl.pallas_call(
        body, out_shape=jax.ShapeDtypeStruct((8 * m_per, n), xs.dtype),
        in_specs=[pl.BlockSpec(memory_space=pltpu.VMEM)] * (1 + len(extra)),
        out_specs=pl.BlockSpec(memory_space=pltpu.VMEM),
        scratch_shapes=[pltpu.SemaphoreType.DMA((7,)), pltpu.SemaphoreType.DMA((7,)), pltpu.SemaphoreType.DMA],
        compiler_params=pltpu.CompilerParams(vmem_limit_bytes=VMEM_LIMIT), name=name)(xs, *extra)


def sibling_merge(name, fulls):
    n = len(fulls)
    slots = [(a, l) for a in range(n) for l in range(fulls[a].shape[0])]

    def body(*refs):
        buf = refs[n:2 * n]
        send_sems, recv_sems = refs[2 * n], refs[2 * n + 1]
        c = lax.axis_index("c")
        sibling = (lax.axis_index("x"), lax.axis_index("y"), 1 - c)
        sends, recvs = [], []
        for k, (a, l) in enumerate(slots):
            kw = dict(send_sem=send_sems.at[k], recv_sem=recv_sems.at[k], device_id=sibling, device_id_type=MESH)
            sends.append(pltpu.make_async_remote_copy(src_ref=buf[a].at[l, c], dst_ref=buf[a].at[l, c], **kw))
            recvs.append(pltpu.make_async_remote_copy(src_ref=buf[a].at[l, c], dst_ref=buf[a].at[l, 1 - c], **kw))
        for cp in sends:
            cp.start()
        for cp in recvs:
            cp.wait_recv()
        for cp in sends:
            cp.wait_send()

    anyspec = pl.BlockSpec(memory_space=pl.ANY)
    return pl.pallas_call(
        body, out_shape=[jax.ShapeDtypeStruct(s.shape, s.dtype) for s in fulls],
        in_specs=[anyspec] * n, out_specs=[anyspec] * n, input_output_aliases={a: a for a in range(n)},
        scratch_shapes=[pltpu.SemaphoreType.DMA((len(slots),)), pltpu.SemaphoreType.DMA((len(slots),))],
        name=name)(*fulls)


def place_own(name, land, src, chip):
    c = src.shape[-1]
    r = src.size // c
    tr = r
    for cand in (1024, 512, 256, 128, 64, 32, 16):
        if r % cand == 0 and cand * c * 2 <= 2 * 1024 * 1024:
            tr = cand
            break

    def kern(chip_ref, land_ref, src_ref, out_ref):
        out_ref[...] = src_ref[...]

    grid_spec = pltpu.PrefetchScalarGridSpec(
        num_scalar_prefetch=1, grid=(r // tr,),
        in_specs=[pl.BlockSpec(memory_space=pl.ANY), pl.BlockSpec((tr, c), lambda i, m: (i, 0))],
        out_specs=pl.BlockSpec((None, tr, c), lambda i, m: (m[0], i, 0)))
    out = pl.pallas_call(kern, grid_spec=grid_spec, out_shape=jax.ShapeDtypeStruct((4, r, c), land.dtype),
                         input_output_aliases={1: 0}, compiler_params=_cparams(("parallel",)),
                         name=name)(chip, land.reshape(4, r, c), src.reshape(r, c))
    return out.reshape(land.shape)


def _half_copies(src, land, send_sems, recv_sems):
    c = lax.axis_index("c")
    sibling = (lax.axis_index("x"), lax.axis_index("y"), 1 - c)
    pairs = []
    for a in range(len(src)):
        cp = pltpu.make_async_remote_copy(src_ref=src[a].at[1 - c], dst_ref=land[a], send_sem=send_sems.at[a],
                                          recv_sem=recv_sems.at[a], device_id=sibling, device_id_type=MESH)
        pairs.append((cp, cp))
    return pairs


def _chip_copies(src, land, send_sems, recv_sems, scatter):
    x, y, c = lax.axis_index("x"), lax.axis_index("y"), lax.axis_index("c")
    me = 2 * x + y
    pairs = []
    for a in range(len(src)):
        for j, (px, py) in enumerate([(1 - x, y), (x, 1 - y), (1 - x, 1 - y)]):
            to = 2 * px + py
            out = src[a].at[to] if scatter else src[a]
            kw = dict(send_sem=send_sems.at[3 * a + j], recv_sem=recv_sems.at[3 * a + j], device_id=(px, py, c),
                      device_id_type=MESH)
            pairs.append((pltpu.make_async_remote_copy(src_ref=out, dst_ref=land[a].at[me], **kw),
                          pltpu.make_async_remote_copy(src_ref=out, dst_ref=land[a].at[to], **kw)))
    return pairs


_HBM = pl.BlockSpec(memory_space=pltpu.HBM)
_SEM = pl.BlockSpec(memory_space=pltpu.SEMAPHORE)


GATHER = (functools.partial(_chip_copies, scatter=False), 3)
SCATTER = (functools.partial(_chip_copies, scatter=True), 3)
TO_SIBLING = (_half_copies, 1)


def _landing(shapes, dtype):
    return [lax.empty(tuple(s), dtype) for s in shapes]


def exchange_start(name, groups, plan):
    copies, per = plan
    sizes = [len(s) for s, _ in groups]
    flat = [a for s, l in groups for a in list(s) + list(l)]
    ng = len(groups)

    def body(*refs):
        ins, outs = refs[:len(flat)], refs[len(flat):]
        off = 0
        for g, n in enumerate(sizes):
            src, land = ins[off:off + n], ins[off + n:off + 2 * n]
            off += 2 * n
            for send, _ in copies(src, land, outs[2 * g], outs[2 * g + 1]):
                send.start()
        outs[-1][...] = jnp.zeros_like(outs[-1])

    out_shape = []
    for n in sizes:
        out_shape += [pltpu.SemaphoreType.DMA((per * n,)), pltpu.SemaphoreType.DMA((per * n,))]
    out_shape += [pltpu.HBM(a.shape, a.dtype) for a in flat] + [jax.ShapeDtypeStruct((8, 128), F32)]
    res = pl.pallas_call(
        body, out_shape=tuple(out_shape), in_specs=[_HBM] * len(flat),
        out_specs=tuple([_SEM] * (2 * ng) + [_HBM] * len(flat) + [pl.BlockSpec(memory_space=pltpu.VMEM)]),
        input_output_aliases={k: 2 * ng + k for k in range(len(flat))},
        compiler_params=pltpu.CompilerParams(has_side_effects=pltpu.SideEffectType.DATAFLOW_SIDE_EFFECTING),
        name=name)(*[pltpu.with_memory_space_constraint(a, pltpu.HBM) for a in flat])
    handles, off = [], 2 * ng
    for g, n in enumerate(sizes):
        handles.append((res[2 * g], res[2 * g + 1], list(res[off:off + n]), list(res[off + n:off + 2 * n])))
        off += 2 * n
    return handles, res[-1]


def exchange_wait(name, handle, after, plan):
    send_sems, recv_sems, srcs, lands = handle
    n = len(srcs)

    def body(*refs):
        src, land = refs[:n], refs[n:2 * n]
        for send, recv in plan[0](src, land, refs[2 * n], refs[2 * n + 1]):
            send.wait_send()
            recv.wait_recv()

    res = pl.pallas_call(
        body, out_shape=tuple(pltpu.HBM(a.shape, a.dtype) for a in srcs + lands),
        in_specs=[_HBM] * (2 * n) + [_SEM, _SEM, pl.BlockSpec(memory_space=pl.ANY)],
        out_specs=tuple([_HBM] * (2 * n)), input_output_aliases={k: k for k in range(2 * n)},
        compiler_params=pltpu.CompilerParams(has_side_effects=pltpu.SideEffectType.DATAFLOW_SIDE_EFFECTING),
        name=name)(*srcs, *lands, send_sems, recv_sems, after)
    return list(res[:n]), list(res[n:])


def _rope_tables(d_rot, reps):
    rows = L // GRID_W
    row = np.repeat(np.arange(rows), GRID_W).astype(np.float32)
    col = np.tile(np.arange(GRID_W), rows).astype(np.float32)
    d_axis = d_rot // 2
    inv = (ROPE_THETA ** (-np.arange(0, d_axis, 2, dtype=np.float32) / d_axis)).astype(np.float32)
    ang = np.concatenate([row[:, None] * inv, col[:, None] * inv], axis=-1).astype(np.float32)
    cos, sin = np.cos(ang).astype(np.float32), np.sin(ang).astype(np.float32)
    c = np.repeat(cos, 2, axis=-1)
    s = np.stack([-sin, sin], axis=-1).reshape(L, d_rot)
    c = np.concatenate([np.ones((LC, d_rot), np.float32), c], axis=0)
    s = np.concatenate([np.zeros((LC, d_rot), np.float32), s], axis=0)
    return np.tile(c, (1, reps)), np.tile(s, (1, reps))


def _group_consts():
    g = np.arange(512) // 64
    avg = (g[:, None] == g[None, :]).astype(np.float32) / 64.0
    masks = (np.arange(8)[:, None] == g[None, :]).astype(np.float32).reshape(8, 1, 512)
    return jnp.asarray(avg, BF), jnp.asarray(masks)


def _pack(items):
    flat = jnp.concatenate([a.reshape(-1).astype(F32) for a in items])
    n = flat.shape[0]
    rows = -(-n // D)
    rows = -(-rows // 8) * 8
    return jnp.pad(flat, (0, rows * D - n)).reshape(rows, D)


def _unpack(buf, shapes):
    lead = buf.shape[:-2]
    flat = buf.reshape(lead + (-1,))
    out, off = [], 0
    for shp in shapes:
        n = int(np.prod(shp))
        out.append(flat[..., off:off + n].reshape(lead + tuple(shp)))
        off += n
    return out


def _arrive(prm, key, after):
    if callable(prm[key]):
        prm[key](after)
    return prm[key]


def _layer_fwd(i, x, h, mods, prm, consts, nxt):
    sv = {}
    sv["x0"] = x
    sv["h"] = h
    p = proj_in(f"proj_in_{i}", h, _arrive(prm, "w_in", h))
    sv["p"] = p
    if i == 0:
        q, kv, m2 = even_tok_fwd(p, consts["cos_e"], consts["sin_e"], prm["gq"], prm["gk"], prm["gs"],
                                 prm["sgu_w"], prm["sgu_b"], consts["avg"], consts["masks"])
        o, lse = attn_fwd("attn_fwd_0", q, kv, GQA_HEADS)
        sv.update(q=q, kv=kv)
    else:
        q, kv, y = odd_tok_fwd(p, consts["cos_o"], consts["sin_o"], prm["gq"], prm["gkv"], prm["wq"], prm["wkk"],
                               prm["wkv"], consts["spread"])
        o, lse = attn_fwd("attn_fwd_1", q, kv, MLA_HEADS)
        z, m2 = conf_fwd(y, prm["conv_w"], prm["conv_b"], prm["ln_g"], prm["ln_b"])
        sv.update(q=q, kv=kv, y=y, z=z)
    sv.update(o=o, lse=lse, m2=m2)
    x1, y1, h2 = proj_out(f"proj_out_{i}", o, m2, _arrive(prm, "w_out", o), x, mods, 2, prm["norm2_g"], 3, 4)
    sv.update(x1=x1, y1=y1)
    a, f = mlp_up(f"mlp_up_{i}", h2, _arrive(prm, "w1", h2))
    x2, y2, *h_next = mlp_down(f"mlp_down_{i}", f, prm["w2"], x1, mods, 5, nxt)
    sv.update(h2=h2, a=a, f=f, y2=y2)
    return x2, (h_next[0] if h_next else None), sv


def _layer_bwd(i, dx, dy2, dg2, sv, mods, prm, consts, hook, entry, below):
    gr = {}
    da = mlp_bwd_da(f"mlp_bwd_da_{i}", dy2, prm["w2"], sv["a"], after=entry)
    tiles8 = [(h, j) for h in range(2) for j in range(4)]
    gr["w1"] = mm_tn(f"grad_w1_{i}", sv["h2"], da, tiles8, 512, D).reshape(2, 4, 512, D)
    gr["w2"] = mm_tn(f"grad_w2_{i}", sv["f"], dy2, [(2 * j + h, 0) for h in range(2) for j in range(4)],
                     512, D).reshape(2, 4, 512, D)
    dh2 = mlp_bwd_dh(f"mlp_bwd_dh_{i}", da, prm["w1"])
    dx1, dy1, dg1, dsh2, dsc2, gr["norm2_g"] = modnorm_bwd(
        f"norm2_bwd_{i}", sv["x1"], dh2, dx, mods, prm["norm2_g"], 3, 4, gate=(sv["y1"], mods, 2),
        after=hook(f"{i}:mlp", gr, dh2))
    dcat = mm_nt(f"proj_out_bwd_{i}", dy1, prm["w_out"], after=hook(f"{i}:mid", gr, dy1))
    go = mm_tn(f"grad_wout_a_{i}", sv["o"], dy1, [(0, 0)], 512, D).reshape(2, 2, 128, D)
    gm = mm_tn(f"grad_wout_b_{i}", sv["m2"], dy1, [(0, 0)], 512, D).reshape(2, 2, 128, D)
    gr["w_out"] = jnp.concatenate([go, gm], axis=0).transpose(1, 0, 2, 3)
    if i == 0:
        dq, dkv = attn_bwd("attn_bwd_0", sv["q"], sv["kv"], sv["o"], dcat, sv["lse"], GQA_HEADS)
        dp, gr["gq"], gr["gk"], gr["gs"], gr["sgu_w"], gr["sgu_b"] = even_tok_bwd(
            sv["p"], dq, dkv, dcat, consts["cos_e"], consts["sin_e"], prm["gq"], prm["gk"],
            prm["gs"], prm["sgu_w"], prm["sgu_b"], consts["avg"], consts["masks"])
    else:
        dq, dkv = attn_bwd("attn_bwd_1", sv["q"], sv["kv"], sv["o"], dcat, sv["lse"], MLA_HEADS)
        dz, gr["ln_g"], gr["ln_b"], gr["conv_b"] = conf_bwd_ln(sv["z"], dcat, prm["ln_g"], prm["ln_b"])
        dyc, gr["conv_w"] = conf_bwd_conv(sv["y"], dz, prm["conv_w"])
        dp, gr["gq"], gr["gkv"], gr["wq"], gr["wkk"], gr["wkv"] = odd_tok_bwd(
            sv["p"], dq, dkv, dyc, consts["cos_o"], consts["sin_o"], prm["gq"], prm["gkv"], prm["wq"], prm["wkk"],
            prm["wkv"], consts["spread"])
    n_in = prm["w_in"].shape[1]
    gr["w_in"] = mm_tn(f"grad_win_{i}", sv["h"], dp, [(0, 0), (1, 0)], 512, n_in)
    dh = mm_nt(f"proj_in_bwd_{i}", dp, prm["w_in"])
    if below:
        dx0, dy2b, dg2b, dsh1, dsc1, gr["norm1_g"] = modnorm_bwd(
            f"norm1_bwd_{i}", sv["x0"], dh, dx1, mods, prm["norm1_g"], 0, 1, gate=(below[0], below[1], 5))
        down = (dy2b, dg2b)
    else:
        dx0, dsh1, dsc1, gr["norm1_g"] = modnorm_bwd(f"norm1_bwd_{i}", sv["x0"], dh, dx1, mods, prm["norm1_g"], 0, 1,
                                                     lat_only=True)
        down = None
    dmods = jnp.concatenate([dsh1, dsc1, dg1, dsh2, dsc2, dg2], axis=1)
    return dx0, down, dmods, gr, hook(f"{i}:end", gr, dx0)


def local_step(xcat, target, mods, prms, final_g, hook=lambda point, grads, fresh: None):
    avg, masks = _group_consts()
    cos_e, sin_e = _rope_tables(64, 8)
    ck, sk = _rope_tables(32, 1)
    one64, zero64 = np.ones((SEQ, 64), np.float32), np.zeros((SEQ, 64), np.float32)
    one96, zero96 = np.ones((SEQ, 96), np.float32), np.zeros((SEQ, 96), np.float32)
    cos_o = np.concatenate([np.tile(np.concatenate([one64, ck], axis=1), (1, 8)), ck, one96], axis=1)
    sin_o = np.concatenate([np.tile(np.concatenate([zero64, sk], axis=1), (1, 8)), sk, zero96], axis=1)
    lane = np.arange(768)
    spread = np.zeros((128, 768), np.float32)
    spread[lane % 96 - 64, lane] = (lane % 96 >= 64)
    consts = dict(avg=avg, masks=masks, cos_e=jnp.asarray(cos_e), sin_e=jnp.asarray(sin_e),
                  cos_o=jnp.asarray(cos_o), sin_o=jnp.asarray(sin_o), spread=jnp.asarray(spread, BF))
    x = xcat
    h = modnorm_fwd("norm1_fwd_0", x, mods[0], prms[0]["norm1_g"], 0, 1)
    saved = []
    for i in range(2):
        x, h, sv = _layer_fwd(i, x, h, mods[i], prms[i], consts, (mods[1], prms[1]["norm1_g"]) if i == 0 else None)
        saved.append(sv)
    dx, dy2, dg2, loss, dfg = final_loss(x, target, final_g, saved[1]["y2"], mods[1], 5)
    dmods, grads = [None, None], [None, None]
    entry, down = None, (dy2, dg2)
    for i in (1, 0):
        below = (saved[0]["y2"], mods[0]) if i == 1 else None
        dx, down, dmods[i], grads[i], entry = _layer_bwd(i, dx, down[0], down[1], saved[i], mods[i], prms[i], consts,
                                                         hook, entry, below)
    return loss, dx, dmods, grads, dfg, entry


def _row(v):
    return v.reshape(1, -1).astype(F32)


def odd_in_params(od_w_in, w_uq, w_ukv):
    od = jnp.concatenate([od_w_in[:, 0:416], jnp.zeros((D, 96), od_w_in.dtype), od_w_in[:, 416:OD_IN]], axis=1)
    ukv = w_ukv.reshape(128, 8, 128)
    wkk = jnp.pad(ukv[:, :, :64], ((0, 0), (0, 0), (0, 32))).reshape(128, 768)
    return dict(w_in=od, wq=w_uq, wkk=wkk, wkv=ukv[:, :, 64:].reshape(128, 512))


def small_params(small):
    p0 = dict(norm1_g=_row(small["norm1_g"][0]), norm2_g=_row(small["norm2_g"][0]),
              gq=jnp.tile(_row(small["ev_q_norm_g"]), (1, 8)), gk=jnp.tile(_row(small["ev_k_norm_g"]), (1, 2)),
              gs=_row(small["ev_sgu_norm_g"]), sgu_w=small["ev_sgu_w"].reshape(8, 128, 128).astype(F32),
              sgu_b=small["ev_sgu_b"].reshape(8, 128, 1).astype(F32))
    p1 = dict(norm1_g=_row(small["norm1_g"][1]), norm2_g=_row(small["norm2_g"][1]),
              gq=_row(small["od_q_norm_g"]), gkv=_row(small["od_kv_norm_g"]),
              conv_w=jnp.pad(small["od_conv_w"].reshape(CONV_K, 512).astype(F32), ((0, 1), (0, 0))),
              conv_b=_row(small["od_conv_b"]), ln_g=_row(small["od_ln_g"]), ln_b=_row(small["od_ln_b"]))
    return [p0, p1]


def prep_params(ev_w_in, od_w_in, w_out, w1, w2, w_uq, w_ukv, small):
    p0, p1 = small_params(small)
    p0.update(w_in=ev_w_in, w_out=w_out[0], w1=w1[0], w2=w2[0])
    p1.update(odd_in_params(od_w_in, w_uq, w_ukv), w_out=w_out[1], w1=w1[1], w2=w2[1])
    return [p0, p1]


def small_grads_natural(grads, dfg):
    g0, g1 = grads
    return dict(
        norm1_g=jnp.concatenate([g0["norm1_g"], g1["norm1_g"]], axis=0),
        norm2_g=jnp.concatenate([g0["norm2_g"], g1["norm2_g"]], axis=0),
        ev_q_norm_g=g0["gq"].reshape(8, 64).sum(0).reshape(1, 64),
        ev_k_norm_g=g0["gk"].reshape(2, 64).sum(0).reshape(1, 64),
        ev_sgu_norm_g=g0["gs"].reshape(1, 8, 64),
        ev_sgu_w=g0["sgu_w"].reshape(1, 8, 128, 128),
        ev_sgu_b=g0["sgu_b"].reshape(1, 8, 128),
        od_q_norm_g=g1["gq"].reshape(1, 256),
        od_kv_norm_g=g1["gkv"].reshape(1, 128),
        od_conv_w=g1["conv_w"][0:CONV_K].reshape(1, CONV_K, 512),
        od_conv_b=g1["conv_b"].reshape(1, 512),
        od_ln_g=g1["ln_g"].reshape(1, 512),
        od_ln_b=g1["ln_b"].reshape(1, 512),
        final_g=dfg.reshape(D))


def layer_grads_hs(i, g, part="all"):
    def cols(a):
        k, n = a.shape
        return a.reshape(2, k // 2, 4, n // 4).transpose(0, 2, 1, 3).astype(BF)

    mlp = [(("mlp_w1", i), g["w1"]), (("mlp_w2", i), g["w2"])]
    if part == "mlp":
        return mlp
    rest = [(("w_out", i), g["w_out"])]
    if i == 0:
        rest.append((("ev_w_in", 0), cols(g["w_in"].reshape(D, EV_IN))))
    else:
        od = g["w_in"].reshape(D, OD_PAD)
        od = jnp.concatenate([od[:, 0:416], od[:, 512:OD_PAD]], axis=1)
        ukv = jnp.concatenate([g["wkk"].reshape(128, 8, 96)[:, :, :64], g["wkv"].reshape(128, 8, 64)], axis=2)
        rest += [(("od_w_in", 0), cols(od)), (("od_w_uq", 0), cols(g["wq"])),
                 (("od_w_ukv", 0), cols(ukv.reshape(128, 1024)))]
    return rest if part == "rest" else mlp + rest


def big_grads_hs(grads):
    d = dict(layer_grads_hs(0, grads[0]) + layer_grads_hs(1, grads[1]))
    return dict(ev_w_in=d[("ev_w_in", 0)], od_w_in=d[("od_w_in", 0)], od_w_uq=d[("od_w_uq", 0)],
                od_w_ukv=d[("od_w_ukv", 0)], w_out=[d[("w_out", 0)], d[("w_out", 1)]],
                mlp_w1=[d[("mlp_w1", 0)], d[("mlp_w1", 1)]], mlp_w2=[d[("mlp_w2", 0)], d[("mlp_w2", 1)]])


def grads_to_natural(grads, dfg):
    out = small_grads_natural(grads, dfg)
    hs = big_grads_hs(grads)

    def from_cols(a):
        return a.transpose(0, 2, 1, 3).reshape(2 * a.shape[2], 4 * a.shape[3])

    def from_rows(a):
        return a.transpose(1, 0, 2, 3).reshape(8 * a.shape[2], a.shape[3])

    out["ev_w_in"] = from_cols(hs["ev_w_in"])[None]
    out["od_w_in"] = from_cols(hs["od_w_in"])[None]
    out["od_w_uq"] = from_cols(hs["od_w_uq"])[None]
    out["od_w_ukv"] = from_cols(hs["od_w_ukv"])[None]
    out["w_out"] = jnp.stack([from_rows(a) for a in hs["w_out"]])
    out["mlp_w1"] = jnp.stack([from_cols(a) for a in hs["mlp_w1"]])
    out["mlp_w2"] = jnp.stack([from_rows(a) for a in hs["mlp_w2"]])
    return out


WEIGHT_NAMES = ['c_ctx', 'ada_w', 'ada_b', 'norm1_g', 'norm2_g', 'w_out', 'mlp_w1', 'mlp_w2', 'ev_w_in',
                'ev_q_norm_g', 'ev_k_norm_g', 'ev_sgu_norm_g', 'ev_sgu_w', 'ev_sgu_b', 'od_w_in', 'od_q_norm_g',
                'od_kv_norm_g', 'od_w_uq', 'od_w_ukv', 'od_conv_w', 'od_conv_b', 'od_ln_g', 'od_ln_b', 'final_g']
REPL_SMALL = ['norm1_g', 'norm2_g', 'ev_q_norm_g', 'ev_k_norm_g', 'ev_sgu_norm_g', 'ev_sgu_w', 'ev_sgu_b',
              'od_kv_norm_g', 'final_g']
SHARD_SMALL = ['od_q_norm_g', 'od_conv_w', 'od_conv_b', 'od_ln_g', 'od_ln_b']
BIG = ['w_out', 'mlp_w1', 'mlp_w2', 'ev_w_in', 'od_w_in', 'od_w_uq', 'od_w_ukv']


def _gather_last(parts):
    return jnp.concatenate([parts[k] for k in range(4)], axis=-1)


class _Reduce:
    def __init__(self, tag, named, half, where):
        self.tag, self.half, self.where = tag, half, where
        self.names, self.hs = zip(*named)
        self.hs = list(self.hs)

    def to_sibling(self):
        lands = [lax.empty(a.shape[1:], BF) for a in self.hs]
        (self.h1,), token = exchange_start(f"rs_sibling_start_{self.tag}", [(self.hs, lands)], TO_SIBLING)
        return token

    def to_chips(self, after):
        hs, got = exchange_wait(f"rs_sibling_wait_{self.tag}", self.h1, after, TO_SIBLING)
        pair = [add_pairs(f"rs_add_{self.tag}_{k}", a, b, self.half) for k, (a, b) in enumerate(zip(hs, got))]
        lands = [lax.empty(p.shape, BF) for p in pair]
        (self.h2,), token = exchange_start(f"rs_chips_start_{self.tag}", [(pair, lands)], SCATTER)
        return token

    def finish(self, after, bufs):
        pair, land = exchange_wait(f"rs_chips_wait_{self.tag}", self.h2, after, SCATTER)
        for k, ((n, idx), l, p) in enumerate(zip(self.names, land, pair)):
            bufs[n] = sum_slabs(f"rs_sum_{self.tag}_{k}", l, p, self.where, bufs[n], idx)


def kernel(x, c, ctx, c_ctx, ada_w, ada_b, norm1_g, norm2_g, w_out, mlp_w1, mlp_w2, ev_w_in, ev_q_norm_g, ev_k_norm_g, ev_sgu_norm_g, ev_sgu_w, ev_sgu_b, od_w_in, od_q_norm_g, od_kv_norm_g, od_w_uq, od_w_ukv, od_conv_w, od_conv_b, od_ln_g, od_ln_b, final_g, loss_target, m_c_ctx, m_ada_w, m_ada_b, m_norm1_g, m_norm2_g, m_w_out, m_mlp_w1, m_mlp_w2, m_ev_w_in, m_ev_q_norm_g, m_ev_k_norm_g, m_ev_sgu_norm_g, m_ev_sgu_w, m_ev_sgu_b, m_od_w_in, m_od_q_norm_g, m_od_kv_norm_g, m_od_w_uq, m_od_w_ukv, m_od_conv_w, m_od_conv_b, m_od_ln_g, m_od_ln_b, m_final_g, v_c_ctx, v_ada_w, v_ada_b, v_norm1_g, v_norm2_g, v_w_out, v_mlp_w1, v_mlp_w2, v_ev_w_in, v_ev_q_norm_g, v_ev_k_norm_g, v_ev_sgu_norm_g, v_ev_sgu_w, v_ev_sgu_b, v_od_w_in, v_od_q_norm_g, v_od_kv_norm_g, v_od_w_uq, v_od_w_ukv, v_od_conv_w, v_od_conv_b, v_od_ln_g, v_od_ln_b, v_final_g):
    w = dict(c_ctx=c_ctx, ada_w=ada_w, ada_b=ada_b, norm1_g=norm1_g, norm2_g=norm2_g, w_out=w_out, mlp_w1=mlp_w1,
             mlp_w2=mlp_w2, ev_w_in=ev_w_in, ev_q_norm_g=ev_q_norm_g, ev_k_norm_g=ev_k_norm_g,
             ev_sgu_norm_g=ev_sgu_norm_g, ev_sgu_w=ev_sgu_w, ev_sgu_b=ev_sgu_b, od_w_in=od_w_in,
             od_q_norm_g=od_q_norm_g, od_kv_norm_g=od_kv_norm_g, od_w_uq=od_w_uq, od_w_ukv=od_w_ukv,
             od_conv_w=od_conv_w, od_conv_b=od_conv_b, od_ln_g=od_ln_g, od_ln_b=od_ln_b, final_g=final_g)
    mom = dict(c_ctx=m_c_ctx, ada_w=m_ada_w, ada_b=m_ada_b, norm1_g=m_norm1_g, norm2_g=m_norm2_g, w_out=m_w_out,
               mlp_w1=m_mlp_w1, mlp_w2=m_mlp_w2, ev_w_in=m_ev_w_in, ev_q_norm_g=m_ev_q_norm_g,
               ev_k_norm_g=m_ev_k_norm_g, ev_sgu_norm_g=m_ev_sgu_norm_g, ev_sgu_w=m_ev_sgu_w, ev_sgu_b=m_ev_sgu_b,
               od_w_in=m_od_w_in, od_q_norm_g=m_od_q_norm_g, od_kv_norm_g=m_od_kv_norm_g, od_w_uq=m_od_w_uq,
               od_w_ukv=m_od_w_ukv, od_conv_w=m_od_conv_w, od_conv_b=m_od_conv_b, od_ln_g=m_od_ln_g,
               od_ln_b=m_od_ln_b, final_g=m_final_g)
    var = dict(c_ctx=v_c_ctx, ada_w=v_ada_w, ada_b=v_ada_b, norm1_g=v_norm1_g, norm2_g=v_norm2_g, w_out=v_w_out,
               mlp_w1=v_mlp_w1, mlp_w2=v_mlp_w2, ev_w_in=v_ev_w_in, ev_q_norm_g=v_ev_q_norm_g,
               ev_k_norm_g=v_ev_k_norm_g, ev_sgu_norm_g=v_ev_sgu_norm_g, ev_sgu_w=v_ev_sgu_w, ev_sgu_b=v_ev_sgu_b,
               od_w_in=v_od_w_in, od_q_norm_g=v_od_q_norm_g, od_kv_norm_g=v_od_kv_norm_g, od_w_uq=v_od_w_uq,
               od_w_ukv=v_od_w_ukv, od_conv_w=v_od_conv_w, od_conv_b=v_od_conv_b, od_ln_g=v_od_ln_g,
               od_ln_b=v_od_ln_b, final_g=v_final_g)
    xi, yi, ci = lax.axis_index("x"), lax.axis_index("y"), lax.axis_index("c")
    chip = 2 * xi + yi
    dev = 2 * chip + ci

    shard_shapes = [w[n].shape for n in SHARD_SMALL]
    g0 = all_gather8("ag_small", _pack([c] + [w[n] for n in SHARD_SMALL]))
    g0 = g0.reshape(8, -1, D)
    parts = _unpack(g0, [c.shape] + shard_shapes)
    c_all = parts[0].reshape(16, D)
    small_full = {n: _gather_last(p[0::2]) for n, p in zip(SHARD_SMALL, parts[1:])}
    call = jnp.concatenate([c_all, c_ctx.reshape(1, D), jnp.zeros((NC - 17, D), F32)], axis=0)

    cols = ada_w.shape[2]
    ada_b_sh = lax.dynamic_slice(ada_b, (0, chip * cols), (2, cols)).reshape(2, 1, cols)
    mt = mods_fwd(call, ada_w, ada_b_sh)
    mt = all_gather8("ag_mods", mt.reshape(2 * NC, cols)).reshape(8, 2, NC, cols)
    table = mt[0::2].transpose(1, 2, 0, 3).reshape(2, NC, 4 * cols)
    mods = []
    for i in range(2):
        lat = lax.dynamic_slice(table[i], (2 * dev, 0), (2, 4 * cols))
        mc = table[i, 16]
        mods.append(jnp.stack([mc, lat[0], mc, lat[1]]).reshape(4 * N_MOD, 1, D))

    order = [[("ev_w_in", 0)], [("w_out", 0), ("mlp_w1", 0), ("mlp_w2", 0)],
             [("od_w_in", 0), ("od_w_uq", 0), ("od_w_ukv", 0), ("w_out", 1)], [("mlp_w1", 1), ("mlp_w2", 1)]]
    groups = []
    for names in order:
        srcs = [w[n][i].astype(BF) for n, i in names]
        groups.append((srcs, [lax.empty((4,) + s.shape, BF) for s in srcs]))
    groups[0][0][0], table = lax.optimization_barrier((groups[0][0][0], table))
    handles, token = exchange_start("gather_start", groups, GATHER)
    mods[0] = mods[0] + token[0, 0]
    small = {n: w[n] for n in REPL_SMALL}
    small.update(small_full)
    prms = small_params(small)

    chip1 = chip.reshape(1).astype(jnp.int32)

    def arrived(k, after):
        srcs, lands = exchange_wait(f"gather_wait_{k}", handles[k], after, GATHER)
        return [place_own(f"gather_own_{k}_{a}", l, s, chip1) for a, (l, s) in enumerate(zip(lands, srcs))]

    def arrive_ev_in(after):
        (ev,) = arrived(0, after)
        prms[0]["w_in"] = _gather_last(ev)

    def arrive_ev_rest(after):
        wo, w1, w2 = arrived(1, after)
        prms[0].update(w_out=wo.reshape(D, D), w1=w1, w2=w2)

    def arrive_od(after):
        od, uq, ukv, wo = arrived(2, after)
        prms[1].update(odd_in_params(_gather_last(od), _gather_last(uq), _gather_last(ukv)), w_out=wo.reshape(D, D))

    def arrive_od_mlp(after):
        w1, w2 = arrived(3, after)
        prms[1].update(w1=w1, w2=w2)

    prms[0]["w_in"] = arrive_ev_in
    prms[0]["w_out"] = arrive_ev_rest
    prms[1]["w_in"] = arrive_od
    prms[1]["w1"] = arrive_od_mlp

    half = ci.reshape(1).astype(jnp.int32)
    where = jnp.stack([chip, ci]).astype(jnp.int32)
    red = {}

    def hook(point, g, fresh):
        if point == "1:end":
            red["l1"] = _Reduce("l1", layer_grads_hs(1, g, "all"), half, where)
            return red["l1"].to_sibling()
        if point == "0:mlp":
            red["l0_mlp"] = _Reduce("l0_mlp", layer_grads_hs(0, g, "mlp"), half, where)
            return red["l1"].to_chips(fresh) + red["l0_mlp"].to_sibling()
        if point == "0:mid":
            return red["l0_mlp"].to_chips(fresh)
        if point == "0:end":
            red["l0_rest"] = _Reduce("l0_rest", layer_grads_hs(0, g, "rest"), half, where)
            return red["l0_rest"].to_sibling()
        return None

    xcat = jnp.concatenate([ctx, x], axis=1).reshape(R, D)
    loss_p, dx, dmods, grads, dfg, last = local_step(xcat, loss_target.reshape(NEX * L, D), mods, prms,
                                                     final_g.reshape(1, D), hook)
    grad_x = dx.reshape(NEX, L, D)

    sg = small_grads_natural(grads, dfg)
    dm = jnp.stack([d.reshape(4, N_MOD * D) for d in dmods])
    small_names = REPL_SMALL + SHARD_SMALL
    items = [dm[:, 1::2], dm[:, 0] + dm[:, 2]] + [sg[n] for n in small_names] + [loss_p[0:1, 0:1]]
    shapes = [a.shape for a in items]
    g1 = all_gather8("ag_grads", _pack(items), after=last)
    started = red["l0_rest"].to_chips(g1)
    rows1 = g1.shape[0] // 8
    g1 = g1.reshape(8, rows1, D)
    tot = _unpack(sum_lead("sum_small", g1, after=started), shapes)
    dm_lat = _unpack(g1, shapes[:1])[0]
    dm_lat = dm_lat.transpose(1, 0, 2, 3).reshape(2, 16, N_MOD * D)
    dm_all = jnp.concatenate([dm_lat, tot[1][:, None], jnp.zeros((2, NC - 17, N_MOD * D), F32)], axis=1)
    gsum = dict(zip(small_names, tot[2:2 + len(small_names)]))
    loss = tot[-1].reshape(())
    grad = {n: gsum[n].reshape(w[n].shape) for n in REPL_SMALL}
    for n in SHARD_SMALL:
        k = w[n].shape[-1]
        grad[n] = lax.dynamic_slice_in_dim(gsum[n], chip * k, k, axis=gsum[n].ndim - 1)
    grad["ada_b"] = sum_lead("sum_ada_b", dm_all.transpose(1, 0, 2).reshape(NC, 2 * N_MOD, D)).reshape(2, N_MOD * D)

    dm_sh = lax.dynamic_slice(dm_all, (0, 0, chip * cols), (2, NC, cols))
    grad["ada_w"], dcc = ada_bwd(call, ada_w, dm_sh)
    dcc = all_gather8("ag_cctx", dcc).reshape(8, 8, D)
    grad["c_ctx"] = sum_lead("sum_cctx", dcc[0::2])[0]

    delta, new_m, new_v = {}, {}, {}

    def adam_big(n, again):
        shp = w[n].shape
        two_d = (shp[0] * shp[1], shp[2])
        res = adamw(f"adamw_{n}", w[n].reshape(two_d), grad[n].reshape(two_d), mom[n].reshape(two_d),
                    var[n].reshape(two_d), again)
        delta[n], new_m[n], new_v[n] = [a.reshape(shp) for a in res[:3]]
        if again:
            grad[n] = res[3].reshape(shp)

    adam_big('ada_w', False)
    rest = [n for n in WEIGHT_NAMES if n not in ['ada_w'] + BIG]
    rshapes = [w[n].shape for n in rest]
    d_, m_, v_ = adamw("adamw_small", _pack([w[n] for n in rest]), _pack([grad[n] for n in rest]),
                       _pack([mom[n] for n in rest]), _pack([var[n] for n in rest]))
    for dst, buf in ((delta, d_), (new_m, m_), (new_v, v_)):
        dst.update(zip(rest, _unpack(buf, rshapes)))

    bufs = {n: lax.empty((w[n].shape[0], 2, w[n].shape[1] // 2, w[n].shape[2]), F32) for n in BIG}
    for tag, behind in (("l1", delta["ada_w"]), ("l0_mlp", d_), ("l0_rest", d_)):
        red[tag].finish(behind, bufs)
    for n, full in zip(BIG, sibling_merge("rs_sibling_merge", [bufs[n] for n in BIG])):
        grad[n] = full.reshape(w[n].shape)
    for n in BIG:
        adam_big(n, True)

    return (loss, grad_x, *[grad[n] for n in WEIGHT_NAMES], *[delta[n] for n in WEIGHT_NAMES],
            *[new_m[n] for n in WEIGHT_NAMES], *[new_v[n] for n in WEIGHT_NAMES])
```

```python
import functools
import math

import numpy as np
import jax
import jax.numpy as jnp
from jax import lax
from jax.experimental import pallas as pl
from jax.experimental.pallas import tpu as pltpu

F32 = jnp.float32
BF = jnp.bfloat16
HI = lax.Precision.HIGHEST
MESH = pl.DeviceIdType.MESH

D = 1024
L = 2048
LC = 256
SEQ = L + LC
NEX = 2
R = NEX * SEQ
TB = 256
WIDE = 512
BPE = SEQ // TB
NBLK = R // TB
GRID_W = 64
FF = 4 * D
EPS = 1e-6
ROPE_THETA = 10000.0
N_MOD = 6
EV_IN = 1792
OD_IN = 1440
OD_PAD = 1536
VMEM_LIMIT = 60 * 1024 * 1024

ADAM_LR = 0.001
ADAM_B1 = 0.9
ADAM_B2 = 0.999
ADAM_EPS = 1e-08
ADAM_WD = 0.01
ADAM_STEP = 10

NT = (((1,), (1,)), ((), ()))
TN = (((0,), (0,)), ((), ()))


def _cparams(sem=None):
    return pltpu.CompilerParams(dimension_semantics=sem, vmem_limit_bytes=VMEM_LIMIT)


@jax.custom_vjp
def _mm(a, b):
    return jnp.dot(a.astype(BF), b.astype(BF), preferred_element_type=F32)


def _mm_fwd(a, b):
    return _mm(a, b), (a, b)


def _mm_bwd(res, g):
    a, b = res
    gb = g.astype(BF)
    da = lax.dot_general(gb, b.astype(BF), NT, preferred_element_type=F32)
    db = lax.dot_general(a.astype(BF), gb, TN, preferred_element_type=F32)
    return da, db


_mm.defvjp(_mm_fwd, _mm_bwd)


@jax.custom_vjp
def _swap(x):
    n = x.shape[-1]
    ax = x.ndim - 1
    lane = lax.broadcasted_iota(jnp.int32, x.shape, ax)
    return jnp.where(lane % 2 == 0, pltpu.roll(x, n - 1, ax), pltpu.roll(x, 1, ax))


_swap.defvjp(lambda x: (_swap(x), None), lambda _, g: (_swap(g),))


def _rope(x, cos, sin):
    return x * cos + _swap(x) * sin


def _rmsn(x, g):
    return x * lax.rsqrt(jnp.mean(x * x, axis=-1, keepdims=True) + EPS) * g


def _split_dot(a, m):
    hi = a.astype(BF)
    lo = (a - hi.astype(F32)).astype(BF)
    return jnp.dot(hi, m, preferred_element_type=F32) + jnp.dot(lo, m, preferred_element_type=F32)


@jax.custom_vjp
def _group_mean(a, avg):
    return _split_dot(a, avg)


_group_mean.defvjp(lambda a, avg: (_split_dot(a, avg), avg),
                   lambda avg, g: (_split_dot(g, avg), jnp.zeros_like(avg)))


def _grmsn(x, g, avg):
    return x * lax.rsqrt(_group_mean(x * x, avg) + EPS) * g


def _modnorm(x, g, sh, sc):
    return _rmsn(x, g) * (1.0 + sc) + sh


def _gelu(x):
    return 0.5 * x * (1.0 + jnp.tanh(0.7978845608028654 * (x + 0.044715 * (x * x * x))))


def _silu(x):
    return x * jax.nn.sigmoid(x)


def _acc(ref, val, first):
    @pl.when(first)
    def _():
        ref[...] = val

    @pl.when(jnp.logical_not(first))
    def _():
        ref[...] += val


def _seg(i):
    return 2 * (i // BPE) + jnp.minimum(i % BPE, 1)


def _seg_first(i):
    return (i % BPE) <= 1


class _Either:
    def __init__(self, pick_first, first, second):
        self.pick_first, self.first, self.second = pick_first, first, second

    def __getitem__(self, idx):
        return jnp.where(self.pick_first, self.first[idx], self.second[idx])


def _rb_call(name, body, row_in=(), mod_in=(), pos_in=(), full_in=(), shift_in=(),
             row_out=(), seg_out=(), acc_out=(), scratch=(), after=None, col_in=(), rows=TB):
    assert rows == TB or not (mod_in or pos_in or shift_in or seg_out or col_in)
    in_specs, args, pairs = [], [], []
    for a in row_in:
        if isinstance(a, tuple):
            pairs.append(len(args))
            in_specs.append(pl.BlockSpec((TB, a[0].shape[1]), lambda i: (i // BPE, 0)))
            in_specs.append(pl.BlockSpec(
                (TB, a[1].shape[1]), lambda i: ((i // BPE) * (L // TB) + jnp.maximum(i % BPE - 1, 0), 0)))
            args += list(a)
        else:
            in_specs.append(pl.BlockSpec((rows, a.shape[1]), lambda i: (i, 0)))
            args.append(a)
    for a in col_in:
        in_specs.append(pl.BlockSpec((a.shape[0], TB), lambda i: (0, i)))
        args.append(a)
    for tab, m in mod_in:
        in_specs.append(pl.BlockSpec((1, 1, D), lambda i, m=m: (_seg(i) * N_MOD + m, 0, 0)))
        args.append(tab)
    for a in pos_in:
        in_specs.append(pl.BlockSpec((TB, a.shape[1]), lambda i: (i % BPE, 0)))
        args.append(a)
    for a in full_in:
        in_specs.append(pl.BlockSpec(a.shape, lambda i, n=a.ndim: (0,) * n))
        args.append(a)
    for a, d in shift_in:
        in_specs.append(pl.BlockSpec((TB, a.shape[1]), lambda i, d=d: (jnp.clip(i + d, 0, NBLK - 1), 0)))
        args.append(a)
    n_in = len(args)
    if after is not None:
        in_specs.append(pl.BlockSpec(after.shape, lambda i, n=after.ndim: (0,) * n))
        args.append(after)
    out_specs, out_shape = [], []
    for w, dt, *lat in row_out:
        if lat:
            out_specs.append(pl.BlockSpec(
                (TB, w), lambda i: ((i // BPE) * (L // TB) + jnp.maximum(i % BPE - 1, 0), 0)))
            out_shape.append(jax.ShapeDtypeStruct((NEX * L, w), dt))
        else:
            out_specs.append(pl.BlockSpec((rows, w), lambda i: (i, 0)))
            out_shape.append(jax.ShapeDtypeStruct((R, w), dt))
    for w in seg_out:
        out_specs.append(pl.BlockSpec((1, 1, w), lambda i: (_seg(i), 0, 0)))
        out_shape.append(jax.ShapeDtypeStruct((4, 1, w), F32))
    for shp in acc_out:
        out_specs.append(pl.BlockSpec(shp, lambda i, n=len(shp): (0,) * n))
        out_shape.append(jax.ShapeDtypeStruct(shp, F32))

    def kern(*refs):
        i = pl.program_id(0)
        ins = list(refs[:n_in])
        for k in reversed(pairs):
            ins[k:k + 2] = [_Either((i % BPE) == 0, ins[k], ins[k + 1])]
        body(i, *ins, *refs[len(args):])

    sem = ("arbitrary",) if (seg_out or acc_out or any(len(r) > 2 for r in row_out)) else ("parallel",)
    return pl.pallas_call(kern, grid=(R // rows,), in_specs=in_specs, out_specs=out_specs, out_shape=out_shape,
                          scratch_shapes=list(scratch), compiler_params=_cparams(sem), name=name)(*args)


def modnorm_fwd(name, x, mods, g, m_sh, m_sc):
    def body(i, x_ref, sh_ref, sc_ref, g_ref, h_ref):
        h_ref[...] = _modnorm(x_ref[...], g_ref[...], sh_ref[0], sc_ref[0]).astype(BF)

    return _rb_call(name, body, row_in=(x,), mod_in=((mods, m_sh), (mods, m_sc)), full_in=(g,),
                    row_out=((D, BF),))[0]


def _gate_grads(dx, y_ref, gt_ref, dy_ref, dgt_ref, i):
    dy_ref[...] = (dx * gt_ref[0]).astype(BF)
    _acc(dgt_ref, jnp.sum(dx * y_ref[...].astype(F32), axis=0, keepdims=True)[None], _seg_first(i))


def modnorm_bwd(name, x, dh, dx_in, mods, g, m_sh, m_sc, gate=None, after=None, lat_only=False):
    def body(i, x_ref, dh_ref, dxin_ref, *rest):
        if gate:
            y_ref, sh_ref, sc_ref, gt_ref, g_ref, dx_ref, dy_ref, dgt_ref, dsh_ref, dsc_ref, dg_ref = rest
        else:
            sh_ref, sc_ref, g_ref, dx_ref, dsh_ref, dsc_ref, dg_ref = rest
        _, vjp = jax.vjp(_modnorm, x_ref[...], g_ref[...], sh_ref[0], sc_ref[0])
        dx, dg, dsh, dsc = vjp(dh_ref[...].astype(F32))
        dx = dxin_ref[...] + dx
        dx_ref[...] = dx
        if gate:
            _gate_grads(dx, y_ref, gt_ref, dy_ref, dgt_ref, i)
        _acc(dsh_ref, dsh[None], _seg_first(i))
        _acc(dsc_ref, dsc[None], _seg_first(i))
        _acc(dg_ref, dg, i == 0)

    if gate:
        y, gmods, m = gate
        return _rb_call(name, body, row_in=(x, dh, dx_in, y), mod_in=((mods, m_sh), (mods, m_sc), (gmods, m)),
                        full_in=(g,), row_out=((D, F32), (D, BF)), seg_out=(D, D, D), acc_out=((1, D),), after=after)
    return _rb_call(name, body, row_in=(x, dh, dx_in), mod_in=((mods, m_sh), (mods, m_sc)), full_in=(g,),
                    row_out=((D, F32, "lat") if lat_only else (D, F32),), seg_out=(D, D), acc_out=((1, D),),
                    after=after)


def proj_in(name, h, w):
    n = w.shape[1]

    def body(i, h_ref, w_ref, o_ref):
        o_ref[...] = jnp.dot(h_ref[...], w_ref[...], preferred_element_type=F32).astype(BF)

    return _rb_call(name, body, row_in=(h,), full_in=(w,), row_out=((n, BF),), rows=WIDE)[0]


def proj_out(name, a1, a2, w, x, mods, m_gate, g_next, m_sh, m_sc):
    k1 = a1.shape[1]

    def body(i, a1_ref, a2_ref, x_ref, gt_ref, sh_ref, sc_ref, w_ref, g_ref, xo_ref, y_ref, h_ref):
        y = jnp.dot(a1_ref[...], w_ref[:k1, :], preferred_element_type=F32)
        y = y + jnp.dot(a2_ref[...], w_ref[k1:, :], preferred_element_type=F32)
        y_ref[...] = y.astype(BF)
        xn = x_ref[...] + gt_ref[0] * y
        xo_ref[...] = xn
        h_ref[...] = _modnorm(xn, g_ref[...], sh_ref[0], sc_ref[0]).astype(BF)

    return _rb_call(name, body, row_in=(a1, a2, x), mod_in=((mods, m_gate), (mods, m_sh), (mods, m_sc)),
                    full_in=(w, g_next), row_out=((D, F32), (D, BF), (D, BF)))


def mlp_up(name, h, w1):
    def body(i, h_ref, w_ref, a_ref, f_ref):
        hv = h_ref[...]
        for n in range(4):
            a = jnp.dot(hv, w_ref[n], preferred_element_type=F32)
            a_ref[:, n * D:(n + 1) * D] = a.astype(BF)
            r = jnp.maximum(a, 0.0)
            f_ref[:, n * D:(n + 1) * D] = (r * r).astype(BF)

    return _rb_call(name, body, row_in=(h,), full_in=(w1,), row_out=((FF, BF), (FF, BF)), rows=WIDE)


def mlp_down(name, f, w2, x, mods, m_gate, nxt=None):
    def body(i, f_ref, x_ref, gt_ref, *rest):
        if nxt:
            sh_ref, sc_ref, w_ref, g_ref, xo_ref, y_ref, h_ref = rest
        else:
            w_ref, xo_ref, y_ref = rest
        y = jnp.dot(f_ref[:, 0:D], w_ref[0], preferred_element_type=F32)
        for n in range(1, 4):
            y = y + jnp.dot(f_ref[:, n * D:(n + 1) * D], w_ref[n], preferred_element_type=F32)
        xn = x_ref[...] + gt_ref[0] * y
        y_ref[...] = y.astype(BF)
        xo_ref[...] = xn
        if nxt:
            h_ref[...] = _modnorm(xn, g_ref[...], sh_ref[0], sc_ref[0]).astype(BF)

    if nxt:
        return _rb_call(name, body, row_in=(f, x), mod_in=((mods, m_gate), (nxt[0], 0), (nxt[0], 1)),
                        full_in=(w2, nxt[1]), row_out=((D, F32), (D, BF), (D, BF)))
    return _rb_call(name, body, row_in=(f, x), mod_in=((mods, m_gate),), full_in=(w2,),
                    row_out=((D, F32), (D, BF)))


def mm_nt(name, g, w, after=None):
    k = w.shape[0]

    def body(i, g_ref, w_ref, o_ref):
        o_ref[...] = lax.dot_general(g_ref[...], w_ref[...], NT, preferred_element_type=F32).astype(BF)

    return _rb_call(name, body, row_in=(g,), full_in=(w,), row_out=((k, BF),), after=after, rows=WIDE)[0]


def mlp_bwd_da(name, dy, w2, a, after=None):
    def body(i, dy_ref, a_ref, w_ref, da_ref):
        dyv = dy_ref[...]
        for n in range(4):
            df = lax.dot_general(dyv, w_ref[n], NT, preferred_element_type=F32)
            av = a_ref[:, n * D:(n + 1) * D].astype(F32)
            da_ref[:, n * D:(n + 1) * D] = (df * (2.0 * jnp.maximum(av, 0.0))).astype(BF)

    return _rb_call(name, body, row_in=(dy, a), full_in=(w2,), row_out=((FF, BF),), after=after, rows=WIDE)[0]


def mlp_bwd_dh(name, da, w1):
    def body(i, da_ref, w_ref, dh_ref):
        acc = lax.dot_general(da_ref[:, 0:D], w_ref[0], NT, preferred_element_type=F32)
        for n in range(1, 4):
            acc = acc + lax.dot_general(da_ref[:, n * D:(n + 1) * D], w_ref[n], NT, preferred_element_type=F32)
        dh_ref[...] = acc.astype(BF)

    return _rb_call(name, body, row_in=(da,), full_in=(w1,), row_out=((D, BF),), rows=WIDE)[0]


TN_ROWS = 1536


def mm_tn(name, a, g, tiles, th, tw):
    nt = len(tiles)
    acs = jnp.asarray([t[0] for t in tiles], jnp.int32)
    gcs = jnp.asarray([t[1] for t in tiles], jnp.int32)
    nr = R // TN_ROWS

    def kern(ac_ref, gc_ref, a_ref, g_ref, o_ref, acc_ref):
        r = pl.program_id(1)

        @pl.when(r == 0)
        def _():
            acc_ref[...] = jnp.zeros_like(acc_ref)

        acc_ref[...] += lax.dot_general(a_ref[...], g_ref[...], TN, preferred_element_type=F32)

        @pl.when(r == nr - 1)
        def _():
            o_ref[...] = acc_ref[...].astype(BF)

    grid_spec = pltpu.PrefetchScalarGridSpec(
        num_scalar_prefetch=2, grid=(nt, nr),
        in_specs=[pl.BlockSpec((TN_ROWS, th), lambda t, r, ac, gc: (r, ac[t])),
                  pl.BlockSpec((TN_ROWS, tw), lambda t, r, ac, gc: (r, gc[t]))],
        out_specs=pl.BlockSpec((None, th, tw), lambda t, r, ac, gc: (t, 0, 0)),
        scratch_shapes=[pltpu.VMEM((th, tw), F32)])
    return pl.pallas_call(kern, grid_spec=grid_spec, out_shape=jax.ShapeDtypeStruct((nt, th, tw), BF),
                          compiler_params=_cparams(("parallel", "arbitrary")), name=name)(acs, gcs, a, g)


def _even_tok(q, k, zus, zvs, gq, gk, gss, ws, bs, cq, sq, ck, sk, avg, lo, hi):
    avg2 = avg[:128, :128]
    qr = _rope(_grmsn(q, gq, avg), cq, sq) * GQA_SCALE
    kr = _rope(_grmsn(k, gk, avg2), ck, sk)
    ms = []
    for b in range(4):
        v = _grmsn(_gelu(zvs[b]), gss[b], avg2)
        sv = lo * (_mm(ws[2 * b], v) + bs[2 * b]) + hi * (_mm(ws[2 * b + 1], v) + bs[2 * b + 1])
        ms.append(_gelu(zus[b]) * sv)
    return qr, kr, ms


def _even_operands(p_ref, rs, gq_ref, gk_ref, gs_ref, w_ref, b_ref):
    return (p_ref[rs, 0:512].astype(F32), p_ref[rs, 512:640].astype(F32),
            [p_ref[rs, 768 + 128 * b:896 + 128 * b].astype(F32) for b in range(4)],
            [p_ref[rs, 1280 + 128 * b:1408 + 128 * b].astype(F32) for b in range(4)],
            gq_ref[...], gk_ref[...], [gs_ref[:, 128 * b:128 * b + 128] for b in range(4)],
            [w_ref[g] for g in range(8)], [b_ref[g] for g in range(8)])


def even_tok_fwd(p, cos, sin, gq, gk, gs, sgu_w, sgu_b, avg, masks):
    def body(i, p_ref, cos_ref, sin_ref, gq_ref, gk_ref, gs_ref, w_ref, b_ref, avg_ref, mk_ref, q_ref, kv_ref, m_ref):
        avgv, lo, hi = avg_ref[...], mk_ref[0, :, 0:128], mk_ref[1, :, 0:128]
        for c in range(2):
            rs = pl.ds(c * 128, 128)
            qr, kr, ms = _even_tok(*_even_operands(p_ref, rs, gq_ref, gk_ref, gs_ref, w_ref, b_ref),
                                   cos_ref[rs, :], sin_ref[rs, :], cos_ref[rs, 0:128], sin_ref[rs, 0:128],
                                   avgv, lo, hi)
            q_ref[rs, :] = qr.astype(BF)
            kv_ref[rs, 0:128] = kr.astype(BF)
            kv_ref[rs, 128:256] = p_ref[rs, 640:768]
            for b in range(4):
                m_ref[rs, 128 * b:128 * b + 128] = ms[b].astype(BF)

    return _rb_call("even_tok_fwd", body, row_in=(p,), pos_in=(cos, sin),
                    full_in=(gq, gk, gs, sgu_w, sgu_b, avg, masks), row_out=((512, BF), (256, BF), (512, BF)))


def even_tok_bwd(p, dq, dkvt, dcat, cos, sin, gq, gk, gs, sgu_w, sgu_b, avg, masks):
    def body(i, p_ref, dq_ref, dcat_ref, dkvt_ref, cos_ref, sin_ref, gq_ref, gk_ref, gs_ref, w_ref, b_ref,
             avg_ref, mk_ref, dp_ref, dgq_ref, dgk_ref, dgs_ref, dw_ref, db_ref):
        avgv, lo, hi = avg_ref[...], mk_ref[0, :, 0:128], mk_ref[1, :, 0:128]
        tot = None
        for c in range(2):
            rs = pl.ds(c * 128, 128)
            cq, sq, ck, sk = cos_ref[rs, :], sin_ref[rs, :], cos_ref[rs, 0:128], sin_ref[rs, 0:128]

            def f(q, k, zus, zvs, gq, gk, gss, ws, bs):
                return _even_tok(q, k, zus, zvs, gq, gk, gss, ws, bs, cq, sq, ck, sk, avgv, lo, hi)

            _, vjp = jax.vjp(f, *_even_operands(p_ref, rs, gq_ref, gk_ref, gs_ref, w_ref, b_ref))
            dk = dkvt_ref[0:128, c * 128:(c + 1) * 128].T
            dv = dkvt_ref[128:256, c * 128:(c + 1) * 128].T
            dms = [dcat_ref[rs, 512 + 128 * b:640 + 128 * b].astype(F32) for b in range(4)]
            d = vjp((dq_ref[rs, :].astype(F32), dk, dms))
            dp_ref[rs, 0:512] = d[0].astype(BF)
            dp_ref[rs, 512:640] = d[1].astype(BF)
            dp_ref[rs, 640:768] = dv.astype(BF)
            for b in range(4):
                dp_ref[rs, 768 + 128 * b:896 + 128 * b] = d[2][b].astype(BF)
                dp_ref[rs, 1280 + 128 * b:1408 + 128 * b] = d[3][b].astype(BF)
            part = [d[4], d[5]] + list(d[6]) + list(d[7]) + list(d[8])
            tot = part if tot is None else [x + y for x, y in zip(tot, part)]
        refs = ([dgq_ref, dgk_ref] + [dgs_ref.at[:, 128 * b:128 * b + 128] for b in range(4)]
                + [dw_ref.at[g] for g in range(8)] + [db_ref.at[g] for g in range(8)])
        for ref, val in zip(refs, tot):
            _acc(ref, val, i == 0)

    return _rb_call("even_tok_bwd", body, row_in=(p, dq, dcat), col_in=(dkvt,), pos_in=(cos, sin),
                    full_in=(gq, gk, gs, sgu_w, sgu_b, avg, masks), row_out=((EV_IN, BF),),
                    acc_out=((1, 512), (1, 128), (1, 512), (8, 128, 128), (8, 128, 1)))


MLA_SCALE = 96 ** -0.5
GQA_SCALE = 64 ** -0.5


def _odd_tok(cq, ckv, kr, za, zg, gq, gkv, wq, wkk, wkv, spread, cr, sr, ck, sk):
    cqn = _rmsn(cq, gq)
    q = _rope(_mm(cqn, wq), cr, sr) * MLA_SCALE
    ckn = _rmsn(ckv, gkv)
    k = _mm(ckn, wkk) + _mm(_rope(kr, ck, sk), spread)
    v = _mm(ckn, wkv)
    y = za * jax.nn.sigmoid(zg)
    return q, k, v, y


def odd_tok_fwd(p, cos, sin, gq, gkv, wq, wkk, wkv, spread):
    def body(i, p_ref, cos_ref, sin_ref, gq_ref, gkv_ref, wq_ref, wkk_ref, wkv_ref, sp_ref, q_ref, kv_ref, y_ref):
        q, k, v, y = _odd_tok(
            p_ref[:, 0:256].astype(F32), p_ref[:, 256:384].astype(F32), p_ref[:, 384:512].astype(F32),
            p_ref[:, 512:1024].astype(F32), p_ref[:, 1024:1536].astype(F32),
            gq_ref[...], gkv_ref[...], wq_ref[...], wkk_ref[...], wkv_ref[...], sp_ref[...],
            cos_ref[:, 0:768], sin_ref[:, 0:768], cos_ref[:, 768:896], sin_ref[:, 768:896])
        q_ref[...] = q.astype(BF)
        kv_ref[:, 0:768] = k.astype(BF)
        kv_ref[:, 768:1280] = v.astype(BF)
        y_ref[...] = y.astype(BF)

    return _rb_call("odd_tok_fwd", body, row_in=(p,), pos_in=(cos, sin), full_in=(gq, gkv, wq, wkk, wkv, spread),
                    row_out=((768, BF), (1280, BF), (512, BF)))


def odd_tok_bwd(p, dq, dkvt, dy, cos, sin, gq, gkv, wq, wkk, wkv, spread):
    def body(i, p_ref, dq_ref, dy_ref, dkvt_ref, cos_ref, sin_ref, gq_ref, gkv_ref, wq_ref, wkk_ref, wkv_ref, sp_ref,
             dp_ref, dgq_ref, dgkv_ref, dwq_ref, dwkk_ref, dwkv_ref):
        cr, sr, ck, sk = cos_ref[:, 0:768], sin_ref[:, 0:768], cos_ref[:, 768:896], sin_ref[:, 768:896]
        spread_v = sp_ref[...]

        def f(cq, ckv, kr, za, zg, gq, gkv, wq, wkk, wkv):
            return _odd_tok(cq, ckv, kr, za, zg, gq, gkv, wq, wkk, wkv, spread_v, cr, sr, ck, sk)

        _, vjp = jax.vjp(f, p_ref[:, 0:256].astype(F32), p_ref[:, 256:384].astype(F32),
                         p_ref[:, 384:512].astype(F32), p_ref[:, 512:1024].astype(F32),
                         p_ref[:, 1024:1536].astype(F32), gq_ref[...], gkv_ref[...], wq_ref[...],
                         wkk_ref[...], wkv_ref[...])
        d = vjp((dq_ref[...].astype(F32), dkvt_ref[0:768, :].T, dkvt_ref[768:1280, :].T, dy_ref[...].astype(F32)))
        dp_ref[:, 0:256] = d[0].astype(BF)
        dp_ref[:, 256:384] = d[1].astype(BF)
        dp_ref[:, 384:512] = d[2].astype(BF)
        dp_ref[:, 512:1024] = d[3].astype(BF)
        dp_ref[:, 1024:1536] = d[4].astype(BF)
        for ref, val in zip((dgq_ref, dgkv_ref, dwq_ref, dwkk_ref, dwkv_ref), d[5:]):
            _acc(ref, val, i == 0)

    return _rb_call("odd_tok_bwd", body, row_in=(p, dq, dy), col_in=(dkvt,), pos_in=(cos, sin),
                    full_in=(gq, gkv, wq, wkk, wkv, spread), row_out=((OD_PAD, BF),),
                    acc_out=((1, 256), (1, 128), (256, 768), (128, 768), (128, 512)))


GQA_HEADS = [(64 * h, 64 * (h // 4), 64, 128 + 64 * (h // 4)) for h in range(8)]
MLA_HEADS = [(96 * h, 96 * h, 96, 768 + 64 * h) for h in range(8)]


def _by_block(j, run):
    @pl.when(j == 0)
    def _():
        run(LC)

    @pl.when(j > 0)
    def _():
        run(SEQ)


def attn_fwd(name, q, kv, heads):
    qw, kvw = q.shape[1], kv.shape[1]

    def kern(q_ref, kv_ref, o_ref, lse_ref):
        def run(nk):
            for h, (qo, ko, w, vo) in enumerate(heads):
                s = lax.dot_general(q_ref[:, qo:qo + w], kv_ref[0:nk, ko:ko + w], NT, preferred_element_type=F32)
                m = jnp.max(s, axis=-1, keepdims=True)
                p = jnp.exp(s - m)
                l = jnp.sum(p, axis=-1, keepdims=True)
                o = jnp.dot(p.astype(BF), kv_ref[0:nk, vo:vo + 64], preferred_element_type=F32) / l
                o_ref[:, 64 * h:64 * h + 64] = o.astype(BF)
                lse_ref[:, h:h + 1] = m + jnp.log(l)

        _by_block(pl.program_id(1), run)

    return pl.pallas_call(
        kern, grid=(NEX, BPE),
        in_specs=[pl.BlockSpec((TB, qw), lambda e, j: (e * BPE + j, 0)),
                  pl.BlockSpec((SEQ, kvw), lambda e, j: (e, 0))],
        out_specs=[pl.BlockSpec((TB, 512), lambda e, j: (e * BPE + j, 0)),
                   pl.BlockSpec((TB, 8), lambda e, j: (e * BPE + j, 0))],
        out_shape=[jax.ShapeDtypeStruct((R, 512), BF), jax.ShapeDtypeStruct((R, 8), F32)],
        compiler_params=_cparams(("parallel", "arbitrary")), name=name)(q, kv)


def attn_bwd(name, q, kv, o, dcat, lse, heads):
    qw, kvw = q.shape[1], kv.shape[1]

    def kern(q_ref, kv_ref, o_ref, do_ref, lse_ref, dq_ref, dkvt_ref):
        j = pl.program_id(1)

        @pl.when(j == 0)
        def _():
            dkvt_ref[...] = jnp.zeros_like(dkvt_ref)

        def run(nk):
            for h, (qo, ko, w, vo) in enumerate(heads):
                qh = q_ref[:, qo:qo + w]
                kh = kv_ref[0:nk, ko:ko + w]
                s = lax.dot_general(qh, kh, NT, preferred_element_type=F32)
                p = jnp.exp(s - lse_ref[:, h:h + 1])
                do = do_ref[:, 64 * h:64 * h + 64]
                dsum = jnp.sum(do.astype(F32) * o_ref[:, 64 * h:64 * h + 64].astype(F32), axis=-1, keepdims=True)
                dp = lax.dot_general(do, kv_ref[0:nk, vo:vo + 64], NT, preferred_element_type=F32)
                ds = (p * (dp - dsum)).astype(BF)
                dkvt_ref[vo:vo + 64, 0:nk] += lax.dot_general(do, p.astype(BF), TN, preferred_element_type=F32)
                dq_ref[:, qo:qo + w] = jnp.dot(ds, kh, preferred_element_type=F32).astype(BF)
                dkvt_ref[ko:ko + w, 0:nk] += lax.dot_general(qh, ds, TN, preferred_element_type=F32)

        _by_block(j, run)

    return pl.pallas_call(
        kern, grid=(NEX, BPE),
        in_specs=[pl.BlockSpec((TB, qw), lambda e, j: (e * BPE + j, 0)),
                  pl.BlockSpec((SEQ, kvw), lambda e, j: (e, 0)),
                  pl.BlockSpec((TB, 512), lambda e, j: (e * BPE + j, 0)),
                  pl.BlockSpec((TB, 512), lambda e, j: (e * BPE + j, 0)),
                  pl.BlockSpec((TB, 8), lambda e, j: (e * BPE + j, 0))],
        out_specs=[pl.BlockSpec((TB, qw), lambda e, j: (e * BPE + j, 0)),
                   pl.BlockSpec((kvw, SEQ), lambda e, j: (0, e))],
        out_shape=[jax.ShapeDtypeStruct((R, qw), BF), jax.ShapeDtypeStruct((kvw, R), F32)],
        compiler_params=_cparams(("parallel", "arbitrary")), name=name)(q, kv, o, dcat, lse)


HALO = 16
CONV_K = 31


def _fill_ext(ext_ref, prev_ref, cur_ref, next_ref, i):
    j = i % BPE
    has_prev = (j >= 2).astype(F32)
    has_next = jnp.logical_and(j >= 1, j <= BPE - 2).astype(F32)
    ext_ref[0:HALO, :] = prev_ref[TB - HALO:TB, :].astype(F32) * has_prev
    ext_ref[HALO:HALO + TB, :] = cur_ref[...].astype(F32)
    ext_ref[HALO + TB:2 * HALO + TB, :] = next_ref[0:HALO, :].astype(F32) * has_next


PHASE_ROWS = TB + 24


def _phases(ext_ref, ph_ref):
    for r in range(8):
        ph_ref[r] = ext_ref[r:r + PHASE_ROWS, :]


def _window(ph_ref, off):
    return ph_ref[off % 8, 8 * (off // 8):8 * (off // 8) + TB, :]


def _ln_silu(z, g, b):
    mu = jnp.mean(z, axis=-1, keepdims=True)
    zc = z - mu
    var = jnp.mean(zc * zc, axis=-1, keepdims=True)
    return _silu(zc * lax.rsqrt(var + EPS) * g + b)


def conf_fwd(y, cw, cb, lg, lb):
    def body(i, cur_ref, cw_ref, cb_ref, lg_ref, lb_ref, prev_ref, next_ref, z_ref, c_ref, ext_ref, ph_ref):
        _fill_ext(ext_ref, prev_ref, cur_ref, next_ref, i)
        _phases(ext_ref, ph_ref)
        acc = _window(ph_ref, 1) * cw_ref[0:1, :]
        for k in range(1, CONV_K):
            acc = acc + _window(ph_ref, k + 1) * cw_ref[k:k + 1, :]
        z = acc + cb_ref[...]
        z_ref[...] = z.astype(BF)
        c_ref[...] = _ln_silu(z, lg_ref[...], lb_ref[...]).astype(BF)

    return _rb_call("conf_fwd", body, row_in=(y,), full_in=(cw, cb, lg, lb), shift_in=((y, -1), (y, 1)),
                    row_out=((512, BF), (512, BF)),
                    scratch=(pltpu.VMEM((TB + 2 * HALO, 512), F32), pltpu.VMEM((8, PHASE_ROWS, 512), F32)))


def conf_bwd_ln(z, dcat, lg, lb):
    def body(i, z_ref, dcat_ref, lg_ref, lb_ref, dz_ref, dlg_ref, dlb_ref, dcb_ref):
        _, vjp = jax.vjp(_ln_silu, z_ref[...].astype(F32), lg_ref[...], lb_ref[...])
        dz, dlg, dlb = vjp(dcat_ref[:, 512:1024].astype(F32))
        dz_ref[...] = dz.astype(BF)
        _acc(dlg_ref, dlg, i == 0)
        _acc(dlb_ref, dlb, i == 0)
        _acc(dcb_ref, jnp.sum(dz, axis=0, keepdims=True), i == 0)

    return _rb_call("conf_bwd_ln", body, row_in=(z, dcat), full_in=(lg, lb), row_out=((512, BF),),
                    acc_out=((1, 512), (1, 512), (1, 512)))


def conf_bwd_conv(y, dz, cw):
    def body(i, y_ref, dz_ref, cw_ref, yp_ref, yn_ref, dzp_ref, dzn_ref, dy_ref, dcw_ref, ext_ref, phy_ref, phd_ref):
        _fill_ext(ext_ref, yp_ref, y_ref, yn_ref, i)
        _phases(ext_ref, phy_ref)
        _fill_ext(ext_ref, dzp_ref, dz_ref, dzn_ref, i)
        _phases(ext_ref, phd_ref)
        dzv = dz_ref[...].astype(F32)

        @pl.when(i == 0)
        def _():
            dcw_ref[...] = jnp.zeros_like(dcw_ref)

        acc = None
        for k in range(CONV_K):
            t = _window(phd_ref, CONV_K - k) * cw_ref[k:k + 1, :]
            acc = t if acc is None else acc + t
            dcw_ref[k:k + 1, :] += jnp.sum(dzv * _window(phy_ref, k + 1), axis=0, keepdims=True)
        dy_ref[...] = acc.astype(BF)

    return _rb_call("conf_bwd_conv", body, row_in=(y, dz), full_in=(cw,),
                    shift_in=((y, -1), (y, 1), (dz, -1), (dz, 1)), row_out=((512, BF),), acc_out=((32, 512),),
                    scratch=(pltpu.VMEM((TB + 2 * HALO, 512), F32), pltpu.VMEM((8, PHASE_ROWS, 512), F32),
                             pltpu.VMEM((8, PHASE_ROWS, 512), F32)))


def final_loss(x, target, fg, y, mods, m_gate):
    lpb = L // TB

    def kern(x_ref, t_ref, g_ref, y_ref, gt_ref, dx_ref, dy_ref, dgt_ref, loss_ref, dg_ref):
        i = pl.program_id(0)
        lat = (i % BPE) >= 1
        xv, tv = x_ref[...], t_ref[...]

        def f(x, g):
            err = _rmsn(x, g) - tv
            rowsum = jnp.sum(err * err, axis=-1, keepdims=True)
            return jnp.sum(rowsum, axis=0, keepdims=True) * (0.5 / D)

        lv, vjp = jax.vjp(f, xv, g_ref[...])
        dx, dg = vjp(jnp.ones((1, 1), F32))
        m = lat.astype(F32)
        dx = dx * m
        dx_ref[...] = dx
        _gate_grads(dx, y_ref, gt_ref, dy_ref, dgt_ref, i)
        _acc(loss_ref, jnp.zeros((8, 128), F32) + lv * m, i == 0)
        _acc(dg_ref, dg * m, i == 0)

    row = pl.BlockSpec((TB, D), lambda i: (i, 0))
    return pl.pallas_call(
        kern, grid=(NBLK,),
        in_specs=[row, pl.BlockSpec((TB, D), lambda i: ((i // BPE) * lpb + jnp.maximum(i % BPE - 1, 0), 0)),
                  pl.BlockSpec((1, D), lambda i: (0, 0)), row,
                  pl.BlockSpec((1, 1, D), lambda i: (_seg(i) * N_MOD + m_gate, 0, 0))],
        out_specs=[row, row, pl.BlockSpec((1, 1, D), lambda i: (_seg(i), 0, 0)),
                   pl.BlockSpec((8, 128), lambda i: (0, 0)), pl.BlockSpec((1, D), lambda i: (0, 0))],
        out_shape=[jax.ShapeDtypeStruct((R, D), F32), jax.ShapeDtypeStruct((R, D), BF),
                   jax.ShapeDtypeStruct((4, 1, D), F32), jax.ShapeDtypeStruct((8, 128), F32),
                   jax.ShapeDtypeStruct((1, D), F32)],
        compiler_params=_cparams(("arbitrary",)), name="final_loss")(x, target, fg, y, mods)


NC = 24


def mods_fwd(call, ada_w, ada_b):
    cols = ada_w.shape[2]

    def kern(c_ref, w_ref, b_ref, o_ref):
        o_ref[...] = jnp.dot(_silu(c_ref[...]), w_ref[...], precision=HI, preferred_element_type=F32) + b_ref[...]

    return pl.pallas_call(
        kern, grid=(2,),
        in_specs=[pl.BlockSpec((NC, D), lambda l: (0, 0)), pl.BlockSpec((None, D, cols), lambda l: (l, 0, 0)),
                  pl.BlockSpec((None, 1, cols), lambda l: (l, 0, 0))],
        out_specs=pl.BlockSpec((None, NC, cols), lambda l: (l, 0, 0)),
        out_shape=jax.ShapeDtypeStruct((2, NC, cols), F32),
        compiler_params=_cparams(("parallel",)), name="mods_fwd")(call, ada_w, ada_b)


def ada_bwd(call, ada_w, dm):
    cols = ada_w.shape[2]

    def kern(c_ref, w_ref, dm_ref, gw_ref, dc_ref):
        l = pl.program_id(0)
        gw_ref[...] = lax.dot_general(_silu(c_ref[...]), dm_ref[...], TN, precision=HI, preferred_element_type=F32)
        part = lax.dot_general(dm_ref[16:24, :], w_ref[...], NT, precision=HI, preferred_element_type=F32)
        cc = c_ref[16:17, :]
        sg = jax.nn.sigmoid(cc)
        _acc(dc_ref, part * (sg * (1.0 + cc * (1.0 - sg))), l == 0)

    return pl.pallas_call(
        kern, grid=(2,),
        in_specs=[pl.BlockSpec((NC, D), lambda l: (0, 0)), pl.BlockSpec((None, D, cols), lambda l: (l, 0, 0)),
                  pl.BlockSpec((None, NC, cols), lambda l: (l, 0, 0))],
        out_specs=[pl.BlockSpec((None, D, cols), lambda l: (l, 0, 0)), pl.BlockSpec((8, D), lambda l: (0, 0))],
        out_shape=[jax.ShapeDtypeStruct((2, D, cols), F32), jax.ShapeDtypeStruct((8, D), F32)],
        compiler_params=_cparams(("arbitrary",)), name="ada_bwd")(call, ada_w, dm)


def sum_lead(name, a, after=None):
    n, r, c = a.shape
    tr = r
    for cand in (512, 256, 128, 64, 32, 16, 8):
        if r % cand == 0 and cand * c * 4 * n <= 8 * 1024 * 1024:
            tr = cand
            break
    extra = [] if after is None else [after]

    def kern(a_ref, *rest):
        acc = a_ref[0].astype(F32)
        for k in range(1, n):
            acc = acc + a_ref[k].astype(F32)
        rest[-1][...] = acc

    return pl.pallas_call(
        kern, grid=(r // tr,),
        in_specs=[pl.BlockSpec((n, tr, c), lambda i: (0, i, 0))]
        + [pl.BlockSpec(e.shape, lambda i, k=e.ndim: (0,) * k) for e in extra],
        out_specs=pl.BlockSpec((tr, c), lambda i: (i, 0)), out_shape=jax.ShapeDtypeStruct((r, c), F32),
        compiler_params=_cparams(("parallel",)), name=name)(a, *extra)


def add_pairs(name, hs, got, half):
    _, _, r, c = hs.shape

    def kern(half_ref, a_ref, b_ref, o_ref):
        o_ref[...] = (a_ref[...].astype(F32) + b_ref[...].astype(F32)).astype(BF)

    spec = pl.BlockSpec((None, r, c), lambda j, h: (j, 0, 0))
    grid_spec = pltpu.PrefetchScalarGridSpec(
        num_scalar_prefetch=1, grid=(4,),
        in_specs=[pl.BlockSpec((None, None, r, c), lambda j, h: (h[0], j, 0, 0)), spec], out_specs=spec)
    return pl.pallas_call(kern, grid_spec=grid_spec, out_shape=jax.ShapeDtypeStruct(got.shape, BF),
                          compiler_params=_cparams(("parallel",)), name=name)(half, hs, got)


def sum_slabs(name, land, own, where, full, lead):
    _, r, c = land.shape
    tr = r
    for cand in (512, 256, 128, 64, 32, 16):
        if r % cand == 0 and cand * c * 16 <= 4 * 1024 * 1024:
            tr = cand
            break

    def kern(where_ref, full_ref, land_ref, own_ref, o_ref):
        me = where_ref[0]
        acc = None
        for k in range(4):
            t = jnp.where(me == k, own_ref[k], land_ref[k]).astype(F32)
            acc = t if acc is None else acc + t
        o_ref[...] = acc

    spec = pl.BlockSpec((4, tr, c), lambda i, m: (0, i, 0))
    grid_spec = pltpu.PrefetchScalarGridSpec(
        num_scalar_prefetch=1, grid=(r // tr,), in_specs=[pl.BlockSpec(memory_space=pl.ANY), spec, spec],
        out_specs=pl.BlockSpec((None, None, tr, c), lambda i, m: (lead, m[1], i, 0)))
    return pl.pallas_call(kern, grid_spec=grid_spec, out_shape=jax.ShapeDtypeStruct(full.shape, F32),
                          input_output_aliases={1: 0}, compiler_params=_cparams(("parallel",)),
                          name=name)(where, full, land, own)


def adamw(name, w, g, m, v, again=False):
    r, c = w.shape
    tr = r
    for cand in (512, 256, 128, 64, 32, 16, 8):
        if r % cand == 0 and cand * c * 4 <= 2 * 1024 * 1024:
            tr = cand
            break
    c1 = 1.0 / (1.0 - ADAM_B1 ** ADAM_STEP)
    c2 = 1.0 / (1.0 - ADAM_B2 ** ADAM_STEP)

    def kern(w_ref, g_ref, m_ref, v_ref, d_ref, mo_ref, vo_ref, *go_ref):
        gv = g_ref[...]
        mn = ADAM_B1 * m_ref[...] + (1.0 - ADAM_B1) * gv
        vn = ADAM_B2 * v_ref[...] + (1.0 - ADAM_B2) * (gv * gv)
        d_ref[...] = -ADAM_LR * ((mn * c1) / (jnp.sqrt(vn * c2) + ADAM_EPS) + ADAM_WD * w_ref[...])
        mo_ref[...] = mn
        vo_ref[...] = vn
        if again:
            go_ref[0][...] = gv

    spec = pl.BlockSpec((tr, c), lambda i: (i, 0))
    shp = jax.ShapeDtypeStruct((r, c), F32)
    n_out = 4 if again else 3
    return pl.pallas_call(kern, grid=(r // tr,), in_specs=[spec] * 4, out_specs=[spec] * n_out,
                          out_shape=[shp] * n_out, compiler_params=_cparams(("parallel",)), name=name)(w, g, m, v)


def adamw_many(name, ws, gs, ms, vs):
    n = len(ws)
    c1 = 1.0 / (1.0 - ADAM_B1 ** ADAM_STEP)
    c2 = 1.0 / (1.0 - ADAM_B2 ** ADAM_STEP)

    def kern(*refs):
        w, g, m, v, d, mo, vo = (refs[k * n:(k + 1) * n] for k in range(7))
        for k in range(n):
            gv = g[k][...]
            mn = ADAM_B1 * m[k][...] + (1.0 - ADAM_B1) * gv
            vn = ADAM_B2 * v[k][...] + (1.0 - ADAM_B2) * (gv * gv)
            d[k][...] = -ADAM_LR * ((mn * c1) / (jnp.sqrt(vn * c2) + ADAM_EPS) + ADAM_WD * w[k][...])
            mo[k][...] = mn
            vo[k][...] = vn

    out = pl.pallas_call(kern, out_shape=[jax.ShapeDtypeStruct(a.shape, F32) for a in ws] * 3,
                         compiler_params=pltpu.CompilerParams(vmem_limit_bytes=VMEM_LIMIT),
                         name=name)(*ws, *gs, *ms, *vs)
    return out[:n], out[n:2 * n], out[2 * n:]


def all_gather8(name, xs, after=None):
    m_per, n = xs.shape
    extra = [] if after is None else [after]

    def body(x_ref, *rest):
        out_ref, send_sems, recv_sems, local_sem = rest[len(extra):]
        x, y, c = lax.axis_index("x"), lax.axis_index("y"), lax.axis_index("c")
        me, sibling = (x, y, c), (x, y, 1 - c)
        chips = [(1 - x, y), (x, 1 - y), (1 - x, 1 - y)]

        def rows(px, py, pc):
            return out_ref.at[pl.ds((4 * px + 2 * py + pc) * m_per, m_per), :]

        def copy(k, block, to, src=None):
            return pltpu.make_async_remote_copy(
                src_ref=rows(*block) if src is None else src, dst_ref=rows(*block),
                send_sem=send_sems.at[k], recv_sem=recv_sems.at[k], device_id=to, device_id_type=MESH)

        mine = pltpu.make_async_copy(x_ref, rows(*me), local_sem)
        mine.start()
        first = [copy(0, me, sibling, src=x_ref)]
        first += [copy(1 + j, me, (*chip, c), src=x_ref) for j, chip in enumerate(chips)]
        for cp in first:
            cp.start()
        passed = [copy(4 + j, (*chip, c), sibling) for j, chip in enumerate(chips)]
        for j, chip in enumerate(chips):
            copy(1 + j, (*chip, c), me).wait_recv()
            passed[j].start()
        copy(0, sibling, me).wait_recv()
        for j, chip in enumerate(chips):
            copy(4 + j, (*chip, 1 - c), me).wait_recv()
        for cp in first + passed:
            cp.wait_send()
        mine.wait()

    return pl.pallas_call(
        body, out_shape=jax.ShapeDtypeStruct((8 * m_per, n), xs.dtype),
        in_specs=[pl.BlockSpec(memory_space=pltpu.VMEM)] * (1 + len(extra)),
        out_specs=pl.BlockSpec(memory_space=pltpu.VMEM),
        scratch_shapes=[pltpu.SemaphoreType.DMA((7,)), pltpu.SemaphoreType.DMA((7,)), pltpu.SemaphoreType.DMA],
        compiler_params=pltpu.CompilerParams(vmem_limit_bytes=VMEM_LIMIT), name=name)(xs, *extra)


def sibling_merge(name, fulls):
    n = len(fulls)
    slots = [(a, l) for a in range(n) for l in range(fulls[a].shape[0])]

    def body(*refs):
        buf = refs[n:2 * n]
        send_sems, recv_sems = refs[2 * n], refs[2 * n + 1]
        c = lax.axis_index("c")
        sibling = (lax.axis_index("x"), lax.axis_index("y"), 1 - c)
        sends, recvs = [], []
        for k, (a, l) in enumerate(slots):
            kw = dict(send_sem=send_sems.at[k], recv_sem=recv_sems.at[k], device_id=sibling, device_id_type=MESH)
            sends.append(pltpu.make_async_remote_copy(src_ref=buf[a].at[l, c], dst_ref=buf[a].at[l, c], **kw))
            recvs.append(pltpu.make_async_remote_copy(src_ref=buf[a].at[l, c], dst_ref=buf[a].at[l, 1 - c], **kw))
        for cp in sends:
            cp.start()
        for cp in recvs:
            cp.wait_recv()
        for cp in sends:
            cp.wait_send()

    anyspec = pl.BlockSpec(memory_space=pl.ANY)
    return pl.pallas_call(
        body, out_shape=[jax.ShapeDtypeStruct(s.shape, s.dtype) for s in fulls],
        in_specs=[anyspec] * n, out_specs=[anyspec] * n, input_output_aliases={a: a for a in range(n)},
        scratch_shapes=[pltpu.SemaphoreType.DMA((len(slots),)), pltpu.SemaphoreType.DMA((len(slots),))],
        name=name)(*fulls)


def place_own(name, land, src, chip):
    c = src.shape[-1]
    r = src.size // c
    tr = r
    for cand in (1024, 512, 256, 128, 64, 32, 16):
        if r % cand == 0 and cand * c * 2 <= 2 * 1024 * 1024:
            tr = cand
            break

    def kern(chip_ref, land_ref, src_ref, out_ref):
        out_ref[...] = src_ref[...]

    grid_spec = pltpu.PrefetchScalarGridSpec(
        num_scalar_prefetch=1, grid=(r // tr,),
        in_specs=[pl.BlockSpec(memory_space=pl.ANY), pl.BlockSpec((tr, c), lambda i, m: (i, 0))],
        out_specs=pl.BlockSpec((None, tr, c), lambda i, m: (m[0], i, 0)))
    out = pl.pallas_call(kern, grid_spec=grid_spec, out_shape=jax.ShapeDtypeStruct((4, r, c), land.dtype),
                         input_output_aliases={1: 0}, compiler_params=_cparams(("parallel",)),
                         name=name)(chip, land.reshape(4, r, c), src.reshape(r, c))
    return out.reshape(land.shape)


def _half_copies(src, land, send_sems, recv_sems):
    c = lax.axis_index("c")
    sibling = (lax.axis_index("x"), lax.axis_index("y"), 1 - c)
    pairs = []
    for a in range(len(src)):
        cp = pltpu.make_async_remote_copy(src_ref=src[a].at[1 - c], dst_ref=land[a], send_sem=send_sems.at[a],
                                          recv_sem=recv_sems.at[a], device_id=sibling, device_id_type=MESH)
        pairs.append((cp, cp))
    return pairs


def _chip_copies(src, land, send_sems, recv_sems, scatter):
    x, y, c = lax.axis_index("x"), lax.axis_index("y"), lax.axis_index("c")
    me = 2 * x + y
    pairs = []
    for a in range(len(src)):
        for j, (px, py) in enumerate([(1 - x, y), (x, 1 - y), (1 - x, 1 - y)]):
            to = 2 * px + py
            out = src[a].at[to] if scatter else src[a]
            kw = dict(send_sem=send_sems.at[3 * a + j], recv_sem=recv_sems.at[3 * a + j], device_id=(px, py, c),
                      device_id_type=MESH)
            pairs.append((pltpu.make_async_remote_copy(src_ref=out, dst_ref=land[a].at[me], **kw),
                          pltpu.make_async_remote_copy(src_ref=out, dst_ref=land[a].at[to], **kw)))
    return pairs


_HBM = pl.BlockSpec(memory_space=pltpu.HBM)
_SEM = pl.BlockSpec(memory_space=pltpu.SEMAPHORE)


GATHER = (functools.partial(_chip_copies, scatter=False), 3)
SCATTER = (functools.partial(_chip_copies, scatter=True), 3)
TO_SIBLING = (_half_copies, 1)


def _landing(shapes, dtype):
    return [lax.empty(tuple(s), dtype) for s in shapes]


def exchange_start(name, groups, plan):
    copies, per = plan
    sizes = [len(s) for s, _ in groups]
    flat = [a for s, l in groups for a in list(s) + list(l)]
    ng = len(groups)

    def body(*refs):
        ins, outs = refs[:len(flat)], refs[len(flat):]
        off = 0
        for g, n in enumerate(sizes):
            src, land = ins[off:off + n], ins[off + n:off + 2 * n]
            off += 2 * n
            for send, _ in copies(src, land, outs[2 * g], outs[2 * g + 1]):
                send.start()
        outs[-1][...] = jnp.zeros_like(outs[-1])

    out_shape = []
    for n in sizes:
        out_shape += [pltpu.SemaphoreType.DMA((per * n,)), pltpu.SemaphoreType.DMA((per * n,))]
    out_shape += [pltpu.HBM(a.shape, a.dtype) for a in flat] + [jax.ShapeDtypeStruct((8, 128), F32)]
    res = pl.pallas_call(
        body, out_shape=tuple(out_shape), in_specs=[_HBM] * len(flat),
        out_specs=tuple([_SEM] * (2 * ng) + [_HBM] * len(flat) + [pl.BlockSpec(memory_space=pltpu.VMEM)]),
        input_output_aliases={k: 2 * ng + k for k in range(len(flat))},
        compiler_params=pltpu.CompilerParams(has_side_effects=pltpu.SideEffectType.DATAFLOW_SIDE_EFFECTING),
        name=name)(*[pltpu.with_memory_space_constraint(a, pltpu.HBM) for a in flat])
    handles, off = [], 2 * ng
    for g, n in enumerate(sizes):
        handles.append((res[2 * g], res[2 * g + 1], list(res[off:off + n]), list(res[off + n:off + 2 * n])))
        off += 2 * n
    return handles, res[-1]


def exchange_wait(name, handle, after, plan):
    send_sems, recv_sems, srcs, lands = handle
    n = len(srcs)

    def body(*refs):
        src, land = refs[:n], refs[n:2 * n]
        for send, recv in plan[0](src, land, refs[2 * n], refs[2 * n + 1]):
            send.wait_send()
            recv.wait_recv()

    res = pl.pallas_call(
        body, out_shape=tuple(pltpu.HBM(a.shape, a.dtype) for a in srcs + lands),
        in_specs=[_HBM] * (2 * n) + [_SEM, _SEM, pl.BlockSpec(memory_space=pl.ANY)],
        out_specs=tuple([_HBM] * (2 * n)), input_output_aliases={k: k for k in range(2 * n)},
        compiler_params=pltpu.CompilerParams(has_side_effects=pltpu.SideEffectType.DATAFLOW_SIDE_EFFECTING),
        name=name)(*srcs, *lands, send_sems, recv_sems, after)
    return list(res[:n]), list(res[n:])


def _rope_tables(d_rot, reps):
    rows = L // GRID_W
    row = np.repeat(np.arange(rows), GRID_W).astype(np.float32)
    col = np.tile(np.arange(GRID_W), rows).astype(np.float32)
    d_axis = d_rot // 2
    inv = (ROPE_THETA ** (-np.arange(0, d_axis, 2, dtype=np.float32) / d_axis)).astype(np.float32)
    ang = np.concatenate([row[:, None] * inv, col[:, None] * inv], axis=-1).astype(np.float32)
    cos, sin = np.cos(ang).astype(np.float32), np.sin(ang).astype(np.float32)
    c = np.repeat(cos, 2, axis=-1)
    s = np.stack([-sin, sin], axis=-1).reshape(L, d_rot)
    c = np.concatenate([np.ones((LC, d_rot), np.float32), c], axis=0)
    s = np.concatenate([np.zeros((LC, d_rot), np.float32), s], axis=0)
    return np.tile(c, (1, reps)), np.tile(s, (1, reps))


def _group_consts():
    g = np.arange(512) // 64
    avg = (g[:, None] == g[None, :]).astype(np.float32) / 64.0
    masks = (np.arange(8)[:, None] == g[None, :]).astype(np.float32).reshape(8, 1, 512)
    return jnp.asarray(avg, BF), jnp.asarray(masks)


def _pack(items):
    flat = jnp.concatenate([a.reshape(-1).astype(F32) for a in items])
    n = flat.shape[0]
    rows = -(-n // D)
    rows = -(-rows // 8) * 8
    return jnp.pad(flat, (0, rows * D - n)).reshape(rows, D)


def _unpack(buf, shapes):
    lead = buf.shape[:-2]
    flat = buf.reshape(lead + (-1,))
    out, off = [], 0
    for shp in shapes:
        n = int(np.prod(shp))
        out.append(flat[..., off:off + n].reshape(lead + tuple(shp)))
        off += n
    return out


def _arrive(prm, key, after):
    if callable(prm[key]):
        prm[key](after)
    return prm[key]


def _layer_fwd(i, x, h, mods, prm, consts, nxt):
    sv = {}
    sv["x0"] = x
    sv["h"] = h
    p = proj_in(f"proj_in_{i}", h, _arrive(prm, "w_in", h))
    sv["p"] = p
    if i == 0:
        q, kv, m2 = even_tok_fwd(p, consts["cos_e"], consts["sin_e"], prm["gq"], prm["gk"], prm["gs"],
                                 prm["sgu_w"], prm["sgu_b"], consts["avg"], consts["masks"])
        o, lse = attn_fwd("attn_fwd_0", q, kv, GQA_HEADS)
        sv.update(q=q, kv=kv)
    else:
        q, kv, y = odd_tok_fwd(p, consts["cos_o"], consts["sin_o"], prm["gq"], prm["gkv"], prm["wq"], prm["wkk"],
                               prm["wkv"], consts["spread"])
        o, lse = attn_fwd("attn_fwd_1", q, kv, MLA_HEADS)
        z, m2 = conf_fwd(y, prm["conv_w"], prm["conv_b"], prm["ln_g"], prm["ln_b"])
        sv.update(q=q, kv=kv, y=y, z=z)
    sv.update(o=o, lse=lse, m2=m2)
    x1, y1, h2 = proj_out(f"proj_out_{i}", o, m2, _arrive(prm, "w_out", o), x, mods, 2, prm["norm2_g"], 3, 4)
    sv.update(x1=x1, y1=y1)
    a, f = mlp_up(f"mlp_up_{i}", h2, _arrive(prm, "w1", h2))
    x2, y2, *h_next = mlp_down(f"mlp_down_{i}", f, prm["w2"], x1, mods, 5, nxt)
    sv.update(h2=h2, a=a, f=f, y2=y2)
    return x2, (h_next[0] if h_next else None), sv


def _layer_bwd(i, dx, dy2, dg2, sv, mods, prm, consts, hook, entry, below):
    gr = {}
    da = mlp_bwd_da(f"mlp_bwd_da_{i}", dy2, prm["w2"], sv["a"], after=entry)
    tiles8 = [(h, j) for h in range(2) for j in range(4)]
    gr["w1"] = mm_tn(f"grad_w1_{i}", sv["h2"], da, tiles8, 512, D).reshape(2, 4, 512, D)
    gr["w2"] = mm_tn(f"grad_w2_{i}", sv["f"], dy2, [(2 * j + h, 0) for h in range(2) for j in range(4)],
                     512, D).reshape(2, 4, 512, D)
    dh2 = mlp_bwd_dh(f"mlp_bwd_dh_{i}", da, prm["w1"])
    dx1, dy1, dg1, dsh2, dsc2, gr["norm2_g"] = modnorm_bwd(
        f"norm2_bwd_{i}", sv["x1"], dh2, dx, mods, prm["norm2_g"], 3, 4, gate=(sv["y1"], mods, 2),
        after=hook(f"{i}:mlp", gr, dh2))
    dcat = mm_nt(f"proj_out_bwd_{i}", dy1, prm["w_out"], after=hook(f"{i}:mid", gr, dy1))
    go = mm_tn(f"grad_wout_a_{i}", sv["o"], dy1, [(0, 0)], 512, D).reshape(2, 2, 128, D)
    gm = mm_tn(f"grad_wout_b_{i}", sv["m2"], dy1, [(0, 0)], 512, D).reshape(2, 2, 128, D)
    gr["w_out"] = jnp.concatenate([go, gm], axis=0).transpose(1, 0, 2, 3)
    if i == 0:
        dq, dkv = attn_bwd("attn_bwd_0", sv["q"], sv["kv"], sv["o"], dcat, sv["lse"], GQA_HEADS)
        dp, gr["gq"], gr["gk"], gr["gs"], gr["sgu_w"], gr["sgu_b"] = even_tok_bwd(
            sv["p"], dq, dkv, dcat, consts["cos_e"], consts["sin_e"], prm["gq"], prm["gk"],
            prm["gs"], prm["sgu_w"], prm["sgu_b"], consts["avg"], consts["masks"])
    else:
        dq, dkv = attn_bwd("attn_bwd_1", sv["q"], sv["kv"], sv["o"], dcat, sv["lse"], MLA_HEADS)
        dz, gr["ln_g"], gr["ln_b"], gr["conv_b"] = conf_bwd_ln(sv["z"], dcat, prm["ln_g"], prm["ln_b"])
        dyc, gr["conv_w"] = conf_bwd_conv(sv["y"], dz, prm["conv_w"])
        dp, gr["gq"], gr["gkv"], gr["wq"], gr["wkk"], gr["wkv"] = odd_tok_bwd(
            sv["p"], dq, dkv, dyc, consts["cos_o"], consts["sin_o"], prm["gq"], prm["gkv"], prm["wq"], prm["wkk"],
            prm["wkv"], consts["spread"])
    n_in = prm["w_in"].shape[1]
    gr["w_in"] = mm_tn(f"grad_win_{i}", sv["h"], dp, [(0, 0), (1, 0)], 512, n_in)
    dh = mm_nt(f"proj_in_bwd_{i}", dp, prm["w_in"])
    if below:
        dx0, dy2b, dg2b, dsh1, dsc1, gr["norm1_g"] = modnorm_bwd(
            f"norm1_bwd_{i}", sv["x0"], dh, dx1, mods, prm["norm1_g"], 0, 1, gate=(below[0], below[1], 5))
        down = (dy2b, dg2b)
    else:
        dx0, dsh1, dsc1, gr["norm1_g"] = modnorm_bwd(f"norm1_bwd_{i}", sv["x0"], dh, dx1, mods, prm["norm1_g"], 0, 1,
                                                     lat_only=True)
        down = None
    dmods = jnp.concatenate([dsh1, dsc1, dg1, dsh2, dsc2, dg2], axis=1)
    return dx0, down, dmods, gr, hook(f"{i}:end", gr, dx0)


def local_step(xcat, target, mods, prms, final_g, hook=lambda point, grads, fresh: None):
    avg, masks = _group_consts()
    cos_e, sin_e = _rope_tables(64, 8)
    ck, sk = _rope_tables(32, 1)
    one64, zero64 = np.ones((SEQ, 64), np.float32), np.zeros((SEQ, 64), np.float32)
    one96, zero96 = np.ones((SEQ, 96), np.float32), np.zeros((SEQ, 96), np.float32)
    cos_o = np.concatenate([np.tile(np.concatenate([one64, ck], axis=1), (1, 8)), ck, one96], axis=1)
    sin_o = np.concatenate([np.tile(np.concatenate([zero64, sk], axis=1), (1, 8)), sk, zero96], axis=1)
    lane = np.arange(768)
    spread = np.zeros((128, 768), np.float32)
    spread[lane % 96 - 64, lane] = (lane % 96 >= 64)
    consts = dict(avg=avg, masks=masks, cos_e=jnp.asarray(cos_e), sin_e=jnp.asarray(sin_e),
                  cos_o=jnp.asarray(cos_o), sin_o=jnp.asarray(sin_o), spread=jnp.asarray(spread, BF))
    x = xcat
    h = modnorm_fwd("norm1_fwd_0", x, mods[0], prms[0]["norm1_g"], 0, 1)
    saved = []
    for i in range(2):
        x, h, sv = _layer_fwd(i, x, h, mods[i], prms[i], consts, (mods[1], prms[1]["norm1_g"]) if i == 0 else None)
        saved.append(sv)
    dx, dy2, dg2, loss, dfg = final_loss(x, target, final_g, saved[1]["y2"], mods[1], 5)
    dmods, grads = [None, None], [None, None]
    entry, down = None, (dy2, dg2)
    for i in (1, 0):
        below = (saved[0]["y2"], mods[0]) if i == 1 else None
        dx, down, dmods[i], grads[i], entry = _layer_bwd(i, dx, down[0], down[1], saved[i], mods[i], prms[i], consts,
                                                         hook, entry, below)
    return loss, dx, dmods, grads, dfg, entry


def _row(v):
    return v.reshape(1, -1).astype(F32)


def odd_in_params(od_w_in, w_uq, w_ukv):
    od = jnp.concatenate([od_w_in[:, 0:416], jnp.zeros((D, 96), od_w_in.dtype), od_w_in[:, 416:OD_IN]], axis=1)
    ukv = w_ukv.reshape(128, 8, 128)
    wkk = jnp.pad(ukv[:, :, :64], ((0, 0), (0, 0), (0, 32))).reshape(128, 768)
    return dict(w_in=od, wq=w_uq, wkk=wkk, wkv=ukv[:, :, 64:].reshape(128, 512))


def small_params(small):
    p0 = dict(norm1_g=_row(small["norm1_g"][0]), norm2_g=_row(small["norm2_g"][0]),
              gq=jnp.tile(_row(small["ev_q_norm_g"]), (1, 8)), gk=jnp.tile(_row(small["ev_k_norm_g"]), (1, 2)),
              gs=_row(small["ev_sgu_norm_g"]), sgu_w=small["ev_sgu_w"].reshape(8, 128, 128).astype(F32),
              sgu_b=small["ev_sgu_b"].reshape(8, 128, 1).astype(F32))
    p1 = dict(norm1_g=_row(small["norm1_g"][1]), norm2_g=_row(small["norm2_g"][1]),
              gq=_row(small["od_q_norm_g"]), gkv=_row(small["od_kv_norm_g"]),
              conv_w=jnp.pad(small["od_conv_w"].reshape(CONV_K, 512).astype(F32), ((0, 1), (0, 0))),
              conv_b=_row(small["od_conv_b"]), ln_g=_row(small["od_ln_g"]), ln_b=_row(small["od_ln_b"]))
    return [p0, p1]


def prep_params(ev_w_in, od_w_in, w_out, w1, w2, w_uq, w_ukv, small):
    p0, p1 = small_params(small)
    p0.update(w_in=ev_w_in, w_out=w_out[0], w1=w1[0], w2=w2[0])
    p1.update(odd_in_params(od_w_in, w_uq, w_ukv), w_out=w_out[1], w1=w1[1], w2=w2[1])
    return [p0, p1]


def small_grads_natural(grads, dfg):
    g0, g1 = grads
    return dict(
        norm1_g=jnp.concatenate([g0["norm1_g"], g1["norm1_g"]], axis=0),
        norm2_g=jnp.concatenate([g0["norm2_g"], g1["norm2_g"]], axis=0),
        ev_q_norm_g=g0["gq"].reshape(8, 64).sum(0).reshape(1, 64),
        ev_k_norm_g=g0["gk"].reshape(2, 64).sum(0).reshape(1, 64),
        ev_sgu_norm_g=g0["gs"].reshape(1, 8, 64),
        ev_sgu_w=g0["sgu_w"].reshape(1, 8, 128, 128),
        ev_sgu_b=g0["sgu_b"].reshape(1, 8, 128),
        od_q_norm_g=g1["gq"].reshape(1, 256),
        od_kv_norm_g=g1["gkv"].reshape(1, 128),
        od_conv_w=g1["conv_w"][0:CONV_K].reshape(1, CONV_K, 512),
        od_conv_b=g1["conv_b"].reshape(1, 512),
        od_ln_g=g1["ln_g"].reshape(1, 512),
        od_ln_b=g1["ln_b"].reshape(1, 512),
        final_g=dfg.reshape(D))


def layer_grads_hs(i, g, part="all"):
    def cols(a):
        k, n = a.shape
        return a.reshape(2, k // 2, 4, n // 4).transpose(0, 2, 1, 3).astype(BF)

    mlp = [(("mlp_w1", i), g["w1"]), (("mlp_w2", i), g["w2"])]
    if part == "mlp":
        return mlp
    rest = [(("w_out", i), g["w_out"])]
    if i == 0:
        rest.append((("ev_w_in", 0), cols(g["w_in"].reshape(D, EV_IN))))
    else:
        od = g["w_in"].reshape(D, OD_PAD)
        od = jnp.concatenate([od[:, 0:416], od[:, 512:OD_PAD]], axis=1)
        ukv = jnp.concatenate([g["wkk"].reshape(128, 8, 96)[:, :, :64], g["wkv"].reshape(128, 8, 64)], axis=2)
        rest += [(("od_w_in", 0), cols(od)), (("od_w_uq", 0), cols(g["wq"])),
                 (("od_w_ukv", 0), cols(ukv.reshape(128, 1024)))]
    return rest if part == "rest" else mlp + rest


def big_grads_hs(grads):
    d = dict(layer_grads_hs(0, grads[0]) + layer_grads_hs(1, grads[1]))
    return dict(ev_w_in=d[("ev_w_in", 0)], od_w_in=d[("od_w_in", 0)], od_w_uq=d[("od_w_uq", 0)],
                od_w_ukv=d[("od_w_ukv", 0)], w_out=[d[("w_out", 0)], d[("w_out", 1)]],
                mlp_w1=[d[("mlp_w1", 0)], d[("mlp_w1", 1)]], mlp_w2=[d[("mlp_w2", 0)], d[("mlp_w2", 1)]])


def grads_to_natural(grads, dfg):
    out = small_grads_natural(grads, dfg)
    hs = big_grads_hs(grads)

    def from_cols(a):
        return a.transpose(0, 2, 1, 3).reshape(2 * a.shape[2], 4 * a.shape[3])

    def from_rows(a):
        return a.transpose(1, 0, 2, 3).reshape(8 * a.shape[2], a.shape[3])

    out["ev_w_in"] = from_cols(hs["ev_w_in"])[None]
    out["od_w_in"] = from_cols(hs["od_w_in"])[None]
    out["od_w_uq"] = from_cols(hs["od_w_uq"])[None]
    out["od_w_ukv"] = from_cols(hs["od_w_ukv"])[None]
    out["w_out"] = jnp.stack([from_rows(a) for a in hs["w_out"]])
    out["mlp_w1"] = jnp.stack([from_cols(a) for a in hs["mlp_w1"]])
    out["mlp_w2"] = jnp.stack([from_rows(a) for a in hs["mlp_w2"]])
    return out


WEIGHT_NAMES = ['c_ctx', 'ada_w', 'ada_b', 'norm1_g', 'norm2_g', 'w_out', 'mlp_w1', 'mlp_w2', 'ev_w_in',
                'ev_q_norm_g', 'ev_k_norm_g', 'ev_sgu_norm_g', 'ev_sgu_w', 'ev_sgu_b', 'od_w_in', 'od_q_norm_g',
                'od_kv_norm_g', 'od_w_uq', 'od_w_ukv', 'od_conv_w', 'od_conv_b', 'od_ln_g', 'od_ln_b', 'final_g']
REPL_SMALL = ['norm1_g', 'norm2_g', 'ev_q_norm_g', 'ev_k_norm_g', 'ev_sgu_norm_g', 'ev_sgu_w', 'ev_sgu_b',
              'od_kv_norm_g', 'final_g']
SHARD_SMALL = ['od_q_norm_g', 'od_conv_w', 'od_conv_b', 'od_ln_g', 'od_ln_b']
BIG = ['w_out', 'mlp_w1', 'mlp_w2', 'ev_w_in', 'od_w_in', 'od_w_uq', 'od_w_ukv']


def _gather_last(parts):
    return jnp.concatenate([parts[k] for k in range(4)], axis=-1)


class _Reduce:
    def __init__(self, tag, named, half, where):
        self.tag, self.half, self.where = tag, half, where
        self.names, self.hs = zip(*named)
        self.hs = list(self.hs)

    def to_sibling(self):
        lands = [lax.empty(a.shape[1:], BF) for a in self.hs]
        (self.h1,), token = exchange_start(f"rs_sibling_start_{self.tag}", [(self.hs, lands)], TO_SIBLING)
        return token

    def to_chips(self, after):
        hs, got = exchange_wait(f"rs_sibling_wait_{self.tag}", self.h1, after, TO_SIBLING)
        pair = [add_pairs(f"rs_add_{self.tag}_{k}", a, b, self.half) for k, (a, b) in enumerate(zip(hs, got))]
        lands = [lax.empty(p.shape, BF) for p in pair]
        (self.h2,), token = exchange_start(f"rs_chips_start_{self.tag}", [(pair, lands)], SCATTER)
        return token

    def finish(self, after, bufs):
        pair, land = exchange_wait(f"rs_chips_wait_{self.tag}", self.h2, after, SCATTER)
        for k, ((n, idx), l, p) in enumerate(zip(self.names, land, pair)):
            bufs[n] = sum_slabs(f"rs_sum_{self.tag}_{k}", l, p, self.where, bufs[n], idx)


def kernel(x, c, ctx, c_ctx, ada_w, ada_b, norm1_g, norm2_g, w_out, mlp_w1, mlp_w2, ev_w_in, ev_q_norm_g, ev_k_norm_g, ev_sgu_norm_g, ev_sgu_w, ev_sgu_b, od_w_in, od_q_norm_g, od_kv_norm_g, od_w_uq, od_w_ukv, od_conv_w, od_conv_b, od_ln_g, od_ln_b, final_g, loss_target, m_c_ctx, m_ada_w, m_ada_b, m_norm1_g, m_norm2_g, m_w_out, m_mlp_w1, m_mlp_w2, m_ev_w_in, m_ev_q_norm_g, m_ev_k_norm_g, m_ev_sgu_norm_g, m_ev_sgu_w, m_ev_sgu_b, m_od_w_in, m_od_q_norm_g, m_od_kv_norm_g, m_od_w_uq, m_od_w_ukv, m_od_conv_w, m_od_conv_b, m_od_ln_g, m_od_ln_b, m_final_g, v_c_ctx, v_ada_w, v_ada_b, v_norm1_g, v_norm2_g, v_w_out, v_mlp_w1, v_mlp_w2, v_ev_w_in, v_ev_q_norm_g, v_ev_k_norm_g, v_ev_sgu_norm_g, v_ev_sgu_w, v_ev_sgu_b, v_od_w_in, v_od_q_norm_g, v_od_kv_norm_g, v_od_w_uq, v_od_w_ukv, v_od_conv_w, v_od_conv_b, v_od_ln_g, v_od_ln_b, v_final_g):
    w = dict(c_ctx=c_ctx, ada_w=ada_w, ada_b=ada_b, norm1_g=norm1_g, norm2_g=norm2_g, w_out=w_out, mlp_w1=mlp_w1,
             mlp_w2=mlp_w2, ev_w_in=ev_w_in, ev_q_norm_g=ev_q_norm_g, ev_k_norm_g=ev_k_norm_g,
             ev_sgu_norm_g=ev_sgu_norm_g, ev_sgu_w=ev_sgu_w, ev_sgu_b=ev_sgu_b, od_w_in=od_w_in,
             od_q_norm_g=od_q_norm_g, od_kv_norm_g=od_kv_norm_g, od_w_uq=od_w_uq, od_w_ukv=od_w_ukv,
             od_conv_w=od_conv_w, od_conv_b=od_conv_b, od_ln_g=od_ln_g, od_ln_b=od_ln_b, final_g=final_g)
    mom = dict(c_ctx=m_c_ctx, ada_w=m_ada_w, ada_b=m_ada_b, norm1_g=m_norm1_g, norm2_g=m_norm2_g, w_out=m_w_out,
               mlp_w1=m_mlp_w1, mlp_w2=m_mlp_w2, ev_w_in=m_ev_w_in, ev_q_norm_g=m_ev_q_norm_g,
               ev_k_norm_g=m_ev_k_norm_g, ev_sgu_norm_g=m_ev_sgu_norm_g, ev_sgu_w=m_ev_sgu_w, ev_sgu_b=m_ev_sgu_b,
               od_w_in=m_od_w_in, od_q_norm_g=m_od_q_norm_g, od_kv_norm_g=m_od_kv_norm_g, od_w_uq=m_od_w_uq,
               od_w_ukv=m_od_w_ukv, od_conv_w=m_od_conv_w, od_conv_b=m_od_conv_b, od_ln_g=m_od_ln_g,
               od_ln_b=m_od_ln_b, final_g=m_final_g)
    var = dict(c_ctx=v_c_ctx, ada_w=v_ada_w, ada_b=v_ada_b, norm1_g=v_norm1_g, norm2_g=v_norm2_g, w_out=v_w_out,
               mlp_w1=v_mlp_w1, mlp_w2=v_mlp_w2, ev_w_in=v_ev_w_in, ev_q_norm_g=v_ev_q_norm_g,
               ev_k_norm_g=v_ev_k_norm_g, ev_sgu_norm_g=v_ev_sgu_norm_g, ev_sgu_w=v_ev_sgu_w, ev_sgu_b=v_ev_sgu_b,
               od_w_in=v_od_w_in, od_q_norm_g=v_od_q_norm_g, od_kv_norm_g=v_od_kv_norm_g, od_w_uq=v_od_w_uq,
               od_w_ukv=v_od_w_ukv, od_conv_w=v_od_conv_w, od_conv_b=v_od_conv_b, od_ln_g=v_od_ln_g,
               od_ln_b=v_od_ln_b, final_g=v_final_g)
    xi, yi, ci = lax.axis_index("x"), lax.axis_index("y"), lax.axis_index("c")
    chip = 2 * xi + yi
    dev = 2 * chip + ci

    shard_shapes = [w[n].shape for n in SHARD_SMALL]
    g0 = all_gather8("ag_small", _pack([c] + [w[n] for n in SHARD_SMALL]))
    g0 = g0.reshape(8, -1, D)
    parts = _unpack(g0, [c.shape] + shard_shapes)
    c_all = parts[0].reshape(16, D)
    small_full = {n: _gather_last(p[0::2]) for n, p in zip(SHARD_SMALL, parts[1:])}
    call = jnp.concatenate([c_all, c_ctx.reshape(1, D), jnp.zeros((NC - 17, D), F32)], axis=0)

    cols = ada_w.shape[2]
    ada_b_sh = lax.dynamic_slice(ada_b, (0, chip * cols), (2, cols)).reshape(2, 1, cols)
    mt = mods_fwd(call, ada_w, ada_b_sh)
    mt = all_gather8("ag_mods", mt.reshape(2 * NC, cols)).reshape(8, 2, NC, cols)
    table = mt[0::2].transpose(1, 2, 0, 3).reshape(2, NC, 4 * cols)
    mods = []
    for i in range(2):
        lat = lax.dynamic_slice(table[i], (2 * dev, 0), (2, 4 * cols))
        mc = table[i, 16]
        mods.append(jnp.stack([mc, lat[0], mc, lat[1]]).reshape(4 * N_MOD, 1, D))

    order = [[("ev_w_in", 0)], [("w_out", 0), ("mlp_w1", 0), ("mlp_w2", 0)],
             [("od_w_in", 0), ("od_w_uq", 0), ("od_w_ukv", 0), ("w_out", 1)], [("mlp_w1", 1), ("mlp_w2", 1)]]
    groups = []
    for names in order:
        srcs = [w[n][i].astype(BF) for n, i in names]
        groups.append((srcs, [lax.empty((4,) + s.shape, BF) for s in srcs]))
    groups[0][0][0], table = lax.optimization_barrier((groups[0][0][0], table))
    handles, token = exchange_start("gather_start", groups, GATHER)
    mods[0] = mods[0] + token[0, 0]
    small = {n: w[n] for n in REPL_SMALL}
    small.update(small_full)
    prms = small_params(small)

    chip1 = chip.reshape(1).astype(jnp.int32)

    def arrived(k, after):
        srcs, lands = exchange_wait(f"gather_wait_{k}", handles[k], after, GATHER)
        return [place_own(f"gather_own_{k}_{a}", l, s, chip1) for a, (l, s) in enumerate(zip(lands, srcs))]

    def arrive_ev_in(after):
        (ev,) = arrived(0, after)
        prms[0]["w_in"] = _gather_last(ev)

    def arrive_ev_rest(after):
        wo, w1, w2 = arrived(1, after)
        prms[0].update(w_out=wo.reshape(D, D), w1=w1, w2=w2)

    def arrive_od(after):
        od, uq, ukv, wo = arrived(2, after)
        prms[1].update(odd_in_params(_gather_last(od), _gather_last(uq), _gather_last(ukv)), w_out=wo.reshape(D, D))

    def arrive_od_mlp(after):
        w1, w2 = arrived(3, after)
        prms[1].update(w1=w1, w2=w2)

    prms[0]["w_in"] = arrive_ev_in
    prms[0]["w_out"] = arrive_ev_rest
    prms[1]["w_in"] = arrive_od
    prms[1]["w1"] = arrive_od_mlp

    half = ci.reshape(1).astype(jnp.int32)
    where = jnp.stack([chip, ci]).astype(jnp.int32)
    red = {}

    def hook(point, g, fresh):
        if point == "1:end":
            red["l1"] = _Reduce("l1", layer_grads_hs(1, g, "all"), half, where)
            return red["l1"].to_sibling()
        if point == "0:mlp":
            red["l0_mlp"] = _Reduce("l0_mlp", layer_grads_hs(0, g, "mlp"), half, where)
            return red["l1"].to_chips(fresh) + red["l0_mlp"].to_sibling()
        if point == "0:mid":
            return red["l0_mlp"].to_chips(fresh)
        if point == "0:end":
            red["l0_rest"] = _Reduce("l0_rest", layer_grads_hs(0, g, "rest"), half, where)
            return red["l0_rest"].to_sibling()
        return None

    xin = (ctx.reshape(NEX * LC, D), x.reshape(NEX * L, D))
    loss_p, dx, dmods, grads, dfg, last = local_step(xin, loss_target.reshape(NEX * L, D), mods, prms,
                                                     final_g.reshape(1, D), hook)
    grad_x = dx.reshape(NEX, L, D)

    sg = small_grads_natural(grads, dfg)
    dm = jnp.stack([d.reshape(4, N_MOD * D) for d in dmods])
    small_names = REPL_SMALL + SHARD_SMALL
    items = [dm[:, 1::2], dm[:, 0] + dm[:, 2]] + [sg[n] for n in small_names] + [loss_p[0:1, 0:1]]
    shapes = [a.shape for a in items]
    g1 = all_gather8("ag_grads", _pack(items), after=last)
    started = red["l0_rest"].to_chips(g1)
    rows1 = g1.shape[0] // 8
    g1 = g1.reshape(8, rows1, D)
    tot = _unpack(sum_lead("sum_small", g1, after=started), shapes)
    dm_lat = _unpack(g1, shapes[:1])[0]
    dm_lat = dm_lat.transpose(1, 0, 2, 3).reshape(2, 16, N_MOD * D)
    dm_all = jnp.concatenate([dm_lat, tot[1][:, None], jnp.zeros((2, NC - 17, N_MOD * D), F32)], axis=1)
    gsum = dict(zip(small_names, tot[2:2 + len(small_names)]))
    loss = tot[-1].reshape(())
    grad = {n: gsum[n].reshape(w[n].shape) for n in REPL_SMALL}
    for n in SHARD_SMALL:
        k = w[n].shape[-1]
        grad[n] = lax.dynamic_slice_in_dim(gsum[n], chip * k, k, axis=gsum[n].ndim - 1)
    grad["ada_b"] = sum_lead("sum_ada_b", dm_all.transpose(1, 0, 2).reshape(NC, 2 * N_MOD, D)).reshape(2, N_MOD * D)

    dm_sh = lax.dynamic_slice(dm_all, (0, 0, chip * cols), (2, NC, cols))
    grad["ada_w"], dcc = ada_bwd(call, ada_w, dm_sh)
    dcc = all_gather8("ag_cctx", dcc).reshape(8, 8, D)
    grad["c_ctx"] = sum_lead("sum_cctx", dcc[0::2])[0]

    delta, new_m, new_v = {}, {}, {}

    def adam_big(n, again):
        shp = w[n].shape
        two_d = (shp[0] * shp[1], shp[2])
        res = adamw(f"adamw_{n}", w[n].reshape(two_d), grad[n].reshape(two_d), mom[n].reshape(two_d),
                    var[n].reshape(two_d), again)
        delta[n], new_m[n], new_v[n] = [a.reshape(shp) for a in res[:3]]
        if again:
            grad[n] = res[3].reshape(shp)

    adam_big('ada_w', False)
    rest = [n for n in WEIGHT_NAMES if n not in ['ada_w'] + BIG]
    flat2 = lambda a: a.reshape(-1, a.shape[-1])
    outs = adamw_many("adamw_small", [flat2(w[n]) for n in rest], [flat2(grad[n]) for n in rest],
                      [flat2(mom[n]) for n in rest], [flat2(var[n]) for n in rest])
    for dst, arrs in zip((delta, new_m, new_v), outs):
        dst.update({n: a.reshape(w[n].shape) for n, a in zip(rest, arrs)})
    d_ = outs[0][1]

    bufs = {n: lax.empty((w[n].shape[0], 2, w[n].shape[1] // 2, w[n].shape[2]), F32) for n in BIG}
    for tag, behind in (("l1", delta["ada_w"]), ("l0_mlp", d_), ("l0_rest", d_)):
        red[tag].finish(behind, bufs)
    for n, full in zip(BIG, sibling_merge("rs_sibling_merge", [bufs[n] for n in BIG])):
        grad[n] = full.reshape(w[n].shape)
    for n in BIG:
        adam_big(n, True)

    return (loss, grad_x, *[grad[n] for n in WEIGHT_NAMES], *[delta[n] for n in WEIGHT_NAMES],
            *[new_m[n] for n in WEIGHT_NAMES], *[new_v[n] for n in WEIGHT_NAMES])
```

```python
import functools

import numpy as np
import jax
import jax.numpy as jnp
from jax import lax
from jax.experimental import pallas as pl
from jax.experimental.pallas import tpu as pltpu

F32 = jnp.float32
BF = jnp.bfloat16
HI = lax.Precision.HIGHEST
MESH = pl.DeviceIdType.MESH

D = 1024
L = 2048
LC = 256
SEQ = L + LC
NEX = 2
R = NEX * SEQ
TB = 256
WIDE = 512
BPE = SEQ // TB
NBLK = R // TB
GRID_W = 64
FF = 4 * D
EPS = 1e-6
ROPE_THETA = 10000.0
N_MOD = 6
EV_IN = 1792
OD_IN = 1440
OD_PAD = 1536
VMEM_LIMIT = 60 * 1024 * 1024

ADAM_LR = 0.001
ADAM_B1 = 0.9
ADAM_B2 = 0.999
ADAM_EPS = 1e-08
ADAM_WD = 0.01
ADAM_STEP = 10

NT = (((1,), (1,)), ((), ()))
TN = (((0,), (0,)), ((), ()))


def _cparams(sem=None):
    return pltpu.CompilerParams(dimension_semantics=sem, vmem_limit_bytes=VMEM_LIMIT)


@jax.custom_vjp
def _mm(a, b):
    return jnp.dot(a.astype(BF), b.astype(BF), preferred_element_type=F32)


def _mm_fwd(a, b):
    return _mm(a, b), (a, b)


def _mm_bwd(res, g):
    a, b = res
    gb = g.astype(BF)
    da = lax.dot_general(gb, b.astype(BF), NT, preferred_element_type=F32)
    db = lax.dot_general(a.astype(BF), gb, TN, preferred_element_type=F32)
    return da, db


_mm.defvjp(_mm_fwd, _mm_bwd)


@jax.custom_vjp
def _swap(x):
    n = x.shape[-1]
    ax = x.ndim - 1
    lane = lax.broadcasted_iota(jnp.int32, x.shape, ax)
    return jnp.where(lane % 2 == 0, pltpu.roll(x, n - 1, ax), pltpu.roll(x, 1, ax))


_swap.defvjp(lambda x: (_swap(x), None), lambda _, g: (_swap(g),))


def _rope(x, cos, sin):
    return x * cos + _swap(x) * sin


def _rmsn(x, g):
    return x * lax.rsqrt(jnp.mean(x * x, axis=-1, keepdims=True) + EPS) * g


def _split_dot(a, m):
    hi = a.astype(BF)
    lo = (a - hi.astype(F32)).astype(BF)
    return jnp.dot(hi, m, preferred_element_type=F32) + jnp.dot(lo, m, preferred_element_type=F32)


@jax.custom_vjp
def _group_mean(a, avg):
    return _split_dot(a, avg)


_group_mean.defvjp(lambda a, avg: (_split_dot(a, avg), avg),
                   lambda avg, g: (_split_dot(g, avg), jnp.zeros_like(avg)))


def _grmsn(x, g, avg):
    return x * lax.rsqrt(_group_mean(x * x, avg) + EPS) * g


def _modnorm(x, g, sh, sc):
    return _rmsn(x, g) * (1.0 + sc) + sh


def _gelu(x):
    return 0.5 * x * (1.0 + jnp.tanh(0.7978845608028654 * (x + 0.044715 * (x * x * x))))


def _silu(x):
    return x * jax.nn.sigmoid(x)


def _acc(ref, val, first):
    @pl.when(first)
    def _():
        ref[...] = val

    @pl.when(jnp.logical_not(first))
    def _():
        ref[...] += val


def _seg(i):
    return 2 * (i // BPE) + jnp.minimum(i % BPE, 1)


def _seg_first(i):
    return (i % BPE) <= 1


class _Either:
    def __init__(self, pick_first, first, second):
        self.pick_first, self.first, self.second = pick_first, first, second

    def __getitem__(self, idx):
        return jnp.where(self.pick_first, self.first[idx], self.second[idx])


def _rb_call(name, body, row_in=(), mod_in=(), pos_in=(), full_in=(), shift_in=(),
             row_out=(), seg_out=(), acc_out=(), scratch=(), after=None, col_in=(), rows=TB, idle=None):
    assert rows == TB or not (mod_in or pos_in or shift_in or seg_out or col_in)
    in_specs, args, pairs = [], [], []
    for a in row_in:
        if isinstance(a, tuple):
            pairs.append(len(args))
            in_specs.append(pl.BlockSpec((TB, a[0].shape[1]), lambda i: (i // BPE, 0)))
            in_specs.append(pl.BlockSpec(
                (TB, a[1].shape[1]), lambda i: ((i // BPE) * (L // TB) + jnp.maximum(i % BPE - 1, 0), 0)))
            args += list(a)
        else:
            in_specs.append(pl.BlockSpec((rows, a.shape[1]), lambda i: (i, 0)))
            args.append(a)
    for a in col_in:
        in_specs.append(pl.BlockSpec((a.shape[0], TB), lambda i: (0, i)))
        args.append(a)
    for tab, m in mod_in:
        in_specs.append(pl.BlockSpec((1, 1, D), lambda i, m=m: (_seg(i) * N_MOD + m, 0, 0)))
        args.append(tab)
    for a in pos_in:
        in_specs.append(pl.BlockSpec((TB, a.shape[1]), lambda i: (i % BPE, 0)))
        args.append(a)
    for a in full_in:
        in_specs.append(pl.BlockSpec(a.shape, lambda i, n=a.ndim: (0,) * n))
        args.append(a)
    for a, d in shift_in:
        in_specs.append(pl.BlockSpec((TB, a.shape[1]), lambda i, d=d: (jnp.clip(i + d, 0, NBLK - 1), 0)))
        args.append(a)
    n_in = len(args)
    if after is not None:
        in_specs.append(pl.BlockSpec(after.shape, lambda i, n=after.ndim: (0,) * n))
        args.append(after)
    out_specs, out_shape = [], []
    for w, dt, *lat in row_out:
        if lat:
            out_specs.append(pl.BlockSpec(
                (TB, w), lambda i: ((i // BPE) * (L // TB) + jnp.maximum(i % BPE - 1, 0), 0)))
            out_shape.append(jax.ShapeDtypeStruct((NEX * L, w), dt))
        else:
            out_specs.append(pl.BlockSpec((rows, w), lambda i: (i, 0)))
            out_shape.append(jax.ShapeDtypeStruct((R, w), dt))
    for w in seg_out:
        out_specs.append(pl.BlockSpec((1, 1, w), lambda i: (_seg(i), 0, 0)))
        out_shape.append(jax.ShapeDtypeStruct((4, 1, w), F32))
    for shp in acc_out:
        out_specs.append(pl.BlockSpec(shp, lambda i, n=len(shp): (0,) * n))
        out_shape.append(jax.ShapeDtypeStruct(shp, F32))

    def kern(*refs):
        i = pl.program_id(0)
        ins = list(refs[:n_in])
        for k in reversed(pairs):
            ins[k:k + 2] = [_Either((i % BPE) == 0, ins[k], ins[k + 1])]
        if idle is None:
            body(i, *ins, *refs[len(args):])
        else:
            @pl.when((i % BPE) >= 1)
            def _():
                body(i, *ins, *refs[len(args):])

            @pl.when((i % BPE) == 0)
            def _():
                idle(i, *ins, *refs[len(args):])

    sem = ("arbitrary",) if (seg_out or acc_out or any(len(r) > 2 for r in row_out)) else ("parallel",)
    return pl.pallas_call(kern, grid=(R // rows,), in_specs=in_specs, out_specs=out_specs, out_shape=out_shape,
                          scratch_shapes=list(scratch), compiler_params=_cparams(sem), name=name)(*args)


def modnorm_fwd(name, x, mods, g, m_sh, m_sc):
    def body(i, x_ref, sh_ref, sc_ref, g_ref, h_ref):
        h_ref[...] = _modnorm(x_ref[...], g_ref[...], sh_ref[0], sc_ref[0]).astype(BF)

    return _rb_call(name, body, row_in=(x,), mod_in=((mods, m_sh), (mods, m_sc)), full_in=(g,),
                    row_out=((D, BF),))[0]


def _gate_grads(dx, y_ref, gt_ref, dy_ref, dgt_ref, i):
    dy_ref[...] = (dx * gt_ref[0]).astype(BF)
    _acc(dgt_ref, jnp.sum(dx * y_ref[...].astype(F32), axis=0, keepdims=True)[None], _seg_first(i))


def modnorm_bwd(name, x, dh, dx_in, mods, g, m_sh, m_sc, gate=None, after=None, lat_only=False, last=False):
    def body(i, x_ref, dh_ref, dxin_ref, *rest):
        if gate:
            y_ref, sh_ref, sc_ref, gt_ref, g_ref, dx_ref, dy_ref, dgt_ref, dsh_ref, dsc_ref, dg_ref = rest
        else:
            sh_ref, sc_ref, g_ref, dx_ref, dsh_ref, dsc_ref, dg_ref = rest
        _, vjp = jax.vjp(_modnorm, x_ref[...], g_ref[...], sh_ref[0], sc_ref[0])
        dx, dg, dsh, dsc = vjp(dh_ref[...].astype(F32))
        dx = dxin_ref[...] + dx
        dx_ref[...] = dx
        if gate:
            _gate_grads(dx, y_ref, gt_ref, dy_ref, dgt_ref, i)
        _acc(dsh_ref, dsh[None], _seg_first(i))
        _acc(dsc_ref, dsc[None], _seg_first(i))
        _acc(dg_ref, dg, i == 0)

    def idle(i, x_ref, dh_ref, dxin_ref, y_ref, sh_ref, sc_ref, gt_ref, g_ref, dx_ref, dy_ref, dgt_ref, dsh_ref,
             dsc_ref, dg_ref):
        dx_ref[...] = dxin_ref[...]
        _zero(dy_ref, dgt_ref, dsh_ref, dsc_ref)
        _zero_at_start(i, dg_ref)

    if gate:
        y, gmods, m = gate
        return _rb_call(name, body, row_in=(x, dh, dx_in, y), mod_in=((mods, m_sh), (mods, m_sc), (gmods, m)),
                        full_in=(g,), row_out=((D, F32), (D, BF)), seg_out=(D, D, D), acc_out=((1, D),), after=after,
                        idle=idle if last else None)
    return _rb_call(name, body, row_in=(x, dh, dx_in), mod_in=((mods, m_sh), (mods, m_sc)), full_in=(g,),
                    row_out=((D, F32, "lat") if lat_only else (D, F32),), seg_out=(D, D), acc_out=((1, D),),
                    after=after)


def proj_in(name, h, w):
    n = w.shape[1]

    def body(i, h_ref, w_ref, o_ref):
        o_ref[...] = jnp.dot(h_ref[...], w_ref[...], preferred_element_type=F32).astype(BF)

    return _rb_call(name, body, row_in=(h,), full_in=(w,), row_out=((n, BF),), rows=WIDE)[0]


def _zero(*refs):
    for r in refs:
        r[...] = jnp.zeros_like(r)


def _zero_at_start(i, *refs):
    @pl.when(i == 0)
    def _():
        _zero(*refs)


def proj_out(name, a1, a2, w, x, mods, m_gate, g_next, m_sh, m_sc, last=False):
    k1 = a1.shape[1]

    def idle(i, a1_ref, a2_ref, x_ref, gt_ref, sh_ref, sc_ref, w_ref, g_ref, xo_ref, y_ref, h_ref):
        xo_ref[...] = x_ref[...]
        _zero(y_ref, h_ref)

    def body(i, a1_ref, a2_ref, x_ref, gt_ref, sh_ref, sc_ref, w_ref, g_ref, xo_ref, y_ref, h_ref):
        y = jnp.dot(a1_ref[...], w_ref[:k1, :], preferred_element_type=F32)
        y = y + jnp.dot(a2_ref[...], w_ref[k1:, :], preferred_element_type=F32)
        y_ref[...] = y.astype(BF)
        xn = x_ref[...] + gt_ref[0] * y
        xo_ref[...] = xn
        h_ref[...] = _modnorm(xn, g_ref[...], sh_ref[0], sc_ref[0]).astype(BF)

    return _rb_call(name, body, row_in=(a1, a2, x), mod_in=((mods, m_gate), (mods, m_sh), (mods, m_sc)),
                    full_in=(w, g_next), row_out=((D, F32), (D, BF), (D, BF)), idle=idle if last else None)


def mlp_up(name, h, w1):
    def body(i, h_ref, w_ref, a_ref, f_ref):
        hv = h_ref[...]
        for n in range(4):
            a = jnp.dot(hv, w_ref[n], preferred_element_type=F32)
            a_ref[:, n * D:(n + 1) * D] = a.astype(BF)
            r = jnp.maximum(a, 0.0)
            f_ref[:, n * D:(n + 1) * D] = (r * r).astype(BF)

    return _rb_call(name, body, row_in=(h,), full_in=(w1,), row_out=((FF, BF), (FF, BF)), rows=WIDE)


def mlp_down(name, f, w2, x, mods, m_gate, nxt=None):
    def body(i, f_ref, x_ref, gt_ref, *rest):
        if nxt:
            sh_ref, sc_ref, w_ref, g_ref, xo_ref, y_ref, h_ref = rest
        else:
            w_ref, xo_ref, y_ref = rest
        y = jnp.dot(f_ref[:, 0:D], w_ref[0], preferred_element_type=F32)
        for n in range(1, 4):
            y = y + jnp.dot(f_ref[:, n * D:(n + 1) * D], w_ref[n], preferred_element_type=F32)
        xn = x_ref[...] + gt_ref[0] * y
        y_ref[...] = y.astype(BF)
        xo_ref[...] = xn
        if nxt:
            h_ref[...] = _modnorm(xn, g_ref[...], sh_ref[0], sc_ref[0]).astype(BF)

    if nxt:
        return _rb_call(name, body, row_in=(f, x), mod_in=((mods, m_gate), (nxt[0], 0), (nxt[0], 1)),
                        full_in=(w2, nxt[1]), row_out=((D, F32), (D, BF), (D, BF)))
    def idle(i, f_ref, x_ref, gt_ref, w_ref, xo_ref, y_ref):
        xo_ref[...] = x_ref[...]
        _zero(y_ref)

    return _rb_call(name, body, row_in=(f, x), mod_in=((mods, m_gate),), full_in=(w2,),
                    row_out=((D, F32), (D, BF)), idle=idle)


def mm_nt(name, g, w, after=None):
    k = w.shape[0]

    def body(i, g_ref, w_ref, o_ref):
        o_ref[...] = lax.dot_general(g_ref[...], w_ref[...], NT, preferred_element_type=F32).astype(BF)

    return _rb_call(name, body, row_in=(g,), full_in=(w,), row_out=((k, BF),), after=after, rows=WIDE)[0]


def mlp_bwd_da(name, dy, w2, a, after=None):
    def body(i, dy_ref, a_ref, w_ref, da_ref):
        dyv = dy_ref[...]
        for n in range(4):
            df = lax.dot_general(dyv, w_ref[n], NT, preferred_element_type=F32)
            av = a_ref[:, n * D:(n + 1) * D].astype(F32)
            da_ref[:, n * D:(n + 1) * D] = (df * (2.0 * jnp.maximum(av, 0.0))).astype(BF)

    return _rb_call(name, body, row_in=(dy, a), full_in=(w2,), row_out=((FF, BF),), after=after, rows=WIDE)[0]


def mlp_bwd_dh(name, da, w1):
    def body(i, da_ref, w_ref, dh_ref):
        acc = lax.dot_general(da_ref[:, 0:D], w_ref[0], NT, preferred_element_type=F32)
        for n in range(1, 4):
            acc = acc + lax.dot_general(da_ref[:, n * D:(n + 1) * D], w_ref[n], NT, preferred_element_type=F32)
        dh_ref[...] = acc.astype(BF)

    return _rb_call(name, body, row_in=(da,), full_in=(w1,), row_out=((D, BF),), rows=WIDE)[0]


TN_ROWS = 1536


def mm_tn(name, a, g, tiles, th, tw):
    nt = len(tiles)
    acs = jnp.asarray([t[0] for t in tiles], jnp.int32)
    gcs = jnp.asarray([t[1] for t in tiles], jnp.int32)
    nr = R // TN_ROWS

    def kern(ac_ref, gc_ref, a_ref, g_ref, o_ref, acc_ref):
        r = pl.program_id(1)

        @pl.when(r == 0)
        def _():
            acc_ref[...] = jnp.zeros_like(acc_ref)

        acc_ref[...] += lax.dot_general(a_ref[...], g_ref[...], TN, preferred_element_type=F32)

        @pl.when(r == nr - 1)
        def _():
            o_ref[...] = acc_ref[...].astype(BF)

    grid_spec = pltpu.PrefetchScalarGridSpec(
        num_scalar_prefetch=2, grid=(nt, nr),
        in_specs=[pl.BlockSpec((TN_ROWS, th), lambda t, r, ac, gc: (r, ac[t])),
                  pl.BlockSpec((TN_ROWS, tw), lambda t, r, ac, gc: (r, gc[t]))],
        out_specs=pl.BlockSpec((None, th, tw), lambda t, r, ac, gc: (t, 0, 0)),
        scratch_shapes=[pltpu.VMEM((th, tw), F32)])
    return pl.pallas_call(kern, grid_spec=grid_spec, out_shape=jax.ShapeDtypeStruct((nt, th, tw), BF),
                          compiler_params=_cparams(("parallel", "arbitrary")), name=name)(acs, gcs, a, g)


def _even_tok(q, k, zus, zvs, gq, gk, gss, ws, bs, cq, sq, ck, sk, avg, lo, hi):
    avg2 = avg[:128, :128]
    qr = _rope(_grmsn(q, gq, avg), cq, sq) * GQA_SCALE
    kr = _rope(_grmsn(k, gk, avg2), ck, sk)
    ms = []
    for b in range(4):
        v = _grmsn(_gelu(zvs[b]), gss[b], avg2)
        sv = lo * (_mm(ws[2 * b], v) + bs[2 * b]) + hi * (_mm(ws[2 * b + 1], v) + bs[2 * b + 1])
        ms.append(_gelu(zus[b]) * sv)
    return qr, kr, ms


def _even_operands(p_ref, rs, gq_ref, gk_ref, gs_ref, w_ref, b_ref):
    return (p_ref[rs, 0:512].astype(F32), p_ref[rs, 512:640].astype(F32),
            [p_ref[rs, 768 + 128 * b:896 + 128 * b].astype(F32) for b in range(4)],
            [p_ref[rs, 1280 + 128 * b:1408 + 128 * b].astype(F32) for b in range(4)],
            gq_ref[...], gk_ref[...], [gs_ref[:, 128 * b:128 * b + 128] for b in range(4)],
            [w_ref[g] for g in range(8)], [b_ref[g] for g in range(8)])


def even_tok_fwd(p, cos, sin, gq, gk, gs, sgu_w, sgu_b, avg, masks):
    def body(i, p_ref, cos_ref, sin_ref, gq_ref, gk_ref, gs_ref, w_ref, b_ref, avg_ref, mk_ref, q_ref, kv_ref, m_ref):
        avgv, lo, hi = avg_ref[...], mk_ref[0, :, 0:128], mk_ref[1, :, 0:128]
        for c in range(2):
            rs = pl.ds(c * 128, 128)
            qr, kr, ms = _even_tok(*_even_operands(p_ref, rs, gq_ref, gk_ref, gs_ref, w_ref, b_ref),
                                   cos_ref[rs, :], sin_ref[rs, :], cos_ref[rs, 0:128], sin_ref[rs, 0:128],
                                   avgv, lo, hi)
            q_ref[rs, :] = qr.astype(BF)
            kv_ref[rs, 0:128] = kr.astype(BF)
            kv_ref[rs, 128:256] = p_ref[rs, 640:768]
            for b in range(4):
                m_ref[rs, 128 * b:128 * b + 128] = ms[b].astype(BF)

    return _rb_call("even_tok_fwd", body, row_in=(p,), pos_in=(cos, sin),
                    full_in=(gq, gk, gs, sgu_w, sgu_b, avg, masks), row_out=((512, BF), (256, BF), (512, BF)))


def even_tok_bwd(p, dq, dkvt, dcat, cos, sin, gq, gk, gs, sgu_w, sgu_b, avg, masks):
    def body(i, p_ref, dq_ref, dcat_ref, dkvt_ref, cos_ref, sin_ref, gq_ref, gk_ref, gs_ref, w_ref, b_ref,
             avg_ref, mk_ref, dp_ref, dgq_ref, dgk_ref, dgs_ref, dw_ref, db_ref):
        avgv, lo, hi = avg_ref[...], mk_ref[0, :, 0:128], mk_ref[1, :, 0:128]
        tot = None
        for c in range(2):
            rs = pl.ds(c * 128, 128)
            cq, sq, ck, sk = cos_ref[rs, :], sin_ref[rs, :], cos_ref[rs, 0:128], sin_ref[rs, 0:128]

            def f(q, k, zus, zvs, gq, gk, gss, ws, bs):
                return _even_tok(q, k, zus, zvs, gq, gk, gss, ws, bs, cq, sq, ck, sk, avgv, lo, hi)

            _, vjp = jax.vjp(f, *_even_operands(p_ref, rs, gq_ref, gk_ref, gs_ref, w_ref, b_ref))
            dk = dkvt_ref[0:128, c * 128:(c + 1) * 128].T
            dv = dkvt_ref[128:256, c * 128:(c + 1) * 128].T
            dms = [dcat_ref[rs, 512 + 128 * b:640 + 128 * b].astype(F32) for b in range(4)]
            d = vjp((dq_ref[rs, :].astype(F32), dk, dms))
            dp_ref[rs, 0:512] = d[0].astype(BF)
            dp_ref[rs, 512:640] = d[1].astype(BF)
            dp_ref[rs, 640:768] = dv.astype(BF)
            for b in range(4):
                dp_ref[rs, 768 + 128 * b:896 + 128 * b] = d[2][b].astype(BF)
                dp_ref[rs, 1280 + 128 * b:1408 + 128 * b] = d[3][b].astype(BF)
            part = [d[4], d[5]] + list(d[6]) + list(d[7]) + list(d[8])
            tot = part if tot is None else [x + y for x, y in zip(tot, part)]
        refs = ([dgq_ref, dgk_ref] + [dgs_ref.at[:, 128 * b:128 * b + 128] for b in range(4)]
                + [dw_ref.at[g] for g in range(8)] + [db_ref.at[g] for g in range(8)])
        for ref, val in zip(refs, tot):
            _acc(ref, val, i == 0)

    return _rb_call("even_tok_bwd", body, row_in=(p, dq, dcat), col_in=(dkvt,), pos_in=(cos, sin),
                    full_in=(gq, gk, gs, sgu_w, sgu_b, avg, masks), row_out=((EV_IN, BF),),
                    acc_out=((1, 512), (1, 128), (1, 512), (8, 128, 128), (8, 128, 1)))


MLA_SCALE = 96 ** -0.5
GQA_SCALE = 64 ** -0.5


def _odd_tok(cq, ckv, kr, za, zg, gq, gkv, wq, wkk, wkv, spread, cr, sr, ck, sk):
    cqn = _rmsn(cq, gq)
    q = _rope(_mm(cqn, wq), cr, sr) * MLA_SCALE
    ckn = _rmsn(ckv, gkv)
    k = _mm(ckn, wkk) + _mm(_rope(kr, ck, sk), spread)
    v = _mm(ckn, wkv)
    y = za * jax.nn.sigmoid(zg)
    return q, k, v, y


def odd_tok_fwd(p, cos, sin, gq, gkv, wq, wkk, wkv, spread):
    def body(i, p_ref, cos_ref, sin_ref, gq_ref, gkv_ref, wq_ref, wkk_ref, wkv_ref, sp_ref, q_ref, kv_ref, y_ref):
        q, k, v, y = _odd_tok(
            p_ref[:, 0:256].astype(F32), p_ref[:, 256:384].astype(F32), p_ref[:, 384:512].astype(F32),
            p_ref[:, 512:1024].astype(F32), p_ref[:, 1024:1536].astype(F32),
            gq_ref[...], gkv_ref[...], wq_ref[...], wkk_ref[...], wkv_ref[...], sp_ref[...],
            cos_ref[:, 0:768], sin_ref[:, 0:768], cos_ref[:, 768:896], sin_ref[:, 768:896])
        q_ref[...] = q.astype(BF)
        kv_ref[:, 0:768] = k.astype(BF)
        kv_ref[:, 768:1280] = v.astype(BF)
        y_ref[...] = y.astype(BF)

    return _rb_call("odd_tok_fwd", body, row_in=(p,), pos_in=(cos, sin), full_in=(gq, gkv, wq, wkk, wkv, spread),
                    row_out=((768, BF), (1280, BF), (512, BF)))


def odd_tok_bwd(p, dq, dkvt, dy, cos, sin, gq, gkv, wq, wkk, wkv, spread):
    def body(i, p_ref, dq_ref, dy_ref, dkvt_ref, cos_ref, sin_ref, gq_ref, gkv_ref, wq_ref, wkk_ref, wkv_ref, sp_ref,
             dp_ref, dgq_ref, dgkv_ref, dwq_ref, dwkk_ref, dwkv_ref):
        cr, sr, ck, sk = cos_ref[:, 0:768], sin_ref[:, 0:768], cos_ref[:, 768:896], sin_ref[:, 768:896]
        spread_v = sp_ref[...]

        def f(cq, ckv, kr, za, zg, gq, gkv, wq, wkk, wkv):
            return _odd_tok(cq, ckv, kr, za, zg, gq, gkv, wq, wkk, wkv, spread_v, cr, sr, ck, sk)

        _, vjp = jax.vjp(f, p_ref[:, 0:256].astype(F32), p_ref[:, 256:384].astype(F32),
                         p_ref[:, 384:512].astype(F32), p_ref[:, 512:1024].astype(F32),
                         p_ref[:, 1024:1536].astype(F32), gq_ref[...], gkv_ref[...], wq_ref[...],
                         wkk_ref[...], wkv_ref[...])
        d = vjp((dq_ref[...].astype(F32), dkvt_ref[0:768, :].T, dkvt_ref[768:1280, :].T, dy_ref[...].astype(F32)))
        dp_ref[:, 0:256] = d[0].astype(BF)
        dp_ref[:, 256:384] = d[1].astype(BF)
        dp_ref[:, 384:512] = d[2].astype(BF)
        dp_ref[:, 512:1024] = d[3].astype(BF)
        dp_ref[:, 1024:1536] = d[4].astype(BF)
        for ref, val in zip((dgq_ref, dgkv_ref, dwq_ref, dwkk_ref, dwkv_ref), d[5:]):
            _acc(ref, val, i == 0)

    return _rb_call("odd_tok_bwd", body, row_in=(p, dq, dy), col_in=(dkvt,), pos_in=(cos, sin),
                    full_in=(gq, gkv, wq, wkk, wkv, spread), row_out=((OD_PAD, BF),),
                    acc_out=((1, 256), (1, 128), (256, 768), (128, 768), (128, 512)))


GQA_HEADS = [(64 * h, 64 * (h // 4), 64, 128 + 64 * (h // 4)) for h in range(8)]
MLA_HEADS = [(96 * h, 96 * h, 96, 768 + 64 * h) for h in range(8)]


def _by_block(j, run):
    @pl.when(j == 0)
    def _():
        run(LC)

    @pl.when(j > 0)
    def _():
        run(SEQ)


def attn_fwd(name, q, kv, heads):
    qw, kvw = q.shape[1], kv.shape[1]

    def kern(q_ref, kv_ref, o_ref, lse_ref):
        def run(nk):
            for h, (qo, ko, w, vo) in enumerate(heads):
                s = lax.dot_general(q_ref[:, qo:qo + w], kv_ref[0:nk, ko:ko + w], NT, preferred_element_type=F32)
                m = jnp.max(s, axis=-1, keepdims=True)
                p = jnp.exp(s - m)
                l = jnp.sum(p, axis=-1, keepdims=True)
                o = jnp.dot(p.astype(BF), kv_ref[0:nk, vo:vo + 64], preferred_element_type=F32) / l
                o_ref[:, 64 * h:64 * h + 64] = o.astype(BF)
                lse_ref[:, h:h + 1] = m + jnp.log(l)

        _by_block(pl.program_id(1), run)

    return pl.pallas_call(
        kern, grid=(NEX, BPE),
        in_specs=[pl.BlockSpec((TB, qw), lambda e, j: (e * BPE + j, 0)),
                  pl.BlockSpec((SEQ, kvw), lambda e, j: (e, 0))],
        out_specs=[pl.BlockSpec((TB, 512), lambda e, j: (e * BPE + j, 0)),
                   pl.BlockSpec((TB, 8), lambda e, j: (e * BPE + j, 0))],
        out_shape=[jax.ShapeDtypeStruct((R, 512), BF), jax.ShapeDtypeStruct((R, 8), F32)],
        compiler_params=_cparams(("parallel", "arbitrary")), name=name)(q, kv)


def attn_bwd(name, q, kv, o, dcat, lse, heads):
    qw, kvw = q.shape[1], kv.shape[1]

    def kern(q_ref, kv_ref, o_ref, do_ref, lse_ref, dq_ref, dkvt_ref):
        j = pl.program_id(1)

        @pl.when(j == 0)
        def _():
            dkvt_ref[...] = jnp.zeros_like(dkvt_ref)

        def run(nk):
            for h, (qo, ko, w, vo) in enumerate(heads):
                qh = q_ref[:, qo:qo + w]
                kh = kv_ref[0:nk, ko:ko + w]
                s = lax.dot_general(qh, kh, NT, preferred_element_type=F32)
                p = jnp.exp(s - lse_ref[:, h:h + 1])
                do = do_ref[:, 64 * h:64 * h + 64]
                dsum = jnp.sum(do.astype(F32) * o_ref[:, 64 * h:64 * h + 64].astype(F32), axis=-1, keepdims=True)
                dp = lax.dot_general(do, kv_ref[0:nk, vo:vo + 64], NT, preferred_element_type=F32)
                ds = (p * (dp - dsum)).astype(BF)
                dkvt_ref[vo:vo + 64, 0:nk] += lax.dot_general(do, p.astype(BF), TN, preferred_element_type=F32)
                dq_ref[:, qo:qo + w] = jnp.dot(ds, kh, preferred_element_type=F32).astype(BF)
                dkvt_ref[ko:ko + w, 0:nk] += lax.dot_general(qh, ds, TN, preferred_element_type=F32)

        _by_block(j, run)

    return pl.pallas_call(
        kern, grid=(NEX, BPE),
        in_specs=[pl.BlockSpec((TB, qw), lambda e, j: (e * BPE + j, 0)),
                  pl.BlockSpec((SEQ, kvw), lambda e, j: (e, 0)),
                  pl.BlockSpec((TB, 512), lambda e, j: (e * BPE + j, 0)),
                  pl.BlockSpec((TB, 512), lambda e, j: (e * BPE + j, 0)),
                  pl.BlockSpec((TB, 8), lambda e, j: (e * BPE + j, 0))],
        out_specs=[pl.BlockSpec((TB, qw), lambda e, j: (e * BPE + j, 0)),
                   pl.BlockSpec((kvw, SEQ), lambda e, j: (0, e))],
        out_shape=[jax.ShapeDtypeStruct((R, qw), BF), jax.ShapeDtypeStruct((kvw, R), F32)],
        compiler_params=_cparams(("parallel", "arbitrary")), name=name)(q, kv, o, dcat, lse)


HALO = 16
CONV_K = 31


def _fill_ext(ext_ref, prev_ref, cur_ref, next_ref, i):
    j = i % BPE
    has_prev = (j >= 2).astype(F32)
    has_next = jnp.logical_and(j >= 1, j <= BPE - 2).astype(F32)
    ext_ref[0:HALO, :] = prev_ref[TB - HALO:TB, :].astype(F32) * has_prev
    ext_ref[HALO:HALO + TB, :] = cur_ref[...].astype(F32)
    ext_ref[HALO + TB:2 * HALO + TB, :] = next_ref[0:HALO, :].astype(F32) * has_next


PHASE_ROWS = TB + 24


def _phases(ext_ref, ph_ref):
    for r in range(8):
        ph_ref[r] = ext_ref[r:r + PHASE_ROWS, :]


def _window(ph_ref, off):
    return ph_ref[off % 8, 8 * (off // 8):8 * (off // 8) + TB, :]


def _ln_silu(z, g, b):
    mu = jnp.mean(z, axis=-1, keepdims=True)
    zc = z - mu
    var = jnp.mean(zc * zc, axis=-1, keepdims=True)
    return _silu(zc * lax.rsqrt(var + EPS) * g + b)


def conf_fwd(y, cw, cb, lg, lb):
    def body(i, cur_ref, cw_ref, cb_ref, lg_ref, lb_ref, prev_ref, next_ref, z_ref, c_ref, ext_ref, ph_ref):
        _fill_ext(ext_ref, prev_ref, cur_ref, next_ref, i)
        _phases(ext_ref, ph_ref)
        acc = _window(ph_ref, 1) * cw_ref[0:1, :]
        for k in range(1, CONV_K):
            acc = acc + _window(ph_ref, k + 1) * cw_ref[k:k + 1, :]
        z = acc + cb_ref[...]
        z_ref[...] = z.astype(BF)
        c_ref[...] = _ln_silu(z, lg_ref[...], lb_ref[...]).astype(BF)

    def idle(i, cur_ref, cw_ref, cb_ref, lg_ref, lb_ref, prev_ref, next_ref, z_ref, c_ref, ext_ref, ph_ref):
        _zero(z_ref, c_ref)

    return _rb_call("conf_fwd", body, row_in=(y,), full_in=(cw, cb, lg, lb), shift_in=((y, -1), (y, 1)),
                    row_out=((512, BF), (512, BF)), idle=idle,
                    scratch=(pltpu.VMEM((TB + 2 * HALO, 512), F32), pltpu.VMEM((8, PHASE_ROWS, 512), F32)))


def conf_bwd_ln(z, dcat, lg, lb):
    def body(i, z_ref, dcat_ref, lg_ref, lb_ref, dz_ref, dlg_ref, dlb_ref, dcb_ref):
        _, vjp = jax.vjp(_ln_silu, z_ref[...].astype(F32), lg_ref[...], lb_ref[...])
        dz, dlg, dlb = vjp(dcat_ref[:, 512:1024].astype(F32))
        dz_ref[...] = dz.astype(BF)
        _acc(dlg_ref, dlg, i == 0)
        _acc(dlb_ref, dlb, i == 0)
        _acc(dcb_ref, jnp.sum(dz, axis=0, keepdims=True), i == 0)

    def idle(i, z_ref, dcat_ref, lg_ref, lb_ref, dz_ref, dlg_ref, dlb_ref, dcb_ref):
        _zero(dz_ref)
        _zero_at_start(i, dlg_ref, dlb_ref, dcb_ref)

    return _rb_call("conf_bwd_ln", body, row_in=(z, dcat), full_in=(lg, lb), row_out=((512, BF),),
                    acc_out=((1, 512), (1, 512), (1, 512)), idle=idle)


def conf_bwd_conv(y, dz, cw):
    def body(i, y_ref, dz_ref, cw_ref, yp_ref, yn_ref, dzp_ref, dzn_ref, dy_ref, dcw_ref, ext_ref, phy_ref, phd_ref):
        _fill_ext(ext_ref, yp_ref, y_ref, yn_ref, i)
        _phases(ext_ref, phy_ref)
        _fill_ext(ext_ref, dzp_ref, dz_ref, dzn_ref, i)
        _phases(ext_ref, phd_ref)
        dzv = dz_ref[...].astype(F32)

        @pl.when(i == 0)
        def _():
            dcw_ref[...] = jnp.zeros_like(dcw_ref)

        acc = None
        for k in range(CONV_K):
            t = _window(phd_ref, CONV_K - k) * cw_ref[k:k + 1, :]
            acc = t if acc is None else acc + t
            dcw_ref[k:k + 1, :] += jnp.sum(dzv * _window(phy_ref, k + 1), axis=0, keepdims=True)
        dy_ref[...] = acc.astype(BF)

    def idle(i, y_ref, dz_ref, cw_ref, yp_ref, yn_ref, dzp_ref, dzn_ref, dy_ref, dcw_ref, ext_ref, phy_ref, phd_ref):
        _zero(dy_ref)
        _zero_at_start(i, dcw_ref)

    return _rb_call("conf_bwd_conv", body, row_in=(y, dz), full_in=(cw,), idle=idle,
                    shift_in=((y, -1), (y, 1), (dz, -1), (dz, 1)), row_out=((512, BF),), acc_out=((32, 512),),
                    scratch=(pltpu.VMEM((TB + 2 * HALO, 512), F32), pltpu.VMEM((8, PHASE_ROWS, 512), F32),
                             pltpu.VMEM((8, PHASE_ROWS, 512), F32)))


def final_loss(x, target, fg, y, mods, m_gate):
    lpb = L // TB

    def kern(x_ref, t_ref, g_ref, y_ref, gt_ref, dx_ref, dy_ref, dgt_ref, loss_ref, dg_ref):
        i = pl.program_id(0)

        @pl.when((i % BPE) == 0)
        def _():
            _zero(dx_ref, dy_ref, dgt_ref)
            _zero_at_start(i, loss_ref, dg_ref)

        @pl.when((i % BPE) >= 1)
        def _():
            tv = t_ref[...]

            def f(x, g):
                err = _rmsn(x, g) - tv
                rowsum = jnp.sum(err * err, axis=-1, keepdims=True)
                return jnp.sum(rowsum, axis=0, keepdims=True) * (0.5 / D)

            lv, vjp = jax.vjp(f, x_ref[...], g_ref[...])
            dx, dg = vjp(jnp.ones((1, 1), F32))
            dx_ref[...] = dx
            _gate_grads(dx, y_ref, gt_ref, dy_ref, dgt_ref, i)
            loss_ref[...] += jnp.zeros((8, 128), F32) + lv
            dg_ref[...] += dg

    row = pl.BlockSpec((TB, D), lambda i: (i, 0))
    return pl.pallas_call(
        kern, grid=(NBLK,),
        in_specs=[row, pl.BlockSpec((TB, D), lambda i: ((i // BPE) * lpb + jnp.maximum(i % BPE - 1, 0), 0)),
                  pl.BlockSpec((1, D), lambda i: (0, 0)), row,
                  pl.BlockSpec((1, 1, D), lambda i: (_seg(i) * N_MOD + m_gate, 0, 0))],
        out_specs=[row, row, pl.BlockSpec((1, 1, D), lambda i: (_seg(i), 0, 0)),
                   pl.BlockSpec((8, 128), lambda i: (0, 0)), pl.BlockSpec((1, D), lambda i: (0, 0))],
        out_shape=[jax.ShapeDtypeStruct((R, D), F32), jax.ShapeDtypeStruct((R, D), BF),
                   jax.ShapeDtypeStruct((4, 1, D), F32), jax.ShapeDtypeStruct((8, 128), F32),
                   jax.ShapeDtypeStruct((1, D), F32)],
        compiler_params=_cparams(("arbitrary",)), name="final_loss")(x, target, fg, y, mods)


NC = 24


def mods_fwd(call, ada_w, ada_b):
    cols = ada_w.shape[2]

    def kern(c_ref, w_ref, b_ref, o_ref):
        o_ref[...] = jnp.dot(_silu(c_ref[...]), w_ref[...], precision=HI, preferred_element_type=F32) + b_ref[...]

    return pl.pallas_call(
        kern, grid=(2,),
        in_specs=[pl.BlockSpec((NC, D), lambda l: (0, 0)), pl.BlockSpec((None, D, cols), lambda l: (l, 0, 0)),
                  pl.BlockSpec((None, 1, cols), lambda l: (l, 0, 0))],
        out_specs=pl.BlockSpec((None, NC, cols), lambda l: (l, 0, 0)),
        out_shape=jax.ShapeDtypeStruct((2, NC, cols), F32),
        compiler_params=_cparams(("parallel",)), name="mods_fwd")(call, ada_w, ada_b)


def ada_bwd(call, ada_w, dm):
    cols = ada_w.shape[2]

    def kern(c_ref, w_ref, dm_ref, gw_ref, dc_ref):
        l = pl.program_id(0)
        gw_ref[...] = lax.dot_general(_silu(c_ref[...]), dm_ref[...], TN, precision=HI, preferred_element_type=F32)
        part = lax.dot_general(dm_ref[16:24, :], w_ref[...], NT, precision=HI, preferred_element_type=F32)
        cc = c_ref[16:17, :]
        sg = jax.nn.sigmoid(cc)
        _acc(dc_ref, part * (sg * (1.0 + cc * (1.0 - sg))), l == 0)

    return pl.pallas_call(
        kern, grid=(2,),
        in_specs=[pl.BlockSpec((NC, D), lambda l: (0, 0)), pl.BlockSpec((None, D, cols), lambda l: (l, 0, 0)),
                  pl.BlockSpec((None, NC, cols), lambda l: (l, 0, 0))],
        out_specs=[pl.BlockSpec((None, D, cols), lambda l: (l, 0, 0)), pl.BlockSpec((8, D), lambda l: (0, 0))],
        out_shape=[jax.ShapeDtypeStruct((2, D, cols), F32), jax.ShapeDtypeStruct((8, D), F32)],
        compiler_params=_cparams(("arbitrary",)), name="ada_bwd")(call, ada_w, dm)


def sum_lead(name, a, after=None):
    n, r, c = a.shape
    tr = r
    for cand in (512, 256, 128, 64, 32, 16, 8):
        if r % cand == 0 and cand * c * 4 * n <= 8 * 1024 * 1024:
            tr = cand
            break
    extra = [] if after is None else [after]

    def kern(a_ref, *rest):
        acc = a_ref[0].astype(F32)
        for k in range(1, n):
            acc = acc + a_ref[k].astype(F32)
        rest[-1][...] = acc

    return pl.pallas_call(
        kern, grid=(r // tr,),
        in_specs=[pl.BlockSpec((n, tr, c), lambda i: (0, i, 0))]
        + [pl.BlockSpec(e.shape, lambda i, k=e.ndim: (0,) * k) for e in extra],
        out_specs=pl.BlockSpec((tr, c), lambda i: (i, 0)), out_shape=jax.ShapeDtypeStruct((r, c), F32),
        compiler_params=_cparams(("parallel",)), name=name)(a, *extra)


def add_pairs(name, hs, got, half):
    _, _, r, c = hs.shape

    def kern(half_ref, a_ref, b_ref, o_ref):
        o_ref[...] = (a_ref[...].astype(F32) + b_ref[...].astype(F32)).astype(BF)

    spec = pl.BlockSpec((None, r, c), lambda j, h: (j, 0, 0))
    grid_spec = pltpu.PrefetchScalarGridSpec(
        num_scalar_prefetch=1, grid=(4,),
        in_specs=[pl.BlockSpec((None, None, r, c), lambda j, h: (h[0], j, 0, 0)), spec], out_specs=spec)
    return pl.pallas_call(kern, grid_spec=grid_spec, out_shape=jax.ShapeDtypeStruct(got.shape, BF),
                          compiler_params=_cparams(("parallel",)), name=name)(half, hs, got)


def sum_slabs(name, land, own, where, full, lead):
    _, r, c = land.shape
    tr = r
    for cand in (512, 256, 128, 64, 32, 16):
        if r % cand == 0 and cand * c * 16 <= 4 * 1024 * 1024:
            tr = cand
            break

    def kern(where_ref, full_ref, land_ref, own_ref, o_ref):
        me = where_ref[0]
        acc = None
        for k in range(4):
            t = jnp.where(me == k, own_ref[k], land_ref[k]).astype(F32)
            acc = t if acc is None else acc + t
        o_ref[...] = acc

    spec = pl.BlockSpec((4, tr, c), lambda i, m: (0, i, 0))
    grid_spec = pltpu.PrefetchScalarGridSpec(
        num_scalar_prefetch=1, grid=(r // tr,), in_specs=[pl.BlockSpec(memory_space=pl.ANY), spec, spec],
        out_specs=pl.BlockSpec((None, None, tr, c), lambda i, m: (lead, m[1], i, 0)))
    return pl.pallas_call(kern, grid_spec=grid_spec, out_shape=jax.ShapeDtypeStruct(full.shape, F32),
                          input_output_aliases={1: 0}, compiler_params=_cparams(("parallel",)),
                          name=name)(where, full, land, own)


def adamw(name, w, g, m, v, again=False):
    r, c = w.shape
    tr = r
    for cand in (512, 256, 128, 64, 32, 16, 8):
        if r % cand == 0 and cand * c * 4 <= 2 * 1024 * 1024:
            tr = cand
            break
    c1 = 1.0 / (1.0 - ADAM_B1 ** ADAM_STEP)
    c2 = 1.0 / (1.0 - ADAM_B2 ** ADAM_STEP)

    def kern(w_ref, g_ref, m_ref, v_ref, d_ref, mo_ref, vo_ref, *go_ref):
        gv = g_ref[...]
        mn = ADAM_B1 * m_ref[...] + (1.0 - ADAM_B1) * gv
        vn = ADAM_B2 * v_ref[...] + (1.0 - ADAM_B2) * (gv * gv)
        d_ref[...] = -ADAM_LR * ((mn * c1) / (jnp.sqrt(vn * c2) + ADAM_EPS) + ADAM_WD * w_ref[...])
        mo_ref[...] = mn
        vo_ref[...] = vn
        if again:
            go_ref[0][...] = gv

    spec = pl.BlockSpec((tr, c), lambda i: (i, 0))
    shp = jax.ShapeDtypeStruct((r, c), F32)
    n_out = 4 if again else 3
    return pl.pallas_call(kern, grid=(r // tr,), in_specs=[spec] * 4, out_specs=[spec] * n_out,
                          out_shape=[shp] * n_out, compiler_params=_cparams(("parallel",)), name=name)(w, g, m, v)


def adamw_many(name, ws, gs, ms, vs):
    n = len(ws)
    c1 = 1.0 / (1.0 - ADAM_B1 ** ADAM_STEP)
    c2 = 1.0 / (1.0 - ADAM_B2 ** ADAM_STEP)

    def kern(*refs):
        w, g, m, v, d, mo, vo = (refs[k * n:(k + 1) * n] for k in range(7))
        for k in range(n):
            gv = g[k][...]
            mn = ADAM_B1 * m[k][...] + (1.0 - ADAM_B1) * gv
            vn = ADAM_B2 * v[k][...] + (1.0 - ADAM_B2) * (gv * gv)
            d[k][...] = -ADAM_LR * ((mn * c1) / (jnp.sqrt(vn * c2) + ADAM_EPS) + ADAM_WD * w[k][...])
            mo[k][...] = mn
            vo[k][...] = vn

    out = pl.pallas_call(kern, out_shape=[jax.ShapeDtypeStruct(a.shape, F32) for a in ws] * 3,
                         compiler_params=pltpu.CompilerParams(vmem_limit_bytes=VMEM_LIMIT),
                         name=name)(*ws, *gs, *ms, *vs)
    return out[:n], out[n:2 * n], out[2 * n:]


def all_gather8(name, xs, after=None):
    m_per, n = xs.shape
    extra = [] if after is None else [after]

    def body(x_ref, *rest):
        out_ref, send_sems, recv_sems, local_sem = rest[len(extra):]
        x, y, c = lax.axis_index("x"), lax.axis_index("y"), lax.axis_index("c")
        me, sibling = (x, y, c), (x, y, 1 - c)
        chips = [(1 - x, y), (x, 1 - y), (1 - x, 1 - y)]

        def rows(px, py, pc):
            return out_ref.at[pl.ds((4 * px + 2 * py + pc) * m_per, m_per), :]

        def copy(k, block, to, src=None):
            return pltpu.make_async_remote_copy(
                src_ref=rows(*block) if src is None else src, dst_ref=rows(*block),
                send_sem=send_sems.at[k], recv_sem=recv_sems.at[k], device_id=to, device_id_type=MESH)

        mine = pltpu.make_async_copy(x_ref, rows(*me), local_sem)
        mine.start()
        first = [copy(0, me, sibling, src=x_ref)]
        first += [copy(1 + j, me, (*chip, c), src=x_ref) for j, chip in enumerate(chips)]
        for cp in first:
            cp.start()
        passed = [copy(4 + j, (*chip, c), sibling) for j, chip in enumerate(chips)]
        for j, chip in enumerate(chips):
            copy(1 + j, (*chip, c), me).wait_recv()
            passed[j].start()
        copy(0, sibling, me).wait_recv()
        for j, chip in enumerate(chips):
            copy(4 + j, (*chip, 1 - c), me).wait_recv()
        for cp in first + passed:
            cp.wait_send()
        mine.wait()

    return pl.pallas_call(
        body, out_shape=jax.ShapeDtypeStruct((8 * m_per, n), xs.dtype),
        in_specs=[pl.BlockSpec(memory_space=pltpu.VMEM)] * (1 + len(extra)),
        out_specs=pl.BlockSpec(memory_space=pltpu.VMEM),
        scratch_shapes=[pltpu.SemaphoreType.DMA((7,)), pltpu.SemaphoreType.DMA((7,)), pltpu.SemaphoreType.DMA],
        compiler_params=pltpu.CompilerParams(vmem_limit_bytes=VMEM_LIMIT), name=name)(xs, *extra)


def sibling_merge(name, fulls):
    n = len(fulls)
    slots = [(a, l) for a in range(n) for l in range(fulls[a].shape[0])]

    def body(*refs):
        buf = refs[n:2 * n]
        send_sems, recv_sems = refs[2 * n], refs[2 * n + 1]
        c = lax.axis_index("c")
        sibling = (lax.axis_index("x"), lax.axis_index("y"), 1 - c)
        sends, recvs = [], []
        for k, (a, l) in enumerate(slots):
            kw = dict(send_sem=send_sems.at[k], recv_sem=recv_sems.at[k], device_id=sibling, device_id_type=MESH)
            sends.append(pltpu.make_async_remote_copy(src_ref=buf[a].at[l, c], dst_ref=buf[a].at[l, c], **kw))
            recvs.append(pltpu.make_async_remote_copy(src_ref=buf[a].at[l, c], dst_ref=buf[a].at[l, 1 - c], **kw))
        for cp in sends:
            cp.start()
        for cp in recvs:
            cp.wait_recv()
        for cp in sends:
            cp.wait_send()

    anyspec = pl.BlockSpec(memory_space=pl.ANY)
    return pl.pallas_call(
        body, out_shape=[jax.ShapeDtypeStruct(s.shape, s.dtype) for s in fulls],
        in_specs=[anyspec] * n, out_specs=[anyspec] * n, input_output_aliases={a: a for a in range(n)},
        scratch_shapes=[pltpu.SemaphoreType.DMA((len(slots),)), pltpu.SemaphoreType.DMA((len(slots),))],
        name=name)(*fulls)


def place_own(name, land, src, chip):
    c = src.shape[-1]
    r = src.size // c
    tr = r
    for cand in (1024, 512, 256, 128, 64, 32, 16):
        if r % cand == 0 and cand * c * 2 <= 2 * 1024 * 1024:
            tr = cand
            break

    def kern(chip_ref, land_ref, src_ref, out_ref):
        out_ref[...] = src_ref[...]

    grid_spec = pltpu.PrefetchScalarGridSpec(
        num_scalar_prefetch=1, grid=(r // tr,),
        in_specs=[pl.BlockSpec(memory_space=pl.ANY), pl.BlockSpec((tr, c), lambda i, m: (i, 0))],
        out_specs=pl.BlockSpec((None, tr, c), lambda i, m: (m[0], i, 0)))
    out = pl.pallas_call(kern, grid_spec=grid_spec, out_shape=jax.ShapeDtypeStruct((4, r, c), land.dtype),
                         input_output_aliases={1: 0}, compiler_params=_cparams(("parallel",)),
                         name=name)(chip, land.reshape(4, r, c), src.reshape(r, c))
    return out.reshape(land.shape)


def _half_copies(src, land, send_sems, recv_sems):
    c = lax.axis_index("c")
    sibling = (lax.axis_index("x"), lax.axis_index("y"), 1 - c)
    pairs = []
    for a in range(len(src)):
        cp = pltpu.make_async_remote_copy(src_ref=src[a].at[1 - c], dst_ref=land[a], send_sem=send_sems.at[a],
                                          recv_sem=recv_sems.at[a], device_id=sibling, device_id_type=MESH)
        pairs.append((cp, cp))
    return pairs


def _chip_copies(src, land, send_sems, recv_sems, scatter):
    x, y, c = lax.axis_index("x"), lax.axis_index("y"), lax.axis_index("c")
    me = 2 * x + y
    pairs = []
    for a in range(len(src)):
        for j, (px, py) in enumerate([(1 - x, y), (x, 1 - y), (1 - x, 1 - y)]):
            to = 2 * px + py
            out = src[a].at[to] if scatter else src[a]
            kw = dict(send_sem=send_sems.at[3 * a + j], recv_sem=recv_sems.at[3 * a + j], device_id=(px, py, c),
                      device_id_type=MESH)
            pairs.append((pltpu.make_async_remote_copy(src_ref=out, dst_ref=land[a].at[me], **kw),
                          pltpu.make_async_remote_copy(src_ref=out, dst_ref=land[a].at[to], **kw)))
    return pairs


_HBM = pl.BlockSpec(memory_space=pltpu.HBM)
_SEM = pl.BlockSpec(memory_space=pltpu.SEMAPHORE)


GATHER = (functools.partial(_chip_copies, scatter=False), 3)
SCATTER = (functools.partial(_chip_copies, scatter=True), 3)
TO_SIBLING = (_half_copies, 1)


def exchange_start(name, groups, plan):
    copies, per = plan
    sizes = [len(s) for s, _ in groups]
    flat = [a for s, l in groups for a in list(s) + list(l)]
    ng = len(groups)

    def body(*refs):
        ins, outs = refs[:len(flat)], refs[len(flat):]
        off = 0
        for g, n in enumerate(sizes):
            src, land = ins[off:off + n], ins[off + n:off + 2 * n]
            off += 2 * n
            for send, _ in copies(src, land, outs[2 * g], outs[2 * g + 1]):
                send.start()
        outs[-1][...] = jnp.zeros_like(outs[-1])

    out_shape = []
    for n in sizes:
        out_shape += [pltpu.SemaphoreType.DMA((per * n,)), pltpu.SemaphoreType.DMA((per * n,))]
    out_shape += [pltpu.HBM(a.shape, a.dtype) for a in flat] + [jax.ShapeDtypeStruct((8, 128), F32)]
    res = pl.pallas_call(
        body, out_shape=tuple(out_shape), in_specs=[_HBM] * len(flat),
        out_specs=tuple([_SEM] * (2 * ng) + [_HBM] * len(flat) + [pl.BlockSpec(memory_space=pltpu.VMEM)]),
        input_output_aliases={k: 2 * ng + k for k in range(len(flat))},
        compiler_params=pltpu.CompilerParams(has_side_effects=pltpu.SideEffectType.DATAFLOW_SIDE_EFFECTING),
        name=name)(*[pltpu.with_memory_space_constraint(a, pltpu.HBM) for a in flat])
    handles, off = [], 2 * ng
    for g, n in enumerate(sizes):
        handles.append((res[2 * g], res[2 * g + 1], list(res[off:off + n]), list(res[off + n:off + 2 * n])))
        off += 2 * n
    return handles, res[-1]


def exchange_wait(name, handle, after, plan):
    send_sems, recv_sems, srcs, lands = handle
    n = len(srcs)

    def body(*refs):
        src, land = refs[:n], refs[n:2 * n]
        for send, recv in plan[0](src, land, refs[2 * n], refs[2 * n + 1]):
            send.wait_send()
            recv.wait_recv()

    res = pl.pallas_call(
        body, out_shape=tuple(pltpu.HBM(a.shape, a.dtype) for a in srcs + lands),
        in_specs=[_HBM] * (2 * n) + [_SEM, _SEM, pl.BlockSpec(memory_space=pl.ANY)],
        out_specs=tuple([_HBM] * (2 * n)), input_output_aliases={k: k for k in range(2 * n)},
        compiler_params=pltpu.CompilerParams(has_side_effects=pltpu.SideEffectType.DATAFLOW_SIDE_EFFECTING),
        name=name)(*srcs, *lands, send_sems, recv_sems, after)
    return list(res[:n]), list(res[n:])


def _rope_tables(d_rot, reps):
    rows = L // GRID_W
    row = np.repeat(np.arange(rows), GRID_W).astype(np.float32)
    col = np.tile(np.arange(GRID_W), rows).astype(np.float32)
    d_axis = d_rot // 2
    inv = (ROPE_THETA ** (-np.arange(0, d_axis, 2, dtype=np.float32) / d_axis)).astype(np.float32)
    ang = np.concatenate([row[:, None] * inv, col[:, None] * inv], axis=-1).astype(np.float32)
    cos, sin = np.cos(ang).astype(np.float32), np.sin(ang).astype(np.float32)
    c = np.repeat(cos, 2, axis=-1)
    s = np.stack([-sin, sin], axis=-1).reshape(L, d_rot)
    c = np.concatenate([np.ones((LC, d_rot), np.float32), c], axis=0)
    s = np.concatenate([np.zeros((LC, d_rot), np.float32), s], axis=0)
    return np.tile(c, (1, reps)), np.tile(s, (1, reps))


def _group_consts():
    g = np.arange(512) // 64
    avg = (g[:, None] == g[None, :]).astype(np.float32) / 64.0
    masks = (np.arange(8)[:, None] == g[None, :]).astype(np.float32).reshape(8, 1, 512)
    return jnp.asarray(avg, BF), jnp.asarray(masks)


def _pack(items):
    flat = jnp.concatenate([a.reshape(-1).astype(F32) for a in items])
    n = flat.shape[0]
    rows = -(-n // D)
    rows = -(-rows // 8) * 8
    return jnp.pad(flat, (0, rows * D - n)).reshape(rows, D)


def _unpack(buf, shapes):
    lead = buf.shape[:-2]
    flat = buf.reshape(lead + (-1,))
    out, off = [], 0
    for shp in shapes:
        n = int(np.prod(shp))
        out.append(flat[..., off:off + n].reshape(lead + tuple(shp)))
        off += n
    return out


def _arrive(prm, key, after):
    if callable(prm[key]):
        prm[key](after)
    return prm[key]


def _layer_fwd(i, x, h, mods, prm, consts, nxt):
    sv = {}
    sv["x0"] = x
    sv["h"] = h
    p = proj_in(f"proj_in_{i}", h, _arrive(prm, "w_in", h))
    sv["p"] = p
    if i == 0:
        q, kv, m2 = even_tok_fwd(p, consts["cos_e"], consts["sin_e"], prm["gq"], prm["gk"], prm["gs"],
                                 prm["sgu_w"], prm["sgu_b"], consts["avg"], consts["masks"])
        o, lse = attn_fwd("attn_fwd_0", q, kv, GQA_HEADS)
        sv.update(q=q, kv=kv)
    else:
        q, kv, y = odd_tok_fwd(p, consts["cos_o"], consts["sin_o"], prm["gq"], prm["gkv"], prm["wq"], prm["wkk"],
                               prm["wkv"], consts["spread"])
        o, lse = attn_fwd("attn_fwd_1", q, kv, MLA_HEADS)
        z, m2 = conf_fwd(y, prm["conv_w"], prm["conv_b"], prm["ln_g"], prm["ln_b"])
        sv.update(q=q, kv=kv, y=y, z=z)
    sv.update(o=o, lse=lse, m2=m2)
    x1, y1, h2 = proj_out(f"proj_out_{i}", o, m2, _arrive(prm, "w_out", o), x, mods, 2, prm["norm2_g"], 3, 4,
                          last=nxt is None)
    sv.update(x1=x1, y1=y1)
    a, f = mlp_up(f"mlp_up_{i}", h2, _arrive(prm, "w1", h2))
    x2, y2, *h_next = mlp_down(f"mlp_down_{i}", f, prm["w2"], x1, mods, 5, nxt)
    sv.update(h2=h2, a=a, f=f, y2=y2)
    return x2, (h_next[0] if h_next else None), sv


def _layer_bwd(i, dx, dy2, dg2, sv, mods, prm, consts, hook, entry, below):
    gr = {}
    da = mlp_bwd_da(f"mlp_bwd_da_{i}", dy2, prm["w2"], sv["a"], after=entry)
    tiles8 = [(h, j) for h in range(2) for j in range(4)]
    gr["w1"] = mm_tn(f"grad_w1_{i}", sv["h2"], da, tiles8, 512, D).reshape(2, 4, 512, D)
    gr["w2"] = mm_tn(f"grad_w2_{i}", sv["f"], dy2, [(2 * j + h, 0) for h in range(2) for j in range(4)],
                     512, D).reshape(2, 4, 512, D)
    dh2 = mlp_bwd_dh(f"mlp_bwd_dh_{i}", da, prm["w1"])
    dx1, dy1, dg1, dsh2, dsc2, gr["norm2_g"] = modnorm_bwd(
        f"norm2_bwd_{i}", sv["x1"], dh2, dx, mods, prm["norm2_g"], 3, 4, gate=(sv["y1"], mods, 2),
        after=hook(f"{i}:mlp", gr, dh2), last=below is not None)
    dcat = mm_nt(f"proj_out_bwd_{i}", dy1, prm["w_out"], after=hook(f"{i}:mid", gr, dy1))
    go = mm_tn(f"grad_wout_a_{i}", sv["o"], dy1, [(0, 0)], 512, D).reshape(2, 2, 128, D)
    gm = mm_tn(f"grad_wout_b_{i}", sv["m2"], dy1, [(0, 0)], 512, D).reshape(2, 2, 128, D)
    gr["w_out"] = jnp.concatenate([go, gm], axis=0).transpose(1, 0, 2, 3)
    if i == 0:
        dq, dkv = attn_bwd("attn_bwd_0", sv["q"], sv["kv"], sv["o"], dcat, sv["lse"], GQA_HEADS)
        dp, gr["gq"], gr["gk"], gr["gs"], gr["sgu_w"], gr["sgu_b"] = even_tok_bwd(
            sv["p"], dq, dkv, dcat, consts["cos_e"], consts["sin_e"], prm["gq"], prm["gk"],
            prm["gs"], prm["sgu_w"], prm["sgu_b"], consts["avg"], consts["masks"])
    else:
        dq, dkv = attn_bwd("attn_bwd_1", sv["q"], sv["kv"], sv["o"], dcat, sv["lse"], MLA_HEADS)
        dz, gr["ln_g"], gr["ln_b"], gr["conv_b"] = conf_bwd_ln(sv["z"], dcat, prm["ln_g"], prm["ln_b"])
        dyc, gr["conv_w"] = conf_bwd_conv(sv["y"], dz, prm["conv_w"])
        dp, gr["gq"], gr["gkv"], gr["wq"], gr["wkk"], gr["wkv"] = odd_tok_bwd(
            sv["p"], dq, dkv, dyc, consts["cos_o"], consts["sin_o"], prm["gq"], prm["gkv"], prm["wq"], prm["wkk"],
            prm["wkv"], consts["spread"])
    n_in = prm["w_in"].shape[1]
    gr["w_in"] = mm_tn(f"grad_win_{i}", sv["h"], dp, [(0, 0), (1, 0)], 512, n_in)
    dh = mm_nt(f"proj_in_bwd_{i}", dp, prm["w_in"])
    if below:
        dx0, dy2b, dg2b, dsh1, dsc1, gr["norm1_g"] = modnorm_bwd(
            f"norm1_bwd_{i}", sv["x0"], dh, dx1, mods, prm["norm1_g"], 0, 1, gate=(below[0], below[1], 5))
        down = (dy2b, dg2b)
    else:
        dx0, dsh1, dsc1, gr["norm1_g"] = modnorm_bwd(f"norm1_bwd_{i}", sv["x0"], dh, dx1, mods, prm["norm1_g"], 0, 1,
                                                     lat_only=True)
        down = None
    dmods = jnp.concatenate([dsh1, dsc1, dg1, dsh2, dsc2, dg2], axis=1)
    return dx0, down, dmods, gr, hook(f"{i}:end", gr, dx0)


def local_step(xcat, target, mods, prms, final_g, hook=lambda point, grads, fresh: None):
    avg, masks = _group_consts()
    cos_e, sin_e = _rope_tables(64, 8)
    ck, sk = _rope_tables(32, 1)
    one64, zero64 = np.ones((SEQ, 64), np.float32), np.zeros((SEQ, 64), np.float32)
    one96, zero96 = np.ones((SEQ, 96), np.float32), np.zeros((SEQ, 96), np.float32)
    cos_o = np.concatenate([np.tile(np.concatenate([one64, ck], axis=1), (1, 8)), ck, one96], axis=1)
    sin_o = np.concatenate([np.tile(np.concatenate([zero64, sk], axis=1), (1, 8)), sk, zero96], axis=1)
    lane = np.arange(768)
    spread = np.zeros((128, 768), np.float32)
    spread[lane % 96 - 64, lane] = (lane % 96 >= 64)
    consts = dict(avg=avg, masks=masks, cos_e=jnp.asarray(cos_e), sin_e=jnp.asarray(sin_e),
                  cos_o=jnp.asarray(cos_o), sin_o=jnp.asarray(sin_o), spread=jnp.asarray(spread, BF))
    x = xcat
    h = modnorm_fwd("norm1_fwd_0", x, mods[0], prms[0]["norm1_g"], 0, 1)
    saved = []
    for i in range(2):
        x, h, sv = _layer_fwd(i, x, h, mods[i], prms[i], consts, (mods[1], prms[1]["norm1_g"]) if i == 0 else None)
        saved.append(sv)
    dx, dy2, dg2, loss, dfg = final_loss(x, target, final_g, saved[1]["y2"], mods[1], 5)
    dmods, grads = [None, None], [None, None]
    entry, down = None, (dy2, dg2)
    for i in (1, 0):
        below = (saved[0]["y2"], mods[0]) if i == 1 else None
        dx, down, dmods[i], grads[i], entry = _layer_bwd(i, dx, down[0], down[1], saved[i], mods[i], prms[i], consts,
                                                         hook, entry, below)
    return loss, dx, dmods, grads, dfg, entry


def _row(v):
    return v.reshape(1, -1).astype(F32)


def odd_in_params(od_w_in, w_uq, w_ukv):
    od = jnp.concatenate([od_w_in[:, 0:416], jnp.zeros((D, 96), od_w_in.dtype), od_w_in[:, 416:OD_IN]], axis=1)
    ukv = w_ukv.reshape(128, 8, 128)
    wkk = jnp.pad(ukv[:, :, :64], ((0, 0), (0, 0), (0, 32))).reshape(128, 768)
    return dict(w_in=od, wq=w_uq, wkk=wkk, wkv=ukv[:, :, 64:].reshape(128, 512))


def small_params(small):
    p0 = dict(norm1_g=_row(small["norm1_g"][0]), norm2_g=_row(small["norm2_g"][0]),
              gq=jnp.tile(_row(small["ev_q_norm_g"]), (1, 8)), gk=jnp.tile(_row(small["ev_k_norm_g"]), (1, 2)),
              gs=_row(small["ev_sgu_norm_g"]), sgu_w=small["ev_sgu_w"].reshape(8, 128, 128).astype(F32),
              sgu_b=small["ev_sgu_b"].reshape(8, 128, 1).astype(F32))
    p1 = dict(norm1_g=_row(small["norm1_g"][1]), norm2_g=_row(small["norm2_g"][1]),
              gq=_row(small["od_q_norm_g"]), gkv=_row(small["od_kv_norm_g"]),
              conv_w=jnp.pad(small["od_conv_w"].reshape(CONV_K, 512).astype(F32), ((0, 1), (0, 0))),
              conv_b=_row(small["od_conv_b"]), ln_g=_row(small["od_ln_g"]), ln_b=_row(small["od_ln_b"]))
    return [p0, p1]


def small_grads_natural(grads, dfg):
    g0, g1 = grads
    return dict(
        norm1_g=jnp.concatenate([g0["norm1_g"], g1["norm1_g"]], axis=0),
        norm2_g=jnp.concatenate([g0["norm2_g"], g1["norm2_g"]], axis=0),
        ev_q_norm_g=g0["gq"].reshape(8, 64).sum(0).reshape(1, 64),
        ev_k_norm_g=g0["gk"].reshape(2, 64).sum(0).reshape(1, 64),
        ev_sgu_norm_g=g0["gs"].reshape(1, 8, 64),
        ev_sgu_w=g0["sgu_w"].reshape(1, 8, 128, 128),
        ev_sgu_b=g0["sgu_b"].reshape(1, 8, 128),
        od_q_norm_g=g1["gq"].reshape(1, 256),
        od_kv_norm_g=g1["gkv"].reshape(1, 128),
        od_conv_w=g1["conv_w"][0:CONV_K].reshape(1, CONV_K, 512),
        od_conv_b=g1["conv_b"].reshape(1, 512),
        od_ln_g=g1["ln_g"].reshape(1, 512),
        od_ln_b=g1["ln_b"].reshape(1, 512),
        final_g=dfg.reshape(D))


def layer_grads_hs(i, g, part="all"):
    def cols(a):
        k, n = a.shape
        return a.reshape(2, k // 2, 4, n // 4).transpose(0, 2, 1, 3).astype(BF)

    mlp = [(("mlp_w1", i), g["w1"]), (("mlp_w2", i), g["w2"])]
    if part == "mlp":
        return mlp
    rest = [(("w_out", i), g["w_out"])]
    if i == 0:
        rest.append((("ev_w_in", 0), cols(g["w_in"].reshape(D, EV_IN))))
    else:
        od = g["w_in"].reshape(D, OD_PAD)
        od = jnp.concatenate([od[:, 0:416], od[:, 512:OD_PAD]], axis=1)
        ukv = jnp.concatenate([g["wkk"].reshape(128, 8, 96)[:, :, :64], g["wkv"].reshape(128, 8, 64)], axis=2)
        rest += [(("od_w_in", 0), cols(od)), (("od_w_uq", 0), cols(g["wq"])),
                 (("od_w_ukv", 0), cols(ukv.reshape(128, 1024)))]
    return rest if part == "rest" else mlp + rest


WEIGHT_NAMES = ['c_ctx', 'ada_w', 'ada_b', 'norm1_g', 'norm2_g', 'w_out', 'mlp_w1', 'mlp_w2', 'ev_w_in',
                'ev_q_norm_g', 'ev_k_norm_g', 'ev_sgu_norm_g', 'ev_sgu_w', 'ev_sgu_b', 'od_w_in', 'od_q_norm_g',
                'od_kv_norm_g', 'od_w_uq', 'od_w_ukv', 'od_conv_w', 'od_conv_b', 'od_ln_g', 'od_ln_b', 'final_g']
REPL_SMALL = ['norm1_g', 'norm2_g', 'ev_q_norm_g', 'ev_k_norm_g', 'ev_sgu_norm_g', 'ev_sgu_w', 'ev_sgu_b',
              'od_kv_norm_g', 'final_g']
SHARD_SMALL = ['od_q_norm_g', 'od_conv_w', 'od_conv_b', 'od_ln_g', 'od_ln_b']
BIG = ['w_out', 'mlp_w1', 'mlp_w2', 'ev_w_in', 'od_w_in', 'od_w_uq', 'od_w_ukv']


def _gather_last(parts):
    return jnp.concatenate([parts[k] for k in range(4)], axis=-1)


class _Reduce:
    def __init__(self, tag, named, half, where):
        self.tag, self.half, self.where = tag, half, where
        self.names, self.hs = zip(*named)
        self.hs = list(self.hs)

    def to_sibling(self):
        lands = [lax.empty(a.shape[1:], BF) for a in self.hs]
        (self.h1,), token = exchange_start(f"rs_sibling_start_{self.tag}", [(self.hs, lands)], TO_SIBLING)
        return token

    def to_chips(self, after):
        hs, got = exchange_wait(f"rs_sibling_wait_{self.tag}", self.h1, after, TO_SIBLING)
        pair = [add_pairs(f"rs_add_{self.tag}_{k}", a, b, self.half) for k, (a, b) in enumerate(zip(hs, got))]
        lands = [lax.empty(p.shape, BF) for p in pair]
        (self.h2,), token = exchange_start(f"rs_chips_start_{self.tag}", [(pair, lands)], SCATTER)
        return token

    def finish(self, after, bufs):
        pair, land = exchange_wait(f"rs_chips_wait_{self.tag}", self.h2, after, SCATTER)
        for k, ((n, idx), l, p) in enumerate(zip(self.names, land, pair)):
            bufs[n] = sum_slabs(f"rs_sum_{self.tag}_{k}", l, p, self.where, bufs[n], idx)


def kernel(x, c, ctx, c_ctx, ada_w, ada_b, norm1_g, norm2_g, w_out, mlp_w1, mlp_w2, ev_w_in, ev_q_norm_g, ev_k_norm_g, ev_sgu_norm_g, ev_sgu_w, ev_sgu_b, od_w_in, od_q_norm_g, od_kv_norm_g, od_w_uq, od_w_ukv, od_conv_w, od_conv_b, od_ln_g, od_ln_b, final_g, loss_target, m_c_ctx, m_ada_w, m_ada_b, m_norm1_g, m_norm2_g, m_w_out, m_mlp_w1, m_mlp_w2, m_ev_w_in, m_ev_q_norm_g, m_ev_k_norm_g, m_ev_sgu_norm_g, m_ev_sgu_w, m_ev_sgu_b, m_od_w_in, m_od_q_norm_g, m_od_kv_norm_g, m_od_w_uq, m_od_w_ukv, m_od_conv_w, m_od_conv_b, m_od_ln_g, m_od_ln_b, m_final_g, v_c_ctx, v_ada_w, v_ada_b, v_norm1_g, v_norm2_g, v_w_out, v_mlp_w1, v_mlp_w2, v_ev_w_in, v_ev_q_norm_g, v_ev_k_norm_g, v_ev_sgu_norm_g, v_ev_sgu_w, v_ev_sgu_b, v_od_w_in, v_od_q_norm_g, v_od_kv_norm_g, v_od_w_uq, v_od_w_ukv, v_od_conv_w, v_od_conv_b, v_od_ln_g, v_od_ln_b, v_final_g):
    w = dict(c_ctx=c_ctx, ada_w=ada_w, ada_b=ada_b, norm1_g=norm1_g, norm2_g=norm2_g, w_out=w_out, mlp_w1=mlp_w1,
             mlp_w2=mlp_w2, ev_w_in=ev_w_in, ev_q_norm_g=ev_q_norm_g, ev_k_norm_g=ev_k_norm_g,
             ev_sgu_norm_g=ev_sgu_norm_g, ev_sgu_w=ev_sgu_w, ev_sgu_b=ev_sgu_b, od_w_in=od_w_in,
             od_q_norm_g=od_q_norm_g, od_kv_norm_g=od_kv_norm_g, od_w_uq=od_w_uq, od_w_ukv=od_w_ukv,
             od_conv_w=od_conv_w, od_conv_b=od_conv_b, od_ln_g=od_ln_g, od_ln_b=od_ln_b, final_g=final_g)
    mom = dict(c_ctx=m_c_ctx, ada_w=m_ada_w, ada_b=m_ada_b, norm1_g=m_norm1_g, norm2_g=m_norm2_g, w_out=m_w_out,
               mlp_w1=m_mlp_w1, mlp_w2=m_mlp_w2, ev_w_in=m_ev_w_in, ev_q_norm_g=m_ev_q_norm_g,
               ev_k_norm_g=m_ev_k_norm_g, ev_sgu_norm_g=m_ev_sgu_norm_g, ev_sgu_w=m_ev_sgu_w, ev_sgu_b=m_ev_sgu_b,
               od_w_in=m_od_w_in, od_q_norm_g=m_od_q_norm_g, od_kv_norm_g=m_od_kv_norm_g, od_w_uq=m_od_w_uq,
               od_w_ukv=m_od_w_ukv, od_conv_w=m_od_conv_w, od_conv_b=m_od_conv_b, od_ln_g=m_od_ln_g,
               od_ln_b=m_od_ln_b, final_g=m_final_g)
    var = dict(c_ctx=v_c_ctx, ada_w=v_ada_w, ada_b=v_ada_b, norm1_g=v_norm1_g, norm2_g=v_norm2_g, w_out=v_w_out,
               mlp_w1=v_mlp_w1, mlp_w2=v_mlp_w2, ev_w_in=v_ev_w_in, ev_q_norm_g=v_ev_q_norm_g,
               ev_k_norm_g=v_ev_k_norm_g, ev_sgu_norm_g=v_ev_sgu_norm_g, ev_sgu_w=v_ev_sgu_w, ev_sgu_b=v_ev_sgu_b,
               od_w_in=v_od_w_in, od_q_norm_g=v_od_q_norm_g, od_kv_norm_g=v_od_kv_norm_g, od_w_uq=v_od_w_uq,
               od_w_ukv=v_od_w_ukv, od_conv_w=v_od_conv_w, od_conv_b=v_od_conv_b, od_ln_g=v_od_ln_g,
               od_ln_b=v_od_ln_b, final_g=v_final_g)
    xi, yi, ci = lax.axis_index("x"), lax.axis_index("y"), lax.axis_index("c")
    chip = 2 * xi + yi
    dev = 2 * chip + ci

    shard_shapes = [w[n].shape for n in SHARD_SMALL]
    g0 = all_gather8("ag_small", _pack([c] + [w[n] for n in SHARD_SMALL]))
    g0 = g0.reshape(8, -1, D)
    parts = _unpack(g0, [c.shape] + shard_shapes)
    c_all = parts[0].reshape(16, D)
    small_full = {n: _gather_last(p[0::2]) for n, p in zip(SHARD_SMALL, parts[1:])}
    call = jnp.concatenate([c_all, c_ctx.reshape(1, D), jnp.zeros((NC - 17, D), F32)], axis=0)

    cols = ada_w.shape[2]
    ada_b_sh = lax.dynamic_slice(ada_b, (0, chip * cols), (2, cols)).reshape(2, 1, cols)
    mt = mods_fwd(call, ada_w, ada_b_sh)
    mt = all_gather8("ag_mods", mt.reshape(2 * NC, cols)).reshape(8, 2, NC, cols)
    table = mt[0::2].transpose(1, 2, 0, 3).reshape(2, NC, 4 * cols)
    mods = []
    for i in range(2):
        lat = lax.dynamic_slice(table[i], (2 * dev, 0), (2, 4 * cols))
        mc = table[i, 16]
        mods.append(jnp.stack([mc, lat[0], mc, lat[1]]).reshape(4 * N_MOD, 1, D))

    order = [[("ev_w_in", 0)], [("w_out", 0), ("mlp_w1", 0), ("mlp_w2", 0)],
             [("od_w_in", 0), ("od_w_uq", 0), ("od_w_ukv", 0), ("w_out", 1)], [("mlp_w1", 1), ("mlp_w2", 1)]]
    groups = []
    for names in order:
        srcs = [w[n][i].astype(BF) for n, i in names]
        groups.append((srcs, [lax.empty((4,) + s.shape, BF) for s in srcs]))
    groups[0][0][0], table = lax.optimization_barrier((groups[0][0][0], table))
    handles, token = exchange_start("gather_start", groups, GATHER)
    mods[0] = mods[0] + token[0, 0]
    small = {n: w[n] for n in REPL_SMALL}
    small.update(small_full)
    prms = small_params(small)

    chip1 = chip.reshape(1).astype(jnp.int32)

    def arrived(k, after):
        srcs, lands = exchange_wait(f"gather_wait_{k}", handles[k], after, GATHER)
        return [place_own(f"gather_own_{k}_{a}", l, s, chip1) for a, (l, s) in enumerate(zip(lands, srcs))]

    def arrive_ev_in(after):
        (ev,) = arrived(0, after)
        prms[0]["w_in"] = _gather_last(ev)

    def arrive_ev_rest(after):
        wo, w1, w2 = arrived(1, after)
        prms[0].update(w_out=wo.reshape(D, D), w1=w1, w2=w2)

    def arrive_od(after):
        od, uq, ukv, wo = arrived(2, after)
        prms[1].update(odd_in_params(_gather_last(od), _gather_last(uq), _gather_last(ukv)), w_out=wo.reshape(D, D))

    def arrive_od_mlp(after):
        w1, w2 = arrived(3, after)
        prms[1].update(w1=w1, w2=w2)

    prms[0]["w_in"] = arrive_ev_in
    prms[0]["w_out"] = arrive_ev_rest
    prms[1]["w_in"] = arrive_od
    prms[1]["w1"] = arrive_od_mlp

    half = ci.reshape(1).astype(jnp.int32)
    where = jnp.stack([chip, ci]).astype(jnp.int32)
    red = {}

    def hook(point, g, fresh):
        if point == "1:end":
            red["l1"] = _Reduce("l1", layer_grads_hs(1, g, "all"), half, where)
            return red["l1"].to_sibling()
        if point == "0:mlp":
            red["l0_mlp"] = _Reduce("l0_mlp", layer_grads_hs(0, g, "mlp"), half, where)
            return red["l1"].to_chips(fresh) + red["l0_mlp"].to_sibling()
        if point == "0:mid":
            return red["l0_mlp"].to_chips(fresh)
        if point == "0:end":
            red["l0_rest"] = _Reduce("l0_rest", layer_grads_hs(0, g, "rest"), half, where)
            return red["l0_rest"].to_sibling()
        return None

    xin = (ctx.reshape(NEX * LC, D), x.reshape(NEX * L, D))
    loss_p, dx, dmods, grads, dfg, last = local_step(xin, loss_target.reshape(NEX * L, D), mods, prms,
                                                     final_g.reshape(1, D), hook)
    grad_x = dx.reshape(NEX, L, D)

    sg = small_grads_natural(grads, dfg)
    dm = jnp.stack([d.reshape(4, N_MOD * D) for d in dmods])
    small_names = REPL_SMALL + SHARD_SMALL
    items = [dm[:, 1::2], dm[:, 0] + dm[:, 2]] + [sg[n] for n in small_names] + [loss_p[0:1, 0:1]]
    shapes = [a.shape for a in items]
    g1 = all_gather8("ag_grads", _pack(items), after=last)
    started = red["l0_rest"].to_chips(g1)
    rows1 = g1.shape[0] // 8
    g1 = g1.reshape(8, rows1, D)
    tot = _unpack(sum_lead("sum_small", g1, after=started), shapes)
    dm_lat = _unpack(g1, shapes[:1])[0]
    dm_lat = dm_lat.transpose(1, 0, 2, 3).reshape(2, 16, N_MOD * D)
    dm_all = jnp.concatenate([dm_lat, tot[1][:, None], jnp.zeros((2, NC - 17, N_MOD * D), F32)], axis=1)
    gsum = dict(zip(small_names, tot[2:2 + len(small_names)]))
    loss = tot[-1].reshape(())
    grad = {n: gsum[n].reshape(w[n].shape) for n in REPL_SMALL}
    for n in SHARD_SMALL:
        k = w[n].shape[-1]
        grad[n] = lax.dynamic_slice_in_dim(gsum[n], chip * k, k, axis=gsum[n].ndim - 1)
    grad["ada_b"] = sum_lead("sum_ada_b", dm_all.transpose(1, 0, 2).reshape(NC, 2 * N_MOD, D)).reshape(2, N_MOD * D)

    dm_sh = lax.dynamic_slice(dm_all, (0, 0, chip * cols), (2, NC, cols))
    grad["ada_w"], dcc = ada_bwd(call, ada_w, dm_sh)
    dcc = all_gather8("ag_cctx", dcc).reshape(8, 8, D)
    grad["c_ctx"] = sum_lead("sum_cctx", dcc[0::2])[0]

    delta, new_m, new_v = {}, {}, {}

    def adam_big(n, again):
        shp = w[n].shape
        two_d = (shp[0] * shp[1], shp[2])
        res = adamw(f"adamw_{n}", w[n].reshape(two_d), grad[n].reshape(two_d), mom[n].reshape(two_d),
                    var[n].reshape(two_d), again)
        delta[n], new_m[n], new_v[n] = [a.reshape(shp) for a in res[:3]]
        if again:
            grad[n] = res[3].reshape(shp)

    adam_big('ada_w', False)
    rest = [n for n in WEIGHT_NAMES if n not in ['ada_w'] + BIG]
    flat2 = lambda a: a.reshape(-1, a.shape[-1])
    outs = adamw_many("adamw_small", [flat2(w[n]) for n in rest], [flat2(grad[n]) for n in rest],
                      [flat2(mom[n]) for n in rest], [flat2(var[n]) for n in rest])
    for dst, arrs in zip((delta, new_m, new_v), outs):
        dst.update({n: a.reshape(w[n].shape) for n, a in zip(rest, arrs)})
    d_ = outs[0][1]

    bufs = {n: lax.empty((w[n].shape[0], 2, w[n].shape[1] // 2, w[n].shape[2]), F32) for n in BIG}
    for tag, behind in (("l1", delta["ada_w"]), ("l0_mlp", d_), ("l0_rest", d_)):
        red[tag].finish(behind, bufs)
    for n, full in zip(BIG, sibling_merge("rs_sibling_merge", [bufs[n] for n in BIG])):
        grad[n] = full.reshape(w[n].shape)
    for n in BIG:
        adam_big(n, True)

    return (loss, grad_x, *[grad[n] for n in WEIGHT_NAMES], *[delta[n] for n in WEIGHT_NAMES],
            *[new_m[n] for n in WEIGHT_NAMES], *[new_v[n] for n in WEIGHT_NAMES])
```

```python
import functools

import numpy as np
import jax
import jax.numpy as jnp
from jax import lax
from jax.experimental import pallas as pl
from jax.experimental.pallas import tpu as pltpu

F32 = jnp.float32
BF = jnp.bfloat16
HI = lax.Precision.HIGHEST
MESH = pl.DeviceIdType.MESH

D = 1024
L = 2048
LC = 256
SEQ = L + LC
NEX = 2
R = NEX * SEQ
TB = 256
WIDE = 512
BPE = SEQ // TB
NBLK = R // TB
GRID_W = 64
FF = 4 * D
EPS = 1e-6
ROPE_THETA = 10000.0
N_MOD = 6
EV_IN = 1792
OD_IN = 1440
OD_PAD = 1536
VMEM_LIMIT = 60 * 1024 * 1024

ADAM_LR = 0.001
ADAM_B1 = 0.9
ADAM_B2 = 0.999
ADAM_EPS = 1e-08
ADAM_WD = 0.01
ADAM_STEP = 10

NT = (((1,), (1,)), ((), ()))
TN = (((0,), (0,)), ((), ()))


def _cparams(sem=None):
    return pltpu.CompilerParams(dimension_semantics=sem, vmem_limit_bytes=VMEM_LIMIT)


@jax.custom_vjp
def _mm(a, b):
    return jnp.dot(a.astype(BF), b.astype(BF), preferred_element_type=F32)


def _mm_fwd(a, b):
    return _mm(a, b), (a, b)


def _mm_bwd(res, g):
    a, b = res
    gb = g.astype(BF)
    da = lax.dot_general(gb, b.astype(BF), NT, preferred_element_type=F32)
    db = lax.dot_general(a.astype(BF), gb, TN, preferred_element_type=F32)
    return da, db


_mm.defvjp(_mm_fwd, _mm_bwd)


@jax.custom_vjp
def _swap(x):
    n = x.shape[-1]
    ax = x.ndim - 1
    lane = lax.broadcasted_iota(jnp.int32, x.shape, ax)
    return jnp.where(lane % 2 == 0, pltpu.roll(x, n - 1, ax), pltpu.roll(x, 1, ax))


_swap.defvjp(lambda x: (_swap(x), None), lambda _, g: (_swap(g),))


def _rope(x, cos, sin):
    return x * cos + _swap(x) * sin


def _rmsn(x, g):
    return x * lax.rsqrt(jnp.mean(x * x, axis=-1, keepdims=True) + EPS) * g


def _split_dot(a, m):
    hi = a.astype(BF)
    lo = (a - hi.astype(F32)).astype(BF)
    return jnp.dot(hi, m, preferred_element_type=F32) + jnp.dot(lo, m, preferred_element_type=F32)


@jax.custom_vjp
def _group_mean(a, avg):
    return _split_dot(a, avg)


_group_mean.defvjp(lambda a, avg: (_split_dot(a, avg), avg),
                   lambda avg, g: (_split_dot(g, avg), jnp.zeros_like(avg)))


def _grmsn(x, g, avg):
    return x * lax.rsqrt(_group_mean(x * x, avg) + EPS) * g


def _modnorm(x, g, sh, sc):
    return _rmsn(x, g) * (1.0 + sc) + sh


def _gelu(x):
    return 0.5 * x * (1.0 + jnp.tanh(0.7978845608028654 * (x + 0.044715 * (x * x * x))))


def _silu(x):
    return x * jax.nn.sigmoid(x)


def _acc(ref, val, first):
    @pl.when(first)
    def _():
        ref[...] = val

    @pl.when(jnp.logical_not(first))
    def _():
        ref[...] += val


def _seg(i):
    return 2 * (i // BPE) + jnp.minimum(i % BPE, 1)


def _seg_first(i):
    return (i % BPE) <= 1


class _Either:
    def __init__(self, pick_first, first, second):
        self.pick_first, self.first, self.second = pick_first, first, second

    def __getitem__(self, idx):
        return jnp.where(self.pick_first, self.first[idx], self.second[idx])


def _rb_call(name, body, row_in=(), mod_in=(), pos_in=(), full_in=(), shift_in=(),
             row_out=(), seg_out=(), acc_out=(), scratch=(), after=None, col_in=(), rows=TB, idle=None):
    assert rows == TB or not (mod_in or pos_in or shift_in or seg_out or col_in)
    in_specs, args, pairs = [], [], []
    for a in row_in:
        if isinstance(a, tuple):
            pairs.append(len(args))
            in_specs.append(pl.BlockSpec((TB, a[0].shape[1]), lambda i: (i // BPE, 0)))
            in_specs.append(pl.BlockSpec(
                (TB, a[1].shape[1]), lambda i: ((i // BPE) * (L // TB) + jnp.maximum(i % BPE - 1, 0), 0)))
            args += list(a)
        else:
            in_specs.append(pl.BlockSpec((rows, a.shape[1]), lambda i: (i, 0)))
            args.append(a)
    for a in col_in:
        in_specs.append(pl.BlockSpec((a.shape[0], TB), lambda i: (0, i)))
        args.append(a)
    for tab, m in mod_in:
        in_specs.append(pl.BlockSpec((1, 1, D), lambda i, m=m: (_seg(i) * N_MOD + m, 0, 0)))
        args.append(tab)
    for a in pos_in:
        in_specs.append(pl.BlockSpec((TB, a.shape[1]), lambda i: (i % BPE, 0)))
        args.append(a)
    for a in full_in:
        in_specs.append(pl.BlockSpec(a.shape, lambda i, n=a.ndim: (0,) * n))
        args.append(a)
    for a, d in shift_in:
        in_specs.append(pl.BlockSpec((TB, a.shape[1]), lambda i, d=d: (jnp.clip(i + d, 0, NBLK - 1), 0)))
        args.append(a)
    n_in = len(args)
    if after is not None:
        in_specs.append(pl.BlockSpec(after.shape, lambda i, n=after.ndim: (0,) * n))
        args.append(after)
    out_specs, out_shape = [], []
    for w, dt, *lat in row_out:
        if lat:
            out_specs.append(pl.BlockSpec(
                (TB, w), lambda i: ((i // BPE) * (L // TB) + jnp.maximum(i % BPE - 1, 0), 0)))
            out_shape.append(jax.ShapeDtypeStruct((NEX * L, w), dt))
        else:
            out_specs.append(pl.BlockSpec((rows, w), lambda i: (i, 0)))
            out_shape.append(jax.ShapeDtypeStruct((R, w), dt))
    for w in seg_out:
        out_specs.append(pl.BlockSpec((1, 1, w), lambda i: (_seg(i), 0, 0)))
        out_shape.append(jax.ShapeDtypeStruct((4, 1, w), F32))
    for shp in acc_out:
        out_specs.append(pl.BlockSpec(shp, lambda i, n=len(shp): (0,) * n))
        out_shape.append(jax.ShapeDtypeStruct(shp, F32))

    def kern(*refs):
        i = pl.program_id(0)
        ins = list(refs[:n_in])
        for k in reversed(pairs):
            ins[k:k + 2] = [_Either((i % BPE) == 0, ins[k], ins[k + 1])]
        if idle is None:
            body(i, *ins, *refs[len(args):])
        else:
            @pl.when((i % BPE) >= 1)
            def _():
                body(i, *ins, *refs[len(args):])

            @pl.when((i % BPE) == 0)
            def _():
                idle(i, *ins, *refs[len(args):])

    sem = ("arbitrary",) if (seg_out or acc_out or any(len(r) > 2 for r in row_out)) else ("parallel",)
    return pl.pallas_call(kern, grid=(R // rows,), in_specs=in_specs, out_specs=out_specs, out_shape=out_shape,
                          scratch_shapes=list(scratch), compiler_params=_cparams(sem), name=name)(*args)


def modnorm_fwd(name, x, mods, g, m_sh, m_sc):
    def body(i, x_ref, sh_ref, sc_ref, g_ref, h_ref):
        h_ref[...] = _modnorm(x_ref[...], g_ref[...], sh_ref[0], sc_ref[0]).astype(BF)

    return _rb_call(name, body, row_in=(x,), mod_in=((mods, m_sh), (mods, m_sc)), full_in=(g,),
                    row_out=((D, BF),))[0]


def _gate_grads(dx, y_ref, gt_ref, dy_ref, dgt_ref, i):
    dy_ref[...] = (dx * gt_ref[0]).astype(BF)
    _acc(dgt_ref, jnp.sum(dx * y_ref[...].astype(F32), axis=0, keepdims=True)[None], _seg_first(i))


def modnorm_bwd(name, x, dh, dx_in, mods, g, m_sh, m_sc, gate=None, after=None, lat_only=False, last=False):
    def body(i, x_ref, dh_ref, dxin_ref, *rest):
        if gate:
            y_ref, sh_ref, sc_ref, gt_ref, g_ref, dx_ref, dy_ref, dgt_ref, dsh_ref, dsc_ref, dg_ref = rest
        else:
            sh_ref, sc_ref, g_ref, dx_ref, dsh_ref, dsc_ref, dg_ref = rest
        _, vjp = jax.vjp(_modnorm, x_ref[...], g_ref[...], sh_ref[0], sc_ref[0])
        dx, dg, dsh, dsc = vjp(dh_ref[...].astype(F32))
        dx = dxin_ref[...] + dx
        dx_ref[...] = dx
        if gate:
            _gate_grads(dx, y_ref, gt_ref, dy_ref, dgt_ref, i)
        _acc(dsh_ref, dsh[None], _seg_first(i))
        _acc(dsc_ref, dsc[None], _seg_first(i))
        _acc(dg_ref, dg, i == 0)

    def idle(i, x_ref, dh_ref, dxin_ref, y_ref, sh_ref, sc_ref, gt_ref, g_ref, dx_ref, dy_ref, dgt_ref, dsh_ref,
             dsc_ref, dg_ref):
        dx_ref[...] = dxin_ref[...]
        _zero(dy_ref, dgt_ref, dsh_ref, dsc_ref)
        _zero_at_start(i, dg_ref)

    if gate:
        y, gmods, m = gate
        return _rb_call(name, body, row_in=(x, dh, dx_in, y), mod_in=((mods, m_sh), (mods, m_sc), (gmods, m)),
                        full_in=(g,), row_out=((D, F32), (D, BF)), seg_out=(D, D, D), acc_out=((1, D),), after=after,
                        idle=idle if last else None)
    return _rb_call(name, body, row_in=(x, dh, dx_in), mod_in=((mods, m_sh), (mods, m_sc)), full_in=(g,),
                    row_out=((D, F32, "lat") if lat_only else (D, F32),), seg_out=(D, D), acc_out=((1, D),),
                    after=after)


def proj_in(name, h, w):
    n = w.shape[1]

    def body(i, h_ref, w_ref, o_ref):
        o_ref[...] = jnp.dot(h_ref[...], w_ref[...], preferred_element_type=F32).astype(BF)

    return _rb_call(name, body, row_in=(h,), full_in=(w,), row_out=((n, BF),), rows=WIDE)[0]


def _zero(*refs):
    for r in refs:
        r[...] = jnp.zeros_like(r)


def _zero_at_start(i, *refs):
    @pl.when(i == 0)
    def _():
        _zero(*refs)


def proj_out(name, a1, a2, w, x, mods, m_gate, g_next, m_sh, m_sc, last=False):
    k1 = a1.shape[1]

    def idle(i, a1_ref, a2_ref, x_ref, gt_ref, sh_ref, sc_ref, w_ref, g_ref, xo_ref, y_ref, h_ref):
        xo_ref[...] = x_ref[...]
        _zero(y_ref, h_ref)

    def body(i, a1_ref, a2_ref, x_ref, gt_ref, sh_ref, sc_ref, w_ref, g_ref, xo_ref, y_ref, h_ref):
        y = jnp.dot(a1_ref[...], w_ref[:k1, :], preferred_element_type=F32)
        y = y + jnp.dot(a2_ref[...], w_ref[k1:, :], preferred_element_type=F32)
        y_ref[...] = y.astype(BF)
        xn = x_ref[...] + gt_ref[0] * y
        xo_ref[...] = xn
        h_ref[...] = _modnorm(xn, g_ref[...], sh_ref[0], sc_ref[0]).astype(BF)

    return _rb_call(name, body, row_in=(a1, a2, x), mod_in=((mods, m_gate), (mods, m_sh), (mods, m_sc)),
                    full_in=(w, g_next), row_out=((D, F32), (D, BF), (D, BF)), idle=idle if last else None)


def mlp_up(name, h, w1):
    def body(i, h_ref, w_ref, f_ref):
        hv = h_ref[...]
        for n in range(4):
            r = jnp.maximum(jnp.dot(hv, w_ref[n], preferred_element_type=F32), 0.0)
            f_ref[:, n * D:(n + 1) * D] = (r * r).astype(BF)

    return _rb_call(name, body, row_in=(h,), full_in=(w1,), row_out=((FF, BF),), rows=WIDE)[0]


def mlp_down(name, f, w2, x, mods, m_gate, nxt=None):
    def body(i, f_ref, x_ref, gt_ref, *rest):
        if nxt:
            sh_ref, sc_ref, w_ref, g_ref, xo_ref, y_ref, h_ref = rest
        else:
            w_ref, xo_ref, y_ref = rest
        y = jnp.dot(f_ref[:, 0:D], w_ref[0], preferred_element_type=F32)
        for n in range(1, 4):
            y = y + jnp.dot(f_ref[:, n * D:(n + 1) * D], w_ref[n], preferred_element_type=F32)
        xn = x_ref[...] + gt_ref[0] * y
        y_ref[...] = y.astype(BF)
        xo_ref[...] = xn
        if nxt:
            h_ref[...] = _modnorm(xn, g_ref[...], sh_ref[0], sc_ref[0]).astype(BF)

    if nxt:
        return _rb_call(name, body, row_in=(f, x), mod_in=((mods, m_gate), (nxt[0], 0), (nxt[0], 1)),
                        full_in=(w2, nxt[1]), row_out=((D, F32), (D, BF), (D, BF)))
    def idle(i, f_ref, x_ref, gt_ref, w_ref, xo_ref, y_ref):
        xo_ref[...] = x_ref[...]
        _zero(y_ref)

    return _rb_call(name, body, row_in=(f, x), mod_in=((mods, m_gate),), full_in=(w2,),
                    row_out=((D, F32), (D, BF)), idle=idle)


def mm_nt(name, g, w, after=None):
    k = w.shape[0]

    def body(i, g_ref, w_ref, o_ref):
        o_ref[...] = lax.dot_general(g_ref[...], w_ref[...], NT, preferred_element_type=F32).astype(BF)

    return _rb_call(name, body, row_in=(g,), full_in=(w,), row_out=((k, BF),), after=after, rows=WIDE)[0]


def mlp_bwd_da(name, dy, w2, f, after=None):
    def body(i, dy_ref, f_ref, w_ref, da_ref):
        dyv = dy_ref[...]
        for n in range(4):
            df = lax.dot_general(dyv, w_ref[n], NT, preferred_element_type=F32)
            r = jnp.sqrt(f_ref[:, n * D:(n + 1) * D].astype(F32))
            da_ref[:, n * D:(n + 1) * D] = (df * (2.0 * r)).astype(BF)

    return _rb_call(name, body, row_in=(dy, f), full_in=(w2,), row_out=((FF, BF),), after=after, rows=WIDE)[0]


def mlp_bwd_dh(name, da, w1):
    def body(i, da_ref, w_ref, dh_ref):
        acc = lax.dot_general(da_ref[:, 0:D], w_ref[0], NT, preferred_element_type=F32)
        for n in range(1, 4):
            acc = acc + lax.dot_general(da_ref[:, n * D:(n + 1) * D], w_ref[n], NT, preferred_element_type=F32)
        dh_ref[...] = acc.astype(BF)

    return _rb_call(name, body, row_in=(da,), full_in=(w1,), row_out=((D, BF),), rows=WIDE)[0]


TN_ROWS = 2304


def mm_tn(name, a, g, tiles, th, tw):
    nt = len(tiles)
    acs = jnp.asarray([t[0] for t in tiles], jnp.int32)
    gcs = jnp.asarray([t[1] for t in tiles], jnp.int32)
    nr = R // TN_ROWS

    def kern(ac_ref, gc_ref, a_ref, g_ref, o_ref, acc_ref):
        r = pl.program_id(1)

        @pl.when(r == 0)
        def _():
            acc_ref[...] = jnp.zeros_like(acc_ref)

        acc_ref[...] += lax.dot_general(a_ref[...], g_ref[...], TN, preferred_element_type=F32)

        @pl.when(r == nr - 1)
        def _():
            o_ref[...] = acc_ref[...].astype(BF)

    grid_spec = pltpu.PrefetchScalarGridSpec(
        num_scalar_prefetch=2, grid=(nt, nr),
        in_specs=[pl.BlockSpec((TN_ROWS, th), lambda t, r, ac, gc: (r, ac[t])),
                  pl.BlockSpec((TN_ROWS, tw), lambda t, r, ac, gc: (r, gc[t]))],
        out_specs=pl.BlockSpec((None, th, tw), lambda t, r, ac, gc: (t, 0, 0)),
        scratch_shapes=[pltpu.VMEM((th, tw), F32)])
    return pl.pallas_call(kern, grid_spec=grid_spec, out_shape=jax.ShapeDtypeStruct((nt, th, tw), BF),
                          compiler_params=_cparams(("parallel", "arbitrary")), name=name)(acs, gcs, a, g)


def _even_tok(q, k, zus, zvs, gq, gk, gss, ws, bs, cq, sq, ck, sk, avg, lo, hi):
    avg2 = avg[:128, :128]
    qr = _rope(_grmsn(q, gq, avg), cq, sq) * GQA_SCALE
    kr = _rope(_grmsn(k, gk, avg2), ck, sk)
    ms = []
    for b in range(4):
        v = _grmsn(_gelu(zvs[b]), gss[b], avg2)
        sv = lo * (_mm(ws[2 * b], v) + bs[2 * b]) + hi * (_mm(ws[2 * b + 1], v) + bs[2 * b + 1])
        ms.append(_gelu(zus[b]) * sv)
    return qr, kr, ms


def _even_operands(p_ref, rs, gq_ref, gk_ref, gs_ref, w_ref, b_ref):
    return (p_ref[rs, 0:512].astype(F32), p_ref[rs, 512:640].astype(F32),
            [p_ref[rs, 768 + 128 * b:896 + 128 * b].astype(F32) for b in range(4)],
            [p_ref[rs, 1280 + 128 * b:1408 + 128 * b].astype(F32) for b in range(4)],
            gq_ref[...], gk_ref[...], [gs_ref[:, 128 * b:128 * b + 128] for b in range(4)],
            [w_ref[g] for g in range(8)], [b_ref[g] for g in range(8)])


def even_tok_fwd(p, cos, sin, gq, gk, gs, sgu_w, sgu_b, avg, masks):
    def body(i, p_ref, cos_ref, sin_ref, gq_ref, gk_ref, gs_ref, w_ref, b_ref, avg_ref, mk_ref, q_ref, kv_ref, m_ref):
        avgv, lo, hi = avg_ref[...], mk_ref[0, :, 0:128], mk_ref[1, :, 0:128]
        for c in range(2):
            rs = pl.ds(c * 128, 128)
            qr, kr, ms = _even_tok(*_even_operands(p_ref, rs, gq_ref, gk_ref, gs_ref, w_ref, b_ref),
                                   cos_ref[rs, :], sin_ref[rs, :], cos_ref[rs, 0:128], sin_ref[rs, 0:128],
                                   avgv, lo, hi)
            q_ref[rs, :] = qr.astype(BF)
            kv_ref[rs, 0:128] = kr.astype(BF)
            kv_ref[rs, 128:256] = p_ref[rs, 640:768]
            for b in range(4):
                m_ref[rs, 128 * b:128 * b + 128] = ms[b].astype(BF)

    return _rb_call("even_tok_fwd", body, row_in=(p,), pos_in=(cos, sin),
                    full_in=(gq, gk, gs, sgu_w, sgu_b, avg, masks), row_out=((512, BF), (256, BF), (512, BF)))


def even_tok_bwd(p, dq, dkvt, dcat, cos, sin, gq, gk, gs, sgu_w, sgu_b, avg, masks):
    def body(i, p_ref, dq_ref, dcat_ref, dkvt_ref, cos_ref, sin_ref, gq_ref, gk_ref, gs_ref, w_ref, b_ref,
             avg_ref, mk_ref, dp_ref, dgq_ref, dgk_ref, dgs_ref, dw_ref, db_ref):
        avgv, lo, hi = avg_ref[...], mk_ref[0, :, 0:128], mk_ref[1, :, 0:128]
        tot = None
        for c in range(2):
            rs = pl.ds(c * 128, 128)
            cq, sq, ck, sk = cos_ref[rs, :], sin_ref[rs, :], cos_ref[rs, 0:128], sin_ref[rs, 0:128]

            def f(q, k, zus, zvs, gq, gk, gss, ws, bs):
                return _even_tok(q, k, zus, zvs, gq, gk, gss, ws, bs, cq, sq, ck, sk, avgv, lo, hi)

            _, vjp = jax.vjp(f, *_even_operands(p_ref, rs, gq_ref, gk_ref, gs_ref, w_ref, b_ref))
            dk = dkvt_ref[0:128, c * 128:(c + 1) * 128].T
            dv = dkvt_ref[128:256, c * 128:(c + 1) * 128].T
            dms = [dcat_ref[rs, 512 + 128 * b:640 + 128 * b].astype(F32) for b in range(4)]
            d = vjp((dq_ref[rs, :].astype(F32), dk, dms))
            dp_ref[rs, 0:512] = d[0].astype(BF)
            dp_ref[rs, 512:640] = d[1].astype(BF)
            dp_ref[rs, 640:768] = dv.astype(BF)
            for b in range(4):
                dp_ref[rs, 768 + 128 * b:896 + 128 * b] = d[2][b].astype(BF)
                dp_ref[rs, 1280 + 128 * b:1408 + 128 * b] = d[3][b].astype(BF)
            part = [d[4], d[5]] + list(d[6]) + list(d[7]) + list(d[8])
            tot = part if tot is None else [x + y for x, y in zip(tot, part)]
        refs = ([dgq_ref, dgk_ref] + [dgs_ref.at[:, 128 * b:128 * b + 128] for b in range(4)]
                + [dw_ref.at[g] for g in range(8)] + [db_ref.at[g] for g in range(8)])
        for ref, val in zip(refs, tot):
            _acc(ref, val, i == 0)

    return _rb_call("even_tok_bwd", body, row_in=(p, dq, dcat), col_in=(dkvt,), pos_in=(cos, sin),
                    full_in=(gq, gk, gs, sgu_w, sgu_b, avg, masks), row_out=((EV_IN, BF),),
                    acc_out=((1, 512), (1, 128), (1, 512), (8, 128, 128), (8, 128, 1)))


MLA_SCALE = 96 ** -0.5
GQA_SCALE = 64 ** -0.5


def _odd_tok(cq, ckv, kr, za, zg, gq, gkv, wq, wkk, wkv, spread, cr, sr, ck, sk):
    cqn = _rmsn(cq, gq)
    q = _rope(_mm(cqn, wq), cr, sr) * MLA_SCALE
    ckn = _rmsn(ckv, gkv)
    k = _mm(ckn, wkk) + _mm(_rope(kr, ck, sk), spread)
    v = _mm(ckn, wkv)
    y = za * jax.nn.sigmoid(zg)
    return q, k, v, y


def odd_tok_fwd(p, cos, sin, gq, gkv, wq, wkk, wkv, spread):
    def body(i, p_ref, cos_ref, sin_ref, gq_ref, gkv_ref, wq_ref, wkk_ref, wkv_ref, sp_ref, q_ref, kv_ref, y_ref):
        q, k, v, y = _odd_tok(
            p_ref[:, 0:256].astype(F32), p_ref[:, 256:384].astype(F32), p_ref[:, 384:512].astype(F32),
            p_ref[:, 512:1024].astype(F32), p_ref[:, 1024:1536].astype(F32),
            gq_ref[...], gkv_ref[...], wq_ref[...], wkk_ref[...], wkv_ref[...], sp_ref[...],
            cos_ref[:, 0:768], sin_ref[:, 0:768], cos_ref[:, 768:896], sin_ref[:, 768:896])
        q_ref[...] = q.astype(BF)
        kv_ref[:, 0:768] = k.astype(BF)
        kv_ref[:, 768:1280] = v.astype(BF)
        y_ref[...] = y.astype(BF)

    return _rb_call("odd_tok_fwd", body, row_in=(p,), pos_in=(cos, sin), full_in=(gq, gkv, wq, wkk, wkv, spread),
                    row_out=((768, BF), (1280, BF), (512, BF)))


def odd_tok_bwd(p, dq, dkvt, dy, cos, sin, gq, gkv, wq, wkk, wkv, spread):
    def body(i, p_ref, dq_ref, dy_ref, dkvt_ref, cos_ref, sin_ref, gq_ref, gkv_ref, wq_ref, wkk_ref, wkv_ref, sp_ref,
             dp_ref, dgq_ref, dgkv_ref, dwq_ref, dwkk_ref, dwkv_ref):
        cr, sr, ck, sk = cos_ref[:, 0:768], sin_ref[:, 0:768], cos_ref[:, 768:896], sin_ref[:, 768:896]
        spread_v = sp_ref[...]

        def f(cq, ckv, kr, za, zg, gq, gkv, wq, wkk, wkv):
            return _odd_tok(cq, ckv, kr, za, zg, gq, gkv, wq, wkk, wkv, spread_v, cr, sr, ck, sk)

        _, vjp = jax.vjp(f, p_ref[:, 0:256].astype(F32), p_ref[:, 256:384].astype(F32),
                         p_ref[:, 384:512].astype(F32), p_ref[:, 512:1024].astype(F32),
                         p_ref[:, 1024:1536].astype(F32), gq_ref[...], gkv_ref[...], wq_ref[...],
                         wkk_ref[...], wkv_ref[...])
        d = vjp((dq_ref[...].astype(F32), dkvt_ref[0:768, :].T, dkvt_ref[768:1280, :].T, dy_ref[...].astype(F32)))
        dp_ref[:, 0:256] = d[0].astype(BF)
        dp_ref[:, 256:384] = d[1].astype(BF)
        dp_ref[:, 384:512] = d[2].astype(BF)
        dp_ref[:, 512:1024] = d[3].astype(BF)
        dp_ref[:, 1024:1536] = d[4].astype(BF)
        for ref, val in zip((dgq_ref, dgkv_ref, dwq_ref, dwkk_ref, dwkv_ref), d[5:]):
            _acc(ref, val, i == 0)

    return _rb_call("odd_tok_bwd", body, row_in=(p, dq, dy), col_in=(dkvt,), pos_in=(cos, sin),
                    full_in=(gq, gkv, wq, wkk, wkv, spread), row_out=((OD_PAD, BF),),
                    acc_out=((1, 256), (1, 128), (256, 768), (128, 768), (128, 512)))


GQA_HEADS = [(64 * h, 64 * (h // 4), 64, 128 + 64 * (h // 4)) for h in range(8)]
MLA_HEADS = [(96 * h, 96 * h, 96, 768 + 64 * h) for h in range(8)]


def _by_block(j, run):
    @pl.when(j == 0)
    def _():
        run(LC)

    @pl.when(j > 0)
    def _():
        run(SEQ)


def attn_fwd(name, q, kv, heads):
    qw, kvw = q.shape[1], kv.shape[1]

    def kern(q_ref, kv_ref, o_ref, lse_ref):
        def run(nk):
            for h, (qo, ko, w, vo) in enumerate(heads):
                s = lax.dot_general(q_ref[:, qo:qo + w], kv_ref[0:nk, ko:ko + w], NT, preferred_element_type=F32)
                m = jnp.max(s, axis=-1, keepdims=True)
                p = jnp.exp(s - m)
                l = jnp.sum(p, axis=-1, keepdims=True)
                o = jnp.dot(p.astype(BF), kv_ref[0:nk, vo:vo + 64], preferred_element_type=F32) / l
                o_ref[:, 64 * h:64 * h + 64] = o.astype(BF)
                lse_ref[:, h:h + 1] = m + jnp.log(l)

        _by_block(pl.program_id(1), run)

    return pl.pallas_call(
        kern, grid=(NEX, BPE),
        in_specs=[pl.BlockSpec((TB, qw), lambda e, j: (e * BPE + j, 0)),
                  pl.BlockSpec((SEQ, kvw), lambda e, j: (e, 0))],
        out_specs=[pl.BlockSpec((TB, 512), lambda e, j: (e * BPE + j, 0)),
                   pl.BlockSpec((TB, 8), lambda e, j: (e * BPE + j, 0))],
        out_shape=[jax.ShapeDtypeStruct((R, 512), BF), jax.ShapeDtypeStruct((R, 8), F32)],
        compiler_params=_cparams(("parallel", "arbitrary")), name=name)(q, kv)


def attn_bwd(name, q, kv, o, dcat, lse, heads):
    qw, kvw = q.shape[1], kv.shape[1]

    def kern(q_ref, kv_ref, o_ref, do_ref, lse_ref, dq_ref, dkvt_ref):
        j = pl.program_id(1)

        @pl.when(j == 0)
        def _():
            dkvt_ref[...] = jnp.zeros_like(dkvt_ref)

        def run(nk):
            for h, (qo, ko, w, vo) in enumerate(heads):
                qh = q_ref[:, qo:qo + w]
                kh = kv_ref[0:nk, ko:ko + w]
                s = lax.dot_general(qh, kh, NT, preferred_element_type=F32)
                p = jnp.exp(s - lse_ref[:, h:h + 1])
                do = do_ref[:, 64 * h:64 * h + 64]
                dsum = jnp.sum(do.astype(F32) * o_ref[:, 64 * h:64 * h + 64].astype(F32), axis=-1, keepdims=True)
                dp = lax.dot_general(do, kv_ref[0:nk, vo:vo + 64], NT, preferred_element_type=F32)
                ds = (p * (dp - dsum)).astype(BF)
                dkvt_ref[vo:vo + 64, 0:nk] += lax.dot_general(do, p.astype(BF), TN, preferred_element_type=F32)
                dq_ref[:, qo:qo + w] = jnp.dot(ds, kh, preferred_element_type=F32).astype(BF)
                dkvt_ref[ko:ko + w, 0:nk] += lax.dot_general(qh, ds, TN, preferred_element_type=F32)

        _by_block(j, run)

    return pl.pallas_call(
        kern, grid=(NEX, BPE),
        in_specs=[pl.BlockSpec((TB, qw), lambda e, j: (e * BPE + j, 0)),
                  pl.BlockSpec((SEQ, kvw), lambda e, j: (e, 0)),
                  pl.BlockSpec((TB, 512), lambda e, j: (e * BPE + j, 0)),
                  pl.BlockSpec((TB, 512), lambda e, j: (e * BPE + j, 0)),
                  pl.BlockSpec((TB, 8), lambda e, j: (e * BPE + j, 0))],
        out_specs=[pl.BlockSpec((TB, qw), lambda e, j: (e * BPE + j, 0)),
                   pl.BlockSpec((kvw, SEQ), lambda e, j: (0, e))],
        out_shape=[jax.ShapeDtypeStruct((R, qw), BF), jax.ShapeDtypeStruct((kvw, R), F32)],
        compiler_params=_cparams(("parallel", "arbitrary")), name=name)(q, kv, o, dcat, lse)


HALO = 16
CONV_K = 31


def _fill_ext(ext_ref, prev_ref, cur_ref, next_ref, i):
    j = i % BPE
    has_prev = (j >= 2).astype(F32)
    has_next = jnp.logical_and(j >= 1, j <= BPE - 2).astype(F32)
    ext_ref[0:HALO, :] = prev_ref[TB - HALO:TB, :].astype(F32) * has_prev
    ext_ref[HALO:HALO + TB, :] = cur_ref[...].astype(F32)
    ext_ref[HALO + TB:2 * HALO + TB, :] = next_ref[0:HALO, :].astype(F32) * has_next


PHASE_ROWS = TB + 24


def _phases(ext_ref, ph_ref):
    for r in range(8):
        ph_ref[r] = ext_ref[r:r + PHASE_ROWS, :]


def _window(ph_ref, off):
    return ph_ref[off % 8, 8 * (off // 8):8 * (off // 8) + TB, :]


def _ln_silu(z, g, b):
    mu = jnp.mean(z, axis=-1, keepdims=True)
    zc = z - mu
    var = jnp.mean(zc * zc, axis=-1, keepdims=True)
    return _silu(zc * lax.rsqrt(var + EPS) * g + b)


def conf_fwd(y, cw, cb, lg, lb):
    def body(i, cur_ref, cw_ref, cb_ref, lg_ref, lb_ref, prev_ref, next_ref, z_ref, c_ref, ext_ref, ph_ref):
        _fill_ext(ext_ref, prev_ref, cur_ref, next_ref, i)
        _phases(ext_ref, ph_ref)
        acc = _window(ph_ref, 1) * cw_ref[0:1, :]
        for k in range(1, CONV_K):
            acc = acc + _window(ph_ref, k + 1) * cw_ref[k:k + 1, :]
        z = acc + cb_ref[...]
        z_ref[...] = z.astype(BF)
        c_ref[...] = _ln_silu(z, lg_ref[...], lb_ref[...]).astype(BF)

    def idle(i, cur_ref, cw_ref, cb_ref, lg_ref, lb_ref, prev_ref, next_ref, z_ref, c_ref, ext_ref, ph_ref):
        _zero(z_ref, c_ref)

    return _rb_call("conf_fwd", body, row_in=(y,), full_in=(cw, cb, lg, lb), shift_in=((y, -1), (y, 1)),
                    row_out=((512, BF), (512, BF)), idle=idle,
                    scratch=(pltpu.VMEM((TB + 2 * HALO, 512), F32), pltpu.VMEM((8, PHASE_ROWS, 512), F32)))


def conf_bwd_ln(z, dcat, lg, lb):
    def body(i, z_ref, dcat_ref, lg_ref, lb_ref, dz_ref, dlg_ref, dlb_ref, dcb_ref):
        _, vjp = jax.vjp(_ln_silu, z_ref[...].astype(F32), lg_ref[...], lb_ref[...])
        dz, dlg, dlb = vjp(dcat_ref[:, 512:1024].astype(F32))
        dz_ref[...] = dz.astype(BF)
        _acc(dlg_ref, dlg, i == 0)
        _acc(dlb_ref, dlb, i == 0)
        _acc(dcb_ref, jnp.sum(dz, axis=0, keepdims=True), i == 0)

    def idle(i, z_ref, dcat_ref, lg_ref, lb_ref, dz_ref, dlg_ref, dlb_ref, dcb_ref):
        _zero(dz_ref)
        _zero_at_start(i, dlg_ref, dlb_ref, dcb_ref)

    return _rb_call("conf_bwd_ln", body, row_in=(z, dcat), full_in=(lg, lb), row_out=((512, BF),),
                    acc_out=((1, 512), (1, 512), (1, 512)), idle=idle)


def conf_bwd_conv(y, dz, cw):
    def body(i, y_ref, dz_ref, cw_ref, yp_ref, yn_ref, dzp_ref, dzn_ref, dy_ref, dcw_ref, ext_ref, phy_ref, phd_ref):
        _fill_ext(ext_ref, yp_ref, y_ref, yn_ref, i)
        _phases(ext_ref, phy_ref)
        _fill_ext(ext_ref, dzp_ref, dz_ref, dzn_ref, i)
        _phases(ext_ref, phd_ref)
        dzv = dz_ref[...].astype(F32)

        @pl.when(i == 0)
        def _():
            dcw_ref[...] = jnp.zeros_like(dcw_ref)

        acc = None
        for k in range(CONV_K):
            t = _window(phd_ref, CONV_K - k) * cw_ref[k:k + 1, :]
            acc = t if acc is None else acc + t
            dcw_ref[k:k + 1, :] += jnp.sum(dzv * _window(phy_ref, k + 1), axis=0, keepdims=True)
        dy_ref[...] = acc.astype(BF)

    def idle(i, y_ref, dz_ref, cw_ref, yp_ref, yn_ref, dzp_ref, dzn_ref, dy_ref, dcw_ref, ext_ref, phy_ref, phd_ref):
        _zero(dy_ref)
        _zero_at_start(i, dcw_ref)

    return _rb_call("conf_bwd_conv", body, row_in=(y, dz), full_in=(cw,), idle=idle,
                    shift_in=((y, -1), (y, 1), (dz, -1), (dz, 1)), row_out=((512, BF),), acc_out=((32, 512),),
                    scratch=(pltpu.VMEM((TB + 2 * HALO, 512), F32), pltpu.VMEM((8, PHASE_ROWS, 512), F32),
                             pltpu.VMEM((8, PHASE_ROWS, 512), F32)))


def final_loss(x, target, fg, y, mods, m_gate):
    lpb = L // TB

    def kern(x_ref, t_ref, g_ref, y_ref, gt_ref, dx_ref, dy_ref, dgt_ref, loss_ref, dg_ref):
        i = pl.program_id(0)

        @pl.when((i % BPE) == 0)
        def _():
            _zero(dx_ref, dy_ref, dgt_ref)
            _zero_at_start(i, loss_ref, dg_ref)

        @pl.when((i % BPE) >= 1)
        def _():
            tv = t_ref[...]

            def f(x, g):
                err = _rmsn(x, g) - tv
                rowsum = jnp.sum(err * err, axis=-1, keepdims=True)
                return jnp.sum(rowsum, axis=0, keepdims=True) * (0.5 / D)

            lv, vjp = jax.vjp(f, x_ref[...], g_ref[...])
            dx, dg = vjp(jnp.ones((1, 1), F32))
            dx_ref[...] = dx
            _gate_grads(dx, y_ref, gt_ref, dy_ref, dgt_ref, i)
            loss_ref[...] += jnp.zeros((8, 128), F32) + lv
            dg_ref[...] += dg

    row = pl.BlockSpec((TB, D), lambda i: (i, 0))
    return pl.pallas_call(
        kern, grid=(NBLK,),
        in_specs=[row, pl.BlockSpec((TB, D), lambda i: ((i // BPE) * lpb + jnp.maximum(i % BPE - 1, 0), 0)),
                  pl.BlockSpec((1, D), lambda i: (0, 0)), row,
                  pl.BlockSpec((1, 1, D), lambda i: (_seg(i) * N_MOD + m_gate, 0, 0))],
        out_specs=[row, row, pl.BlockSpec((1, 1, D), lambda i: (_seg(i), 0, 0)),
                   pl.BlockSpec((8, 128), lambda i: (0, 0)), pl.BlockSpec((1, D), lambda i: (0, 0))],
        out_shape=[jax.ShapeDtypeStruct((R, D), F32), jax.ShapeDtypeStruct((R, D), BF),
                   jax.ShapeDtypeStruct((4, 1, D), F32), jax.ShapeDtypeStruct((8, 128), F32),
                   jax.ShapeDtypeStruct((1, D), F32)],
        compiler_params=_cparams(("arbitrary",)), name="final_loss")(x, target, fg, y, mods)


NC = 24


def mods_fwd(call, ada_w, ada_b):
    cols = ada_w.shape[2]

    def kern(c_ref, w_ref, b_ref, o_ref):
        o_ref[...] = jnp.dot(_silu(c_ref[...]), w_ref[...], precision=HI, preferred_element_type=F32) + b_ref[...]

    return pl.pallas_call(
        kern, grid=(2,),
        in_specs=[pl.BlockSpec((NC, D), lambda l: (0, 0)), pl.BlockSpec((None, D, cols), lambda l: (l, 0, 0)),
                  pl.BlockSpec((None, 1, cols), lambda l: (l, 0, 0))],
        out_specs=pl.BlockSpec((None, NC, cols), lambda l: (l, 0, 0)),
        out_shape=jax.ShapeDtypeStruct((2, NC, cols), F32),
        compiler_params=_cparams(("parallel",)), name="mods_fwd")(call, ada_w, ada_b)


def ada_bwd(call, ada_w, dm):
    cols = ada_w.shape[2]

    def kern(c_ref, w_ref, dm_ref, gw_ref, dc_ref):
        l = pl.program_id(0)
        gw_ref[...] = lax.dot_general(_silu(c_ref[...]), dm_ref[...], TN, precision=HI, preferred_element_type=F32)
        part = lax.dot_general(dm_ref[16:24, :], w_ref[...], NT, precision=HI, preferred_element_type=F32)
        cc = c_ref[16:17, :]
        sg = jax.nn.sigmoid(cc)
        _acc(dc_ref, part * (sg * (1.0 + cc * (1.0 - sg))), l == 0)

    return pl.pallas_call(
        kern, grid=(2,),
        in_specs=[pl.BlockSpec((NC, D), lambda l: (0, 0)), pl.BlockSpec((None, D, cols), lambda l: (l, 0, 0)),
                  pl.BlockSpec((None, NC, cols), lambda l: (l, 0, 0))],
        out_specs=[pl.BlockSpec((None, D, cols), lambda l: (l, 0, 0)), pl.BlockSpec((8, D), lambda l: (0, 0))],
        out_shape=[jax.ShapeDtypeStruct((2, D, cols), F32), jax.ShapeDtypeStruct((8, D), F32)],
        compiler_params=_cparams(("arbitrary",)), name="ada_bwd")(call, ada_w, dm)


def sum_lead(name, a, after=None):
    n, r, c = a.shape
    tr = r
    for cand in (512, 256, 128, 64, 32, 16, 8):
        if r % cand == 0 and cand * c * 4 * n <= 8 * 1024 * 1024:
            tr = cand
            break
    extra = [] if after is None else [after]

    def kern(a_ref, *rest):
        acc = a_ref[0].astype(F32)
        for k in range(1, n):
            acc = acc + a_ref[k].astype(F32)
        rest[-1][...] = acc

    return pl.pallas_call(
        kern, grid=(r // tr,),
        in_specs=[pl.BlockSpec((n, tr, c), lambda i: (0, i, 0))]
        + [pl.BlockSpec(e.shape, lambda i, k=e.ndim: (0,) * k) for e in extra],
        out_specs=pl.BlockSpec((tr, c), lambda i: (i, 0)), out_shape=jax.ShapeDtypeStruct((r, c), F32),
        compiler_params=_cparams(("parallel",)), name=name)(a, *extra)


def add_pairs(name, hs, got, half):
    _, _, r, c = hs.shape

    def kern(half_ref, a_ref, b_ref, o_ref):
        o_ref[...] = (a_ref[...].astype(F32) + b_ref[...].astype(F32)).astype(BF)

    spec = pl.BlockSpec((None, r, c), lambda j, h: (j, 0, 0))
    grid_spec = pltpu.PrefetchScalarGridSpec(
        num_scalar_prefetch=1, grid=(4,),
        in_specs=[pl.BlockSpec((None, None, r, c), lambda j, h: (h[0], j, 0, 0)), spec], out_specs=spec)
    return pl.pallas_call(kern, grid_spec=grid_spec, out_shape=jax.ShapeDtypeStruct(got.shape, BF),
                          compiler_params=_cparams(("parallel",)), name=name)(half, hs, got)


def sum_slabs(name, land, own, where, full, lead):
    _, r, c = land.shape
    tr = r
    for cand in (512, 256, 128, 64, 32, 16):
        if r % cand == 0 and cand * c * 16 <= 4 * 1024 * 1024:
            tr = cand
            break

    def kern(where_ref, full_ref, land_ref, own_ref, o_ref):
        me = where_ref[0]
        acc = None
        for k in range(4):
            t = jnp.where(me == k, own_ref[k], land_ref[k]).astype(F32)
            acc = t if acc is None else acc + t
        o_ref[...] = acc

    spec = pl.BlockSpec((4, tr, c), lambda i, m: (0, i, 0))
    grid_spec = pltpu.PrefetchScalarGridSpec(
        num_scalar_prefetch=1, grid=(r // tr,), in_specs=[pl.BlockSpec(memory_space=pl.ANY), spec, spec],
        out_specs=pl.BlockSpec((None, None, tr, c), lambda i, m: (lead, m[1], i, 0)))
    return pl.pallas_call(kern, grid_spec=grid_spec, out_shape=jax.ShapeDtypeStruct(full.shape, F32),
                          input_output_aliases={1: 0}, compiler_params=_cparams(("parallel",)),
                          name=name)(where, full, land, own)


def adamw(name, w, g, m, v, again=False):
    r, c = w.shape
    tr = r
    for cand in (512, 256, 128, 64, 32, 16, 8):
        if r % cand == 0 and cand * c * 4 <= 2 * 1024 * 1024:
            tr = cand
            break
    c1 = 1.0 / (1.0 - ADAM_B1 ** ADAM_STEP)
    c2 = 1.0 / (1.0 - ADAM_B2 ** ADAM_STEP)

    def kern(w_ref, g_ref, m_ref, v_ref, d_ref, mo_ref, vo_ref, *go_ref):
        gv = g_ref[...]
        mn = ADAM_B1 * m_ref[...] + (1.0 - ADAM_B1) * gv
        vn = ADAM_B2 * v_ref[...] + (1.0 - ADAM_B2) * (gv * gv)
        d_ref[...] = -ADAM_LR * ((mn * c1) / (jnp.sqrt(vn * c2) + ADAM_EPS) + ADAM_WD * w_ref[...])
        mo_ref[...] = mn
        vo_ref[...] = vn
        if again:
            go_ref[0][...] = gv

    spec = pl.BlockSpec((tr, c), lambda i: (i, 0))
    shp = jax.ShapeDtypeStruct((r, c), F32)
    n_out = 4 if again else 3
    return pl.pallas_call(kern, grid=(r // tr,), in_specs=[spec] * 4, out_specs=[spec] * n_out,
                          out_shape=[shp] * n_out, compiler_params=_cparams(("parallel",)), name=name)(w, g, m, v)


def adamw_many(name, ws, gs, ms, vs):
    n = len(ws)
    c1 = 1.0 / (1.0 - ADAM_B1 ** ADAM_STEP)
    c2 = 1.0 / (1.0 - ADAM_B2 ** ADAM_STEP)

    def kern(*refs):
        w, g, m, v, d, mo, vo = (refs[k * n:(k + 1) * n] for k in range(7))
        for k in range(n):
            gv = g[k][...]
            mn = ADAM_B1 * m[k][...] + (1.0 - ADAM_B1) * gv
            vn = ADAM_B2 * v[k][...] + (1.0 - ADAM_B2) * (gv * gv)
            d[k][...] = -ADAM_LR * ((mn * c1) / (jnp.sqrt(vn * c2) + ADAM_EPS) + ADAM_WD * w[k][...])
            mo[k][...] = mn
            vo[k][...] = vn

    out = pl.pallas_call(kern, out_shape=[jax.ShapeDtypeStruct(a.shape, F32) for a in ws] * 3,
                         compiler_params=pltpu.CompilerParams(vmem_limit_bytes=VMEM_LIMIT),
                         name=name)(*ws, *gs, *ms, *vs)
    return out[:n], out[n:2 * n], out[2 * n:]


def all_gather8(name, xs, after=None):
    m_per, n = xs.shape
    extra = [] if after is None else [after]

    def body(x_ref, *rest):
        out_ref, send_sems, recv_sems, local_sem = rest[len(extra):]
        x, y, c = lax.axis_index("x"), lax.axis_index("y"), lax.axis_index("c")
        me, sibling = (x, y, c), (x, y, 1 - c)
        chips = [(1 - x, y), (x, 1 - y), (1 - x, 1 - y)]

        def rows(px, py, pc):
            return out_ref.at[pl.ds((4 * px + 2 * py + pc) * m_per, m_per), :]

        def copy(k, block, to, src=None):
            return pltpu.make_async_remote_copy(
                src_ref=rows(*block) if src is None else src, dst_ref=rows(*block),
                send_sem=send_sems.at[k], recv_sem=recv_sems.at[k], device_id=to, device_id_type=MESH)

        mine = pltpu.make_async_copy(x_ref, rows(*me), local_sem)
        mine.start()
        first = [copy(0, me, sibling, src=x_ref)]
        first += [copy(1 + j, me, (*chip, c), src=x_ref) for j, chip in enumerate(chips)]
        for cp in first:
            cp.start()
        passed = [copy(4 + j, (*chip, c), sibling) for j, chip in enumerate(chips)]
        for j, chip in enumerate(chips):
            copy(1 + j, (*chip, c), me).wait_recv()
            passed[j].start()
        copy(0, sibling, me).wait_recv()
        for j, chip in enumerate(chips):
            copy(4 + j, (*chip, 1 - c), me).wait_recv()
        for cp in first + passed:
            cp.wait_send()
        mine.wait()

    return pl.pallas_call(
        body, out_shape=jax.ShapeDtypeStruct((8 * m_per, n), xs.dtype),
        in_specs=[pl.BlockSpec(memory_space=pltpu.VMEM)] * (1 + len(extra)),
        out_specs=pl.BlockSpec(memory_space=pltpu.VMEM),
        scratch_shapes=[pltpu.SemaphoreType.DMA((7,)), pltpu.SemaphoreType.DMA((7,)), pltpu.SemaphoreType.DMA],
        compiler_params=pltpu.CompilerParams(vmem_limit_bytes=VMEM_LIMIT), name=name)(xs, *extra)


def sibling_merge(name, fulls):
    n = len(fulls)
    slots = [(a, l) for a in range(n) for l in range(fulls[a].shape[0])]

    def body(*refs):
        buf = refs[n:2 * n]
        send_sems, recv_sems = refs[2 * n], refs[2 * n + 1]
        c = lax.axis_index("c")
        sibling = (lax.axis_index("x"), lax.axis_index("y"), 1 - c)
        sends, recvs = [], []
        for k, (a, l) in enumerate(slots):
            kw = dict(send_sem=send_sems.at[k], recv_sem=recv_sems.at[k], device_id=sibling, device_id_type=MESH)
            sends.append(pltpu.make_async_remote_copy(src_ref=buf[a].at[l, c], dst_ref=buf[a].at[l, c], **kw))
            recvs.append(pltpu.make_async_remote_copy(src_ref=buf[a].at[l, c], dst_ref=buf[a].at[l, 1 - c], **kw))
        for cp in sends:
            cp.start()
        for cp in recvs:
            cp.wait_recv()
        for cp in sends:
            cp.wait_send()

    anyspec = pl.BlockSpec(memory_space=pl.ANY)
    return pl.pallas_call(
        body, out_shape=[jax.ShapeDtypeStruct(s.shape, s.dtype) for s in fulls],
        in_specs=[anyspec] * n, out_specs=[anyspec] * n, input_output_aliases={a: a for a in range(n)},
        scratch_shapes=[pltpu.SemaphoreType.DMA((len(slots),)), pltpu.SemaphoreType.DMA((len(slots),))],
        name=name)(*fulls)


def place_own(name, land, src, chip):
    c = src.shape[-1]
    r = src.size // c
    tr = r
    for cand in (1024, 512, 256, 128, 64, 32, 16):
        if r % cand == 0 and cand * c * 2 <= 2 * 1024 * 1024:
            tr = cand
            break

    def kern(chip_ref, land_ref, src_ref, out_ref):
        out_ref[...] = src_ref[...]

    grid_spec = pltpu.PrefetchScalarGridSpec(
        num_scalar_prefetch=1, grid=(r // tr,),
        in_specs=[pl.BlockSpec(memory_space=pl.ANY), pl.BlockSpec((tr, c), lambda i, m: (i, 0))],
        out_specs=pl.BlockSpec((None, tr, c), lambda i, m: (m[0], i, 0)))
    out = pl.pallas_call(kern, grid_spec=grid_spec, out_shape=jax.ShapeDtypeStruct((4, r, c), land.dtype),
                         input_output_aliases={1: 0}, compiler_params=_cparams(("parallel",)),
                         name=name)(chip, land.reshape(4, r, c), src.reshape(r, c))
    return out.reshape(land.shape)


def _half_copies(src, land, send_sems, recv_sems):
    c = lax.axis_index("c")
    sibling = (lax.axis_index("x"), lax.axis_index("y"), 1 - c)
    pairs = []
    for a in range(len(src)):
        cp = pltpu.make_async_remote_copy(src_ref=src[a].at[1 - c], dst_ref=land[a], send_sem=send_sems.at[a],
                                          recv_sem=recv_sems.at[a], device_id=sibling, device_id_type=MESH)
        pairs.append((cp, cp))
    return pairs


def _chip_copies(src, land, send_sems, recv_sems, scatter):
    x, y, c = lax.axis_index("x"), lax.axis_index("y"), lax.axis_index("c")
    me = 2 * x + y
    pairs = []
    for a in range(len(src)):
        for j, (px, py) in enumerate([(1 - x, y), (x, 1 - y), (1 - x, 1 - y)]):
            to = 2 * px + py
            out = src[a].at[to] if scatter else src[a]
            kw = dict(send_sem=send_sems.at[3 * a + j], recv_sem=recv_sems.at[3 * a + j], device_id=(px, py, c),
                      device_id_type=MESH)
            pairs.append((pltpu.make_async_remote_copy(src_ref=out, dst_ref=land[a].at[me], **kw),
                          pltpu.make_async_remote_copy(src_ref=out, dst_ref=land[a].at[to], **kw)))
    return pairs


_HBM = pl.BlockSpec(memory_space=pltpu.HBM)
_SEM = pl.BlockSpec(memory_space=pltpu.SEMAPHORE)


GATHER = (functools.partial(_chip_copies, scatter=False), 3)
SCATTER = (functools.partial(_chip_copies, scatter=True), 3)
TO_SIBLING = (_half_copies, 1)


def exchange_start(name, groups, plan):
    copies, per = plan
    sizes = [len(s) for s, _ in groups]
    flat = [a for s, l in groups for a in list(s) + list(l)]
    ng = len(groups)

    def body(*refs):
        ins, outs = refs[:len(flat)], refs[len(flat):]
        off = 0
        for g, n in enumerate(sizes):
            src, land = ins[off:off + n], ins[off + n:off + 2 * n]
            off += 2 * n
            for send, _ in copies(src, land, outs[2 * g], outs[2 * g + 1]):
                send.start()
        outs[-1][...] = jnp.zeros_like(outs[-1])

    out_shape = []
    for n in sizes:
        out_shape += [pltpu.SemaphoreType.DMA((per * n,)), pltpu.SemaphoreType.DMA((per * n,))]
    out_shape += [pltpu.HBM(a.shape, a.dtype) for a in flat] + [jax.ShapeDtypeStruct((8, 128), F32)]
    res = pl.pallas_call(
        body, out_shape=tuple(out_shape), in_specs=[_HBM] * len(flat),
        out_specs=tuple([_SEM] * (2 * ng) + [_HBM] * len(flat) + [pl.BlockSpec(memory_space=pltpu.VMEM)]),
        input_output_aliases={k: 2 * ng + k for k in range(len(flat))},
        compiler_params=pltpu.CompilerParams(has_side_effects=pltpu.SideEffectType.DATAFLOW_SIDE_EFFECTING),
        name=name)(*[pltpu.with_memory_space_constraint(a, pltpu.HBM) for a in flat])
    handles, off = [], 2 * ng
    for g, n in enumerate(sizes):
        handles.append((res[2 * g], res[2 * g + 1], list(res[off:off + n]), list(res[off + n:off + 2 * n])))
        off += 2 * n
    return handles, res[-1]


def exchange_wait(name, handle, after, plan):
    send_sems, recv_sems, srcs, lands = handle
    n = len(srcs)

    def body(*refs):
        src, land = refs[:n], refs[n:2 * n]
        for send, recv in plan[0](src, land, refs[2 * n], refs[2 * n + 1]):
            send.wait_send()
            recv.wait_recv()

    res = pl.pallas_call(
        body, out_shape=tuple(pltpu.HBM(a.shape, a.dtype) for a in srcs + lands),
        in_specs=[_HBM] * (2 * n) + [_SEM, _SEM, pl.BlockSpec(memory_space=pl.ANY)],
        out_specs=tuple([_HBM] * (2 * n)), input_output_aliases={k: k for k in range(2 * n)},
        compiler_params=pltpu.CompilerParams(has_side_effects=pltpu.SideEffectType.DATAFLOW_SIDE_EFFECTING),
        name=name)(*srcs, *lands, send_sems, recv_sems, after)
    return list(res[:n]), list(res[n:])


def _rope_tables(d_rot, reps):
    rows = L // GRID_W
    row = np.repeat(np.arange(rows), GRID_W).astype(np.float32)
    col = np.tile(np.arange(GRID_W), rows).astype(np.float32)
    d_axis = d_rot // 2
    inv = (ROPE_THETA ** (-np.arange(0, d_axis, 2, dtype=np.float32) / d_axis)).astype(np.float32)
    ang = np.concatenate([row[:, None] * inv, col[:, None] * inv], axis=-1).astype(np.float32)
    cos, sin = np.cos(ang).astype(np.float32), np.sin(ang).astype(np.float32)
    c = np.repeat(cos, 2, axis=-1)
    s = np.stack([-sin, sin], axis=-1).reshape(L, d_rot)
    c = np.concatenate([np.ones((LC, d_rot), np.float32), c], axis=0)
    s = np.concatenate([np.zeros((LC, d_rot), np.float32), s], axis=0)
    return np.tile(c, (1, reps)), np.tile(s, (1, reps))


def _group_consts():
    g = np.arange(512) // 64
    avg = (g[:, None] == g[None, :]).astype(np.float32) / 64.0
    masks = (np.arange(8)[:, None] == g[None, :]).astype(np.float32).reshape(8, 1, 512)
    return jnp.asarray(avg, BF), jnp.asarray(masks)


def _pack(items):
    flat = jnp.concatenate([a.reshape(-1).astype(F32) for a in items])
    n = flat.shape[0]
    rows = -(-n // D)
    rows = -(-rows // 8) * 8
    return jnp.pad(flat, (0, rows * D - n)).reshape(rows, D)


def _unpack(buf, shapes):
    lead = buf.shape[:-2]
    flat = buf.reshape(lead + (-1,))
    out, off = [], 0
    for shp in shapes:
        n = int(np.prod(shp))
        out.append(flat[..., off:off + n].reshape(lead + tuple(shp)))
        off += n
    return out


def _arrive(prm, key, after):
    if callable(prm[key]):
        prm[key](after)
    return prm[key]


def _layer_fwd(i, x, h, mods, prm, consts, nxt):
    sv = {}
    sv["x0"] = x
    sv["h"] = h
    p = proj_in(f"proj_in_{i}", h, _arrive(prm, "w_in", h))
    sv["p"] = p
    if i == 0:
        q, kv, m2 = even_tok_fwd(p, consts["cos_e"], consts["sin_e"], prm["gq"], prm["gk"], prm["gs"],
                                 prm["sgu_w"], prm["sgu_b"], consts["avg"], consts["masks"])
        o, lse = attn_fwd("attn_fwd_0", q, kv, GQA_HEADS)
        sv.update(q=q, kv=kv)
    else:
        q, kv, y = odd_tok_fwd(p, consts["cos_o"], consts["sin_o"], prm["gq"], prm["gkv"], prm["wq"], prm["wkk"],
                               prm["wkv"], consts["spread"])
        o, lse = attn_fwd("attn_fwd_1", q, kv, MLA_HEADS)
        z, m2 = conf_fwd(y, prm["conv_w"], prm["conv_b"], prm["ln_g"], prm["ln_b"])
        sv.update(q=q, kv=kv, y=y, z=z)
    sv.update(o=o, lse=lse, m2=m2)
    x1, y1, h2 = proj_out(f"proj_out_{i}", o, m2, _arrive(prm, "w_out", o), x, mods, 2, prm["norm2_g"], 3, 4,
                          last=nxt is None)
    sv.update(x1=x1, y1=y1)
    f = mlp_up(f"mlp_up_{i}", h2, _arrive(prm, "w1", h2))
    x2, y2, *h_next = mlp_down(f"mlp_down_{i}", f, prm["w2"], x1, mods, 5, nxt)
    sv.update(h2=h2, f=f, y2=y2)
    return x2, (h_next[0] if h_next else None), sv


def _layer_bwd(i, dx, dy2, dg2, sv, mods, prm, consts, hook, entry, below):
    gr = {}
    da = mlp_bwd_da(f"mlp_bwd_da_{i}", dy2, prm["w2"], sv["f"], after=entry)
    tiles8 = [(h, j) for h in range(2) for j in range(4)]
    gr["w1"] = mm_tn(f"grad_w1_{i}", sv["h2"], da, tiles8, 512, D).reshape(2, 4, 512, D)
    gr["w2"] = mm_tn(f"grad_w2_{i}", sv["f"], dy2, [(2 * j + h, 0) for h in range(2) for j in range(4)],
                     512, D).reshape(2, 4, 512, D)
    dh2 = mlp_bwd_dh(f"mlp_bwd_dh_{i}", da, prm["w1"])
    dx1, dy1, dg1, dsh2, dsc2, gr["norm2_g"] = modnorm_bwd(
        f"norm2_bwd_{i}", sv["x1"], dh2, dx, mods, prm["norm2_g"], 3, 4, gate=(sv["y1"], mods, 2),
        after=hook(f"{i}:mlp", gr, dh2), last=below is not None)
    dcat = mm_nt(f"proj_out_bwd_{i}", dy1, prm["w_out"], after=hook(f"{i}:mid", gr, dy1))
    go = mm_tn(f"grad_wout_a_{i}", sv["o"], dy1, [(0, 0)], 512, D).reshape(2, 2, 128, D)
    gm = mm_tn(f"grad_wout_b_{i}", sv["m2"], dy1, [(0, 0)], 512, D).reshape(2, 2, 128, D)
    gr["w_out"] = jnp.concatenate([go, gm], axis=0).transpose(1, 0, 2, 3)
    if i == 0:
        dq, dkv = attn_bwd("attn_bwd_0", sv["q"], sv["kv"], sv["o"], dcat, sv["lse"], GQA_HEADS)
        dp, gr["gq"], gr["gk"], gr["gs"], gr["sgu_w"], gr["sgu_b"] = even_tok_bwd(
            sv["p"], dq, dkv, dcat, consts["cos_e"], consts["sin_e"], prm["gq"], prm["gk"],
            prm["gs"], prm["sgu_w"], prm["sgu_b"], consts["avg"], consts["masks"])
    else:
        dq, dkv = attn_bwd("attn_bwd_1", sv["q"], sv["kv"], sv["o"], dcat, sv["lse"], MLA_HEADS)
        dz, gr["ln_g"], gr["ln_b"], gr["conv_b"] = conf_bwd_ln(sv["z"], dcat, prm["ln_g"], prm["ln_b"])
        dyc, gr["conv_w"] = conf_bwd_conv(sv["y"], dz, prm["conv_w"])
        dp, gr["gq"], gr["gkv"], gr["wq"], gr["wkk"], gr["wkv"] = odd_tok_bwd(
            sv["p"], dq, dkv, dyc, consts["cos_o"], consts["sin_o"], prm["gq"], prm["gkv"], prm["wq"], prm["wkk"],
            prm["wkv"], consts["spread"])
    n_in = prm["w_in"].shape[1]
    gr["w_in"] = mm_tn(f"grad_win_{i}", sv["h"], dp, [(0, 0), (1, 0)], 512, n_in)
    dh = mm_nt(f"proj_in_bwd_{i}", dp, prm["w_in"])
    if below:
        dx0, dy2b, dg2b, dsh1, dsc1, gr["norm1_g"] = modnorm_bwd(
            f"norm1_bwd_{i}", sv["x0"], dh, dx1, mods, prm["norm1_g"], 0, 1, gate=(below[0], below[1], 5))
        down = (dy2b, dg2b)
    else:
        dx0, dsh1, dsc1, gr["norm1_g"] = modnorm_bwd(f"norm1_bwd_{i}", sv["x0"], dh, dx1, mods, prm["norm1_g"], 0, 1,
                                                     lat_only=True)
        down = None
    dmods = jnp.concatenate([dsh1, dsc1, dg1, dsh2, dsc2, dg2], axis=1)
    return dx0, down, dmods, gr, hook(f"{i}:end", gr, dx0)


def local_step(xcat, target, mods, prms, final_g, hook=lambda point, grads, fresh: None):
    avg, masks = _group_consts()
    cos_e, sin_e = _rope_tables(64, 8)
    ck, sk = _rope_tables(32, 1)
    one64, zero64 = np.ones((SEQ, 64), np.float32), np.zeros((SEQ, 64), np.float32)
    one96, zero96 = np.ones((SEQ, 96), np.float32), np.zeros((SEQ, 96), np.float32)
    cos_o = np.concatenate([np.tile(np.concatenate([one64, ck], axis=1), (1, 8)), ck, one96], axis=1)
    sin_o = np.concatenate([np.tile(np.concatenate([zero64, sk], axis=1), (1, 8)), sk, zero96], axis=1)
    lane = np.arange(768)
    spread = np.zeros((128, 768), np.float32)
    spread[lane % 96 - 64, lane] = (lane % 96 >= 64)
    consts = dict(avg=avg, masks=masks, cos_e=jnp.asarray(cos_e), sin_e=jnp.asarray(sin_e),
                  cos_o=jnp.asarray(cos_o), sin_o=jnp.asarray(sin_o), spread=jnp.asarray(spread, BF))
    x = xcat
    h = modnorm_fwd("norm1_fwd_0", x, mods[0], prms[0]["norm1_g"], 0, 1)
    saved = []
    for i in range(2):
        x, h, sv = _layer_fwd(i, x, h, mods[i], prms[i], consts, (mods[1], prms[1]["norm1_g"]) if i == 0 else None)
        saved.append(sv)
    dx, dy2, dg2, loss, dfg = final_loss(x, target, final_g, saved[1]["y2"], mods[1], 5)
    dmods, grads = [None, None], [None, None]
    entry, down = None, (dy2, dg2)
    for i in (1, 0):
        below = (saved[0]["y2"], mods[0]) if i == 1 else None
        dx, down, dmods[i], grads[i], entry = _layer_bwd(i, dx, down[0], down[1], saved[i], mods[i], prms[i], consts,
                                                         hook, entry, below)
    return loss, dx, dmods, grads, dfg, entry


def _row(v):
    return v.reshape(1, -1).astype(F32)


def odd_in_params(od_w_in, w_uq, w_ukv):
    od = jnp.concatenate([od_w_in[:, 0:416], jnp.zeros((D, 96), od_w_in.dtype), od_w_in[:, 416:OD_IN]], axis=1)
    ukv = w_ukv.reshape(128, 8, 128)
    wkk = jnp.pad(ukv[:, :, :64], ((0, 0), (0, 0), (0, 32))).reshape(128, 768)
    return dict(w_in=od, wq=w_uq, wkk=wkk, wkv=ukv[:, :, 64:].reshape(128, 512))


def small_params(small):
    p0 = dict(norm1_g=_row(small["norm1_g"][0]), norm2_g=_row(small["norm2_g"][0]),
              gq=jnp.tile(_row(small["ev_q_norm_g"]), (1, 8)), gk=jnp.tile(_row(small["ev_k_norm_g"]), (1, 2)),
              gs=_row(small["ev_sgu_norm_g"]), sgu_w=small["ev_sgu_w"].reshape(8, 128, 128).astype(F32),
              sgu_b=small["ev_sgu_b"].reshape(8, 128, 1).astype(F32))
    p1 = dict(norm1_g=_row(small["norm1_g"][1]), norm2_g=_row(small["norm2_g"][1]),
              gq=_row(small["od_q_norm_g"]), gkv=_row(small["od_kv_norm_g"]),
              conv_w=jnp.pad(small["od_conv_w"].reshape(CONV_K, 512).astype(F32), ((0, 1), (0, 0))),
              conv_b=_row(small["od_conv_b"]), ln_g=_row(small["od_ln_g"]), ln_b=_row(small["od_ln_b"]))
    return [p0, p1]


def small_grads_natural(grads, dfg):
    g0, g1 = grads
    return dict(
        norm1_g=jnp.concatenate([g0["norm1_g"], g1["norm1_g"]], axis=0),
        norm2_g=jnp.concatenate([g0["norm2_g"], g1["norm2_g"]], axis=0),
        ev_q_norm_g=g0["gq"].reshape(8, 64).sum(0).reshape(1, 64),
        ev_k_norm_g=g0["gk"].reshape(2, 64).sum(0).reshape(1, 64),
        ev_sgu_norm_g=g0["gs"].reshape(1, 8, 64),
        ev_sgu_w=g0["sgu_w"].reshape(1, 8, 128, 128),
        ev_sgu_b=g0["sgu_b"].reshape(1, 8, 128),
        od_q_norm_g=g1["gq"].reshape(1, 256),
        od_kv_norm_g=g1["gkv"].reshape(1, 128),
        od_conv_w=g1["conv_w"][0:CONV_K].reshape(1, CONV_K, 512),
        od_conv_b=g1["conv_b"].reshape(1, 512),
        od_ln_g=g1["ln_g"].reshape(1, 512),
        od_ln_b=g1["ln_b"].reshape(1, 512),
        final_g=dfg.reshape(D))


def layer_grads_hs(i, g, part="all"):
    def cols(a):
        k, n = a.shape
        return a.reshape(2, k // 2, 4, n // 4).transpose(0, 2, 1, 3).astype(BF)

    mlp = [(("mlp_w1", i), g["w1"]), (("mlp_w2", i), g["w2"])]
    if part == "mlp":
        return mlp
    rest = [(("w_out", i), g["w_out"])]
    if i == 0:
        rest.append((("ev_w_in", 0), cols(g["w_in"].reshape(D, EV_IN))))
    else:
        od = g["w_in"].reshape(D, OD_PAD)
        od = jnp.concatenate([od[:, 0:416], od[:, 512:OD_PAD]], axis=1)
        ukv = jnp.concatenate([g["wkk"].reshape(128, 8, 96)[:, :, :64], g["wkv"].reshape(128, 8, 64)], axis=2)
        rest += [(("od_w_in", 0), cols(od)), (("od_w_uq", 0), cols(g["wq"])),
                 (("od_w_ukv", 0), cols(ukv.reshape(128, 1024)))]
    return rest if part == "rest" else mlp + rest


WEIGHT_NAMES = ['c_ctx', 'ada_w', 'ada_b', 'norm1_g', 'norm2_g', 'w_out', 'mlp_w1', 'mlp_w2', 'ev_w_in',
                'ev_q_norm_g', 'ev_k_norm_g', 'ev_sgu_norm_g', 'ev_sgu_w', 'ev_sgu_b', 'od_w_in', 'od_q_norm_g',
                'od_kv_norm_g', 'od_w_uq', 'od_w_ukv', 'od_conv_w', 'od_conv_b', 'od_ln_g', 'od_ln_b', 'final_g']
REPL_SMALL = ['norm1_g', 'norm2_g', 'ev_q_norm_g', 'ev_k_norm_g', 'ev_sgu_norm_g', 'ev_sgu_w', 'ev_sgu_b',
              'od_kv_norm_g', 'final_g']
SHARD_SMALL = ['od_q_norm_g', 'od_conv_w', 'od_conv_b', 'od_ln_g', 'od_ln_b']
BIG = ['w_out', 'mlp_w1', 'mlp_w2', 'ev_w_in', 'od_w_in', 'od_w_uq', 'od_w_ukv']


def _gather_last(parts):
    return jnp.concatenate([parts[k] for k in range(4)], axis=-1)


class _Reduce:
    def __init__(self, tag, named, half, where):
        self.tag, self.half, self.where = tag, half, where
        self.names, self.hs = zip(*named)
        self.hs = list(self.hs)

    def to_sibling(self):
        lands = [lax.empty(a.shape[1:], BF) for a in self.hs]
        (self.h1,), token = exchange_start(f"rs_sibling_start_{self.tag}", [(self.hs, lands)], TO_SIBLING)
        return token

    def to_chips(self, after):
        hs, got = exchange_wait(f"rs_sibling_wait_{self.tag}", self.h1, after, TO_SIBLING)
        pair = [add_pairs(f"rs_add_{self.tag}_{k}", a, b, self.half) for k, (a, b) in enumerate(zip(hs, got))]
        lands = [lax.empty(p.shape, BF) for p in pair]
        (self.h2,), token = exchange_start(f"rs_chips_start_{self.tag}", [(pair, lands)], SCATTER)
        return token

    def finish(self, after, bufs):
        pair, land = exchange_wait(f"rs_chips_wait_{self.tag}", self.h2, after, SCATTER)
        for k, ((n, idx), l, p) in enumerate(zip(self.names, land, pair)):
            bufs[n] = sum_slabs(f"rs_sum_{self.tag}_{k}", l, p, self.where, bufs[n], idx)


def kernel(x, c, ctx, c_ctx, ada_w, ada_b, norm1_g, norm2_g, w_out, mlp_w1, mlp_w2, ev_w_in, ev_q_norm_g, ev_k_norm_g, ev_sgu_norm_g, ev_sgu_w, ev_sgu_b, od_w_in, od_q_norm_g, od_kv_norm_g, od_w_uq, od_w_ukv, od_conv_w, od_conv_b, od_ln_g, od_ln_b, final_g, loss_target, m_c_ctx, m_ada_w, m_ada_b, m_norm1_g, m_norm2_g, m_w_out, m_mlp_w1, m_mlp_w2, m_ev_w_in, m_ev_q_norm_g, m_ev_k_norm_g, m_ev_sgu_norm_g, m_ev_sgu_w, m_ev_sgu_b, m_od_w_in, m_od_q_norm_g, m_od_kv_norm_g, m_od_w_uq, m_od_w_ukv, m_od_conv_w, m_od_conv_b, m_od_ln_g, m_od_ln_b, m_final_g, v_c_ctx, v_ada_w, v_ada_b, v_norm1_g, v_norm2_g, v_w_out, v_mlp_w1, v_mlp_w2, v_ev_w_in, v_ev_q_norm_g, v_ev_k_norm_g, v_ev_sgu_norm_g, v_ev_sgu_w, v_ev_sgu_b, v_od_w_in, v_od_q_norm_g, v_od_kv_norm_g, v_od_w_uq, v_od_w_ukv, v_od_conv_w, v_od_conv_b, v_od_ln_g, v_od_ln_b, v_final_g):
    w = dict(c_ctx=c_ctx, ada_w=ada_w, ada_b=ada_b, norm1_g=norm1_g, norm2_g=norm2_g, w_out=w_out, mlp_w1=mlp_w1,
             mlp_w2=mlp_w2, ev_w_in=ev_w_in, ev_q_norm_g=ev_q_norm_g, ev_k_norm_g=ev_k_norm_g,
             ev_sgu_norm_g=ev_sgu_norm_g, ev_sgu_w=ev_sgu_w, ev_sgu_b=ev_sgu_b, od_w_in=od_w_in,
             od_q_norm_g=od_q_norm_g, od_kv_norm_g=od_kv_norm_g, od_w_uq=od_w_uq, od_w_ukv=od_w_ukv,
             od_conv_w=od_conv_w, od_conv_b=od_conv_b, od_ln_g=od_ln_g, od_ln_b=od_ln_b, final_g=final_g)
    mom = dict(c_ctx=m_c_ctx, ada_w=m_ada_w, ada_b=m_ada_b, norm1_g=m_norm1_g, norm2_g=m_norm2_g, w_out=m_w_out,
               mlp_w1=m_mlp_w1, mlp_w2=m_mlp_w2, ev_w_in=m_ev_w_in, ev_q_norm_g=m_ev_q_norm_g,
               ev_k_norm_g=m_ev_k_norm_g, ev_sgu_norm_g=m_ev_sgu_norm_g, ev_sgu_w=m_ev_sgu_w, ev_sgu_b=m_ev_sgu_b,
               od_w_in=m_od_w_in, od_q_norm_g=m_od_q_norm_g, od_kv_norm_g=m_od_kv_norm_g, od_w_uq=m_od_w_uq,
               od_w_ukv=m_od_w_ukv, od_conv_w=m_od_conv_w, od_conv_b=m_od_conv_b, od_ln_g=m_od_ln_g,
               od_ln_b=m_od_ln_b, final_g=m_final_g)
    var = dict(c_ctx=v_c_ctx, ada_w=v_ada_w, ada_b=v_ada_b, norm1_g=v_norm1_g, norm2_g=v_norm2_g, w_out=v_w_out,
               mlp_w1=v_mlp_w1, mlp_w2=v_mlp_w2, ev_w_in=v_ev_w_in, ev_q_norm_g=v_ev_q_norm_g,
               ev_k_norm_g=v_ev_k_norm_g, ev_sgu_norm_g=v_ev_sgu_norm_g, ev_sgu_w=v_ev_sgu_w, ev_sgu_b=v_ev_sgu_b,
               od_w_in=v_od_w_in, od_q_norm_g=v_od_q_norm_g, od_kv_norm_g=v_od_kv_norm_g, od_w_uq=v_od_w_uq,
               od_w_ukv=v_od_w_ukv, od_conv_w=v_od_conv_w, od_conv_b=v_od_conv_b, od_ln_g=v_od_ln_g,
               od_ln_b=v_od_ln_b, final_g=v_final_g)
    xi, yi, ci = lax.axis_index("x"), lax.axis_index("y"), lax.axis_index("c")
    chip = 2 * xi + yi
    dev = 2 * chip + ci

    shard_shapes = [w[n].shape for n in SHARD_SMALL]
    g0 = all_gather8("ag_small", _pack([c] + [w[n] for n in SHARD_SMALL]))
    g0 = g0.reshape(8, -1, D)
    parts = _unpack(g0, [c.shape] + shard_shapes)
    c_all = parts[0].reshape(16, D)
    small_full = {n: _gather_last(p[0::2]) for n, p in zip(SHARD_SMALL, parts[1:])}
    call = jnp.concatenate([c_all, c_ctx.reshape(1, D), jnp.zeros((NC - 17, D), F32)], axis=0)

    cols = ada_w.shape[2]
    ada_b_sh = lax.dynamic_slice(ada_b, (0, chip * cols), (2, cols)).reshape(2, 1, cols)
    mt = mods_fwd(call, ada_w, ada_b_sh)
    mt = all_gather8("ag_mods", mt.reshape(2 * NC, cols)).reshape(8, 2, NC, cols)
    table = mt[0::2].transpose(1, 2, 0, 3).reshape(2, NC, 4 * cols)
    mods = []
    for i in range(2):
        lat = lax.dynamic_slice(table[i], (2 * dev, 0), (2, 4 * cols))
        mc = table[i, 16]
        mods.append(jnp.stack([mc, lat[0], mc, lat[1]]).reshape(4 * N_MOD, 1, D))

    order = [[("ev_w_in", 0)], [("w_out", 0), ("mlp_w1", 0), ("mlp_w2", 0)],
             [("od_w_in", 0), ("od_w_uq", 0), ("od_w_ukv", 0), ("w_out", 1)], [("mlp_w1", 1), ("mlp_w2", 1)]]
    groups = []
    for names in order:
        srcs = [w[n][i].astype(BF) for n, i in names]
        groups.append((srcs, [lax.empty((4,) + s.shape, BF) for s in srcs]))
    groups[0][0][0], table = lax.optimization_barrier((groups[0][0][0], table))
    handles, token = exchange_start("gather_start", groups, GATHER)
    mods[0] = mods[0] + token[0, 0]
    small = {n: w[n] for n in REPL_SMALL}
    small.update(small_full)
    prms = small_params(small)

    chip1 = chip.reshape(1).astype(jnp.int32)

    def arrived(k, after):
        srcs, lands = exchange_wait(f"gather_wait_{k}", handles[k], after, GATHER)
        return [place_own(f"gather_own_{k}_{a}", l, s, chip1) for a, (l, s) in enumerate(zip(lands, srcs))]

    def arrive_ev_in(after):
        (ev,) = arrived(0, after)
        prms[0]["w_in"] = _gather_last(ev)

    def arrive_ev_rest(after):
        wo, w1, w2 = arrived(1, after)
        prms[0].update(w_out=wo.reshape(D, D), w1=w1, w2=w2)

    def arrive_od(after):
        od, uq, ukv, wo = arrived(2, after)
        prms[1].update(odd_in_params(_gather_last(od), _gather_last(uq), _gather_last(ukv)), w_out=wo.reshape(D, D))

    def arrive_od_mlp(after):
        w1, w2 = arrived(3, after)
        prms[1].update(w1=w1, w2=w2)

    prms[0]["w_in"] = arrive_ev_in
    prms[0]["w_out"] = arrive_ev_rest
    prms[1]["w_in"] = arrive_od
    prms[1]["w1"] = arrive_od_mlp

    half = ci.reshape(1).astype(jnp.int32)
    where = jnp.stack([chip, ci]).astype(jnp.int32)
    red = {}

    def hook(point, g, fresh):
        if point == "1:end":
            red["l1"] = _Reduce("l1", layer_grads_hs(1, g, "all"), half, where)
            return red["l1"].to_sibling()
        if point == "0:mlp":
            red["l0_mlp"] = _Reduce("l0_mlp", layer_grads_hs(0, g, "mlp"), half, where)
            return red["l1"].to_chips(fresh) + red["l0_mlp"].to_sibling()
        if point == "0:mid":
            return red["l0_mlp"].to_chips(fresh)
        if point == "0:end":
            red["l0_rest"] = _Reduce("l0_rest", layer_grads_hs(0, g, "rest"), half, where)
            return red["l0_rest"].to_sibling()
        return None

    xin = (ctx.reshape(NEX * LC, D), x.reshape(NEX * L, D))
    loss_p, dx, dmods, grads, dfg, last = local_step(xin, loss_target.reshape(NEX * L, D), mods, prms,
                                                     final_g.reshape(1, D), hook)
    grad_x = dx.reshape(NEX, L, D)

    sg = small_grads_natural(grads, dfg)
    dm = jnp.stack([d.reshape(4, N_MOD * D) for d in dmods])
    small_names = REPL_SMALL + SHARD_SMALL
    items = [dm[:, 1::2], dm[:, 0] + dm[:, 2]] + [sg[n] for n in small_names] + [loss_p[0:1, 0:1]]
    shapes = [a.shape for a in items]
    g1 = all_gather8("ag_grads", _pack(items), after=last)
    started = red["l0_rest"].to_chips(g1)
    rows1 = g1.shape[0] // 8
    g1 = g1.reshape(8, rows1, D)
    tot = _unpack(sum_lead("sum_small", g1, after=started), shapes)
    dm_lat = _unpack(g1, shapes[:1])[0]
    dm_lat = dm_lat.transpose(1, 0, 2, 3).reshape(2, 16, N_MOD * D)
    dm_all = jnp.concatenate([dm_lat, tot[1][:, None], jnp.zeros((2, NC - 17, N_MOD * D), F32)], axis=1)
    gsum = dict(zip(small_names, tot[2:2 + len(small_names)]))
    loss = tot[-1].reshape(())
    grad = {n: gsum[n].reshape(w[n].shape) for n in REPL_SMALL}
    for n in SHARD_SMALL:
        k = w[n].shape[-1]
        grad[n] = lax.dynamic_slice_in_dim(gsum[n], chip * k, k, axis=gsum[n].ndim - 1)
    grad["ada_b"] = sum_lead("sum_ada_b", dm_all.transpose(1, 0, 2).reshape(NC, 2 * N_MOD, D)).reshape(2, N_MOD * D)

    dm_sh = lax.dynamic_slice(dm_all, (0, 0, chip * cols), (2, NC, cols))
    grad["ada_w"], dcc = ada_bwd(call, ada_w, dm_sh)
    dcc = all_gather8("ag_cctx", dcc).reshape(8, 8, D)
    grad["c_ctx"] = sum_lead("sum_cctx", dcc[0::2])[0]

    delta, new_m, new_v = {}, {}, {}

    def adam_big(n, again):
        shp = w[n].shape
        two_d = (shp[0] * shp[1], shp[2])
        res = adamw(f"adamw_{n}", w[n].reshape(two_d), grad[n].reshape(two_d), mom[n].reshape(two_d),
                    var[n].reshape(two_d), again)
        delta[n], new_m[n], new_v[n] = [a.reshape(shp) for a in res[:3]]
        if again:
            grad[n] = res[3].reshape(shp)

    adam_big('ada_w', False)
    rest = [n for n in WEIGHT_NAMES if n not in ['ada_w'] + BIG]
    flat2 = lambda a: a.reshape(-1, a.shape[-1])
    outs = adamw_many("adamw_small", [flat2(w[n]) for n in rest], [flat2(grad[n]) for n in rest],
                      [flat2(mom[n]) for n in rest], [flat2(var[n]) for n in rest])
    for dst, arrs in zip((delta, new_m, new_v), outs):
        dst.update({n: a.reshape(w[n].shape) for n, a in zip(rest, arrs)})
    d_ = outs[0][1]

    bufs = {n: lax.empty((w[n].shape[0], 2, w[n].shape[1] // 2, w[n].shape[2]), F32) for n in BIG}
    for tag, behind in (("l1", delta["ada_w"]), ("l0_mlp", d_), ("l0_rest", d_)):
        red[tag].finish(behind, bufs)
    for n, full in zip(BIG, sibling_merge("rs_sibling_merge", [bufs[n] for n in BIG])):
        grad[n] = full.reshape(w[n].shape)
    for n in BIG:
        adam_big(n, True)

    return (loss, grad_x, *[grad[n] for n in WEIGHT_NAMES], *[delta[n] for n in WEIGHT_NAMES],
            *[new_m[n] for n in WEIGHT_NAMES], *[new_v[n] for n in WEIGHT_NAMES])
```

```python
import functools

import numpy as np
import jax
import jax.numpy as jnp
from jax import lax
from jax.experimental import pallas as pl
from jax.experimental.pallas import tpu as pltpu

F32 = jnp.float32
BF = jnp.bfloat16
HI = lax.Precision.HIGHEST
MESH = pl.DeviceIdType.MESH

D = 1024
L = 2048
LC = 256
SEQ = L + LC
NEX = 2
R = NEX * SEQ
TB = 256
WIDE = 512
BPE = SEQ // TB
NBLK = R // TB
GRID_W = 64
FF = 4 * D
EPS = 1e-6
ROPE_THETA = 10000.0
N_MOD = 6
EV_IN = 1792
OD_IN = 1440
OD_PAD = 1536
VMEM_LIMIT = 60 * 1024 * 1024

ADAM_LR = 0.001
ADAM_B1 = 0.9
ADAM_B2 = 0.999
ADAM_EPS = 1e-08
ADAM_WD = 0.01
ADAM_STEP = 10

NT = (((1,), (1,)), ((), ()))
TN = (((0,), (0,)), ((), ()))


def _cparams(sem=None):
    return pltpu.CompilerParams(dimension_semantics=sem, vmem_limit_bytes=VMEM_LIMIT)


@jax.custom_vjp
def _mm(a, b):
    return jnp.dot(a.astype(BF), b.astype(BF), preferred_element_type=F32)


def _mm_fwd(a, b):
    return _mm(a, b), (a, b)


def _mm_bwd(res, g):
    a, b = res
    gb = g.astype(BF)
    da = lax.dot_general(gb, b.astype(BF), NT, preferred_element_type=F32)
    db = lax.dot_general(a.astype(BF), gb, TN, preferred_element_type=F32)
    return da, db


_mm.defvjp(_mm_fwd, _mm_bwd)


@jax.custom_vjp
def _swap(x):
    n = x.shape[-1]
    ax = x.ndim - 1
    lane = lax.broadcasted_iota(jnp.int32, x.shape, ax)
    return jnp.where(lane % 2 == 0, pltpu.roll(x, n - 1, ax), pltpu.roll(x, 1, ax))


_swap.defvjp(lambda x: (_swap(x), None), lambda _, g: (_swap(g),))


def _rope(x, cos, sin):
    return x * cos + _swap(x) * sin


def _rmsn(x, g):
    return x * lax.rsqrt(jnp.mean(x * x, axis=-1, keepdims=True) + EPS) * g


def _split_dot(a, m):
    hi = a.astype(BF)
    lo = (a - hi.astype(F32)).astype(BF)
    return jnp.dot(hi, m, preferred_element_type=F32) + jnp.dot(lo, m, preferred_element_type=F32)


@jax.custom_vjp
def _group_mean(a, avg):
    return _split_dot(a, avg)


_group_mean.defvjp(lambda a, avg: (_split_dot(a, avg), avg),
                   lambda avg, g: (_split_dot(g, avg), jnp.zeros_like(avg)))


def _grmsn(x, g, avg):
    return x * lax.rsqrt(_group_mean(x * x, avg) + EPS) * g


def _modnorm(x, g, sh, sc):
    return _rmsn(x, g) * (1.0 + sc) + sh


def _gelu(x):
    return 0.5 * x * (1.0 + jnp.tanh(0.7978845608028654 * (x + 0.044715 * (x * x * x))))


def _silu(x):
    return x * jax.nn.sigmoid(x)


def _acc(ref, val, first):
    @pl.when(first)
    def _():
        ref[...] = val

    @pl.when(jnp.logical_not(first))
    def _():
        ref[...] += val


def _seg(i):
    return 2 * (i // BPE) + jnp.minimum(i % BPE, 1)


def _seg_first(i):
    return (i % BPE) <= 1


class _Either:
    def __init__(self, pick_first, first, second):
        self.pick_first, self.first, self.second = pick_first, first, second

    def __getitem__(self, idx):
        return jnp.where(self.pick_first, self.first[idx], self.second[idx])


def _rb_call(name, body, row_in=(), mod_in=(), pos_in=(), full_in=(), shift_in=(),
             row_out=(), seg_out=(), acc_out=(), scratch=(), after=None, col_in=(), rows=TB, idle=None):
    assert rows == TB or not (mod_in or pos_in or shift_in or seg_out or col_in)
    in_specs, args, pairs = [], [], []
    for a in row_in:
        if isinstance(a, tuple):
            pairs.append(len(args))
            in_specs.append(pl.BlockSpec((TB, a[0].shape[1]), lambda i: (i // BPE, 0)))
            in_specs.append(pl.BlockSpec(
                (TB, a[1].shape[1]), lambda i: ((i // BPE) * (L // TB) + jnp.maximum(i % BPE - 1, 0), 0)))
            args += list(a)
        else:
            in_specs.append(pl.BlockSpec((rows, a.shape[1]), lambda i: (i, 0)))
            args.append(a)
    for a in col_in:
        in_specs.append(pl.BlockSpec((a.shape[0], TB), lambda i: (0, i)))
        args.append(a)
    for tab, m in mod_in:
        in_specs.append(pl.BlockSpec((1, 1, D), lambda i, m=m: (_seg(i) * N_MOD + m, 0, 0)))
        args.append(tab)
    for a in pos_in:
        in_specs.append(pl.BlockSpec((TB, a.shape[1]), lambda i: (i % BPE, 0)))
        args.append(a)
    for a in full_in:
        in_specs.append(pl.BlockSpec(a.shape, lambda i, n=a.ndim: (0,) * n))
        args.append(a)
    for a, d in shift_in:
        in_specs.append(pl.BlockSpec((TB, a.shape[1]), lambda i, d=d: (jnp.clip(i + d, 0, NBLK - 1), 0)))
        args.append(a)
    n_in = len(args)
    if after is not None:
        in_specs.append(pl.BlockSpec(after.shape, lambda i, n=after.ndim: (0,) * n))
        args.append(after)
    out_specs, out_shape = [], []
    for w, dt, *lat in row_out:
        if lat:
            out_specs.append(pl.BlockSpec(
                (TB, w), lambda i: ((i // BPE) * (L // TB) + jnp.maximum(i % BPE - 1, 0), 0)))
            out_shape.append(jax.ShapeDtypeStruct((NEX * L, w), dt))
        else:
            out_specs.append(pl.BlockSpec((rows, w), lambda i: (i, 0)))
            out_shape.append(jax.ShapeDtypeStruct((R, w), dt))
    for w in seg_out:
        out_specs.append(pl.BlockSpec((1, 1, w), lambda i: (_seg(i), 0, 0)))
        out_shape.append(jax.ShapeDtypeStruct((4, 1, w), F32))
    for shp in acc_out:
        out_specs.append(pl.BlockSpec(shp, lambda i, n=len(shp): (0,) * n))
        out_shape.append(jax.ShapeDtypeStruct(shp, F32))

    def kern(*refs):
        i = pl.program_id(0)
        ins = list(refs[:n_in])
        for k in reversed(pairs):
            ins[k:k + 2] = [_Either((i % BPE) == 0, ins[k], ins[k + 1])]
        if idle is None:
            body(i, *ins, *refs[len(args):])
        else:
            @pl.when((i % BPE) >= 1)
            def _():
                body(i, *ins, *refs[len(args):])

            @pl.when((i % BPE) == 0)
            def _():
                idle(i, *ins, *refs[len(args):])

    sem = ("arbitrary",) if (seg_out or acc_out or any(len(r) > 2 for r in row_out)) else ("parallel",)
    return pl.pallas_call(kern, grid=(R // rows,), in_specs=in_specs, out_specs=out_specs, out_shape=out_shape,
                          scratch_shapes=list(scratch), compiler_params=_cparams(sem), name=name)(*args)


def modnorm_fwd(name, x, mods, g, m_sh, m_sc):
    def body(i, x_ref, sh_ref, sc_ref, g_ref, h_ref):
        h_ref[...] = _modnorm(x_ref[...], g_ref[...], sh_ref[0], sc_ref[0]).astype(BF)

    return _rb_call(name, body, row_in=(x,), mod_in=((mods, m_sh), (mods, m_sc)), full_in=(g,),
                    row_out=((D, BF),))[0]


def _gate_grads(dx, y_ref, gt_ref, dy_ref, dgt_ref, i):
    dy_ref[...] = (dx * gt_ref[0]).astype(BF)
    _acc(dgt_ref, jnp.sum(dx * y_ref[...].astype(F32), axis=0, keepdims=True)[None], _seg_first(i))


def modnorm_bwd(name, x, dh, dx_in, mods, g, m_sh, m_sc, gate=None, after=None, lat_only=False, last=False):
    def body(i, x_ref, dh_ref, dxin_ref, *rest):
        if gate:
            y_ref, sh_ref, sc_ref, gt_ref, g_ref, dx_ref, dy_ref, dgt_ref, dsh_ref, dsc_ref, dg_ref = rest
        else:
            sh_ref, sc_ref, g_ref, dx_ref, dsh_ref, dsc_ref, dg_ref = rest
        _, vjp = jax.vjp(_modnorm, x_ref[...], g_ref[...], sh_ref[0], sc_ref[0])
        dx, dg, dsh, dsc = vjp(dh_ref[...].astype(F32))
        dx = dxin_ref[...] + dx
        dx_ref[...] = dx
        if gate:
            _gate_grads(dx, y_ref, gt_ref, dy_ref, dgt_ref, i)
        _acc(dsh_ref, dsh[None], _seg_first(i))
        _acc(dsc_ref, dsc[None], _seg_first(i))
        _acc(dg_ref, dg, i == 0)

    def idle(i, x_ref, dh_ref, dxin_ref, y_ref, sh_ref, sc_ref, gt_ref, g_ref, dx_ref, dy_ref, dgt_ref, dsh_ref,
             dsc_ref, dg_ref):
        dx_ref[...] = dxin_ref[...]
        _zero(dy_ref, dgt_ref, dsh_ref, dsc_ref)
        _zero_at_start(i, dg_ref)

    if gate:
        y, gmods, m = gate
        return _rb_call(name, body, row_in=(x, dh, dx_in, y), mod_in=((mods, m_sh), (mods, m_sc), (gmods, m)),
                        full_in=(g,), row_out=((D, F32), (D, BF)), seg_out=(D, D, D), acc_out=((1, D),), after=after,
                        idle=idle if last else None)
    return _rb_call(name, body, row_in=(x, dh, dx_in), mod_in=((mods, m_sh), (mods, m_sc)), full_in=(g,),
                    row_out=((D, F32, "lat") if lat_only else (D, F32),), seg_out=(D, D), acc_out=((1, D),),
                    after=after)


def proj_in(name, h, w):
    n = w.shape[1]

    def body(i, h_ref, w_ref, o_ref):
        o_ref[...] = jnp.dot(h_ref[...], w_ref[...], preferred_element_type=F32).astype(BF)

    return _rb_call(name, body, row_in=(h,), full_in=(w,), row_out=((n, BF),), rows=WIDE)[0]


def _zero(*refs):
    for r in refs:
        r[...] = jnp.zeros_like(r)


def _zero_at_start(i, *refs):
    @pl.when(i == 0)
    def _():
        _zero(*refs)


def proj_out(name, a1, a2, w, x, mods, m_gate, g_next, m_sh, m_sc, last=False):
    k1 = a1.shape[1]

    def idle(i, a1_ref, a2_ref, x_ref, gt_ref, sh_ref, sc_ref, w_ref, g_ref, xo_ref, y_ref, h_ref):
        xo_ref[...] = x_ref[...]
        _zero(y_ref, h_ref)

    def body(i, a1_ref, a2_ref, x_ref, gt_ref, sh_ref, sc_ref, w_ref, g_ref, xo_ref, y_ref, h_ref):
        y = jnp.dot(a1_ref[...], w_ref[:k1, :], preferred_element_type=F32)
        y = y + jnp.dot(a2_ref[...], w_ref[k1:, :], preferred_element_type=F32)
        y_ref[...] = y.astype(BF)
        xn = x_ref[...] + gt_ref[0] * y
        xo_ref[...] = xn
        h_ref[...] = _modnorm(xn, g_ref[...], sh_ref[0], sc_ref[0]).astype(BF)

    return _rb_call(name, body, row_in=(a1, a2, x), mod_in=((mods, m_gate), (mods, m_sh), (mods, m_sc)),
                    full_in=(w, g_next), row_out=((D, F32), (D, BF), (D, BF)), idle=idle if last else None)


def mlp_up(name, h, w1):
    def body(i, h_ref, w_ref, a_ref, f_ref):
        hv = h_ref[...]
        for n in range(4):
            a = jnp.dot(hv, w_ref[n], preferred_element_type=F32)
            a_ref[:, n * D:(n + 1) * D] = a.astype(BF)
            r = jnp.maximum(a, 0.0)
            f_ref[:, n * D:(n + 1) * D] = (r * r).astype(BF)

    return _rb_call(name, body, row_in=(h,), full_in=(w1,), row_out=((FF, BF), (FF, BF)), rows=WIDE)


def mlp_down(name, f, w2, x, mods, m_gate, nxt=None):
    def body(i, f_ref, x_ref, gt_ref, *rest):
        if nxt:
            sh_ref, sc_ref, w_ref, g_ref, xo_ref, y_ref, h_ref = rest
        else:
            w_ref, xo_ref, y_ref = rest
        y = jnp.dot(f_ref[:, 0:D], w_ref[0], preferred_element_type=F32)
        for n in range(1, 4):
            y = y + jnp.dot(f_ref[:, n * D:(n + 1) * D], w_ref[n], preferred_element_type=F32)
        xn = x_ref[...] + gt_ref[0] * y
        y_ref[...] = y.astype(BF)
        xo_ref[...] = xn
        if nxt:
            h_ref[...] = _modnorm(xn, g_ref[...], sh_ref[0], sc_ref[0]).astype(BF)

    if nxt:
        return _rb_call(name, body, row_in=(f, x), mod_in=((mods, m_gate), (nxt[0], 0), (nxt[0], 1)),
                        full_in=(w2, nxt[1]), row_out=((D, F32), (D, BF), (D, BF)))
    def idle(i, f_ref, x_ref, gt_ref, w_ref, xo_ref, y_ref):
        xo_ref[...] = x_ref[...]
        _zero(y_ref)

    return _rb_call(name, body, row_in=(f, x), mod_in=((mods, m_gate),), full_in=(w2,),
                    row_out=((D, F32), (D, BF)), idle=idle)


def mm_nt(name, g, w, after=None):
    k = w.shape[0]

    def body(i, g_ref, w_ref, o_ref):
        o_ref[...] = lax.dot_general(g_ref[...], w_ref[...], NT, preferred_element_type=F32).astype(BF)

    return _rb_call(name, body, row_in=(g,), full_in=(w,), row_out=((k, BF),), after=after, rows=WIDE)[0]


def mlp_bwd_da(name, dy, w2, a, after=None):
    def body(i, dy_ref, a_ref, w_ref, da_ref):
        dyv = dy_ref[...]
        for n in range(4):
            df = lax.dot_general(dyv, w_ref[n], NT, preferred_element_type=F32)
            av = a_ref[:, n * D:(n + 1) * D].astype(F32)
            da_ref[:, n * D:(n + 1) * D] = (df * (2.0 * jnp.maximum(av, 0.0))).astype(BF)

    return _rb_call(name, body, row_in=(dy, a), full_in=(w2,), row_out=((FF, BF),), after=after, rows=WIDE)[0]


def mlp_bwd_dh(name, da, w1):
    def body(i, da_ref, w_ref, dh_ref):
        acc = lax.dot_general(da_ref[:, 0:D], w_ref[0], NT, preferred_element_type=F32)
        for n in range(1, 4):
            acc = acc + lax.dot_general(da_ref[:, n * D:(n + 1) * D], w_ref[n], NT, preferred_element_type=F32)
        dh_ref[...] = acc.astype(BF)

    return _rb_call(name, body, row_in=(da,), full_in=(w1,), row_out=((D, BF),), rows=WIDE)[0]


TN_ROWS = 2304


def mm_tn(name, a, g, tiles, th, tw):
    nt = len(tiles)
    acs = jnp.asarray([t[0] for t in tiles], jnp.int32)
    gcs = jnp.asarray([t[1] for t in tiles], jnp.int32)
    nr = R // TN_ROWS

    def kern(ac_ref, gc_ref, a_ref, g_ref, o_ref, acc_ref):
        r = pl.program_id(1)

        @pl.when(r == 0)
        def _():
            acc_ref[...] = jnp.zeros_like(acc_ref)

        acc_ref[...] += lax.dot_general(a_ref[...], g_ref[...], TN, preferred_element_type=F32)

        @pl.when(r == nr - 1)
        def _():
            o_ref[...] = acc_ref[...].astype(BF)

    grid_spec = pltpu.PrefetchScalarGridSpec(
        num_scalar_prefetch=2, grid=(nt, nr),
        in_specs=[pl.BlockSpec((TN_ROWS, th), lambda t, r, ac, gc: (r, ac[t])),
                  pl.BlockSpec((TN_ROWS, tw), lambda t, r, ac, gc: (r, gc[t]))],
        out_specs=pl.BlockSpec((None, th, tw), lambda t, r, ac, gc: (t, 0, 0)),
        scratch_shapes=[pltpu.VMEM((th, tw), F32)])
    return pl.pallas_call(kern, grid_spec=grid_spec, out_shape=jax.ShapeDtypeStruct((nt, th, tw), BF),
                          compiler_params=_cparams(("parallel", "arbitrary")), name=name)(acs, gcs, a, g)


def _even_tok(q, k, zus, zvs, gq, gk, gss, ws, bs, cq, sq, ck, sk, avg, lo, hi):
    avg2 = avg[:128, :128]
    qr = _rope(_grmsn(q, gq, avg), cq, sq) * GQA_SCALE
    kr = _rope(_grmsn(k, gk, avg2), ck, sk)
    ms = []
    for b in range(4):
        v = _grmsn(_gelu(zvs[b]), gss[b], avg2)
        sv = lo * (_mm(ws[2 * b], v) + bs[2 * b]) + hi * (_mm(ws[2 * b + 1], v) + bs[2 * b + 1])
        ms.append(_gelu(zus[b]) * sv)
    return qr, kr, ms


def _even_operands(p_ref, rs, gq_ref, gk_ref, gs_ref, w_ref, b_ref):
    return (p_ref[rs, 0:512].astype(F32), p_ref[rs, 512:640].astype(F32),
            [p_ref[rs, 768 + 128 * b:896 + 128 * b].astype(F32) for b in range(4)],
            [p_ref[rs, 1280 + 128 * b:1408 + 128 * b].astype(F32) for b in range(4)],
            gq_ref[...], gk_ref[...], [gs_ref[:, 128 * b:128 * b + 128] for b in range(4)],
            [w_ref[g] for g in range(8)], [b_ref[g] for g in range(8)])


def even_tok_fwd(p, cos, sin, gq, gk, gs, sgu_w, sgu_b, avg, masks):
    def body(i, p_ref, cos_ref, sin_ref, gq_ref, gk_ref, gs_ref, w_ref, b_ref, avg_ref, mk_ref, q_ref, kv_ref, m_ref):
        avgv, lo, hi = avg_ref[...], mk_ref[0, :, 0:128], mk_ref[1, :, 0:128]
        for c in range(2):
            rs = pl.ds(c * 128, 128)
            qr, kr, ms = _even_tok(*_even_operands(p_ref, rs, gq_ref, gk_ref, gs_ref, w_ref, b_ref),
                                   cos_ref[rs, :], sin_ref[rs, :], cos_ref[rs, 0:128], sin_ref[rs, 0:128],
                                   avgv, lo, hi)
            q_ref[rs, :] = qr.astype(BF)
            kv_ref[rs, 0:128] = kr.astype(BF)
            kv_ref[rs, 128:256] = p_ref[rs, 640:768]
            for b in range(4):
                m_ref[rs, 128 * b:128 * b + 128] = ms[b].astype(BF)

    return _rb_call("even_tok_fwd", body, row_in=(p,), pos_in=(cos, sin),
                    full_in=(gq, gk, gs, sgu_w, sgu_b, avg, masks), row_out=((512, BF), (256, BF), (512, BF)))


def even_tok_bwd(p, dq, dkvt, dcat, cos, sin, gq, gk, gs, sgu_w, sgu_b, avg, masks):
    def body(i, p_ref, dq_ref, dcat_ref, dkvt_ref, cos_ref, sin_ref, gq_ref, gk_ref, gs_ref, w_ref, b_ref,
             avg_ref, mk_ref, dp_ref, dgq_ref, dgk_ref, dgs_ref, dw_ref, db_ref):
        avgv, lo, hi = avg_ref[...], mk_ref[0, :, 0:128], mk_ref[1, :, 0:128]
        tot = None
        for c in range(2):
            rs = pl.ds(c * 128, 128)
            cq, sq, ck, sk = cos_ref[rs, :], sin_ref[rs, :], cos_ref[rs, 0:128], sin_ref[rs, 0:128]

            def f(q, k, zus, zvs, gq, gk, gss, ws, bs):
                return _even_tok(q, k, zus, zvs, gq, gk, gss, ws, bs, cq, sq, ck, sk, avgv, lo, hi)

            _, vjp = jax.vjp(f, *_even_operands(p_ref, rs, gq_ref, gk_ref, gs_ref, w_ref, b_ref))
            dk = dkvt_ref[0:128, c * 128:(c + 1) * 128].T
            dv = dkvt_ref[128:256, c * 128:(c + 1) * 128].T
            dms = [dcat_ref[rs, 512 + 128 * b:640 + 128 * b].astype(F32) for b in range(4)]
            d = vjp((dq_ref[rs, :].astype(F32), dk, dms))
            dp_ref[rs, 0:512] = d[0].astype(BF)
            dp_ref[rs, 512:640] = d[1].astype(BF)
            dp_ref[rs, 640:768] = dv.astype(BF)
            for b in range(4):
                dp_ref[rs, 768 + 128 * b:896 + 128 * b] = d[2][b].astype(BF)
                dp_ref[rs, 1280 + 128 * b:1408 + 128 * b] = d[3][b].astype(BF)
            part = [d[4], d[5]] + list(d[6]) + list(d[7]) + list(d[8])
            tot = part if tot is None else [x + y for x, y in zip(tot, part)]
        refs = ([dgq_ref, dgk_ref] + [dgs_ref.at[:, 128 * b:128 * b + 128] for b in range(4)]
                + [dw_ref.at[g] for g in range(8)] + [db_ref.at[g] for g in range(8)])
        for ref, val in zip(refs, tot):
            _acc(ref, val, i == 0)

    return _rb_call("even_tok_bwd", body, row_in=(p, dq, dcat), col_in=(dkvt,), pos_in=(cos, sin),
                    full_in=(gq, gk, gs, sgu_w, sgu_b, avg, masks), row_out=((EV_IN, BF),),
                    acc_out=((1, 512), (1, 128), (1, 512), (8, 128, 128), (8, 128, 1)))


MLA_SCALE = 96 ** -0.5
GQA_SCALE = 64 ** -0.5


def _odd_tok(cq, ckv, kr, za, zg, gq, gkv, wq, wkk, wkv, spread, cr, sr, ck, sk):
    cqn = _rmsn(cq, gq)
    q = _rope(_mm(cqn, wq), cr, sr) * MLA_SCALE
    ckn = _rmsn(ckv, gkv)
    k = _mm(ckn, wkk) + _mm(_rope(kr, ck, sk), spread)
    v = _mm(ckn, wkv)
    y = za * jax.nn.sigmoid(zg)
    return q, k, v, y


def odd_tok_fwd(p, cos, sin, gq, gkv, wq, wkk, wkv, spread):
    def body(i, p_ref, cos_ref, sin_ref, gq_ref, gkv_ref, wq_ref, wkk_ref, wkv_ref, sp_ref, q_ref, kv_ref, y_ref):
        q, k, v, y = _odd_tok(
            p_ref[:, 0:256].astype(F32), p_ref[:, 256:384].astype(F32), p_ref[:, 384:512].astype(F32),
            p_ref[:, 512:1024].astype(F32), p_ref[:, 1024:1536].astype(F32),
            gq_ref[...], gkv_ref[...], wq_ref[...], wkk_ref[...], wkv_ref[...], sp_ref[...],
            cos_ref[:, 0:768], sin_ref[:, 0:768], cos_ref[:, 768:896], sin_ref[:, 768:896])
        q_ref[...] = q.astype(BF)
        kv_ref[:, 0:768] = k.astype(BF)
        kv_ref[:, 768:1280] = v.astype(BF)
        y_ref[...] = y.astype(BF)

    return _rb_call("odd_tok_fwd", body, row_in=(p,), pos_in=(cos, sin), full_in=(gq, gkv, wq, wkk, wkv, spread),
                    row_out=((768, BF), (1280, BF), (512, BF)))


def odd_tok_bwd(p, dq, dkvt, dy, cos, sin, gq, gkv, wq, wkk, wkv, spread):
    def body(i, p_ref, dq_ref, dy_ref, dkvt_ref, cos_ref, sin_ref, gq_ref, gkv_ref, wq_ref, wkk_ref, wkv_ref, sp_ref,
             dp_ref, dgq_ref, dgkv_ref, dwq_ref, dwkk_ref, dwkv_ref):
        cr, sr, ck, sk = cos_ref[:, 0:768], sin_ref[:, 0:768], cos_ref[:, 768:896], sin_ref[:, 768:896]
        spread_v = sp_ref[...]

        def f(cq, ckv, kr, za, zg, gq, gkv, wq, wkk, wkv):
            return _odd_tok(cq, ckv, kr, za, zg, gq, gkv, wq, wkk, wkv, spread_v, cr, sr, ck, sk)

        _, vjp = jax.vjp(f, p_ref[:, 0:256].astype(F32), p_ref[:, 256:384].astype(F32),
                         p_ref[:, 384:512].astype(F32), p_ref[:, 512:1024].astype(F32),
                         p_ref[:, 1024:1536].astype(F32), gq_ref[...], gkv_ref[...], wq_ref[...],
                         wkk_ref[...], wkv_ref[...])
        d = vjp((dq_ref[...].astype(F32), dkvt_ref[0:768, :].T, dkvt_ref[768:1280, :].T, dy_ref[...].astype(F32)))
        dp_ref[:, 0:256] = d[0].astype(BF)
        dp_ref[:, 256:384] = d[1].astype(BF)
        dp_ref[:, 384:512] = d[2].astype(BF)
        dp_ref[:, 512:1024] = d[3].astype(BF)
        dp_ref[:, 1024:1536] = d[4].astype(BF)
        for ref, val in zip((dgq_ref, dgkv_ref, dwq_ref, dwkk_ref, dwkv_ref), d[5:]):
            _acc(ref, val, i == 0)

    return _rb_call("odd_tok_bwd", body, row_in=(p, dq, dy), col_in=(dkvt,), pos_in=(cos, sin),
                    full_in=(gq, gkv, wq, wkk, wkv, spread), row_out=((OD_PAD, BF),),
                    acc_out=((1, 256), (1, 128), (256, 768), (128, 768), (128, 512)))


GQA_HEADS = [(64 * h, 64 * (h // 4), 64, 128 + 64 * (h // 4)) for h in range(8)]
MLA_HEADS = [(96 * h, 96 * h, 96, 768 + 64 * h) for h in range(8)]


def _by_block(j, run):
    @pl.when(j == 0)
    def _():
        run(LC)

    @pl.when(j > 0)
    def _():
        run(SEQ)


def attn_fwd(name, q, kv, heads):
    qw, kvw = q.shape[1], kv.shape[1]

    def kern(q_ref, kv_ref, o_ref, lse_ref):
        def run(nk):
            for h, (qo, ko, w, vo) in enumerate(heads):
                s = lax.dot_general(q_ref[:, qo:qo + w], kv_ref[0:nk, ko:ko + w], NT, preferred_element_type=F32)
                m = jnp.max(s, axis=-1, keepdims=True)
                p = jnp.exp(s - m)
                l = jnp.sum(p, axis=-1, keepdims=True)
                o = jnp.dot(p.astype(BF), kv_ref[0:nk, vo:vo + 64], preferred_element_type=F32) / l
                o_ref[:, 64 * h:64 * h + 64] = o.astype(BF)
                lse_ref[:, h:h + 1] = m + jnp.log(l)

        _by_block(pl.program_id(1), run)

    return pl.pallas_call(
        kern, grid=(NEX, BPE),
        in_specs=[pl.BlockSpec((TB, qw), lambda e, j: (e * BPE + j, 0)),
                  pl.BlockSpec((SEQ, kvw), lambda e, j: (e, 0))],
        out_specs=[pl.BlockSpec((TB, 512), lambda e, j: (e * BPE + j, 0)),
                   pl.BlockSpec((TB, 8), lambda e, j: (e * BPE + j, 0))],
        out_shape=[jax.ShapeDtypeStruct((R, 512), BF), jax.ShapeDtypeStruct((R, 8), F32)],
        compiler_params=_cparams(("parallel", "arbitrary")), name=name)(q, kv)


def attn_bwd(name, q, kv, o, dcat, lse, heads):
    qw, kvw = q.shape[1], kv.shape[1]

    def kern(q_ref, kv_ref, o_ref, do_ref, lse_ref, dq_ref, dkvt_ref):
        j = pl.program_id(1)

        @pl.when(j == 0)
        def _():
            dkvt_ref[...] = jnp.zeros_like(dkvt_ref)

        def run(nk):
            for h, (qo, ko, w, vo) in enumerate(heads):
                qh = q_ref[:, qo:qo + w]
                kh = kv_ref[0:nk, ko:ko + w]
                s = lax.dot_general(qh, kh, NT, preferred_element_type=F32)
                p = jnp.exp(s - lse_ref[:, h:h + 1])
                do = do_ref[:, 64 * h:64 * h + 64]
                dsum = jnp.sum(do.astype(F32) * o_ref[:, 64 * h:64 * h + 64].astype(F32), axis=-1, keepdims=True)
                dp = lax.dot_general(do, kv_ref[0:nk, vo:vo + 64], NT, preferred_element_type=F32)
                ds = (p * (dp - dsum)).astype(BF)
                dkvt_ref[vo:vo + 64, 0:nk] += lax.dot_general(do, p.astype(BF), TN, preferred_element_type=F32)
                dq_ref[:, qo:qo + w] = jnp.dot(ds, kh, preferred_element_type=F32).astype(BF)
                dkvt_ref[ko:ko + w, 0:nk] += lax.dot_general(qh, ds, TN, preferred_element_type=F32)

        _by_block(j, run)

    return pl.pallas_call(
        kern, grid=(NEX, BPE),
        in_specs=[pl.BlockSpec((TB, qw), lambda e, j: (e * BPE + j, 0)),
                  pl.BlockSpec((SEQ, kvw), lambda e, j: (e, 0)),
                  pl.BlockSpec((TB, 512), lambda e, j: (e * BPE + j, 0)),
                  pl.BlockSpec((TB, 512), lambda e, j: (e * BPE + j, 0)),
                  pl.BlockSpec((TB, 8), lambda e, j: (e * BPE + j, 0))],
        out_specs=[pl.BlockSpec((TB, qw), lambda e, j: (e * BPE + j, 0)),
                   pl.BlockSpec((kvw, SEQ), lambda e, j: (0, e))],
        out_shape=[jax.ShapeDtypeStruct((R, qw), BF), jax.ShapeDtypeStruct((kvw, R), F32)],
        compiler_params=_cparams(("parallel", "arbitrary")), name=name)(q, kv, o, dcat, lse)


HALO = 16
CONV_K = 31


def _fill_ext(ext_ref, prev_ref, cur_ref, next_ref, i):
    j = i % BPE
    has_prev = (j >= 2).astype(F32)
    has_next = jnp.logical_and(j >= 1, j <= BPE - 2).astype(F32)
    ext_ref[0:HALO, :] = prev_ref[TB - HALO:TB, :].astype(F32) * has_prev
    ext_ref[HALO:HALO + TB, :] = cur_ref[...].astype(F32)
    ext_ref[HALO + TB:2 * HALO + TB, :] = next_ref[0:HALO, :].astype(F32) * has_next


PHASE_ROWS = TB + 24


def _phases(ext_ref, ph_ref):
    for r in range(8):
        ph_ref[r] = ext_ref[r:r + PHASE_ROWS, :]


def _window(ph_ref, off):
    return ph_ref[off % 8, 8 * (off // 8):8 * (off // 8) + TB, :]


def _ln_silu(z, g, b):
    mu = jnp.mean(z, axis=-1, keepdims=True)
    zc = z - mu
    var = jnp.mean(zc * zc, axis=-1, keepdims=True)
    return _silu(zc * lax.rsqrt(var + EPS) * g + b)


def conf_fwd(y, cw, cb, lg, lb):
    def body(i, cur_ref, cw_ref, cb_ref, lg_ref, lb_ref, prev_ref, next_ref, z_ref, c_ref, ext_ref, ph_ref):
        _fill_ext(ext_ref, prev_ref, cur_ref, next_ref, i)
        _phases(ext_ref, ph_ref)
        acc = _window(ph_ref, 1) * cw_ref[0:1, :]
        for k in range(1, CONV_K):
            acc = acc + _window(ph_ref, k + 1) * cw_ref[k:k + 1, :]
        z = acc + cb_ref[...]
        z_ref[...] = z.astype(BF)
        c_ref[...] = _ln_silu(z, lg_ref[...], lb_ref[...]).astype(BF)

    def idle(i, cur_ref, cw_ref, cb_ref, lg_ref, lb_ref, prev_ref, next_ref, z_ref, c_ref, ext_ref, ph_ref):
        _zero(z_ref, c_ref)

    return _rb_call("conf_fwd", body, row_in=(y,), full_in=(cw, cb, lg, lb), shift_in=((y, -1), (y, 1)),
                    row_out=((512, BF), (512, BF)), idle=idle,
                    scratch=(pltpu.VMEM((TB + 2 * HALO, 512), F32), pltpu.VMEM((8, PHASE_ROWS, 512), F32)))


def conf_bwd_ln(z, dcat, lg, lb):
    def body(i, z_ref, dcat_ref, lg_ref, lb_ref, dz_ref, dlg_ref, dlb_ref, dcb_ref):
        _, vjp = jax.vjp(_ln_silu, z_ref[...].astype(F32), lg_ref[...], lb_ref[...])
        dz, dlg, dlb = vjp(dcat_ref[:, 512:1024].astype(F32))
        dz_ref[...] = dz.astype(BF)
        _acc(dlg_ref, dlg, i == 0)
        _acc(dlb_ref, dlb, i == 0)
        _acc(dcb_ref, jnp.sum(dz, axis=0, keepdims=True), i == 0)

    def idle(i, z_ref, dcat_ref, lg_ref, lb_ref, dz_ref, dlg_ref, dlb_ref, dcb_ref):
        _zero(dz_ref)
        _zero_at_start(i, dlg_ref, dlb_ref, dcb_ref)

    return _rb_call("conf_bwd_ln", body, row_in=(z, dcat), full_in=(lg, lb), row_out=((512, BF),),
                    acc_out=((1, 512), (1, 512), (1, 512)), idle=idle)


def conf_bwd_conv(y, dz, cw):
    def body(i, y_ref, dz_ref, cw_ref, yp_ref, yn_ref, dzp_ref, dzn_ref, dy_ref, dcw_ref, ext_ref, phy_ref, phd_ref):
        _fill_ext(ext_ref, yp_ref, y_ref, yn_ref, i)
        _phases(ext_ref, phy_ref)
        _fill_ext(ext_ref, dzp_ref, dz_ref, dzn_ref, i)
        _phases(ext_ref, phd_ref)
        dzv = dz_ref[...].astype(F32)

        @pl.when(i == 0)
        def _():
            dcw_ref[...] = jnp.zeros_like(dcw_ref)

        acc = None
        for k in range(CONV_K):
            t = _window(phd_ref, CONV_K - k) * cw_ref[k:k + 1, :]
            acc = t if acc is None else acc + t
            dcw_ref[k:k + 1, :] += jnp.sum(dzv * _window(phy_ref, k + 1), axis=0, keepdims=True)
        dy_ref[...] = acc.astype(BF)

    def idle(i, y_ref, dz_ref, cw_ref, yp_ref, yn_ref, dzp_ref, dzn_ref, dy_ref, dcw_ref, ext_ref, phy_ref, phd_ref):
        _zero(dy_ref)
        _zero_at_start(i, dcw_ref)

    return _rb_call("conf_bwd_conv", body, row_in=(y, dz), full_in=(cw,), idle=idle,
                    shift_in=((y, -1), (y, 1), (dz, -1), (dz, 1)), row_out=((512, BF),), acc_out=((32, 512),),
                    scratch=(pltpu.VMEM((TB + 2 * HALO, 512), F32), pltpu.VMEM((8, PHASE_ROWS, 512), F32),
                             pltpu.VMEM((8, PHASE_ROWS, 512), F32)))


def final_loss(x, target, fg, y, mods, m_gate):
    lpb = L // TB

    def kern(x_ref, t_ref, g_ref, y_ref, gt_ref, dx_ref, dy_ref, dgt_ref, loss_ref, dg_ref):
        i = pl.program_id(0)

        @pl.when((i % BPE) == 0)
        def _():
            _zero(dx_ref, dy_ref, dgt_ref)
            _zero_at_start(i, loss_ref, dg_ref)

        @pl.when((i % BPE) >= 1)
        def _():
            tv = t_ref[...]

            def f(x, g):
                err = _rmsn(x, g) - tv
                rowsum = jnp.sum(err * err, axis=-1, keepdims=True)
                return jnp.sum(rowsum, axis=0, keepdims=True) * (0.5 / D)

            lv, vjp = jax.vjp(f, x_ref[...], g_ref[...])
            dx, dg = vjp(jnp.ones((1, 1), F32))
            dx_ref[...] = dx
            _gate_grads(dx, y_ref, gt_ref, dy_ref, dgt_ref, i)
            loss_ref[...] += jnp.zeros((8, 128), F32) + lv
            dg_ref[...] += dg

    row = pl.BlockSpec((TB, D), lambda i: (i, 0))
    return pl.pallas_call(
        kern, grid=(NBLK,),
        in_specs=[row, pl.BlockSpec((TB, D), lambda i: ((i // BPE) * lpb + jnp.maximum(i % BPE - 1, 0), 0)),
                  pl.BlockSpec((1, D), lambda i: (0, 0)), row,
                  pl.BlockSpec((1, 1, D), lambda i: (_seg(i) * N_MOD + m_gate, 0, 0))],
        out_specs=[row, row, pl.BlockSpec((1, 1, D), lambda i: (_seg(i), 0, 0)),
                   pl.BlockSpec((8, 128), lambda i: (0, 0)), pl.BlockSpec((1, D), lambda i: (0, 0))],
        out_shape=[jax.ShapeDtypeStruct((R, D), F32), jax.ShapeDtypeStruct((R, D), BF),
                   jax.ShapeDtypeStruct((4, 1, D), F32), jax.ShapeDtypeStruct((8, 128), F32),
                   jax.ShapeDtypeStruct((1, D), F32)],
        compiler_params=_cparams(("arbitrary",)), name="final_loss")(x, target, fg, y, mods)


NC = 24


def mods_fwd(call, ada_w, ada_b):
    cols = ada_w.shape[2]

    def kern(c_ref, w_ref, b_ref, o_ref):
        o_ref[...] = jnp.dot(_silu(c_ref[...]), w_ref[...], precision=HI, preferred_element_type=F32) + b_ref[...]

    return pl.pallas_call(
        kern, grid=(2,),
        in_specs=[pl.BlockSpec((NC, D), lambda l: (0, 0)), pl.BlockSpec((None, D, cols), lambda l: (l, 0, 0)),
                  pl.BlockSpec((None, 1, cols), lambda l: (l, 0, 0))],
        out_specs=pl.BlockSpec((None, NC, cols), lambda l: (l, 0, 0)),
        out_shape=jax.ShapeDtypeStruct((2, NC, cols), F32),
        compiler_params=_cparams(("parallel",)), name="mods_fwd")(call, ada_w, ada_b)


def ada_bwd(call, ada_w, dm):
    cols = ada_w.shape[2]

    def kern(c_ref, w_ref, dm_ref, gw_ref, dc_ref):
        l = pl.program_id(0)
        gw_ref[...] = lax.dot_general(_silu(c_ref[...]), dm_ref[...], TN, precision=HI, preferred_element_type=F32)
        part = lax.dot_general(dm_ref[16:24, :], w_ref[...], NT, precision=HI, preferred_element_type=F32)
        cc = c_ref[16:17, :]
        sg = jax.nn.sigmoid(cc)
        _acc(dc_ref, part * (sg * (1.0 + cc * (1.0 - sg))), l == 0)

    return pl.pallas_call(
        kern, grid=(2,),
        in_specs=[pl.BlockSpec((NC, D), lambda l: (0, 0)), pl.BlockSpec((None, D, cols), lambda l: (l, 0, 0)),
                  pl.BlockSpec((None, NC, cols), lambda l: (l, 0, 0))],
        out_specs=[pl.BlockSpec((None, D, cols), lambda l: (l, 0, 0)), pl.BlockSpec((8, D), lambda l: (0, 0))],
        out_shape=[jax.ShapeDtypeStruct((2, D, cols), F32), jax.ShapeDtypeStruct((8, D), F32)],
        compiler_params=_cparams(("arbitrary",)), name="ada_bwd")(call, ada_w, dm)


def sum_lead(name, a, after=None):
    n, r, c = a.shape
    tr = r
    for cand in (512, 256, 128, 64, 32, 16, 8):
        if r % cand == 0 and cand * c * 4 * n <= 8 * 1024 * 1024:
            tr = cand
            break
    extra = [] if after is None else [after]

    def kern(a_ref, *rest):
        acc = a_ref[0].astype(F32)
        for k in range(1, n):
            acc = acc + a_ref[k].astype(F32)
        rest[-1][...] = acc

    return pl.pallas_call(
        kern, grid=(r // tr,),
        in_specs=[pl.BlockSpec((n, tr, c), lambda i: (0, i, 0))]
        + [pl.BlockSpec(e.shape, lambda i, k=e.ndim: (0,) * k) for e in extra],
        out_specs=pl.BlockSpec((tr, c), lambda i: (i, 0)), out_shape=jax.ShapeDtypeStruct((r, c), F32),
        compiler_params=_cparams(("parallel",)), name=name)(a, *extra)


def add_pairs(name, hs, got, half):
    _, _, r, c = hs.shape

    def kern(half_ref, a_ref, b_ref, o_ref):
        o_ref[...] = (a_ref[...].astype(F32) + b_ref[...].astype(F32)).astype(BF)

    spec = pl.BlockSpec((None, r, c), lambda j, h: (j, 0, 0))
    grid_spec = pltpu.PrefetchScalarGridSpec(
        num_scalar_prefetch=1, grid=(4,),
        in_specs=[pl.BlockSpec((None, None, r, c), lambda j, h: (h[0], j, 0, 0)), spec], out_specs=spec)
    return pl.pallas_call(kern, grid_spec=grid_spec, out_shape=jax.ShapeDtypeStruct(got.shape, BF),
                          compiler_params=_cparams(("parallel",)), name=name)(half, hs, got)


def sum_slabs(name, land, own, where, full, lead):
    _, r, c = land.shape
    tr = r
    for cand in (512, 256, 128, 64, 32, 16):
        if r % cand == 0 and cand * c * 16 <= 4 * 1024 * 1024:
            tr = cand
            break

    def kern(where_ref, full_ref, land_ref, own_ref, o_ref):
        me = where_ref[0]
        acc = None
        for k in range(4):
            t = jnp.where(me == k, own_ref[k], land_ref[k]).astype(F32)
            acc = t if acc is None else acc + t
        o_ref[...] = acc

    spec = pl.BlockSpec((4, tr, c), lambda i, m: (0, i, 0))
    grid_spec = pltpu.PrefetchScalarGridSpec(
        num_scalar_prefetch=1, grid=(r // tr,), in_specs=[pl.BlockSpec(memory_space=pl.ANY), spec, spec],
        out_specs=pl.BlockSpec((None, None, tr, c), lambda i, m: (lead, m[1], i, 0)))
    return pl.pallas_call(kern, grid_spec=grid_spec, out_shape=jax.ShapeDtypeStruct(full.shape, F32),
                          input_output_aliases={1: 0}, compiler_params=_cparams(("parallel",)),
                          name=name)(where, full, land, own)


def adamw(name, w, g, m, v, again=False):
    r, c = w.shape
    tr = r
    for cand in (512, 256, 128, 64, 32, 16, 8):
        if r % cand == 0 and cand * c * 4 <= 2 * 1024 * 1024:
            tr = cand
            break
    c1 = 1.0 / (1.0 - ADAM_B1 ** ADAM_STEP)
    c2 = 1.0 / (1.0 - ADAM_B2 ** ADAM_STEP)

    def kern(w_ref, g_ref, m_ref, v_ref, d_ref, mo_ref, vo_ref, *go_ref):
        gv = g_ref[...]
        mn = ADAM_B1 * m_ref[...] + (1.0 - ADAM_B1) * gv
        vn = ADAM_B2 * v_ref[...] + (1.0 - ADAM_B2) * (gv * gv)
        d_ref[...] = -ADAM_LR * ((mn * c1) / (jnp.sqrt(vn * c2) + ADAM_EPS) + ADAM_WD * w_ref[...])
        mo_ref[...] = mn
        vo_ref[...] = vn
        if again:
            go_ref[0][...] = gv

    spec = pl.BlockSpec((tr, c), lambda i: (i, 0))
    shp = jax.ShapeDtypeStruct((r, c), F32)
    n_out = 4 if again else 3
    return pl.pallas_call(kern, grid=(r // tr,), in_specs=[spec] * 4, out_specs=[spec] * n_out,
                          out_shape=[shp] * n_out, compiler_params=_cparams(("parallel",)), name=name)(w, g, m, v)


def adamw_many(name, ws, gs, ms, vs):
    n = len(ws)
    c1 = 1.0 / (1.0 - ADAM_B1 ** ADAM_STEP)
    c2 = 1.0 / (1.0 - ADAM_B2 ** ADAM_STEP)

    def kern(*refs):
        w, g, m, v, d, mo, vo = (refs[k * n:(k + 1) * n] for k in range(7))
        for k in range(n):
            gv = g[k][...]
            mn = ADAM_B1 * m[k][...] + (1.0 - ADAM_B1) * gv
            vn = ADAM_B2 * v[k][...] + (1.0 - ADAM_B2) * (gv * gv)
            d[k][...] = -ADAM_LR * ((mn * c1) / (jnp.sqrt(vn * c2) + ADAM_EPS) + ADAM_WD * w[k][...])
            mo[k][...] = mn
            vo[k][...] = vn

    out = pl.pallas_call(kern, out_shape=[jax.ShapeDtypeStruct(a.shape, F32) for a in ws] * 3,
                         compiler_params=pltpu.CompilerParams(vmem_limit_bytes=VMEM_LIMIT),
                         name=name)(*ws, *gs, *ms, *vs)
    return out[:n], out[n:2 * n], out[2 * n:]


def all_gather8(name, xs, after=None):
    m_per, n = xs.shape
    extra = [] if after is None else [after]

    def body(x_ref, *rest):
        out_ref, send_sems, recv_sems, local_sem = rest[len(extra):]
        x, y, c = lax.axis_index("x"), lax.axis_index("y"), lax.axis_index("c")
        me, sibling = (x, y, c), (x, y, 1 - c)
        chips = [(1 - x, y), (x, 1 - y), (1 - x, 1 - y)]

        def rows(px, py, pc):
            return out_ref.at[pl.ds((4 * px + 2 * py + pc) * m_per, m_per), :]

        def copy(k, block, to, src=None):
            return pltpu.make_async_remote_copy(
                src_ref=rows(*block) if src is None else src, dst_ref=rows(*block),
                send_sem=send_sems.at[k], recv_sem=recv_sems.at[k], device_id=to, device_id_type=MESH)

        mine = pltpu.make_async_copy(x_ref, rows(*me), local_sem)
        mine.start()
        first = [copy(0, me, sibling, src=x_ref)]
        first += [copy(1 + j, me, (*chip, c), src=x_ref) for j, chip in enumerate(chips)]
        for cp in first:
            cp.start()
        passed = [copy(4 + j, (*chip, c), sibling) for j, chip in enumerate(chips)]
        for j, chip in enumerate(chips):
            copy(1 + j, (*chip, c), me).wait_recv()
            passed[j].start()
        copy(0, sibling, me).wait_recv()
        for j, chip in enumerate(chips):
            copy(4 + j, (*chip, 1 - c), me).wait_recv()
        for cp in first + passed:
            cp.wait_send()
        mine.wait()

    return pl.pallas_call(
        body, out_shape=jax.ShapeDtypeStruct((8 * m_per, n), xs.dtype),
        in_specs=[pl.BlockSpec(memory_space=pltpu.VMEM)] * (1 + len(extra)),
        out_specs=pl.BlockSpec(memory_space=pltpu.VMEM),
        scratch_shapes=[pltpu.SemaphoreType.DMA((7,)), pltpu.SemaphoreType.DMA((7,)), pltpu.SemaphoreType.DMA],
        compiler_params=pltpu.CompilerParams(vmem_limit_bytes=VMEM_LIMIT), name=name)(xs, *extra)


def sibling_merge(name, fulls):
    n = len(fulls)
    slots = [(a, l) for a in range(n) for l in range(fulls[a].shape[0])]

    def body(*refs):
        buf = refs[n:2 * n]
        send_sems, recv_sems = refs[2 * n], refs[2 * n + 1]
        c = lax.axis_index("c")
        sibling = (lax.axis_index("x"), lax.axis_index("y"), 1 - c)
        sends, recvs = [], []
        for k, (a, l) in enumerate(slots):
            kw = dict(send_sem=send_sems.at[k], recv_sem=recv_sems.at[k], device_id=sibling, device_id_type=MESH)
            sends.append(pltpu.make_async_remote_copy(src_ref=buf[a].at[l, c], dst_ref=buf[a].at[l, c], **kw))
            recvs.append(pltpu.make_async_remote_copy(src_ref=buf[a].at[l, c], dst_ref=buf[a].at[l, 1 - c], **kw))
        for cp in sends:
            cp.start()
        for cp in recvs:
            cp.wait_recv()
        for cp in sends:
            cp.wait_send()

    anyspec = pl.BlockSpec(memory_space=pl.ANY)
    return pl.pallas_call(
        body, out_shape=[jax.ShapeDtypeStruct(s.shape, s.dtype) for s in fulls],
        in_specs=[anyspec] * n, out_specs=[anyspec] * n, input_output_aliases={a: a for a in range(n)},
        scratch_shapes=[pltpu.SemaphoreType.DMA((len(slots),)), pltpu.SemaphoreType.DMA((len(slots),))],
        name=name)(*fulls)


def place_own(name, land, src, chip):
    c = src.shape[-1]
    r = src.size // c
    tr = r
    for cand in (1024, 512, 256, 128, 64, 32, 16):
        if r % cand == 0 and cand * c * 2 <= 2 * 1024 * 1024:
            tr = cand
            break

    def kern(chip_ref, land_ref, src_ref, out_ref):
        out_ref[...] = src_ref[...]

    grid_spec = pltpu.PrefetchScalarGridSpec(
        num_scalar_prefetch=1, grid=(r // tr,),
        in_specs=[pl.BlockSpec(memory_space=pl.ANY), pl.BlockSpec((tr, c), lambda i, m: (i, 0))],
        out_specs=pl.BlockSpec((None, tr, c), lambda i, m: (m[0], i, 0)))
    out = pl.pallas_call(kern, grid_spec=grid_spec, out_shape=jax.ShapeDtypeStruct((4, r, c), land.dtype),
                         input_output_aliases={1: 0}, compiler_params=_cparams(("parallel",)),
                         name=name)(chip, land.reshape(4, r, c), src.reshape(r, c))
    return out.reshape(land.shape)


def _half_copies(src, land, send_sems, recv_sems):
    c = lax.axis_index("c")
    sibling = (lax.axis_index("x"), lax.axis_index("y"), 1 - c)
    pairs = []
    for a in range(len(src)):
        cp = pltpu.make_async_remote_copy(src_ref=src[a].at[1 - c], dst_ref=land[a], send_sem=send_sems.at[a],
                                          recv_sem=recv_sems.at[a], device_id=sibling, device_id_type=MESH)
        pairs.append((cp, cp))
    return pairs


def _chip_copies(src, land, send_sems, recv_sems, scatter):
    x, y, c = lax.axis_index("x"), lax.axis_index("y"), lax.axis_index("c")
    me = 2 * x + y
    pairs = []
    for a in range(len(src)):
        for j, (px, py) in enumerate([(1 - x, y), (x, 1 - y), (1 - x, 1 - y)]):
            to = 2 * px + py
            out = src[a].at[to] if scatter else src[a]
            kw = dict(send_sem=send_sems.at[3 * a + j], recv_sem=recv_sems.at[3 * a + j], device_id=(px, py, c),
                      device_id_type=MESH)
            pairs.append((pltpu.make_async_remote_copy(src_ref=out, dst_ref=land[a].at[me], **kw),
                          pltpu.make_async_remote_copy(src_ref=out, dst_ref=land[a].at[to], **kw)))
    return pairs


_HBM = pl.BlockSpec(memory_space=pltpu.HBM)
_SEM = pl.BlockSpec(memory_space=pltpu.SEMAPHORE)


GATHER = (functools.partial(_chip_copies, scatter=False), 3)
SCATTER = (functools.partial(_chip_copies, scatter=True), 3)
TO_SIBLING = (_half_copies, 1)


def exchange_start(name, groups, plan):
    copies, per = plan
    sizes = [len(s) for s, _ in groups]
    flat = [a for s, l in groups for a in list(s) + list(l)]
    ng = len(groups)

    def body(*refs):
        ins, outs = refs[:len(flat)], refs[len(flat):]
        off = 0
        for g, n in enumerate(sizes):
            src, land = ins[off:off + n], ins[off + n:off + 2 * n]
            off += 2 * n
            for send, _ in copies(src, land, outs[2 * g], outs[2 * g + 1]):
                send.start()
        outs[-1][...] = jnp.zeros_like(outs[-1])

    out_shape = []
    for n in sizes:
        out_shape += [pltpu.SemaphoreType.DMA((per * n,)), pltpu.SemaphoreType.DMA((per * n,))]
    out_shape += [pltpu.HBM(a.shape, a.dtype) for a in flat] + [jax.ShapeDtypeStruct((8, 128), F32)]
    res = pl.pallas_call(
        body, out_shape=tuple(out_shape), in_specs=[_HBM] * len(flat),
        out_specs=tuple([_SEM] * (2 * ng) + [_HBM] * len(flat) + [pl.BlockSpec(memory_space=pltpu.VMEM)]),
        input_output_aliases={k: 2 * ng + k for k in range(len(flat))},
        compiler_params=pltpu.CompilerParams(has_side_effects=pltpu.SideEffectType.DATAFLOW_SIDE_EFFECTING),
        name=name)(*[pltpu.with_memory_space_constraint(a, pltpu.HBM) for a in flat])
    handles, off = [], 2 * ng
    for g, n in enumerate(sizes):
        handles.append((res[2 * g], res[2 * g + 1], list(res[off:off + n]), list(res[off + n:off + 2 * n])))
        off += 2 * n
    return handles, res[-1]


def exchange_wait(name, handle, after, plan):
    send_sems, recv_sems, srcs, lands = handle
    n = len(srcs)

    def body(*refs):
        src, land = refs[:n], refs[n:2 * n]
        for send, recv in plan[0](src, land, refs[2 * n], refs[2 * n + 1]):
            send.wait_send()
            recv.wait_recv()

    res = pl.pallas_call(
        body, out_shape=tuple(pltpu.HBM(a.shape, a.dtype) for a in srcs + lands),
        in_specs=[_HBM] * (2 * n) + [_SEM, _SEM, pl.BlockSpec(memory_space=pl.ANY)],
        out_specs=tuple([_HBM] * (2 * n)), input_output_aliases={k: k for k in range(2 * n)},
        compiler_params=pltpu.CompilerParams(has_side_effects=pltpu.SideEffectType.DATAFLOW_SIDE_EFFECTING),
        name=name)(*srcs, *lands, send_sems, recv_sems, after)
    return list(res[:n]), list(res[n:])


def _rope_tables(d_rot, reps):
    rows = L // GRID_W
    row = np.repeat(np.arange(rows), GRID_W).astype(np.float32)
    col = np.tile(np.arange(GRID_W), rows).astype(np.float32)
    d_axis = d_rot // 2
    inv = (ROPE_THETA ** (-np.arange(0, d_axis, 2, dtype=np.float32) / d_axis)).astype(np.float32)
    ang = np.concatenate([row[:, None] * inv, col[:, None] * inv], axis=-1).astype(np.float32)
    cos, sin = np.cos(ang).astype(np.float32), np.sin(ang).astype(np.float32)
    c = np.repeat(cos, 2, axis=-1)
    s = np.stack([-sin, sin], axis=-1).reshape(L, d_rot)
    c = np.concatenate([np.ones((LC, d_rot), np.float32), c], axis=0)
    s = np.concatenate([np.zeros((LC, d_rot), np.float32), s], axis=0)
    return np.tile(c, (1, reps)), np.tile(s, (1, reps))


def _group_consts():
    g = np.arange(512) // 64
    avg = (g[:, None] == g[None, :]).astype(np.float32) / 64.0
    masks = (np.arange(8)[:, None] == g[None, :]).astype(np.float32).reshape(8, 1, 512)
    return jnp.asarray(avg, BF), jnp.asarray(masks)


def _pack(items):
    flat = jnp.concatenate([a.reshape(-1).astype(F32) for a in items])
    n = flat.shape[0]
    rows = -(-n // D)
    rows = -(-rows // 8) * 8
    return jnp.pad(flat, (0, rows * D - n)).reshape(rows, D)


def _unpack(buf, shapes):
    lead = buf.shape[:-2]
    flat = buf.reshape(lead + (-1,))
    out, off = [], 0
    for shp in shapes:
        n = int(np.prod(shp))
        out.append(flat[..., off:off + n].reshape(lead + tuple(shp)))
        off += n
    return out


def _arrive(prm, key, after):
    if callable(prm[key]):
        prm[key](after)
    return prm[key]


def _layer_fwd(i, x, h, mods, prm, consts, nxt):
    sv = {}
    sv["x0"] = x
    sv["h"] = h
    p = proj_in(f"proj_in_{i}", h, _arrive(prm, "w_in", h))
    sv["p"] = p
    if i == 0:
        q, kv, m2 = even_tok_fwd(p, consts["cos_e"], consts["sin_e"], prm["gq"], prm["gk"], prm["gs"],
                                 prm["sgu_w"], prm["sgu_b"], consts["avg"], consts["masks"])
        o, lse = attn_fwd("attn_fwd_0", q, kv, GQA_HEADS)
        sv.update(q=q, kv=kv)
    else:
        q, kv, y = odd_tok_fwd(p, consts["cos_o"], consts["sin_o"], prm["gq"], prm["gkv"], prm["wq"], prm["wkk"],
                               prm["wkv"], consts["spread"])
        o, lse = attn_fwd("attn_fwd_1", q, kv, MLA_HEADS)
        z, m2 = conf_fwd(y, prm["conv_w"], prm["conv_b"], prm["ln_g"], prm["ln_b"])
        sv.update(q=q, kv=kv, y=y, z=z)
    sv.update(o=o, lse=lse, m2=m2)
    x1, y1, h2 = proj_out(f"proj_out_{i}", o, m2, _arrive(prm, "w_out", o), x, mods, 2, prm["norm2_g"], 3, 4,
                          last=nxt is None)
    sv.update(x1=x1, y1=y1)
    a, f = mlp_up(f"mlp_up_{i}", h2, _arrive(prm, "w1", h2))
    x2, y2, *h_next = mlp_down(f"mlp_down_{i}", f, prm["w2"], x1, mods, 5, nxt)
    sv.update(h2=h2, a=a, f=f, y2=y2)
    return x2, (h_next[0] if h_next else None), sv


def _layer_bwd(i, dx, dy2, dg2, sv, mods, prm, consts, hook, entry, below):
    gr = {}
    da = mlp_bwd_da(f"mlp_bwd_da_{i}", dy2, prm["w2"], sv["a"], after=entry)
    tiles8 = [(h, j) for h in range(2) for j in range(4)]
    gr["w1"] = mm_tn(f"grad_w1_{i}", sv["h2"], da, tiles8, 512, D).reshape(2, 4, 512, D)
    gr["w2"] = mm_tn(f"grad_w2_{i}", sv["f"], dy2, [(2 * j + h, 0) for h in range(2) for j in range(4)],
                     512, D).reshape(2, 4, 512, D)
    dh2 = mlp_bwd_dh(f"mlp_bwd_dh_{i}", da, prm["w1"])
    dx1, dy1, dg1, dsh2, dsc2, gr["norm2_g"] = modnorm_bwd(
        f"norm2_bwd_{i}", sv["x1"], dh2, dx, mods, prm["norm2_g"], 3, 4, gate=(sv["y1"], mods, 2),
        after=hook(f"{i}:mlp", gr, dh2), last=below is not None)
    dcat = mm_nt(f"proj_out_bwd_{i}", dy1, prm["w_out"], after=hook(f"{i}:mid", gr, dy1))
    go = mm_tn(f"grad_wout_a_{i}", sv["o"], dy1, [(0, 0)], 512, D).reshape(2, 2, 128, D)
    gm = mm_tn(f"grad_wout_b_{i}", sv["m2"], dy1, [(0, 0)], 512, D).reshape(2, 2, 128, D)
    gr["w_out"] = jnp.concatenate([go, gm], axis=0).transpose(1, 0, 2, 3)
    if i == 0:
        dq, dkv = attn_bwd("attn_bwd_0", sv["q"], sv["kv"], sv["o"], dcat, sv["lse"], GQA_HEADS)
        dp, gr["gq"], gr["gk"], gr["gs"], gr["sgu_w"], gr["sgu_b"] = even_tok_bwd(
            sv["p"], dq, dkv, dcat, consts["cos_e"], consts["sin_e"], prm["gq"], prm["gk"],
            prm["gs"], prm["sgu_w"], prm["sgu_b"], consts["avg"], consts["masks"])
    else:
        dq, dkv = attn_bwd("attn_bwd_1", sv["q"], sv["kv"], sv["o"], dcat, sv["lse"], MLA_HEADS)
        dz, gr["ln_g"], gr["ln_b"], gr["conv_b"] = conf_bwd_ln(sv["z"], dcat, prm["ln_g"], prm["ln_b"])
        dyc, gr["conv_w"] = conf_bwd_conv(sv["y"], dz, prm["conv_w"])
        dp, gr["gq"], gr["gkv"], gr["wq"], gr["wkk"], gr["wkv"] = odd_tok_bwd(
            sv["p"], dq, dkv, dyc, consts["cos_o"], consts["sin_o"], prm["gq"], prm["gkv"], prm["wq"], prm["wkk"],
            prm["wkv"], consts["spread"])
    n_in = prm["w_in"].shape[1]
    gr["w_in"] = mm_tn(f"grad_win_{i}", sv["h"], dp, [(0, 0), (1, 0)], 512, n_in)
    dh = mm_nt(f"proj_in_bwd_{i}", dp, prm["w_in"])
    if below:
        dx0, dy2b, dg2b, dsh1, dsc1, gr["norm1_g"] = modnorm_bwd(
            f"norm1_bwd_{i}", sv["x0"], dh, dx1, mods, prm["norm1_g"], 0, 1, gate=(below[0], below[1], 5))
        down = (dy2b, dg2b)
    else:
        dx0, dsh1, dsc1, gr["norm1_g"] = modnorm_bwd(f"norm1_bwd_{i}", sv["x0"], dh, dx1, mods, prm["norm1_g"], 0, 1,
                                                     lat_only=True)
        down = None
    dmods = jnp.concatenate([dsh1, dsc1, dg1, dsh2, dsc2, dg2], axis=1)
    return dx0, down, dmods, gr, hook(f"{i}:end", gr, dx0)


def local_step(xcat, target, mods, prms, final_g, hook=lambda point, grads, fresh: None):
    avg, masks = _group_consts()
    cos_e, sin_e = _rope_tables(64, 8)
    ck, sk = _rope_tables(32, 1)
    one64, zero64 = np.ones((SEQ, 64), np.float32), np.zeros((SEQ, 64), np.float32)
    one96, zero96 = np.ones((SEQ, 96), np.float32), np.zeros((SEQ, 96), np.float32)
    cos_o = np.concatenate([np.tile(np.concatenate([one64, ck], axis=1), (1, 8)), ck, one96], axis=1)
    sin_o = np.concatenate([np.tile(np.concatenate([zero64, sk], axis=1), (1, 8)), sk, zero96], axis=1)
    lane = np.arange(768)
    spread = np.zeros((128, 768), np.float32)
    spread[lane % 96 - 64, lane] = (lane % 96 >= 64)
    consts = dict(avg=avg, masks=masks, cos_e=jnp.asarray(cos_e), sin_e=jnp.asarray(sin_e),
                  cos_o=jnp.asarray(cos_o), sin_o=jnp.asarray(sin_o), spread=jnp.asarray(spread, BF))
    x = xcat
    h = modnorm_fwd("norm1_fwd_0", x, mods[0], prms[0]["norm1_g"], 0, 1)
    saved = []
    for i in range(2):
        x, h, sv = _layer_fwd(i, x, h, mods[i], prms[i], consts, (mods[1], prms[1]["norm1_g"]) if i == 0 else None)
        saved.append(sv)
    dx, dy2, dg2, loss, dfg = final_loss(x, target, final_g, saved[1]["y2"], mods[1], 5)
    dmods, grads = [None, None], [None, None]
    entry, down = None, (dy2, dg2)
    for i in (1, 0):
        below = (saved[0]["y2"], mods[0]) if i == 1 else None
        dx, down, dmods[i], grads[i], entry = _layer_bwd(i, dx, down[0], down[1], saved[i], mods[i], prms[i], consts,
                                                         hook, entry, below)
    return loss, dx, dmods, grads, dfg, entry


def _row(v):
    return v.reshape(1, -1).astype(F32)


def odd_in_params(od_w_in, w_uq, w_ukv):
    od = jnp.concatenate([od_w_in[:, 0:416], jnp.zeros((D, 96), od_w_in.dtype), od_w_in[:, 416:OD_IN]], axis=1)
    ukv = w_ukv.reshape(128, 8, 128)
    wkk = jnp.pad(ukv[:, :, :64], ((0, 0), (0, 0), (0, 32))).reshape(128, 768)
    return dict(w_in=od, wq=w_uq, wkk=wkk, wkv=ukv[:, :, 64:].reshape(128, 512))


def small_params(small):
    p0 = dict(norm1_g=_row(small["norm1_g"][0]), norm2_g=_row(small["norm2_g"][0]),
              gq=jnp.tile(_row(small["ev_q_norm_g"]), (1, 8)), gk=jnp.tile(_row(small["ev_k_norm_g"]), (1, 2)),
              gs=_row(small["ev_sgu_norm_g"]), sgu_w=small["ev_sgu_w"].reshape(8, 128, 128).astype(F32),
              sgu_b=small["ev_sgu_b"].reshape(8, 128, 1).astype(F32))
    p1 = dict(norm1_g=_row(small["norm1_g"][1]), norm2_g=_row(small["norm2_g"][1]),
              gq=_row(small["od_q_norm_g"]), gkv=_row(small["od_kv_norm_g"]),
              conv_w=jnp.pad(small["od_conv_w"].reshape(CONV_K, 512).astype(F32), ((0, 1), (0, 0))),
              conv_b=_row(small["od_conv_b"]), ln_g=_row(small["od_ln_g"]), ln_b=_row(small["od_ln_b"]))
    return [p0, p1]


def small_grads_natural(grads, dfg):
    g0, g1 = grads
    return dict(
        norm1_g=jnp.concatenate([g0["norm1_g"], g1["norm1_g"]], axis=0),
        norm2_g=jnp.concatenate([g0["norm2_g"], g1["norm2_g"]], axis=0),
        ev_q_norm_g=g0["gq"].reshape(8, 64).sum(0).reshape(1, 64),
        ev_k_norm_g=g0["gk"].reshape(2, 64).sum(0).reshape(1, 64),
        ev_sgu_norm_g=g0["gs"].reshape(1, 8, 64),
        ev_sgu_w=g0["sgu_w"].reshape(1, 8, 128, 128),
        ev_sgu_b=g0["sgu_b"].reshape(1, 8, 128),
        od_q_norm_g=g1["gq"].reshape(1, 256),
        od_kv_norm_g=g1["gkv"].reshape(1, 128),
        od_conv_w=g1["conv_w"][0:CONV_K].reshape(1, CONV_K, 512),
        od_conv_b=g1["conv_b"].reshape(1, 512),
        od_ln_g=g1["ln_g"].reshape(1, 512),
        od_ln_b=g1["ln_b"].reshape(1, 512),
        final_g=dfg.reshape(D))


def layer_grads_hs(i, g, part="all"):
    def cols(a):
        k, n = a.shape
        return a.reshape(2, k // 2, 4, n // 4).transpose(0, 2, 1, 3).astype(BF)

    mlp = [(("mlp_w1", i), g["w1"]), (("mlp_w2", i), g["w2"])]
    if part == "mlp":
        return mlp
    rest = [(("w_out", i), g["w_out"])]
    if i == 0:
        rest.append((("ev_w_in", 0), cols(g["w_in"].reshape(D, EV_IN))))
    else:
        od = g["w_in"].reshape(D, OD_PAD)
        od = jnp.concatenate([od[:, 0:416], od[:, 512:OD_PAD]], axis=1)
        ukv = jnp.concatenate([g["wkk"].reshape(128, 8, 96)[:, :, :64], g["wkv"].reshape(128, 8, 64)], axis=2)
        rest += [(("od_w_in", 0), cols(od)), (("od_w_uq", 0), cols(g["wq"])),
                 (("od_w_ukv", 0), cols(ukv.reshape(128, 1024)))]
    return rest if part == "rest" else mlp + rest


WEIGHT_NAMES = ['c_ctx', 'ada_w', 'ada_b', 'norm1_g', 'norm2_g', 'w_out', 'mlp_w1', 'mlp_w2', 'ev_w_in',
                'ev_q_norm_g', 'ev_k_norm_g', 'ev_sgu_norm_g', 'ev_sgu_w', 'ev_sgu_b', 'od_w_in', 'od_q_norm_g',
                'od_kv_norm_g', 'od_w_uq', 'od_w_ukv', 'od_conv_w', 'od_conv_b', 'od_ln_g', 'od_ln_b', 'final_g']
REPL_SMALL = ['norm1_g', 'norm2_g', 'ev_q_norm_g', 'ev_k_norm_g', 'ev_sgu_norm_g', 'ev_sgu_w', 'ev_sgu_b',
              'od_kv_norm_g', 'final_g']
SHARD_SMALL = ['od_q_norm_g', 'od_conv_w', 'od_conv_b', 'od_ln_g', 'od_ln_b']
BIG = ['w_out', 'mlp_w1', 'mlp_w2', 'ev_w_in', 'od_w_in', 'od_w_uq', 'od_w_ukv']


def _gather_last(parts):
    return jnp.concatenate([parts[k] for k in range(4)], axis=-1)


class _Reduce:
    def __init__(self, tag, named, half, where):
        self.tag, self.half, self.where = tag, half, where
        self.names, self.hs = zip(*named)
        self.hs = list(self.hs)

    def to_sibling(self):
        lands = [lax.empty(a.shape[1:], BF) for a in self.hs]
        (self.h1,), token = exchange_start(f"rs_sibling_start_{self.tag}", [(self.hs, lands)], TO_SIBLING)
        return token

    def to_chips(self, after):
        hs, got = exchange_wait(f"rs_sibling_wait_{self.tag}", self.h1, after, TO_SIBLING)
        pair = [add_pairs(f"rs_add_{self.tag}_{k}", a, b, self.half) for k, (a, b) in enumerate(zip(hs, got))]
        lands = [lax.empty(p.shape, BF) for p in pair]
        (self.h2,), token = exchange_start(f"rs_chips_start_{self.tag}", [(pair, lands)], SCATTER)
        return token

    def finish(self, after, bufs):
        pair, land = exchange_wait(f"rs_chips_wait_{self.tag}", self.h2, after, SCATTER)
        for k, ((n, idx), l, p) in enumerate(zip(self.names, land, pair)):
            bufs[n] = sum_slabs(f"rs_sum_{self.tag}_{k}", l, p, self.where, bufs[n], idx)


def kernel(x, c, ctx, c_ctx, ada_w, ada_b, norm1_g, norm2_g, w_out, mlp_w1, mlp_w2, ev_w_in, ev_q_norm_g, ev_k_norm_g, ev_sgu_norm_g, ev_sgu_w, ev_sgu_b, od_w_in, od_q_norm_g, od_kv_norm_g, od_w_uq, od_w_ukv, od_conv_w, od_conv_b, od_ln_g, od_ln_b, final_g, loss_target, m_c_ctx, m_ada_w, m_ada_b, m_norm1_g, m_norm2_g, m_w_out, m_mlp_w1, m_mlp_w2, m_ev_w_in, m_ev_q_norm_g, m_ev_k_norm_g, m_ev_sgu_norm_g, m_ev_sgu_w, m_ev_sgu_b, m_od_w_in, m_od_q_norm_g, m_od_kv_norm_g, m_od_w_uq, m_od_w_ukv, m_od_conv_w, m_od_conv_b, m_od_ln_g, m_od_ln_b, m_final_g, v_c_ctx, v_ada_w, v_ada_b, v_norm1_g, v_norm2_g, v_w_out, v_mlp_w1, v_mlp_w2, v_ev_w_in, v_ev_q_norm_g, v_ev_k_norm_g, v_ev_sgu_norm_g, v_ev_sgu_w, v_ev_sgu_b, v_od_w_in, v_od_q_norm_g, v_od_kv_norm_g, v_od_w_uq, v_od_w_ukv, v_od_conv_w, v_od_conv_b, v_od_ln_g, v_od_ln_b, v_final_g):
    w = dict(c_ctx=c_ctx, ada_w=ada_w, ada_b=ada_b, norm1_g=norm1_g, norm2_g=norm2_g, w_out=w_out, mlp_w1=mlp_w1,
             mlp_w2=mlp_w2, ev_w_in=ev_w_in, ev_q_norm_g=ev_q_norm_g, ev_k_norm_g=ev_k_norm_g,
             ev_sgu_norm_g=ev_sgu_norm_g, ev_sgu_w=ev_sgu_w, ev_sgu_b=ev_sgu_b, od_w_in=od_w_in,
             od_q_norm_g=od_q_norm_g, od_kv_norm_g=od_kv_norm_g, od_w_uq=od_w_uq, od_w_ukv=od_w_ukv,
             od_conv_w=od_conv_w, od_conv_b=od_conv_b, od_ln_g=od_ln_g, od_ln_b=od_ln_b, final_g=final_g)
    mom = dict(c_ctx=m_c_ctx, ada_w=m_ada_w, ada_b=m_ada_b, norm1_g=m_norm1_g, norm2_g=m_norm2_g, w_out=m_w_out,
               mlp_w1=m_mlp_w1, mlp_w2=m_mlp_w2, ev_w_in=m_ev_w_in, ev_q_norm_g=m_ev_q_norm_g,
               ev_k_norm_g=m_ev_k_norm_g, ev_sgu_norm_g=m_ev_sgu_norm_g, ev_sgu_w=m_ev_sgu_w, ev_sgu_b=m_ev_sgu_b,
               od_w_in=m_od_w_in, od_q_norm_g=m_od_q_norm_g, od_kv_norm_g=m_od_kv_norm_g, od_w_uq=m_od_w_uq,
               od_w_ukv=m_od_w_ukv, od_conv_w=m_od_conv_w, od_conv_b=m_od_conv_b, od_ln_g=m_od_ln_g,
               od_ln_b=m_od_ln_b, final_g=m_final_g)
    var = dict(c_ctx=v_c_ctx, ada_w=v_ada_w, ada_b=v_ada_b, norm1_g=v_norm1_g, norm2_g=v_norm2_g, w_out=v_w_out,
               mlp_w1=v_mlp_w1, mlp_w2=v_mlp_w2, ev_w_in=v_ev_w_in, ev_q_norm_g=v_ev_q_norm_g,
               ev_k_norm_g=v_ev_k_norm_g, ev_sgu_norm_g=v_ev_sgu_norm_g, ev_sgu_w=v_ev_sgu_w, ev_sgu_b=v_ev_sgu_b,
               od_w_in=v_od_w_in, od_q_norm_g=v_od_q_norm_g, od_kv_norm_g=v_od_kv_norm_g, od_w_uq=v_od_w_uq,
               od_w_ukv=v_od_w_ukv, od_conv_w=v_od_conv_w, od_conv_b=v_od_conv_b, od_ln_g=v_od_ln_g,
               od_ln_b=v_od_ln_b, final_g=v_final_g)
    xi, yi, ci = lax.axis_index("x"), lax.axis_index("y"), lax.axis_index("c")
    chip = 2 * xi + yi
    dev = 2 * chip + ci

    shard_shapes = [w[n].shape for n in SHARD_SMALL]
    g0 = all_gather8("ag_small", _pack([c] + [w[n] for n in SHARD_SMALL]))
    g0 = g0.reshape(8, -1, D)
    parts = _unpack(g0, [c.shape] + shard_shapes)
    c_all = parts[0].reshape(16, D)
    small_full = {n: _gather_last(p[0::2]) for n, p in zip(SHARD_SMALL, parts[1:])}
    call = jnp.concatenate([c_all, c_ctx.reshape(1, D), jnp.zeros((NC - 17, D), F32)], axis=0)

    cols = ada_w.shape[2]
    ada_b_sh = lax.dynamic_slice(ada_b, (0, chip * cols), (2, cols)).reshape(2, 1, cols)
    mt = mods_fwd(call, ada_w, ada_b_sh)
    mt = all_gather8("ag_mods", mt.reshape(2 * NC, cols)).reshape(8, 2, NC, cols)
    table = mt[0::2].transpose(1, 2, 0, 3).reshape(2, NC, 4 * cols)
    mods = []
    for i in range(2):
        lat = lax.dynamic_slice(table[i], (2 * dev, 0), (2, 4 * cols))
        mc = table[i, 16]
        mods.append(jnp.stack([mc, lat[0], mc, lat[1]]).reshape(4 * N_MOD, 1, D))

    order = [[("ev_w_in", 0)], [("w_out", 0), ("mlp_w1", 0), ("mlp_w2", 0)],
             [("od_w_in", 0), ("od_w_uq", 0), ("od_w_ukv", 0), ("w_out", 1)], [("mlp_w1", 1), ("mlp_w2", 1)]]
    groups = []
    for names in order:
        srcs = [w[n][i].astype(BF) for n, i in names]
        groups.append((srcs, [lax.empty((4,) + s.shape, BF) for s in srcs]))
    groups[0][0][0], table = lax.optimization_barrier((groups[0][0][0], table))
    handles, token = exchange_start("gather_start", groups, GATHER)
    mods[0] = mods[0] + token[0, 0]
    small = {n: w[n] for n in REPL_SMALL}
    small.update(small_full)
    prms = small_params(small)

    chip1 = chip.reshape(1).astype(jnp.int32)

    def arrived(k, after):
        srcs, lands = exchange_wait(f"gather_wait_{k}", handles[k], after, GATHER)
        return [place_own(f"gather_own_{k}_{a}", l, s, chip1) for a, (l, s) in enumerate(zip(lands, srcs))]

    def arrive_ev_in(after):
        (ev,) = arrived(0, after)
        prms[0]["w_in"] = _gather_last(ev)

    def arrive_ev_rest(after):
        wo, w1, w2 = arrived(1, after)
        prms[0].update(w_out=wo.reshape(D, D), w1=w1, w2=w2)

    def arrive_od(after):
        od, uq, ukv, wo = arrived(2, after)
        prms[1].update(odd_in_params(_gather_last(od), _gather_last(uq), _gather_last(ukv)), w_out=wo.reshape(D, D))

    def arrive_od_mlp(after):
        w1, w2 = arrived(3, after)
        prms[1].update(w1=w1, w2=w2)

    prms[0]["w_in"] = arrive_ev_in
    prms[0]["w_out"] = arrive_ev_rest
    prms[1]["w_in"] = arrive_od
    prms[1]["w1"] = arrive_od_mlp

    half = ci.reshape(1).astype(jnp.int32)
    where = jnp.stack([chip, ci]).astype(jnp.int32)
    red = {}

    def hook(point, g, fresh):
        if point == "1:end":
            red["l1"] = _Reduce("l1", layer_grads_hs(1, g, "all"), half, where)
            return red["l1"].to_sibling()
        if point == "0:mlp":
            red["l0_mlp"] = _Reduce("l0_mlp", layer_grads_hs(0, g, "mlp"), half, where)
            return red["l1"].to_chips(fresh) + red["l0_mlp"].to_sibling()
        if point == "0:mid":
            return red["l0_mlp"].to_chips(fresh)
        if point == "0:end":
            red["l0_rest"] = _Reduce("l0_rest", layer_grads_hs(0, g, "rest"), half, where)
            return red["l0_rest"].to_sibling()
        return None

    xin = (ctx.reshape(NEX * LC, D), x.reshape(NEX * L, D))
    loss_p, dx, dmods, grads, dfg, last = local_step(xin, loss_target.reshape(NEX * L, D), mods, prms,
                                                     final_g.reshape(1, D), hook)
    grad_x = dx.reshape(NEX, L, D)

    sg = small_grads_natural(grads, dfg)
    dm = jnp.stack([d.reshape(4, N_MOD * D) for d in dmods])
    small_names = REPL_SMALL + SHARD_SMALL
    items = [dm[:, 1::2], dm[:, 0] + dm[:, 2]] + [sg[n] for n in small_names] + [loss_p[0:1, 0:1]]
    shapes = [a.shape for a in items]
    g1 = all_gather8("ag_grads", _pack(items), after=last)
    started = red["l0_rest"].to_chips(g1)
    rows1 = g1.shape[0] // 8
    g1 = g1.reshape(8, rows1, D)
    tot = _unpack(sum_lead("sum_small", g1, after=started), shapes)
    dm_lat = _unpack(g1, shapes[:1])[0]
    dm_lat = dm_lat.transpose(1, 0, 2, 3).reshape(2, 16, N_MOD * D)
    dm_all = jnp.concatenate([dm_lat, tot[1][:, None], jnp.zeros((2, NC - 17, N_MOD * D), F32)], axis=1)
    gsum = dict(zip(small_names, tot[2:2 + len(small_names)]))
    loss = tot[-1].reshape(())
    grad = {n: gsum[n].reshape(w[n].shape) for n in REPL_SMALL}
    for n in SHARD_SMALL:
        k = w[n].shape[-1]
        grad[n] = lax.dynamic_slice_in_dim(gsum[n], chip * k, k, axis=gsum[n].ndim - 1)
    grad["ada_b"] = sum_lead("sum_ada_b", dm_all.transpose(1, 0, 2).reshape(NC, 2 * N_MOD, D)).reshape(2, N_MOD * D)

    dm_sh = lax.dynamic_slice(dm_all, (0, 0, chip * cols), (2, NC, cols))
    grad["ada_w"], dcc = ada_bwd(call, ada_w, dm_sh)
    dcc = all_gather8("ag_cctx", dcc).reshape(8, 8, D)
    grad["c_ctx"] = sum_lead("sum_cctx", dcc[0::2])[0]

    delta, new_m, new_v = {}, {}, {}

    def adam_big(n, again):
        shp = w[n].shape
        two_d = (shp[0] * shp[1], shp[2])
        res = adamw(f"adamw_{n}", w[n].reshape(two_d), grad[n].reshape(two_d), mom[n].reshape(two_d),
                    var[n].reshape(two_d), again)
        delta[n], new_m[n], new_v[n] = [a.reshape(shp) for a in res[:3]]
        if again:
            grad[n] = res[3].reshape(shp)

    adam_big('ada_w', False)
    rest = [n for n in WEIGHT_NAMES if n not in ['ada_w'] + BIG]
    flat2 = lambda a: a.reshape(-1, a.shape[-1])
    outs = adamw_many("adamw_small", [flat2(w[n]) for n in rest], [flat2(grad[n]) for n in rest],
                      [flat2(mom[n]) for n in rest], [flat2(var[n]) for n in rest])
    for dst, arrs in zip((delta, new_m, new_v), outs):
        dst.update({n: a.reshape(w[n].shape) for n, a in zip(rest, arrs)})
    d_ = outs[0][1]

    bufs = {n: lax.empty((w[n].shape[0], 2, w[n].shape[1] // 2, w[n].shape[2]), F32) for n in BIG}
    for tag, behind in (("l1", delta["ada_w"]), ("l0_mlp", d_), ("l0_rest", d_)):
        red[tag].finish(behind, bufs)
    for n, full in zip(BIG, sibling_merge("rs_sibling_merge", [bufs[n] for n in BIG])):
        grad[n] = full.reshape(w[n].shape)
    for n in BIG:
        adam_big(n, True)

    return (loss, grad_x, *[grad[n] for n in WEIGHT_NAMES], *[delta[n] for n in WEIGHT_NAMES],
            *[new_m[n] for n in WEIGHT_NAMES], *[new_v[n] for n in WEIGHT_NAMES])
```

```python
import functools

import numpy as np
import jax
import jax.numpy as jnp
from jax import lax
from jax.experimental import pallas as pl
from jax.experimental.pallas import tpu as pltpu

F32 = jnp.float32
BF = jnp.bfloat16
HI = lax.Precision.HIGHEST
MESH = pl.DeviceIdType.MESH

D = 1024
L = 2048
LC = 256
SEQ = L + LC
NEX = 2
R = NEX * SEQ
TB = 256
WIDE = 512
BPE = SEQ // TB
NBLK = R // TB
GRID_W = 64
FF = 4 * D
EPS = 1e-6
ROPE_THETA = 10000.0
N_MOD = 6
EV_IN = 1792
OD_IN = 1440
OD_PAD = 1536
VMEM_LIMIT = 60 * 1024 * 1024

ADAM_LR = 0.001
ADAM_B1 = 0.9
ADAM_B2 = 0.999
ADAM_EPS = 1e-08
ADAM_WD = 0.01
ADAM_STEP = 10

NT = (((1,), (1,)), ((), ()))
TN = (((0,), (0,)), ((), ()))


def _cparams(sem=None):
    return pltpu.CompilerParams(dimension_semantics=sem, vmem_limit_bytes=VMEM_LIMIT)


@jax.custom_vjp
def _mm(a, b):
    return jnp.dot(a.astype(BF), b.astype(BF), preferred_element_type=F32)


def _mm_fwd(a, b):
    return _mm(a, b), (a, b)


def _mm_bwd(res, g):
    a, b = res
    gb = g.astype(BF)
    da = lax.dot_general(gb, b.astype(BF), NT, preferred_element_type=F32)
    db = lax.dot_general(a.astype(BF), gb, TN, preferred_element_type=F32)
    return da, db


_mm.defvjp(_mm_fwd, _mm_bwd)


@jax.custom_vjp
def _swap(x):
    n = x.shape[-1]
    ax = x.ndim - 1
    lane = lax.broadcasted_iota(jnp.int32, x.shape, ax)
    return jnp.where(lane % 2 == 0, pltpu.roll(x, n - 1, ax), pltpu.roll(x, 1, ax))


_swap.defvjp(lambda x: (_swap(x), None), lambda _, g: (_swap(g),))


def _rope(x, cos, sin):
    return x * cos + _swap(x) * sin


def _rmsn(x, g):
    return x * lax.rsqrt(jnp.mean(x * x, axis=-1, keepdims=True) + EPS) * g


def _split_dot(a, m):
    hi = a.astype(BF)
    lo = (a - hi.astype(F32)).astype(BF)
    return jnp.dot(hi, m, preferred_element_type=F32) + jnp.dot(lo, m, preferred_element_type=F32)


@jax.custom_vjp
def _group_mean(a, avg):
    return _split_dot(a, avg)


_group_mean.defvjp(lambda a, avg: (_split_dot(a, avg), avg),
                   lambda avg, g: (_split_dot(g, avg), jnp.zeros_like(avg)))


def _grmsn(x, g, avg):
    return x * lax.rsqrt(_group_mean(x * x, avg) + EPS) * g


def _modnorm(x, g, sh, sc):
    return _rmsn(x, g) * (1.0 + sc) + sh


def _gelu(x):
    return 0.5 * x * (1.0 + jnp.tanh(0.7978845608028654 * (x + 0.044715 * (x * x * x))))


def _silu(x):
    return x * jax.nn.sigmoid(x)


def _acc(ref, val, first):
    @pl.when(first)
    def _():
        ref[...] = val

    @pl.when(jnp.logical_not(first))
    def _():
        ref[...] += val


def _seg(i):
    return 2 * (i // BPE) + jnp.minimum(i % BPE, 1)


def _seg_first(i):
    return (i % BPE) <= 1


class _Either:
    def __init__(self, pick_first, first, second):
        self.pick_first, self.first, self.second = pick_first, first, second

    def __getitem__(self, idx):
        return jnp.where(self.pick_first, self.first[idx], self.second[idx])


def _rb_call(name, body, row_in=(), mod_in=(), pos_in=(), full_in=(), shift_in=(),
             row_out=(), seg_out=(), acc_out=(), scratch=(), after=None, col_in=(), rows=TB, idle=None):
    assert rows == TB or not (mod_in or pos_in or shift_in or seg_out or col_in)
    in_specs, args, pairs = [], [], []
    for a in row_in:
        if isinstance(a, tuple):
            pairs.append(len(args))
            in_specs.append(pl.BlockSpec((TB, a[0].shape[1]), lambda i: (i // BPE, 0)))
            in_specs.append(pl.BlockSpec(
                (TB, a[1].shape[1]), lambda i: ((i // BPE) * (L // TB) + jnp.maximum(i % BPE - 1, 0), 0)))
            args += list(a)
        else:
            in_specs.append(pl.BlockSpec((rows, a.shape[1]), lambda i: (i, 0)))
            args.append(a)
    for a in col_in:
        in_specs.append(pl.BlockSpec((a.shape[0], TB), lambda i: (0, i)))
        args.append(a)
    for tab, m in mod_in:
        in_specs.append(pl.BlockSpec((1, 1, D), lambda i, m=m: (_seg(i) * N_MOD + m, 0, 0)))
        args.append(tab)
    for a in pos_in:
        in_specs.append(pl.BlockSpec((TB, a.shape[1]), lambda i: (i % BPE, 0)))
        args.append(a)
    for a in full_in:
        in_specs.append(pl.BlockSpec(a.shape, lambda i, n=a.ndim: (0,) * n))
        args.append(a)
    for a, d in shift_in:
        in_specs.append(pl.BlockSpec((TB, a.shape[1]), lambda i, d=d: (jnp.clip(i + d, 0, NBLK - 1), 0)))
        args.append(a)
    n_in = len(args)
    if after is not None:
        in_specs.append(pl.BlockSpec(after.shape, lambda i, n=after.ndim: (0,) * n))
        args.append(after)
    out_specs, out_shape = [], []
    for w, dt, *lat in row_out:
        if lat:
            out_specs.append(pl.BlockSpec(
                (TB, w), lambda i: ((i // BPE) * (L // TB) + jnp.maximum(i % BPE - 1, 0), 0)))
            out_shape.append(jax.ShapeDtypeStruct((NEX * L, w), dt))
        else:
            out_specs.append(pl.BlockSpec((rows, w), lambda i: (i, 0)))
            out_shape.append(jax.ShapeDtypeStruct((R, w), dt))
    for w in seg_out:
        out_specs.append(pl.BlockSpec((1, 1, w), lambda i: (_seg(i), 0, 0)))
        out_shape.append(jax.ShapeDtypeStruct((4, 1, w), F32))
    for shp in acc_out:
        out_specs.append(pl.BlockSpec(shp, lambda i, n=len(shp): (0,) * n))
        out_shape.append(jax.ShapeDtypeStruct(shp, F32))

    def kern(*refs):
        i = pl.program_id(0)
        ins = list(refs[:n_in])
        for k in reversed(pairs):
            ins[k:k + 2] = [_Either((i % BPE) == 0, ins[k], ins[k + 1])]
        if idle is None:
            body(i, *ins, *refs[len(args):])
        else:
            @pl.when((i % BPE) >= 1)
            def _():
                body(i, *ins, *refs[len(args):])

            @pl.when((i % BPE) == 0)
            def _():
                idle(i, *ins, *refs[len(args):])

    sem = ("arbitrary",) if (seg_out or acc_out or any(len(r) > 2 for r in row_out)) else ("parallel",)
    return pl.pallas_call(kern, grid=(R // rows,), in_specs=in_specs, out_specs=out_specs, out_shape=out_shape,
                          scratch_shapes=list(scratch), compiler_params=_cparams(sem), name=name)(*args)


def modnorm_fwd(name, x, mods, g, m_sh, m_sc):
    def body(i, x_ref, sh_ref, sc_ref, g_ref, h_ref):
        h_ref[...] = _modnorm(x_ref[...], g_ref[...], sh_ref[0], sc_ref[0]).astype(BF)

    return _rb_call(name, body, row_in=(x,), mod_in=((mods, m_sh), (mods, m_sc)), full_in=(g,),
                    row_out=((D, BF),))[0]


def _gate_grads(dx, y_ref, gt_ref, dy_ref, dgt_ref, i):
    dy_ref[...] = (dx * gt_ref[0]).astype(BF)
    _acc(dgt_ref, jnp.sum(dx * y_ref[...].astype(F32), axis=0, keepdims=True)[None], _seg_first(i))


def modnorm_bwd(name, x, dh, dx_in, mods, g, m_sh, m_sc, gate=None, after=None, lat_only=False, last=False):
    def body(i, x_ref, dh_ref, dxin_ref, *rest):
        if gate:
            y_ref, sh_ref, sc_ref, gt_ref, g_ref, dx_ref, dy_ref, dgt_ref, dsh_ref, dsc_ref, dg_ref = rest
        else:
            sh_ref, sc_ref, g_ref, dx_ref, dsh_ref, dsc_ref, dg_ref = rest
        _, vjp = jax.vjp(_modnorm, x_ref[...], g_ref[...], sh_ref[0], sc_ref[0])
        dx, dg, dsh, dsc = vjp(dh_ref[...].astype(F32))
        dx = dxin_ref[...] + dx
        dx_ref[...] = dx
        if gate:
            _gate_grads(dx, y_ref, gt_ref, dy_ref, dgt_ref, i)
        _acc(dsh_ref, dsh[None], _seg_first(i))
        _acc(dsc_ref, dsc[None], _seg_first(i))
        _acc(dg_ref, dg, i == 0)

    def idle(i, x_ref, dh_ref, dxin_ref, y_ref, sh_ref, sc_ref, gt_ref, g_ref, dx_ref, dy_ref, dgt_ref, dsh_ref,
             dsc_ref, dg_ref):
        dx_ref[...] = dxin_ref[...]
        _zero(dy_ref, dgt_ref, dsh_ref, dsc_ref)
        _zero_at_start(i, dg_ref)

    if gate:
        y, gmods, m = gate
        return _rb_call(name, body, row_in=(x, dh, dx_in, y), mod_in=((mods, m_sh), (mods, m_sc), (gmods, m)),
                        full_in=(g,), row_out=((D, F32), (D, BF)), seg_out=(D, D, D), acc_out=((1, D),), after=after,
                        idle=idle if last else None)
    return _rb_call(name, body, row_in=(x, dh, dx_in), mod_in=((mods, m_sh), (mods, m_sc)), full_in=(g,),
                    row_out=((D, F32, "lat") if lat_only else (D, F32),), seg_out=(D, D), acc_out=((1, D),),
                    after=after)


def proj_in(name, h, w):
    n = w.shape[1]

    def body(i, h_ref, w_ref, o_ref):
        o_ref[...] = jnp.dot(h_ref[...], w_ref[...], preferred_element_type=F32).astype(BF)

    return _rb_call(name, body, row_in=(h,), full_in=(w,), row_out=((n, BF),), rows=WIDE)[0]


def _zero(*refs):
    for r in refs:
        r[...] = jnp.zeros_like(r)


def _zero_at_start(i, *refs):
    @pl.when(i == 0)
    def _():
        _zero(*refs)


def proj_out(name, a1, a2, w, x, mods, m_gate, g_next, m_sh, m_sc, last=False):
    k1 = a1.shape[1]

    def idle(i, a1_ref, a2_ref, x_ref, gt_ref, sh_ref, sc_ref, w_ref, g_ref, xo_ref, y_ref, h_ref):
        xo_ref[...] = x_ref[...]
        _zero(y_ref, h_ref)

    def body(i, a1_ref, a2_ref, x_ref, gt_ref, sh_ref, sc_ref, w_ref, g_ref, xo_ref, y_ref, h_ref):
        y = jnp.dot(a1_ref[...], w_ref[:k1, :], preferred_element_type=F32)
        y = y + jnp.dot(a2_ref[...], w_ref[k1:, :], preferred_element_type=F32)
        y_ref[...] = y.astype(BF)
        xn = x_ref[...] + gt_ref[0] * y
        xo_ref[...] = xn
        h_ref[...] = _modnorm(xn, g_ref[...], sh_ref[0], sc_ref[0]).astype(BF)

    return _rb_call(name, body, row_in=(a1, a2, x), mod_in=((mods, m_gate), (mods, m_sh), (mods, m_sc)),
                    full_in=(w, g_next), row_out=((D, F32), (D, BF), (D, BF)), idle=idle if last else None)


def mlp_up(name, h, w1):
    def body(i, h_ref, w_ref, a_ref, f_ref):
        hv = h_ref[...]
        for n in range(4):
            a = jnp.dot(hv, w_ref[n], preferred_element_type=F32)
            a_ref[:, n * D:(n + 1) * D] = a.astype(BF)
            r = jnp.maximum(a, 0.0)
            f_ref[:, n * D:(n + 1) * D] = (r * r).astype(BF)

    return _rb_call(name, body, row_in=(h,), full_in=(w1,), row_out=((FF, BF), (FF, BF)), rows=WIDE)


def mlp_down(name, f, w2, x, mods, m_gate, nxt=None):
    def body(i, f_ref, x_ref, gt_ref, *rest):
        if nxt:
            sh_ref, sc_ref, w_ref, g_ref, xo_ref, y_ref, h_ref = rest
        else:
            w_ref, xo_ref, y_ref = rest
        y = jnp.dot(f_ref[:, 0:D], w_ref[0], preferred_element_type=F32)
        for n in range(1, 4):
            y = y + jnp.dot(f_ref[:, n * D:(n + 1) * D], w_ref[n], preferred_element_type=F32)
        xn = x_ref[...] + gt_ref[0] * y
        y_ref[...] = y.astype(BF)
        xo_ref[...] = xn
        if nxt:
            h_ref[...] = _modnorm(xn, g_ref[...], sh_ref[0], sc_ref[0]).astype(BF)

    if nxt:
        return _rb_call(name, body, row_in=(f, x), mod_in=((mods, m_gate), (nxt[0], 0), (nxt[0], 1)),
                        full_in=(w2, nxt[1]), row_out=((D, F32), (D, BF), (D, BF)))
    def idle(i, f_ref, x_ref, gt_ref, w_ref, xo_ref, y_ref):
        xo_ref[...] = x_ref[...]
        _zero(y_ref)

    return _rb_call(name, body, row_in=(f, x), mod_in=((mods, m_gate),), full_in=(w2,),
                    row_out=((D, F32), (D, BF)), idle=idle)


def mm_nt(name, g, w, after=None):
    k = w.shape[0]

    def body(i, g_ref, w_ref, o_ref):
        o_ref[...] = lax.dot_general(g_ref[...], w_ref[...], NT, preferred_element_type=F32).astype(BF)

    return _rb_call(name, body, row_in=(g,), full_in=(w,), row_out=((k, BF),), after=after, rows=WIDE)[0]


def mlp_bwd_da(name, dy, w2, a, after=None):
    def body(i, dy_ref, a_ref, w_ref, da_ref):
        dyv = dy_ref[...]
        for n in range(4):
            df = lax.dot_general(dyv, w_ref[n], NT, preferred_element_type=F32)
            av = a_ref[:, n * D:(n + 1) * D].astype(F32)
            da_ref[:, n * D:(n + 1) * D] = (df * (2.0 * jnp.maximum(av, 0.0))).astype(BF)

    return _rb_call(name, body, row_in=(dy, a), full_in=(w2,), row_out=((FF, BF),), after=after, rows=WIDE)[0]


def mlp_bwd_dh(name, da, w1):
    def body(i, da_ref, w_ref, dh_ref):
        acc = lax.dot_general(da_ref[:, 0:D], w_ref[0], NT, preferred_element_type=F32)
        for n in range(1, 4):
            acc = acc + lax.dot_general(da_ref[:, n * D:(n + 1) * D], w_ref[n], NT, preferred_element_type=F32)
        dh_ref[...] = acc.astype(BF)

    return _rb_call(name, body, row_in=(da,), full_in=(w1,), row_out=((D, BF),), rows=WIDE)[0]


TN_ROWS = 2304


def mm_tn(name, a, g, tiles, th, tw, rows=TN_ROWS):
    nt = len(tiles)
    acs = jnp.asarray([t[0] for t in tiles], jnp.int32)
    gcs = jnp.asarray([t[1] for t in tiles], jnp.int32)
    nr = R // rows

    def kern(ac_ref, gc_ref, a_ref, g_ref, o_ref, acc_ref):
        r = pl.program_id(1)

        @pl.when(r == 0)
        def _():
            acc_ref[...] = jnp.zeros_like(acc_ref)

        acc_ref[...] += lax.dot_general(a_ref[...], g_ref[...], TN, preferred_element_type=F32)

        @pl.when(r == nr - 1)
        def _():
            o_ref[...] = acc_ref[...].astype(BF)

    grid_spec = pltpu.PrefetchScalarGridSpec(
        num_scalar_prefetch=2, grid=(nt, nr),
        in_specs=[pl.BlockSpec((rows, th), lambda t, r, ac, gc: (r, ac[t])),
                  pl.BlockSpec((rows, tw), lambda t, r, ac, gc: (r, gc[t]))],
        out_specs=pl.BlockSpec((None, th, tw), lambda t, r, ac, gc: (t, 0, 0)),
        scratch_shapes=[pltpu.VMEM((th, tw), F32)])
    return pl.pallas_call(kern, grid_spec=grid_spec, out_shape=jax.ShapeDtypeStruct((nt, th, tw), BF),
                          compiler_params=_cparams(("parallel", "arbitrary")), name=name)(acs, gcs, a, g)


def _even_tok(q, k, zus, zvs, gq, gk, gss, ws, bs, cq, sq, ck, sk, avg, lo, hi):
    avg2 = avg[:128, :128]
    qr = _rope(_grmsn(q, gq, avg), cq, sq) * GQA_SCALE
    kr = _rope(_grmsn(k, gk, avg2), ck, sk)
    ms = []
    for b in range(4):
        v = _grmsn(_gelu(zvs[b]), gss[b], avg2)
        sv = lo * (_mm(ws[2 * b], v) + bs[2 * b]) + hi * (_mm(ws[2 * b + 1], v) + bs[2 * b + 1])
        ms.append(_gelu(zus[b]) * sv)
    return qr, kr, ms


def _even_operands(p_ref, rs, gq_ref, gk_ref, gs_ref, w_ref, b_ref):
    return (p_ref[rs, 0:512].astype(F32), p_ref[rs, 512:640].astype(F32),
            [p_ref[rs, 768 + 128 * b:896 + 128 * b].astype(F32) for b in range(4)],
            [p_ref[rs, 1280 + 128 * b:1408 + 128 * b].astype(F32) for b in range(4)],
            gq_ref[...], gk_ref[...], [gs_ref[:, 128 * b:128 * b + 128] for b in range(4)],
            [w_ref[g] for g in range(8)], [b_ref[g] for g in range(8)])


def even_tok_fwd(p, cos, sin, gq, gk, gs, sgu_w, sgu_b, avg, masks):
    def body(i, p_ref, cos_ref, sin_ref, gq_ref, gk_ref, gs_ref, w_ref, b_ref, avg_ref, mk_ref, q_ref, kv_ref, m_ref):
        avgv, lo, hi = avg_ref[...], mk_ref[0, :, 0:128], mk_ref[1, :, 0:128]
        for c in range(2):
            rs = pl.ds(c * 128, 128)
            qr, kr, ms = _even_tok(*_even_operands(p_ref, rs, gq_ref, gk_ref, gs_ref, w_ref, b_ref),
                                   cos_ref[rs, :], sin_ref[rs, :], cos_ref[rs, 0:128], sin_ref[rs, 0:128],
                                   avgv, lo, hi)
            q_ref[rs, :] = qr.astype(BF)
            kv_ref[rs, 0:128] = kr.astype(BF)
            kv_ref[rs, 128:256] = p_ref[rs, 640:768]
            for b in range(4):
                m_ref[rs, 128 * b:128 * b + 128] = ms[b].astype(BF)

    return _rb_call("even_tok_fwd", body, row_in=(p,), pos_in=(cos, sin),
                    full_in=(gq, gk, gs, sgu_w, sgu_b, avg, masks), row_out=((512, BF), (256, BF), (512, BF)))


def even_tok_bwd(p, dq, dkvt, dcat, cos, sin, gq, gk, gs, sgu_w, sgu_b, avg, masks):
    def body(i, p_ref, dq_ref, dcat_ref, dkvt_ref, cos_ref, sin_ref, gq_ref, gk_ref, gs_ref, w_ref, b_ref,
             avg_ref, mk_ref, dp_ref, dgq_ref, dgk_ref, dgs_ref, dw_ref, db_ref):
        avgv, lo, hi = avg_ref[...], mk_ref[0, :, 0:128], mk_ref[1, :, 0:128]
        tot = None
        for c in range(2):
            rs = pl.ds(c * 128, 128)
            cq, sq, ck, sk = cos_ref[rs, :], sin_ref[rs, :], cos_ref[rs, 0:128], sin_ref[rs, 0:128]

            def f(q, k, zus, zvs, gq, gk, gss, ws, bs):
                return _even_tok(q, k, zus, zvs, gq, gk, gss, ws, bs, cq, sq, ck, sk, avgv, lo, hi)

            _, vjp = jax.vjp(f, *_even_operands(p_ref, rs, gq_ref, gk_ref, gs_ref, w_ref, b_ref))
            dk = dkvt_ref[0:128, c * 128:(c + 1) * 128].T
            dv = dkvt_ref[128:256, c * 128:(c + 1) * 128].T
            dms = [dcat_ref[rs, 512 + 128 * b:640 + 128 * b].astype(F32) for b in range(4)]
            d = vjp((dq_ref[rs, :].astype(F32), dk, dms))
            dp_ref[rs, 0:512] = d[0].astype(BF)
            dp_ref[rs, 512:640] = d[1].astype(BF)
            dp_ref[rs, 640:768] = dv.astype(BF)
            for b in range(4):
                dp_ref[rs, 768 + 128 * b:896 + 128 * b] = d[2][b].astype(BF)
                dp_ref[rs, 1280 + 128 * b:1408 + 128 * b] = d[3][b].astype(BF)
            part = [d[4], d[5]] + list(d[6]) + list(d[7]) + list(d[8])
            tot = part if tot is None else [x + y for x, y in zip(tot, part)]
        refs = ([dgq_ref, dgk_ref] + [dgs_ref.at[:, 128 * b:128 * b + 128] for b in range(4)]
                + [dw_ref.at[g] for g in range(8)] + [db_ref.at[g] for g in range(8)])
        for ref, val in zip(refs, tot):
            _acc(ref, val, i == 0)

    return _rb_call("even_tok_bwd", body, row_in=(p, dq, dcat), col_in=(dkvt,), pos_in=(cos, sin),
                    full_in=(gq, gk, gs, sgu_w, sgu_b, avg, masks), row_out=((EV_IN, BF),),
                    acc_out=((1, 512), (1, 128), (1, 512), (8, 128, 128), (8, 128, 1)))


MLA_SCALE = 96 ** -0.5
GQA_SCALE = 64 ** -0.5


def _odd_tok(cq, ckv, kr, za, zg, gq, gkv, wq, wkk, wkv, spread, cr, sr, ck, sk):
    cqn = _rmsn(cq, gq)
    q = _rope(_mm(cqn, wq), cr, sr) * MLA_SCALE
    ckn = _rmsn(ckv, gkv)
    k = _mm(ckn, wkk) + _mm(_rope(kr, ck, sk), spread)
    v = _mm(ckn, wkv)
    y = za * jax.nn.sigmoid(zg)
    return q, k, v, y


def odd_tok_fwd(p, cos, sin, gq, gkv, wq, wkk, wkv, spread):
    def body(i, p_ref, cos_ref, sin_ref, gq_ref, gkv_ref, wq_ref, wkk_ref, wkv_ref, sp_ref, q_ref, kv_ref, y_ref):
        q, k, v, y = _odd_tok(
            p_ref[:, 0:256].astype(F32), p_ref[:, 256:384].astype(F32), p_ref[:, 384:512].astype(F32),
            p_ref[:, 512:1024].astype(F32), p_ref[:, 1024:1536].astype(F32),
            gq_ref[...], gkv_ref[...], wq_ref[...], wkk_ref[...], wkv_ref[...], sp_ref[...],
            cos_ref[:, 0:768], sin_ref[:, 0:768], cos_ref[:, 768:896], sin_ref[:, 768:896])
        q_ref[...] = q.astype(BF)
        kv_ref[:, 0:768] = k.astype(BF)
        kv_ref[:, 768:1280] = v.astype(BF)
        y_ref[...] = y.astype(BF)

    return _rb_call("odd_tok_fwd", body, row_in=(p,), pos_in=(cos, sin), full_in=(gq, gkv, wq, wkk, wkv, spread),
                    row_out=((768, BF), (1280, BF), (512, BF)))


def odd_tok_bwd(p, dq, dkvt, dy, cos, sin, gq, gkv, wq, wkk, wkv, spread):
    def body(i, p_ref, dq_ref, dy_ref, dkvt_ref, cos_ref, sin_ref, gq_ref, gkv_ref, wq_ref, wkk_ref, wkv_ref, sp_ref,
             dp_ref, dgq_ref, dgkv_ref, dwq_ref, dwkk_ref, dwkv_ref):
        cr, sr, ck, sk = cos_ref[:, 0:768], sin_ref[:, 0:768], cos_ref[:, 768:896], sin_ref[:, 768:896]
        spread_v = sp_ref[...]

        def f(cq, ckv, kr, za, zg, gq, gkv, wq, wkk, wkv):
            return _odd_tok(cq, ckv, kr, za, zg, gq, gkv, wq, wkk, wkv, spread_v, cr, sr, ck, sk)

        _, vjp = jax.vjp(f, p_ref[:, 0:256].astype(F32), p_ref[:, 256:384].astype(F32),
                         p_ref[:, 384:512].astype(F32), p_ref[:, 512:1024].astype(F32),
                         p_ref[:, 1024:1536].astype(F32), gq_ref[...], gkv_ref[...], wq_ref[...],
                         wkk_ref[...], wkv_ref[...])
        d = vjp((dq_ref[...].astype(F32), dkvt_ref[0:768, :].T, dkvt_ref[768:1280, :].T, dy_ref[...].astype(F32)))
        dp_ref[:, 0:256] = d[0].astype(BF)
        dp_ref[:, 256:384] = d[1].astype(BF)
        dp_ref[:, 384:512] = d[2].astype(BF)
        dp_ref[:, 512:1024] = d[3].astype(BF)
        dp_ref[:, 1024:1536] = d[4].astype(BF)
        for ref, val in zip((dgq_ref, dgkv_ref, dwq_ref, dwkk_ref, dwkv_ref), d[5:]):
            _acc(ref, val, i == 0)

    return _rb_call("odd_tok_bwd", body, row_in=(p, dq, dy), col_in=(dkvt,), pos_in=(cos, sin),
                    full_in=(gq, gkv, wq, wkk, wkv, spread), row_out=((OD_PAD, BF),),
                    acc_out=((1, 256), (1, 128), (256, 768), (128, 768), (128, 512)))


GQA_HEADS = [(64 * h, 64 * (h // 4), 64, 128 + 64 * (h // 4)) for h in range(8)]
MLA_HEADS = [(96 * h, 96 * h, 96, 768 + 64 * h) for h in range(8)]


def _by_block(j, run):
    @pl.when(j == 0)
    def _():
        run(LC)

    @pl.when(j > 0)
    def _():
        run(SEQ)


def attn_fwd(name, q, kv, heads):
    qw, kvw = q.shape[1], kv.shape[1]

    def kern(q_ref, kv_ref, o_ref, lse_ref):
        def run(nk):
            for h, (qo, ko, w, vo) in enumerate(heads):
                s = lax.dot_general(q_ref[:, qo:qo + w], kv_ref[0:nk, ko:ko + w], NT, preferred_element_type=F32)
                m = jnp.max(s, axis=-1, keepdims=True)
                p = jnp.exp(s - m)
                l = jnp.sum(p, axis=-1, keepdims=True)
                o = jnp.dot(p.astype(BF), kv_ref[0:nk, vo:vo + 64], preferred_element_type=F32) / l
                o_ref[:, 64 * h:64 * h + 64] = o.astype(BF)
                lse_ref[:, h:h + 1] = m + jnp.log(l)

        _by_block(pl.program_id(1), run)

    return pl.pallas_call(
        kern, grid=(NEX, BPE),
        in_specs=[pl.BlockSpec((TB, qw), lambda e, j: (e * BPE + j, 0)),
                  pl.BlockSpec((SEQ, kvw), lambda e, j: (e, 0))],
        out_specs=[pl.BlockSpec((TB, 512), lambda e, j: (e * BPE + j, 0)),
                   pl.BlockSpec((TB, 8), lambda e, j: (e * BPE + j, 0))],
        out_shape=[jax.ShapeDtypeStruct((R, 512), BF), jax.ShapeDtypeStruct((R, 8), F32)],
        compiler_params=_cparams(("parallel", "arbitrary")), name=name)(q, kv)


def attn_bwd(name, q, kv, o, dcat, lse, heads):
    qw, kvw = q.shape[1], kv.shape[1]

    def kern(q_ref, kv_ref, o_ref, do_ref, lse_ref, dq_ref, dkvt_ref):
        j = pl.program_id(1)

        @pl.when(j == 0)
        def _():
            dkvt_ref[...] = jnp.zeros_like(dkvt_ref)

        def run(nk):
            for h, (qo, ko, w, vo) in enumerate(heads):
                qh = q_ref[:, qo:qo + w]
                kh = kv_ref[0:nk, ko:ko + w]
                s = lax.dot_general(qh, kh, NT, preferred_element_type=F32)
                p = jnp.exp(s - lse_ref[:, h:h + 1])
                do = do_ref[:, 64 * h:64 * h + 64]
                dsum = jnp.sum(do.astype(F32) * o_ref[:, 64 * h:64 * h + 64].astype(F32), axis=-1, keepdims=True)
                dp = lax.dot_general(do, kv_ref[0:nk, vo:vo + 64], NT, preferred_element_type=F32)
                ds = (p * (dp - dsum)).astype(BF)
                dkvt_ref[vo:vo + 64, 0:nk] += lax.dot_general(do, p.astype(BF), TN, preferred_element_type=F32)
                dq_ref[:, qo:qo + w] = jnp.dot(ds, kh, preferred_element_type=F32).astype(BF)
                dkvt_ref[ko:ko + w, 0:nk] += lax.dot_general(qh, ds, TN, preferred_element_type=F32)

        _by_block(j, run)

    return pl.pallas_call(
        kern, grid=(NEX, BPE),
        in_specs=[pl.BlockSpec((TB, qw), lambda e, j: (e * BPE + j, 0)),
                  pl.BlockSpec((SEQ, kvw), lambda e, j: (e, 0)),
                  pl.BlockSpec((TB, 512), lambda e, j: (e * BPE + j, 0)),
                  pl.BlockSpec((TB, 512), lambda e, j: (e * BPE + j, 0)),
                  pl.BlockSpec((TB, 8), lambda e, j: (e * BPE + j, 0))],
        out_specs=[pl.BlockSpec((TB, qw), lambda e, j: (e * BPE + j, 0)),
                   pl.BlockSpec((kvw, SEQ), lambda e, j: (0, e))],
        out_shape=[jax.ShapeDtypeStruct((R, qw), BF), jax.ShapeDtypeStruct((kvw, R), F32)],
        compiler_params=_cparams(("parallel", "arbitrary")), name=name)(q, kv, o, dcat, lse)


HALO = 16
CONV_K = 31


def _fill_ext(ext_ref, prev_ref, cur_ref, next_ref, i):
    j = i % BPE
    has_prev = (j >= 2).astype(F32)
    has_next = jnp.logical_and(j >= 1, j <= BPE - 2).astype(F32)
    ext_ref[0:HALO, :] = prev_ref[TB - HALO:TB, :].astype(F32) * has_prev
    ext_ref[HALO:HALO + TB, :] = cur_ref[...].astype(F32)
    ext_ref[HALO + TB:2 * HALO + TB, :] = next_ref[0:HALO, :].astype(F32) * has_next


PHASE_ROWS = TB + 24


def _phases(ext_ref, ph_ref):
    for r in range(8):
        ph_ref[r] = ext_ref[r:r + PHASE_ROWS, :]


def _window(ph_ref, off):
    return ph_ref[off % 8, 8 * (off // 8):8 * (off // 8) + TB, :]


def _ln_silu(z, g, b):
    mu = jnp.mean(z, axis=-1, keepdims=True)
    zc = z - mu
    var = jnp.mean(zc * zc, axis=-1, keepdims=True)
    return _silu(zc * lax.rsqrt(var + EPS) * g + b)


def conf_fwd(y, cw, cb, lg, lb):
    def body(i, cur_ref, cw_ref, cb_ref, lg_ref, lb_ref, prev_ref, next_ref, z_ref, c_ref, ext_ref, ph_ref):
        _fill_ext(ext_ref, prev_ref, cur_ref, next_ref, i)
        _phases(ext_ref, ph_ref)
        acc = _window(ph_ref, 1) * cw_ref[0:1, :]
        for k in range(1, CONV_K):
            acc = acc + _window(ph_ref, k + 1) * cw_ref[k:k + 1, :]
        z = acc + cb_ref[...]
        z_ref[...] = z.astype(BF)
        c_ref[...] = _ln_silu(z, lg_ref[...], lb_ref[...]).astype(BF)

    def idle(i, cur_ref, cw_ref, cb_ref, lg_ref, lb_ref, prev_ref, next_ref, z_ref, c_ref, ext_ref, ph_ref):
        _zero(z_ref, c_ref)

    return _rb_call("conf_fwd", body, row_in=(y,), full_in=(cw, cb, lg, lb), shift_in=((y, -1), (y, 1)),
                    row_out=((512, BF), (512, BF)), idle=idle,
                    scratch=(pltpu.VMEM((TB + 2 * HALO, 512), F32), pltpu.VMEM((8, PHASE_ROWS, 512), F32)))


def conf_bwd_ln(z, dcat, lg, lb):
    def body(i, z_ref, dcat_ref, lg_ref, lb_ref, dz_ref, dlg_ref, dlb_ref, dcb_ref):
        _, vjp = jax.vjp(_ln_silu, z_ref[...].astype(F32), lg_ref[...], lb_ref[...])
        dz, dlg, dlb = vjp(dcat_ref[:, 512:1024].astype(F32))
        dz_ref[...] = dz.astype(BF)
        _acc(dlg_ref, dlg, i == 0)
        _acc(dlb_ref, dlb, i == 0)
        _acc(dcb_ref, jnp.sum(dz, axis=0, keepdims=True), i == 0)

    def idle(i, z_ref, dcat_ref, lg_ref, lb_ref, dz_ref, dlg_ref, dlb_ref, dcb_ref):
        _zero(dz_ref)
        _zero_at_start(i, dlg_ref, dlb_ref, dcb_ref)

    return _rb_call("conf_bwd_ln", body, row_in=(z, dcat), full_in=(lg, lb), row_out=((512, BF),),
                    acc_out=((1, 512), (1, 512), (1, 512)), idle=idle)


def conf_bwd_conv(y, dz, cw):
    def body(i, y_ref, dz_ref, cw_ref, yp_ref, yn_ref, dzp_ref, dzn_ref, dy_ref, dcw_ref, ext_ref, phy_ref, phd_ref):
        _fill_ext(ext_ref, yp_ref, y_ref, yn_ref, i)
        _phases(ext_ref, phy_ref)
        _fill_ext(ext_ref, dzp_ref, dz_ref, dzn_ref, i)
        _phases(ext_ref, phd_ref)
        dzv = dz_ref[...].astype(F32)

        @pl.when(i == 0)
        def _():
            dcw_ref[...] = jnp.zeros_like(dcw_ref)

        acc = None
        for k in range(CONV_K):
            t = _window(phd_ref, CONV_K - k) * cw_ref[k:k + 1, :]
            acc = t if acc is None else acc + t
            dcw_ref[k:k + 1, :] += jnp.sum(dzv * _window(phy_ref, k + 1), axis=0, keepdims=True)
        dy_ref[...] = acc.astype(BF)

    def idle(i, y_ref, dz_ref, cw_ref, yp_ref, yn_ref, dzp_ref, dzn_ref, dy_ref, dcw_ref, ext_ref, phy_ref, phd_ref):
        _zero(dy_ref)
        _zero_at_start(i, dcw_ref)

    return _rb_call("conf_bwd_conv", body, row_in=(y, dz), full_in=(cw,), idle=idle,
                    shift_in=((y, -1), (y, 1), (dz, -1), (dz, 1)), row_out=((512, BF),), acc_out=((32, 512),),
                    scratch=(pltpu.VMEM((TB + 2 * HALO, 512), F32), pltpu.VMEM((8, PHASE_ROWS, 512), F32),
                             pltpu.VMEM((8, PHASE_ROWS, 512), F32)))


def final_loss(x, target, fg, y, mods, m_gate):
    lpb = L // TB

    def kern(x_ref, t_ref, g_ref, y_ref, gt_ref, dx_ref, dy_ref, dgt_ref, loss_ref, dg_ref):
        i = pl.program_id(0)

        @pl.when((i % BPE) == 0)
        def _():
            _zero(dx_ref, dy_ref, dgt_ref)
            _zero_at_start(i, loss_ref, dg_ref)

        @pl.when((i % BPE) >= 1)
        def _():
            tv = t_ref[...]

            def f(x, g):
                err = _rmsn(x, g) - tv
                rowsum = jnp.sum(err * err, axis=-1, keepdims=True)
                return jnp.sum(rowsum, axis=0, keepdims=True) * (0.5 / D)

            lv, vjp = jax.vjp(f, x_ref[...], g_ref[...])
            dx, dg = vjp(jnp.ones((1, 1), F32))
            dx_ref[...] = dx
            _gate_grads(dx, y_ref, gt_ref, dy_ref, dgt_ref, i)
            loss_ref[...] += jnp.zeros((8, 128), F32) + lv
            dg_ref[...] += dg

    row = pl.BlockSpec((TB, D), lambda i: (i, 0))
    return pl.pallas_call(
        kern, grid=(NBLK,),
        in_specs=[row, pl.BlockSpec((TB, D), lambda i: ((i // BPE) * lpb + jnp.maximum(i % BPE - 1, 0), 0)),
                  pl.BlockSpec((1, D), lambda i: (0, 0)), row,
                  pl.BlockSpec((1, 1, D), lambda i: (_seg(i) * N_MOD + m_gate, 0, 0))],
        out_specs=[row, row, pl.BlockSpec((1, 1, D), lambda i: (_seg(i), 0, 0)),
                   pl.BlockSpec((8, 128), lambda i: (0, 0)), pl.BlockSpec((1, D), lambda i: (0, 0))],
        out_shape=[jax.ShapeDtypeStruct((R, D), F32), jax.ShapeDtypeStruct((R, D), BF),
                   jax.ShapeDtypeStruct((4, 1, D), F32), jax.ShapeDtypeStruct((8, 128), F32),
                   jax.ShapeDtypeStruct((1, D), F32)],
        compiler_params=_cparams(("arbitrary",)), name="final_loss")(x, target, fg, y, mods)


NC = 24


def mods_fwd(call, ada_w, ada_b):
    cols = ada_w.shape[2]

    def kern(c_ref, w_ref, b_ref, o_ref):
        o_ref[...] = jnp.dot(_silu(c_ref[...]), w_ref[...], precision=HI, preferred_element_type=F32) + b_ref[...]

    return pl.pallas_call(
        kern, grid=(2,),
        in_specs=[pl.BlockSpec((NC, D), lambda l: (0, 0)), pl.BlockSpec((None, D, cols), lambda l: (l, 0, 0)),
                  pl.BlockSpec((None, 1, cols), lambda l: (l, 0, 0))],
        out_specs=pl.BlockSpec((None, NC, cols), lambda l: (l, 0, 0)),
        out_shape=jax.ShapeDtypeStruct((2, NC, cols), F32),
        compiler_params=_cparams(("parallel",)), name="mods_fwd")(call, ada_w, ada_b)


def ada_bwd(call, ada_w, dm):
    cols = ada_w.shape[2]

    def kern(c_ref, w_ref, dm_ref, gw_ref, dc_ref):
        l = pl.program_id(0)
        gw_ref[...] = lax.dot_general(_silu(c_ref[...]), dm_ref[...], TN, precision=HI, preferred_element_type=F32)
        part = lax.dot_general(dm_ref[16:24, :], w_ref[...], NT, precision=HI, preferred_element_type=F32)
        cc = c_ref[16:17, :]
        sg = jax.nn.sigmoid(cc)
        _acc(dc_ref, part * (sg * (1.0 + cc * (1.0 - sg))), l == 0)

    return pl.pallas_call(
        kern, grid=(2,),
        in_specs=[pl.BlockSpec((NC, D), lambda l: (0, 0)), pl.BlockSpec((None, D, cols), lambda l: (l, 0, 0)),
                  pl.BlockSpec((None, NC, cols), lambda l: (l, 0, 0))],
        out_specs=[pl.BlockSpec((None, D, cols), lambda l: (l, 0, 0)), pl.BlockSpec((8, D), lambda l: (0, 0))],
        out_shape=[jax.ShapeDtypeStruct((2, D, cols), F32), jax.ShapeDtypeStruct((8, D), F32)],
        compiler_params=_cparams(("arbitrary",)), name="ada_bwd")(call, ada_w, dm)


def sum_lead(name, a, after=None):
    n, r, c = a.shape
    tr = r
    for cand in (512, 256, 128, 64, 32, 16, 8):
        if r % cand == 0 and cand * c * 4 * n <= 8 * 1024 * 1024:
            tr = cand
            break
    extra = [] if after is None else [after]

    def kern(a_ref, *rest):
        acc = a_ref[0].astype(F32)
        for k in range(1, n):
            acc = acc + a_ref[k].astype(F32)
        rest[-1][...] = acc

    return pl.pallas_call(
        kern, grid=(r // tr,),
        in_specs=[pl.BlockSpec((n, tr, c), lambda i: (0, i, 0))]
        + [pl.BlockSpec(e.shape, lambda i, k=e.ndim: (0,) * k) for e in extra],
        out_specs=pl.BlockSpec((tr, c), lambda i: (i, 0)), out_shape=jax.ShapeDtypeStruct((r, c), F32),
        compiler_params=_cparams(("parallel",)), name=name)(a, *extra)


def add_pairs(name, hs, got, half):
    _, _, r, c = hs.shape

    def kern(half_ref, a_ref, b_ref, o_ref):
        o_ref[...] = (a_ref[...].astype(F32) + b_ref[...].astype(F32)).astype(BF)

    spec = pl.BlockSpec((None, r, c), lambda j, h: (j, 0, 0))
    grid_spec = pltpu.PrefetchScalarGridSpec(
        num_scalar_prefetch=1, grid=(4,),
        in_specs=[pl.BlockSpec((None, None, r, c), lambda j, h: (h[0], j, 0, 0)), spec], out_specs=spec)
    return pl.pallas_call(kern, grid_spec=grid_spec, out_shape=jax.ShapeDtypeStruct(got.shape, BF),
                          compiler_params=_cparams(("parallel",)), name=name)(half, hs, got)


def sum_slabs(name, land, own, where, full, lead):
    _, r, c = land.shape
    tr = r
    for cand in (512, 256, 128, 64, 32, 16):
        if r % cand == 0 and cand * c * 16 <= 4 * 1024 * 1024:
            tr = cand
            break

    def kern(where_ref, full_ref, land_ref, own_ref, o_ref):
        me = where_ref[0]
        acc = None
        for k in range(4):
            t = jnp.where(me == k, own_ref[k], land_ref[k]).astype(F32)
            acc = t if acc is None else acc + t
        o_ref[...] = acc

    spec = pl.BlockSpec((4, tr, c), lambda i, m: (0, i, 0))
    grid_spec = pltpu.PrefetchScalarGridSpec(
        num_scalar_prefetch=1, grid=(r // tr,), in_specs=[pl.BlockSpec(memory_space=pl.ANY), spec, spec],
        out_specs=pl.BlockSpec((None, None, tr, c), lambda i, m: (lead, m[1], i, 0)))
    return pl.pallas_call(kern, grid_spec=grid_spec, out_shape=jax.ShapeDtypeStruct(full.shape, F32),
                          input_output_aliases={1: 0}, compiler_params=_cparams(("parallel",)),
                          name=name)(where, full, land, own)


def adamw(name, w, g, m, v, again=False):
    r, c = w.shape
    tr = r
    for cand in (512, 256, 128, 64, 32, 16, 8):
        if r % cand == 0 and cand * c * 4 <= 2 * 1024 * 1024:
            tr = cand
            break
    c1 = 1.0 / (1.0 - ADAM_B1 ** ADAM_STEP)
    c2 = 1.0 / (1.0 - ADAM_B2 ** ADAM_STEP)

    def kern(w_ref, g_ref, m_ref, v_ref, d_ref, mo_ref, vo_ref, *go_ref):
        gv = g_ref[...]
        mn = ADAM_B1 * m_ref[...] + (1.0 - ADAM_B1) * gv
        vn = ADAM_B2 * v_ref[...] + (1.0 - ADAM_B2) * (gv * gv)
        d_ref[...] = -ADAM_LR * ((mn * c1) / (jnp.sqrt(vn * c2) + ADAM_EPS) + ADAM_WD * w_ref[...])
        mo_ref[...] = mn
        vo_ref[...] = vn
        if again:
            go_ref[0][...] = gv

    spec = pl.BlockSpec((tr, c), lambda i: (i, 0))
    shp = jax.ShapeDtypeStruct((r, c), F32)
    n_out = 4 if again else 3
    return pl.pallas_call(kern, grid=(r // tr,), in_specs=[spec] * 4, out_specs=[spec] * n_out,
                          out_shape=[shp] * n_out, compiler_params=_cparams(("parallel",)), name=name)(w, g, m, v)


def adamw_many(name, ws, gs, ms, vs):
    n = len(ws)
    c1 = 1.0 / (1.0 - ADAM_B1 ** ADAM_STEP)
    c2 = 1.0 / (1.0 - ADAM_B2 ** ADAM_STEP)

    def kern(*refs):
        w, g, m, v, d, mo, vo = (refs[k * n:(k + 1) * n] for k in range(7))
        for k in range(n):
            gv = g[k][...]
            mn = ADAM_B1 * m[k][...] + (1.0 - ADAM_B1) * gv
            vn = ADAM_B2 * v[k][...] + (1.0 - ADAM_B2) * (gv * gv)
            d[k][...] = -ADAM_LR * ((mn * c1) / (jnp.sqrt(vn * c2) + ADAM_EPS) + ADAM_WD * w[k][...])
            mo[k][...] = mn
            vo[k][...] = vn

    out = pl.pallas_call(kern, out_shape=[jax.ShapeDtypeStruct(a.shape, F32) for a in ws] * 3,
                         compiler_params=pltpu.CompilerParams(vmem_limit_bytes=VMEM_LIMIT),
                         name=name)(*ws, *gs, *ms, *vs)
    return out[:n], out[n:2 * n], out[2 * n:]


def all_gather8(name, xs, after=None):
    m_per, n = xs.shape
    extra = [] if after is None else [after]

    def body(x_ref, *rest):
        out_ref, send_sems, recv_sems, local_sem = rest[len(extra):]
        x, y, c = lax.axis_index("x"), lax.axis_index("y"), lax.axis_index("c")
        me, sibling = (x, y, c), (x, y, 1 - c)
        chips = [(1 - x, y), (x, 1 - y), (1 - x, 1 - y)]

        def rows(px, py, pc):
            return out_ref.at[pl.ds((4 * px + 2 * py + pc) * m_per, m_per), :]

        def copy(k, block, to, src=None):
            return pltpu.make_async_remote_copy(
                src_ref=rows(*block) if src is None else src, dst_ref=rows(*block),
                send_sem=send_sems.at[k], recv_sem=recv_sems.at[k], device_id=to, device_id_type=MESH)

        mine = pltpu.make_async_copy(x_ref, rows(*me), local_sem)
        mine.start()
        first = [copy(0, me, sibling, src=x_ref)]
        first += [copy(1 + j, me, (*chip, c), src=x_ref) for j, chip in enumerate(chips)]
        for cp in first:
            cp.start()
        passed = [copy(4 + j, (*chip, c), sibling) for j, chip in enumerate(chips)]
        for j, chip in enumerate(chips):
            copy(1 + j, (*chip, c), me).wait_recv()
            passed[j].start()
        copy(0, sibling, me).wait_recv()
        for j, chip in enumerate(chips):
            copy(4 + j, (*chip, 1 - c), me).wait_recv()
        for cp in first + passed:
            cp.wait_send()
        mine.wait()

    return pl.pallas_call(
        body, out_shape=jax.ShapeDtypeStruct((8 * m_per, n), xs.dtype),
        in_specs=[pl.BlockSpec(memory_space=pltpu.VMEM)] * (1 + len(extra)),
        out_specs=pl.BlockSpec(memory_space=pltpu.VMEM),
        scratch_shapes=[pltpu.SemaphoreType.DMA((7,)), pltpu.SemaphoreType.DMA((7,)), pltpu.SemaphoreType.DMA],
        compiler_params=pltpu.CompilerParams(vmem_limit_bytes=VMEM_LIMIT), name=name)(xs, *extra)


def sibling_merge(name, fulls):
    n = len(fulls)
    slots = [(a, l) for a in range(n) for l in range(fulls[a].shape[0])]

    def body(*refs):
        buf = refs[n:2 * n]
        send_sems, recv_sems = refs[2 * n], refs[2 * n + 1]
        c = lax.axis_index("c")
        sibling = (lax.axis_index("x"), lax.axis_index("y"), 1 - c)
        sends, recvs = [], []
        for k, (a, l) in enumerate(slots):
            kw = dict(send_sem=send_sems.at[k], recv_sem=recv_sems.at[k], device_id=sibling, device_id_type=MESH)
            sends.append(pltpu.make_async_remote_copy(src_ref=buf[a].at[l, c], dst_ref=buf[a].at[l, c], **kw))
            recvs.append(pltpu.make_async_remote_copy(src_ref=buf[a].at[l, c], dst_ref=buf[a].at[l, 1 - c], **kw))
        for cp in sends:
            cp.start()
        for cp in recvs:
            cp.wait_recv()
        for cp in sends:
            cp.wait_send()

    anyspec = pl.BlockSpec(memory_space=pl.ANY)
    return pl.pallas_call(
        body, out_shape=[jax.ShapeDtypeStruct(s.shape, s.dtype) for s in fulls],
        in_specs=[anyspec] * n, out_specs=[anyspec] * n, input_output_aliases={a: a for a in range(n)},
        scratch_shapes=[pltpu.SemaphoreType.DMA((len(slots),)), pltpu.SemaphoreType.DMA((len(slots),))],
        name=name)(*fulls)


def place_own(name, land, src, chip):
    c = src.shape[-1]
    r = src.size // c
    tr = r
    for cand in (1024, 512, 256, 128, 64, 32, 16):
        if r % cand == 0 and cand * c * 2 <= 2 * 1024 * 1024:
            tr = cand
            break

    def kern(chip_ref, land_ref, src_ref, out_ref):
        out_ref[...] = src_ref[...]

    grid_spec = pltpu.PrefetchScalarGridSpec(
        num_scalar_prefetch=1, grid=(r // tr,),
        in_specs=[pl.BlockSpec(memory_space=pl.ANY), pl.BlockSpec((tr, c), lambda i, m: (i, 0))],
        out_specs=pl.BlockSpec((None, tr, c), lambda i, m: (m[0], i, 0)))
    out = pl.pallas_call(kern, grid_spec=grid_spec, out_shape=jax.ShapeDtypeStruct((4, r, c), land.dtype),
                         input_output_aliases={1: 0}, compiler_params=_cparams(("parallel",)),
                         name=name)(chip, land.reshape(4, r, c), src.reshape(r, c))
    return out.reshape(land.shape)


def _half_copies(src, land, send_sems, recv_sems):
    c = lax.axis_index("c")
    sibling = (lax.axis_index("x"), lax.axis_index("y"), 1 - c)
    pairs = []
    for a in range(len(src)):
        cp = pltpu.make_async_remote_copy(src_ref=src[a].at[1 - c], dst_ref=land[a], send_sem=send_sems.at[a],
                                          recv_sem=recv_sems.at[a], device_id=sibling, device_id_type=MESH)
        pairs.append((cp, cp))
    return pairs


def _chip_copies(src, land, send_sems, recv_sems, scatter):
    x, y, c = lax.axis_index("x"), lax.axis_index("y"), lax.axis_index("c")
    me = 2 * x + y
    pairs = []
    for a in range(len(src)):
        for j, (px, py) in enumerate([(1 - x, y), (x, 1 - y), (1 - x, 1 - y)]):
            to = 2 * px + py
            out = src[a].at[to] if scatter else src[a]
            kw = dict(send_sem=send_sems.at[3 * a + j], recv_sem=recv_sems.at[3 * a + j], device_id=(px, py, c),
                      device_id_type=MESH)
            pairs.append((pltpu.make_async_remote_copy(src_ref=out, dst_ref=land[a].at[me], **kw),
                          pltpu.make_async_remote_copy(src_ref=out, dst_ref=land[a].at[to], **kw)))
    return pairs


_HBM = pl.BlockSpec(memory_space=pltpu.HBM)
_SEM = pl.BlockSpec(memory_space=pltpu.SEMAPHORE)


GATHER = (functools.partial(_chip_copies, scatter=False), 3)
SCATTER = (functools.partial(_chip_copies, scatter=True), 3)
TO_SIBLING = (_half_copies, 1)


def exchange_start(name, groups, plan):
    copies, per = plan
    sizes = [len(s) for s, _ in groups]
    flat = [a for s, l in groups for a in list(s) + list(l)]
    ng = len(groups)

    def body(*refs):
        ins, outs = refs[:len(flat)], refs[len(flat):]
        off = 0
        for g, n in enumerate(sizes):
            src, land = ins[off:off + n], ins[off + n:off + 2 * n]
            off += 2 * n
            for send, _ in copies(src, land, outs[2 * g], outs[2 * g + 1]):
                send.start()
        outs[-1][...] = jnp.zeros_like(outs[-1])

    out_shape = []
    for n in sizes:
        out_shape += [pltpu.SemaphoreType.DMA((per * n,)), pltpu.SemaphoreType.DMA((per * n,))]
    out_shape += [pltpu.HBM(a.shape, a.dtype) for a in flat] + [jax.ShapeDtypeStruct((8, 128), F32)]
    res = pl.pallas_call(
        body, out_shape=tuple(out_shape), in_specs=[_HBM] * len(flat),
        out_specs=tuple([_SEM] * (2 * ng) + [_HBM] * len(flat) + [pl.BlockSpec(memory_space=pltpu.VMEM)]),
        input_output_aliases={k: 2 * ng + k for k in range(len(flat))},
        compiler_params=pltpu.CompilerParams(has_side_effects=pltpu.SideEffectType.DATAFLOW_SIDE_EFFECTING),
        name=name)(*[pltpu.with_memory_space_constraint(a, pltpu.HBM) for a in flat])
    handles, off = [], 2 * ng
    for g, n in enumerate(sizes):
        handles.append((res[2 * g], res[2 * g + 1], list(res[off:off + n]), list(res[off + n:off + 2 * n])))
        off += 2 * n
    return handles, res[-1]


def exchange_wait(name, handle, after, plan):
    send_sems, recv_sems, srcs, lands = handle
    n = len(srcs)

    def body(*refs):
        src, land = refs[:n], refs[n:2 * n]
        for send, recv in plan[0](src, land, refs[2 * n], refs[2 * n + 1]):
            send.wait_send()
            recv.wait_recv()

    res = pl.pallas_call(
        body, out_shape=tuple(pltpu.HBM(a.shape, a.dtype) for a in srcs + lands),
        in_specs=[_HBM] * (2 * n) + [_SEM, _SEM, pl.BlockSpec(memory_space=pl.ANY)],
        out_specs=tuple([_HBM] * (2 * n)), input_output_aliases={k: k for k in range(2 * n)},
        compiler_params=pltpu.CompilerParams(has_side_effects=pltpu.SideEffectType.DATAFLOW_SIDE_EFFECTING),
        name=name)(*srcs, *lands, send_sems, recv_sems, after)
    return list(res[:n]), list(res[n:])


def _rope_tables(d_rot, reps):
    rows = L // GRID_W
    row = np.repeat(np.arange(rows), GRID_W).astype(np.float32)
    col = np.tile(np.arange(GRID_W), rows).astype(np.float32)
    d_axis = d_rot // 2
    inv = (ROPE_THETA ** (-np.arange(0, d_axis, 2, dtype=np.float32) / d_axis)).astype(np.float32)
    ang = np.concatenate([row[:, None] * inv, col[:, None] * inv], axis=-1).astype(np.float32)
    cos, sin = np.cos(ang).astype(np.float32), np.sin(ang).astype(np.float32)
    c = np.repeat(cos, 2, axis=-1)
    s = np.stack([-sin, sin], axis=-1).reshape(L, d_rot)
    c = np.concatenate([np.ones((LC, d_rot), np.float32), c], axis=0)
    s = np.concatenate([np.zeros((LC, d_rot), np.float32), s], axis=0)
    return np.tile(c, (1, reps)), np.tile(s, (1, reps))


def _group_consts():
    g = np.arange(512) // 64
    avg = (g[:, None] == g[None, :]).astype(np.float32) / 64.0
    masks = (np.arange(8)[:, None] == g[None, :]).astype(np.float32).reshape(8, 1, 512)
    return jnp.asarray(avg, BF), jnp.asarray(masks)


def _pack(items):
    flat = jnp.concatenate([a.reshape(-1).astype(F32) for a in items])
    n = flat.shape[0]
    rows = -(-n // D)
    rows = -(-rows // 8) * 8
    return jnp.pad(flat, (0, rows * D - n)).reshape(rows, D)


def _unpack(buf, shapes):
    lead = buf.shape[:-2]
    flat = buf.reshape(lead + (-1,))
    out, off = [], 0
    for shp in shapes:
        n = int(np.prod(shp))
        out.append(flat[..., off:off + n].reshape(lead + tuple(shp)))
        off += n
    return out


def _arrive(prm, key, after):
    if callable(prm[key]):
        prm[key](after)
    return prm[key]


def _layer_fwd(i, x, h, mods, prm, consts, nxt):
    sv = {}
    sv["x0"] = x
    sv["h"] = h
    p = proj_in(f"proj_in_{i}", h, _arrive(prm, "w_in", h))
    sv["p"] = p
    if i == 0:
        q, kv, m2 = even_tok_fwd(p, consts["cos_e"], consts["sin_e"], prm["gq"], prm["gk"], prm["gs"],
                                 prm["sgu_w"], prm["sgu_b"], consts["avg"], consts["masks"])
        o, lse = attn_fwd("attn_fwd_0", q, kv, GQA_HEADS)
        sv.update(q=q, kv=kv)
    else:
        q, kv, y = odd_tok_fwd(p, consts["cos_o"], consts["sin_o"], prm["gq"], prm["gkv"], prm["wq"], prm["wkk"],
                               prm["wkv"], consts["spread"])
        o, lse = attn_fwd("attn_fwd_1", q, kv, MLA_HEADS)
        z, m2 = conf_fwd(y, prm["conv_w"], prm["conv_b"], prm["ln_g"], prm["ln_b"])
        sv.update(q=q, kv=kv, y=y, z=z)
    sv.update(o=o, lse=lse, m2=m2)
    x1, y1, h2 = proj_out(f"proj_out_{i}", o, m2, _arrive(prm, "w_out", o), x, mods, 2, prm["norm2_g"], 3, 4,
                          last=nxt is None)
    sv.update(x1=x1, y1=y1)
    a, f = mlp_up(f"mlp_up_{i}", h2, _arrive(prm, "w1", h2))
    x2, y2, *h_next = mlp_down(f"mlp_down_{i}", f, prm["w2"], x1, mods, 5, nxt)
    sv.update(h2=h2, a=a, f=f, y2=y2)
    return x2, (h_next[0] if h_next else None), sv


def _layer_bwd(i, dx, dy2, dg2, sv, mods, prm, consts, hook, entry, below):
    gr = {}
    da = mlp_bwd_da(f"mlp_bwd_da_{i}", dy2, prm["w2"], sv["a"], after=entry)
    tiles8 = [(h, j) for h in range(2) for j in range(4)]
    gr["w1"] = mm_tn(f"grad_w1_{i}", sv["h2"], da, tiles8, 512, D, rows=R).reshape(2, 4, 512, D)
    gr["w2"] = mm_tn(f"grad_w2_{i}", sv["f"], dy2, [(2 * j + h, 0) for h in range(2) for j in range(4)],
                     512, D, rows=R).reshape(2, 4, 512, D)
    dh2 = mlp_bwd_dh(f"mlp_bwd_dh_{i}", da, prm["w1"])
    dx1, dy1, dg1, dsh2, dsc2, gr["norm2_g"] = modnorm_bwd(
        f"norm2_bwd_{i}", sv["x1"], dh2, dx, mods, prm["norm2_g"], 3, 4, gate=(sv["y1"], mods, 2),
        after=hook(f"{i}:mlp", gr, dh2), last=below is not None)
    dcat = mm_nt(f"proj_out_bwd_{i}", dy1, prm["w_out"], after=hook(f"{i}:mid", gr, dy1))
    go = mm_tn(f"grad_wout_a_{i}", sv["o"], dy1, [(0, 0)], 512, D, rows=R).reshape(2, 2, 128, D)
    gm = mm_tn(f"grad_wout_b_{i}", sv["m2"], dy1, [(0, 0)], 512, D, rows=R).reshape(2, 2, 128, D)
    gr["w_out"] = jnp.concatenate([go, gm], axis=0).transpose(1, 0, 2, 3)
    if i == 0:
        dq, dkv = attn_bwd("attn_bwd_0", sv["q"], sv["kv"], sv["o"], dcat, sv["lse"], GQA_HEADS)
        dp, gr["gq"], gr["gk"], gr["gs"], gr["sgu_w"], gr["sgu_b"] = even_tok_bwd(
            sv["p"], dq, dkv, dcat, consts["cos_e"], consts["sin_e"], prm["gq"], prm["gk"],
            prm["gs"], prm["sgu_w"], prm["sgu_b"], consts["avg"], consts["masks"])
    else:
        dq, dkv = attn_bwd("attn_bwd_1", sv["q"], sv["kv"], sv["o"], dcat, sv["lse"], MLA_HEADS)
        dz, gr["ln_g"], gr["ln_b"], gr["conv_b"] = conf_bwd_ln(sv["z"], dcat, prm["ln_g"], prm["ln_b"])
        dyc, gr["conv_w"] = conf_bwd_conv(sv["y"], dz, prm["conv_w"])
        dp, gr["gq"], gr["gkv"], gr["wq"], gr["wkk"], gr["wkv"] = odd_tok_bwd(
            sv["p"], dq, dkv, dyc, consts["cos_o"], consts["sin_o"], prm["gq"], prm["gkv"], prm["wq"], prm["wkk"],
            prm["wkv"], consts["spread"])
    n_in = prm["w_in"].shape[1]
    gr["w_in"] = mm_tn(f"grad_win_{i}", sv["h"], dp, [(0, 0), (1, 0)], 512, n_in)
    dh = mm_nt(f"proj_in_bwd_{i}", dp, prm["w_in"])
    if below:
        dx0, dy2b, dg2b, dsh1, dsc1, gr["norm1_g"] = modnorm_bwd(
            f"norm1_bwd_{i}", sv["x0"], dh, dx1, mods, prm["norm1_g"], 0, 1, gate=(below[0], below[1], 5))
        down = (dy2b, dg2b)
    else:
        dx0, dsh1, dsc1, gr["norm1_g"] = modnorm_bwd(f"norm1_bwd_{i}", sv["x0"], dh, dx1, mods, prm["norm1_g"], 0, 1,
                                                     lat_only=True)
        down = None
    dmods = jnp.concatenate([dsh1, dsc1, dg1, dsh2, dsc2, dg2], axis=1)
    return dx0, down, dmods, gr, hook(f"{i}:end", gr, dx0)


def local_step(xcat, target, mods, prms, final_g, hook=lambda point, grads, fresh: None):
    avg, masks = _group_consts()
    cos_e, sin_e = _rope_tables(64, 8)
    ck, sk = _rope_tables(32, 1)
    one64, zero64 = np.ones((SEQ, 64), np.float32), np.zeros((SEQ, 64), np.float32)
    one96, zero96 = np.ones((SEQ, 96), np.float32), np.zeros((SEQ, 96), np.float32)
    cos_o = np.concatenate([np.tile(np.concatenate([one64, ck], axis=1), (1, 8)), ck, one96], axis=1)
    sin_o = np.concatenate([np.tile(np.concatenate([zero64, sk], axis=1), (1, 8)), sk, zero96], axis=1)
    lane = np.arange(768)
    spread = np.zeros((128, 768), np.float32)
    spread[lane % 96 - 64, lane] = (lane % 96 >= 64)
    consts = dict(avg=avg, masks=masks, cos_e=jnp.asarray(cos_e), sin_e=jnp.asarray(sin_e),
                  cos_o=jnp.asarray(cos_o), sin_o=jnp.asarray(sin_o), spread=jnp.asarray(spread, BF))
    x = xcat
    h = modnorm_fwd("norm1_fwd_0", x, mods[0], prms[0]["norm1_g"], 0, 1)
    saved = []
    for i in range(2):
        x, h, sv = _layer_fwd(i, x, h, mods[i], prms[i], consts, (mods[1], prms[1]["norm1_g"]) if i == 0 else None)
        saved.append(sv)
    dx, dy2, dg2, loss, dfg = final_loss(x, target, final_g, saved[1]["y2"], mods[1], 5)
    dmods, grads = [None, None], [None, None]
    entry, down = None, (dy2, dg2)
    for i in (1, 0):
        below = (saved[0]["y2"], mods[0]) if i == 1 else None
        dx, down, dmods[i], grads[i], entry = _layer_bwd(i, dx, down[0], down[1], saved[i], mods[i], prms[i], consts,
                                                         hook, entry, below)
    return loss, dx, dmods, grads, dfg, entry


def _row(v):
    return v.reshape(1, -1).astype(F32)


def odd_in_params(od_w_in, w_uq, w_ukv):
    od = jnp.concatenate([od_w_in[:, 0:416], jnp.zeros((D, 96), od_w_in.dtype), od_w_in[:, 416:OD_IN]], axis=1)
    ukv = w_ukv.reshape(128, 8, 128)
    wkk = jnp.pad(ukv[:, :, :64], ((0, 0), (0, 0), (0, 32))).reshape(128, 768)
    return dict(w_in=od, wq=w_uq, wkk=wkk, wkv=ukv[:, :, 64:].reshape(128, 512))


def small_params(small):
    p0 = dict(norm1_g=_row(small["norm1_g"][0]), norm2_g=_row(small["norm2_g"][0]),
              gq=jnp.tile(_row(small["ev_q_norm_g"]), (1, 8)), gk=jnp.tile(_row(small["ev_k_norm_g"]), (1, 2)),
              gs=_row(small["ev_sgu_norm_g"]), sgu_w=small["ev_sgu_w"].reshape(8, 128, 128).astype(F32),
              sgu_b=small["ev_sgu_b"].reshape(8, 128, 1).astype(F32))
    p1 = dict(norm1_g=_row(small["norm1_g"][1]), norm2_g=_row(small["norm2_g"][1]),
              gq=_row(small["od_q_norm_g"]), gkv=_row(small["od_kv_norm_g"]),
              conv_w=jnp.pad(small["od_conv_w"].reshape(CONV_K, 512).astype(F32), ((0, 1), (0, 0))),
              conv_b=_row(small["od_conv_b"]), ln_g=_row(small["od_ln_g"]), ln_b=_row(small["od_ln_b"]))
    return [p0, p1]


def small_grads_natural(grads, dfg):
    g0, g1 = grads
    return dict(
        norm1_g=jnp.concatenate([g0["norm1_g"], g1["norm1_g"]], axis=0),
        norm2_g=jnp.concatenate([g0["norm2_g"], g1["norm2_g"]], axis=0),
        ev_q_norm_g=g0["gq"].reshape(8, 64).sum(0).reshape(1, 64),
        ev_k_norm_g=g0["gk"].reshape(2, 64).sum(0).reshape(1, 64),
        ev_sgu_norm_g=g0["gs"].reshape(1, 8, 64),
        ev_sgu_w=g0["sgu_w"].reshape(1, 8, 128, 128),
        ev_sgu_b=g0["sgu_b"].reshape(1, 8, 128),
        od_q_norm_g=g1["gq"].reshape(1, 256),
        od_kv_norm_g=g1["gkv"].reshape(1, 128),
        od_conv_w=g1["conv_w"][0:CONV_K].reshape(1, CONV_K, 512),
        od_conv_b=g1["conv_b"].reshape(1, 512),
        od_ln_g=g1["ln_g"].reshape(1, 512),
        od_ln_b=g1["ln_b"].reshape(1, 512),
        final_g=dfg.reshape(D))


def layer_grads_hs(i, g, part="all"):
    def cols(a):
        k, n = a.shape
        return a.reshape(2, k // 2, 4, n // 4).transpose(0, 2, 1, 3).astype(BF)

    mlp = [(("mlp_w1", i), g["w1"]), (("mlp_w2", i), g["w2"])]
    if part == "mlp":
        return mlp
    rest = [(("w_out", i), g["w_out"])]
    if i == 0:
        rest.append((("ev_w_in", 0), cols(g["w_in"].reshape(D, EV_IN))))
    else:
        od = g["w_in"].reshape(D, OD_PAD)
        od = jnp.concatenate([od[:, 0:416], od[:, 512:OD_PAD]], axis=1)
        ukv = jnp.concatenate([g["wkk"].reshape(128, 8, 96)[:, :, :64], g["wkv"].reshape(128, 8, 64)], axis=2)
        rest += [(("od_w_in", 0), cols(od)), (("od_w_uq", 0), cols(g["wq"])),
                 (("od_w_ukv", 0), cols(ukv.reshape(128, 1024)))]
    return rest if part == "rest" else mlp + rest


WEIGHT_NAMES = ['c_ctx', 'ada_w', 'ada_b', 'norm1_g', 'norm2_g', 'w_out', 'mlp_w1', 'mlp_w2', 'ev_w_in',
                'ev_q_norm_g', 'ev_k_norm_g', 'ev_sgu_norm_g', 'ev_sgu_w', 'ev_sgu_b', 'od_w_in', 'od_q_norm_g',
                'od_kv_norm_g', 'od_w_uq', 'od_w_ukv', 'od_conv_w', 'od_conv_b', 'od_ln_g', 'od_ln_b', 'final_g']
REPL_SMALL = ['norm1_g', 'norm2_g', 'ev_q_norm_g', 'ev_k_norm_g', 'ev_sgu_norm_g', 'ev_sgu_w', 'ev_sgu_b',
              'od_kv_norm_g', 'final_g']
SHARD_SMALL = ['od_q_norm_g', 'od_conv_w', 'od_conv_b', 'od_ln_g', 'od_ln_b']
BIG = ['w_out', 'mlp_w1', 'mlp_w2', 'ev_w_in', 'od_w_in', 'od_w_uq', 'od_w_ukv']


def _gather_last(parts):
    return jnp.concatenate([parts[k] for k in range(4)], axis=-1)


class _Reduce:
    def __init__(self, tag, named, half, where):
        self.tag, self.half, self.where = tag, half, where
        self.names, self.hs = zip(*named)
        self.hs = list(self.hs)

    def to_sibling(self):
        lands = [lax.empty(a.shape[1:], BF) for a in self.hs]
        (self.h1,), token = exchange_start(f"rs_sibling_start_{self.tag}", [(self.hs, lands)], TO_SIBLING)
        return token

    def to_chips(self, after):
        hs, got = exchange_wait(f"rs_sibling_wait_{self.tag}", self.h1, after, TO_SIBLING)
        pair = [add_pairs(f"rs_add_{self.tag}_{k}", a, b, self.half) for k, (a, b) in enumerate(zip(hs, got))]
        lands = [lax.empty(p.shape, BF) for p in pair]
        (self.h2,), token = exchange_start(f"rs_chips_start_{self.tag}", [(pair, lands)], SCATTER)
        return token

    def finish(self, after, bufs):
        pair, land = exchange_wait(f"rs_chips_wait_{self.tag}", self.h2, after, SCATTER)
        for k, ((n, idx), l, p) in enumerate(zip(self.names, land, pair)):
            bufs[n] = sum_slabs(f"rs_sum_{self.tag}_{k}", l, p, self.where, bufs[n], idx)


def kernel(x, c, ctx, c_ctx, ada_w, ada_b, norm1_g, norm2_g, w_out, mlp_w1, mlp_w2, ev_w_in, ev_q_norm_g, ev_k_norm_g, ev_sgu_norm_g, ev_sgu_w, ev_sgu_b, od_w_in, od_q_norm_g, od_kv_norm_g, od_w_uq, od_w_ukv, od_conv_w, od_conv_b, od_ln_g, od_ln_b, final_g, loss_target, m_c_ctx, m_ada_w, m_ada_b, m_norm1_g, m_norm2_g, m_w_out, m_mlp_w1, m_mlp_w2, m_ev_w_in, m_ev_q_norm_g, m_ev_k_norm_g, m_ev_sgu_norm_g, m_ev_sgu_w, m_ev_sgu_b, m_od_w_in, m_od_q_norm_g, m_od_kv_norm_g, m_od_w_uq, m_od_w_ukv, m_od_conv_w, m_od_conv_b, m_od_ln_g, m_od_ln_b, m_final_g, v_c_ctx, v_ada_w, v_ada_b, v_norm1_g, v_norm2_g, v_w_out, v_mlp_w1, v_mlp_w2, v_ev_w_in, v_ev_q_norm_g, v_ev_k_norm_g, v_ev_sgu_norm_g, v_ev_sgu_w, v_ev_sgu_b, v_od_w_in, v_od_q_norm_g, v_od_kv_norm_g, v_od_w_uq, v_od_w_ukv, v_od_conv_w, v_od_conv_b, v_od_ln_g, v_od_ln_b, v_final_g):
    w = dict(c_ctx=c_ctx, ada_w=ada_w, ada_b=ada_b, norm1_g=norm1_g, norm2_g=norm2_g, w_out=w_out, mlp_w1=mlp_w1,
             mlp_w2=mlp_w2, ev_w_in=ev_w_in, ev_q_norm_g=ev_q_norm_g, ev_k_norm_g=ev_k_norm_g,
             ev_sgu_norm_g=ev_sgu_norm_g, ev_sgu_w=ev_sgu_w, ev_sgu_b=ev_sgu_b, od_w_in=od_w_in,
             od_q_norm_g=od_q_norm_g, od_kv_norm_g=od_kv_norm_g, od_w_uq=od_w_uq, od_w_ukv=od_w_ukv,
             od_conv_w=od_conv_w, od_conv_b=od_conv_b, od_ln_g=od_ln_g, od_ln_b=od_ln_b, final_g=final_g)
    mom = dict(c_ctx=m_c_ctx, ada_w=m_ada_w, ada_b=m_ada_b, norm1_g=m_norm1_g, norm2_g=m_norm2_g, w_out=m_w_out,
               mlp_w1=m_mlp_w1, mlp_w2=m_mlp_w2, ev_w_in=m_ev_w_in, ev_q_norm_g=m_ev_q_norm_g,
               ev_k_norm_g=m_ev_k_norm_g, ev_sgu_norm_g=m_ev_sgu_norm_g, ev_sgu_w=m_ev_sgu_w, ev_sgu_b=m_ev_sgu_b,
               od_w_in=m_od_w_in, od_q_norm_g=m_od_q_norm_g, od_kv_norm_g=m_od_kv_norm_g, od_w_uq=m_od_w_uq,
               od_w_ukv=m_od_w_ukv, od_conv_w=m_od_conv_w, od_conv_b=m_od_conv_b, od_ln_g=m_od_ln_g,
               od_ln_b=m_od_ln_b, final_g=m_final_g)
    var = dict(c_ctx=v_c_ctx, ada_w=v_ada_w, ada_b=v_ada_b, norm1_g=v_norm1_g, norm2_g=v_norm2_g, w_out=v_w_out,
               mlp_w1=v_mlp_w1, mlp_w2=v_mlp_w2, ev_w_in=v_ev_w_in, ev_q_norm_g=v_ev_q_norm_g,
               ev_k_norm_g=v_ev_k_norm_g, ev_sgu_norm_g=v_ev_sgu_norm_g, ev_sgu_w=v_ev_sgu_w, ev_sgu_b=v_ev_sgu_b,
               od_w_in=v_od_w_in, od_q_norm_g=v_od_q_norm_g, od_kv_norm_g=v_od_kv_norm_g, od_w_uq=v_od_w_uq,
               od_w_ukv=v_od_w_ukv, od_conv_w=v_od_conv_w, od_conv_b=v_od_conv_b, od_ln_g=v_od_ln_g,
               od_ln_b=v_od_ln_b, final_g=v_final_g)
    xi, yi, ci = lax.axis_index("x"), lax.axis_index("y"), lax.axis_index("c")
    chip = 2 * xi + yi
    dev = 2 * chip + ci

    shard_shapes = [w[n].shape for n in SHARD_SMALL]
    g0 = all_gather8("ag_small", _pack([c] + [w[n] for n in SHARD_SMALL]))
    g0 = g0.reshape(8, -1, D)
    parts = _unpack(g0, [c.shape] + shard_shapes)
    c_all = parts[0].reshape(16, D)
    small_full = {n: _gather_last(p[0::2]) for n, p in zip(SHARD_SMALL, parts[1:])}
    call = jnp.concatenate([c_all, c_ctx.reshape(1, D), jnp.zeros((NC - 17, D), F32)], axis=0)

    cols = ada_w.shape[2]
    ada_b_sh = lax.dynamic_slice(ada_b, (0, chip * cols), (2, cols)).reshape(2, 1, cols)
    mt = mods_fwd(call, ada_w, ada_b_sh)
    mt = all_gather8("ag_mods", mt.reshape(2 * NC, cols)).reshape(8, 2, NC, cols)
    table = mt[0::2].transpose(1, 2, 0, 3).reshape(2, NC, 4 * cols)
    mods = []
    for i in range(2):
        lat = lax.dynamic_slice(table[i], (2 * dev, 0), (2, 4 * cols))
        mc = table[i, 16]
        mods.append(jnp.stack([mc, lat[0], mc, lat[1]]).reshape(4 * N_MOD, 1, D))

    order = [[("ev_w_in", 0)], [("w_out", 0), ("mlp_w1", 0), ("mlp_w2", 0)],
             [("od_w_in", 0), ("od_w_uq", 0), ("od_w_ukv", 0), ("w_out", 1)], [("mlp_w1", 1), ("mlp_w2", 1)]]
    groups = []
    for names in order:
        srcs = [w[n][i].astype(BF) for n, i in names]
        groups.append((srcs, [lax.empty((4,) + s.shape, BF) for s in srcs]))
    groups[0][0][0], table = lax.optimization_barrier((groups[0][0][0], table))
    handles, token = exchange_start("gather_start", groups, GATHER)
    mods[0] = mods[0] + token[0, 0]
    small = {n: w[n] for n in REPL_SMALL}
    small.update(small_full)
    prms = small_params(small)

    chip1 = chip.reshape(1).astype(jnp.int32)

    def arrived(k, after):
        srcs, lands = exchange_wait(f"gather_wait_{k}", handles[k], after, GATHER)
        return [place_own(f"gather_own_{k}_{a}", l, s, chip1) for a, (l, s) in enumerate(zip(lands, srcs))]

    def arrive_ev_in(after):
        (ev,) = arrived(0, after)
        prms[0]["w_in"] = _gather_last(ev)

    def arrive_ev_rest(after):
        wo, w1, w2 = arrived(1, after)
        prms[0].update(w_out=wo.reshape(D, D), w1=w1, w2=w2)

    def arrive_od(after):
        od, uq, ukv, wo = arrived(2, after)
        prms[1].update(odd_in_params(_gather_last(od), _gather_last(uq), _gather_last(ukv)), w_out=wo.reshape(D, D))

    def arrive_od_mlp(after):
        w1, w2 = arrived(3, after)
        prms[1].update(w1=w1, w2=w2)

    prms[0]["w_in"] = arrive_ev_in
    prms[0]["w_out"] = arrive_ev_rest
    prms[1]["w_in"] = arrive_od
    prms[1]["w1"] = arrive_od_mlp

    half = ci.reshape(1).astype(jnp.int32)
    where = jnp.stack([chip, ci]).astype(jnp.int32)
    red = {}

    def hook(point, g, fresh):
        if point == "1:end":
            red["l1"] = _Reduce("l1", layer_grads_hs(1, g, "all"), half, where)
            return red["l1"].to_sibling()
        if point == "0:mlp":
            red["l0_mlp"] = _Reduce("l0_mlp", layer_grads_hs(0, g, "mlp"), half, where)
            return red["l1"].to_chips(fresh) + red["l0_mlp"].to_sibling()
        if point == "0:mid":
            return red["l0_mlp"].to_chips(fresh)
        if point == "0:end":
            red["l0_rest"] = _Reduce("l0_rest", layer_grads_hs(0, g, "rest"), half, where)
            return red["l0_rest"].to_sibling()
        return None

    xin = (ctx.reshape(NEX * LC, D), x.reshape(NEX * L, D))
    loss_p, dx, dmods, grads, dfg, last = local_step(xin, loss_target.reshape(NEX * L, D), mods, prms,
                                                     final_g.reshape(1, D), hook)
    grad_x = dx.reshape(NEX, L, D)

    sg = small_grads_natural(grads, dfg)
    dm = jnp.stack([d.reshape(4, N_MOD * D) for d in dmods])
    small_names = REPL_SMALL + SHARD_SMALL
    items = [dm[:, 1::2], dm[:, 0] + dm[:, 2]] + [sg[n] for n in small_names] + [loss_p[0:1, 0:1]]
    shapes = [a.shape for a in items]
    g1 = all_gather8("ag_grads", _pack(items), after=last)
    started = red["l0_rest"].to_chips(g1)
    rows1 = g1.shape[0] // 8
    g1 = g1.reshape(8, rows1, D)
    tot = _unpack(sum_lead("sum_small", g1, after=started), shapes)
    dm_lat = _unpack(g1, shapes[:1])[0]
    dm_lat = dm_lat.transpose(1, 0, 2, 3).reshape(2, 16, N_MOD * D)
    dm_all = jnp.concatenate([dm_lat, tot[1][:, None], jnp.zeros((2, NC - 17, N_MOD * D), F32)], axis=1)
    gsum = dict(zip(small_names, tot[2:2 + len(small_names)]))
    loss = tot[-1].reshape(())
    grad = {n: gsum[n].reshape(w[n].shape) for n in REPL_SMALL}
    for n in SHARD_SMALL:
        k = w[n].shape[-1]
        grad[n] = lax.dynamic_slice_in_dim(gsum[n], chip * k, k, axis=gsum[n].ndim - 1)
    grad["ada_b"] = sum_lead("sum_ada_b", dm_all.transpose(1, 0, 2).reshape(NC, 2 * N_MOD, D)).reshape(2, N_MOD * D)

    dm_sh = lax.dynamic_slice(dm_all, (0, 0, chip * cols), (2, NC, cols))
    grad["ada_w"], dcc = ada_bwd(call, ada_w, dm_sh)
    dcc = all_gather8("ag_cctx", dcc).reshape(8, 8, D)
    grad["c_ctx"] = sum_lead("sum_cctx", dcc[0::2])[0]

    delta, new_m, new_v = {}, {}, {}

    def adam_big(n, again):
        shp = w[n].shape
        two_d = (shp[0] * shp[1], shp[2])
        res = adamw(f"adamw_{n}", w[n].reshape(two_d), grad[n].reshape(two_d), mom[n].reshape(two_d),
                    var[n].reshape(two_d), again)
        delta[n], new_m[n], new_v[n] = [a.reshape(shp) for a in res[:3]]
        if again:
            grad[n] = res[3].reshape(shp)

    adam_big('ada_w', False)
    rest = [n for n in WEIGHT_NAMES if n not in ['ada_w'] + BIG]
    flat2 = lambda a: a.reshape(-1, a.shape[-1])
    outs = adamw_many("adamw_small", [flat2(w[n]) for n in rest], [flat2(grad[n]) for n in rest],
                      [flat2(mom[n]) for n in rest], [flat2(var[n]) for n in rest])
    for dst, arrs in zip((delta, new_m, new_v), outs):
        dst.update({n: a.reshape(w[n].shape) for n, a in zip(rest, arrs)})
    d_ = outs[0][1]

    bufs = {n: lax.empty((w[n].shape[0], 2, w[n].shape[1] // 2, w[n].shape[2]), F32) for n in BIG}
    for tag, behind in (("l1", delta["ada_w"]), ("l0_mlp", d_), ("l0_rest", d_)):
        red[tag].finish(behind, bufs)
    for n, full in zip(BIG, sibling_merge("rs_sibling_merge", [bufs[n] for n in BIG])):
        grad[n] = full.reshape(w[n].shape)
    for n in BIG:
        adam_big(n, True)

    return (loss, grad_x, *[grad[n] for n in WEIGHT_NAMES], *[delta[n] for n in WEIGHT_NAMES],
            *[new_m[n] for n in WEIGHT_NAMES], *[new_v[n] for n in WEIGHT_NAMES])
```

```python
import functools

import numpy as np
import jax
import jax.numpy as jnp
from jax import lax
from jax.experimental import pallas as pl
from jax.experimental.pallas import tpu as pltpu

F32 = jnp.float32
BF = jnp.bfloat16
HI = lax.Precision.HIGHEST
MESH = pl.DeviceIdType.MESH

D = 1024
L = 2048
LC = 256
SEQ = L + LC
NEX = 2
R = NEX * SEQ
TB = 256
WIDE = 512
BPE = SEQ // TB
NBLK = R // TB
GRID_W = 64
FF = 4 * D
EPS = 1e-6
ROPE_THETA = 10000.0
N_MOD = 6
EV_IN = 1792
OD_IN = 1440
OD_PAD = 1536
VMEM_LIMIT = 60 * 1024 * 1024

ADAM_LR = 0.001
ADAM_B1 = 0.9
ADAM_B2 = 0.999
ADAM_EPS = 1e-08
ADAM_WD = 0.01
ADAM_STEP = 10

NT = (((1,), (1,)), ((), ()))
TN = (((0,), (0,)), ((), ()))


def _cparams(sem=None):
    return pltpu.CompilerParams(dimension_semantics=sem, vmem_limit_bytes=VMEM_LIMIT)


@jax.custom_vjp
def _mm(a, b):
    return jnp.dot(a.astype(BF), b.astype(BF), preferred_element_type=F32)


def _mm_fwd(a, b):
    return _mm(a, b), (a, b)


def _mm_bwd(res, g):
    a, b = res
    gb = g.astype(BF)
    da = lax.dot_general(gb, b.astype(BF), NT, preferred_element_type=F32)
    db = lax.dot_general(a.astype(BF), gb, TN, preferred_element_type=F32)
    return da, db


_mm.defvjp(_mm_fwd, _mm_bwd)


@jax.custom_vjp
def _swap(x):
    n = x.shape[-1]
    ax = x.ndim - 1
    lane = lax.broadcasted_iota(jnp.int32, x.shape, ax)
    return jnp.where(lane % 2 == 0, pltpu.roll(x, n - 1, ax), pltpu.roll(x, 1, ax))


_swap.defvjp(lambda x: (_swap(x), None), lambda _, g: (_swap(g),))


def _rope(x, cos, sin):
    return x * cos + _swap(x) * sin


def _rmsn(x, g):
    return x * lax.rsqrt(jnp.mean(x * x, axis=-1, keepdims=True) + EPS) * g


def _split_dot(a, m):
    hi = a.astype(BF)
    lo = (a - hi.astype(F32)).astype(BF)
    return jnp.dot(hi, m, preferred_element_type=F32) + jnp.dot(lo, m, preferred_element_type=F32)


@jax.custom_vjp
def _group_mean(a, avg):
    return _split_dot(a, avg)


_group_mean.defvjp(lambda a, avg: (_split_dot(a, avg), avg),
                   lambda avg, g: (_split_dot(g, avg), jnp.zeros_like(avg)))


def _grmsn(x, g, avg):
    return x * lax.rsqrt(_group_mean(x * x, avg) + EPS) * g


def _modnorm(x, g, sh, sc):
    return _rmsn(x, g) * (1.0 + sc) + sh


def _gelu(x):
    return 0.5 * x * (1.0 + jnp.tanh(0.7978845608028654 * (x + 0.044715 * (x * x * x))))


def _silu(x):
    return x * jax.nn.sigmoid(x)


def _acc(ref, val, first):
    @pl.when(first)
    def _():
        ref[...] = val

    @pl.when(jnp.logical_not(first))
    def _():
        ref[...] += val


def _seg(i):
    return 2 * (i // BPE) + jnp.minimum(i % BPE, 1)


def _seg_first(i):
    return (i % BPE) <= 1


class _Either:
    def __init__(self, pick_first, first, second):
        self.pick_first, self.first, self.second = pick_first, first, second

    def __getitem__(self, idx):
        return jnp.where(self.pick_first, self.first[idx], self.second[idx])


def _rb_call(name, body, row_in=(), mod_in=(), pos_in=(), full_in=(), shift_in=(),
             row_out=(), seg_out=(), acc_out=(), scratch=(), after=None, col_in=(), rows=TB, idle=None):
    assert rows == TB or not (mod_in or pos_in or shift_in or seg_out or col_in)
    in_specs, args, pairs = [], [], []
    for a in row_in:
        if isinstance(a, tuple):
            pairs.append(len(args))
            in_specs.append(pl.BlockSpec((TB, a[0].shape[1]), lambda i: (i // BPE, 0)))
            in_specs.append(pl.BlockSpec(
                (TB, a[1].shape[1]), lambda i: ((i // BPE) * (L // TB) + jnp.maximum(i % BPE - 1, 0), 0)))
            args += list(a)
        else:
            in_specs.append(pl.BlockSpec((rows, a.shape[1]), lambda i: (i, 0)))
            args.append(a)
    for a in col_in:
        in_specs.append(pl.BlockSpec((a.shape[0], TB), lambda i: (0, i)))
        args.append(a)
    for tab, m in mod_in:
        in_specs.append(pl.BlockSpec((1, 1, D), lambda i, m=m: (_seg(i) * N_MOD + m, 0, 0)))
        args.append(tab)
    for a in pos_in:
        in_specs.append(pl.BlockSpec((TB, a.shape[1]), lambda i: (i % BPE, 0)))
        args.append(a)
    for a in full_in:
        in_specs.append(pl.BlockSpec(a.shape, lambda i, n=a.ndim: (0,) * n))
        args.append(a)
    for a, d in shift_in:
        in_specs.append(pl.BlockSpec((TB, a.shape[1]), lambda i, d=d: (jnp.clip(i + d, 0, NBLK - 1), 0)))
        args.append(a)
    n_in = len(args)
    if after is not None:
        in_specs.append(pl.BlockSpec(after.shape, lambda i, n=after.ndim: (0,) * n))
        args.append(after)
    out_specs, out_shape = [], []
    for w, dt, *lat in row_out:
        if lat:
            out_specs.append(pl.BlockSpec(
                (TB, w), lambda i: ((i // BPE) * (L // TB) + jnp.maximum(i % BPE - 1, 0), 0)))
            out_shape.append(jax.ShapeDtypeStruct((NEX * L, w), dt))
        else:
            out_specs.append(pl.BlockSpec((rows, w), lambda i: (i, 0)))
            out_shape.append(jax.ShapeDtypeStruct((R, w), dt))
    for w in seg_out:
        out_specs.append(pl.BlockSpec((1, 1, w), lambda i: (_seg(i), 0, 0)))
        out_shape.append(jax.ShapeDtypeStruct((4, 1, w), F32))
    for shp in acc_out:
        out_specs.append(pl.BlockSpec(shp, lambda i, n=len(shp): (0,) * n))
        out_shape.append(jax.ShapeDtypeStruct(shp, F32))

    def kern(*refs):
        i = pl.program_id(0)
        ins = list(refs[:n_in])
        for k in reversed(pairs):
            ins[k:k + 2] = [_Either((i % BPE) == 0, ins[k], ins[k + 1])]
        if idle is None:
            body(i, *ins, *refs[len(args):])
        else:
            @pl.when((i % BPE) >= 1)
            def _():
                body(i, *ins, *refs[len(args):])

            @pl.when((i % BPE) == 0)
            def _():
                idle(i, *ins, *refs[len(args):])

    sem = ("arbitrary",) if (seg_out or acc_out or any(len(r) > 2 for r in row_out)) else ("parallel",)
    return pl.pallas_call(kern, grid=(R // rows,), in_specs=in_specs, out_specs=out_specs, out_shape=out_shape,
                          scratch_shapes=list(scratch), compiler_params=_cparams(sem), name=name)(*args)


def modnorm_fwd(name, x, mods, g, m_sh, m_sc):
    def body(i, x_ref, sh_ref, sc_ref, g_ref, h_ref):
        h_ref[...] = _modnorm(x_ref[...], g_ref[...], sh_ref[0], sc_ref[0]).astype(BF)

    return _rb_call(name, body, row_in=(x,), mod_in=((mods, m_sh), (mods, m_sc)), full_in=(g,),
                    row_out=((D, BF),))[0]


def _gate_grads(dx, y_ref, gt_ref, dy_ref, dgt_ref, i):
    dy_ref[...] = (dx * gt_ref[0]).astype(BF)
    _acc(dgt_ref, jnp.sum(dx * y_ref[...].astype(F32), axis=0, keepdims=True)[None], _seg_first(i))


def modnorm_bwd(name, x, dh, dx_in, mods, g, m_sh, m_sc, gate=None, after=None, lat_only=False, last=False):
    def body(i, x_ref, dh_ref, dxin_ref, *rest):
        if gate:
            y_ref, sh_ref, sc_ref, gt_ref, g_ref, dx_ref, dy_ref, dgt_ref, dsh_ref, dsc_ref, dg_ref = rest
        else:
            sh_ref, sc_ref, g_ref, dx_ref, dsh_ref, dsc_ref, dg_ref = rest
        _, vjp = jax.vjp(_modnorm, x_ref[...], g_ref[...], sh_ref[0], sc_ref[0])
        dx, dg, dsh, dsc = vjp(dh_ref[...].astype(F32))
        dx = dxin_ref[...] + dx
        dx_ref[...] = dx
        if gate:
            _gate_grads(dx, y_ref, gt_ref, dy_ref, dgt_ref, i)
        _acc(dsh_ref, dsh[None], _seg_first(i))
        _acc(dsc_ref, dsc[None], _seg_first(i))
        _acc(dg_ref, dg, i == 0)

    def idle(i, x_ref, dh_ref, dxin_ref, y_ref, sh_ref, sc_ref, gt_ref, g_ref, dx_ref, dy_ref, dgt_ref, dsh_ref,
             dsc_ref, dg_ref):
        dx_ref[...] = dxin_ref[...]
        _zero(dy_ref, dgt_ref, dsh_ref, dsc_ref)
        _zero_at_start(i, dg_ref)

    if gate:
        y, gmods, m = gate
        return _rb_call(name, body, row_in=(x, dh, dx_in, y), mod_in=((mods, m_sh), (mods, m_sc), (gmods, m)),
                        full_in=(g,), row_out=((D, F32), (D, BF)), seg_out=(D, D, D), acc_out=((1, D),), after=after,
                        idle=idle if last else None)
    return _rb_call(name, body, row_in=(x, dh, dx_in), mod_in=((mods, m_sh), (mods, m_sc)), full_in=(g,),
                    row_out=((D, F32, "lat") if lat_only else (D, F32),), seg_out=(D, D), acc_out=((1, D),),
                    after=after)


def proj_in(name, h, w):
    n = w.shape[1]

    def body(i, h_ref, w_ref, o_ref):
        o_ref[...] = jnp.dot(h_ref[...], w_ref[...], preferred_element_type=F32).astype(BF)

    return _rb_call(name, body, row_in=(h,), full_in=(w,), row_out=((n, BF),), rows=WIDE)[0]


def _zero(*refs):
    for r in refs:
        r[...] = jnp.zeros_like(r)


def _zero_at_start(i, *refs):
    @pl.when(i == 0)
    def _():
        _zero(*refs)


def proj_out(name, a1, a2, w, x, mods, m_gate, g_next, m_sh, m_sc, last=False):
    k1 = a1.shape[1]

    def idle(i, a1_ref, a2_ref, x_ref, gt_ref, sh_ref, sc_ref, w_ref, g_ref, xo_ref, y_ref, h_ref):
        xo_ref[...] = x_ref[...]
        _zero(y_ref, h_ref)

    def body(i, a1_ref, a2_ref, x_ref, gt_ref, sh_ref, sc_ref, w_ref, g_ref, xo_ref, y_ref, h_ref):
        y = jnp.dot(a1_ref[...], w_ref[:k1, :], preferred_element_type=F32)
        y = y + jnp.dot(a2_ref[...], w_ref[k1:, :], preferred_element_type=F32)
        y_ref[...] = y.astype(BF)
        xn = x_ref[...] + gt_ref[0] * y
        xo_ref[...] = xn
        h_ref[...] = _modnorm(xn, g_ref[...], sh_ref[0], sc_ref[0]).astype(BF)

    return _rb_call(name, body, row_in=(a1, a2, x), mod_in=((mods, m_gate), (mods, m_sh), (mods, m_sc)),
                    full_in=(w, g_next), row_out=((D, F32), (D, BF), (D, BF)), idle=idle if last else None)


def mlp_up(name, h, w1):
    def body(i, h_ref, w_ref, a_ref, f_ref):
        hv = h_ref[...]
        for n in range(4):
            a = jnp.dot(hv, w_ref[n], preferred_element_type=F32)
            a_ref[:, n * D:(n + 1) * D] = a.astype(BF)
            r = jnp.maximum(a, 0.0)
            f_ref[:, n * D:(n + 1) * D] = (r * r).astype(BF)

    return _rb_call(name, body, row_in=(h,), full_in=(w1,), row_out=((FF, BF), (FF, BF)), rows=WIDE)


def mlp_down(name, f, w2, x, mods, m_gate, nxt=None):
    def body(i, f_ref, x_ref, gt_ref, *rest):
        if nxt:
            sh_ref, sc_ref, w_ref, g_ref, xo_ref, y_ref, h_ref = rest
        else:
            w_ref, xo_ref, y_ref = rest
        y = jnp.dot(f_ref[:, 0:D], w_ref[0], preferred_element_type=F32)
        for n in range(1, 4):
            y = y + jnp.dot(f_ref[:, n * D:(n + 1) * D], w_ref[n], preferred_element_type=F32)
        xn = x_ref[...] + gt_ref[0] * y
        y_ref[...] = y.astype(BF)
        xo_ref[...] = xn
        if nxt:
            h_ref[...] = _modnorm(xn, g_ref[...], sh_ref[0], sc_ref[0]).astype(BF)

    if nxt:
        return _rb_call(name, body, row_in=(f, x), mod_in=((mods, m_gate), (nxt[0], 0), (nxt[0], 1)),
                        full_in=(w2, nxt[1]), row_out=((D, F32), (D, BF), (D, BF)))
    def idle(i, f_ref, x_ref, gt_ref, w_ref, xo_ref, y_ref):
        xo_ref[...] = x_ref[...]
        _zero(y_ref)

    return _rb_call(name, body, row_in=(f, x), mod_in=((mods, m_gate),), full_in=(w2,),
                    row_out=((D, F32), (D, BF)), idle=idle)


def mm_nt(name, g, w, after=None):
    k = w.shape[0]

    def body(i, g_ref, w_ref, o_ref):
        o_ref[...] = lax.dot_general(g_ref[...], w_ref[...], NT, preferred_element_type=F32).astype(BF)

    return _rb_call(name, body, row_in=(g,), full_in=(w,), row_out=((k, BF),), after=after, rows=WIDE)[0]


def mlp_bwd_da(name, dy, w2, a, after=None):
    def body(i, dy_ref, a_ref, w_ref, da_ref):
        dyv = dy_ref[...]
        for n in range(4):
            df = lax.dot_general(dyv, w_ref[n], NT, preferred_element_type=F32)
            av = a_ref[:, n * D:(n + 1) * D].astype(F32)
            da_ref[:, n * D:(n + 1) * D] = (df * (2.0 * jnp.maximum(av, 0.0))).astype(BF)

    return _rb_call(name, body, row_in=(dy, a), full_in=(w2,), row_out=((FF, BF),), after=after, rows=WIDE)[0]


def mlp_bwd_dh(name, da, w1):
    def body(i, da_ref, w_ref, dh_ref):
        acc = lax.dot_general(da_ref[:, 0:D], w_ref[0], NT, preferred_element_type=F32)
        for n in range(1, 4):
            acc = acc + lax.dot_general(da_ref[:, n * D:(n + 1) * D], w_ref[n], NT, preferred_element_type=F32)
        dh_ref[...] = acc.astype(BF)

    return _rb_call(name, body, row_in=(da,), full_in=(w1,), row_out=((D, BF),), rows=WIDE)[0]


TN_ROWS = 2304


def mm_tn(name, a, g, tiles, th, tw, rows=TN_ROWS):
    nt = len(tiles)
    acs = jnp.asarray([t[0] for t in tiles], jnp.int32)
    gcs = jnp.asarray([t[1] for t in tiles], jnp.int32)
    nr = R // rows

    def kern(ac_ref, gc_ref, a_ref, g_ref, o_ref, acc_ref):
        r = pl.program_id(1)

        @pl.when(r == 0)
        def _():
            acc_ref[...] = jnp.zeros_like(acc_ref)

        acc_ref[...] += lax.dot_general(a_ref[...], g_ref[...], TN, preferred_element_type=F32)

        @pl.when(r == nr - 1)
        def _():
            o_ref[...] = acc_ref[...].astype(BF)

    grid_spec = pltpu.PrefetchScalarGridSpec(
        num_scalar_prefetch=2, grid=(nt, nr),
        in_specs=[pl.BlockSpec((rows, th), lambda t, r, ac, gc: (r, ac[t])),
                  pl.BlockSpec((rows, tw), lambda t, r, ac, gc: (r, gc[t]))],
        out_specs=pl.BlockSpec((None, th, tw), lambda t, r, ac, gc: (t, 0, 0)),
        scratch_shapes=[pltpu.VMEM((th, tw), F32)])
    return pl.pallas_call(kern, grid_spec=grid_spec, out_shape=jax.ShapeDtypeStruct((nt, th, tw), BF),
                          compiler_params=_cparams(("parallel", "arbitrary")), name=name)(acs, gcs, a, g)


def _even_tok(q, k, zus, zvs, gq, gk, gss, ws, bs, cq, sq, ck, sk, avg, lo, hi):
    avg2 = avg[:128, :128]
    qr = _rope(_grmsn(q, gq, avg), cq, sq) * GQA_SCALE
    kr = _rope(_grmsn(k, gk, avg2), ck, sk)
    ms = []
    for b in range(4):
        v = _grmsn(_gelu(zvs[b]), gss[b], avg2)
        sv = lo * (_mm(ws[2 * b], v) + bs[2 * b]) + hi * (_mm(ws[2 * b + 1], v) + bs[2 * b + 1])
        ms.append(_gelu(zus[b]) * sv)
    return qr, kr, ms


def _even_operands(p_ref, rs, gq_ref, gk_ref, gs_ref, w_ref, b_ref):
    return (p_ref[rs, 0:512].astype(F32), p_ref[rs, 512:640].astype(F32),
            [p_ref[rs, 768 + 128 * b:896 + 128 * b].astype(F32) for b in range(4)],
            [p_ref[rs, 1280 + 128 * b:1408 + 128 * b].astype(F32) for b in range(4)],
            gq_ref[...], gk_ref[...], [gs_ref[:, 128 * b:128 * b + 128] for b in range(4)],
            [w_ref[g] for g in range(8)], [b_ref[g] for g in range(8)])


def even_tok_fwd(p, cos, sin, gq, gk, gs, sgu_w, sgu_b, avg, masks):
    def body(i, p_ref, cos_ref, sin_ref, gq_ref, gk_ref, gs_ref, w_ref, b_ref, avg_ref, mk_ref, q_ref, kv_ref, m_ref):
        avgv, lo, hi = avg_ref[...], mk_ref[0, :, 0:128], mk_ref[1, :, 0:128]
        for c in range(2):
            rs = pl.ds(c * 128, 128)
            qr, kr, ms = _even_tok(*_even_operands(p_ref, rs, gq_ref, gk_ref, gs_ref, w_ref, b_ref),
                                   cos_ref[rs, :], sin_ref[rs, :], cos_ref[rs, 0:128], sin_ref[rs, 0:128],
                                   avgv, lo, hi)
            q_ref[rs, :] = qr.astype(BF)
            kv_ref[rs, 0:128] = kr.astype(BF)
            kv_ref[rs, 128:256] = p_ref[rs, 640:768]
            for b in range(4):
                m_ref[rs, 128 * b:128 * b + 128] = ms[b].astype(BF)

    return _rb_call("even_tok_fwd", body, row_in=(p,), pos_in=(cos, sin),
                    full_in=(gq, gk, gs, sgu_w, sgu_b, avg, masks), row_out=((512, BF), (256, BF), (512, BF)))


def even_tok_bwd(p, dq, dkvt, dcat, cos, sin, gq, gk, gs, sgu_w, sgu_b, avg, masks):
    def body(i, p_ref, dq_ref, dcat_ref, dkvt_ref, cos_ref, sin_ref, gq_ref, gk_ref, gs_ref, w_ref, b_ref,
             avg_ref, mk_ref, dp_ref, dgq_ref, dgk_ref, dgs_ref, dw_ref, db_ref):
        avgv, lo, hi = avg_ref[...], mk_ref[0, :, 0:128], mk_ref[1, :, 0:128]
        tot = None
        for c in range(2):
            rs = pl.ds(c * 128, 128)
            cq, sq, ck, sk = cos_ref[rs, :], sin_ref[rs, :], cos_ref[rs, 0:128], sin_ref[rs, 0:128]

            def f(q, k, zus, zvs, gq, gk, gss, ws, bs):
                return _even_tok(q, k, zus, zvs, gq, gk, gss, ws, bs, cq, sq, ck, sk, avgv, lo, hi)

            _, vjp = jax.vjp(f, *_even_operands(p_ref, rs, gq_ref, gk_ref, gs_ref, w_ref, b_ref))
            dk = dkvt_ref[0:128, c * 128:(c + 1) * 128].T
            dv = dkvt_ref[128:256, c * 128:(c + 1) * 128].T
            dms = [dcat_ref[rs, 512 + 128 * b:640 + 128 * b].astype(F32) for b in range(4)]
            d = vjp((dq_ref[rs, :].astype(F32), dk, dms))
            dp_ref[rs, 0:512] = d[0].astype(BF)
            dp_ref[rs, 512:640] = d[1].astype(BF)
            dp_ref[rs, 640:768] = dv.astype(BF)
            for b in range(4):
                dp_ref[rs, 768 + 128 * b:896 + 128 * b] = d[2][b].astype(BF)
                dp_ref[rs, 1280 + 128 * b:1408 + 128 * b] = d[3][b].astype(BF)
            part = [d[4], d[5]] + list(d[6]) + list(d[7]) + list(d[8])
            tot = part if tot is None else [x + y for x, y in zip(tot, part)]
        refs = ([dgq_ref, dgk_ref] + [dgs_ref.at[:, 128 * b:128 * b + 128] for b in range(4)]
                + [dw_ref.at[g] for g in range(8)] + [db_ref.at[g] for g in range(8)])
        for ref, val in zip(refs, tot):
            _acc(ref, val, i == 0)

    return _rb_call("even_tok_bwd", body, row_in=(p, dq, dcat), col_in=(dkvt,), pos_in=(cos, sin),
                    full_in=(gq, gk, gs, sgu_w, sgu_b, avg, masks), row_out=((EV_IN, BF),),
                    acc_out=((1, 512), (1, 128), (1, 512), (8, 128, 128), (8, 128, 1)))


MLA_SCALE = 96 ** -0.5
GQA_SCALE = 64 ** -0.5


def _odd_tok(cq, ckv, kr, za, zg, gq, gkv, wq, wkk, wkv, spread, cr, sr, ck, sk):
    cqn = _rmsn(cq, gq)
    q = _rope(_mm(cqn, wq), cr, sr) * MLA_SCALE
    ckn = _rmsn(ckv, gkv)
    k = _mm(ckn, wkk) + _mm(_rope(kr, ck, sk), spread)
    v = _mm(ckn, wkv)
    y = za * jax.nn.sigmoid(zg)
    return q, k, v, y


def odd_tok_fwd(p, cos, sin, gq, gkv, wq, wkk, wkv, spread):
    def body(i, p_ref, cos_ref, sin_ref, gq_ref, gkv_ref, wq_ref, wkk_ref, wkv_ref, sp_ref, q_ref, kv_ref, y_ref):
        q, k, v, y = _odd_tok(
            p_ref[:, 0:256].astype(F32), p_ref[:, 256:384].astype(F32), p_ref[:, 384:512].astype(F32),
            p_ref[:, 512:1024].astype(F32), p_ref[:, 1024:1536].astype(F32),
            gq_ref[...], gkv_ref[...], wq_ref[...], wkk_ref[...], wkv_ref[...], sp_ref[...],
            cos_ref[:, 0:768], sin_ref[:, 0:768], cos_ref[:, 768:896], sin_ref[:, 768:896])
        q_ref[...] = q.astype(BF)
        kv_ref[:, 0:768] = k.astype(BF)
        kv_ref[:, 768:1280] = v.astype(BF)
        y_ref[...] = y.astype(BF)

    return _rb_call("odd_tok_fwd", body, row_in=(p,), pos_in=(cos, sin), full_in=(gq, gkv, wq, wkk, wkv, spread),
                    row_out=((768, BF), (1280, BF), (512, BF)))


def odd_tok_bwd(p, dq, dkvt, dy, cos, sin, gq, gkv, wq, wkk, wkv, spread):
    def body(i, p_ref, dq_ref, dy_ref, dkvt_ref, cos_ref, sin_ref, gq_ref, gkv_ref, wq_ref, wkk_ref, wkv_ref, sp_ref,
             dp_ref, dgq_ref, dgkv_ref, dwq_ref, dwkk_ref, dwkv_ref):
        cr, sr, ck, sk = cos_ref[:, 0:768], sin_ref[:, 0:768], cos_ref[:, 768:896], sin_ref[:, 768:896]
        spread_v = sp_ref[...]

        def f(cq, ckv, kr, za, zg, gq, gkv, wq, wkk, wkv):
            return _odd_tok(cq, ckv, kr, za, zg, gq, gkv, wq, wkk, wkv, spread_v, cr, sr, ck, sk)

        _, vjp = jax.vjp(f, p_ref[:, 0:256].astype(F32), p_ref[:, 256:384].astype(F32),
                         p_ref[:, 384:512].astype(F32), p_ref[:, 512:1024].astype(F32),
                         p_ref[:, 1024:1536].astype(F32), gq_ref[...], gkv_ref[...], wq_ref[...],
                         wkk_ref[...], wkv_ref[...])
        d = vjp((dq_ref[...].astype(F32), dkvt_ref[0:768, :].T, dkvt_ref[768:1280, :].T, dy_ref[...].astype(F32)))
        dp_ref[:, 0:256] = d[0].astype(BF)
        dp_ref[:, 256:384] = d[1].astype(BF)
        dp_ref[:, 384:512] = d[2].astype(BF)
        dp_ref[:, 512:1024] = d[3].astype(BF)
        dp_ref[:, 1024:1536] = d[4].astype(BF)
        for ref, val in zip((dgq_ref, dgkv_ref, dwq_ref, dwkk_ref, dwkv_ref), d[5:]):
            _acc(ref, val, i == 0)

    return _rb_call("odd_tok_bwd", body, row_in=(p, dq, dy), col_in=(dkvt,), pos_in=(cos, sin),
                    full_in=(gq, gkv, wq, wkk, wkv, spread), row_out=((OD_PAD, BF),),
                    acc_out=((1, 256), (1, 128), (256, 768), (128, 768), (128, 512)))


GQA_HEADS = [(64 * h, 64 * (h // 4), 64, 128 + 64 * (h // 4)) for h in range(8)]
MLA_HEADS = [(96 * h, 96 * h, 96, 768 + 64 * h) for h in range(8)]


def _by_block(j, run):
    @pl.when(j == 0)
    def _():
        run(LC)

    @pl.when(j > 0)
    def _():
        run(SEQ)


def attn_fwd(name, q, kv, heads):
    qw, kvw = q.shape[1], kv.shape[1]

    def kern(q_ref, kv_ref, o_ref, lse_ref):
        def run(nk):
            for h, (qo, ko, w, vo) in enumerate(heads):
                s = lax.dot_general(q_ref[:, qo:qo + w], kv_ref[0:nk, ko:ko + w], NT, preferred_element_type=F32)
                m = jnp.max(s, axis=-1, keepdims=True)
                p = jnp.exp(s - m)
                l = jnp.sum(p, axis=-1, keepdims=True)
                o = jnp.dot(p.astype(BF), kv_ref[0:nk, vo:vo + 64], preferred_element_type=F32) / l
                o_ref[:, 64 * h:64 * h + 64] = o.astype(BF)
                lse_ref[:, h:h + 1] = m + jnp.log(l)

        _by_block(pl.program_id(1), run)

    return pl.pallas_call(
        kern, grid=(NEX, BPE),
        in_specs=[pl.BlockSpec((TB, qw), lambda e, j: (e * BPE + j, 0)),
                  pl.BlockSpec((SEQ, kvw), lambda e, j: (e, 0))],
        out_specs=[pl.BlockSpec((TB, 512), lambda e, j: (e * BPE + j, 0)),
                   pl.BlockSpec((TB, 8), lambda e, j: (e * BPE + j, 0))],
        out_shape=[jax.ShapeDtypeStruct((R, 512), BF), jax.ShapeDtypeStruct((R, 8), F32)],
        compiler_params=_cparams(("parallel", "arbitrary")), name=name)(q, kv)


def attn_bwd(name, q, kv, o, dcat, lse, heads):
    qw, kvw = q.shape[1], kv.shape[1]

    def kern(q_ref, kv_ref, o_ref, do_ref, lse_ref, dq_ref, dkvt_ref):
        j = pl.program_id(1)

        @pl.when(j == 0)
        def _():
            dkvt_ref[...] = jnp.zeros_like(dkvt_ref)

        def run(nk):
            for h, (qo, ko, w, vo) in enumerate(heads):
                qh = q_ref[:, qo:qo + w]
                kh = kv_ref[0:nk, ko:ko + w]
                s = lax.dot_general(qh, kh, NT, preferred_element_type=F32)
                p = jnp.exp(s - lse_ref[:, h:h + 1])
                do = do_ref[:, 64 * h:64 * h + 64]
                dsum = jnp.sum(do.astype(F32) * o_ref[:, 64 * h:64 * h + 64].astype(F32), axis=-1, keepdims=True)
                dp = lax.dot_general(do, kv_ref[0:nk, vo:vo + 64], NT, preferred_element_type=F32)
                ds = (p * (dp - dsum)).astype(BF)
                dkvt_ref[vo:vo + 64, 0:nk] += lax.dot_general(do, p.astype(BF), TN, preferred_element_type=F32)
                dq_ref[:, qo:qo + w] = jnp.dot(ds, kh, preferred_element_type=F32).astype(BF)
                dkvt_ref[ko:ko + w, 0:nk] += lax.dot_general(qh, ds, TN, preferred_element_type=F32)

        _by_block(j, run)

    return pl.pallas_call(
        kern, grid=(NEX, BPE),
        in_specs=[pl.BlockSpec((TB, qw), lambda e, j: (e * BPE + j, 0)),
                  pl.BlockSpec((SEQ, kvw), lambda e, j: (e, 0)),
                  pl.BlockSpec((TB, 512), lambda e, j: (e * BPE + j, 0)),
                  pl.BlockSpec((TB, 512), lambda e, j: (e * BPE + j, 0)),
                  pl.BlockSpec((TB, 8), lambda e, j: (e * BPE + j, 0))],
        out_specs=[pl.BlockSpec((TB, qw), lambda e, j: (e * BPE + j, 0)),
                   pl.BlockSpec((kvw, SEQ), lambda e, j: (0, e))],
        out_shape=[jax.ShapeDtypeStruct((R, qw), BF), jax.ShapeDtypeStruct((kvw, R), F32)],
        compiler_params=_cparams(("parallel", "arbitrary")), name=name)(q, kv, o, dcat, lse)


HALO = 16
CONV_K = 31


def _fill_ext(ext_ref, prev_ref, cur_ref, next_ref, i):
    j = i % BPE
    has_prev = (j >= 2).astype(F32)
    has_next = jnp.logical_and(j >= 1, j <= BPE - 2).astype(F32)
    ext_ref[0:HALO, :] = prev_ref[TB - HALO:TB, :].astype(F32) * has_prev
    ext_ref[HALO:HALO + TB, :] = cur_ref[...].astype(F32)
    ext_ref[HALO + TB:2 * HALO + TB, :] = next_ref[0:HALO, :].astype(F32) * has_next


PHASE_ROWS = TB + 24


def _phases(ext_ref, ph_ref):
    for r in range(8):
        ph_ref[r] = ext_ref[r:r + PHASE_ROWS, :]


def _window(ph_ref, off):
    return ph_ref[off % 8, 8 * (off // 8):8 * (off // 8) + TB, :]


def _ln_silu(z, g, b):
    mu = jnp.mean(z, axis=-1, keepdims=True)
    zc = z - mu
    var = jnp.mean(zc * zc, axis=-1, keepdims=True)
    return _silu(zc * lax.rsqrt(var + EPS) * g + b)


def conf_fwd(y, cw, cb, lg, lb):
    def body(i, cur_ref, cw_ref, cb_ref, lg_ref, lb_ref, prev_ref, next_ref, z_ref, c_ref, ext_ref, ph_ref):
        _fill_ext(ext_ref, prev_ref, cur_ref, next_ref, i)
        _phases(ext_ref, ph_ref)
        acc = _window(ph_ref, 1) * cw_ref[0:1, :]
        for k in range(1, CONV_K):
            acc = acc + _window(ph_ref, k + 1) * cw_ref[k:k + 1, :]
        z = acc + cb_ref[...]
        z_ref[...] = z.astype(BF)
        c_ref[...] = _ln_silu(z, lg_ref[...], lb_ref[...]).astype(BF)

    def idle(i, cur_ref, cw_ref, cb_ref, lg_ref, lb_ref, prev_ref, next_ref, z_ref, c_ref, ext_ref, ph_ref):
        _zero(z_ref, c_ref)

    return _rb_call("conf_fwd", body, row_in=(y,), full_in=(cw, cb, lg, lb), shift_in=((y, -1), (y, 1)),
                    row_out=((512, BF), (512, BF)), idle=idle,
                    scratch=(pltpu.VMEM((TB + 2 * HALO, 512), F32), pltpu.VMEM((8, PHASE_ROWS, 512), F32)))


def conf_bwd_ln(z, dcat, lg, lb):
    def body(i, z_ref, dcat_ref, lg_ref, lb_ref, dz_ref, dlg_ref, dlb_ref, dcb_ref):
        _, vjp = jax.vjp(_ln_silu, z_ref[...].astype(F32), lg_ref[...], lb_ref[...])
        dz, dlg, dlb = vjp(dcat_ref[:, 512:1024].astype(F32))
        dz_ref[...] = dz.astype(BF)
        _acc(dlg_ref, dlg, i == 0)
        _acc(dlb_ref, dlb, i == 0)
        _acc(dcb_ref, jnp.sum(dz, axis=0, keepdims=True), i == 0)

    def idle(i, z_ref, dcat_ref, lg_ref, lb_ref, dz_ref, dlg_ref, dlb_ref, dcb_ref):
        _zero(dz_ref)
        _zero_at_start(i, dlg_ref, dlb_ref, dcb_ref)

    return _rb_call("conf_bwd_ln", body, row_in=(z, dcat), full_in=(lg, lb), row_out=((512, BF),),
                    acc_out=((1, 512), (1, 512), (1, 512)), idle=idle)


def conf_bwd_conv(y, dz, cw):
    def body(i, y_ref, dz_ref, cw_ref, yp_ref, yn_ref, dzp_ref, dzn_ref, dy_ref, dcw_ref, ext_ref, phy_ref, phd_ref):
        _fill_ext(ext_ref, yp_ref, y_ref, yn_ref, i)
        _phases(ext_ref, phy_ref)
        _fill_ext(ext_ref, dzp_ref, dz_ref, dzn_ref, i)
        _phases(ext_ref, phd_ref)
        dzv = dz_ref[...].astype(F32)

        @pl.when(i == 0)
        def _():
            dcw_ref[...] = jnp.zeros_like(dcw_ref)

        acc = None
        for k in range(CONV_K):
            t = _window(phd_ref, CONV_K - k) * cw_ref[k:k + 1, :]
            acc = t if acc is None else acc + t
            dcw_ref[k:k + 1, :] += jnp.sum(dzv * _window(phy_ref, k + 1), axis=0, keepdims=True)
        dy_ref[...] = acc.astype(BF)

    def idle(i, y_ref, dz_ref, cw_ref, yp_ref, yn_ref, dzp_ref, dzn_ref, dy_ref, dcw_ref, ext_ref, phy_ref, phd_ref):
        _zero(dy_ref)
        _zero_at_start(i, dcw_ref)

    return _rb_call("conf_bwd_conv", body, row_in=(y, dz), full_in=(cw,), idle=idle,
                    shift_in=((y, -1), (y, 1), (dz, -1), (dz, 1)), row_out=((512, BF),), acc_out=((32, 512),),
                    scratch=(pltpu.VMEM((TB + 2 * HALO, 512), F32), pltpu.VMEM((8, PHASE_ROWS, 512), F32),
                             pltpu.VMEM((8, PHASE_ROWS, 512), F32)))


def final_loss(x, target, fg, y, mods, m_gate):
    lpb = L // TB

    def kern(x_ref, t_ref, g_ref, y_ref, gt_ref, dx_ref, dy_ref, dgt_ref, loss_ref, dg_ref):
        i = pl.program_id(0)

        @pl.when((i % BPE) == 0)
        def _():
            _zero(dx_ref, dy_ref, dgt_ref)
            _zero_at_start(i, loss_ref, dg_ref)

        @pl.when((i % BPE) >= 1)
        def _():
            tv = t_ref[...]

            def f(x, g):
                err = _rmsn(x, g) - tv
                rowsum = jnp.sum(err * err, axis=-1, keepdims=True)
                return jnp.sum(rowsum, axis=0, keepdims=True) * (0.5 / D)

            lv, vjp = jax.vjp(f, x_ref[...], g_ref[...])
            dx, dg = vjp(jnp.ones((1, 1), F32))
            dx_ref[...] = dx
            _gate_grads(dx, y_ref, gt_ref, dy_ref, dgt_ref, i)
            loss_ref[...] += jnp.zeros((8, 128), F32) + lv
            dg_ref[...] += dg

    row = pl.BlockSpec((TB, D), lambda i: (i, 0))
    return pl.pallas_call(
        kern, grid=(NBLK,),
        in_specs=[row, pl.BlockSpec((TB, D), lambda i: ((i // BPE) * lpb + jnp.maximum(i % BPE - 1, 0), 0)),
                  pl.BlockSpec((1, D), lambda i: (0, 0)), row,
                  pl.BlockSpec((1, 1, D), lambda i: (_seg(i) * N_MOD + m_gate, 0, 0))],
        out_specs=[row, row, pl.BlockSpec((1, 1, D), lambda i: (_seg(i), 0, 0)),
                   pl.BlockSpec((8, 128), lambda i: (0, 0)), pl.BlockSpec((1, D), lambda i: (0, 0))],
        out_shape=[jax.ShapeDtypeStruct((R, D), F32), jax.ShapeDtypeStruct((R, D), BF),
                   jax.ShapeDtypeStruct((4, 1, D), F32), jax.ShapeDtypeStruct((8, 128), F32),
                   jax.ShapeDtypeStruct((1, D), F32)],
        compiler_params=_cparams(("arbitrary",)), name="final_loss")(x, target, fg, y, mods)


NC = 24


def mods_fwd(call, ada_w, ada_b):
    cols = ada_w.shape[2]

    def kern(c_ref, w_ref, b_ref, o_ref):
        o_ref[...] = jnp.dot(_silu(c_ref[...]), w_ref[...], precision=HI, preferred_element_type=F32) + b_ref[...]

    return pl.pallas_call(
        kern, grid=(2,),
        in_specs=[pl.BlockSpec((NC, D), lambda l: (0, 0)), pl.BlockSpec((None, D, cols), lambda l: (l, 0, 0)),
                  pl.BlockSpec((None, 1, cols), lambda l: (l, 0, 0))],
        out_specs=pl.BlockSpec((None, NC, cols), lambda l: (l, 0, 0)),
        out_shape=jax.ShapeDtypeStruct((2, NC, cols), F32),
        compiler_params=_cparams(("parallel",)), name="mods_fwd")(call, ada_w, ada_b)


def ada_bwd(call, ada_w, dm):
    cols = ada_w.shape[2]

    def kern(c_ref, w_ref, dm_ref, gw_ref, dc_ref):
        l = pl.program_id(0)
        gw_ref[...] = lax.dot_general(_silu(c_ref[...]), dm_ref[...], TN, precision=HI, preferred_element_type=F32)
        part = lax.dot_general(dm_ref[16:24, :], w_ref[...], NT, precision=HI, preferred_element_type=F32)
        cc = c_ref[16:17, :]
        sg = jax.nn.sigmoid(cc)
        _acc(dc_ref, part * (sg * (1.0 + cc * (1.0 - sg))), l == 0)

    return pl.pallas_call(
        kern, grid=(2,),
        in_specs=[pl.BlockSpec((NC, D), lambda l: (0, 0)), pl.BlockSpec((None, D, cols), lambda l: (l, 0, 0)),
                  pl.BlockSpec((None, NC, cols), lambda l: (l, 0, 0))],
        out_specs=[pl.BlockSpec((None, D, cols), lambda l: (l, 0, 0)), pl.BlockSpec((8, D), lambda l: (0, 0))],
        out_shape=[jax.ShapeDtypeStruct((2, D, cols), F32), jax.ShapeDtypeStruct((8, D), F32)],
        compiler_params=_cparams(("arbitrary",)), name="ada_bwd")(call, ada_w, dm)


def sum_lead(name, a, after=None):
    n, r, c = a.shape
    tr = r
    for cand in (512, 256, 128, 64, 32, 16, 8):
        if r % cand == 0 and cand * c * 4 * n <= 8 * 1024 * 1024:
            tr = cand
            break
    extra = [] if after is None else [after]

    def kern(a_ref, *rest):
        acc = a_ref[0].astype(F32)
        for k in range(1, n):
            acc = acc + a_ref[k].astype(F32)
        rest[-1][...] = acc

    return pl.pallas_call(
        kern, grid=(r // tr,),
        in_specs=[pl.BlockSpec((n, tr, c), lambda i: (0, i, 0))]
        + [pl.BlockSpec(e.shape, lambda i, k=e.ndim: (0,) * k) for e in extra],
        out_specs=pl.BlockSpec((tr, c), lambda i: (i, 0)), out_shape=jax.ShapeDtypeStruct((r, c), F32),
        compiler_params=_cparams(("parallel",)), name=name)(a, *extra)


def add_pairs(name, hs, got, half):
    _, _, r, c = hs.shape

    def kern(half_ref, a_ref, b_ref, o_ref):
        o_ref[...] = (a_ref[...].astype(F32) + b_ref[...].astype(F32)).astype(BF)

    spec = pl.BlockSpec((None, r, c), lambda j, h: (j, 0, 0))
    grid_spec = pltpu.PrefetchScalarGridSpec(
        num_scalar_prefetch=1, grid=(4,),
        in_specs=[pl.BlockSpec((None, None, r, c), lambda j, h: (h[0], j, 0, 0)), spec], out_specs=spec)
    return pl.pallas_call(kern, grid_spec=grid_spec, out_shape=jax.ShapeDtypeStruct(got.shape, BF),
                          compiler_params=_cparams(("parallel",)), name=name)(half, hs, got)


def sum_slabs(name, land, own, where, full, lead):
    _, r, c = land.shape
    tr = r
    for cand in (512, 256, 128, 64, 32, 16):
        if r % cand == 0 and cand * c * 16 <= 4 * 1024 * 1024:
            tr = cand
            break

    def kern(where_ref, full_ref, land_ref, own_ref, o_ref):
        me = where_ref[0]
        acc = None
        for k in range(4):
            t = jnp.where(me == k, own_ref[k], land_ref[k]).astype(F32)
            acc = t if acc is None else acc + t
        o_ref[...] = acc

    spec = pl.BlockSpec((4, tr, c), lambda i, m: (0, i, 0))
    grid_spec = pltpu.PrefetchScalarGridSpec(
        num_scalar_prefetch=1, grid=(r // tr,), in_specs=[pl.BlockSpec(memory_space=pl.ANY), spec, spec],
        out_specs=pl.BlockSpec((None, None, tr, c), lambda i, m: (lead, m[1], i, 0)))
    return pl.pallas_call(kern, grid_spec=grid_spec, out_shape=jax.ShapeDtypeStruct(full.shape, F32),
                          input_output_aliases={1: 0}, compiler_params=_cparams(("parallel",)),
                          name=name)(where, full, land, own)


def adamw(name, w, g, m, v, again=False):
    r, c = w.shape
    tr = r
    for cand in (512, 256, 128, 64, 32, 16, 8):
        if r % cand == 0 and cand * c * 4 <= 2 * 1024 * 1024:
            tr = cand
            break
    c1 = 1.0 / (1.0 - ADAM_B1 ** ADAM_STEP)
    c2 = 1.0 / (1.0 - ADAM_B2 ** ADAM_STEP)

    def kern(w_ref, g_ref, m_ref, v_ref, d_ref, mo_ref, vo_ref, *go_ref):
        gv = g_ref[...]
        mn = ADAM_B1 * m_ref[...] + (1.0 - ADAM_B1) * gv
        vn = ADAM_B2 * v_ref[...] + (1.0 - ADAM_B2) * (gv * gv)
        d_ref[...] = -ADAM_LR * ((mn * c1) / (jnp.sqrt(vn * c2) + ADAM_EPS) + ADAM_WD * w_ref[...])
        mo_ref[...] = mn
        vo_ref[...] = vn
        if again:
            go_ref[0][...] = gv

    spec = pl.BlockSpec((tr, c), lambda i: (i, 0))
    shp = jax.ShapeDtypeStruct((r, c), F32)
    n_out = 4 if again else 3
    return pl.pallas_call(kern, grid=(r // tr,), in_specs=[spec] * 4, out_specs=[spec] * n_out,
                          out_shape=[shp] * n_out, compiler_params=_cparams(("parallel",)), name=name)(w, g, m, v)


def adamw_many(name, ws, gs, ms, vs):
    n = len(ws)
    c1 = 1.0 / (1.0 - ADAM_B1 ** ADAM_STEP)
    c2 = 1.0 / (1.0 - ADAM_B2 ** ADAM_STEP)

    def kern(*refs):
        w, g, m, v, d, mo, vo = (refs[k * n:(k + 1) * n] for k in range(7))
        for k in range(n):
            gv = g[k][...]
            mn = ADAM_B1 * m[k][...] + (1.0 - ADAM_B1) * gv
            vn = ADAM_B2 * v[k][...] + (1.0 - ADAM_B2) * (gv * gv)
            d[k][...] = -ADAM_LR * ((mn * c1) / (jnp.sqrt(vn * c2) + ADAM_EPS) + ADAM_WD * w[k][...])
            mo[k][...] = mn
            vo[k][...] = vn

    out = pl.pallas_call(kern, out_shape=[jax.ShapeDtypeStruct(a.shape, F32) for a in ws] * 3,
                         compiler_params=pltpu.CompilerParams(vmem_limit_bytes=VMEM_LIMIT),
                         name=name)(*ws, *gs, *ms, *vs)
    return out[:n], out[n:2 * n], out[2 * n:]


def all_gather8(name, xs, after=None):
    m_per, n = xs.shape
    extra = [] if after is None else [after]

    def body(x_ref, *rest):
        out_ref, send_sems, recv_sems, local_sem = rest[len(extra):]
        x, y, c = lax.axis_index("x"), lax.axis_index("y"), lax.axis_index("c")
        me, sibling = (x, y, c), (x, y, 1 - c)
        chips = [(1 - x, y), (x, 1 - y), (1 - x, 1 - y)]

        def rows(px, py, pc):
            return out_ref.at[pl.ds((4 * px + 2 * py + pc) * m_per, m_per), :]

        def copy(k, block, to, src=None):
            return pltpu.make_async_remote_copy(
                src_ref=rows(*block) if src is None else src, dst_ref=rows(*block),
                send_sem=send_sems.at[k], recv_sem=recv_sems.at[k], device_id=to, device_id_type=MESH)

        mine = pltpu.make_async_copy(x_ref, rows(*me), local_sem)
        mine.start()
        first = [copy(0, me, sibling, src=x_ref)]
        first += [copy(1 + j, me, (*chip, c), src=x_ref) for j, chip in enumerate(chips)]
        for cp in first:
            cp.start()
        passed = [copy(4 + j, (*chip, c), sibling) for j, chip in enumerate(chips)]
        for j, chip in enumerate(chips):
            copy(1 + j, (*chip, c), me).wait_recv()
            passed[j].start()
        copy(0, sibling, me).wait_recv()
        for j, chip in enumerate(chips):
            copy(4 + j, (*chip, 1 - c), me).wait_recv()
        for cp in first + passed:
            cp.wait_send()
        mine.wait()

    return pl.pallas_call(
        body, out_shape=jax.ShapeDtypeStruct((8 * m_per, n), xs.dtype),
        in_specs=[pl.BlockSpec(memory_space=pltpu.VMEM)] * (1 + len(extra)),
        out_specs=pl.BlockSpec(memory_space=pltpu.VMEM),
        scratch_shapes=[pltpu.SemaphoreType.DMA((7,)), pltpu.SemaphoreType.DMA((7,)), pltpu.SemaphoreType.DMA],
        compiler_params=pltpu.CompilerParams(vmem_limit_bytes=VMEM_LIMIT), name=name)(xs, *extra)


def sibling_merge(name, fulls):
    n = len(fulls)
    slots = [(a, l) for a in range(n) for l in range(fulls[a].shape[0])]

    def body(*refs):
        buf = refs[n:2 * n]
        send_sems, recv_sems = refs[2 * n], refs[2 * n + 1]
        c = lax.axis_index("c")
        sibling = (lax.axis_index("x"), lax.axis_index("y"), 1 - c)
        sends, recvs = [], []
        for k, (a, l) in enumerate(slots):
            kw = dict(send_sem=send_sems.at[k], recv_sem=recv_sems.at[k], device_id=sibling, device_id_type=MESH)
            sends.append(pltpu.make_async_remote_copy(src_ref=buf[a].at[l, c], dst_ref=buf[a].at[l, c], **kw))
            recvs.append(pltpu.make_async_remote_copy(src_ref=buf[a].at[l, c], dst_ref=buf[a].at[l, 1 - c], **kw))
        for cp in sends:
            cp.start()
        for cp in recvs:
            cp.wait_recv()
        for cp in sends:
            cp.wait_send()

    anyspec = pl.BlockSpec(memory_space=pl.ANY)
    return pl.pallas_call(
        body, out_shape=[jax.ShapeDtypeStruct(s.shape, s.dtype) for s in fulls],
        in_specs=[anyspec] * n, out_specs=[anyspec] * n, input_output_aliases={a: a for a in range(n)},
        scratch_shapes=[pltpu.SemaphoreType.DMA((len(slots),)), pltpu.SemaphoreType.DMA((len(slots),))],
        name=name)(*fulls)


def place_own(name, land, src, chip):
    c = src.shape[-1]
    r = src.size // c
    tr = r
    for cand in (1024, 512, 256, 128, 64, 32, 16):
        if r % cand == 0 and cand * c * 2 <= 2 * 1024 * 1024:
            tr = cand
            break

    def kern(chip_ref, land_ref, src_ref, out_ref):
        out_ref[...] = src_ref[...]

    grid_spec = pltpu.PrefetchScalarGridSpec(
        num_scalar_prefetch=1, grid=(r // tr,),
        in_specs=[pl.BlockSpec(memory_space=pl.ANY), pl.BlockSpec((tr, c), lambda i, m: (i, 0))],
        out_specs=pl.BlockSpec((None, tr, c), lambda i, m: (m[0], i, 0)))
    out = pl.pallas_call(kern, grid_spec=grid_spec, out_shape=jax.ShapeDtypeStruct((4, r, c), land.dtype),
                         input_output_aliases={1: 0}, compiler_params=_cparams(("parallel",)),
                         name=name)(chip, land.reshape(4, r, c), src.reshape(r, c))
    return out.reshape(land.shape)


def _half_copies(src, land, send_sems, recv_sems):
    c = lax.axis_index("c")
    sibling = (lax.axis_index("x"), lax.axis_index("y"), 1 - c)
    pairs = []
    for a in range(len(src)):
        cp = pltpu.make_async_remote_copy(src_ref=src[a].at[1 - c], dst_ref=land[a], send_sem=send_sems.at[a],
                                          recv_sem=recv_sems.at[a], device_id=sibling, device_id_type=MESH)
        pairs.append((cp, cp))
    return pairs


def _chip_copies(src, land, send_sems, recv_sems, scatter):
    x, y, c = lax.axis_index("x"), lax.axis_index("y"), lax.axis_index("c")
    me = 2 * x + y
    pairs = []
    for a in range(len(src)):
        for j, (px, py) in enumerate([(1 - x, y), (x, 1 - y), (1 - x, 1 - y)]):
            to = 2 * px + py
            out = src[a].at[to] if scatter else src[a]
            kw = dict(send_sem=send_sems.at[3 * a + j], recv_sem=recv_sems.at[3 * a + j], device_id=(px, py, c),
                      device_id_type=MESH)
            pairs.append((pltpu.make_async_remote_copy(src_ref=out, dst_ref=land[a].at[me], **kw),
                          pltpu.make_async_remote_copy(src_ref=out, dst_ref=land[a].at[to], **kw)))
    return pairs


_HBM = pl.BlockSpec(memory_space=pltpu.HBM)
_SEM = pl.BlockSpec(memory_space=pltpu.SEMAPHORE)


GATHER = (functools.partial(_chip_copies, scatter=False), 3)
SCATTER = (functools.partial(_chip_copies, scatter=True), 3)
TO_SIBLING = (_half_copies, 1)


def exchange_start(name, groups, plan):
    copies, per = plan
    sizes = [len(s) for s, _ in groups]
    flat = [a for s, l in groups for a in list(s) + list(l)]
    ng = len(groups)

    def body(*refs):
        ins, outs = refs[:len(flat)], refs[len(flat):]
        off = 0
        for g, n in enumerate(sizes):
            src, land = ins[off:off + n], ins[off + n:off + 2 * n]
            off += 2 * n
            for send, _ in copies(src, land, outs[2 * g], outs[2 * g + 1]):
                send.start()
        outs[-1][...] = jnp.zeros_like(outs[-1])

    out_shape = []
    for n in sizes:
        out_shape += [pltpu.SemaphoreType.DMA((per * n,)), pltpu.SemaphoreType.DMA((per * n,))]
    out_shape += [pltpu.HBM(a.shape, a.dtype) for a in flat] + [jax.ShapeDtypeStruct((8, 128), F32)]
    res = pl.pallas_call(
        body, out_shape=tuple(out_shape), in_specs=[_HBM] * len(flat),
        out_specs=tuple([_SEM] * (2 * ng) + [_HBM] * len(flat) + [pl.BlockSpec(memory_space=pltpu.VMEM)]),
        input_output_aliases={k: 2 * ng + k for k in range(len(flat))},
        compiler_params=pltpu.CompilerParams(has_side_effects=pltpu.SideEffectType.DATAFLOW_SIDE_EFFECTING),
        name=name)(*[pltpu.with_memory_space_constraint(a, pltpu.HBM) for a in flat])
    handles, off = [], 2 * ng
    for g, n in enumerate(sizes):
        handles.append((res[2 * g], res[2 * g + 1], list(res[off:off + n]), list(res[off + n:off + 2 * n])))
        off += 2 * n
    return handles, res[-1]


def exchange_wait(name, handle, after, plan):
    send_sems, recv_sems, srcs, lands = handle
    n = len(srcs)

    def body(*refs):
        src, land = refs[:n], refs[n:2 * n]
        for send, recv in plan[0](src, land, refs[2 * n], refs[2 * n + 1]):
            send.wait_send()
            recv.wait_recv()

    res = pl.pallas_call(
        body, out_shape=tuple(pltpu.HBM(a.shape, a.dtype) for a in srcs + lands),
        in_specs=[_HBM] * (2 * n) + [_SEM, _SEM, pl.BlockSpec(memory_space=pl.ANY)],
        out_specs=tuple([_HBM] * (2 * n)), input_output_aliases={k: k for k in range(2 * n)},
        compiler_params=pltpu.CompilerParams(has_side_effects=pltpu.SideEffectType.DATAFLOW_SIDE_EFFECTING),
        name=name)(*srcs, *lands, send_sems, recv_sems, after)
    return list(res[:n]), list(res[n:])


def _rope_tables(d_rot, reps):
    rows = L // GRID_W
    row = np.repeat(np.arange(rows), GRID_W).astype(np.float32)
    col = np.tile(np.arange(GRID_W), rows).astype(np.float32)
    d_axis = d_rot // 2
    inv = (ROPE_THETA ** (-np.arange(0, d_axis, 2, dtype=np.float32) / d_axis)).astype(np.float32)
    ang = np.concatenate([row[:, None] * inv, col[:, None] * inv], axis=-1).astype(np.float32)
    cos, sin = np.cos(ang).astype(np.float32), np.sin(ang).astype(np.float32)
    c = np.repeat(cos, 2, axis=-1)
    s = np.stack([-sin, sin], axis=-1).reshape(L, d_rot)
    c = np.concatenate([np.ones((LC, d_rot), np.float32), c], axis=0)
    s = np.concatenate([np.zeros((LC, d_rot), np.float32), s], axis=0)
    return np.tile(c, (1, reps)), np.tile(s, (1, reps))


def _group_consts():
    g = np.arange(512) // 64
    avg = (g[:, None] == g[None, :]).astype(np.float32) / 64.0
    masks = (np.arange(8)[:, None] == g[None, :]).astype(np.float32).reshape(8, 1, 512)
    return jnp.asarray(avg, BF), jnp.asarray(masks)


def _pack(items):
    flat = jnp.concatenate([a.reshape(-1).astype(F32) for a in items])
    n = flat.shape[0]
    rows = -(-n // D)
    rows = -(-rows // 8) * 8
    return jnp.pad(flat, (0, rows * D - n)).reshape(rows, D)


def _unpack(buf, shapes):
    lead = buf.shape[:-2]
    flat = buf.reshape(lead + (-1,))
    out, off = [], 0
    for shp in shapes:
        n = int(np.prod(shp))
        out.append(flat[..., off:off + n].reshape(lead + tuple(shp)))
        off += n
    return out


def _arrive(prm, key, after):
    if callable(prm[key]):
        prm[key](after)
    return prm[key]


def _layer_fwd(i, x, h, mods, prm, consts, nxt):
    sv = {}
    sv["x0"] = x
    sv["h"] = h
    p = proj_in(f"proj_in_{i}", h, _arrive(prm, "w_in", h))
    sv["p"] = p
    if i == 0:
        q, kv, m2 = even_tok_fwd(p, consts["cos_e"], consts["sin_e"], prm["gq"], prm["gk"], prm["gs"],
                                 prm["sgu_w"], prm["sgu_b"], consts["avg"], consts["masks"])
        o, lse = attn_fwd("attn_fwd_0", q, kv, GQA_HEADS)
        sv.update(q=q, kv=kv)
    else:
        q, kv, y = odd_tok_fwd(p, consts["cos_o"], consts["sin_o"], prm["gq"], prm["gkv"], prm["wq"], prm["wkk"],
                               prm["wkv"], consts["spread"])
        o, lse = attn_fwd("attn_fwd_1", q, kv, MLA_HEADS)
        z, m2 = conf_fwd(y, prm["conv_w"], prm["conv_b"], prm["ln_g"], prm["ln_b"])
        sv.update(q=q, kv=kv, y=y, z=z)
    sv.update(o=o, lse=lse, m2=m2)
    x1, y1, h2 = proj_out(f"proj_out_{i}", o, m2, _arrive(prm, "w_out", o), x, mods, 2, prm["norm2_g"], 3, 4,
                          last=nxt is None)
    sv.update(x1=x1, y1=y1)
    a, f = mlp_up(f"mlp_up_{i}", h2, _arrive(prm, "w1", h2))
    x2, y2, *h_next = mlp_down(f"mlp_down_{i}", f, prm["w2"], x1, mods, 5, nxt)
    sv.update(h2=h2, a=a, f=f, y2=y2)
    return x2, (h_next[0] if h_next else None), sv


def _layer_bwd(i, dx, dy2, dg2, sv, mods, prm, consts, hook, entry, below):
    gr = {}
    da = mlp_bwd_da(f"mlp_bwd_da_{i}", dy2, prm["w2"], sv["a"], after=entry)
    tiles8 = [(h, j) for h in range(2) for j in range(4)]
    gr["w1"] = mm_tn(f"grad_w1_{i}", sv["h2"], da, tiles8, 512, D, rows=R).reshape(2, 4, 512, D)
    gr["w2"] = mm_tn(f"grad_w2_{i}", sv["f"], dy2, [(2 * j + h, 0) for h in range(2) for j in range(4)],
                     512, D, rows=R).reshape(2, 4, 512, D)
    dh2 = mlp_bwd_dh(f"mlp_bwd_dh_{i}", da, prm["w1"])
    dx1, dy1, dg1, dsh2, dsc2, gr["norm2_g"] = modnorm_bwd(
        f"norm2_bwd_{i}", sv["x1"], dh2, dx, mods, prm["norm2_g"], 3, 4, gate=(sv["y1"], mods, 2),
        after=hook(f"{i}:mlp", gr, dh2), last=below is not None)
    dcat = mm_nt(f"proj_out_bwd_{i}", dy1, prm["w_out"], after=hook(f"{i}:mid", gr, dy1))
    go = mm_tn(f"grad_wout_a_{i}", sv["o"], dy1, [(0, 0)], 512, D).reshape(2, 2, 128, D)
    gm = mm_tn(f"grad_wout_b_{i}", sv["m2"], dy1, [(0, 0)], 512, D).reshape(2, 2, 128, D)
    gr["w_out"] = jnp.concatenate([go, gm], axis=0).transpose(1, 0, 2, 3)
    if i == 0:
        dq, dkv = attn_bwd("attn_bwd_0", sv["q"], sv["kv"], sv["o"], dcat, sv["lse"], GQA_HEADS)
        dp, gr["gq"], gr["gk"], gr["gs"], gr["sgu_w"], gr["sgu_b"] = even_tok_bwd(
            sv["p"], dq, dkv, dcat, consts["cos_e"], consts["sin_e"], prm["gq"], prm["gk"],
            prm["gs"], prm["sgu_w"], prm["sgu_b"], consts["avg"], consts["masks"])
    else:
        dq, dkv = attn_bwd("attn_bwd_1", sv["q"], sv["kv"], sv["o"], dcat, sv["lse"], MLA_HEADS)
        dz, gr["ln_g"], gr["ln_b"], gr["conv_b"] = conf_bwd_ln(sv["z"], dcat, prm["ln_g"], prm["ln_b"])
        dyc, gr["conv_w"] = conf_bwd_conv(sv["y"], dz, prm["conv_w"])
        dp, gr["gq"], gr["gkv"], gr["wq"], gr["wkk"], gr["wkv"] = odd_tok_bwd(
            sv["p"], dq, dkv, dyc, consts["cos_o"], consts["sin_o"], prm["gq"], prm["gkv"], prm["wq"], prm["wkk"],
            prm["wkv"], consts["spread"])
    n_in = prm["w_in"].shape[1]
    gr["w_in"] = mm_tn(f"grad_win_{i}", sv["h"], dp, [(0, 0), (1, 0)], 512, n_in, rows=R)
    dh = mm_nt(f"proj_in_bwd_{i}", dp, prm["w_in"])
    if below:
        dx0, dy2b, dg2b, dsh1, dsc1, gr["norm1_g"] = modnorm_bwd(
            f"norm1_bwd_{i}", sv["x0"], dh, dx1, mods, prm["norm1_g"], 0, 1, gate=(below[0], below[1], 5))
        down = (dy2b, dg2b)
    else:
        dx0, dsh1, dsc1, gr["norm1_g"] = modnorm_bwd(f"norm1_bwd_{i}", sv["x0"], dh, dx1, mods, prm["norm1_g"], 0, 1,
                                                     lat_only=True)
        down = None
    dmods = jnp.concatenate([dsh1, dsc1, dg1, dsh2, dsc2, dg2], axis=1)
    return dx0, down, dmods, gr, hook(f"{i}:end", gr, dx0)


def local_step(xcat, target, mods, prms, final_g, hook=lambda point, grads, fresh: None):
    avg, masks = _group_consts()
    cos_e, sin_e = _rope_tables(64, 8)
    ck, sk = _rope_tables(32, 1)
    one64, zero64 = np.ones((SEQ, 64), np.float32), np.zeros((SEQ, 64), np.float32)
    one96, zero96 = np.ones((SEQ, 96), np.float32), np.zeros((SEQ, 96), np.float32)
    cos_o = np.concatenate([np.tile(np.concatenate([one64, ck], axis=1), (1, 8)), ck, one96], axis=1)
    sin_o = np.concatenate([np.tile(np.concatenate([zero64, sk], axis=1), (1, 8)), sk, zero96], axis=1)
    lane = np.arange(768)
    spread = np.zeros((128, 768), np.float32)
    spread[lane % 96 - 64, lane] = (lane % 96 >= 64)
    consts = dict(avg=avg, masks=masks, cos_e=jnp.asarray(cos_e), sin_e=jnp.asarray(sin_e),
                  cos_o=jnp.asarray(cos_o), sin_o=jnp.asarray(sin_o), spread=jnp.asarray(spread, BF))
    x = xcat
    h = modnorm_fwd("norm1_fwd_0", x, mods[0], prms[0]["norm1_g"], 0, 1)
    saved = []
    for i in range(2):
        x, h, sv = _layer_fwd(i, x, h, mods[i], prms[i], consts, (mods[1], prms[1]["norm1_g"]) if i == 0 else None)
        saved.append(sv)
    dx, dy2, dg2, loss, dfg = final_loss(x, target, final_g, saved[1]["y2"], mods[1], 5)
    dmods, grads = [None, None], [None, None]
    entry, down = None, (dy2, dg2)
    for i in (1, 0):
        below = (saved[0]["y2"], mods[0]) if i == 1 else None
        dx, down, dmods[i], grads[i], entry = _layer_bwd(i, dx, down[0], down[1], saved[i], mods[i], prms[i], consts,
                                                         hook, entry, below)
    return loss, dx, dmods, grads, dfg, entry


def _row(v):
    return v.reshape(1, -1).astype(F32)


def odd_in_params(od_w_in, w_uq, w_ukv):
    od = jnp.concatenate([od_w_in[:, 0:416], jnp.zeros((D, 96), od_w_in.dtype), od_w_in[:, 416:OD_IN]], axis=1)
    ukv = w_ukv.reshape(128, 8, 128)
    wkk = jnp.pad(ukv[:, :, :64], ((0, 0), (0, 0), (0, 32))).reshape(128, 768)
    return dict(w_in=od, wq=w_uq, wkk=wkk, wkv=ukv[:, :, 64:].reshape(128, 512))


def small_params(small):
    p0 = dict(norm1_g=_row(small["norm1_g"][0]), norm2_g=_row(small["norm2_g"][0]),
              gq=jnp.tile(_row(small["ev_q_norm_g"]), (1, 8)), gk=jnp.tile(_row(small["ev_k_norm_g"]), (1, 2)),
              gs=_row(small["ev_sgu_norm_g"]), sgu_w=small["ev_sgu_w"].reshape(8, 128, 128).astype(F32),
              sgu_b=small["ev_sgu_b"].reshape(8, 128, 1).astype(F32))
    p1 = dict(norm1_g=_row(small["norm1_g"][1]), norm2_g=_row(small["norm2_g"][1]),
              gq=_row(small["od_q_norm_g"]), gkv=_row(small["od_kv_norm_g"]),
              conv_w=jnp.pad(small["od_conv_w"].reshape(CONV_K, 512).astype(F32), ((0, 1), (0, 0))),
              conv_b=_row(small["od_conv_b"]), ln_g=_row(small["od_ln_g"]), ln_b=_row(small["od_ln_b"]))
    return [p0, p1]


def small_grads_natural(grads, dfg):
    g0, g1 = grads
    return dict(
        norm1_g=jnp.concatenate([g0["norm1_g"], g1["norm1_g"]], axis=0),
        norm2_g=jnp.concatenate([g0["norm2_g"], g1["norm2_g"]], axis=0),
        ev_q_norm_g=g0["gq"].reshape(8, 64).sum(0).reshape(1, 64),
        ev_k_norm_g=g0["gk"].reshape(2, 64).sum(0).reshape(1, 64),
        ev_sgu_norm_g=g0["gs"].reshape(1, 8, 64),
        ev_sgu_w=g0["sgu_w"].reshape(1, 8, 128, 128),
        ev_sgu_b=g0["sgu_b"].reshape(1, 8, 128),
        od_q_norm_g=g1["gq"].reshape(1, 256),
        od_kv_norm_g=g1["gkv"].reshape(1, 128),
        od_conv_w=g1["conv_w"][0:CONV_K].reshape(1, CONV_K, 512),
        od_conv_b=g1["conv_b"].reshape(1, 512),
        od_ln_g=g1["ln_g"].reshape(1, 512),
        od_ln_b=g1["ln_b"].reshape(1, 512),
        final_g=dfg.reshape(D))


def layer_grads_hs(i, g, part="all"):
    def cols(a):
        k, n = a.shape
        return a.reshape(2, k // 2, 4, n // 4).transpose(0, 2, 1, 3).astype(BF)

    mlp = [(("mlp_w1", i), g["w1"]), (("mlp_w2", i), g["w2"])]
    if part == "mlp":
        return mlp
    rest = [(("w_out", i), g["w_out"])]
    if i == 0:
        rest.append((("ev_w_in", 0), cols(g["w_in"].reshape(D, EV_IN))))
    else:
        od = g["w_in"].reshape(D, OD_PAD)
        od = jnp.concatenate([od[:, 0:416], od[:, 512:OD_PAD]], axis=1)
        ukv = jnp.concatenate([g["wkk"].reshape(128, 8, 96)[:, :, :64], g["wkv"].reshape(128, 8, 64)], axis=2)
        rest += [(("od_w_in", 0), cols(od)), (("od_w_uq", 0), cols(g["wq"])),
                 (("od_w_ukv", 0), cols(ukv.reshape(128, 1024)))]
    return rest if part == "rest" else mlp + rest


WEIGHT_NAMES = ['c_ctx', 'ada_w', 'ada_b', 'norm1_g', 'norm2_g', 'w_out', 'mlp_w1', 'mlp_w2', 'ev_w_in',
                'ev_q_norm_g', 'ev_k_norm_g', 'ev_sgu_norm_g', 'ev_sgu_w', 'ev_sgu_b', 'od_w_in', 'od_q_norm_g',
                'od_kv_norm_g', 'od_w_uq', 'od_w_ukv', 'od_conv_w', 'od_conv_b', 'od_ln_g', 'od_ln_b', 'final_g']
REPL_SMALL = ['norm1_g', 'norm2_g', 'ev_q_norm_g', 'ev_k_norm_g', 'ev_sgu_norm_g', 'ev_sgu_w', 'ev_sgu_b',
              'od_kv_norm_g', 'final_g']
SHARD_SMALL = ['od_q_norm_g', 'od_conv_w', 'od_conv_b', 'od_ln_g', 'od_ln_b']
BIG = ['w_out', 'mlp_w1', 'mlp_w2', 'ev_w_in', 'od_w_in', 'od_w_uq', 'od_w_ukv']


def _gather_last(parts):
    return jnp.concatenate([parts[k] for k in range(4)], axis=-1)


class _Reduce:
    def __init__(self, tag, named, half, where):
        self.tag, self.half, self.where = tag, half, where
        self.names, self.hs = zip(*named)
        self.hs = list(self.hs)

    def to_sibling(self):
        lands = [lax.empty(a.shape[1:], BF) for a in self.hs]
        (self.h1,), token = exchange_start(f"rs_sibling_start_{self.tag}", [(self.hs, lands)], TO_SIBLING)
        return token

    def to_chips(self, after):
        hs, got = exchange_wait(f"rs_sibling_wait_{self.tag}", self.h1, after, TO_SIBLING)
        pair = [add_pairs(f"rs_add_{self.tag}_{k}", a, b, self.half) for k, (a, b) in enumerate(zip(hs, got))]
        lands = [lax.empty(p.shape, BF) for p in pair]
        (self.h2,), token = exchange_start(f"rs_chips_start_{self.tag}", [(pair, lands)], SCATTER)
        return token

    def finish(self, after, bufs):
        pair, land = exchange_wait(f"rs_chips_wait_{self.tag}", self.h2, after, SCATTER)
        for k, ((n, idx), l, p) in enumerate(zip(self.names, land, pair)):
            bufs[n] = sum_slabs(f"rs_sum_{self.tag}_{k}", l, p, self.where, bufs[n], idx)


def kernel(x, c, ctx, c_ctx, ada_w, ada_b, norm1_g, norm2_g, w_out, mlp_w1, mlp_w2, ev_w_in, ev_q_norm_g, ev_k_norm_g, ev_sgu_norm_g, ev_sgu_w, ev_sgu_b, od_w_in, od_q_norm_g, od_kv_norm_g, od_w_uq, od_w_ukv, od_conv_w, od_conv_b, od_ln_g, od_ln_b, final_g, loss_target, m_c_ctx, m_ada_w, m_ada_b, m_norm1_g, m_norm2_g, m_w_out, m_mlp_w1, m_mlp_w2, m_ev_w_in, m_ev_q_norm_g, m_ev_k_norm_g, m_ev_sgu_norm_g, m_ev_sgu_w, m_ev_sgu_b, m_od_w_in, m_od_q_norm_g, m_od_kv_norm_g, m_od_w_uq, m_od_w_ukv, m_od_conv_w, m_od_conv_b, m_od_ln_g, m_od_ln_b, m_final_g, v_c_ctx, v_ada_w, v_ada_b, v_norm1_g, v_norm2_g, v_w_out, v_mlp_w1, v_mlp_w2, v_ev_w_in, v_ev_q_norm_g, v_ev_k_norm_g, v_ev_sgu_norm_g, v_ev_sgu_w, v_ev_sgu_b, v_od_w_in, v_od_q_norm_g, v_od_kv_norm_g, v_od_w_uq, v_od_w_ukv, v_od_conv_w, v_od_conv_b, v_od_ln_g, v_od_ln_b, v_final_g):
    w = dict(c_ctx=c_ctx, ada_w=ada_w, ada_b=ada_b, norm1_g=norm1_g, norm2_g=norm2_g, w_out=w_out, mlp_w1=mlp_w1,
             mlp_w2=mlp_w2, ev_w_in=ev_w_in, ev_q_norm_g=ev_q_norm_g, ev_k_norm_g=ev_k_norm_g,
             ev_sgu_norm_g=ev_sgu_norm_g, ev_sgu_w=ev_sgu_w, ev_sgu_b=ev_sgu_b, od_w_in=od_w_in,
             od_q_norm_g=od_q_norm_g, od_kv_norm_g=od_kv_norm_g, od_w_uq=od_w_uq, od_w_ukv=od_w_ukv,
             od_conv_w=od_conv_w, od_conv_b=od_conv_b, od_ln_g=od_ln_g, od_ln_b=od_ln_b, final_g=final_g)
    mom = dict(c_ctx=m_c_ctx, ada_w=m_ada_w, ada_b=m_ada_b, norm1_g=m_norm1_g, norm2_g=m_norm2_g, w_out=m_w_out,
               mlp_w1=m_mlp_w1, mlp_w2=m_mlp_w2, ev_w_in=m_ev_w_in, ev_q_norm_g=m_ev_q_norm_g,
               ev_k_norm_g=m_ev_k_norm_g, ev_sgu_norm_g=m_ev_sgu_norm_g, ev_sgu_w=m_ev_sgu_w, ev_sgu_b=m_ev_sgu_b,
               od_w_in=m_od_w_in, od_q_norm_g=m_od_q_norm_g, od_kv_norm_g=m_od_kv_norm_g, od_w_uq=m_od_w_uq,
               od_w_ukv=m_od_w_ukv, od_conv_w=m_od_conv_w, od_conv_b=m_od_conv_b, od_ln_g=m_od_ln_g,
               od_ln_b=m_od_ln_b, final_g=m_final_g)
    var = dict(c_ctx=v_c_ctx, ada_w=v_ada_w, ada_b=v_ada_b, norm1_g=v_norm1_g, norm2_g=v_norm2_g, w_out=v_w_out,
               mlp_w1=v_mlp_w1, mlp_w2=v_mlp_w2, ev_w_in=v_ev_w_in, ev_q_norm_g=v_ev_q_norm_g,
               ev_k_norm_g=v_ev_k_norm_g, ev_sgu_norm_g=v_ev_sgu_norm_g, ev_sgu_w=v_ev_sgu_w, ev_sgu_b=v_ev_sgu_b,
               od_w_in=v_od_w_in, od_q_norm_g=v_od_q_norm_g, od_kv_norm_g=v_od_kv_norm_g, od_w_uq=v_od_w_uq,
               od_w_ukv=v_od_w_ukv, od_conv_w=v_od_conv_w, od_conv_b=v_od_conv_b, od_ln_g=v_od_ln_g,
               od_ln_b=v_od_ln_b, final_g=v_final_g)
    xi, yi, ci = lax.axis_index("x"), lax.axis_index("y"), lax.axis_index("c")
    chip = 2 * xi + yi
    dev = 2 * chip + ci

    shard_shapes = [w[n].shape for n in SHARD_SMALL]
    g0 = all_gather8("ag_small", _pack([c] + [w[n] for n in SHARD_SMALL]))
    g0 = g0.reshape(8, -1, D)
    parts = _unpack(g0, [c.shape] + shard_shapes)
    c_all = parts[0].reshape(16, D)
    small_full = {n: _gather_last(p[0::2]) for n, p in zip(SHARD_SMALL, parts[1:])}
    call = jnp.concatenate([c_all, c_ctx.reshape(1, D), jnp.zeros((NC - 17, D), F32)], axis=0)

    cols = ada_w.shape[2]
    ada_b_sh = lax.dynamic_slice(ada_b, (0, chip * cols), (2, cols)).reshape(2, 1, cols)
    mt = mods_fwd(call, ada_w, ada_b_sh)
    mt = all_gather8("ag_mods", mt.reshape(2 * NC, cols)).reshape(8, 2, NC, cols)
    table = mt[0::2].transpose(1, 2, 0, 3).reshape(2, NC, 4 * cols)
    mods = []
    for i in range(2):
        lat = lax.dynamic_slice(table[i], (2 * dev, 0), (2, 4 * cols))
        mc = table[i, 16]
        mods.append(jnp.stack([mc, lat[0], mc, lat[1]]).reshape(4 * N_MOD, 1, D))

    order = [[("ev_w_in", 0)], [("w_out", 0), ("mlp_w1", 0), ("mlp_w2", 0)],
             [("od_w_in", 0), ("od_w_uq", 0), ("od_w_ukv", 0), ("w_out", 1)], [("mlp_w1", 1), ("mlp_w2", 1)]]
    groups = []
    for names in order:
        srcs = [w[n][i].astype(BF) for n, i in names]
        groups.append((srcs, [lax.empty((4,) + s.shape, BF) for s in srcs]))
    groups[0][0][0], table = lax.optimization_barrier((groups[0][0][0], table))
    handles, token = exchange_start("gather_start", groups, GATHER)
    mods[0] = mods[0] + token[0, 0]
    small = {n: w[n] for n in REPL_SMALL}
    small.update(small_full)
    prms = small_params(small)

    chip1 = chip.reshape(1).astype(jnp.int32)

    def arrived(k, after):
        srcs, lands = exchange_wait(f"gather_wait_{k}", handles[k], after, GATHER)
        return [place_own(f"gather_own_{k}_{a}", l, s, chip1) for a, (l, s) in enumerate(zip(lands, srcs))]

    def arrive_ev_in(after):
        (ev,) = arrived(0, after)
        prms[0]["w_in"] = _gather_last(ev)

    def arrive_ev_rest(after):
        wo, w1, w2 = arrived(1, after)
        prms[0].update(w_out=wo.reshape(D, D), w1=w1, w2=w2)

    def arrive_od(after):
        od, uq, ukv, wo = arrived(2, after)
        prms[1].update(odd_in_params(_gather_last(od), _gather_last(uq), _gather_last(ukv)), w_out=wo.reshape(D, D))

    def arrive_od_mlp(after):
        w1, w2 = arrived(3, after)
        prms[1].update(w1=w1, w2=w2)

    prms[0]["w_in"] = arrive_ev_in
    prms[0]["w_out"] = arrive_ev_rest
    prms[1]["w_in"] = arrive_od
    prms[1]["w1"] = arrive_od_mlp

    half = ci.reshape(1).astype(jnp.int32)
    where = jnp.stack([chip, ci]).astype(jnp.int32)
    red = {}

    def hook(point, g, fresh):
        if point == "1:end":
            red["l1"] = _Reduce("l1", layer_grads_hs(1, g, "all"), half, where)
            return red["l1"].to_sibling()
        if point == "0:mlp":
            red["l0_mlp"] = _Reduce("l0_mlp", layer_grads_hs(0, g, "mlp"), half, where)
            return red["l1"].to_chips(fresh) + red["l0_mlp"].to_sibling()
        if point == "0:mid":
            return red["l0_mlp"].to_chips(fresh)
        if point == "0:end":
            red["l0_rest"] = _Reduce("l0_rest", layer_grads_hs(0, g, "rest"), half, where)
            return red["l0_rest"].to_sibling()
        return None

    xin = (ctx.reshape(NEX * LC, D), x.reshape(NEX * L, D))
    loss_p, dx, dmods, grads, dfg, last = local_step(xin, loss_target.reshape(NEX * L, D), mods, prms,
                                                     final_g.reshape(1, D), hook)
    grad_x = dx.reshape(NEX, L, D)

    sg = small_grads_natural(grads, dfg)
    dm = jnp.stack([d.reshape(4, N_MOD * D) for d in dmods])
    small_names = REPL_SMALL + SHARD_SMALL
    items = [dm[:, 1::2], dm[:, 0] + dm[:, 2]] + [sg[n] for n in small_names] + [loss_p[0:1, 0:1]]
    shapes = [a.shape for a in items]
    g1 = all_gather8("ag_grads", _pack(items), after=last)
    started = red["l0_rest"].to_chips(g1)
    rows1 = g1.shape[0] // 8
    g1 = g1.reshape(8, rows1, D)
    tot = _unpack(sum_lead("sum_small", g1, after=started), shapes)
    dm_lat = _unpack(g1, shapes[:1])[0]
    dm_lat = dm_lat.transpose(1, 0, 2, 3).reshape(2, 16, N_MOD * D)
    dm_all = jnp.concatenate([dm_lat, tot[1][:, None], jnp.zeros((2, NC - 17, N_MOD * D), F32)], axis=1)
    gsum = dict(zip(small_names, tot[2:2 + len(small_names)]))
    loss = tot[-1].reshape(())
    grad = {n: gsum[n].reshape(w[n].shape) for n in REPL_SMALL}
    for n in SHARD_SMALL:
        k = w[n].shape[-1]
        grad[n] = lax.dynamic_slice_in_dim(gsum[n], chip * k, k, axis=gsum[n].ndim - 1)
    grad["ada_b"] = sum_lead("sum_ada_b", dm_all.transpose(1, 0, 2).reshape(NC, 2 * N_MOD, D)).reshape(2, N_MOD * D)

    dm_sh = lax.dynamic_slice(dm_all, (0, 0, chip * cols), (2, NC, cols))
    grad["ada_w"], dcc = ada_bwd(call, ada_w, dm_sh)
    dcc = all_gather8("ag_cctx", dcc).reshape(8, 8, D)
    grad["c_ctx"] = sum_lead("sum_cctx", dcc[0::2])[0]

    delta, new_m, new_v = {}, {}, {}

    def adam_big(n, again):
        shp = w[n].shape
        two_d = (shp[0] * shp[1], shp[2])
        res = adamw(f"adamw_{n}", w[n].reshape(two_d), grad[n].reshape(two_d), mom[n].reshape(two_d),
                    var[n].reshape(two_d), again)
        delta[n], new_m[n], new_v[n] = [a.reshape(shp) for a in res[:3]]
        if again:
            grad[n] = res[3].reshape(shp)

    adam_big('ada_w', False)
    rest = [n for n in WEIGHT_NAMES if n not in ['ada_w'] + BIG]
    flat2 = lambda a: a.reshape(-1, a.shape[-1])
    outs = adamw_many("adamw_small", [flat2(w[n]) for n in rest], [flat2(grad[n]) for n in rest],
                      [flat2(mom[n]) for n in rest], [flat2(var[n]) for n in rest])
    for dst, arrs in zip((delta, new_m, new_v), outs):
        dst.update({n: a.reshape(w[n].shape) for n, a in zip(rest, arrs)})
    d_ = outs[0][1]

    bufs = {n: lax.empty((w[n].shape[0], 2, w[n].shape[1] // 2, w[n].shape[2]), F32) for n in BIG}
    for tag, behind in (("l1", delta["ada_w"]), ("l0_mlp", d_), ("l0_rest", d_)):
        red[tag].finish(behind, bufs)
    for n, full in zip(BIG, sibling_merge("rs_sibling_merge", [bufs[n] for n in BIG])):
        grad[n] = full.reshape(w[n].shape)
    for n in BIG:
        adam_big(n, True)

    return (loss, grad_x, *[grad[n] for n in WEIGHT_NAMES], *[delta[n] for n in WEIGHT_NAMES],
            *[new_m[n] for n in WEIGHT_NAMES], *[new_v[n] for n in WEIGHT_NAMES])
```

```python
import functools

import numpy as np
import jax
import jax.numpy as jnp
from jax import lax
from jax.experimental import pallas as pl
from jax.experimental.pallas import tpu as pltpu

F32 = jnp.float32
BF = jnp.bfloat16
MESH = pl.DeviceIdType.MESH

D = 1024
L = 2048
LC = 256
SEQ = L + LC
NEX = 2
R = NEX * SEQ
TB = 256
WIDE = 512
BPE = SEQ // TB
NBLK = R // TB
GRID_W = 64
FF = 4 * D
EPS = 1e-6
ROPE_THETA = 10000.0
N_MOD = 6
EV_IN = 1792
OD_IN = 1440
OD_PAD = 1536
VMEM_LIMIT = 60 * 1024 * 1024

ADAM_LR = 0.001
ADAM_B1 = 0.9
ADAM_B2 = 0.999
ADAM_EPS = 1e-08
ADAM_WD = 0.01
ADAM_STEP = 10

NN = (((1,), (0,)), ((), ()))
NT = (((1,), (1,)), ((), ()))
TN = (((0,), (0,)), ((), ()))


def _cparams(sem=None):
    return pltpu.CompilerParams(dimension_semantics=sem, vmem_limit_bytes=VMEM_LIMIT)


@jax.custom_vjp
def _mm(a, b):
    return jnp.dot(a.astype(BF), b.astype(BF), preferred_element_type=F32)


def _mm_fwd(a, b):
    return _mm(a, b), (a, b)


def _mm_bwd(res, g):
    a, b = res
    gb = g.astype(BF)
    da = lax.dot_general(gb, b.astype(BF), NT, preferred_element_type=F32)
    db = lax.dot_general(a.astype(BF), gb, TN, preferred_element_type=F32)
    return da, db


_mm.defvjp(_mm_fwd, _mm_bwd)


@jax.custom_vjp
def _swap(x):
    n = x.shape[-1]
    ax = x.ndim - 1
    lane = lax.broadcasted_iota(jnp.int32, x.shape, ax)
    return jnp.where(lane % 2 == 0, pltpu.roll(x, n - 1, ax), pltpu.roll(x, 1, ax))


_swap.defvjp(lambda x: (_swap(x), None), lambda _, g: (_swap(g),))


def _rope(x, cos, sin):
    return x * cos + _swap(x) * sin


def _rmsn(x, g):
    return x * lax.rsqrt(jnp.mean(x * x, axis=-1, keepdims=True) + EPS) * g


def _split_dot(a, m):
    hi = a.astype(BF)
    lo = (a - hi.astype(F32)).astype(BF)
    return jnp.dot(hi, m, preferred_element_type=F32) + jnp.dot(lo, m, preferred_element_type=F32)


@jax.custom_vjp
def _group_mean(a, avg):
    return _split_dot(a, avg)


_group_mean.defvjp(lambda a, avg: (_split_dot(a, avg), avg),
                   lambda avg, g: (_split_dot(g, avg), jnp.zeros_like(avg)))


def _grmsn(x, g, avg):
    return x * lax.rsqrt(_group_mean(x * x, avg) + EPS) * g


def _modnorm(x, g, sh, sc):
    return _rmsn(x, g) * (1.0 + sc) + sh


def _gelu(x):
    return 0.5 * x * (1.0 + jnp.tanh(0.7978845608028654 * (x + 0.044715 * (x * x * x))))


def _silu(x):
    return x * jax.nn.sigmoid(x)


def _acc(ref, val, first):
    @pl.when(first)
    def _():
        ref[...] = val

    @pl.when(jnp.logical_not(first))
    def _():
        ref[...] += val


def _seg(i):
    return 2 * (i // BPE) + jnp.minimum(i % BPE, 1)


def _seg_first(i):
    return (i % BPE) <= 1


class _Either:
    def __init__(self, pick_first, first, second):
        self.pick_first, self.first, self.second = pick_first, first, second

    def __getitem__(self, idx):
        return jnp.where(self.pick_first, self.first[idx], self.second[idx])


def _rb_call(name, body, row_in=(), mod_in=(), pos_in=(), full_in=(), shift_in=(),
             row_out=(), seg_out=(), acc_out=(), scratch=(), after=None, col_in=(), rows=TB, idle=None):
    assert rows == TB or not (mod_in or pos_in or shift_in or seg_out or col_in)
    in_specs, args, pairs = [], [], []
    for a in row_in:
        if isinstance(a, tuple):
            pairs.append(len(args))
            in_specs.append(pl.BlockSpec((TB, a[0].shape[1]), lambda i: (i // BPE, 0)))
            in_specs.append(pl.BlockSpec(
                (TB, a[1].shape[1]), lambda i: ((i // BPE) * (L // TB) + jnp.maximum(i % BPE - 1, 0), 0)))
            args += list(a)
        else:
            in_specs.append(pl.BlockSpec((rows, a.shape[1]), lambda i: (i, 0)))
            args.append(a)
    for a in col_in:
        in_specs.append(pl.BlockSpec((a.shape[0], TB), lambda i: (0, i)))
        args.append(a)
    for tab, m in mod_in:
        in_specs.append(pl.BlockSpec((1, 1, D), lambda i, m=m: (_seg(i) * N_MOD + m, 0, 0)))
        args.append(tab)
    for a in pos_in:
        in_specs.append(pl.BlockSpec((TB, a.shape[1]), lambda i: (i % BPE, 0)))
        args.append(a)
    for a in full_in:
        in_specs.append(pl.BlockSpec(a.shape, lambda i, n=a.ndim: (0,) * n))
        args.append(a)
    for a, d in shift_in:
        in_specs.append(pl.BlockSpec((TB, a.shape[1]), lambda i, d=d: (jnp.clip(i + d, 0, NBLK - 1), 0)))
        args.append(a)
    n_in = len(args)
    if after is not None:
        in_specs.append(pl.BlockSpec(after.shape, lambda i, n=after.ndim: (0,) * n))
        args.append(after)
    out_specs, out_shape = [], []
    for w, dt, *lat in row_out:
        if lat:
            out_specs.append(pl.BlockSpec(
                (TB, w), lambda i: ((i // BPE) * (L // TB) + jnp.maximum(i % BPE - 1, 0), 0)))
            out_shape.append(jax.ShapeDtypeStruct((NEX * L, w), dt))
        else:
            out_specs.append(pl.BlockSpec((rows, w), lambda i: (i, 0)))
            out_shape.append(jax.ShapeDtypeStruct((R, w), dt))
    for w in seg_out:
        out_specs.append(pl.BlockSpec((1, 1, w), lambda i: (_seg(i), 0, 0)))
        out_shape.append(jax.ShapeDtypeStruct((4, 1, w), F32))
    for shp in acc_out:
        out_specs.append(pl.BlockSpec(shp, lambda i, n=len(shp): (0,) * n))
        out_shape.append(jax.ShapeDtypeStruct(shp, F32))

    def kern(*refs):
        i = pl.program_id(0)
        ins = list(refs[:n_in])
        for k in reversed(pairs):
            ins[k:k + 2] = [_Either((i % BPE) == 0, ins[k], ins[k + 1])]
        if idle is None:
            body(i, *ins, *refs[len(args):])
        else:
            @pl.when((i % BPE) >= 1)
            def _():
                body(i, *ins, *refs[len(args):])

            @pl.when((i % BPE) == 0)
            def _():
                idle(i, *ins, *refs[len(args):])

    sem = ("arbitrary",) if (seg_out or acc_out or any(len(r) > 2 for r in row_out)) else ("parallel",)
    return pl.pallas_call(kern, grid=(R // rows,), in_specs=in_specs, out_specs=out_specs, out_shape=out_shape,
                          scratch_shapes=list(scratch), compiler_params=_cparams(sem), name=name)(*args)


def modnorm_fwd(name, x, mods, g, m_sh, m_sc):
    def body(i, x_ref, sh_ref, sc_ref, g_ref, h_ref):
        h_ref[...] = _modnorm(x_ref[...], g_ref[...], sh_ref[0], sc_ref[0]).astype(BF)

    return _rb_call(name, body, row_in=(x,), mod_in=((mods, m_sh), (mods, m_sc)), full_in=(g,),
                    row_out=((D, BF),))[0]


def _gate_grads(dx, y_ref, gt_ref, dy_ref, dgt_ref, i):
    dy_ref[...] = (dx * gt_ref[0]).astype(BF)
    _acc(dgt_ref, jnp.sum(dx * y_ref[...].astype(F32), axis=0, keepdims=True)[None], _seg_first(i))


def modnorm_bwd(name, x, dh, dx_in, mods, g, m_sh, m_sc, gate=None, after=None, lat_only=False, last=False):
    def body(i, x_ref, dh_ref, dxin_ref, *rest):
        if gate:
            y_ref, sh_ref, sc_ref, gt_ref, g_ref, dx_ref, dy_ref, dgt_ref, dsh_ref, dsc_ref, dg_ref = rest
        else:
            sh_ref, sc_ref, g_ref, dx_ref, dsh_ref, dsc_ref, dg_ref = rest
        _, vjp = jax.vjp(_modnorm, x_ref[...], g_ref[...], sh_ref[0], sc_ref[0])
        dx, dg, dsh, dsc = vjp(dh_ref[...].astype(F32))
        dx = dxin_ref[...] + dx
        dx_ref[...] = dx
        if gate:
            _gate_grads(dx, y_ref, gt_ref, dy_ref, dgt_ref, i)
        _acc(dsh_ref, dsh[None], _seg_first(i))
        _acc(dsc_ref, dsc[None], _seg_first(i))
        _acc(dg_ref, dg, i == 0)

    def idle(i, x_ref, dh_ref, dxin_ref, y_ref, sh_ref, sc_ref, gt_ref, g_ref, dx_ref, dy_ref, dgt_ref, dsh_ref,
             dsc_ref, dg_ref):
        dx_ref[...] = dxin_ref[...]
        _zero(dy_ref, dgt_ref, dsh_ref, dsc_ref)
        _zero_at_start(i, dg_ref)

    if gate:
        y, gmods, m = gate
        return _rb_call(name, body, row_in=(x, dh, dx_in, y), mod_in=((mods, m_sh), (mods, m_sc), (gmods, m)),
                        full_in=(g,), row_out=((D, F32), (D, BF)), seg_out=(D, D, D), acc_out=((1, D),), after=after,
                        idle=idle if last else None)
    return _rb_call(name, body, row_in=(x, dh, dx_in), mod_in=((mods, m_sh), (mods, m_sc)), full_in=(g,),
                    row_out=((D, F32, "lat") if lat_only else (D, F32),), seg_out=(D, D), acc_out=((1, D),),
                    after=after)


def proj_in(name, h, w):
    n = w.shape[1]

    def body(i, h_ref, w_ref, o_ref):
        o_ref[...] = jnp.dot(h_ref[...], w_ref[...], preferred_element_type=F32).astype(BF)

    return _rb_call(name, body, row_in=(h,), full_in=(w,), row_out=((n, BF),), rows=WIDE)[0]


def _zero(*refs):
    for r in refs:
        r[...] = jnp.zeros_like(r)


def _zero_at_start(i, *refs):
    @pl.when(i == 0)
    def _():
        _zero(*refs)


def proj_out(name, a1, a2, w, x, mods, m_gate, g_next, m_sh, m_sc, last=False):
    k1 = a1.shape[1]

    def idle(i, a1_ref, a2_ref, x_ref, gt_ref, sh_ref, sc_ref, w_ref, g_ref, xo_ref, y_ref, h_ref):
        xo_ref[...] = x_ref[...]
        _zero(y_ref, h_ref)

    def body(i, a1_ref, a2_ref, x_ref, gt_ref, sh_ref, sc_ref, w_ref, g_ref, xo_ref, y_ref, h_ref):
        y = jnp.dot(a1_ref[...], w_ref[:k1, :], preferred_element_type=F32)
        y = y + jnp.dot(a2_ref[...], w_ref[k1:, :], preferred_element_type=F32)
        y_ref[...] = y.astype(BF)
        xn = x_ref[...] + gt_ref[0] * y
        xo_ref[...] = xn
        h_ref[...] = _modnorm(xn, g_ref[...], sh_ref[0], sc_ref[0]).astype(BF)

    return _rb_call(name, body, row_in=(a1, a2, x), mod_in=((mods, m_gate), (mods, m_sh), (mods, m_sc)),
                    full_in=(w, g_next), row_out=((D, F32), (D, BF), (D, BF)), idle=idle if last else None)


def mlp_up(name, h, w1):
    def body(i, h_ref, w_ref, a_ref, f_ref):
        hv = h_ref[...]
        for n in range(4):
            a = jnp.dot(hv, w_ref[n], preferred_element_type=F32)
            a_ref[:, n * D:(n + 1) * D] = a.astype(BF)
            r = jnp.maximum(a, 0.0)
            f_ref[:, n * D:(n + 1) * D] = (r * r).astype(BF)

    return _rb_call(name, body, row_in=(h,), full_in=(w1,), row_out=((FF, BF), (FF, BF)), rows=WIDE)


def mlp_down(name, f, w2, x, mods, m_gate, nxt=None):
    def body(i, f_ref, x_ref, gt_ref, *rest):
        if nxt:
            sh_ref, sc_ref, w_ref, g_ref, xo_ref, y_ref, h_ref = rest
        else:
            w_ref, xo_ref, y_ref = rest
        y = jnp.dot(f_ref[:, 0:D], w_ref[0], preferred_element_type=F32)
        for n in range(1, 4):
            y = y + jnp.dot(f_ref[:, n * D:(n + 1) * D], w_ref[n], preferred_element_type=F32)
        xn = x_ref[...] + gt_ref[0] * y
        y_ref[...] = y.astype(BF)
        xo_ref[...] = xn
        if nxt:
            h_ref[...] = _modnorm(xn, g_ref[...], sh_ref[0], sc_ref[0]).astype(BF)

    if nxt:
        return _rb_call(name, body, row_in=(f, x), mod_in=((mods, m_gate), (nxt[0], 0), (nxt[0], 1)),
                        full_in=(w2, nxt[1]), row_out=((D, F32), (D, BF), (D, BF)))
    def idle(i, f_ref, x_ref, gt_ref, w_ref, xo_ref, y_ref):
        xo_ref[...] = x_ref[...]
        _zero(y_ref)

    return _rb_call(name, body, row_in=(f, x), mod_in=((mods, m_gate),), full_in=(w2,),
                    row_out=((D, F32), (D, BF)), idle=idle)


def mm_nt(name, g, w, after=None):
    k = w.shape[0]

    def body(i, g_ref, w_ref, o_ref):
        o_ref[...] = lax.dot_general(g_ref[...], w_ref[...], NT, preferred_element_type=F32).astype(BF)

    return _rb_call(name, body, row_in=(g,), full_in=(w,), row_out=((k, BF),), after=after, rows=WIDE)[0]


def mlp_bwd_da(name, dy, w2, a, after=None):
    def body(i, dy_ref, a_ref, w_ref, da_ref):
        dyv = dy_ref[...]
        for n in range(4):
            df = lax.dot_general(dyv, w_ref[n], NT, preferred_element_type=F32)
            av = a_ref[:, n * D:(n + 1) * D].astype(F32)
            da_ref[:, n * D:(n + 1) * D] = (df * (2.0 * jnp.maximum(av, 0.0))).astype(BF)

    return _rb_call(name, body, row_in=(dy, a), full_in=(w2,), row_out=((FF, BF),), after=after, rows=WIDE)[0]


def mlp_bwd_dh(name, da, w1):
    def body(i, da_ref, w_ref, dh_ref):
        acc = lax.dot_general(da_ref[:, 0:D], w_ref[0], NT, preferred_element_type=F32)
        for n in range(1, 4):
            acc = acc + lax.dot_general(da_ref[:, n * D:(n + 1) * D], w_ref[n], NT, preferred_element_type=F32)
        dh_ref[...] = acc.astype(BF)

    return _rb_call(name, body, row_in=(da,), full_in=(w1,), row_out=((D, BF),), rows=WIDE)[0]


TN_ROWS = 2304


def mm_tn(name, a, g, tiles, th, tw, rows=TN_ROWS):
    nt = len(tiles)
    acs = jnp.asarray([t[0] for t in tiles], jnp.int32)
    gcs = jnp.asarray([t[1] for t in tiles], jnp.int32)
    nr = R // rows

    def kern(ac_ref, gc_ref, a_ref, g_ref, o_ref, acc_ref):
        r = pl.program_id(1)

        @pl.when(r == 0)
        def _():
            acc_ref[...] = jnp.zeros_like(acc_ref)

        acc_ref[...] += lax.dot_general(a_ref[...], g_ref[...], TN, preferred_element_type=F32)

        @pl.when(r == nr - 1)
        def _():
            o_ref[...] = acc_ref[...].astype(BF)

    grid_spec = pltpu.PrefetchScalarGridSpec(
        num_scalar_prefetch=2, grid=(nt, nr),
        in_specs=[pl.BlockSpec((rows, th), lambda t, r, ac, gc: (r, ac[t])),
                  pl.BlockSpec((rows, tw), lambda t, r, ac, gc: (r, gc[t]))],
        out_specs=pl.BlockSpec((None, th, tw), lambda t, r, ac, gc: (t, 0, 0)),
        scratch_shapes=[pltpu.VMEM((th, tw), F32)])
    return pl.pallas_call(kern, grid_spec=grid_spec, out_shape=jax.ShapeDtypeStruct((nt, th, tw), BF),
                          compiler_params=_cparams(("parallel", "arbitrary")), name=name)(acs, gcs, a, g)


def _even_tok(q, k, zus, zvs, gq, gk, gss, ws, bs, cq, sq, ck, sk, avg, lo, hi):
    avg2 = avg[:128, :128]
    qr = _rope(_grmsn(q, gq, avg), cq, sq) * GQA_SCALE
    kr = _rope(_grmsn(k, gk, avg2), ck, sk)
    ms = []
    for b in range(4):
        v = _grmsn(_gelu(zvs[b]), gss[b], avg2)
        sv = lo * (_mm(ws[2 * b], v) + bs[2 * b]) + hi * (_mm(ws[2 * b + 1], v) + bs[2 * b + 1])
        ms.append(_gelu(zus[b]) * sv)
    return qr, kr, ms


def _even_operands(p_ref, rs, gq_ref, gk_ref, gs_ref, w_ref, b_ref):
    return (p_ref[rs, 0:512].astype(F32), p_ref[rs, 512:640].astype(F32),
            [p_ref[rs, 768 + 128 * b:896 + 128 * b].astype(F32) for b in range(4)],
            [p_ref[rs, 1280 + 128 * b:1408 + 128 * b].astype(F32) for b in range(4)],
            gq_ref[...], gk_ref[...], [gs_ref[:, 128 * b:128 * b + 128] for b in range(4)],
            [w_ref[g] for g in range(8)], [b_ref[g] for g in range(8)])


def even_tok_fwd(p, cos, sin, gq, gk, gs, sgu_w, sgu_b, avg, masks):
    def body(i, p_ref, cos_ref, sin_ref, gq_ref, gk_ref, gs_ref, w_ref, b_ref, avg_ref, mk_ref, q_ref, kv_ref, m_ref):
        avgv, lo, hi = avg_ref[...], mk_ref[0, :, 0:128], mk_ref[1, :, 0:128]
        for c in range(2):
            rs = pl.ds(c * 128, 128)
            qr, kr, ms = _even_tok(*_even_operands(p_ref, rs, gq_ref, gk_ref, gs_ref, w_ref, b_ref),
                                   cos_ref[rs, :], sin_ref[rs, :], cos_ref[rs, 0:128], sin_ref[rs, 0:128],
                                   avgv, lo, hi)
            q_ref[rs, :] = qr.astype(BF)
            kv_ref[rs, 0:128] = kr.astype(BF)
            kv_ref[rs, 128:256] = p_ref[rs, 640:768]
            for b in range(4):
                m_ref[rs, 128 * b:128 * b + 128] = ms[b].astype(BF)

    return _rb_call("even_tok_fwd", body, row_in=(p,), pos_in=(cos, sin),
                    full_in=(gq, gk, gs, sgu_w, sgu_b, avg, masks), row_out=((512, BF), (256, BF), (512, BF)))


def even_tok_bwd(p, dq, dkvt, dcat, cos, sin, gq, gk, gs, sgu_w, sgu_b, avg, masks):
    def body(i, p_ref, dq_ref, dcat_ref, dkvt_ref, cos_ref, sin_ref, gq_ref, gk_ref, gs_ref, w_ref, b_ref,
             avg_ref, mk_ref, dp_ref, dgq_ref, dgk_ref, dgs_ref, dw_ref, db_ref):
        avgv, lo, hi = avg_ref[...], mk_ref[0, :, 0:128], mk_ref[1, :, 0:128]
        tot = None
        for c in range(2):
            rs = pl.ds(c * 128, 128)
            cq, sq, ck, sk = cos_ref[rs, :], sin_ref[rs, :], cos_ref[rs, 0:128], sin_ref[rs, 0:128]

            def f(q, k, zus, zvs, gq, gk, gss, ws, bs):
                return _even_tok(q, k, zus, zvs, gq, gk, gss, ws, bs, cq, sq, ck, sk, avgv, lo, hi)

            _, vjp = jax.vjp(f, *_even_operands(p_ref, rs, gq_ref, gk_ref, gs_ref, w_ref, b_ref))
            dk = dkvt_ref[0:128, c * 128:(c + 1) * 128].T
            dv = dkvt_ref[128:256, c * 128:(c + 1) * 128].T
            dms = [dcat_ref[rs, 512 + 128 * b:640 + 128 * b].astype(F32) for b in range(4)]
            d = vjp((dq_ref[rs, :].astype(F32), dk, dms))
            dp_ref[rs, 0:512] = d[0].astype(BF)
            dp_ref[rs, 512:640] = d[1].astype(BF)
            dp_ref[rs, 640:768] = dv.astype(BF)
            for b in range(4):
                dp_ref[rs, 768 + 128 * b:896 + 128 * b] = d[2][b].astype(BF)
                dp_ref[rs, 1280 + 128 * b:1408 + 128 * b] = d[3][b].astype(BF)
            part = [d[4], d[5]] + list(d[6]) + list(d[7]) + list(d[8])
            tot = part if tot is None else [x + y for x, y in zip(tot, part)]
        refs = ([dgq_ref, dgk_ref] + [dgs_ref.at[:, 128 * b:128 * b + 128] for b in range(4)]
                + [dw_ref.at[g] for g in range(8)] + [db_ref.at[g] for g in range(8)])
        for ref, val in zip(refs, tot):
            _acc(ref, val, i == 0)

    return _rb_call("even_tok_bwd", body, row_in=(p, dq, dcat), col_in=(dkvt,), pos_in=(cos, sin),
                    full_in=(gq, gk, gs, sgu_w, sgu_b, avg, masks), row_out=((EV_IN, BF),),
                    acc_out=((1, 512), (1, 128), (1, 512), (8, 128, 128), (8, 128, 1)))


MLA_SCALE = 96 ** -0.5
GQA_SCALE = 64 ** -0.5


def _odd_tok(cq, ckv, kr, za, zg, gq, gkv, wq, wkk, wkv, spread, cr, sr, ck, sk):
    cqn = _rmsn(cq, gq)
    q = _rope(_mm(cqn, wq), cr, sr) * MLA_SCALE
    ckn = _rmsn(ckv, gkv)
    k = _mm(ckn, wkk) + _mm(_rope(kr, ck, sk), spread)
    v = _mm(ckn, wkv)
    y = za * jax.nn.sigmoid(zg)
    return q, k, v, y


def odd_tok_fwd(p, cos, sin, gq, gkv, wq, wkk, wkv, spread):
    def body(i, p_ref, cos_ref, sin_ref, gq_ref, gkv_ref, wq_ref, wkk_ref, wkv_ref, sp_ref, q_ref, kv_ref, y_ref):
        q, k, v, y = _odd_tok(
            p_ref[:, 0:256].astype(F32), p_ref[:, 256:384].astype(F32), p_ref[:, 384:512].astype(F32),
            p_ref[:, 512:1024].astype(F32), p_ref[:, 1024:1536].astype(F32),
            gq_ref[...], gkv_ref[...], wq_ref[...], wkk_ref[...], wkv_ref[...], sp_ref[...],
            cos_ref[:, 0:768], sin_ref[:, 0:768], cos_ref[:, 768:896], sin_ref[:, 768:896])
        q_ref[...] = q.astype(BF)
        kv_ref[:, 0:768] = k.astype(BF)
        kv_ref[:, 768:1280] = v.astype(BF)
        y_ref[...] = y.astype(BF)

    return _rb_call("odd_tok_fwd", body, row_in=(p,), pos_in=(cos, sin), full_in=(gq, gkv, wq, wkk, wkv, spread),
                    row_out=((768, BF), (1280, BF), (512, BF)))


def odd_tok_bwd(p, dq, dkvt, dy, cos, sin, gq, gkv, wq, wkk, wkv, spread):
    def body(i, p_ref, dq_ref, dy_ref, dkvt_ref, cos_ref, sin_ref, gq_ref, gkv_ref, wq_ref, wkk_ref, wkv_ref, sp_ref,
             dp_ref, dgq_ref, dgkv_ref, dwq_ref, dwkk_ref, dwkv_ref):
        cr, sr, ck, sk = cos_ref[:, 0:768], sin_ref[:, 0:768], cos_ref[:, 768:896], sin_ref[:, 768:896]
        spread_v = sp_ref[...]

        def f(cq, ckv, kr, za, zg, gq, gkv, wq, wkk, wkv):
            return _odd_tok(cq, ckv, kr, za, zg, gq, gkv, wq, wkk, wkv, spread_v, cr, sr, ck, sk)

        _, vjp = jax.vjp(f, p_ref[:, 0:256].astype(F32), p_ref[:, 256:384].astype(F32),
                         p_ref[:, 384:512].astype(F32), p_ref[:, 512:1024].astype(F32),
                         p_ref[:, 1024:1536].astype(F32), gq_ref[...], gkv_ref[...], wq_ref[...],
                         wkk_ref[...], wkv_ref[...])
        d = vjp((dq_ref[...].astype(F32), dkvt_ref[0:768, :].T, dkvt_ref[768:1280, :].T, dy_ref[...].astype(F32)))
        dp_ref[:, 0:256] = d[0].astype(BF)
        dp_ref[:, 256:384] = d[1].astype(BF)
        dp_ref[:, 384:512] = d[2].astype(BF)
        dp_ref[:, 512:1024] = d[3].astype(BF)
        dp_ref[:, 1024:1536] = d[4].astype(BF)
        for ref, val in zip((dgq_ref, dgkv_ref, dwq_ref, dwkk_ref, dwkv_ref), d[5:]):
            _acc(ref, val, i == 0)

    return _rb_call("odd_tok_bwd", body, row_in=(p, dq, dy), col_in=(dkvt,), pos_in=(cos, sin),
                    full_in=(gq, gkv, wq, wkk, wkv, spread), row_out=((OD_PAD, BF),),
                    acc_out=((1, 256), (1, 128), (256, 768), (128, 768), (128, 512)))


GQA_HEADS = [(64 * h, 64 * (h // 4), 64, 128 + 64 * (h // 4)) for h in range(8)]
MLA_HEADS = [(96 * h, 96 * h, 96, 768 + 64 * h) for h in range(8)]


def _by_block(j, run):
    @pl.when(j == 0)
    def _():
        run(LC)

    @pl.when(j > 0)
    def _():
        run(SEQ)


def attn_fwd(name, q, kv, heads):
    qw, kvw = q.shape[1], kv.shape[1]

    def kern(q_ref, kv_ref, o_ref, lse_ref):
        def run(nk):
            for h, (qo, ko, w, vo) in enumerate(heads):
                s = lax.dot_general(q_ref[:, qo:qo + w], kv_ref[0:nk, ko:ko + w], NT, preferred_element_type=F32)
                m = jnp.max(s, axis=-1, keepdims=True)
                p = jnp.exp(s - m)
                l = jnp.sum(p, axis=-1, keepdims=True)
                o = jnp.dot(p.astype(BF), kv_ref[0:nk, vo:vo + 64], preferred_element_type=F32) / l
                o_ref[:, 64 * h:64 * h + 64] = o.astype(BF)
                lse_ref[:, h:h + 1] = m + jnp.log(l)

        _by_block(pl.program_id(1), run)

    return pl.pallas_call(
        kern, grid=(NEX, BPE),
        in_specs=[pl.BlockSpec((TB, qw), lambda e, j: (e * BPE + j, 0)),
                  pl.BlockSpec((SEQ, kvw), lambda e, j: (e, 0))],
        out_specs=[pl.BlockSpec((TB, 512), lambda e, j: (e * BPE + j, 0)),
                   pl.BlockSpec((TB, 8), lambda e, j: (e * BPE + j, 0))],
        out_shape=[jax.ShapeDtypeStruct((R, 512), BF), jax.ShapeDtypeStruct((R, 8), F32)],
        compiler_params=_cparams(("parallel", "arbitrary")), name=name)(q, kv)


def attn_bwd(name, q, kv, o, dcat, lse, heads):
    qw, kvw = q.shape[1], kv.shape[1]

    def kern(q_ref, kv_ref, o_ref, do_ref, lse_ref, dq_ref, dkvt_ref):
        j = pl.program_id(1)

        @pl.when(j == 0)
        def _():
            dkvt_ref[...] = jnp.zeros_like(dkvt_ref)

        def run(nk):
            for h, (qo, ko, w, vo) in enumerate(heads):
                qh = q_ref[:, qo:qo + w]
                kh = kv_ref[0:nk, ko:ko + w]
                s = lax.dot_general(qh, kh, NT, preferred_element_type=F32)
                p = jnp.exp(s - lse_ref[:, h:h + 1])
                do = do_ref[:, 64 * h:64 * h + 64]
                dsum = jnp.sum(do.astype(F32) * o_ref[:, 64 * h:64 * h + 64].astype(F32), axis=-1, keepdims=True)
                dp = lax.dot_general(do, kv_ref[0:nk, vo:vo + 64], NT, preferred_element_type=F32)
                ds = (p * (dp - dsum)).astype(BF)
                dkvt_ref[vo:vo + 64, 0:nk] += lax.dot_general(do, p.astype(BF), TN, preferred_element_type=F32)
                dq_ref[:, qo:qo + w] = jnp.dot(ds, kh, preferred_element_type=F32).astype(BF)
                dkvt_ref[ko:ko + w, 0:nk] += lax.dot_general(qh, ds, TN, preferred_element_type=F32)

        _by_block(j, run)

    return pl.pallas_call(
        kern, grid=(NEX, BPE),
        in_specs=[pl.BlockSpec((TB, qw), lambda e, j: (e * BPE + j, 0)),
                  pl.BlockSpec((SEQ, kvw), lambda e, j: (e, 0)),
                  pl.BlockSpec((TB, 512), lambda e, j: (e * BPE + j, 0)),
                  pl.BlockSpec((TB, 512), lambda e, j: (e * BPE + j, 0)),
                  pl.BlockSpec((TB, 8), lambda e, j: (e * BPE + j, 0))],
        out_specs=[pl.BlockSpec((TB, qw), lambda e, j: (e * BPE + j, 0)),
                   pl.BlockSpec((kvw, SEQ), lambda e, j: (0, e))],
        out_shape=[jax.ShapeDtypeStruct((R, qw), BF), jax.ShapeDtypeStruct((kvw, R), F32)],
        compiler_params=_cparams(("parallel", "arbitrary")), name=name)(q, kv, o, dcat, lse)


HALO = 16
CONV_K = 31


def _fill_ext(ext_ref, prev_ref, cur_ref, next_ref, i):
    j = i % BPE
    has_prev = (j >= 2).astype(F32)
    has_next = jnp.logical_and(j >= 1, j <= BPE - 2).astype(F32)
    ext_ref[0:HALO, :] = prev_ref[TB - HALO:TB, :].astype(F32) * has_prev
    ext_ref[HALO:HALO + TB, :] = cur_ref[...].astype(F32)
    ext_ref[HALO + TB:2 * HALO + TB, :] = next_ref[0:HALO, :].astype(F32) * has_next


PHASE_ROWS = TB + 24


def _phases(ext_ref, ph_ref):
    for r in range(8):
        ph_ref[r] = ext_ref[r:r + PHASE_ROWS, :]


def _window(ph_ref, off):
    return ph_ref[off % 8, 8 * (off // 8):8 * (off // 8) + TB, :]


def _ln_silu(z, g, b):
    mu = jnp.mean(z, axis=-1, keepdims=True)
    zc = z - mu
    var = jnp.mean(zc * zc, axis=-1, keepdims=True)
    return _silu(zc * lax.rsqrt(var + EPS) * g + b)


def conf_fwd(y, cw, cb, lg, lb):
    def body(i, cur_ref, cw_ref, cb_ref, lg_ref, lb_ref, prev_ref, next_ref, z_ref, c_ref, ext_ref, ph_ref):
        _fill_ext(ext_ref, prev_ref, cur_ref, next_ref, i)
        _phases(ext_ref, ph_ref)
        acc = _window(ph_ref, 1) * cw_ref[0:1, :]
        for k in range(1, CONV_K):
            acc = acc + _window(ph_ref, k + 1) * cw_ref[k:k + 1, :]
        z = acc + cb_ref[...]
        z_ref[...] = z.astype(BF)
        c_ref[...] = _ln_silu(z, lg_ref[...], lb_ref[...]).astype(BF)

    def idle(i, cur_ref, cw_ref, cb_ref, lg_ref, lb_ref, prev_ref, next_ref, z_ref, c_ref, ext_ref, ph_ref):
        _zero(z_ref, c_ref)

    return _rb_call("conf_fwd", body, row_in=(y,), full_in=(cw, cb, lg, lb), shift_in=((y, -1), (y, 1)),
                    row_out=((512, BF), (512, BF)), idle=idle,
                    scratch=(pltpu.VMEM((TB + 2 * HALO, 512), F32), pltpu.VMEM((8, PHASE_ROWS, 512), F32)))


def conf_bwd_ln(z, dcat, lg, lb):
    def body(i, z_ref, dcat_ref, lg_ref, lb_ref, dz_ref, dlg_ref, dlb_ref, dcb_ref):
        _, vjp = jax.vjp(_ln_silu, z_ref[...].astype(F32), lg_ref[...], lb_ref[...])
        dz, dlg, dlb = vjp(dcat_ref[:, 512:1024].astype(F32))
        dz_ref[...] = dz.astype(BF)
        _acc(dlg_ref, dlg, i == 0)
        _acc(dlb_ref, dlb, i == 0)
        _acc(dcb_ref, jnp.sum(dz, axis=0, keepdims=True), i == 0)

    def idle(i, z_ref, dcat_ref, lg_ref, lb_ref, dz_ref, dlg_ref, dlb_ref, dcb_ref):
        _zero(dz_ref)
        _zero_at_start(i, dlg_ref, dlb_ref, dcb_ref)

    return _rb_call("conf_bwd_ln", body, row_in=(z, dcat), full_in=(lg, lb), row_out=((512, BF),),
                    acc_out=((1, 512), (1, 512), (1, 512)), idle=idle)


def conf_bwd_conv(y, dz, cw):
    def body(i, y_ref, dz_ref, cw_ref, yp_ref, yn_ref, dzp_ref, dzn_ref, dy_ref, dcw_ref, ext_ref, phy_ref, phd_ref):
        _fill_ext(ext_ref, yp_ref, y_ref, yn_ref, i)
        _phases(ext_ref, phy_ref)
        _fill_ext(ext_ref, dzp_ref, dz_ref, dzn_ref, i)
        _phases(ext_ref, phd_ref)
        dzv = dz_ref[...].astype(F32)

        @pl.when(i == 0)
        def _():
            dcw_ref[...] = jnp.zeros_like(dcw_ref)

        acc = None
        for k in range(CONV_K):
            t = _window(phd_ref, CONV_K - k) * cw_ref[k:k + 1, :]
            acc = t if acc is None else acc + t
            dcw_ref[k:k + 1, :] += jnp.sum(dzv * _window(phy_ref, k + 1), axis=0, keepdims=True)
        dy_ref[...] = acc.astype(BF)

    def idle(i, y_ref, dz_ref, cw_ref, yp_ref, yn_ref, dzp_ref, dzn_ref, dy_ref, dcw_ref, ext_ref, phy_ref, phd_ref):
        _zero(dy_ref)
        _zero_at_start(i, dcw_ref)

    return _rb_call("conf_bwd_conv", body, row_in=(y, dz), full_in=(cw,), idle=idle,
                    shift_in=((y, -1), (y, 1), (dz, -1), (dz, 1)), row_out=((512, BF),), acc_out=((32, 512),),
                    scratch=(pltpu.VMEM((TB + 2 * HALO, 512), F32), pltpu.VMEM((8, PHASE_ROWS, 512), F32),
                             pltpu.VMEM((8, PHASE_ROWS, 512), F32)))


def final_loss(x, target, fg, y, mods, m_gate):
    lpb = L // TB

    def kern(x_ref, t_ref, g_ref, y_ref, gt_ref, dx_ref, dy_ref, dgt_ref, loss_ref, dg_ref):
        i = pl.program_id(0)

        @pl.when((i % BPE) == 0)
        def _():
            _zero(dx_ref, dy_ref, dgt_ref)
            _zero_at_start(i, loss_ref, dg_ref)

        @pl.when((i % BPE) >= 1)
        def _():
            tv = t_ref[...]

            def f(x, g):
                err = _rmsn(x, g) - tv
                rowsum = jnp.sum(err * err, axis=-1, keepdims=True)
                return jnp.sum(rowsum, axis=0, keepdims=True) * (0.5 / D)

            lv, vjp = jax.vjp(f, x_ref[...], g_ref[...])
            dx, dg = vjp(jnp.ones((1, 1), F32))
            dx_ref[...] = dx
            _gate_grads(dx, y_ref, gt_ref, dy_ref, dgt_ref, i)
            loss_ref[...] += jnp.zeros((8, 128), F32) + lv
            dg_ref[...] += dg

    row = pl.BlockSpec((TB, D), lambda i: (i, 0))
    return pl.pallas_call(
        kern, grid=(NBLK,),
        in_specs=[row, pl.BlockSpec((TB, D), lambda i: ((i // BPE) * lpb + jnp.maximum(i % BPE - 1, 0), 0)),
                  pl.BlockSpec((1, D), lambda i: (0, 0)), row,
                  pl.BlockSpec((1, 1, D), lambda i: (_seg(i) * N_MOD + m_gate, 0, 0))],
        out_specs=[row, row, pl.BlockSpec((1, 1, D), lambda i: (_seg(i), 0, 0)),
                   pl.BlockSpec((8, 128), lambda i: (0, 0)), pl.BlockSpec((1, D), lambda i: (0, 0))],
        out_shape=[jax.ShapeDtypeStruct((R, D), F32), jax.ShapeDtypeStruct((R, D), BF),
                   jax.ShapeDtypeStruct((4, 1, D), F32), jax.ShapeDtypeStruct((8, 128), F32),
                   jax.ShapeDtypeStruct((1, D), F32)],
        compiler_params=_cparams(("arbitrary",)), name="final_loss")(x, target, fg, y, mods)


NC = 24


def _dot3(a, b, dims):
    ah, bh = a.astype(BF), b.astype(BF)
    al, bl = (a - ah.astype(F32)).astype(BF), (b - bh.astype(F32)).astype(BF)

    def dot(x, y):
        return lax.dot_general(x, y, dims, preferred_element_type=F32)

    return dot(ah, bh) + dot(ah, bl) + dot(al, bh)


def mods_fwd(call, ada_w, ada_b):
    cols = ada_w.shape[2]

    def kern(c_ref, w_ref, b_ref, o_ref):
        o_ref[...] = _dot3(_silu(c_ref[...]), w_ref[...], NN) + b_ref[...]

    return pl.pallas_call(
        kern, grid=(2,),
        in_specs=[pl.BlockSpec((NC, D), lambda l: (0, 0)), pl.BlockSpec((None, D, cols), lambda l: (l, 0, 0)),
                  pl.BlockSpec((None, 1, cols), lambda l: (l, 0, 0))],
        out_specs=pl.BlockSpec((None, NC, cols), lambda l: (l, 0, 0)),
        out_shape=jax.ShapeDtypeStruct((2, NC, cols), F32),
        compiler_params=_cparams(("parallel",)), name="mods_fwd")(call, ada_w, ada_b)


def ada_bwd(call, ada_w, dm):
    cols = ada_w.shape[2]

    def kern(c_ref, w_ref, dm_ref, gw_ref, dc_ref):
        l = pl.program_id(0)
        gw_ref[...] = _dot3(_silu(c_ref[...]), dm_ref[...], TN)
        part = _dot3(dm_ref[16:24, :], w_ref[...], NT)
        cc = c_ref[16:17, :]
        sg = jax.nn.sigmoid(cc)
        _acc(dc_ref, part * (sg * (1.0 + cc * (1.0 - sg))), l == 0)

    return pl.pallas_call(
        kern, grid=(2,),
        in_specs=[pl.BlockSpec((NC, D), lambda l: (0, 0)), pl.BlockSpec((None, D, cols), lambda l: (l, 0, 0)),
                  pl.BlockSpec((None, NC, cols), lambda l: (l, 0, 0))],
        out_specs=[pl.BlockSpec((None, D, cols), lambda l: (l, 0, 0)), pl.BlockSpec((8, D), lambda l: (0, 0))],
        out_shape=[jax.ShapeDtypeStruct((2, D, cols), F32), jax.ShapeDtypeStruct((8, D), F32)],
        compiler_params=_cparams(("arbitrary",)), name="ada_bwd")(call, ada_w, dm)


def sum_lead(name, a, after=None):
    n, r, c = a.shape
    tr = r
    for cand in (512, 256, 128, 64, 32, 16, 8):
        if r % cand == 0 and cand * c * 4 * n <= 8 * 1024 * 1024:
            tr = cand
            break
    extra = [] if after is None else [after]

    def kern(a_ref, *rest):
        acc = a_ref[0].astype(F32)
        for k in range(1, n):
            acc = acc + a_ref[k].astype(F32)
        rest[-1][...] = acc

    return pl.pallas_call(
        kern, grid=(r // tr,),
        in_specs=[pl.BlockSpec((n, tr, c), lambda i: (0, i, 0))]
        + [pl.BlockSpec(e.shape, lambda i, k=e.ndim: (0,) * k) for e in extra],
        out_specs=pl.BlockSpec((tr, c), lambda i: (i, 0)), out_shape=jax.ShapeDtypeStruct((r, c), F32),
        compiler_params=_cparams(("parallel",)), name=name)(a, *extra)


def add_pairs(name, hs, got, half):
    _, _, r, c = hs.shape

    def kern(half_ref, a_ref, b_ref, o_ref):
        o_ref[...] = (a_ref[...].astype(F32) + b_ref[...].astype(F32)).astype(BF)

    spec = pl.BlockSpec((None, r, c), lambda j, h: (j, 0, 0))
    grid_spec = pltpu.PrefetchScalarGridSpec(
        num_scalar_prefetch=1, grid=(4,),
        in_specs=[pl.BlockSpec((None, None, r, c), lambda j, h: (h[0], j, 0, 0)), spec], out_specs=spec)
    return pl.pallas_call(kern, grid_spec=grid_spec, out_shape=jax.ShapeDtypeStruct(got.shape, BF),
                          compiler_params=_cparams(("parallel",)), name=name)(half, hs, got)


def sum_slabs(name, land, own, where, full, lead):
    _, r, c = land.shape
    tr = r
    for cand in (512, 256, 128, 64, 32, 16):
        if r % cand == 0 and cand * c * 16 <= 4 * 1024 * 1024:
            tr = cand
            break

    def kern(where_ref, full_ref, land_ref, own_ref, o_ref):
        me = where_ref[0]
        acc = None
        for k in range(4):
            t = jnp.where(me == k, own_ref[k], land_ref[k]).astype(F32)
            acc = t if acc is None else acc + t
        o_ref[...] = acc

    spec = pl.BlockSpec((4, tr, c), lambda i, m: (0, i, 0))
    grid_spec = pltpu.PrefetchScalarGridSpec(
        num_scalar_prefetch=1, grid=(r // tr,), in_specs=[pl.BlockSpec(memory_space=pl.ANY), spec, spec],
        out_specs=pl.BlockSpec((None, None, tr, c), lambda i, m: (lead, m[1], i, 0)))
    return pl.pallas_call(kern, grid_spec=grid_spec, out_shape=jax.ShapeDtypeStruct(full.shape, F32),
                          input_output_aliases={1: 0}, compiler_params=_cparams(("parallel",)),
                          name=name)(where, full, land, own)


def adamw(name, w, g, m, v, again=False):
    r, c = w.shape
    tr = r
    for cand in (512, 256, 128, 64, 32, 16, 8):
        if r % cand == 0 and cand * c * 4 <= 2 * 1024 * 1024:
            tr = cand
            break
    c1 = 1.0 / (1.0 - ADAM_B1 ** ADAM_STEP)
    c2 = 1.0 / (1.0 - ADAM_B2 ** ADAM_STEP)

    def kern(w_ref, g_ref, m_ref, v_ref, d_ref, mo_ref, vo_ref, *go_ref):
        gv = g_ref[...]
        mn = ADAM_B1 * m_ref[...] + (1.0 - ADAM_B1) * gv
        vn = ADAM_B2 * v_ref[...] + (1.0 - ADAM_B2) * (gv * gv)
        d_ref[...] = -ADAM_LR * ((mn * c1) / (jnp.sqrt(vn * c2) + ADAM_EPS) + ADAM_WD * w_ref[...])
        mo_ref[...] = mn
        vo_ref[...] = vn
        if again:
            go_ref[0][...] = gv

    spec = pl.BlockSpec((tr, c), lambda i: (i, 0))
    shp = jax.ShapeDtypeStruct((r, c), F32)
    n_out = 4 if again else 3
    return pl.pallas_call(kern, grid=(r // tr,), in_specs=[spec] * 4, out_specs=[spec] * n_out,
                          out_shape=[shp] * n_out, compiler_params=_cparams(("parallel",)), name=name)(w, g, m, v)


def adamw_many(name, ws, gs, ms, vs):
    n = len(ws)
    c1 = 1.0 / (1.0 - ADAM_B1 ** ADAM_STEP)
    c2 = 1.0 / (1.0 - ADAM_B2 ** ADAM_STEP)

    def kern(*refs):
        w, g, m, v, d, mo, vo = (refs[k * n:(k + 1) * n] for k in range(7))
        for k in range(n):
            gv = g[k][...]
            mn = ADAM_B1 * m[k][...] + (1.0 - ADAM_B1) * gv
            vn = ADAM_B2 * v[k][...] + (1.0 - ADAM_B2) * (gv * gv)
            d[k][...] = -ADAM_LR * ((mn * c1) / (jnp.sqrt(vn * c2) + ADAM_EPS) + ADAM_WD * w[k][...])
            mo[k][...] = mn
            vo[k][...] = vn

    out = pl.pallas_call(kern, out_shape=[jax.ShapeDtypeStruct(a.shape, F32) for a in ws] * 3,
                         compiler_params=pltpu.CompilerParams(vmem_limit_bytes=VMEM_LIMIT),
                         name=name)(*ws, *gs, *ms, *vs)
    return out[:n], out[n:2 * n], out[2 * n:]


def all_gather8(name, xs, after=None):
    m_per, n = xs.shape
    extra = [] if after is None else [after]

    def body(x_ref, *rest):
        out_ref, send_sems, recv_sems, local_sem = rest[len(extra):]
        x, y, c = lax.axis_index("x"), lax.axis_index("y"), lax.axis_index("c")
        me, sibling = (x, y, c), (x, y, 1 - c)
        chips = [(1 - x, y), (x, 1 - y), (1 - x, 1 - y)]

        def rows(px, py, pc):
            return out_ref.at[pl.ds((4 * px + 2 * py + pc) * m_per, m_per), :]

        def copy(k, block, to, src=None):
            return pltpu.make_async_remote_copy(
                src_ref=rows(*block) if src is None else src, dst_ref=rows(*block),
                send_sem=send_sems.at[k], recv_sem=recv_sems.at[k], device_id=to, device_id_type=MESH)

        mine = pltpu.make_async_copy(x_ref, rows(*me), local_sem)
        mine.start()
        first = [copy(0, me, sibling, src=x_ref)]
        first += [copy(1 + j, me, (*chip, c), src=x_ref) for j, chip in enumerate(chips)]
        for cp in first:
            cp.start()
        passed = [copy(4 + j, (*chip, c), sibling) for j, chip in enumerate(chips)]
        for j, chip in enumerate(chips):
            copy(1 + j, (*chip, c), me).wait_recv()
            passed[j].start()
        copy(0, sibling, me).wait_recv()
        for j, chip in enumerate(chips):
            copy(4 + j, (*chip, 1 - c), me).wait_recv()
        for cp in first + passed:
            cp.wait_send()
        mine.wait()

    return pl.pallas_call(
        body, out_shape=jax.ShapeDtypeStruct((8 * m_per, n), xs.dtype),
        in_specs=[pl.BlockSpec(memory_space=pltpu.VMEM)] * (1 + len(extra)),
        out_specs=pl.BlockSpec(memory_space=pltpu.VMEM),
        scratch_shapes=[pltpu.SemaphoreType.DMA((7,)), pltpu.SemaphoreType.DMA((7,)), pltpu.SemaphoreType.DMA],
        compiler_params=pltpu.CompilerParams(vmem_limit_bytes=VMEM_LIMIT), name=name)(xs, *extra)


def sibling_merge(name, fulls):
    n = len(fulls)
    slots = [(a, l) for a in range(n) for l in range(fulls[a].shape[0])]

    def body(*refs):
        buf = refs[n:2 * n]
        send_sems, recv_sems = refs[2 * n], refs[2 * n + 1]
        c = lax.axis_index("c")
        sibling = (lax.axis_index("x"), lax.axis_index("y"), 1 - c)
        sends, recvs = [], []
        for k, (a, l) in enumerate(slots):
            kw = dict(send_sem=send_sems.at[k], recv_sem=recv_sems.at[k], device_id=sibling, device_id_type=MESH)
            sends.append(pltpu.make_async_remote_copy(src_ref=buf[a].at[l, c], dst_ref=buf[a].at[l, c], **kw))
            recvs.append(pltpu.make_async_remote_copy(src_ref=buf[a].at[l, c], dst_ref=buf[a].at[l, 1 - c], **kw))
        for cp in sends:
            cp.start()
        for cp in recvs:
            cp.wait_recv()
        for cp in sends:
            cp.wait_send()

    anyspec = pl.BlockSpec(memory_space=pl.ANY)
    return pl.pallas_call(
        body, out_shape=[jax.ShapeDtypeStruct(s.shape, s.dtype) for s in fulls],
        in_specs=[anyspec] * n, out_specs=[anyspec] * n, input_output_aliases={a: a for a in range(n)},
        scratch_shapes=[pltpu.SemaphoreType.DMA((len(slots),)), pltpu.SemaphoreType.DMA((len(slots),))],
        name=name)(*fulls)


def place_own(name, land, src, chip):
    c = src.shape[-1]
    r = src.size // c
    tr = r
    for cand in (1024, 512, 256, 128, 64, 32, 16):
        if r % cand == 0 and cand * c * 2 <= 2 * 1024 * 1024:
            tr = cand
            break

    def kern(chip_ref, land_ref, src_ref, out_ref):
        out_ref[...] = src_ref[...]

    grid_spec = pltpu.PrefetchScalarGridSpec(
        num_scalar_prefetch=1, grid=(r // tr,),
        in_specs=[pl.BlockSpec(memory_space=pl.ANY), pl.BlockSpec((tr, c), lambda i, m: (i, 0))],
        out_specs=pl.BlockSpec((None, tr, c), lambda i, m: (m[0], i, 0)))
    out = pl.pallas_call(kern, grid_spec=grid_spec, out_shape=jax.ShapeDtypeStruct((4, r, c), land.dtype),
                         input_output_aliases={1: 0}, compiler_params=_cparams(("parallel",)),
                         name=name)(chip, land.reshape(4, r, c), src.reshape(r, c))
    return out.reshape(land.shape)


def _half_copies(src, land, send_sems, recv_sems):
    c = lax.axis_index("c")
    sibling = (lax.axis_index("x"), lax.axis_index("y"), 1 - c)
    pairs = []
    for a in range(len(src)):
        cp = pltpu.make_async_remote_copy(src_ref=src[a].at[1 - c], dst_ref=land[a], send_sem=send_sems.at[a],
                                          recv_sem=recv_sems.at[a], device_id=sibling, device_id_type=MESH)
        pairs.append((cp, cp))
    return pairs


def _chip_copies(src, land, send_sems, recv_sems, scatter):
    x, y, c = lax.axis_index("x"), lax.axis_index("y"), lax.axis_index("c")
    me = 2 * x + y
    pairs = []
    for a in range(len(src)):
        for j, (px, py) in enumerate([(1 - x, y), (x, 1 - y), (1 - x, 1 - y)]):
            to = 2 * px + py
            out = src[a].at[to] if scatter else src[a]
            kw = dict(send_sem=send_sems.at[3 * a + j], recv_sem=recv_sems.at[3 * a + j], device_id=(px, py, c),
                      device_id_type=MESH)
            pairs.append((pltpu.make_async_remote_copy(src_ref=out, dst_ref=land[a].at[me], **kw),
                          pltpu.make_async_remote_copy(src_ref=out, dst_ref=land[a].at[to], **kw)))
    return pairs


_HBM = pl.BlockSpec(memory_space=pltpu.HBM)
_SEM = pl.BlockSpec(memory_space=pltpu.SEMAPHORE)


GATHER = (functools.partial(_chip_copies, scatter=False), 3)
SCATTER = (functools.partial(_chip_copies, scatter=True), 3)
TO_SIBLING = (_half_copies, 1)


def exchange_start(name, groups, plan):
    copies, per = plan
    sizes = [len(s) for s, _ in groups]
    flat = [a for s, l in groups for a in list(s) + list(l)]
    ng = len(groups)

    def body(*refs):
        ins, outs = refs[:len(flat)], refs[len(flat):]
        off = 0
        for g, n in enumerate(sizes):
            src, land = ins[off:off + n], ins[off + n:off + 2 * n]
            off += 2 * n
            for send, _ in copies(src, land, outs[2 * g], outs[2 * g + 1]):
                send.start()
        outs[-1][...] = jnp.zeros_like(outs[-1])

    out_shape = []
    for n in sizes:
        out_shape += [pltpu.SemaphoreType.DMA((per * n,)), pltpu.SemaphoreType.DMA((per * n,))]
    out_shape += [pltpu.HBM(a.shape, a.dtype) for a in flat] + [jax.ShapeDtypeStruct((8, 128), F32)]
    res = pl.pallas_call(
        body, out_shape=tuple(out_shape), in_specs=[_HBM] * len(flat),
        out_specs=tuple([_SEM] * (2 * ng) + [_HBM] * len(flat) + [pl.BlockSpec(memory_space=pltpu.VMEM)]),
        input_output_aliases={k: 2 * ng + k for k in range(len(flat))},
        compiler_params=pltpu.CompilerParams(has_side_effects=pltpu.SideEffectType.DATAFLOW_SIDE_EFFECTING),
        name=name)(*[pltpu.with_memory_space_constraint(a, pltpu.HBM) for a in flat])
    handles, off = [], 2 * ng
    for g, n in enumerate(sizes):
        handles.append((res[2 * g], res[2 * g + 1], list(res[off:off + n]), list(res[off + n:off + 2 * n])))
        off += 2 * n
    return handles, res[-1]


def exchange_wait(name, handle, after, plan):
    send_sems, recv_sems, srcs, lands = handle
    n = len(srcs)

    def body(*refs):
        src, land = refs[:n], refs[n:2 * n]
        for send, recv in plan[0](src, land, refs[2 * n], refs[2 * n + 1]):
            send.wait_send()
            recv.wait_recv()

    res = pl.pallas_call(
        body, out_shape=tuple(pltpu.HBM(a.shape, a.dtype) for a in srcs + lands),
        in_specs=[_HBM] * (2 * n) + [_SEM, _SEM, pl.BlockSpec(memory_space=pl.ANY)],
        out_specs=tuple([_HBM] * (2 * n)), input_output_aliases={k: k for k in range(2 * n)},
        compiler_params=pltpu.CompilerParams(has_side_effects=pltpu.SideEffectType.DATAFLOW_SIDE_EFFECTING),
        name=name)(*srcs, *lands, send_sems, recv_sems, after)
    return list(res[:n]), list(res[n:])


def _rope_tables(d_rot, reps):
    rows = L // GRID_W
    row = np.repeat(np.arange(rows), GRID_W).astype(np.float32)
    col = np.tile(np.arange(GRID_W), rows).astype(np.float32)
    d_axis = d_rot // 2
    inv = (ROPE_THETA ** (-np.arange(0, d_axis, 2, dtype=np.float32) / d_axis)).astype(np.float32)
    ang = np.concatenate([row[:, None] * inv, col[:, None] * inv], axis=-1).astype(np.float32)
    cos, sin = np.cos(ang).astype(np.float32), np.sin(ang).astype(np.float32)
    c = np.repeat(cos, 2, axis=-1)
    s = np.stack([-sin, sin], axis=-1).reshape(L, d_rot)
    c = np.concatenate([np.ones((LC, d_rot), np.float32), c], axis=0)
    s = np.concatenate([np.zeros((LC, d_rot), np.float32), s], axis=0)
    return np.tile(c, (1, reps)), np.tile(s, (1, reps))


def _group_consts():
    g = np.arange(512) // 64
    avg = (g[:, None] == g[None, :]).astype(np.float32) / 64.0
    masks = (np.arange(8)[:, None] == g[None, :]).astype(np.float32).reshape(8, 1, 512)
    return jnp.asarray(avg, BF), jnp.asarray(masks)


def _pack(items):
    flat = jnp.concatenate([a.reshape(-1).astype(F32) for a in items])
    n = flat.shape[0]
    rows = -(-n // D)
    rows = -(-rows // 8) * 8
    return jnp.pad(flat, (0, rows * D - n)).reshape(rows, D)


def _unpack(buf, shapes):
    lead = buf.shape[:-2]
    flat = buf.reshape(lead + (-1,))
    out, off = [], 0
    for shp in shapes:
        n = int(np.prod(shp))
        out.append(flat[..., off:off + n].reshape(lead + tuple(shp)))
        off += n
    return out


def _arrive(prm, key, after):
    if callable(prm[key]):
        prm[key](after)
    return prm[key]


def _layer_fwd(i, x, h, mods, prm, consts, nxt):
    sv = {}
    sv["x0"] = x
    sv["h"] = h
    p = proj_in(f"proj_in_{i}", h, _arrive(prm, "w_in", h))
    sv["p"] = p
    if i == 0:
        q, kv, m2 = even_tok_fwd(p, consts["cos_e"], consts["sin_e"], prm["gq"], prm["gk"], prm["gs"],
                                 prm["sgu_w"], prm["sgu_b"], consts["avg"], consts["masks"])
        o, lse = attn_fwd("attn_fwd_0", q, kv, GQA_HEADS)
        sv.update(q=q, kv=kv)
    else:
        q, kv, y = odd_tok_fwd(p, consts["cos_o"], consts["sin_o"], prm["gq"], prm["gkv"], prm["wq"], prm["wkk"],
                               prm["wkv"], consts["spread"])
        o, lse = attn_fwd("attn_fwd_1", q, kv, MLA_HEADS)
        z, m2 = conf_fwd(y, prm["conv_w"], prm["conv_b"], prm["ln_g"], prm["ln_b"])
        sv.update(q=q, kv=kv, y=y, z=z)
    sv.update(o=o, lse=lse, m2=m2)
    x1, y1, h2 = proj_out(f"proj_out_{i}", o, m2, _arrive(prm, "w_out", o), x, mods, 2, prm["norm2_g"], 3, 4,
                          last=nxt is None)
    sv.update(x1=x1, y1=y1)
    a, f = mlp_up(f"mlp_up_{i}", h2, _arrive(prm, "w1", h2))
    x2, y2, *h_next = mlp_down(f"mlp_down_{i}", f, prm["w2"], x1, mods, 5, nxt)
    sv.update(h2=h2, a=a, f=f, y2=y2)
    return x2, (h_next[0] if h_next else None), sv


def _layer_bwd(i, dx, dy2, dg2, sv, mods, prm, consts, hook, entry, below):
    gr = {}
    da = mlp_bwd_da(f"mlp_bwd_da_{i}", dy2, prm["w2"], sv["a"], after=entry)
    tiles8 = [(h, j) for h in range(2) for j in range(4)]
    gr["w1"] = mm_tn(f"grad_w1_{i}", sv["h2"], da, tiles8, 512, D, rows=R).reshape(2, 4, 512, D)
    gr["w2"] = mm_tn(f"grad_w2_{i}", sv["f"], dy2, [(2 * j + h, 0) for h in range(2) for j in range(4)],
                     512, D, rows=R).reshape(2, 4, 512, D)
    dh2 = mlp_bwd_dh(f"mlp_bwd_dh_{i}", da, prm["w1"])
    dx1, dy1, dg1, dsh2, dsc2, gr["norm2_g"] = modnorm_bwd(
        f"norm2_bwd_{i}", sv["x1"], dh2, dx, mods, prm["norm2_g"], 3, 4, gate=(sv["y1"], mods, 2),
        after=hook(f"{i}:mlp", gr, dh2), last=below is not None)
    dcat = mm_nt(f"proj_out_bwd_{i}", dy1, prm["w_out"], after=hook(f"{i}:mid", gr, dy1))
    go = mm_tn(f"grad_wout_a_{i}", sv["o"], dy1, [(0, 0)], 512, D, rows=R).reshape(2, 2, 128, D)
    gm = mm_tn(f"grad_wout_b_{i}", sv["m2"], dy1, [(0, 0)], 512, D, rows=R).reshape(2, 2, 128, D)
    gr["w_out"] = jnp.concatenate([go, gm], axis=0).transpose(1, 0, 2, 3)
    if i == 0:
        dq, dkv = attn_bwd("attn_bwd_0", sv["q"], sv["kv"], sv["o"], dcat, sv["lse"], GQA_HEADS)
        dp, gr["gq"], gr["gk"], gr["gs"], gr["sgu_w"], gr["sgu_b"] = even_tok_bwd(
            sv["p"], dq, dkv, dcat, consts["cos_e"], consts["sin_e"], prm["gq"], prm["gk"],
            prm["gs"], prm["sgu_w"], prm["sgu_b"], consts["avg"], consts["masks"])
    else:
        dq, dkv = attn_bwd("attn_bwd_1", sv["q"], sv["kv"], sv["o"], dcat, sv["lse"], MLA_HEADS)
        dz, gr["ln_g"], gr["ln_b"], gr["conv_b"] = conf_bwd_ln(sv["z"], dcat, prm["ln_g"], prm["ln_b"])
        dyc, gr["conv_w"] = conf_bwd_conv(sv["y"], dz, prm["conv_w"])
        dp, gr["gq"], gr["gkv"], gr["wq"], gr["wkk"], gr["wkv"] = odd_tok_bwd(
            sv["p"], dq, dkv, dyc, consts["cos_o"], consts["sin_o"], prm["gq"], prm["gkv"], prm["wq"], prm["wkk"],
            prm["wkv"], consts["spread"])
    n_in = prm["w_in"].shape[1]
    gr["w_in"] = mm_tn(f"grad_win_{i}", sv["h"], dp, [(0, 0), (1, 0)], 512, n_in)
    dh = mm_nt(f"proj_in_bwd_{i}", dp, prm["w_in"])
    if below:
        dx0, dy2b, dg2b, dsh1, dsc1, gr["norm1_g"] = modnorm_bwd(
            f"norm1_bwd_{i}", sv["x0"], dh, dx1, mods, prm["norm1_g"], 0, 1, gate=(below[0], below[1], 5))
        down = (dy2b, dg2b)
    else:
        dx0, dsh1, dsc1, gr["norm1_g"] = modnorm_bwd(f"norm1_bwd_{i}", sv["x0"], dh, dx1, mods, prm["norm1_g"], 0, 1,
                                                     lat_only=True)
        down = None
    dmods = jnp.concatenate([dsh1, dsc1, dg1, dsh2, dsc2, dg2], axis=1)
    return dx0, down, dmods, gr, hook(f"{i}:end", gr, dx0)


def local_step(xcat, target, mods, prms, final_g, hook=lambda point, grads, fresh: None):
    avg, masks = _group_consts()
    cos_e, sin_e = _rope_tables(64, 8)
    ck, sk = _rope_tables(32, 1)
    one64, zero64 = np.ones((SEQ, 64), np.float32), np.zeros((SEQ, 64), np.float32)
    one96, zero96 = np.ones((SEQ, 96), np.float32), np.zeros((SEQ, 96), np.float32)
    cos_o = np.concatenate([np.tile(np.concatenate([one64, ck], axis=1), (1, 8)), ck, one96], axis=1)
    sin_o = np.concatenate([np.tile(np.concatenate([zero64, sk], axis=1), (1, 8)), sk, zero96], axis=1)
    lane = np.arange(768)
    spread = np.zeros((128, 768), np.float32)
    spread[lane % 96 - 64, lane] = (lane % 96 >= 64)
    consts = dict(avg=avg, masks=masks, cos_e=jnp.asarray(cos_e), sin_e=jnp.asarray(sin_e),
                  cos_o=jnp.asarray(cos_o), sin_o=jnp.asarray(sin_o), spread=jnp.asarray(spread, BF))
    x = xcat
    h = modnorm_fwd("norm1_fwd_0", x, mods[0], prms[0]["norm1_g"], 0, 1)
    saved = []
    for i in range(2):
        x, h, sv = _layer_fwd(i, x, h, mods[i], prms[i], consts, (mods[1], prms[1]["norm1_g"]) if i == 0 else None)
        saved.append(sv)
    dx, dy2, dg2, loss, dfg = final_loss(x, target, final_g, saved[1]["y2"], mods[1], 5)
    dmods, grads = [None, None], [None, None]
    entry, down = None, (dy2, dg2)
    for i in (1, 0):
        below = (saved[0]["y2"], mods[0]) if i == 1 else None
        dx, down, dmods[i], grads[i], entry = _layer_bwd(i, dx, down[0], down[1], saved[i], mods[i], prms[i], consts,
                                                         hook, entry, below)
    return loss, dx, dmods, grads, dfg, entry


def _row(v):
    return v.reshape(1, -1).astype(F32)


def odd_in_params(od_w_in, w_uq, w_ukv):
    od = jnp.concatenate([od_w_in[:, 0:416], jnp.zeros((D, 96), od_w_in.dtype), od_w_in[:, 416:OD_IN]], axis=1)
    ukv = w_ukv.reshape(128, 8, 128)
    wkk = jnp.pad(ukv[:, :, :64], ((0, 0), (0, 0), (0, 32))).reshape(128, 768)
    return dict(w_in=od, wq=w_uq, wkk=wkk, wkv=ukv[:, :, 64:].reshape(128, 512))


def small_params(small):
    p0 = dict(norm1_g=_row(small["norm1_g"][0]), norm2_g=_row(small["norm2_g"][0]),
              gq=jnp.tile(_row(small["ev_q_norm_g"]), (1, 8)), gk=jnp.tile(_row(small["ev_k_norm_g"]), (1, 2)),
              gs=_row(small["ev_sgu_norm_g"]), sgu_w=small["ev_sgu_w"].reshape(8, 128, 128).astype(F32),
              sgu_b=small["ev_sgu_b"].reshape(8, 128, 1).astype(F32))
    p1 = dict(norm1_g=_row(small["norm1_g"][1]), norm2_g=_row(small["norm2_g"][1]),
              gq=_row(small["od_q_norm_g"]), gkv=_row(small["od_kv_norm_g"]),
              conv_w=jnp.pad(small["od_conv_w"].reshape(CONV_K, 512).astype(F32), ((0, 1), (0, 0))),
              conv_b=_row(small["od_conv_b"]), ln_g=_row(small["od_ln_g"]), ln_b=_row(small["od_ln_b"]))
    return [p0, p1]


def small_grads_natural(grads, dfg):
    g0, g1 = grads
    return dict(
        norm1_g=jnp.concatenate([g0["norm1_g"], g1["norm1_g"]], axis=0),
        norm2_g=jnp.concatenate([g0["norm2_g"], g1["norm2_g"]], axis=0),
        ev_q_norm_g=g0["gq"].reshape(8, 64).sum(0).reshape(1, 64),
        ev_k_norm_g=g0["gk"].reshape(2, 64).sum(0).reshape(1, 64),
        ev_sgu_norm_g=g0["gs"].reshape(1, 8, 64),
        ev_sgu_w=g0["sgu_w"].reshape(1, 8, 128, 128),
        ev_sgu_b=g0["sgu_b"].reshape(1, 8, 128),
        od_q_norm_g=g1["gq"].reshape(1, 256),
        od_kv_norm_g=g1["gkv"].reshape(1, 128),
        od_conv_w=g1["conv_w"][0:CONV_K].reshape(1, CONV_K, 512),
        od_conv_b=g1["conv_b"].reshape(1, 512),
        od_ln_g=g1["ln_g"].reshape(1, 512),
        od_ln_b=g1["ln_b"].reshape(1, 512),
        final_g=dfg.reshape(D))


def layer_grads_hs(i, g, part="all"):
    def cols(a):
        k, n = a.shape
        return a.reshape(2, k // 2, 4, n // 4).transpose(0, 2, 1, 3).astype(BF)

    mlp = [(("mlp_w1", i), g["w1"]), (("mlp_w2", i), g["w2"])]
    if part == "mlp":
        return mlp
    rest = [(("w_out", i), g["w_out"])]
    if i == 0:
        rest.append((("ev_w_in", 0), cols(g["w_in"].reshape(D, EV_IN))))
    else:
        od = g["w_in"].reshape(D, OD_PAD)
        od = jnp.concatenate([od[:, 0:416], od[:, 512:OD_PAD]], axis=1)
        ukv = jnp.concatenate([g["wkk"].reshape(128, 8, 96)[:, :, :64], g["wkv"].reshape(128, 8, 64)], axis=2)
        rest += [(("od_w_in", 0), cols(od)), (("od_w_uq", 0), cols(g["wq"])),
                 (("od_w_ukv", 0), cols(ukv.reshape(128, 1024)))]
    return rest if part == "rest" else mlp + rest


WEIGHT_NAMES = ['c_ctx', 'ada_w', 'ada_b', 'norm1_g', 'norm2_g', 'w_out', 'mlp_w1', 'mlp_w2', 'ev_w_in',
                'ev_q_norm_g', 'ev_k_norm_g', 'ev_sgu_norm_g', 'ev_sgu_w', 'ev_sgu_b', 'od_w_in', 'od_q_norm_g',
                'od_kv_norm_g', 'od_w_uq', 'od_w_ukv', 'od_conv_w', 'od_conv_b', 'od_ln_g', 'od_ln_b', 'final_g']
REPL_SMALL = ['norm1_g', 'norm2_g', 'ev_q_norm_g', 'ev_k_norm_g', 'ev_sgu_norm_g', 'ev_sgu_w', 'ev_sgu_b',
              'od_kv_norm_g', 'final_g']
SHARD_SMALL = ['od_q_norm_g', 'od_conv_w', 'od_conv_b', 'od_ln_g', 'od_ln_b']
BIG = ['w_out', 'mlp_w1', 'mlp_w2', 'ev_w_in', 'od_w_in', 'od_w_uq', 'od_w_ukv']


def _gather_last(parts):
    return jnp.concatenate([parts[k] for k in range(4)], axis=-1)


class _Reduce:
    def __init__(self, tag, named, half, where):
        self.tag, self.half, self.where = tag, half, where
        self.names, self.hs = zip(*named)
        self.hs = list(self.hs)

    def to_sibling(self):
        lands = [lax.empty(a.shape[1:], BF) for a in self.hs]
        (self.h1,), token = exchange_start(f"rs_sibling_start_{self.tag}", [(self.hs, lands)], TO_SIBLING)
        return token

    def to_chips(self, after):
        hs, got = exchange_wait(f"rs_sibling_wait_{self.tag}", self.h1, after, TO_SIBLING)
        pair = [add_pairs(f"rs_add_{self.tag}_{k}", a, b, self.half) for k, (a, b) in enumerate(zip(hs, got))]
        lands = [lax.empty(p.shape, BF) for p in pair]
        (self.h2,), token = exchange_start(f"rs_chips_start_{self.tag}", [(pair, lands)], SCATTER)
        return token

    def finish(self, after, bufs):
        pair, land = exchange_wait(f"rs_chips_wait_{self.tag}", self.h2, after, SCATTER)
        for k, ((n, idx), l, p) in enumerate(zip(self.names, land, pair)):
            bufs[n] = sum_slabs(f"rs_sum_{self.tag}_{k}", l, p, self.where, bufs[n], idx)


def kernel(x, c, ctx, c_ctx, ada_w, ada_b, norm1_g, norm2_g, w_out, mlp_w1, mlp_w2, ev_w_in, ev_q_norm_g, ev_k_norm_g, ev_sgu_norm_g, ev_sgu_w, ev_sgu_b, od_w_in, od_q_norm_g, od_kv_norm_g, od_w_uq, od_w_ukv, od_conv_w, od_conv_b, od_ln_g, od_ln_b, final_g, loss_target, m_c_ctx, m_ada_w, m_ada_b, m_norm1_g, m_norm2_g, m_w_out, m_mlp_w1, m_mlp_w2, m_ev_w_in, m_ev_q_norm_g, m_ev_k_norm_g, m_ev_sgu_norm_g, m_ev_sgu_w, m_ev_sgu_b, m_od_w_in, m_od_q_norm_g, m_od_kv_norm_g, m_od_w_uq, m_od_w_ukv, m_od_conv_w, m_od_conv_b, m_od_ln_g, m_od_ln_b, m_final_g, v_c_ctx, v_ada_w, v_ada_b, v_norm1_g, v_norm2_g, v_w_out, v_mlp_w1, v_mlp_w2, v_ev_w_in, v_ev_q_norm_g, v_ev_k_norm_g, v_ev_sgu_norm_g, v_ev_sgu_w, v_ev_sgu_b, v_od_w_in, v_od_q_norm_g, v_od_kv_norm_g, v_od_w_uq, v_od_w_ukv, v_od_conv_w, v_od_conv_b, v_od_ln_g, v_od_ln_b, v_final_g):
    w = dict(c_ctx=c_ctx, ada_w=ada_w, ada_b=ada_b, norm1_g=norm1_g, norm2_g=norm2_g, w_out=w_out, mlp_w1=mlp_w1,
             mlp_w2=mlp_w2, ev_w_in=ev_w_in, ev_q_norm_g=ev_q_norm_g, ev_k_norm_g=ev_k_norm_g,
             ev_sgu_norm_g=ev_sgu_norm_g, ev_sgu_w=ev_sgu_w, ev_sgu_b=ev_sgu_b, od_w_in=od_w_in,
             od_q_norm_g=od_q_norm_g, od_kv_norm_g=od_kv_norm_g, od_w_uq=od_w_uq, od_w_ukv=od_w_ukv,
             od_conv_w=od_conv_w, od_conv_b=od_conv_b, od_ln_g=od_ln_g, od_ln_b=od_ln_b, final_g=final_g)
    mom = dict(c_ctx=m_c_ctx, ada_w=m_ada_w, ada_b=m_ada_b, norm1_g=m_norm1_g, norm2_g=m_norm2_g, w_out=m_w_out,
               mlp_w1=m_mlp_w1, mlp_w2=m_mlp_w2, ev_w_in=m_ev_w_in, ev_q_norm_g=m_ev_q_norm_g,
               ev_k_norm_g=m_ev_k_norm_g, ev_sgu_norm_g=m_ev_sgu_norm_g, ev_sgu_w=m_ev_sgu_w, ev_sgu_b=m_ev_sgu_b,
               od_w_in=m_od_w_in, od_q_norm_g=m_od_q_norm_g, od_kv_norm_g=m_od_kv_norm_g, od_w_uq=m_od_w_uq,
               od_w_ukv=m_od_w_ukv, od_conv_w=m_od_conv_w, od_conv_b=m_od_conv_b, od_ln_g=m_od_ln_g,
               od_ln_b=m_od_ln_b, final_g=m_final_g)
    var = dict(c_ctx=v_c_ctx, ada_w=v_ada_w, ada_b=v_ada_b, norm1_g=v_norm1_g, norm2_g=v_norm2_g, w_out=v_w_out,
               mlp_w1=v_mlp_w1, mlp_w2=v_mlp_w2, ev_w_in=v_ev_w_in, ev_q_norm_g=v_ev_q_norm_g,
               ev_k_norm_g=v_ev_k_norm_g, ev_sgu_norm_g=v_ev_sgu_norm_g, ev_sgu_w=v_ev_sgu_w, ev_sgu_b=v_ev_sgu_b,
               od_w_in=v_od_w_in, od_q_norm_g=v_od_q_norm_g, od_kv_norm_g=v_od_kv_norm_g, od_w_uq=v_od_w_uq,
               od_w_ukv=v_od_w_ukv, od_conv_w=v_od_conv_w, od_conv_b=v_od_conv_b, od_ln_g=v_od_ln_g,
               od_ln_b=v_od_ln_b, final_g=v_final_g)
    xi, yi, ci = lax.axis_index("x"), lax.axis_index("y"), lax.axis_index("c")
    chip = 2 * xi + yi
    dev = 2 * chip + ci

    shard_shapes = [w[n].shape for n in SHARD_SMALL]
    g0 = all_gather8("ag_small", _pack([c] + [w[n] for n in SHARD_SMALL]))
    g0 = g0.reshape(8, -1, D)
    parts = _unpack(g0, [c.shape] + shard_shapes)
    c_all = parts[0].reshape(16, D)
    small_full = {n: _gather_last(p[0::2]) for n, p in zip(SHARD_SMALL, parts[1:])}
    call = jnp.concatenate([c_all, c_ctx.reshape(1, D), jnp.zeros((NC - 17, D), F32)], axis=0)

    cols = ada_w.shape[2]
    ada_b_sh = lax.dynamic_slice(ada_b, (0, chip * cols), (2, cols)).reshape(2, 1, cols)
    mt = mods_fwd(call, ada_w, ada_b_sh)
    mt = all_gather8("ag_mods", mt.reshape(2 * NC, cols)).reshape(8, 2, NC, cols)
    table = mt[0::2].transpose(1, 2, 0, 3).reshape(2, NC, 4 * cols)
    mods = []
    for i in range(2):
        lat = lax.dynamic_slice(table[i], (2 * dev, 0), (2, 4 * cols))
        mc = table[i, 16]
        mods.append(jnp.stack([mc, lat[0], mc, lat[1]]).reshape(4 * N_MOD, 1, D))

    order = [[("ev_w_in", 0)], [("w_out", 0), ("mlp_w1", 0), ("mlp_w2", 0)],
             [("od_w_in", 0), ("od_w_uq", 0), ("od_w_ukv", 0), ("w_out", 1)], [("mlp_w1", 1), ("mlp_w2", 1)]]
    groups = []
    for names in order:
        srcs = [w[n][i].astype(BF) for n, i in names]
        groups.append((srcs, [lax.empty((4,) + s.shape, BF) for s in srcs]))
    groups[0][0][0], table = lax.optimization_barrier((groups[0][0][0], table))
    handles, token = exchange_start("gather_start", groups, GATHER)
    mods[0] = mods[0] + token[0, 0]
    small = {n: w[n] for n in REPL_SMALL}
    small.update(small_full)
    prms = small_params(small)

    chip1 = chip.reshape(1).astype(jnp.int32)

    def arrived(k, after):
        srcs, lands = exchange_wait(f"gather_wait_{k}", handles[k], after, GATHER)
        return [place_own(f"gather_own_{k}_{a}", l, s, chip1) for a, (l, s) in enumerate(zip(lands, srcs))]

    def arrive_ev_in(after):
        (ev,) = arrived(0, after)
        prms[0]["w_in"] = _gather_last(ev)

    def arrive_ev_rest(after):
        wo, w1, w2 = arrived(1, after)
        prms[0].update(w_out=wo.reshape(D, D), w1=w1, w2=w2)

    def arrive_od(after):
        od, uq, ukv, wo = arrived(2, after)
        prms[1].update(odd_in_params(_gather_last(od), _gather_last(uq), _gather_last(ukv)), w_out=wo.reshape(D, D))

    def arrive_od_mlp(after):
        w1, w2 = arrived(3, after)
        prms[1].update(w1=w1, w2=w2)

    prms[0]["w_in"] = arrive_ev_in
    prms[0]["w_out"] = arrive_ev_rest
    prms[1]["w_in"] = arrive_od
    prms[1]["w1"] = arrive_od_mlp

    half = ci.reshape(1).astype(jnp.int32)
    where = jnp.stack([chip, ci]).astype(jnp.int32)
    red = {}

    def hook(point, g, fresh):
        if point == "1:end":
            red["l1"] = _Reduce("l1", layer_grads_hs(1, g, "all"), half, where)
            return red["l1"].to_sibling()
        if point == "0:mlp":
            red["l0_mlp"] = _Reduce("l0_mlp", layer_grads_hs(0, g, "mlp"), half, where)
            return red["l1"].to_chips(fresh) + red["l0_mlp"].to_sibling()
        if point == "0:mid":
            return red["l0_mlp"].to_chips(fresh)
        if point == "0:end":
            red["l0_rest"] = _Reduce("l0_rest", layer_grads_hs(0, g, "rest"), half, where)
            return red["l0_rest"].to_sibling()
        return None

    xin = (ctx.reshape(NEX * LC, D), x.reshape(NEX * L, D))
    loss_p, dx, dmods, grads, dfg, last = local_step(xin, loss_target.reshape(NEX * L, D), mods, prms,
                                                     final_g.reshape(1, D), hook)
    grad_x = dx.reshape(NEX, L, D)

    sg = small_grads_natural(grads, dfg)
    dm = jnp.stack([d.reshape(4, N_MOD * D) for d in dmods])
    small_names = REPL_SMALL + SHARD_SMALL
    items = [dm[:, 1::2], dm[:, 0] + dm[:, 2]] + [sg[n] for n in small_names] + [loss_p[0:1, 0:1]]
    shapes = [a.shape for a in items]
    g1 = all_gather8("ag_grads", _pack(items), after=last)
    started = red["l0_rest"].to_chips(g1)
    rows1 = g1.shape[0] // 8
    g1 = g1.reshape(8, rows1, D)
    tot = _unpack(sum_lead("sum_small", g1, after=started), shapes)
    dm_lat = _unpack(g1, shapes[:1])[0]
    dm_lat = dm_lat.transpose(1, 0, 2, 3).reshape(2, 16, N_MOD * D)
    dm_all = jnp.concatenate([dm_lat, tot[1][:, None], jnp.zeros((2, NC - 17, N_MOD * D), F32)], axis=1)
    gsum = dict(zip(small_names, tot[2:2 + len(small_names)]))
    loss = tot[-1].reshape(())
    grad = {n: gsum[n].reshape(w[n].shape) for n in REPL_SMALL}
    for n in SHARD_SMALL:
        k = w[n].shape[-1]
        grad[n] = lax.dynamic_slice_in_dim(gsum[n], chip * k, k, axis=gsum[n].ndim - 1)
    grad["ada_b"] = sum_lead("sum_ada_b", dm_all.transpose(1, 0, 2).reshape(NC, 2 * N_MOD, D)).reshape(2, N_MOD * D)

    dm_sh = lax.dynamic_slice(dm_all, (0, 0, chip * cols), (2, NC, cols))
    grad["ada_w"], dcc = ada_bwd(call, ada_w, dm_sh)
    dcc = all_gather8("ag_cctx", dcc).reshape(8, 8, D)
    grad["c_ctx"] = sum_lead("sum_cctx", dcc[0::2])[0]

    delta, new_m, new_v = {}, {}, {}

    def adam_big(n, again):
        shp = w[n].shape
        two_d = (shp[0] * shp[1], shp[2])
        res = adamw(f"adamw_{n}", w[n].reshape(two_d), grad[n].reshape(two_d), mom[n].reshape(two_d),
                    var[n].reshape(two_d), again)
        delta[n], new_m[n], new_v[n] = [a.reshape(shp) for a in res[:3]]
        if again:
            grad[n] = res[3].reshape(shp)

    adam_big('ada_w', False)
    rest = [n for n in WEIGHT_NAMES if n not in ['ada_w'] + BIG]
    flat2 = lambda a: a.reshape(-1, a.shape[-1])
    outs = adamw_many("adamw_small", [flat2(w[n]) for n in rest], [flat2(grad[n]) for n in rest],
                      [flat2(mom[n]) for n in rest], [flat2(var[n]) for n in rest])
    for dst, arrs in zip((delta, new_m, new_v), outs):
        dst.update({n: a.reshape(w[n].shape) for n, a in zip(rest, arrs)})
    d_ = outs[0][1]

    bufs = {n: lax.empty((w[n].shape[0], 2, w[n].shape[1] // 2, w[n].shape[2]), F32) for n in BIG}
    for tag, behind in (("l1", delta["ada_w"]), ("l0_mlp", d_), ("l0_rest", d_)):
        red[tag].finish(behind, bufs)
    for n, full in zip(BIG, sibling_merge("rs_sibling_merge", [bufs[n] for n in BIG])):
        grad[n] = full.reshape(w[n].shape)
    for n in BIG:
        adam_big(n, True)

    return (loss, grad_x, *[grad[n] for n in WEIGHT_NAMES], *[delta[n] for n in WEIGHT_NAMES],
            *[new_m[n] for n in WEIGHT_NAMES], *[new_v[n] for n in WEIGHT_NAMES])
```

```python
import functools

import numpy as np
import jax
import jax.numpy as jnp
from jax import lax
from jax.experimental import pallas as pl
from jax.experimental.pallas import tpu as pltpu

F32 = jnp.float32
BF = jnp.bfloat16
MESH = pl.DeviceIdType.MESH

D = 1024
L = 2048
LC = 256
SEQ = L + LC
NEX = 2
R = NEX * SEQ
TB = 256
WIDE = 512
BPE = SEQ // TB
NBLK = R // TB
GRID_W = 64
FF = 4 * D
EPS = 1e-6
ROPE_THETA = 10000.0
N_MOD = 6
EV_IN = 1792
OD_IN = 1440
OD_PAD = 1536
VMEM_LIMIT = 60 * 1024 * 1024

ADAM_LR = 0.001
ADAM_B1 = 0.9
ADAM_B2 = 0.999
ADAM_EPS = 1e-08
ADAM_WD = 0.01
ADAM_STEP = 10

NN = (((1,), (0,)), ((), ()))
NT = (((1,), (1,)), ((), ()))
TN = (((0,), (0,)), ((), ()))


def _cparams(sem=None):
    return pltpu.CompilerParams(dimension_semantics=sem, vmem_limit_bytes=VMEM_LIMIT)


@jax.custom_vjp
def _mm(a, b):
    return jnp.dot(a.astype(BF), b.astype(BF), preferred_element_type=F32)


def _mm_fwd(a, b):
    return _mm(a, b), (a, b)


def _mm_bwd(res, g):
    a, b = res
    gb = g.astype(BF)
    da = lax.dot_general(gb, b.astype(BF), NT, preferred_element_type=F32)
    db = lax.dot_general(a.astype(BF), gb, TN, preferred_element_type=F32)
    return da, db


_mm.defvjp(_mm_fwd, _mm_bwd)


@jax.custom_vjp
def _swap(x):
    n = x.shape[-1]
    ax = x.ndim - 1
    lane = lax.broadcasted_iota(jnp.int32, x.shape, ax)
    return jnp.where(lane % 2 == 0, pltpu.roll(x, n - 1, ax), pltpu.roll(x, 1, ax))


_swap.defvjp(lambda x: (_swap(x), None), lambda _, g: (_swap(g),))


def _rope(x, cos, sin):
    return x * cos + _swap(x) * sin


def _rmsn(x, g):
    return x * lax.rsqrt(jnp.mean(x * x, axis=-1, keepdims=True) + EPS) * g


def _split_dot(a, m):
    hi = a.astype(BF)
    lo = (a - hi.astype(F32)).astype(BF)
    return jnp.dot(hi, m, preferred_element_type=F32) + jnp.dot(lo, m, preferred_element_type=F32)


@jax.custom_vjp
def _group_mean(a, avg):
    return _split_dot(a, avg)


_group_mean.defvjp(lambda a, avg: (_split_dot(a, avg), avg),
                   lambda avg, g: (_split_dot(g, avg), jnp.zeros_like(avg)))


def _grmsn(x, g, avg):
    return x * lax.rsqrt(_group_mean(x * x, avg) + EPS) * g


def _modnorm(x, g, sh, sc):
    return _rmsn(x, g) * (1.0 + sc) + sh


def _gelu(x):
    return 0.5 * x * (1.0 + jnp.tanh(0.7978845608028654 * (x + 0.044715 * (x * x * x))))


def _silu(x):
    return x * jax.nn.sigmoid(x)


def _acc(ref, val, first):
    @pl.when(first)
    def _():
        ref[...] = val

    @pl.when(jnp.logical_not(first))
    def _():
        ref[...] += val


def _seg(i):
    return 2 * (i // BPE) + jnp.minimum(i % BPE, 1)


def _seg_first(i):
    return (i % BPE) <= 1


class _Either:
    def __init__(self, pick_first, first, second):
        self.pick_first, self.first, self.second = pick_first, first, second

    def __getitem__(self, idx):
        return jnp.where(self.pick_first, self.first[idx], self.second[idx])


def _rb_call(name, body, row_in=(), mod_in=(), pos_in=(), full_in=(), shift_in=(),
             row_out=(), seg_out=(), acc_out=(), scratch=(), after=None, col_in=(), rows=TB, idle=None):
    assert rows == TB or not (mod_in or pos_in or shift_in or seg_out or col_in)
    in_specs, args, pairs = [], [], []
    for a in row_in:
        if isinstance(a, tuple):
            pairs.append(len(args))
            in_specs.append(pl.BlockSpec((TB, a[0].shape[1]), lambda i: (i // BPE, 0)))
            in_specs.append(pl.BlockSpec(
                (TB, a[1].shape[1]), lambda i: ((i // BPE) * (L // TB) + jnp.maximum(i % BPE - 1, 0), 0)))
            args += list(a)
        else:
            in_specs.append(pl.BlockSpec((rows, a.shape[1]), lambda i: (i, 0)))
            args.append(a)
    for a in col_in:
        in_specs.append(pl.BlockSpec((a.shape[0], TB), lambda i: (0, i)))
        args.append(a)
    for tab, m in mod_in:
        in_specs.append(pl.BlockSpec((1, 1, D), lambda i, m=m: (_seg(i) * N_MOD + m, 0, 0)))
        args.append(tab)
    for a in pos_in:
        in_specs.append(pl.BlockSpec((TB, a.shape[1]), lambda i: (i % BPE, 0)))
        args.append(a)
    for a in full_in:
        in_specs.append(pl.BlockSpec(a.shape, lambda i, n=a.ndim: (0,) * n))
        args.append(a)
    for a, d in shift_in:
        in_specs.append(pl.BlockSpec((TB, a.shape[1]), lambda i, d=d: (jnp.clip(i + d, 0, NBLK - 1), 0)))
        args.append(a)
    n_in = len(args)
    if after is not None:
        in_specs.append(pl.BlockSpec(after.shape, lambda i, n=after.ndim: (0,) * n))
        args.append(after)
    out_specs, out_shape = [], []
    for w, dt, *lat in row_out:
        if lat:
            out_specs.append(pl.BlockSpec(
                (TB, w), lambda i: ((i // BPE) * (L // TB) + jnp.maximum(i % BPE - 1, 0), 0)))
            out_shape.append(jax.ShapeDtypeStruct((NEX * L, w), dt))
        else:
            out_specs.append(pl.BlockSpec((rows, w), lambda i: (i, 0)))
            out_shape.append(jax.ShapeDtypeStruct((R, w), dt))
    for w in seg_out:
        out_specs.append(pl.BlockSpec((1, 1, w), lambda i: (_seg(i), 0, 0)))
        out_shape.append(jax.ShapeDtypeStruct((4, 1, w), F32))
    for shp in acc_out:
        out_specs.append(pl.BlockSpec(shp, lambda i, n=len(shp): (0,) * n))
        out_shape.append(jax.ShapeDtypeStruct(shp, F32))

    def kern(*refs):
        i = pl.program_id(0)
        ins = list(refs[:n_in])
        for k in reversed(pairs):
            ins[k:k + 2] = [_Either((i % BPE) == 0, ins[k], ins[k + 1])]
        if idle is None:
            body(i, *ins, *refs[len(args):])
        else:
            @pl.when((i % BPE) >= 1)
            def _():
                body(i, *ins, *refs[len(args):])

            @pl.when((i % BPE) == 0)
            def _():
                idle(i, *ins, *refs[len(args):])

    sem = ("arbitrary",) if (seg_out or acc_out or any(len(r) > 2 for r in row_out)) else ("parallel",)
    return pl.pallas_call(kern, grid=(R // rows,), in_specs=in_specs, out_specs=out_specs, out_shape=out_shape,
                          scratch_shapes=list(scratch), compiler_params=_cparams(sem), name=name)(*args)


def modnorm_fwd(name, x, mods, g, m_sh, m_sc):
    def body(i, x_ref, sh_ref, sc_ref, g_ref, h_ref):
        h_ref[...] = _modnorm(x_ref[...], g_ref[...], sh_ref[0], sc_ref[0]).astype(BF)

    return _rb_call(name, body, row_in=(x,), mod_in=((mods, m_sh), (mods, m_sc)), full_in=(g,),
                    row_out=((D, BF),))[0]


def _gate_grads(dx, y_ref, gt_ref, dy_ref, dgt_ref, i):
    dy_ref[...] = (dx * gt_ref[0]).astype(BF)
    _acc(dgt_ref, jnp.sum(dx * y_ref[...].astype(F32), axis=0, keepdims=True)[None], _seg_first(i))


def modnorm_bwd(name, x, dh, dx_in, mods, g, m_sh, m_sc, gate=None, after=None, lat_only=False, last=False):
    def body(i, x_ref, dh_ref, dxin_ref, *rest):
        if gate:
            y_ref, sh_ref, sc_ref, gt_ref, g_ref, dx_ref, dy_ref, dgt_ref, dsh_ref, dsc_ref, dg_ref = rest
        else:
            sh_ref, sc_ref, g_ref, dx_ref, dsh_ref, dsc_ref, dg_ref = rest
        _, vjp = jax.vjp(_modnorm, x_ref[...], g_ref[...], sh_ref[0], sc_ref[0])
        dx, dg, dsh, dsc = vjp(dh_ref[...].astype(F32))
        dx = dxin_ref[...] + dx
        dx_ref[...] = dx
        if gate:
            _gate_grads(dx, y_ref, gt_ref, dy_ref, dgt_ref, i)
        _acc(dsh_ref, dsh[None], _seg_first(i))
        _acc(dsc_ref, dsc[None], _seg_first(i))
        _acc(dg_ref, dg, i == 0)

    def idle(i, x_ref, dh_ref, dxin_ref, y_ref, sh_ref, sc_ref, gt_ref, g_ref, dx_ref, dy_ref, dgt_ref, dsh_ref,
             dsc_ref, dg_ref):
        dx_ref[...] = dxin_ref[...]
        _zero(dy_ref, dgt_ref, dsh_ref, dsc_ref)
        _zero_at_start(i, dg_ref)

    if gate:
        y, gmods, m = gate
        return _rb_call(name, body, row_in=(x, dh, dx_in, y), mod_in=((mods, m_sh), (mods, m_sc), (gmods, m)),
                        full_in=(g,), row_out=((D, F32), (D, BF)), seg_out=(D, D, D), acc_out=((1, D),), after=after,
                        idle=idle if last else None)
    return _rb_call(name, body, row_in=(x, dh, dx_in), mod_in=((mods, m_sh), (mods, m_sc)), full_in=(g,),
                    row_out=((D, F32, "lat") if lat_only else (D, F32),), seg_out=(D, D), acc_out=((1, D),),
                    after=after)


def proj_in(name, h, w):
    n = w.shape[1]

    def body(i, h_ref, w_ref, o_ref):
        o_ref[...] = jnp.dot(h_ref[...], w_ref[...], preferred_element_type=F32).astype(BF)

    return _rb_call(name, body, row_in=(h,), full_in=(w,), row_out=((n, BF),), rows=WIDE)[0]


def _zero(*refs):
    for r in refs:
        r[...] = jnp.zeros_like(r)


def _zero_at_start(i, *refs):
    @pl.when(i == 0)
    def _():
        _zero(*refs)


def proj_out(name, a1, a2, w, x, mods, m_gate, g_next, m_sh, m_sc, last=False):
    k1 = a1.shape[1]

    def idle(i, a1_ref, a2_ref, x_ref, gt_ref, sh_ref, sc_ref, w_ref, g_ref, xo_ref, y_ref, h_ref):
        xo_ref[...] = x_ref[...]
        _zero(y_ref, h_ref)

    def body(i, a1_ref, a2_ref, x_ref, gt_ref, sh_ref, sc_ref, w_ref, g_ref, xo_ref, y_ref, h_ref):
        y = jnp.dot(a1_ref[...], w_ref[:k1, :], preferred_element_type=F32)
        y = y + jnp.dot(a2_ref[...], w_ref[k1:, :], preferred_element_type=F32)
        y_ref[...] = y.astype(BF)
        xn = x_ref[...] + gt_ref[0] * y
        xo_ref[...] = xn
        h_ref[...] = _modnorm(xn, g_ref[...], sh_ref[0], sc_ref[0]).astype(BF)

    return _rb_call(name, body, row_in=(a1, a2, x), mod_in=((mods, m_gate), (mods, m_sh), (mods, m_sc)),
                    full_in=(w, g_next), row_out=((D, F32), (D, BF), (D, BF)), idle=idle if last else None)


def mlp_up(name, h, w1):
    def body(i, h_ref, w_ref, a_ref, f_ref):
        hv = h_ref[...]
        for n in range(4):
            a = jnp.dot(hv, w_ref[n], preferred_element_type=F32)
            a_ref[:, n * D:(n + 1) * D] = a.astype(BF)
            r = jnp.maximum(a, 0.0)
            f_ref[:, n * D:(n + 1) * D] = (r * r).astype(BF)

    return _rb_call(name, body, row_in=(h,), full_in=(w1,), row_out=((FF, BF), (FF, BF)), rows=WIDE)


def mlp_down(name, f, w2, x, mods, m_gate, nxt=None):
    def body(i, f_ref, x_ref, gt_ref, *rest):
        if nxt:
            sh_ref, sc_ref, w_ref, g_ref, xo_ref, y_ref, h_ref = rest
        else:
            w_ref, xo_ref, y_ref = rest
        y = jnp.dot(f_ref[:, 0:D], w_ref[0], preferred_element_type=F32)
        for n in range(1, 4):
            y = y + jnp.dot(f_ref[:, n * D:(n + 1) * D], w_ref[n], preferred_element_type=F32)
        xn = x_ref[...] + gt_ref[0] * y
        y_ref[...] = y.astype(BF)
        xo_ref[...] = xn
        if nxt:
            h_ref[...] = _modnorm(xn, g_ref[...], sh_ref[0], sc_ref[0]).astype(BF)

    if nxt:
        return _rb_call(name, body, row_in=(f, x), mod_in=((mods, m_gate), (nxt[0], 0), (nxt[0], 1)),
                        full_in=(w2, nxt[1]), row_out=((D, F32), (D, BF), (D, BF)))
    def idle(i, f_ref, x_ref, gt_ref, w_ref, xo_ref, y_ref):
        xo_ref[...] = x_ref[...]
        _zero(y_ref)

    return _rb_call(name, body, row_in=(f, x), mod_in=((mods, m_gate),), full_in=(w2,),
                    row_out=((D, F32), (D, BF)), idle=idle)


def mm_nt(name, g, w, after=None):
    k = w.shape[0]

    def body(i, g_ref, w_ref, o_ref):
        o_ref[...] = lax.dot_general(g_ref[...], w_ref[...], NT, preferred_element_type=F32).astype(BF)

    return _rb_call(name, body, row_in=(g,), full_in=(w,), row_out=((k, BF),), after=after, rows=WIDE)[0]


def mlp_bwd_da(name, dy, w2, a, after=None):
    def body(i, dy_ref, a_ref, w_ref, da_ref):
        dyv = dy_ref[...]
        for n in range(4):
            df = lax.dot_general(dyv, w_ref[n], NT, preferred_element_type=F32)
            av = a_ref[:, n * D:(n + 1) * D].astype(F32)
            da_ref[:, n * D:(n + 1) * D] = (df * (2.0 * jnp.maximum(av, 0.0))).astype(BF)

    return _rb_call(name, body, row_in=(dy, a), full_in=(w2,), row_out=((FF, BF),), after=after, rows=WIDE)[0]


def mlp_bwd_dh(name, da, w1):
    def body(i, da_ref, w_ref, dh_ref):
        acc = lax.dot_general(da_ref[:, 0:D], w_ref[0], NT, preferred_element_type=F32)
        for n in range(1, 4):
            acc = acc + lax.dot_general(da_ref[:, n * D:(n + 1) * D], w_ref[n], NT, preferred_element_type=F32)
        dh_ref[...] = acc.astype(BF)

    return _rb_call(name, body, row_in=(da,), full_in=(w1,), row_out=((D, BF),), rows=WIDE)[0]


TN_ROWS = 2304


def mm_tn(name, a, g, tiles, th, tw, rows=TN_ROWS):
    nt = len(tiles)
    acs = jnp.asarray([t[0] for t in tiles], jnp.int32)
    gcs = jnp.asarray([t[1] for t in tiles], jnp.int32)
    nr = R // rows

    def kern(ac_ref, gc_ref, a_ref, g_ref, o_ref, acc_ref):
        r = pl.program_id(1)

        @pl.when(r == 0)
        def _():
            acc_ref[...] = jnp.zeros_like(acc_ref)

        acc_ref[...] += lax.dot_general(a_ref[...], g_ref[...], TN, preferred_element_type=F32)

        @pl.when(r == nr - 1)
        def _():
            o_ref[...] = acc_ref[...].astype(BF)

    grid_spec = pltpu.PrefetchScalarGridSpec(
        num_scalar_prefetch=2, grid=(nt, nr),
        in_specs=[pl.BlockSpec((rows, th), lambda t, r, ac, gc: (r, ac[t])),
                  pl.BlockSpec((rows, tw), lambda t, r, ac, gc: (r, gc[t]))],
        out_specs=pl.BlockSpec((None, th, tw), lambda t, r, ac, gc: (t, 0, 0)),
        scratch_shapes=[pltpu.VMEM((th, tw), F32)])
    return pl.pallas_call(kern, grid_spec=grid_spec, out_shape=jax.ShapeDtypeStruct((nt, th, tw), BF),
                          compiler_params=_cparams(("parallel", "arbitrary")), name=name)(acs, gcs, a, g)


def _even_tok(q, k, zus, zvs, gq, gk, gss, ws, bs, cq, sq, ck, sk, avg, lo, hi):
    avg2 = avg[:128, :128]
    qr = _rope(_grmsn(q, gq, avg), cq, sq) * GQA_SCALE
    kr = _rope(_grmsn(k, gk, avg2), ck, sk)
    ms = []
    for b in range(4):
        v = _grmsn(_gelu(zvs[b]), gss[b], avg2)
        sv = lo * (_mm(ws[2 * b], v) + bs[2 * b]) + hi * (_mm(ws[2 * b + 1], v) + bs[2 * b + 1])
        ms.append(_gelu(zus[b]) * sv)
    return qr, kr, ms


def _even_operands(p_ref, rs, gq_ref, gk_ref, gs_ref, w_ref, b_ref):
    return (p_ref[rs, 0:512].astype(F32), p_ref[rs, 512:640].astype(F32),
            [p_ref[rs, 768 + 128 * b:896 + 128 * b].astype(F32) for b in range(4)],
            [p_ref[rs, 1280 + 128 * b:1408 + 128 * b].astype(F32) for b in range(4)],
            gq_ref[...], gk_ref[...], [gs_ref[:, 128 * b:128 * b + 128] for b in range(4)],
            [w_ref[g] for g in range(8)], [b_ref[g] for g in range(8)])


def even_tok_fwd(p, cos, sin, gq, gk, gs, sgu_w, sgu_b, avg, masks):
    def body(i, p_ref, cos_ref, sin_ref, gq_ref, gk_ref, gs_ref, w_ref, b_ref, avg_ref, mk_ref, q_ref, kv_ref, m_ref):
        avgv, lo, hi = avg_ref[...], mk_ref[0, :, 0:128], mk_ref[1, :, 0:128]
        for c in range(2):
            rs = pl.ds(c * 128, 128)
            qr, kr, ms = _even_tok(*_even_operands(p_ref, rs, gq_ref, gk_ref, gs_ref, w_ref, b_ref),
                                   cos_ref[rs, :], sin_ref[rs, :], cos_ref[rs, 0:128], sin_ref[rs, 0:128],
                                   avgv, lo, hi)
            q_ref[rs, :] = qr.astype(BF)
            kv_ref[rs, 0:128] = kr.astype(BF)
            kv_ref[rs, 128:256] = p_ref[rs, 640:768]
            for b in range(4):
                m_ref[rs, 128 * b:128 * b + 128] = ms[b].astype(BF)

    return _rb_call("even_tok_fwd", body, row_in=(p,), pos_in=(cos, sin),
                    full_in=(gq, gk, gs, sgu_w, sgu_b, avg, masks), row_out=((512, BF), (256, BF), (512, BF)))


def even_tok_bwd(p, dq, dkvt, dcat, cos, sin, gq, gk, gs, sgu_w, sgu_b, avg, masks):
    def body(i, p_ref, dq_ref, dcat_ref, dkvt_ref, cos_ref, sin_ref, gq_ref, gk_ref, gs_ref, w_ref, b_ref,
             avg_ref, mk_ref, dp_ref, dgq_ref, dgk_ref, dgs_ref, dw_ref, db_ref):
        avgv, lo, hi = avg_ref[...], mk_ref[0, :, 0:128], mk_ref[1, :, 0:128]
        tot = None
        for c in range(2):
            rs = pl.ds(c * 128, 128)
            cq, sq, ck, sk = cos_ref[rs, :], sin_ref[rs, :], cos_ref[rs, 0:128], sin_ref[rs, 0:128]

            def f(q, k, zus, zvs, gq, gk, gss, ws, bs):
                return _even_tok(q, k, zus, zvs, gq, gk, gss, ws, bs, cq, sq, ck, sk, avgv, lo, hi)

            _, vjp = jax.vjp(f, *_even_operands(p_ref, rs, gq_ref, gk_ref, gs_ref, w_ref, b_ref))
            dk = dkvt_ref[0:128, c * 128:(c + 1) * 128].T
            dv = dkvt_ref[128:256, c * 128:(c + 1) * 128].T
            dms = [dcat_ref[rs, 512 + 128 * b:640 + 128 * b].astype(F32) for b in range(4)]
            d = vjp((dq_ref[rs, :].astype(F32), dk, dms))
            dp_ref[rs, 0:512] = d[0].astype(BF)
            dp_ref[rs, 512:640] = d[1].astype(BF)
            dp_ref[rs, 640:768] = dv.astype(BF)
            for b in range(4):
                dp_ref[rs, 768 + 128 * b:896 + 128 * b] = d[2][b].astype(BF)
                dp_ref[rs, 1280 + 128 * b:1408 + 128 * b] = d[3][b].astype(BF)
            part = [d[4], d[5]] + list(d[6]) + list(d[7]) + list(d[8])
            tot = part if tot is None else [x + y for x, y in zip(tot, part)]
        refs = ([dgq_ref, dgk_ref] + [dgs_ref.at[:, 128 * b:128 * b + 128] for b in range(4)]
                + [dw_ref.at[g] for g in range(8)] + [db_ref.at[g] for g in range(8)])
        for ref, val in zip(refs, tot):
            _acc(ref, val, i == 0)

    return _rb_call("even_tok_bwd", body, row_in=(p, dq, dcat), col_in=(dkvt,), pos_in=(cos, sin),
                    full_in=(gq, gk, gs, sgu_w, sgu_b, avg, masks), row_out=((EV_IN, BF),),
                    acc_out=((1, 512), (1, 128), (1, 512), (8, 128, 128), (8, 128, 1)))


MLA_SCALE = 96 ** -0.5
GQA_SCALE = 64 ** -0.5


def _odd_tok(cq, ckv, kr, za, zg, gq, gkv, wq, wkk, wkv, spread, cr, sr, ck, sk):
    cqn = _rmsn(cq, gq)
    q = _rope(_mm(cqn, wq), cr, sr) * MLA_SCALE
    ckn = _rmsn(ckv, gkv)
    k = _mm(ckn, wkk) + _mm(_rope(kr, ck, sk), spread)
    v = _mm(ckn, wkv)
    y = za * jax.nn.sigmoid(zg)
    return q, k, v, y


def odd_tok_fwd(p, cos, sin, gq, gkv, wq, wkk, wkv, spread):
    def body(i, p_ref, cos_ref, sin_ref, gq_ref, gkv_ref, wq_ref, wkk_ref, wkv_ref, sp_ref, q_ref, kv_ref, y_ref):
        q, k, v, y = _odd_tok(
            p_ref[:, 0:256].astype(F32), p_ref[:, 256:384].astype(F32), p_ref[:, 384:512].astype(F32),
            p_ref[:, 512:1024].astype(F32), p_ref[:, 1024:1536].astype(F32),
            gq_ref[...], gkv_ref[...], wq_ref[...], wkk_ref[...], wkv_ref[...], sp_ref[...],
            cos_ref[:, 0:768], sin_ref[:, 0:768], cos_ref[:, 768:896], sin_ref[:, 768:896])
        q_ref[...] = q.astype(BF)
        kv_ref[:, 0:768] = k.astype(BF)
        kv_ref[:, 768:1280] = v.astype(BF)
        y_ref[...] = y.astype(BF)

    return _rb_call("odd_tok_fwd", body, row_in=(p,), pos_in=(cos, sin), full_in=(gq, gkv, wq, wkk, wkv, spread),
                    row_out=((768, BF), (1280, BF), (512, BF)))


def odd_tok_bwd(p, dq, dkvt, dy, cos, sin, gq, gkv, wq, wkk, wkv, spread):
    def body(i, p_ref, dq_ref, dy_ref, dkvt_ref, cos_ref, sin_ref, gq_ref, gkv_ref, wq_ref, wkk_ref, wkv_ref, sp_ref,
             dp_ref, dgq_ref, dgkv_ref, dwq_ref, dwkk_ref, dwkv_ref):
        cr, sr, ck, sk = cos_ref[:, 0:768], sin_ref[:, 0:768], cos_ref[:, 768:896], sin_ref[:, 768:896]
        spread_v = sp_ref[...]

        def f(cq, ckv, kr, za, zg, gq, gkv, wq, wkk, wkv):
            return _odd_tok(cq, ckv, kr, za, zg, gq, gkv, wq, wkk, wkv, spread_v, cr, sr, ck, sk)

        _, vjp = jax.vjp(f, p_ref[:, 0:256].astype(F32), p_ref[:, 256:384].astype(F32),
                         p_ref[:, 384:512].astype(F32), p_ref[:, 512:1024].astype(F32),
                         p_ref[:, 1024:1536].astype(F32), gq_ref[...], gkv_ref[...], wq_ref[...],
                         wkk_ref[...], wkv_ref[...])
        d = vjp((dq_ref[...].astype(F32), dkvt_ref[0:768, :].T, dkvt_ref[768:1280, :].T, dy_ref[...].astype(F32)))
        dp_ref[:, 0:256] = d[0].astype(BF)
        dp_ref[:, 256:384] = d[1].astype(BF)
        dp_ref[:, 384:512] = d[2].astype(BF)
        dp_ref[:, 512:1024] = d[3].astype(BF)
        dp_ref[:, 1024:1536] = d[4].astype(BF)
        for ref, val in zip((dgq_ref, dgkv_ref, dwq_ref, dwkk_ref, dwkv_ref), d[5:]):
            _acc(ref, val, i == 0)

    return _rb_call("odd_tok_bwd", body, row_in=(p, dq, dy), col_in=(dkvt,), pos_in=(cos, sin),
                    full_in=(gq, gkv, wq, wkk, wkv, spread), row_out=((OD_PAD, BF),),
                    acc_out=((1, 256), (1, 128), (256, 768), (128, 768), (128, 512)))


GQA_HEADS = [(64 * h, 64 * (h // 4), 64, 128 + 64 * (h // 4)) for h in range(8)]
MLA_HEADS = [(96 * h, 96 * h, 96, 768 + 64 * h) for h in range(8)]


def _by_block(j, run):
    @pl.when(j == 0)
    def _():
        run(LC)

    @pl.when(j > 0)
    def _():
        run(SEQ)


def attn_fwd(name, q, kv, heads):
    qw, kvw = q.shape[1], kv.shape[1]

    def kern(q_ref, kv_ref, o_ref, lse_ref):
        def run(nk):
            for h, (qo, ko, w, vo) in enumerate(heads):
                s = lax.dot_general(q_ref[:, qo:qo + w], kv_ref[0:nk, ko:ko + w], NT, preferred_element_type=F32)
                m = jnp.max(s, axis=-1, keepdims=True)
                p = jnp.exp(s - m)
                l = jnp.sum(p, axis=-1, keepdims=True)
                o = jnp.dot(p.astype(BF), kv_ref[0:nk, vo:vo + 64], preferred_element_type=F32) / l
                o_ref[:, 64 * h:64 * h + 64] = o.astype(BF)
                lse_ref[:, h:h + 1] = m + jnp.log(l)

        _by_block(pl.program_id(1), run)

    return pl.pallas_call(
        kern, grid=(NEX, BPE),
        in_specs=[pl.BlockSpec((TB, qw), lambda e, j: (e * BPE + j, 0)),
                  pl.BlockSpec((SEQ, kvw), lambda e, j: (e, 0))],
        out_specs=[pl.BlockSpec((TB, 512), lambda e, j: (e * BPE + j, 0)),
                   pl.BlockSpec((TB, 8), lambda e, j: (e * BPE + j, 0))],
        out_shape=[jax.ShapeDtypeStruct((R, 512), BF), jax.ShapeDtypeStruct((R, 8), F32)],
        compiler_params=_cparams(("parallel", "arbitrary")), name=name)(q, kv)


def attn_bwd(name, q, kv, o, dcat, lse, heads):
    qw, kvw = q.shape[1], kv.shape[1]

    def kern(q_ref, kv_ref, o_ref, do_ref, lse_ref, dq_ref, dkvt_ref):
        j = pl.program_id(1)

        @pl.when(j == 0)
        def _():
            dkvt_ref[...] = jnp.zeros_like(dkvt_ref)

        def run(nk):
            for h, (qo, ko, w, vo) in enumerate(heads):
                qh = q_ref[:, qo:qo + w]
                kh = kv_ref[0:nk, ko:ko + w]
                s = lax.dot_general(qh, kh, NT, preferred_element_type=F32)
                p = jnp.exp(s - lse_ref[:, h:h + 1])
                do = do_ref[:, 64 * h:64 * h + 64]
                dsum = jnp.sum(do.astype(F32) * o_ref[:, 64 * h:64 * h + 64].astype(F32), axis=-1, keepdims=True)
                dp = lax.dot_general(do, kv_ref[0:nk, vo:vo + 64], NT, preferred_element_type=F32)
                ds = (p * (dp - dsum)).astype(BF)
                dkvt_ref[vo:vo + 64, 0:nk] += lax.dot_general(do, p.astype(BF), TN, preferred_element_type=F32)
                dq_ref[:, qo:qo + w] = jnp.dot(ds, kh, preferred_element_type=F32).astype(BF)
                dkvt_ref[ko:ko + w, 0:nk] += lax.dot_general(qh, ds, TN, preferred_element_type=F32)

        _by_block(j, run)

    return pl.pallas_call(
        kern, grid=(NEX, BPE),
        in_specs=[pl.BlockSpec((TB, qw), lambda e, j: (e * BPE + j, 0)),
                  pl.BlockSpec((SEQ, kvw), lambda e, j: (e, 0)),
                  pl.BlockSpec((TB, 512), lambda e, j: (e * BPE + j, 0)),
                  pl.BlockSpec((TB, 512), lambda e, j: (e * BPE + j, 0)),
                  pl.BlockSpec((TB, 8), lambda e, j: (e * BPE + j, 0))],
        out_specs=[pl.BlockSpec((TB, qw), lambda e, j: (e * BPE + j, 0)),
                   pl.BlockSpec((kvw, SEQ), lambda e, j: (0, e))],
        out_shape=[jax.ShapeDtypeStruct((R, qw), BF), jax.ShapeDtypeStruct((kvw, R), F32)],
        compiler_params=_cparams(("parallel", "arbitrary")), name=name)(q, kv, o, dcat, lse)


HALO = 16
CONV_K = 31


def _fill_ext(ext_ref, prev_ref, cur_ref, next_ref, i):
    j = i % BPE
    has_prev = (j >= 2).astype(F32)
    has_next = jnp.logical_and(j >= 1, j <= BPE - 2).astype(F32)
    ext_ref[0:HALO, :] = prev_ref[TB - HALO:TB, :].astype(F32) * has_prev
    ext_ref[HALO:HALO + TB, :] = cur_ref[...].astype(F32)
    ext_ref[HALO + TB:2 * HALO + TB, :] = next_ref[0:HALO, :].astype(F32) * has_next


PHASE_ROWS = TB + 24


def _phases(ext_ref, ph_ref):
    for r in range(8):
        ph_ref[r] = ext_ref[r:r + PHASE_ROWS, :]


def _window(ph_ref, off):
    return ph_ref[off % 8, 8 * (off // 8):8 * (off // 8) + TB, :]


def _ln_silu(z, g, b):
    mu = jnp.mean(z, axis=-1, keepdims=True)
    zc = z - mu
    var = jnp.mean(zc * zc, axis=-1, keepdims=True)
    return _silu(zc * lax.rsqrt(var + EPS) * g + b)


def conf_fwd(y, cw, cb, lg, lb):
    def body(i, cur_ref, cw_ref, cb_ref, lg_ref, lb_ref, prev_ref, next_ref, z_ref, c_ref, ext_ref, ph_ref):
        _fill_ext(ext_ref, prev_ref, cur_ref, next_ref, i)
        _phases(ext_ref, ph_ref)
        acc = _window(ph_ref, 1) * cw_ref[0:1, :]
        for k in range(1, CONV_K):
            acc = acc + _window(ph_ref, k + 1) * cw_ref[k:k + 1, :]
        z = acc + cb_ref[...]
        z_ref[...] = z.astype(BF)
        c_ref[...] = _ln_silu(z, lg_ref[...], lb_ref[...]).astype(BF)

    def idle(i, cur_ref, cw_ref, cb_ref, lg_ref, lb_ref, prev_ref, next_ref, z_ref, c_ref, ext_ref, ph_ref):
        _zero(z_ref, c_ref)

    return _rb_call("conf_fwd", body, row_in=(y,), full_in=(cw, cb, lg, lb), shift_in=((y, -1), (y, 1)),
                    row_out=((512, BF), (512, BF)), idle=idle,
                    scratch=(pltpu.VMEM((TB + 2 * HALO, 512), F32), pltpu.VMEM((8, PHASE_ROWS, 512), F32)))


def conf_bwd_ln(z, dcat, lg, lb):
    def body(i, z_ref, dcat_ref, lg_ref, lb_ref, dz_ref, dlg_ref, dlb_ref, dcb_ref):
        _, vjp = jax.vjp(_ln_silu, z_ref[...].astype(F32), lg_ref[...], lb_ref[...])
        dz, dlg, dlb = vjp(dcat_ref[:, 512:1024].astype(F32))
        dz_ref[...] = dz.astype(BF)
        _acc(dlg_ref, dlg, i == 0)
        _acc(dlb_ref, dlb, i == 0)
        _acc(dcb_ref, jnp.sum(dz, axis=0, keepdims=True), i == 0)

    def idle(i, z_ref, dcat_ref, lg_ref, lb_ref, dz_ref, dlg_ref, dlb_ref, dcb_ref):
        _zero(dz_ref)
        _zero_at_start(i, dlg_ref, dlb_ref, dcb_ref)

    return _rb_call("conf_bwd_ln", body, row_in=(z, dcat), full_in=(lg, lb), row_out=((512, BF),),
                    acc_out=((1, 512), (1, 512), (1, 512)), idle=idle)


def conf_bwd_conv(y, dz, cw):
    def body(i, y_ref, dz_ref, cw_ref, yp_ref, yn_ref, dzp_ref, dzn_ref, dy_ref, dcw_ref, ext_ref, phy_ref, phd_ref):
        _fill_ext(ext_ref, yp_ref, y_ref, yn_ref, i)
        _phases(ext_ref, phy_ref)
        _fill_ext(ext_ref, dzp_ref, dz_ref, dzn_ref, i)
        _phases(ext_ref, phd_ref)
        dzv = dz_ref[...].astype(F32)

        @pl.when(i == 0)
        def _():
            dcw_ref[...] = jnp.zeros_like(dcw_ref)

        acc = None
        for k in range(CONV_K):
            t = _window(phd_ref, CONV_K - k) * cw_ref[k:k + 1, :]
            acc = t if acc is None else acc + t
            dcw_ref[k:k + 1, :] += jnp.sum(dzv * _window(phy_ref, k + 1), axis=0, keepdims=True)
        dy_ref[...] = acc.astype(BF)

    def idle(i, y_ref, dz_ref, cw_ref, yp_ref, yn_ref, dzp_ref, dzn_ref, dy_ref, dcw_ref, ext_ref, phy_ref, phd_ref):
        _zero(dy_ref)
        _zero_at_start(i, dcw_ref)

    return _rb_call("conf_bwd_conv", body, row_in=(y, dz), full_in=(cw,), idle=idle,
                    shift_in=((y, -1), (y, 1), (dz, -1), (dz, 1)), row_out=((512, BF),), acc_out=((32, 512),),
                    scratch=(pltpu.VMEM((TB + 2 * HALO, 512), F32), pltpu.VMEM((8, PHASE_ROWS, 512), F32),
                             pltpu.VMEM((8, PHASE_ROWS, 512), F32)))


def final_loss(x, target, fg, y, mods, m_gate):
    lpb = L // TB

    def kern(x_ref, t_ref, g_ref, y_ref, gt_ref, dx_ref, dy_ref, dgt_ref, loss_ref, dg_ref):
        i = pl.program_id(0)

        @pl.when((i % BPE) == 0)
        def _():
            _zero(dx_ref, dy_ref, dgt_ref)
            _zero_at_start(i, loss_ref, dg_ref)

        @pl.when((i % BPE) >= 1)
        def _():
            tv = t_ref[...]

            def f(x, g):
                err = _rmsn(x, g) - tv
                rowsum = jnp.sum(err * err, axis=-1, keepdims=True)
                return jnp.sum(rowsum, axis=0, keepdims=True) * (0.5 / D)

            lv, vjp = jax.vjp(f, x_ref[...], g_ref[...])
            dx, dg = vjp(jnp.ones((1, 1), F32))
            dx_ref[...] = dx
            _gate_grads(dx, y_ref, gt_ref, dy_ref, dgt_ref, i)
            loss_ref[...] += jnp.zeros((8, 128), F32) + lv
            dg_ref[...] += dg

    row = pl.BlockSpec((TB, D), lambda i: (i, 0))
    return pl.pallas_call(
        kern, grid=(NBLK,),
        in_specs=[row, pl.BlockSpec((TB, D), lambda i: ((i // BPE) * lpb + jnp.maximum(i % BPE - 1, 0), 0)),
                  pl.BlockSpec((1, D), lambda i: (0, 0)), row,
                  pl.BlockSpec((1, 1, D), lambda i: (_seg(i) * N_MOD + m_gate, 0, 0))],
        out_specs=[row, row, pl.BlockSpec((1, 1, D), lambda i: (_seg(i), 0, 0)),
                   pl.BlockSpec((8, 128), lambda i: (0, 0)), pl.BlockSpec((1, D), lambda i: (0, 0))],
        out_shape=[jax.ShapeDtypeStruct((R, D), F32), jax.ShapeDtypeStruct((R, D), BF),
                   jax.ShapeDtypeStruct((4, 1, D), F32), jax.ShapeDtypeStruct((8, 128), F32),
                   jax.ShapeDtypeStruct((1, D), F32)],
        compiler_params=_cparams(("arbitrary",)), name="final_loss")(x, target, fg, y, mods)


NC = 24


def _dot3(a, b, dims):
    ah, bh = a.astype(BF), b.astype(BF)
    al, bl = (a - ah.astype(F32)).astype(BF), (b - bh.astype(F32)).astype(BF)

    def dot(x, y):
        return lax.dot_general(x, y, dims, preferred_element_type=F32)

    return dot(ah, bh) + dot(ah, bl) + dot(al, bh)


def mods_fwd(call, ada_w, ada_b):
    cols = ada_w.shape[2]

    def kern(c_ref, w_ref, b_ref, o_ref):
        o_ref[...] = _dot3(_silu(c_ref[...]), w_ref[...], NN) + b_ref[...]

    return pl.pallas_call(
        kern, grid=(2,),
        in_specs=[pl.BlockSpec((NC, D), lambda l: (0, 0)), pl.BlockSpec((None, D, cols), lambda l: (l, 0, 0)),
                  pl.BlockSpec((None, 1, cols), lambda l: (l, 0, 0))],
        out_specs=pl.BlockSpec((None, NC, cols), lambda l: (l, 0, 0)),
        out_shape=jax.ShapeDtypeStruct((2, NC, cols), F32),
        compiler_params=_cparams(("parallel",)), name="mods_fwd")(call, ada_w, ada_b)


def ada_bwd(call, ada_w, dm):
    cols = ada_w.shape[2]

    def kern(c_ref, w_ref, dm_ref, gw_ref, dc_ref):
        l = pl.program_id(0)
        gw_ref[...] = _dot3(_silu(c_ref[...]), dm_ref[...], TN)
        part = _dot3(dm_ref[16:24, :], w_ref[...], NT)
        cc = c_ref[16:17, :]
        sg = jax.nn.sigmoid(cc)
        _acc(dc_ref, part * (sg * (1.0 + cc * (1.0 - sg))), l == 0)

    return pl.pallas_call(
        kern, grid=(2,),
        in_specs=[pl.BlockSpec((NC, D), lambda l: (0, 0)), pl.BlockSpec((None, D, cols), lambda l: (l, 0, 0)),
                  pl.BlockSpec((None, NC, cols), lambda l: (l, 0, 0))],
        out_specs=[pl.BlockSpec((None, D, cols), lambda l: (l, 0, 0)), pl.BlockSpec((8, D), lambda l: (0, 0))],
        out_shape=[jax.ShapeDtypeStruct((2, D, cols), F32), jax.ShapeDtypeStruct((8, D), F32)],
        compiler_params=_cparams(("arbitrary",)), name="ada_bwd")(call, ada_w, dm)


def sum_lead(name, a, after=None):
    n, r, c = a.shape
    tr = r
    for cand in (512, 256, 128, 64, 32, 16, 8):
        if r % cand == 0 and cand * c * 4 * n <= 8 * 1024 * 1024:
            tr = cand
            break
    extra = [] if after is None else [after]

    def kern(a_ref, *rest):
        acc = a_ref[0].astype(F32)
        for k in range(1, n):
            acc = acc + a_ref[k].astype(F32)
        rest[-1][...] = acc

    return pl.pallas_call(
        kern, grid=(r // tr,),
        in_specs=[pl.BlockSpec((n, tr, c), lambda i: (0, i, 0))]
        + [pl.BlockSpec(e.shape, lambda i, k=e.ndim: (0,) * k) for e in extra],
        out_specs=pl.BlockSpec((tr, c), lambda i: (i, 0)), out_shape=jax.ShapeDtypeStruct((r, c), F32),
        compiler_params=_cparams(("parallel",)), name=name)(a, *extra)


def add_pairs(name, hs, got, half):
    _, _, r, c = hs.shape

    def kern(half_ref, a_ref, b_ref, o_ref):
        o_ref[...] = (a_ref[...].astype(F32) + b_ref[...].astype(F32)).astype(BF)

    spec = pl.BlockSpec((None, r, c), lambda j, h: (j, 0, 0))
    grid_spec = pltpu.PrefetchScalarGridSpec(
        num_scalar_prefetch=1, grid=(4,),
        in_specs=[pl.BlockSpec((None, None, r, c), lambda j, h: (h[0], j, 0, 0)), spec], out_specs=spec)
    return pl.pallas_call(kern, grid_spec=grid_spec, out_shape=jax.ShapeDtypeStruct(got.shape, BF),
                          compiler_params=_cparams(("parallel",)), name=name)(half, hs, got)


def sum_slabs(name, land, own, where, full, lead):
    _, r, c = land.shape
    tr = r
    for cand in (512, 256, 128, 64, 32, 16):
        if r % cand == 0 and cand * c * 16 <= 4 * 1024 * 1024:
            tr = cand
            break

    def kern(where_ref, full_ref, land_ref, own_ref, o_ref):
        me = where_ref[0]
        acc = None
        for k in range(4):
            t = jnp.where(me == k, own_ref[k], land_ref[k]).astype(F32)
            acc = t if acc is None else acc + t
        o_ref[...] = acc

    spec = pl.BlockSpec((4, tr, c), lambda i, m: (0, i, 0))
    grid_spec = pltpu.PrefetchScalarGridSpec(
        num_scalar_prefetch=1, grid=(r // tr,), in_specs=[pl.BlockSpec(memory_space=pl.ANY), spec, spec],
        out_specs=pl.BlockSpec((None, None, tr, c), lambda i, m: (lead, m[1], i, 0)))
    return pl.pallas_call(kern, grid_spec=grid_spec, out_shape=jax.ShapeDtypeStruct(full.shape, F32),
                          input_output_aliases={1: 0}, compiler_params=_cparams(("parallel",)),
                          name=name)(where, full, land, own)


def adamw(name, w, g, m, v, again=False):
    r, c = w.shape
    tr = r
    for cand in (512, 256, 128, 64, 32, 16, 8):
        if r % cand == 0 and cand * c * 4 <= 2 * 1024 * 1024:
            tr = cand
            break
    c1 = 1.0 / (1.0 - ADAM_B1 ** ADAM_STEP)
    c2 = 1.0 / (1.0 - ADAM_B2 ** ADAM_STEP)

    def kern(w_ref, g_ref, m_ref, v_ref, d_ref, mo_ref, vo_ref, *go_ref):
        gv = g_ref[...]
        mn = ADAM_B1 * m_ref[...] + (1.0 - ADAM_B1) * gv
        vn = ADAM_B2 * v_ref[...] + (1.0 - ADAM_B2) * (gv * gv)
        d_ref[...] = -ADAM_LR * ((mn * c1) / (jnp.sqrt(vn * c2) + ADAM_EPS) + ADAM_WD * w_ref[...])
        mo_ref[...] = mn
        vo_ref[...] = vn
        if again:
            go_ref[0][...] = gv

    spec = pl.BlockSpec((tr, c), lambda i: (i, 0))
    shp = jax.ShapeDtypeStruct((r, c), F32)
    n_out = 4 if again else 3
    return pl.pallas_call(kern, grid=(r // tr,), in_specs=[spec] * 4, out_specs=[spec] * n_out,
                          out_shape=[shp] * n_out, compiler_params=_cparams(("parallel",)), name=name)(w, g, m, v)


def adamw_many(name, ws, gs, ms, vs):
    n = len(ws)
    c1 = 1.0 / (1.0 - ADAM_B1 ** ADAM_STEP)
    c2 = 1.0 / (1.0 - ADAM_B2 ** ADAM_STEP)

    def kern(*refs):
        w, g, m, v, d, mo, vo = (refs[k * n:(k + 1) * n] for k in range(7))
        for k in range(n):
            gv = g[k][...]
            mn = ADAM_B1 * m[k][...] + (1.0 - ADAM_B1) * gv
            vn = ADAM_B2 * v[k][...] + (1.0 - ADAM_B2) * (gv * gv)
            d[k][...] = -ADAM_LR * ((mn * c1) / (jnp.sqrt(vn * c2) + ADAM_EPS) + ADAM_WD * w[k][...])
            mo[k][...] = mn
            vo[k][...] = vn

    out = pl.pallas_call(kern, out_shape=[jax.ShapeDtypeStruct(a.shape, F32) for a in ws] * 3,
                         compiler_params=pltpu.CompilerParams(vmem_limit_bytes=VMEM_LIMIT),
                         name=name)(*ws, *gs, *ms, *vs)
    return out[:n], out[n:2 * n], out[2 * n:]


def all_gather8(name, xs, after=None):
    m_per, n = xs.shape
    extra = [] if after is None else [after]

    def body(x_ref, *rest):
        out_ref, send_sems, recv_sems, local_sem = rest[len(extra):]
        x, y, c = lax.axis_index("x"), lax.axis_index("y"), lax.axis_index("c")
        me, sibling = (x, y, c), (x, y, 1 - c)
        chips = [(1 - x, y), (x, 1 - y), (1 - x, 1 - y)]

        def rows(px, py, pc):
            return out_ref.at[pl.ds((4 * px + 2 * py + pc) * m_per, m_per), :]

        def copy(k, block, to, src=None):
            return pltpu.make_async_remote_copy(
                src_ref=rows(*block) if src is None else src, dst_ref=rows(*block),
                send_sem=send_sems.at[k], recv_sem=recv_sems.at[k], device_id=to, device_id_type=MESH)

        mine = pltpu.make_async_copy(x_ref, rows(*me), local_sem)
        mine.start()
        first = [copy(0, me, sibling, src=x_ref)]
        first += [copy(1 + j, me, (*chip, c), src=x_ref) for j, chip in enumerate(chips)]
        for cp in first:
            cp.start()
        passed = [copy(4 + j, (*chip, c), sibling) for j, chip in enumerate(chips)]
        for j, chip in enumerate(chips):
            copy(1 + j, (*chip, c), me).wait_recv()
            passed[j].start()
        copy(0, sibling, me).wait_recv()
        for j, chip in enumerate(chips):
            copy(4 + j, (*chip, 1 - c), me).wait_recv()
        for cp in first + passed:
            cp.wait_send()
        mine.wait()

    return pl.pallas_call(
        body, out_shape=jax.ShapeDtypeStruct((8 * m_per, n), xs.dtype),
        in_specs=[pl.BlockSpec(memory_space=pltpu.VMEM)] * (1 + len(extra)),
        out_specs=pl.BlockSpec(memory_space=pltpu.VMEM),
        scratch_shapes=[pltpu.SemaphoreType.DMA((7,)), pltpu.SemaphoreType.DMA((7,)), pltpu.SemaphoreType.DMA],
        compiler_params=pltpu.CompilerParams(vmem_limit_bytes=VMEM_LIMIT), name=name)(xs, *extra)


def sibling_merge(name, fulls):
    n = len(fulls)
    slots = [(a, l) for a in range(n) for l in range(fulls[a].shape[0])]

    def body(*refs):
        buf = refs[n:2 * n]
        send_sems, recv_sems = refs[2 * n], refs[2 * n + 1]
        c = lax.axis_index("c")
        sibling = (lax.axis_index("x"), lax.axis_index("y"), 1 - c)
        sends, recvs = [], []
        for k, (a, l) in enumerate(slots):
            kw = dict(send_sem=send_sems.at[k], recv_sem=recv_sems.at[k], device_id=sibling, device_id_type=MESH)
            sends.append(pltpu.make_async_remote_copy(src_ref=buf[a].at[l, c], dst_ref=buf[a].at[l, c], **kw))
            recvs.append(pltpu.make_async_remote_copy(src_ref=buf[a].at[l, c], dst_ref=buf[a].at[l, 1 - c], **kw))
        for cp in sends:
            cp.start()
        for cp in recvs:
            cp.wait_recv()
        for cp in sends:
            cp.wait_send()

    anyspec = pl.BlockSpec(memory_space=pl.ANY)
    return pl.pallas_call(
        body, out_shape=[jax.ShapeDtypeStruct(s.shape, s.dtype) for s in fulls],
        in_specs=[anyspec] * n, out_specs=[anyspec] * n, input_output_aliases={a: a for a in range(n)},
        scratch_shapes=[pltpu.SemaphoreType.DMA((len(slots),)), pltpu.SemaphoreType.DMA((len(slots),))],
        name=name)(*fulls)


def place_own(name, land, src, chip):
    c = src.shape[-1]
    r = src.size // c
    tr = r
    for cand in (1024, 512, 256, 128, 64, 32, 16):
        if r % cand == 0 and cand * c * 2 <= 2 * 1024 * 1024:
            tr = cand
            break

    def kern(chip_ref, land_ref, src_ref, out_ref):
        out_ref[...] = src_ref[...]

    grid_spec = pltpu.PrefetchScalarGridSpec(
        num_scalar_prefetch=1, grid=(r // tr,),
        in_specs=[pl.BlockSpec(memory_space=pl.ANY), pl.BlockSpec((tr, c), lambda i, m: (i, 0))],
        out_specs=pl.BlockSpec((None, tr, c), lambda i, m: (m[0], i, 0)))
    out = pl.pallas_call(kern, grid_spec=grid_spec, out_shape=jax.ShapeDtypeStruct((4, r, c), land.dtype),
                         input_output_aliases={1: 0}, compiler_params=_cparams(("parallel",)),
                         name=name)(chip, land.reshape(4, r, c), src.reshape(r, c))
    return out.reshape(land.shape)


def _half_copies(src, land, send_sems, recv_sems):
    c = lax.axis_index("c")
    sibling = (lax.axis_index("x"), lax.axis_index("y"), 1 - c)
    pairs = []
    for a in range(len(src)):
        cp = pltpu.make_async_remote_copy(src_ref=src[a].at[1 - c], dst_ref=land[a], send_sem=send_sems.at[a],
                                          recv_sem=recv_sems.at[a], device_id=sibling, device_id_type=MESH)
        pairs.append((cp, cp))
    return pairs


def _chip_copies(src, land, send_sems, recv_sems, scatter):
    x, y, c = lax.axis_index("x"), lax.axis_index("y"), lax.axis_index("c")
    me = 2 * x + y
    pairs = []
    for a in range(len(src)):
        for j, (px, py) in enumerate([(1 - x, y), (x, 1 - y), (1 - x, 1 - y)]):
            to = 2 * px + py
            out = src[a].at[to] if scatter else src[a]
            kw = dict(send_sem=send_sems.at[3 * a + j], recv_sem=recv_sems.at[3 * a + j], device_id=(px, py, c),
                      device_id_type=MESH)
            pairs.append((pltpu.make_async_remote_copy(src_ref=out, dst_ref=land[a].at[me], **kw),
                          pltpu.make_async_remote_copy(src_ref=out, dst_ref=land[a].at[to], **kw)))
    return pairs


_HBM = pl.BlockSpec(memory_space=pltpu.HBM)
_SEM = pl.BlockSpec(memory_space=pltpu.SEMAPHORE)


GATHER = (functools.partial(_chip_copies, scatter=False), 3)
SCATTER = (functools.partial(_chip_copies, scatter=True), 3)
TO_SIBLING = (_half_copies, 1)


def exchange_start(name, groups, plan):
    copies, per = plan
    sizes = [len(s) for s, _ in groups]
    flat = [a for s, l in groups for a in list(s) + list(l)]
    ng = len(groups)

    def body(*refs):
        ins, outs = refs[:len(flat)], refs[len(flat):]
        off = 0
        for g, n in enumerate(sizes):
            src, land = ins[off:off + n], ins[off + n:off + 2 * n]
            off += 2 * n
            for send, _ in copies(src, land, outs[2 * g], outs[2 * g + 1]):
                send.start()
        outs[-1][...] = jnp.zeros_like(outs[-1])

    out_shape = []
    for n in sizes:
        out_shape += [pltpu.SemaphoreType.DMA((per * n,)), pltpu.SemaphoreType.DMA((per * n,))]
    out_shape += [pltpu.HBM(a.shape, a.dtype) for a in flat] + [jax.ShapeDtypeStruct((8, 128), F32)]
    res = pl.pallas_call(
        body, out_shape=tuple(out_shape), in_specs=[_HBM] * len(flat),
        out_specs=tuple([_SEM] * (2 * ng) + [_HBM] * len(flat) + [pl.BlockSpec(memory_space=pltpu.VMEM)]),
        input_output_aliases={k: 2 * ng + k for k in range(len(flat))},
        compiler_params=pltpu.CompilerParams(has_side_effects=pltpu.SideEffectType.DATAFLOW_SIDE_EFFECTING),
        name=name)(*[pltpu.with_memory_space_constraint(a, pltpu.HBM) for a in flat])
    handles, off = [], 2 * ng
    for g, n in enumerate(sizes):
        handles.append((res[2 * g], res[2 * g + 1], list(res[off:off + n]), list(res[off + n:off + 2 * n])))
        off += 2 * n
    return handles, res[-1]


def exchange_wait(name, handle, after, plan):
    send_sems, recv_sems, srcs, lands = handle
    n = len(srcs)

    def body(*refs):
        src, land = refs[:n], refs[n:2 * n]
        for send, recv in plan[0](src, land, refs[2 * n], refs[2 * n + 1]):
            send.wait_send()
            recv.wait_recv()

    res = pl.pallas_call(
        body, out_shape=tuple(pltpu.HBM(a.shape, a.dtype) for a in srcs + lands),
        in_specs=[_HBM] * (2 * n) + [_SEM, _SEM, pl.BlockSpec(memory_space=pl.ANY)],
        out_specs=tuple([_HBM] * (2 * n)), input_output_aliases={k: k for k in range(2 * n)},
        compiler_params=pltpu.CompilerParams(has_side_effects=pltpu.SideEffectType.DATAFLOW_SIDE_EFFECTING),
        name=name)(*srcs, *lands, send_sems, recv_sems, after)
    return list(res[:n]), list(res[n:])


def _rope_tables(d_rot, reps):
    rows = L // GRID_W
    row = np.repeat(np.arange(rows), GRID_W).astype(np.float32)
    col = np.tile(np.arange(GRID_W), rows).astype(np.float32)
    d_axis = d_rot // 2
    inv = (ROPE_THETA ** (-np.arange(0, d_axis, 2, dtype=np.float32) / d_axis)).astype(np.float32)
    ang = np.concatenate([row[:, None] * inv, col[:, None] * inv], axis=-1).astype(np.float32)
    cos, sin = np.cos(ang).astype(np.float32), np.sin(ang).astype(np.float32)
    c = np.repeat(cos, 2, axis=-1)
    s = np.stack([-sin, sin], axis=-1).reshape(L, d_rot)
    c = np.concatenate([np.ones((LC, d_rot), np.float32), c], axis=0)
    s = np.concatenate([np.zeros((LC, d_rot), np.float32), s], axis=0)
    return np.tile(c, (1, reps)), np.tile(s, (1, reps))


def _group_consts():
    g = np.arange(512) // 64
    avg = (g[:, None] == g[None, :]).astype(np.float32) / 64.0
    masks = (np.arange(8)[:, None] == g[None, :]).astype(np.float32).reshape(8, 1, 512)
    return jnp.asarray(avg, BF), jnp.asarray(masks)


def _pack(items):
    flat = jnp.concatenate([a.reshape(-1).astype(F32) for a in items])
    n = flat.shape[0]
    rows = -(-n // D)
    rows = -(-rows // 8) * 8
    return jnp.pad(flat, (0, rows * D - n)).reshape(rows, D)


def _unpack(buf, shapes):
    lead = buf.shape[:-2]
    flat = buf.reshape(lead + (-1,))
    out, off = [], 0
    for shp in shapes:
        n = int(np.prod(shp))
        out.append(flat[..., off:off + n].reshape(lead + tuple(shp)))
        off += n
    return out


def _arrive(prm, key, after):
    if callable(prm[key]):
        prm[key](after)
    return prm[key]


def _layer_fwd(i, x, h, mods, prm, consts, nxt):
    sv = {}
    sv["x0"] = x
    sv["h"] = h
    p = proj_in(f"proj_in_{i}", h, _arrive(prm, "w_in", h))
    sv["p"] = p
    if i == 0:
        q, kv, m2 = even_tok_fwd(p, consts["cos_e"], consts["sin_e"], prm["gq"], prm["gk"], prm["gs"],
                                 prm["sgu_w"], prm["sgu_b"], consts["avg"], consts["masks"])
        o, lse = attn_fwd("attn_fwd_0", q, kv, GQA_HEADS)
        sv.update(q=q, kv=kv)
    else:
        q, kv, y = odd_tok_fwd(p, consts["cos_o"], consts["sin_o"], prm["gq"], prm["gkv"], prm["wq"], prm["wkk"],
                               prm["wkv"], consts["spread"])
        o, lse = attn_fwd("attn_fwd_1", q, kv, MLA_HEADS)
        z, m2 = conf_fwd(y, prm["conv_w"], prm["conv_b"], prm["ln_g"], prm["ln_b"])
        sv.update(q=q, kv=kv, y=y, z=z)
    sv.update(o=o, lse=lse, m2=m2)
    x1, y1, h2 = proj_out(f"proj_out_{i}", o, m2, _arrive(prm, "w_out", o), x, mods, 2, prm["norm2_g"], 3, 4,
                          last=nxt is None)
    sv.update(x1=x1, y1=y1)
    a, f = mlp_up(f"mlp_up_{i}", h2, _arrive(prm, "w1", h2))
    x2, y2, *h_next = mlp_down(f"mlp_down_{i}", f, prm["w2"], x1, mods, 5, nxt)
    sv.update(h2=h2, a=a, f=f, y2=y2)
    return x2, (h_next[0] if h_next else None), sv


def _layer_bwd(i, dx, dy2, dg2, sv, mods, prm, consts, hook, entry, below):
    gr = {}
    da = mlp_bwd_da(f"mlp_bwd_da_{i}", dy2, prm["w2"], sv["a"], after=entry)
    tiles8 = [(h, j) for h in range(2) for j in range(4)]
    gr["w1"] = mm_tn(f"grad_w1_{i}", sv["h2"], da, tiles8, 512, D, rows=R).reshape(2, 4, 512, D)
    gr["w2"] = mm_tn(f"grad_w2_{i}", sv["f"], dy2, [(2 * j + h, 0) for h in range(2) for j in range(4)],
                     512, D, rows=R).reshape(2, 4, 512, D)
    dh2 = mlp_bwd_dh(f"mlp_bwd_dh_{i}", da, prm["w1"])
    dx1, dy1, dg1, dsh2, dsc2, gr["norm2_g"] = modnorm_bwd(
        f"norm2_bwd_{i}", sv["x1"], dh2, dx, mods, prm["norm2_g"], 3, 4, gate=(sv["y1"], mods, 2),
        after=hook(f"{i}:mlp", gr, dh2), last=below is not None)
    dcat = mm_nt(f"proj_out_bwd_{i}", dy1, prm["w_out"], after=hook(f"{i}:mid", gr, dy1))
    go = mm_tn(f"grad_wout_a_{i}", sv["o"], dy1, [(0, 0)], 512, D).reshape(2, 2, 128, D)
    gm = mm_tn(f"grad_wout_b_{i}", sv["m2"], dy1, [(0, 0)], 512, D).reshape(2, 2, 128, D)
    gr["w_out"] = jnp.concatenate([go, gm], axis=0).transpose(1, 0, 2, 3)
    if i == 0:
        dq, dkv = attn_bwd("attn_bwd_0", sv["q"], sv["kv"], sv["o"], dcat, sv["lse"], GQA_HEADS)
        dp, gr["gq"], gr["gk"], gr["gs"], gr["sgu_w"], gr["sgu_b"] = even_tok_bwd(
            sv["p"], dq, dkv, dcat, consts["cos_e"], consts["sin_e"], prm["gq"], prm["gk"],
            prm["gs"], prm["sgu_w"], prm["sgu_b"], consts["avg"], consts["masks"])
    else:
        dq, dkv = attn_bwd("attn_bwd_1", sv["q"], sv["kv"], sv["o"], dcat, sv["lse"], MLA_HEADS)
        dz, gr["ln_g"], gr["ln_b"], gr["conv_b"] = conf_bwd_ln(sv["z"], dcat, prm["ln_g"], prm["ln_b"])
        dyc, gr["conv_w"] = conf_bwd_conv(sv["y"], dz, prm["conv_w"])
        dp, gr["gq"], gr["gkv"], gr["wq"], gr["wkk"], gr["wkv"] = odd_tok_bwd(
            sv["p"], dq, dkv, dyc, consts["cos_o"], consts["sin_o"], prm["gq"], prm["gkv"], prm["wq"], prm["wkk"],
            prm["wkv"], consts["spread"])
    n_in = prm["w_in"].shape[1]
    gr["w_in"] = mm_tn(f"grad_win_{i}", sv["h"], dp, [(0, 0), (1, 0)], 512, n_in)
    dh = mm_nt(f"proj_in_bwd_{i}", dp, prm["w_in"])
    if below:
        dx0, dy2b, dg2b, dsh1, dsc1, gr["norm1_g"] = modnorm_bwd(
            f"norm1_bwd_{i}", sv["x0"], dh, dx1, mods, prm["norm1_g"], 0, 1, gate=(below[0], below[1], 5))
        down = (dy2b, dg2b)
    else:
        dx0, dsh1, dsc1, gr["norm1_g"] = modnorm_bwd(f"norm1_bwd_{i}", sv["x0"], dh, dx1, mods, prm["norm1_g"], 0, 1,
                                                     lat_only=True)
        down = None
    dmods = jnp.concatenate([dsh1, dsc1, dg1, dsh2, dsc2, dg2], axis=1)
    return dx0, down, dmods, gr, hook(f"{i}:end", gr, dx0)


def local_step(xcat, target, mods, prms, final_g, hook=lambda point, grads, fresh: None):
    avg, masks = _group_consts()
    cos_e, sin_e = _rope_tables(64, 8)
    ck, sk = _rope_tables(32, 1)
    one64, zero64 = np.ones((SEQ, 64), np.float32), np.zeros((SEQ, 64), np.float32)
    one96, zero96 = np.ones((SEQ, 96), np.float32), np.zeros((SEQ, 96), np.float32)
    cos_o = np.concatenate([np.tile(np.concatenate([one64, ck], axis=1), (1, 8)), ck, one96], axis=1)
    sin_o = np.concatenate([np.tile(np.concatenate([zero64, sk], axis=1), (1, 8)), sk, zero96], axis=1)
    lane = np.arange(768)
    spread = np.zeros((128, 768), np.float32)
    spread[lane % 96 - 64, lane] = (lane % 96 >= 64)
    consts = dict(avg=avg, masks=masks, cos_e=jnp.asarray(cos_e), sin_e=jnp.asarray(sin_e),
                  cos_o=jnp.asarray(cos_o), sin_o=jnp.asarray(sin_o), spread=jnp.asarray(spread, BF))
    x = xcat
    h = modnorm_fwd("norm1_fwd_0", x, mods[0], prms[0]["norm1_g"], 0, 1)
    saved = []
    for i in range(2):
        x, h, sv = _layer_fwd(i, x, h, mods[i], prms[i], consts, (mods[1], prms[1]["norm1_g"]) if i == 0 else None)
        saved.append(sv)
    dx, dy2, dg2, loss, dfg = final_loss(x, target, final_g, saved[1]["y2"], mods[1], 5)
    dmods, grads = [None, None], [None, None]
    entry, down = None, (dy2, dg2)
    for i in (1, 0):
        below = (saved[0]["y2"], mods[0]) if i == 1 else None
        dx, down, dmods[i], grads[i], entry = _layer_bwd(i, dx, down[0], down[1], saved[i], mods[i], prms[i], consts,
                                                         hook, entry, below)
    return loss, dx, dmods, grads, dfg, entry


def _row(v):
    return v.reshape(1, -1).astype(F32)


def odd_in_params(od_w_in, w_uq, w_ukv):
    od = jnp.concatenate([od_w_in[:, 0:416], jnp.zeros((D, 96), od_w_in.dtype), od_w_in[:, 416:OD_IN]], axis=1)
    ukv = w_ukv.reshape(128, 8, 128)
    wkk = jnp.pad(ukv[:, :, :64], ((0, 0), (0, 0), (0, 32))).reshape(128, 768)
    return dict(w_in=od, wq=w_uq, wkk=wkk, wkv=ukv[:, :, 64:].reshape(128, 512))


def small_params(small):
    p0 = dict(norm1_g=_row(small["norm1_g"][0]), norm2_g=_row(small["norm2_g"][0]),
              gq=jnp.tile(_row(small["ev_q_norm_g"]), (1, 8)), gk=jnp.tile(_row(small["ev_k_norm_g"]), (1, 2)),
              gs=_row(small["ev_sgu_norm_g"]), sgu_w=small["ev_sgu_w"].reshape(8, 128, 128).astype(F32),
              sgu_b=small["ev_sgu_b"].reshape(8, 128, 1).astype(F32))
    p1 = dict(norm1_g=_row(small["norm1_g"][1]), norm2_g=_row(small["norm2_g"][1]),
              gq=_row(small["od_q_norm_g"]), gkv=_row(small["od_kv_norm_g"]),
              conv_w=jnp.pad(small["od_conv_w"].reshape(CONV_K, 512).astype(F32), ((0, 1), (0, 0))),
              conv_b=_row(small["od_conv_b"]), ln_g=_row(small["od_ln_g"]), ln_b=_row(small["od_ln_b"]))
    return [p0, p1]


def small_grads_natural(grads, dfg):
    g0, g1 = grads
    return dict(
        norm1_g=jnp.concatenate([g0["norm1_g"], g1["norm1_g"]], axis=0),
        norm2_g=jnp.concatenate([g0["norm2_g"], g1["norm2_g"]], axis=0),
        ev_q_norm_g=g0["gq"].reshape(8, 64).sum(0).reshape(1, 64),
        ev_k_norm_g=g0["gk"].reshape(2, 64).sum(0).reshape(1, 64),
        ev_sgu_norm_g=g0["gs"].reshape(1, 8, 64),
        ev_sgu_w=g0["sgu_w"].reshape(1, 8, 128, 128),
        ev_sgu_b=g0["sgu_b"].reshape(1, 8, 128),
        od_q_norm_g=g1["gq"].reshape(1, 256),
        od_kv_norm_g=g1["gkv"].reshape(1, 128),
        od_conv_w=g1["conv_w"][0:CONV_K].reshape(1, CONV_K, 512),
        od_conv_b=g1["conv_b"].reshape(1, 512),
        od_ln_g=g1["ln_g"].reshape(1, 512),
        od_ln_b=g1["ln_b"].reshape(1, 512),
        final_g=dfg.reshape(D))


def layer_grads_hs(i, g, part="all"):
    def cols(a):
        k, n = a.shape
        return a.reshape(2, k // 2, 4, n // 4).transpose(0, 2, 1, 3).astype(BF)

    mlp = [(("mlp_w1", i), g["w1"]), (("mlp_w2", i), g["w2"])]
    if part == "mlp":
        return mlp
    rest = [(("w_out", i), g["w_out"])]
    if i == 0:
        rest.append((("ev_w_in", 0), cols(g["w_in"].reshape(D, EV_IN))))
    else:
        od = g["w_in"].reshape(D, OD_PAD)
        od = jnp.concatenate([od[:, 0:416], od[:, 512:OD_PAD]], axis=1)
        ukv = jnp.concatenate([g["wkk"].reshape(128, 8, 96)[:, :, :64], g["wkv"].reshape(128, 8, 64)], axis=2)
        rest += [(("od_w_in", 0), cols(od)), (("od_w_uq", 0), cols(g["wq"])),
                 (("od_w_ukv", 0), cols(ukv.reshape(128, 1024)))]
    return rest if part == "rest" else mlp + rest


WEIGHT_NAMES = ['c_ctx', 'ada_w', 'ada_b', 'norm1_g', 'norm2_g', 'w_out', 'mlp_w1', 'mlp_w2', 'ev_w_in',
                'ev_q_norm_g', 'ev_k_norm_g', 'ev_sgu_norm_g', 'ev_sgu_w', 'ev_sgu_b', 'od_w_in', 'od_q_norm_g',
                'od_kv_norm_g', 'od_w_uq', 'od_w_ukv', 'od_conv_w', 'od_conv_b', 'od_ln_g', 'od_ln_b', 'final_g']
REPL_SMALL = ['norm1_g', 'norm2_g', 'ev_q_norm_g', 'ev_k_norm_g', 'ev_sgu_norm_g', 'ev_sgu_w', 'ev_sgu_b',
              'od_kv_norm_g', 'final_g']
SHARD_SMALL = ['od_q_norm_g', 'od_conv_w', 'od_conv_b', 'od_ln_g', 'od_ln_b']
BIG = ['w_out', 'mlp_w1', 'mlp_w2', 'ev_w_in', 'od_w_in', 'od_w_uq', 'od_w_ukv']


def _gather_last(parts):
    return jnp.concatenate([parts[k] for k in range(4)], axis=-1)


class _Reduce:
    def __init__(self, tag, named, half, where):
        self.tag, self.half, self.where = tag, half, where
        self.names, self.hs = zip(*named)
        self.hs = list(self.hs)

    def to_sibling(self):
        lands = [lax.empty(a.shape[1:], BF) for a in self.hs]
        (self.h1,), token = exchange_start(f"rs_sibling_start_{self.tag}", [(self.hs, lands)], TO_SIBLING)
        return token

    def to_chips(self, after):
        hs, got = exchange_wait(f"rs_sibling_wait_{self.tag}", self.h1, after, TO_SIBLING)
        pair = [add_pairs(f"rs_add_{self.tag}_{k}", a, b, self.half) for k, (a, b) in enumerate(zip(hs, got))]
        lands = [lax.empty(p.shape, BF) for p in pair]
        (self.h2,), token = exchange_start(f"rs_chips_start_{self.tag}", [(pair, lands)], SCATTER)
        return token

    def finish(self, after, bufs):
        pair, land = exchange_wait(f"rs_chips_wait_{self.tag}", self.h2, after, SCATTER)
        for k, ((n, idx), l, p) in enumerate(zip(self.names, land, pair)):
            bufs[n] = sum_slabs(f"rs_sum_{self.tag}_{k}", l, p, self.where, bufs[n], idx)


def kernel(x, c, ctx, c_ctx, ada_w, ada_b, norm1_g, norm2_g, w_out, mlp_w1, mlp_w2, ev_w_in, ev_q_norm_g, ev_k_norm_g, ev_sgu_norm_g, ev_sgu_w, ev_sgu_b, od_w_in, od_q_norm_g, od_kv_norm_g, od_w_uq, od_w_ukv, od_conv_w, od_conv_b, od_ln_g, od_ln_b, final_g, loss_target, m_c_ctx, m_ada_w, m_ada_b, m_norm1_g, m_norm2_g, m_w_out, m_mlp_w1, m_mlp_w2, m_ev_w_in, m_ev_q_norm_g, m_ev_k_norm_g, m_ev_sgu_norm_g, m_ev_sgu_w, m_ev_sgu_b, m_od_w_in, m_od_q_norm_g, m_od_kv_norm_g, m_od_w_uq, m_od_w_ukv, m_od_conv_w, m_od_conv_b, m_od_ln_g, m_od_ln_b, m_final_g, v_c_ctx, v_ada_w, v_ada_b, v_norm1_g, v_norm2_g, v_w_out, v_mlp_w1, v_mlp_w2, v_ev_w_in, v_ev_q_norm_g, v_ev_k_norm_g, v_ev_sgu_norm_g, v_ev_sgu_w, v_ev_sgu_b, v_od_w_in, v_od_q_norm_g, v_od_kv_norm_g, v_od_w_uq, v_od_w_ukv, v_od_conv_w, v_od_conv_b, v_od_ln_g, v_od_ln_b, v_final_g):
    w = dict(c_ctx=c_ctx, ada_w=ada_w, ada_b=ada_b, norm1_g=norm1_g, norm2_g=norm2_g, w_out=w_out, mlp_w1=mlp_w1,
             mlp_w2=mlp_w2, ev_w_in=ev_w_in, ev_q_norm_g=ev_q_norm_g, ev_k_norm_g=ev_k_norm_g,
             ev_sgu_norm_g=ev_sgu_norm_g, ev_sgu_w=ev_sgu_w, ev_sgu_b=ev_sgu_b, od_w_in=od_w_in,
             od_q_norm_g=od_q_norm_g, od_kv_norm_g=od_kv_norm_g, od_w_uq=od_w_uq, od_w_ukv=od_w_ukv,
             od_conv_w=od_conv_w, od_conv_b=od_conv_b, od_ln_g=od_ln_g, od_ln_b=od_ln_b, final_g=final_g)
    mom = dict(c_ctx=m_c_ctx, ada_w=m_ada_w, ada_b=m_ada_b, norm1_g=m_norm1_g, norm2_g=m_norm2_g, w_out=m_w_out,
               mlp_w1=m_mlp_w1, mlp_w2=m_mlp_w2, ev_w_in=m_ev_w_in, ev_q_norm_g=m_ev_q_norm_g,
               ev_k_norm_g=m_ev_k_norm_g, ev_sgu_norm_g=m_ev_sgu_norm_g, ev_sgu_w=m_ev_sgu_w, ev_sgu_b=m_ev_sgu_b,
               od_w_in=m_od_w_in, od_q_norm_g=m_od_q_norm_g, od_kv_norm_g=m_od_kv_norm_g, od_w_uq=m_od_w_uq,
               od_w_ukv=m_od_w_ukv, od_conv_w=m_od_conv_w, od_conv_b=m_od_conv_b, od_ln_g=m_od_ln_g,
               od_ln_b=m_od_ln_b, final_g=m_final_g)
    var = dict(c_ctx=v_c_ctx, ada_w=v_ada_w, ada_b=v_ada_b, norm1_g=v_norm1_g, norm2_g=v_norm2_g, w_out=v_w_out,
               mlp_w1=v_mlp_w1, mlp_w2=v_mlp_w2, ev_w_in=v_ev_w_in, ev_q_norm_g=v_ev_q_norm_g,
               ev_k_norm_g=v_ev_k_norm_g, ev_sgu_norm_g=v_ev_sgu_norm_g, ev_sgu_w=v_ev_sgu_w, ev_sgu_b=v_ev_sgu_b,
               od_w_in=v_od_w_in, od_q_norm_g=v_od_q_norm_g, od_kv_norm_g=v_od_kv_norm_g, od_w_uq=v_od_w_uq,
               od_w_ukv=v_od_w_ukv, od_conv_w=v_od_conv_w, od_conv_b=v_od_conv_b, od_ln_g=v_od_ln_g,
               od_ln_b=v_od_ln_b, final_g=v_final_g)
    xi, yi, ci = lax.axis_index("x"), lax.axis_index("y"), lax.axis_index("c")
    chip = 2 * xi + yi
    dev = 2 * chip + ci

    shard_shapes = [w[n].shape for n in SHARD_SMALL]
    g0 = all_gather8("ag_small", _pack([c] + [w[n] for n in SHARD_SMALL]))
    g0 = g0.reshape(8, -1, D)
    parts = _unpack(g0, [c.shape] + shard_shapes)
    c_all = parts[0].reshape(16, D)
    small_full = {n: _gather_last(p[0::2]) for n, p in zip(SHARD_SMALL, parts[1:])}
    call = jnp.concatenate([c_all, c_ctx.reshape(1, D), jnp.zeros((NC - 17, D), F32)], axis=0)

    cols = ada_w.shape[2]
    ada_b_sh = lax.dynamic_slice(ada_b, (0, chip * cols), (2, cols)).reshape(2, 1, cols)
    mt = mods_fwd(call, ada_w, ada_b_sh)
    mt = all_gather8("ag_mods", mt.reshape(2 * NC, cols)).reshape(8, 2, NC, cols)
    table = mt[0::2].transpose(1, 2, 0, 3).reshape(2, NC, 4 * cols)
    mods = []
    for i in range(2):
        lat = lax.dynamic_slice(table[i], (2 * dev, 0), (2, 4 * cols))
        mc = table[i, 16]
        mods.append(jnp.stack([mc, lat[0], mc, lat[1]]).reshape(4 * N_MOD, 1, D))

    order = [[("ev_w_in", 0)], [("w_out", 0), ("mlp_w1", 0), ("mlp_w2", 0)],
             [("od_w_in", 0), ("od_w_uq", 0), ("od_w_ukv", 0), ("w_out", 1)], [("mlp_w1", 1), ("mlp_w2", 1)]]
    groups = []
    for names in order:
        srcs = [w[n][i].astype(BF) for n, i in names]
        groups.append((srcs, [lax.empty((4,) + s.shape, BF) for s in srcs]))
    groups[0][0][0], table = lax.optimization_barrier((groups[0][0][0], table))
    handles, token = exchange_start("gather_start", groups, GATHER)
    mods[0] = mods[0] + token[0, 0]
    small = {n: w[n] for n in REPL_SMALL}
    small.update(small_full)
    prms = small_params(small)

    chip1 = chip.reshape(1).astype(jnp.int32)

    def arrived(k, after):
        srcs, lands = exchange_wait(f"gather_wait_{k}", handles[k], after, GATHER)
        return [place_own(f"gather_own_{k}_{a}", l, s, chip1) for a, (l, s) in enumerate(zip(lands, srcs))]

    def arrive_ev_in(after):
        (ev,) = arrived(0, after)
        prms[0]["w_in"] = _gather_last(ev)

    def arrive_ev_rest(after):
        wo, w1, w2 = arrived(1, after)
        prms[0].update(w_out=wo.reshape(D, D), w1=w1, w2=w2)

    def arrive_od(after):
        od, uq, ukv, wo = arrived(2, after)
        prms[1].update(odd_in_params(_gather_last(od), _gather_last(uq), _gather_last(ukv)), w_out=wo.reshape(D, D))

    def arrive_od_mlp(after):
        w1, w2 = arrived(3, after)
        prms[1].update(w1=w1, w2=w2)

    prms[0]["w_in"] = arrive_ev_in
    prms[0]["w_out"] = arrive_ev_rest
    prms[1]["w_in"] = arrive_od
    prms[1]["w1"] = arrive_od_mlp

    half = ci.reshape(1).astype(jnp.int32)
    where = jnp.stack([chip, ci]).astype(jnp.int32)
    red = {}

    def hook(point, g, fresh):
        if point == "1:end":
            red["l1"] = _Reduce("l1", layer_grads_hs(1, g, "all"), half, where)
            return red["l1"].to_sibling()
        if point == "0:mlp":
            red["l0_mlp"] = _Reduce("l0_mlp", layer_grads_hs(0, g, "mlp"), half, where)
            return red["l1"].to_chips(fresh) + red["l0_mlp"].to_sibling()
        if point == "0:mid":
            return red["l0_mlp"].to_chips(fresh)
        if point == "0:end":
            red["l0_rest"] = _Reduce("l0_rest", layer_grads_hs(0, g, "rest"), half, where)
            return red["l0_rest"].to_sibling()
        return None

    xin = (ctx.reshape(NEX * LC, D), x.reshape(NEX * L, D))
    loss_p, dx, dmods, grads, dfg, last = local_step(xin, loss_target.reshape(NEX * L, D), mods, prms,
                                                     final_g.reshape(1, D), hook)
    grad_x = dx.reshape(NEX, L, D)

    sg = small_grads_natural(grads, dfg)
    dm = jnp.stack([d.reshape(4, N_MOD * D) for d in dmods])
    small_names = REPL_SMALL + SHARD_SMALL
    items = [dm[:, 1::2], dm[:, 0] + dm[:, 2]] + [sg[n] for n in small_names] + [loss_p[0:1, 0:1]]
    shapes = [a.shape for a in items]
    g1 = all_gather8("ag_grads", _pack(items), after=last)
    started = red["l0_rest"].to_chips(g1)
    rows1 = g1.shape[0] // 8
    g1 = g1.reshape(8, rows1, D)
    tot = _unpack(sum_lead("sum_small", g1, after=started), shapes)
    dm_lat = _unpack(g1, shapes[:1])[0]
    dm_lat = dm_lat.transpose(1, 0, 2, 3).reshape(2, 16, N_MOD * D)
    dm_all = jnp.concatenate([dm_lat, tot[1][:, None], jnp.zeros((2, NC - 17, N_MOD * D), F32)], axis=1)
    gsum = dict(zip(small_names, tot[2:2 + len(small_names)]))
    loss = tot[-1].reshape(())
    grad = {n: gsum[n].reshape(w[n].shape) for n in REPL_SMALL}
    for n in SHARD_SMALL:
        k = w[n].shape[-1]
        grad[n] = lax.dynamic_slice_in_dim(gsum[n], chip * k, k, axis=gsum[n].ndim - 1)
    grad["ada_b"] = sum_lead("sum_ada_b", dm_all.transpose(1, 0, 2).reshape(NC, 2 * N_MOD, D)).reshape(2, N_MOD * D)

    dm_sh = lax.dynamic_slice(dm_all, (0, 0, chip * cols), (2, NC, cols))
    grad["ada_w"], dcc = ada_bwd(call, ada_w, dm_sh)
    dcc = all_gather8("ag_cctx", dcc).reshape(8, 8, D)
    grad["c_ctx"] = sum_lead("sum_cctx", dcc[0::2])[0]

    delta, new_m, new_v = {}, {}, {}

    def adam_big(n, again):
        shp = w[n].shape
        two_d = (shp[0] * shp[1], shp[2])
        res = adamw(f"adamw_{n}", w[n].reshape(two_d), grad[n].reshape(two_d), mom[n].reshape(two_d),
                    var[n].reshape(two_d), again)
        delta[n], new_m[n], new_v[n] = [a.reshape(shp) for a in res[:3]]
        if again:
            grad[n] = res[3].reshape(shp)

    adam_big('ada_w', False)
    rest = [n for n in WEIGHT_NAMES if n not in ['ada_w'] + BIG]
    flat2 = lambda a: a.reshape(-1, a.shape[-1])
    outs = adamw_many("adamw_small", [flat2(w[n]) for n in rest], [flat2(grad[n]) for n in rest],
                      [flat2(mom[n]) for n in rest], [flat2(var[n]) for n in rest])
    for dst, arrs in zip((delta, new_m, new_v), outs):
        dst.update({n: a.reshape(w[n].shape) for n, a in zip(rest, arrs)})
    d_ = outs[0][1]

    bufs = {n: lax.empty((w[n].shape[0], 2, w[n].shape[1] // 2, w[n].shape[2]), F32) for n in BIG}
    for tag, behind in (("l1", delta["ada_w"]), ("l0_mlp", d_), ("l0_rest", d_)):
        red[tag].finish(behind, bufs)
    for n, full in zip(BIG, sibling_merge("rs_sibling_merge", [bufs[n] for n in BIG])):
        grad[n] = full.reshape(w[n].shape)
    for n in BIG:
        adam_big(n, True)

    return (loss, grad_x, *[grad[n] for n in WEIGHT_NAMES], *[delta[n] for n in WEIGHT_NAMES],
            *[new_m[n] for n in WEIGHT_NAMES], *[new_v[n] for n in WEIGHT_NAMES])
```
